```python
import math
import jax, jax.numpy as jnp
from jax import lax
import numpy as np

D_MODEL = 1024
BATCH = 8
SEQ = 4096
DEPTH = 2

D_G = D_MODEL // 4
N_GROUPS = 5
D_MIX = N_GROUPS * D_G
N_SUB = 4
HEAD_DIM = D_G // N_SUB
CONV_A = 3
CONV_D = 31
CHUNK = 128
POOL_WINDOWS = (2, 4, 8, 16)
MEM_LEN = 256
LN_EPS = 1e-5
DEEPNORM_ALPHA = (2.0 * DEPTH) ** 0.25
DEEPNORM_BETA = (8.0 * DEPTH) ** -0.25

SPLIT_SIZES = (D_G, D_G, D_G, D_G, D_G, D_G, D_G, D_G, D_G, D_MIX)
D_IN = sum(SPLIT_SIZES)
SPLIT_OFFSETS = tuple(int(o) for o in np.cumsum(SPLIT_SIZES)[:-1])

kernel_name = "hybrid_parallel_conv_sgu_pool_conformer_memxattn"


def layer_norm(x, g, b):
    xf = x.astype(jnp.float32)
    mu = jnp.mean(xf, axis=-1, keepdims=True)
    var = jnp.mean(jnp.square(xf - mu), axis=-1, keepdims=True)
    y = (xf - mu) * lax.rsqrt(var + LN_EPS) * g.astype(jnp.float32) + b.astype(jnp.float32)
    return y.astype(x.dtype)


def causal_depthwise_conv(x, w):
    k, c = w.shape
    return lax.conv_general_dilated(
        x, w[:, None, :].astype(x.dtype), window_strides=(1,), padding=[(k - 1, 0)],
        dimension_numbers=("NWC", "WIO", "NWC"), feature_group_count=c)


def short_gated_conv(xa, ba, ca, w_conv):
    return ba * causal_depthwise_conv(ca * xa, w_conv)


def spatial_gating(u, v, ln_g, ln_b, w_s, b_s):
    bn, s, _ = v.shape
    u = jax.nn.gelu(u)
    v = layer_norm(jax.nn.gelu(v), ln_g, ln_b)
    v = v.reshape(bn, s // CHUNK, CHUNK, N_SUB, HEAD_DIM)
    mask = jnp.tril(jnp.ones((CHUNK, CHUNK), dtype=bool))
    w = jnp.where(mask[None], w_s, jnp.zeros_like(w_s))
    mixed = jnp.einsum("hts,bcshd->bcthd", w, v) + b_s.T[:, :, None]
    return u * mixed.reshape(bn, s, D_G)


def multiscale_pool(xc, w_grp, scale):
    bn, s, _ = xc.shape
    xf = xc.astype(jnp.float32).reshape(bn, s, N_SUB, HEAD_DIM)
    cs = jnp.pad(jnp.cumsum(xf, axis=1), ((0, 0), (1, 0), (0, 0), (0, 0)))
    t = jnp.arange(s)
    win = jnp.array(POOL_WINDOWS, dtype=jnp.int32)
    lo = jnp.maximum(t[:, None] + 1 - win[None, :], 0)
    window_sum = cs[:, 1:] - cs[:, lo, jnp.arange(N_SUB)[None, :]]
    count = jnp.minimum(t[:, None] + 1, win[None, :]).astype(jnp.float32)
    y = (window_sum / count[None, :, :, None] - xf).astype(xc.dtype)
    y = jnp.einsum("bsgc,gcd->bsgd", y, w_grp)
    return y.reshape(bn, s, D_G) * scale


def conformer_conv(a, g, w_dw, b_dw, ln_g, ln_b, w_pw):
    h = a * jax.nn.sigmoid(g)
    h = causal_depthwise_conv(h, w_dw) + b_dw
    h = jax.nn.silu(layer_norm(h, ln_g, ln_b))
    return h @ w_pw


def memory_cross_attention(q, mem, w_kv):
    bn, s, _ = q.shape
    k, v = jnp.split(mem @ w_kv, 2, axis=-1)
    q = q.reshape(bn, s, N_SUB, HEAD_DIM)
    k = k.reshape(bn, -1, N_SUB, HEAD_DIM)
    v = v.reshape(bn, -1, N_SUB, HEAD_DIM)
    scores = jnp.einsum("bshd,bmhd->bhsm", q, k).astype(jnp.float32) * (1.0 / math.sqrt(HEAD_DIM))
    p = jax.nn.softmax(scores, axis=-1).astype(v.dtype)
    return jnp.einsum("bhsm,bmhd->bshd", p, v).reshape(bn, s, D_G)


def hybrid_mixer(x, mem, w_in, conv_a_w, sg_ln_g, sg_ln_b, sg_w, sg_b, pool_w, pool_scale,
                 cc_dw_w, cc_dw_b, cc_ln_g, cc_ln_b, cc_pw_w, w_kv, w_out):
    proj = x @ w_in
    xa, ba, ca, u, v, xc, da, dg, q, gate = jnp.split(proj, SPLIT_OFFSETS, axis=-1)
    y_a = short_gated_conv(xa, ba, ca, conv_a_w)
    y_b = spatial_gating(u, v, sg_ln_g, sg_ln_b, sg_w, sg_b)
    y_c = multiscale_pool(xc, pool_w, pool_scale)
    y_d = conformer_conv(da, dg, cc_dw_w, cc_dw_b, cc_ln_g, cc_ln_b, cc_pw_w)
    y_e = memory_cross_attention(q, mem, w_kv)
    h = jnp.concatenate([y_a, y_b, y_c, y_d, y_e], axis=-1) * jax.nn.silu(gate)
    return h @ w_out


def _fwd_setup_inputs(seed: int = 0) -> dict:
    key = jax.random.key(seed)
    ks = jax.random.split(key, 20)
    L = DEPTH

    def nrm(k, shape, scale):
        return jax.random.normal(k, shape, jnp.float32) * scale

    return {
        "x": nrm(ks[0], (BATCH, SEQ, D_MODEL), 1.0),
        "mem": nrm(ks[1], (BATCH, MEM_LEN, D_MODEL), 1.0),
        "w_in": nrm(ks[2], (L, D_MODEL, D_IN), D_MODEL ** -0.5),
        "conv_a_w": nrm(ks[3], (L, CONV_A, D_G), CONV_A ** -0.5),
        "sg_ln_g": 1.0 + nrm(ks[4], (L, D_G), 0.05),
        "sg_ln_b": nrm(ks[5], (L, D_G), 0.05),
        "sg_w": nrm(ks[6], (L, N_SUB, CHUNK, CHUNK), CHUNK ** -0.5),
        "sg_b": 1.0 + nrm(ks[7], (L, N_SUB, CHUNK), 0.1),
        "pool_w": nrm(ks[8], (L, N_SUB, HEAD_DIM, HEAD_DIM), HEAD_DIM ** -0.5),
        "pool_scale": 1.0 + nrm(ks[9], (L, D_G), 0.1),
        "cc_dw_w": nrm(ks[10], (L, CONV_D, D_G), CONV_D ** -0.5),
        "cc_dw_b": nrm(ks[11], (L, D_G), 0.02),
        "cc_ln_g": 1.0 + nrm(ks[12], (L, D_G), 0.05),
        "cc_ln_b": nrm(ks[13], (L, D_G), 0.05),
        "cc_pw_w": nrm(ks[14], (L, D_G, D_G), D_G ** -0.5),
        "w_kv": nrm(ks[15], (L, D_MODEL, 2 * D_G), D_MODEL ** -0.5),
        "w_out": nrm(ks[16], (L, D_MIX, D_MODEL), D_MIX ** -0.5 * DEEPNORM_BETA),
        "ln_g": 1.0 + nrm(ks[17], (L, D_MODEL), 0.05),
        "ln_b": nrm(ks[18], (L, D_MODEL), 0.05),
    }


def _fwd_reference(x, mem, w_in, conv_a_w, sg_ln_g, sg_ln_b, sg_w, sg_b, pool_w, pool_scale,
              cc_dw_w, cc_dw_b, cc_ln_g, cc_ln_b, cc_pw_w, w_kv, w_out, ln_g, ln_b):
    for l in range(DEPTH):
        y = hybrid_mixer(x, mem, w_in[l], conv_a_w[l], sg_ln_g[l], sg_ln_b[l], sg_w[l], sg_b[l],
                         pool_w[l], pool_scale[l], cc_dw_w[l], cc_dw_b[l], cc_ln_g[l], cc_ln_b[l],
                         cc_pw_w[l], w_kv[l], w_out[l])
        x = layer_norm(DEEPNORM_ALPHA * x + y, ln_g[l], ln_b[l])
    return x


import jax as _jax
import jax.numpy as _jnp

TWIN_FORMAT = 'train_step'
FWD_PARAMS = ['x', 'mem', 'w_in', 'conv_a_w', 'sg_ln_g', 'sg_ln_b', 'sg_w', 'sg_b', 'pool_w', 'pool_scale', 'cc_dw_w', 'cc_dw_b', 'cc_ln_g', 'cc_ln_b', 'cc_pw_w', 'w_kv', 'w_out', 'ln_g', 'ln_b']
TWIN_WEIGHTS = ['w_in', 'conv_a_w', 'sg_ln_g', 'sg_ln_b', 'sg_w', 'sg_b', 'pool_w', 'pool_scale', 'cc_dw_w', 'cc_dw_b', 'cc_ln_g', 'cc_ln_b', 'cc_pw_w', 'w_kv', 'w_out', 'ln_g', 'ln_b']
TWIN_DIFF_INPUT = 'x'
TWIN_INPUTS = ['x', 'mem', 'w_in', 'conv_a_w', 'sg_ln_g', 'sg_ln_b', 'sg_w', 'sg_b', 'pool_w', 'pool_scale', 'cc_dw_w', 'cc_dw_b', 'cc_ln_g', 'cc_ln_b', 'cc_pw_w', 'w_kv', 'w_out', 'ln_g', 'ln_b', 'loss_target', 'm_w_in', 'm_conv_a_w', 'm_sg_ln_g', 'm_sg_ln_b', 'm_sg_w', 'm_sg_b', 'm_pool_w', 'm_pool_scale', 'm_cc_dw_w', 'm_cc_dw_b', 'm_cc_ln_g', 'm_cc_ln_b', 'm_cc_pw_w', 'm_w_kv', 'm_w_out', 'm_ln_g', 'm_ln_b', 'v_w_in', 'v_conv_a_w', 'v_sg_ln_g', 'v_sg_ln_b', 'v_sg_w', 'v_sg_b', 'v_pool_w', 'v_pool_scale', 'v_cc_dw_w', 'v_cc_dw_b', 'v_cc_ln_g', 'v_cc_ln_b', 'v_cc_pw_w', 'v_w_kv', 'v_w_out', 'v_ln_g', 'v_ln_b']
TWIN_OUTPUTS = ['loss', 'grad_x', 'grad_w_in', 'grad_conv_a_w', 'grad_sg_ln_g', 'grad_sg_ln_b', 'grad_sg_w', 'grad_sg_b', 'grad_pool_w', 'grad_pool_scale', 'grad_cc_dw_w', 'grad_cc_dw_b', 'grad_cc_ln_g', 'grad_cc_ln_b', 'grad_cc_pw_w', 'grad_w_kv', 'grad_w_out', 'grad_ln_g', 'grad_ln_b', 'delta_w_in', 'delta_conv_a_w', 'delta_sg_ln_g', 'delta_sg_ln_b', 'delta_sg_w', 'delta_sg_b', 'delta_pool_w', 'delta_pool_scale', 'delta_cc_dw_w', 'delta_cc_dw_b', 'delta_cc_ln_g', 'delta_cc_ln_b', 'delta_cc_pw_w', 'delta_w_kv', 'delta_w_out', 'delta_ln_g', 'delta_ln_b', 'new_m_w_in', 'new_m_conv_a_w', 'new_m_sg_ln_g', 'new_m_sg_ln_b', 'new_m_sg_w', 'new_m_sg_b', 'new_m_pool_w', 'new_m_pool_scale', 'new_m_cc_dw_w', 'new_m_cc_dw_b', 'new_m_cc_ln_g', 'new_m_cc_ln_b', 'new_m_cc_pw_w', 'new_m_w_kv', 'new_m_w_out', 'new_m_ln_g', 'new_m_ln_b', 'new_v_w_in', 'new_v_conv_a_w', 'new_v_sg_ln_g', 'new_v_sg_ln_b', 'new_v_sg_w', 'new_v_sg_b', 'new_v_pool_w', 'new_v_pool_scale', 'new_v_cc_dw_w', 'new_v_cc_dw_b', 'new_v_cc_ln_g', 'new_v_cc_ln_b', 'new_v_cc_pw_w', 'new_v_w_kv', 'new_v_w_out', 'new_v_ln_g', 'new_v_ln_b']
TWIN_LEAF_KINDS = {'loss': 'loss', 'grad_x': 'grad_x', 'grad_w_in': 'grad_w', 'grad_conv_a_w': 'grad_w', 'grad_sg_ln_g': 'grad_w', 'grad_sg_ln_b': 'grad_w', 'grad_sg_w': 'grad_w', 'grad_sg_b': 'grad_w', 'grad_pool_w': 'grad_w', 'grad_pool_scale': 'grad_w', 'grad_cc_dw_w': 'grad_w', 'grad_cc_dw_b': 'grad_w', 'grad_cc_ln_g': 'grad_w', 'grad_cc_ln_b': 'grad_w', 'grad_cc_pw_w': 'grad_w', 'grad_w_kv': 'grad_w', 'grad_w_out': 'grad_w', 'grad_ln_g': 'grad_w', 'grad_ln_b': 'grad_w', 'delta_w_in': 'delta_w', 'delta_conv_a_w': 'delta_w', 'delta_sg_ln_g': 'delta_w', 'delta_sg_ln_b': 'delta_w', 'delta_sg_w': 'delta_w', 'delta_sg_b': 'delta_w', 'delta_pool_w': 'delta_w', 'delta_pool_scale': 'delta_w', 'delta_cc_dw_w': 'delta_w', 'delta_cc_dw_b': 'delta_w', 'delta_cc_ln_g': 'delta_w', 'delta_cc_ln_b': 'delta_w', 'delta_cc_pw_w': 'delta_w', 'delta_w_kv': 'delta_w', 'delta_w_out': 'delta_w', 'delta_ln_g': 'delta_w', 'delta_ln_b': 'delta_w', 'new_m_w_in': 'new_m', 'new_m_conv_a_w': 'new_m', 'new_m_sg_ln_g': 'new_m', 'new_m_sg_ln_b': 'new_m', 'new_m_sg_w': 'new_m', 'new_m_sg_b': 'new_m', 'new_m_pool_w': 'new_m', 'new_m_pool_scale': 'new_m', 'new_m_cc_dw_w': 'new_m', 'new_m_cc_dw_b': 'new_m', 'new_m_cc_ln_g': 'new_m', 'new_m_cc_ln_b': 'new_m', 'new_m_cc_pw_w': 'new_m', 'new_m_w_kv': 'new_m', 'new_m_w_out': 'new_m', 'new_m_ln_g': 'new_m', 'new_m_ln_b': 'new_m', 'new_v_w_in': 'new_v', 'new_v_conv_a_w': 'new_v', 'new_v_sg_ln_g': 'new_v', 'new_v_sg_ln_b': 'new_v', 'new_v_sg_w': 'new_v', 'new_v_sg_b': 'new_v', 'new_v_pool_w': 'new_v', 'new_v_pool_scale': 'new_v', 'new_v_cc_dw_w': 'new_v', 'new_v_cc_dw_b': 'new_v', 'new_v_cc_ln_g': 'new_v', 'new_v_cc_ln_b': 'new_v', 'new_v_cc_pw_w': 'new_v', 'new_v_w_kv': 'new_v', 'new_v_w_out': 'new_v', 'new_v_ln_g': 'new_v', 'new_v_ln_b': 'new_v'}


def _forward(args):
    return _fwd_reference(*[args[k] for k in FWD_PARAMS])


def _output_shape():
    out = _jax.eval_shape(lambda: _forward(_fwd_setup_inputs(0)))
    return out.shape, out.dtype

N_MICROBATCH = 1
ADAM_LR = 0.001
ADAM_B1 = 0.9
ADAM_B2 = 0.999
ADAM_EPS = 1e-08
ADAM_WD = 0.01
ADAM_STEP = 10
PER_EXAMPLE_BATCH_AXIS = {'x': 0, 'mem': 0, 'loss_target': 0}
SHARED_INPUTS = []
_WEIGHT_DTYPES = {'w_in': _jnp.float32, 'conv_a_w': _jnp.float32, 'sg_ln_g': _jnp.float32, 'sg_ln_b': _jnp.float32, 'sg_w': _jnp.float32, 'sg_b': _jnp.float32, 'pool_w': _jnp.float32, 'pool_scale': _jnp.float32, 'cc_dw_w': _jnp.float32, 'cc_dw_b': _jnp.float32, 'cc_ln_g': _jnp.float32, 'cc_ln_b': _jnp.float32, 'cc_pw_w': _jnp.float32, 'w_kv': _jnp.float32, 'w_out': _jnp.float32, 'ln_g': _jnp.float32, 'ln_b': _jnp.float32}
MOMENT_SCALE = {'w_in': 2.540984e-02, 'conv_a_w': 3.532089e-02, 'sg_ln_g': 1.672458e-02, 'sg_ln_b': 1.702632e-02, 'sg_w': 1.125771e-02, 'sg_b': 1.552225e-02, 'pool_w': 2.982623e-02, 'pool_scale': 3.093264e-02, 'cc_dw_w': 2.148073e-02, 'cc_dw_b': 7.295029e-02, 'cc_ln_g': 3.379076e-02, 'cc_ln_b': 4.427576e-02, 'cc_pw_w': 2.417726e-02, 'w_kv': 3.799686e-03, 'w_out': 5.862488e-02, 'ln_g': 2.283902e+01, 'ln_b': 1.760292e+00}


def _to_microbatches(a, axis):
    t = _jnp.moveaxis(a, axis, 0)
    t = t.reshape((N_MICROBATCH, t.shape[0] // N_MICROBATCH) + t.shape[1:])
    return _jnp.moveaxis(t, 1, axis + 1)


def setup_inputs(seed: int = 0) -> dict:
    inp = _fwd_setup_inputs(seed)
    key = _jax.random.fold_in(_jax.random.key(seed), 7919)
    shape, _ = _output_shape()
    out = dict(inp)
    out["loss_target"] = _jax.random.normal(_jax.random.fold_in(key, 0), shape, _jnp.float32)
    for i, name in enumerate(TWIN_WEIGHTS):
        w = inp[name].astype(_jnp.float32)
        if MOMENT_SCALE is None:
            s = _jnp.sqrt(_jnp.mean(_jnp.square(w)) + 1e-30)
        else:
            s = MOMENT_SCALE[name]
        km, kv = _jax.random.split(_jax.random.fold_in(key, i + 1))
        out[name] = w
        out["m_" + name] = s * _jax.random.normal(km, w.shape, _jnp.float32)
        out["v_" + name] = (s * s) * _jax.random.uniform(kv, w.shape, _jnp.float32, 0.5, 1.5)
    if N_MICROBATCH > 1:
        for name, axis in PER_EXAMPLE_BATCH_AXIS.items():
            out[name] = _to_microbatches(out[name], axis)
    return {'x': out['x'], 'mem': out['mem'], 'w_in': out['w_in'], 'conv_a_w': out['conv_a_w'], 'sg_ln_g': out['sg_ln_g'], 'sg_ln_b': out['sg_ln_b'], 'sg_w': out['sg_w'], 'sg_b': out['sg_b'], 'pool_w': out['pool_w'], 'pool_scale': out['pool_scale'], 'cc_dw_w': out['cc_dw_w'], 'cc_dw_b': out['cc_dw_b'], 'cc_ln_g': out['cc_ln_g'], 'cc_ln_b': out['cc_ln_b'], 'cc_pw_w': out['cc_pw_w'], 'w_kv': out['w_kv'], 'w_out': out['w_out'], 'ln_g': out['ln_g'], 'ln_b': out['ln_b'], 'loss_target': out['loss_target'], 'm_w_in': out['m_w_in'], 'm_conv_a_w': out['m_conv_a_w'], 'm_sg_ln_g': out['m_sg_ln_g'], 'm_sg_ln_b': out['m_sg_ln_b'], 'm_sg_w': out['m_sg_w'], 'm_sg_b': out['m_sg_b'], 'm_pool_w': out['m_pool_w'], 'm_pool_scale': out['m_pool_scale'], 'm_cc_dw_w': out['m_cc_dw_w'], 'm_cc_dw_b': out['m_cc_dw_b'], 'm_cc_ln_g': out['m_cc_ln_g'], 'm_cc_ln_b': out['m_cc_ln_b'], 'm_cc_pw_w': out['m_cc_pw_w'], 'm_w_kv': out['m_w_kv'], 'm_w_out': out['m_w_out'], 'm_ln_g': out['m_ln_g'], 'm_ln_b': out['m_ln_b'], 'v_w_in': out['v_w_in'], 'v_conv_a_w': out['v_conv_a_w'], 'v_sg_ln_g': out['v_sg_ln_g'], 'v_sg_ln_b': out['v_sg_ln_b'], 'v_sg_w': out['v_sg_w'], 'v_sg_b': out['v_sg_b'], 'v_pool_w': out['v_pool_w'], 'v_pool_scale': out['v_pool_scale'], 'v_cc_dw_w': out['v_cc_dw_w'], 'v_cc_dw_b': out['v_cc_dw_b'], 'v_cc_ln_g': out['v_cc_ln_g'], 'v_cc_ln_b': out['v_cc_ln_b'], 'v_cc_pw_w': out['v_cc_pw_w'], 'v_w_kv': out['v_w_kv'], 'v_w_out': out['v_w_out'], 'v_ln_g': out['v_ln_g'], 'v_ln_b': out['v_ln_b']}


def _loss(weights, diff, rest, loss_target):
    with _jax.named_scope("forward"):
        args = {**rest, TWIN_DIFF_INPUT: diff, **{k: w.astype(_WEIGHT_DTYPES[k]) for k, w in weights.items()}}
        y = _forward(args)
    with _jax.named_scope("loss_head"):
        err = _jnp.square(y.astype(_jnp.float32) - loss_target)
        return 0.5 * _jnp.sum(_jnp.mean(err, axis=-1)) if err.ndim else 0.5 * err


def _adamw(w, g, m, v):
    m = ADAM_B1 * m + (1.0 - ADAM_B1) * g
    v = ADAM_B2 * v + (1.0 - ADAM_B2) * _jnp.square(g)
    m_hat = m / (1.0 - ADAM_B1 ** ADAM_STEP)
    v_hat = v / (1.0 - ADAM_B2 ** ADAM_STEP)
    delta = -ADAM_LR * (m_hat / (_jnp.sqrt(v_hat) + ADAM_EPS) + ADAM_WD * w)
    return delta, m, v


def reference(x, mem, w_in, conv_a_w, sg_ln_g, sg_ln_b, sg_w, sg_b, pool_w, pool_scale, cc_dw_w, cc_dw_b, cc_ln_g, cc_ln_b, cc_pw_w, w_kv, w_out, ln_g, ln_b, loss_target, m_w_in, m_conv_a_w, m_sg_ln_g, m_sg_ln_b, m_sg_w, m_sg_b, m_pool_w, m_pool_scale, m_cc_dw_w, m_cc_dw_b, m_cc_ln_g, m_cc_ln_b, m_cc_pw_w, m_w_kv, m_w_out, m_ln_g, m_ln_b, v_w_in, v_conv_a_w, v_sg_ln_g, v_sg_ln_b, v_sg_w, v_sg_b, v_pool_w, v_pool_scale, v_cc_dw_w, v_cc_dw_b, v_cc_ln_g, v_cc_ln_b, v_cc_pw_w, v_w_kv, v_w_out, v_ln_g, v_ln_b):
    given = dict(x=x, mem=mem, w_in=w_in, conv_a_w=conv_a_w, sg_ln_g=sg_ln_g, sg_ln_b=sg_ln_b, sg_w=sg_w, sg_b=sg_b, pool_w=pool_w, pool_scale=pool_scale, cc_dw_w=cc_dw_w, cc_dw_b=cc_dw_b, cc_ln_g=cc_ln_g, cc_ln_b=cc_ln_b, cc_pw_w=cc_pw_w, w_kv=w_kv, w_out=w_out, ln_g=ln_g, ln_b=ln_b, loss_target=loss_target, m_w_in=m_w_in, m_conv_a_w=m_conv_a_w, m_sg_ln_g=m_sg_ln_g, m_sg_ln_b=m_sg_ln_b, m_sg_w=m_sg_w, m_sg_b=m_sg_b, m_pool_w=m_pool_w, m_pool_scale=m_pool_scale, m_cc_dw_w=m_cc_dw_w, m_cc_dw_b=m_cc_dw_b, m_cc_ln_g=m_cc_ln_g, m_cc_ln_b=m_cc_ln_b, m_cc_pw_w=m_cc_pw_w, m_w_kv=m_w_kv, m_w_out=m_w_out, m_ln_g=m_ln_g, m_ln_b=m_ln_b, v_w_in=v_w_in, v_conv_a_w=v_conv_a_w, v_sg_ln_g=v_sg_ln_g, v_sg_ln_b=v_sg_ln_b, v_sg_w=v_sg_w, v_sg_b=v_sg_b, v_pool_w=v_pool_w, v_pool_scale=v_pool_scale, v_cc_dw_w=v_cc_dw_w, v_cc_dw_b=v_cc_dw_b, v_cc_ln_g=v_cc_ln_g, v_cc_ln_b=v_cc_ln_b, v_cc_pw_w=v_cc_pw_w, v_w_kv=v_w_kv, v_w_out=v_w_out, v_ln_g=v_ln_g, v_ln_b=v_ln_b)
    weights = {n: given[n] for n in TWIN_WEIGHTS}
    shared = {n: given[n] for n in SHARED_INPUTS}
    per_example = {n: given[n] for n in ['x', 'mem']}
    grad_fn = _jax.value_and_grad(_loss, argnums=(0, 1))

    def one_microbatch(ex, loss_target):
        ex = dict(ex)
        diff = ex.pop(TWIN_DIFF_INPUT)
        return grad_fn(weights, diff, {**shared, **ex}, loss_target)

    if N_MICROBATCH == 1:
        loss, (grad_w, grad_x) = one_microbatch(per_example, given["loss_target"])
    else:
        def body(carry, xs):
            loss_sum, grad_sum = carry
            l_k, (gw_k, gx_k) = one_microbatch(xs[0], xs[1])
            with _jax.named_scope("update"):
                return (loss_sum + l_k, _jax.tree.map(_jnp.add, grad_sum, gw_k)), gx_k

        init = (_jnp.zeros((), _jnp.float32), _jax.tree.map(_jnp.zeros_like, weights))
        (loss, grad_w), grad_x = _jax.lax.scan(body, init, (per_example, given["loss_target"]))
    with _jax.named_scope("update"):
        delta_w, new_m, new_v = {}, {}, {}
        for n in TWIN_WEIGHTS:
            delta_w[n], new_m[n], new_v[n] = _adamw(weights[n], grad_w[n], given["m_" + n], given["v_" + n])
    return (loss, grad_x, *[grad_w[n] for n in TWIN_WEIGHTS], *[delta_w[n] for n in TWIN_WEIGHTS],
            *[new_m[n] for n in TWIN_WEIGHTS], *[new_v[n] for n in TWIN_WEIGHTS])
```

```python
import functools
import math

import jax
import jax.numpy as jnp
from jax import lax
from jax.experimental import pallas as pl
from jax.experimental.pallas import tpu as pltpu

F32 = jnp.float32
BF16 = jnp.bfloat16

D_MODEL = 1024
DEPTH = 2
D_G = 256
D_MIX = 5 * D_G
D_IN = 9 * D_G + D_MIX
N_SUB = 4
HEAD_DIM = 64
CONV_A = 3
CONV_D = 31
CHUNK = 128
MEM_LEN = 256
N_CHIPS = 4
W_IN_SHARD = D_IN // N_CHIPS
LN_EPS = 1e-5
ALPHA = (2.0 * DEPTH) ** 0.25
ATT_SCALE = 1.0 / math.sqrt(HEAD_DIM)
GELU_C = math.sqrt(2.0 / math.pi)
GELU_A = 0.044715

ADAM_LR = 0.001
ADAM_B1 = 0.9
ADAM_B2 = 0.999
ADAM_EPS = 1e-08
ADAM_WD = 0.01
ADAM_STEP = 10

C_XA, C_BA, C_CA, C_U, C_V, C_XC, C_DA, C_DG, C_Q, C_GATE = (D_G * i for i in range(10))

HALO_A = 8
HALO_C = 16
HALO_D = 32

RW_VEC = 0
RW_CONVA = 16
RW_DW = 24
RW_SGB = 56
RW_ROWS = RW_SGB + CHUNK

RG_VEC = 0
RG_CONVA = 16
RG_DW = 24
RG_SGW = 56
RG_SGB = RG_SGW + 2 * CHUNK
RG_POOL = RG_SGB + CHUNK
RG_LN = RG_POOL + D_G
RG_ROWS = 768

VMEM_LIMIT = 56 * 1024 * 1024
SEQ_TILE = 256
MM_TILE = 512

MESH = pl.DeviceIdType.MESH
ANY = pl.BlockSpec(memory_space=pl.ANY)
NT = (((1,), (1,)), ((), ()))
TN = (((0,), (0,)), ((), ()))


def _dot(a, b):
    return jnp.dot(a, b, preferred_element_type=F32)


def _dot_nt(a, b):
    return lax.dot_general(a, b, NT, preferred_element_type=F32)


def _dot_tn(a, b):
    return lax.dot_general(a, b, TN, preferred_element_type=F32)


def _full(shape):
    zeros = (0,) * len(shape)
    return pl.BlockSpec(shape, lambda *_: zeros)


def _sigmoid(x):
    return 1.0 / (1.0 + jnp.exp(-x))


def _gelu(x):
    t = jnp.tanh(GELU_C * (x + GELU_A * x * x * x))
    return 0.5 * x * (1.0 + t), t


def _gelu_grad(x, t):
    return 0.5 * (1.0 + t) + 0.5 * x * (1.0 - t * t) * (GELU_C * (1.0 + 3.0 * GELU_A * x * x))


def _normalize(v):
    mu = jnp.mean(v, axis=-1, keepdims=True)
    d = v - mu
    var = jnp.mean(d * d, axis=-1, keepdims=True)
    rstd = lax.rsqrt(var + LN_EPS)
    return d * rstd, rstd


def _normalize_grad(dhat, hat, rstd):
    m1 = jnp.mean(dhat, axis=-1, keepdims=True)
    m2 = jnp.mean(dhat * hat, axis=-1, keepdims=True)
    return rstd * (dhat - m1 - hat * m2)


def _lane(width=D_G):
    return lax.broadcasted_iota(jnp.int32, (1, width), 1)


def _head_masks():
    head = _lane() // HEAD_DIM
    return [(head == h).astype(F32) for h in range(N_SUB)]


def _stack_heads(v, masks):
    return jnp.concatenate([v * m for m in masks], axis=0)


def _tril_mask_cat():
    t = lax.broadcasted_iota(jnp.int32, (CHUNK, N_SUB * CHUNK), 0)
    s = lax.broadcasted_iota(jnp.int32, (CHUNK, N_SUB * CHUNK), 1) % CHUNK
    return s <= t


def _triu_mask_cat():
    s = lax.broadcasted_iota(jnp.int32, (CHUNK, N_SUB * CHUNK), 0)
    t = lax.broadcasted_iota(jnp.int32, (CHUNK, N_SUB * CHUNK), 1) % CHUNK
    return t >= s


def _pool_select(a2, a4, a8, a16):
    lane = _lane()
    return jnp.where(lane < 64, a2, jnp.where(lane < 128, a4, jnp.where(lane < 192, a8, a16)))


def _pool_count(row0, rows):
    t = row0 + lax.broadcasted_iota(jnp.int32, (rows, D_G), 0)
    lane = lax.broadcasted_iota(jnp.int32, (rows, D_G), 1)
    win = jnp.where(lane < 64, 2, jnp.where(lane < 128, 4, jnp.where(lane < 192, 8, 16)))
    return jnp.minimum(t + 1, win).astype(F32)


def _trailing_window_sum(halo, cur):
    e = jnp.concatenate([halo, cur], axis=0)
    s2 = e + pltpu.roll(e, 1, 0)
    s4 = s2 + pltpu.roll(s2, 2, 0)
    s8 = s4 + pltpu.roll(s4, 4, 0)
    s16 = s8 + pltpu.roll(s8, 8, 0)
    return _pool_select(s2, s4, s8, s16)[HALO_C:]


def _leading_window_sum(cur, halo):
    e = jnp.concatenate([cur, halo], axis=0)
    n = e.shape[0]
    s2 = e + pltpu.roll(e, n - 1, 0)
    s4 = s2 + pltpu.roll(s2, n - 2, 0)
    s8 = s4 + pltpu.roll(s4, n - 4, 0)
    s16 = s8 + pltpu.roll(s8, n - 8, 0)
    return _pool_select(s2, s4, s8, s16)[: cur.shape[0]]


def _softmax_blocks(sc):
    out = []
    for h in range(N_SUB):
        s = sc[:, h * MEM_LEN : (h + 1) * MEM_LEN]
        e = jnp.exp(s - jnp.max(s, axis=-1, keepdims=True))
        out.append(e / jnp.sum(e, axis=-1, keepdims=True))
    return jnp.concatenate(out, axis=-1)


def _sgu_mix(vn, wcat_b, sgb, masks):
    vbd = _stack_heads(vn, masks).astype(BF16)
    return _dot(wcat_b, vbd) + sgb, vbd


def _layer_forward(x, win, kst, vst, wout, sw, wcat, poolw, pww, ln, tgt, *, name):
    seq = x.shape[0]
    tile = min(SEQ_TILE, seq)
    n_tiles = seq // tile
    last = tgt is not None

    def body(*refs):
        x_ref, win_ref, kst_ref, vst_ref, wout_ref, sw_ref, wcat_ref, pool_ref, pw_ref, ln_ref = refs[:10]
        refs = refs[10:]
        if last:
            tgt_ref, refs = refs[0], refs[1:]
        proj_ref, y_ref, z_ref, out_ref = refs[:4]
        refs = refs[4:]
        if last:
            loss_ref, refs = refs[0], refs[1:]
        pbuf, xchalo, gbuf = refs
        i = pl.program_id(0)

        @pl.when(i == 0)
        def _():
            pbuf[0:HALO_A, :] = jnp.zeros((HALO_A, D_G), F32)
            xchalo[...] = jnp.zeros((HALO_C, D_G), F32)
            gbuf[0:HALO_D, :] = jnp.zeros((HALO_D, D_G), F32)
            if last:
                loss_ref[...] = jnp.zeros((8, 128), F32)

        xt = x_ref[...]
        xb = xt.astype(BF16)
        for k in range(N_CHIPS):
            proj_ref[:, k * W_IN_SHARD : (k + 1) * W_IN_SHARD] = _dot(xb, win_ref[k])
        masks = _head_masks()

        pbuf[HALO_A : HALO_A + tile, :] = proj_ref[:, C_CA : C_CA + D_G] * proj_ref[:, C_XA : C_XA + D_G]
        cv = jnp.zeros((tile, D_G), F32)
        for k in range(CONV_A):
            off = HALO_A - (CONV_A - 1) + k
            cv = cv + sw_ref[RW_CONVA + k : RW_CONVA + k + 1, :] * pbuf[off : off + tile, :]
        y_ref[:, 0:D_G] = proj_ref[:, C_BA : C_BA + D_G] * cv
        pbuf[0:HALO_A, :] = pbuf[tile : tile + HALO_A, :]

        ua, _ = _gelu(proj_ref[:, C_U : C_U + D_G])
        vg, _ = _gelu(proj_ref[:, C_V : C_V + D_G])
        vhat, _ = _normalize(vg)
        vn = vhat * sw_ref[RW_VEC : RW_VEC + 1, :] + sw_ref[RW_VEC + 1 : RW_VEC + 2, :]
        wcat_b = jnp.where(_tril_mask_cat(), wcat_ref[...], 0.0).astype(BF16)
        sgb = sw_ref[RW_SGB : RW_SGB + CHUNK, :]
        for j in range(tile // CHUNK):
            rows = slice(j * CHUNK, (j + 1) * CHUNK)
            mixed, _ = _sgu_mix(vn[rows], wcat_b, sgb, masks)
            y_ref[rows, D_G : 2 * D_G] = ua[rows] * mixed

        xc = proj_ref[:, C_XC : C_XC + D_G]
        wsum = _trailing_window_sum(xchalo[...], xc)
        pm = wsum / _pool_count(i * tile, tile) - xc
        y_ref[:, 2 * D_G : 3 * D_G] = _dot(pm.astype(BF16), pool_ref[...]) * sw_ref[RW_VEC + 2 : RW_VEC + 3, :]
        xchalo[...] = xc[tile - HALO_C :, :]

        gbuf[HALO_D : HALO_D + tile, :] = proj_ref[:, C_DA : C_DA + D_G] * _sigmoid(proj_ref[:, C_DG : C_DG + D_G])
        cvd = jnp.zeros((tile, D_G), F32) + sw_ref[RW_VEC + 3 : RW_VEC + 4, :]
        for k in range(CONV_D):
            off = HALO_D - (CONV_D - 1) + k
            cvd = cvd + sw_ref[RW_DW + k : RW_DW + k + 1, :] * gbuf[off : off + tile, :]
        nhat, _ = _normalize(cvd)
        nrm = nhat * sw_ref[RW_VEC + 4 : RW_VEC + 5, :] + sw_ref[RW_VEC + 5 : RW_VEC + 6, :]
        y_ref[:, 3 * D_G : 4 * D_G] = _dot((nrm * _sigmoid(nrm)).astype(BF16), pw_ref[...])
        gbuf[0:HALO_D, :] = gbuf[tile : tile + HALO_D, :]

        qb = proj_ref[:, C_Q : C_Q + D_G].astype(BF16)
        p_all = _softmax_blocks(_dot_nt(qb, kst_ref[...]) * ATT_SCALE)
        y_ref[:, 4 * D_G : 5 * D_G] = _dot(p_all.astype(BF16), vst_ref[...])

        gate = proj_ref[:, C_GATE:]
        hid = y_ref[...] * (gate * _sigmoid(gate))
        z = ALPHA * xt + _dot(hid.astype(BF16), wout_ref[...])
        z_ref[...] = z
        zhat, _ = _normalize(z)
        xn = zhat * ln_ref[0:1, :] + ln_ref[1:2, :]
        if last:
            err = xn - tgt_ref[...]
            out_ref[...] = err * (1.0 / D_MODEL)
            loss_ref[...] += jnp.sum(err * err) * (0.5 / D_MODEL)
        else:
            out_ref[...] = xn

    def rows(width):
        return pl.BlockSpec((tile, width), lambda i: (i, 0))

    operands = [x, win, kst, vst, wout, sw, wcat, poolw, pww, ln]
    in_specs = [rows(D_MODEL)] + [_full(a.shape) for a in operands[1:]]
    out_shape = [
        jax.ShapeDtypeStruct((seq, D_IN), F32),
        jax.ShapeDtypeStruct((seq, D_MIX), F32),
        jax.ShapeDtypeStruct((seq, D_MODEL), F32),
        jax.ShapeDtypeStruct((seq, D_MODEL), F32),
    ]
    out_specs = [rows(D_IN), rows(D_MIX), rows(D_MODEL), rows(D_MODEL)]
    if last:
        operands.append(tgt)
        in_specs.append(rows(D_MODEL))
        out_shape.append(jax.ShapeDtypeStruct((8, 128), F32))
        out_specs.append(_full((8, 128)))
    return pl.pallas_call(
        body,
        name=name,
        grid=(n_tiles,),
        in_specs=in_specs,
        out_specs=out_specs,
        out_shape=out_shape,
        scratch_shapes=[
            pltpu.VMEM((HALO_A + tile, D_G), F32),
            pltpu.VMEM((HALO_C, D_G), F32),
            pltpu.VMEM((HALO_D + tile, D_G), F32),
        ],
        compiler_params=pltpu.CompilerParams(dimension_semantics=("arbitrary",), vmem_limit_bytes=VMEM_LIMIT),
    )(*operands)


def _layer_backward(dxn, z, proj, y, kst, vst, wout, sw, wcat, wcat_t, poolw, pww, ln, *, name):
    seq = dxn.shape[0]
    tile = min(SEQ_TILE, seq)
    n_tiles = seq // tile
    halo_blocks = tile // HALO_D

    def body(
        dxn_ref, z_ref, proj_ref, halo_ref, y_ref, kst_ref, vst_ref, wout_ref, sw_ref, wcat_ref, wcat_t_ref, pool_ref,
        pw_ref, ln_ref, dproj_ref, dz_ref, dwout_ref, dkst_ref, dvst_ref, dpw_ref, sg_ref,
        pbuf, dcvbuf, rhalo, gbuf, dgbuf,
    ):
        i = pl.program_id(0)
        ti = n_tiles - 1 - i

        @pl.when(i == 0)
        def _():
            dwout_ref[...] = jnp.zeros(dwout_ref.shape, F32)
            dkst_ref[...] = jnp.zeros(dkst_ref.shape, F32)
            dvst_ref[...] = jnp.zeros(dvst_ref.shape, F32)
            dpw_ref[...] = jnp.zeros(dpw_ref.shape, F32)
            sg_ref[...] = jnp.zeros(sg_ref.shape, F32)
            dcvbuf[tile : tile + HALO_A, :] = jnp.zeros((HALO_A, D_G), F32)
            rhalo[...] = jnp.zeros((HALO_C, D_G), F32)
            dgbuf[tile : tile + HALO_D, :] = jnp.zeros((HALO_D, D_G), F32)

        def acc_row(row, val):
            sg_ref[row : row + 1, :] += jnp.sum(val, axis=0, keepdims=True)

        masks = _head_masks()
        has_past = (ti > 0).astype(F32)

        zhat, zrstd = _normalize(z_ref[...])
        dxn_t = dxn_ref[...]
        dlg = jnp.sum(dxn_t * zhat, axis=0, keepdims=True)
        dlb = jnp.sum(dxn_t, axis=0, keepdims=True)
        for j in range(D_MODEL // D_G):
            sg_ref[RG_LN + j : RG_LN + j + 1, :] += dlg[:, j * D_G : (j + 1) * D_G]
            sg_ref[RG_LN + 4 + j : RG_LN + 5 + j, :] += dlb[:, j * D_G : (j + 1) * D_G]
        dz = _normalize_grad(dxn_t * ln_ref[0:1, :], zhat, zrstd)
        dz_ref[...] = dz
        dzb = dz.astype(BF16)

        gate = proj_ref[:, C_GATE:]
        sgm = _sigmoid(gate)
        silu = gate * sgm
        yc = y_ref[...]
        dwout_ref[...] += _dot_tn((yc * silu).astype(BF16), dzb)
        dh = _dot_nt(dzb, wout_ref[...])
        dproj_ref[:, C_GATE:] = (dh * yc * (sgm * (1.0 + gate * (1.0 - sgm)))).astype(BF16)
        dy = dh * silu

        dya = dy[:, 0:D_G]
        xa = proj_ref[:, C_XA : C_XA + D_G]
        ba = proj_ref[:, C_BA : C_BA + D_G]
        ca = proj_ref[:, C_CA : C_CA + D_G]
        past = slice(HALO_D - HALO_A, HALO_D)
        pbuf[0:HALO_A, :] = halo_ref[past, C_CA : C_CA + D_G] * halo_ref[past, C_XA : C_XA + D_G] * has_past
        pbuf[HALO_A : HALO_A + tile, :] = ca * xa
        cv = jnp.zeros((tile, D_G), F32)
        for k in range(CONV_A):
            off = HALO_A - (CONV_A - 1) + k
            cv = cv + sw_ref[RW_CONVA + k : RW_CONVA + k + 1, :] * pbuf[off : off + tile, :]
        dproj_ref[:, C_BA : C_BA + D_G] = (dya * cv).astype(BF16)
        dcv = dya * ba
        dcvbuf[0:tile, :] = dcv
        dp = jnp.zeros((tile, D_G), F32)
        for k in range(CONV_A):
            off = HALO_A - (CONV_A - 1) + k
            acc_row(RG_CONVA + k, dcv * pbuf[off : off + tile, :])
            back = CONV_A - 1 - k
            dp = dp + sw_ref[RW_CONVA + k : RW_CONVA + k + 1, :] * dcvbuf[back : back + tile, :]
        dproj_ref[:, C_CA : C_CA + D_G] = (dp * xa).astype(BF16)
        dproj_ref[:, C_XA : C_XA + D_G] = (dp * ca).astype(BF16)
        dcvbuf[tile : tile + HALO_A, :] = dcvbuf[0:HALO_A, :]

        dyb = dy[:, D_G : 2 * D_G]
        u = proj_ref[:, C_U : C_U + D_G]
        v = proj_ref[:, C_V : C_V + D_G]
        ua, ut = _gelu(u)
        vg, vt = _gelu(v)
        vhat, vrstd = _normalize(vg)
        sg_g = sw_ref[RW_VEC : RW_VEC + 1, :]
        vn = vhat * sg_g + sw_ref[RW_VEC + 1 : RW_VEC + 2, :]
        tril = _tril_mask_cat()
        wcat_b = jnp.where(tril, wcat_ref[...], 0.0).astype(BF16)
        wcat_tb = jnp.where(_triu_mask_cat(), wcat_t_ref[...], 0.0).astype(BF16)
        sgb = sw_ref[RW_SGB : RW_SGB + CHUNK, :]
        dmixed = dyb * ua
        dvn_parts = []
        du_parts = []
        dwcat = jnp.zeros((CHUNK, N_SUB * CHUNK), F32)
        dsgb = jnp.zeros((CHUNK, D_G), F32)
        for j in range(tile // CHUNK):
            rows = slice(j * CHUNK, (j + 1) * CHUNK)
            mixed, vbd = _sgu_mix(vn[rows], wcat_b, sgb, masks)
            du_parts.append(dyb[rows] * mixed)
            dmx = dmixed[rows]
            dsgb = dsgb + dmx
            dwcat = dwcat + _dot_nt(dmx.astype(BF16), vbd)
            dvn_parts.append(_dot(wcat_tb, _stack_heads(dmx, masks).astype(BF16)))
        dwcat = jnp.where(tril, dwcat, 0.0)
        sg_ref[RG_SGW : RG_SGW + CHUNK, :] += dwcat[:, 0:D_G]
        sg_ref[RG_SGW + CHUNK : RG_SGW + 2 * CHUNK, :] += dwcat[:, D_G:]
        sg_ref[RG_SGB : RG_SGB + CHUNK, :] += dsgb
        dvn = jnp.concatenate(dvn_parts, axis=0)
        du_act = jnp.concatenate(du_parts, axis=0)
        acc_row(RG_VEC, dvn * vhat)
        acc_row(RG_VEC + 1, dvn)
        dvg = _normalize_grad(dvn * sg_g, vhat, vrstd)
        dproj_ref[:, C_U : C_U + D_G] = (du_act * _gelu_grad(u, ut)).astype(BF16)
        dproj_ref[:, C_V : C_V + D_G] = (dvg * _gelu_grad(v, vt)).astype(BF16)

        dyc = dy[:, 2 * D_G : 3 * D_G]
        xc = proj_ref[:, C_XC : C_XC + D_G]
        xc_past = halo_ref[HALO_D - HALO_C : HALO_D, C_XC : C_XC + D_G] * has_past
        cnt = _pool_count(ti * tile, tile)
        pm = _trailing_window_sum(xc_past, xc) / cnt - xc
        pmb = pm.astype(BF16)
        pool_b = pool_ref[...]
        scale = sw_ref[RW_VEC + 2 : RW_VEC + 3, :]
        acc_row(RG_VEC + 2, dyc * _dot(pmb, pool_b))
        dpre = (dyc * scale).astype(BF16)
        sg_ref[RG_POOL : RG_POOL + D_G, :] += _dot_tn(pmb, dpre)
        dpm = _dot_nt(dpre, pool_b)
        r = dpm / cnt
        dproj_ref[:, C_XC : C_XC + D_G] = (_leading_window_sum(r, rhalo[...]) - dpm).astype(BF16)
        rhalo[...] = r[0:HALO_C, :]

        dyd = dy[:, 3 * D_G : 4 * D_G]
        da = proj_ref[:, C_DA : C_DA + D_G]
        sgd = _sigmoid(proj_ref[:, C_DG : C_DG + D_G])
        gbuf[0:HALO_D, :] = halo_ref[:, C_DA : C_DA + D_G] * _sigmoid(halo_ref[:, C_DG : C_DG + D_G]) * has_past
        gbuf[HALO_D : HALO_D + tile, :] = da * sgd
        cvd = jnp.zeros((tile, D_G), F32) + sw_ref[RW_VEC + 3 : RW_VEC + 4, :]
        for k in range(CONV_D):
            off = HALO_D - (CONV_D - 1) + k
            cvd = cvd + sw_ref[RW_DW + k : RW_DW + k + 1, :] * gbuf[off : off + tile, :]
        nhat, nrstd = _normalize(cvd)
        cc_g = sw_ref[RW_VEC + 4 : RW_VEC + 5, :]
        nrm = nhat * cc_g + sw_ref[RW_VEC + 5 : RW_VEC + 6, :]
        sgn = _sigmoid(nrm)
        dydb = dyd.astype(BF16)
        dpw_ref[...] += _dot_tn((nrm * sgn).astype(BF16), dydb)
        dn = _dot_nt(dydb, pw_ref[...]) * (sgn * (1.0 + nrm * (1.0 - sgn)))
        acc_row(RG_VEC + 4, dn * nhat)
        acc_row(RG_VEC + 5, dn)
        dcvd = _normalize_grad(dn * cc_g, nhat, nrstd)
        acc_row(RG_VEC + 3, dcvd)
        dgbuf[0:tile, :] = dcvd
        dg = jnp.zeros((tile, D_G), F32)
        for k in range(CONV_D):
            off = HALO_D - (CONV_D - 1) + k
            acc_row(RG_DW + k, dcvd * gbuf[off : off + tile, :])
            back = CONV_D - 1 - k
            dg = dg + sw_ref[RW_DW + k : RW_DW + k + 1, :] * dgbuf[back : back + tile, :]
        dproj_ref[:, C_DA : C_DA + D_G] = (dg * sgd).astype(BF16)
        dproj_ref[:, C_DG : C_DG + D_G] = (dg * da * sgd * (1.0 - sgd)).astype(BF16)
        dgbuf[tile : tile + HALO_D, :] = dgbuf[0:HALO_D, :]

        dyeb = dy[:, 4 * D_G : 5 * D_G].astype(BF16)
        qb = proj_ref[:, C_Q : C_Q + D_G].astype(BF16)
        kst_b = kst_ref[...]
        p_all = _softmax_blocks(_dot_nt(qb, kst_b) * ATT_SCALE)
        dvst_ref[...] += _dot_tn(p_all.astype(BF16), dyeb)
        dp_all = _dot_nt(dyeb, vst_ref[...])
        ds = []
        for h in range(N_SUB):
            blk = slice(h * MEM_LEN, (h + 1) * MEM_LEN)
            p, dpb = p_all[:, blk], dp_all[:, blk]
            ds.append(p * (dpb - jnp.sum(dpb * p, axis=-1, keepdims=True)))
        dsb = (jnp.concatenate(ds, axis=-1) * ATT_SCALE).astype(BF16)
        dproj_ref[:, C_Q : C_Q + D_G] = _dot(dsb, kst_b).astype(BF16)
        dkst_ref[...] += _dot_tn(dsb, qb)

    def rows(width):
        return pl.BlockSpec((tile, width), lambda i: (n_tiles - 1 - i, 0))

    halo_spec = pl.BlockSpec((HALO_D, D_IN), lambda i: (jnp.maximum((n_tiles - 1 - i) * halo_blocks - 1, 0), 0))
    weights = [kst, vst, wout, sw, wcat, wcat_t, poolw, pww, ln]
    acc_shapes = [(D_MIX, D_MODEL), (N_SUB * MEM_LEN, D_G), (N_SUB * MEM_LEN, D_G), (D_G, D_G), (RG_ROWS, D_G)]
    return pl.pallas_call(
        body,
        name=name,
        grid=(n_tiles,),
        in_specs=[rows(D_MODEL), rows(D_MODEL), rows(D_IN), halo_spec, rows(D_MIX)] + [_full(a.shape) for a in weights],
        out_specs=[rows(D_IN), rows(D_MODEL)] + [_full(s) for s in acc_shapes],
        out_shape=[jax.ShapeDtypeStruct((seq, D_IN), BF16), jax.ShapeDtypeStruct((seq, D_MODEL), F32)]
        + [jax.ShapeDtypeStruct(s, F32) for s in acc_shapes],
        scratch_shapes=[
            pltpu.VMEM((HALO_A + tile, D_G), F32),
            pltpu.VMEM((tile + HALO_A, D_G), F32),
            pltpu.VMEM((HALO_C, D_G), F32),
            pltpu.VMEM((HALO_D + tile, D_G), F32),
            pltpu.VMEM((tile + HALO_D, D_G), F32),
        ],
        compiler_params=pltpu.CompilerParams(dimension_semantics=("arbitrary",), vmem_limit_bytes=VMEM_LIMIT),
    )(dxn, z, proj, proj, y, *weights)


def _kv_forward(mem, wkv, *, name):
    def body(mem_ref, wkv_ref, kst_ref, vst_ref):
        kv = _dot(mem_ref[...].astype(BF16), wkv_ref[...])
        masks = _head_masks()
        kst_ref[...] = _stack_heads(kv[:, 0:D_G], masks).astype(BF16)
        vst_ref[...] = _stack_heads(kv[:, D_G:], masks).astype(BF16)

    shape = jax.ShapeDtypeStruct((N_SUB * MEM_LEN, D_G), BF16)
    return pl.pallas_call(body, name=name, out_shape=[shape, shape])(mem, wkv)


def _kv_backward(mem, dkst, dvst, *, name):
    def body(mem_ref, dkst_ref, dvst_ref, dwkv_ref):
        masks = _head_masks()
        memb = mem_ref[...].astype(BF16)
        for col, ref in ((0, dkst_ref), (D_G, dvst_ref)):
            d = jnp.zeros((MEM_LEN, D_G), F32)
            for h in range(N_SUB):
                d = d + ref[h * MEM_LEN : (h + 1) * MEM_LEN, :] * masks[h]
            dwkv_ref[:, col : col + D_G] = _dot_tn(memb, d.astype(BF16))

    return pl.pallas_call(body, name=name, out_shape=jax.ShapeDtypeStruct((D_MODEL, 2 * D_G), F32))(mem, dkst, dvst)


def _input_grad(dproj, dz, win, *, name):
    seq = dproj.shape[0]
    tile = min(MM_TILE, seq)

    def body(dproj_ref, dz_ref, win_ref, dx_ref):
        acc = ALPHA * dz_ref[...]
        for k in range(N_CHIPS):
            acc = acc + _dot_nt(dproj_ref[:, k * W_IN_SHARD : (k + 1) * W_IN_SHARD], win_ref[k])
        dx_ref[...] = acc

    return pl.pallas_call(
        body,
        name=name,
        grid=(seq // tile,),
        in_specs=[
            pl.BlockSpec((tile, D_IN), lambda i: (i, 0)),
            pl.BlockSpec((tile, D_MODEL), lambda i: (i, 0)),
            _full(win.shape),
        ],
        out_specs=pl.BlockSpec((tile, D_MODEL), lambda i: (i, 0)),
        out_shape=jax.ShapeDtypeStruct((seq, D_MODEL), F32),
        compiler_params=pltpu.CompilerParams(dimension_semantics=("arbitrary",), vmem_limit_bytes=VMEM_LIMIT),
    )(dproj, dz, win)


def _input_weight_grad(x, dproj, *, name):
    seq = x.shape[0]
    tile = min(MM_TILE, seq)

    def body(x_ref, dproj_ref, dwin_ref):
        @pl.when(pl.program_id(1) == 0)
        def _():
            dwin_ref[...] = jnp.zeros(dwin_ref.shape, F32)

        dwin_ref[0] += _dot_tn(x_ref[...].astype(BF16), dproj_ref[...])

    return pl.pallas_call(
        body,
        name=name,
        grid=(N_CHIPS, seq // tile),
        in_specs=[
            pl.BlockSpec((tile, D_MODEL), lambda k, i: (i, 0)),
            pl.BlockSpec((tile, W_IN_SHARD), lambda k, i: (i, k)),
        ],
        out_specs=pl.BlockSpec((1, D_MODEL, W_IN_SHARD), lambda k, i: (k, 0, 0)),
        out_shape=jax.ShapeDtypeStruct((N_CHIPS, D_MODEL, W_IN_SHARD), F32),
        compiler_params=pltpu.CompilerParams(dimension_semantics=("arbitrary", "arbitrary"), vmem_limit_bytes=VMEM_LIMIT),
    )(x, dproj)


def _expand_sgb(sg_b):
    return jnp.repeat(sg_b.T, HEAD_DIM, axis=1)


def _pack_small_weights(sg_ln_g, sg_ln_b, pool_scale, cc_dw_b, cc_ln_g, cc_ln_b, conv_a_w, cc_dw_w, sg_b):
    vec = jnp.stack([sg_ln_g, sg_ln_b, pool_scale, cc_dw_b, cc_ln_g, cc_ln_b])
    return jnp.concatenate(
        [
            jnp.pad(vec, ((0, RW_CONVA - RW_VEC - 6), (0, 0))),
            jnp.pad(conv_a_w, ((0, RW_DW - RW_CONVA - CONV_A), (0, 0))),
            jnp.pad(cc_dw_w, ((0, RW_SGB - RW_DW - CONV_D), (0, 0))),
            _expand_sgb(sg_b),
        ]
    )


def _sg_w_cat(sg_w):
    cat = jnp.transpose(sg_w, (1, 0, 2)).reshape(CHUNK, N_SUB * CHUNK)
    cat_t = jnp.transpose(sg_w, (2, 0, 1)).reshape(CHUNK, N_SUB * CHUNK)
    return cat, cat_t


def _pool_block_diag(pool_w):
    out = jnp.zeros((D_G, D_G), pool_w.dtype)
    for g in range(N_SUB):
        out = out.at[g * HEAD_DIM : (g + 1) * HEAD_DIM, g * HEAD_DIM : (g + 1) * HEAD_DIM].set(pool_w[g])
    return out


def _local_step(x, mem, tgt, layers):
    prepared = []
    for l, w in enumerate(layers):
        cat, cat_t = _sg_w_cat(w["sg_w"])
        kst, vst = _kv_forward(mem, w["w_kv"], name=f"kv_fwd{l}")
        prepared.append(
            dict(
                win=w["w_in"],
                wout=w["w_out"],
                pww=w["cc_pw_w"],
                sw=_pack_small_weights(
                    w["sg_ln_g"], w["sg_ln_b"], w["pool_scale"], w["cc_dw_b"], w["cc_ln_g"], w["cc_ln_b"],
                    w["conv_a_w"], w["cc_dw_w"], w["sg_b"],
                ),
                wcat=cat,
                wcat_t=cat_t,
                poolw=_pool_block_diag(w["pool_w"]).astype(BF16),
                ln=jnp.stack([w["ln_g"], w["ln_b"]]),
                kst=kst,
                vst=vst,
            )
        )

    saved = []
    h = x
    loss = None
    for l, p in enumerate(prepared):
        last = l == len(prepared) - 1
        outs = _layer_forward(
            h, p["win"], p["kst"], p["vst"], p["wout"], p["sw"], p["wcat"], p["poolw"], p["pww"], p["ln"],
            tgt if last else None, name=f"layer_fwd{l}",
        )
        saved.append(dict(x=h, proj=outs[0], y=outs[1], z=outs[2]))
        h = outs[3]
        if last:
            loss = outs[4][0, 0]

    grads = [None] * len(prepared)
    dxn = h
    for l in reversed(range(len(prepared))):
        p, s = prepared[l], saved[l]
        dproj, dz, dwout, dkst, dvst, dpw, small = _layer_backward(
            dxn, s["z"], s["proj"], s["y"], p["kst"], p["vst"], p["wout"], p["sw"], p["wcat"], p["wcat_t"], p["poolw"],
            p["pww"], p["ln"], name=f"layer_bwd{l}",
        )
        grads[l] = dict(
            w_in=_input_weight_grad(s["x"], dproj, name=f"w_in_grad{l}"),
            w_out=dwout,
            w_kv=_kv_backward(mem, dkst, dvst, name=f"kv_bwd{l}"),
            cc_pw_w=dpw,
            small=small,
        )
        dxn = _input_grad(dproj, dz, p["win"], name=f"input_grad{l}")
    return loss, dxn, grads


def _place():
    x, y, c = lax.axis_index("x"), lax.axis_index("y"), lax.axis_index("c")
    others = [(1 - x, y), (x, 1 - y), (1 - x, 1 - y)]
    return x, y, c, others


def _half(ref, c, axis):
    n = ref.shape[axis] // 2
    if axis == 0:
        return ref.at[pl.ds(c * n, n)]
    return ref.at[:, pl.ds(c * n, n)]


def _all_gather_chips(shards, *, name):
    n = len(shards)

    def body(*refs):
        ins, outs = refs[:n], refs[n : 2 * n]
        send_sems, recv_sems, local_sems = refs[2 * n :]
        x, y, c, others = _place()
        me_chip = 2 * x + y
        sibling = (x, y, 1 - c)

        def remote(src, dst, k, to):
            return pltpu.make_async_remote_copy(
                src_ref=src, dst_ref=dst, send_sem=send_sems.at[k], recv_sem=recv_sems.at[k], device_id=to, device_id_type=MESH
            )

        local = [pltpu.make_async_copy(ins[a], outs[a].at[me_chip], local_sems.at[a]) for a in range(n)]
        for cp in local:
            cp.start()
        started = []
        for j, (px, py) in enumerate(others):
            for a in range(n):
                cp = remote(_half(ins[a], c, 0), _half(outs[a].at[me_chip], c, 0), 3 * a + j, (px, py, c))
                cp.start()
                started.append(cp)
        for j, (px, py) in enumerate(others):
            for a in range(n):
                landed = _half(outs[a].at[2 * px + py], c, 0)
                remote(landed, landed, 3 * a + j, (px, py, c)).wait_recv()
                cp = remote(landed, landed, 3 * n + 3 * a + j, sibling)
                cp.start()
                started.append(cp)
        for j, (px, py) in enumerate(others):
            for a in range(n):
                passed = _half(outs[a].at[2 * px + py], 1 - c, 0)
                remote(passed, passed, 3 * n + 3 * a + j, sibling).wait_recv()
        for cp in started:
            cp.wait_send()
        for cp in local:
            cp.wait()

    return pl.pallas_call(
        body,
        name=name,
        in_specs=[ANY] * n,
        out_specs=[ANY] * n,
        out_shape=[jax.ShapeDtypeStruct((N_CHIPS,) + s.shape, s.dtype) for s in shards],
        scratch_shapes=[pltpu.SemaphoreType.DMA((6 * n,)), pltpu.SemaphoreType.DMA((6 * n,)), pltpu.SemaphoreType.DMA((n,))],
    )(*shards)


def _swap_halves_with_sibling(grads, *, name):
    n = len(grads)

    def body(*refs):
        ins, outs = refs[:n], refs[n : 2 * n]
        send_sems, recv_sems = refs[2 * n :]
        x, y, c, _ = _place()
        copies = [
            pltpu.make_async_remote_copy(
                src_ref=_half(ins[a], 1 - c, 1), dst_ref=outs[a], send_sem=send_sems.at[a], recv_sem=recv_sems.at[a],
                device_id=(x, y, 1 - c), device_id_type=MESH,
            )
            for a in range(n)
        ]
        for cp in copies:
            cp.start()
        for cp in copies:
            cp.wait()

    return pl.pallas_call(
        body,
        name=name,
        in_specs=[ANY] * n,
        out_specs=[ANY] * n,
        out_shape=[jax.ShapeDtypeStruct((N_CHIPS, g.shape[1] // 2, g.shape[2]), g.dtype) for g in grads],
        scratch_shapes=[pltpu.SemaphoreType.DMA((n,)), pltpu.SemaphoreType.DMA((n,))],
    )(*grads)


def _add_sibling_half(core, grads, received, *, name):
    n = len(grads)

    def body(core_ref, *refs):
        for a in range(n):
            refs[2 * n + a][...] = refs[a][...] + refs[n + a][...]

    def half_spec(g):
        return pl.BlockSpec((1, g.shape[1] // 2, g.shape[2]), lambda k, core_ref: (k, core_ref[0], 0))

    def out_spec(g):
        return pl.BlockSpec((1, g.shape[1] // 2, g.shape[2]), lambda k, core_ref: (k, 0, 0))

    return pl.pallas_call(
        body,
        name=name,
        grid_spec=pltpu.PrefetchScalarGridSpec(
            num_scalar_prefetch=1,
            grid=(N_CHIPS,),
            in_specs=[half_spec(g) for g in grads] + [out_spec(g) for g in grads],
            out_specs=[out_spec(g) for g in grads],
        ),
        out_shape=[jax.ShapeDtypeStruct(r.shape, r.dtype) for r in received],
        compiler_params=pltpu.CompilerParams(dimension_semantics=("arbitrary",), vmem_limit_bytes=VMEM_LIMIT),
    )(core, *grads, *received)


def _scatter_to_chips(pairs, *, name):
    n = len(pairs)

    def body(*refs):
        ins, outs = refs[:n], refs[n : 2 * n]
        send_sems, recv_sems, local_sems = refs[2 * n :]
        x, y, c, others = _place()
        me_chip = 2 * x + y
        local = [pltpu.make_async_copy(ins[a].at[me_chip], outs[a].at[me_chip], local_sems.at[a]) for a in range(n)]
        for cp in local:
            cp.start()
        copies = []
        for j, (px, py) in enumerate(others):
            for a in range(n):
                copies.append(
                    pltpu.make_async_remote_copy(
                        src_ref=ins[a].at[2 * px + py], dst_ref=outs[a].at[me_chip], send_sem=send_sems.at[3 * a + j],
                        recv_sem=recv_sems.at[3 * a + j], device_id=(px, py, c), device_id_type=MESH,
                    )
                )
        for cp in copies:
            cp.start()
        for j, (px, py) in enumerate(others):
            for a in range(n):
                landed = outs[a].at[2 * px + py]
                pltpu.make_async_remote_copy(
                    src_ref=landed, dst_ref=landed, send_sem=send_sems.at[3 * a + j], recv_sem=recv_sems.at[3 * a + j],
                    device_id=(px, py, c), device_id_type=MESH,
                ).wait_recv()
        for cp in copies:
            cp.wait_send()
        for cp in local:
            cp.wait()

    return pl.pallas_call(
        body,
        name=name,
        in_specs=[ANY] * n,
        out_specs=[ANY] * n,
        out_shape=[jax.ShapeDtypeStruct(p.shape, p.dtype) for p in pairs],
        scratch_shapes=[pltpu.SemaphoreType.DMA((3 * n,)), pltpu.SemaphoreType.DMA((3 * n,)), pltpu.SemaphoreType.DMA((n,))],
    )(*pairs)


def _sum_chip_blocks(parts, *, name):
    n = len(parts)
    quarters = 4

    def body(*refs):
        for a in range(n):
            p = refs[a]
            refs[n + a][...] = (p[0] + p[1]) + (p[2] + p[3])

    def in_spec(p):
        return pl.BlockSpec((N_CHIPS, p.shape[1] // quarters, p.shape[2]), lambda i: (0, i, 0))

    def out_spec(p):
        return pl.BlockSpec((p.shape[1] // quarters, p.shape[2]), lambda i: (i, 0))

    return pl.pallas_call(
        body,
        name=name,
        grid=(quarters,),
        in_specs=[in_spec(p) for p in parts],
        out_specs=[out_spec(p) for p in parts],
        out_shape=[jax.ShapeDtypeStruct(p.shape[1:], p.dtype) for p in parts],
        compiler_params=pltpu.CompilerParams(dimension_semantics=("arbitrary",), vmem_limit_bytes=VMEM_LIMIT),
    )(*parts)


def _join_halves_with_sibling(halves, *, name):
    n = len(halves)

    def body(*refs):
        ins, outs = refs[:n], refs[n : 2 * n]
        send_sems, recv_sems, local_sems = refs[2 * n :]
        x, y, c, _ = _place()
        local = [pltpu.make_async_copy(ins[a], _half(outs[a], c, 0), local_sems.at[a]) for a in range(n)]
        for cp in local:
            cp.start()
        copies = [
            pltpu.make_async_remote_copy(
                src_ref=ins[a], dst_ref=_half(outs[a], c, 0), send_sem=send_sems.at[a], recv_sem=recv_sems.at[a],
                device_id=(x, y, 1 - c), device_id_type=MESH,
            )
            for a in range(n)
        ]
        for cp in copies:
            cp.start()
        for a in range(n):
            theirs = _half(outs[a], 1 - c, 0)
            pltpu.make_async_remote_copy(
                src_ref=ins[a], dst_ref=theirs, send_sem=send_sems.at[a], recv_sem=recv_sems.at[a],
                device_id=(x, y, 1 - c), device_id_type=MESH,
            ).wait_recv()
        for cp in copies:
            cp.wait_send()
        for cp in local:
            cp.wait()

    return pl.pallas_call(
        body,
        name=name,
        in_specs=[ANY] * n,
        out_specs=[ANY] * n,
        out_shape=[jax.ShapeDtypeStruct((2 * h.shape[0], h.shape[1]), h.dtype) for h in halves],
        scratch_shapes=[pltpu.SemaphoreType.DMA((n,)), pltpu.SemaphoreType.DMA((n,)), pltpu.SemaphoreType.DMA((n,))],
    )(*halves)


def _reduce_scatter(grads, core, *, tag):
    received = _swap_halves_with_sibling(grads, name=f"rs_swap{tag}")
    pairs = _add_sibling_half(core, grads, received, name=f"rs_pair{tag}")
    parts = _scatter_to_chips(pairs, name=f"rs_scatter{tag}")
    halves = _sum_chip_blocks(parts, name=f"rs_sum{tag}")
    return _join_halves_with_sibling(halves, name=f"rs_join{tag}")


def _adamw(w, g, m, v):
    m = ADAM_B1 * m + (1.0 - ADAM_B1) * g
    v = ADAM_B2 * v + (1.0 - ADAM_B2) * (g * g)
    m_hat = m / (1.0 - ADAM_B1**ADAM_STEP)
    v_hat = v / (1.0 - ADAM_B2**ADAM_STEP)
    delta = -ADAM_LR * (m_hat / (jnp.sqrt(v_hat) + ADAM_EPS) + ADAM_WD * w)
    return delta, m, v


def _adamw_large(w, m, v, layer_grads, *, name):
    depth, rows, cols = w.shape
    tile = math.gcd(rows, MM_TILE)
    assert tile % 8 == 0

    def body(w_ref, m_ref, v_ref, *refs):
        g_refs, (g_out, d_out, m_out, v_out) = refs[:depth], refs[depth:]
        for l in range(depth):

            @pl.when(pl.program_id(0) == l)
            def _(l=l):
                g = g_refs[l][...]
                delta, m_new, v_new = _adamw(w_ref[0], g, m_ref[0], v_ref[0])
                g_out[0], d_out[0], m_out[0], v_out[0] = g, delta, m_new, v_new

    def stacked():
        return pl.BlockSpec((1, tile, cols), lambda l, i: (l, i, 0))

    def layer_spec(l):
        return pl.BlockSpec((tile, cols), lambda k, i: (jnp.where(k == l, i, 0), 0))

    shape = jax.ShapeDtypeStruct(w.shape, F32)
    return pl.pallas_call(
        body,
        name=name,
        grid=(depth, rows // tile),
        in_specs=[stacked(), stacked(), stacked()] + [layer_spec(l) for l in range(depth)],
        out_specs=[stacked()] * 4,
        out_shape=[shape] * 4,
        compiler_params=pltpu.CompilerParams(dimension_semantics=("arbitrary", "arbitrary"), vmem_limit_bytes=VMEM_LIMIT),
    )(w, m, v, *layer_grads)


def _adamw_small(ws, gs, ms, vs, *, name):
    n = len(ws)

    def body(*refs):
        for a in range(n):
            delta, m_new, v_new = _adamw(refs[a][...], refs[n + a][...], refs[2 * n + a][...], refs[3 * n + a][...])
            refs[4 * n + a][...] = delta
            refs[5 * n + a][...] = m_new
            refs[6 * n + a][...] = v_new

    shapes = [jax.ShapeDtypeStruct(w.shape, F32) for w in ws]
    outs = pl.pallas_call(body, name=name, out_shape=shapes * 3)(*ws, *gs, *ms, *vs)
    return outs[:n], outs[n : 2 * n], outs[2 * n :]


WEIGHT_NAMES = (
    "w_in", "conv_a_w", "sg_ln_g", "sg_ln_b", "sg_w", "sg_b", "pool_w", "pool_scale", "cc_dw_w", "cc_dw_b", "cc_ln_g",
    "cc_ln_b", "cc_pw_w", "w_kv", "w_out", "ln_g", "ln_b",
)
LARGE = ("w_in", "cc_pw_w", "w_kv", "w_out")
TAPS_ROWS = 48


def _unpack_small_grads(small, chip):
    out = {}
    for r, k in enumerate(("sg_ln_g", "sg_ln_b", "pool_scale", "cc_dw_b", "cc_ln_g", "cc_ln_b")):
        out[k] = small[RG_VEC + r]
    out["conv_a_w"] = lax.dynamic_slice_in_dim(small[RG_CONVA : RG_CONVA + CONV_A], chip * HEAD_DIM, HEAD_DIM, axis=1)
    out["cc_dw_w"] = lax.dynamic_slice_in_dim(small[RG_DW : RG_DW + CONV_D], chip * HEAD_DIM, HEAD_DIM, axis=1)
    cat = jnp.concatenate([small[RG_SGW : RG_SGW + CHUNK], small[RG_SGW + CHUNK : RG_SGW + 2 * CHUNK]], axis=1)
    out["sg_w"] = jnp.transpose(cat.reshape(CHUNK, N_SUB, CHUNK), (1, 0, 2))
    out["sg_b"] = small[RG_SGB : RG_SGB + CHUNK].reshape(CHUNK, N_SUB, HEAD_DIM).sum(-1).T
    pool = small[RG_POOL : RG_POOL + D_G]
    out["pool_w"] = jnp.stack(
        [pool[g * HEAD_DIM : (g + 1) * HEAD_DIM, g * HEAD_DIM : (g + 1) * HEAD_DIM] for g in range(N_SUB)]
    )
    out["ln_g"] = small[RG_LN : RG_LN + 4].reshape(D_MODEL)
    out["ln_b"] = small[RG_LN + 4 : RG_LN + 8].reshape(D_MODEL)
    return out


def kernel(x, mem, w_in, conv_a_w, sg_ln_g, sg_ln_b, sg_w, sg_b, pool_w, pool_scale, cc_dw_w, cc_dw_b, cc_ln_g, cc_ln_b, cc_pw_w, w_kv, w_out, ln_g, ln_b, loss_target, m_w_in, m_conv_a_w, m_sg_ln_g, m_sg_ln_b, m_sg_w, m_sg_b, m_pool_w, m_pool_scale, m_cc_dw_w, m_cc_dw_b, m_cc_ln_g, m_cc_ln_b, m_cc_pw_w, m_w_kv, m_w_out, m_ln_g, m_ln_b, v_w_in, v_conv_a_w, v_sg_ln_g, v_sg_ln_b, v_sg_w, v_sg_b, v_pool_w, v_pool_scale, v_cc_dw_w, v_cc_dw_b, v_cc_ln_g, v_cc_ln_b, v_cc_pw_w, v_w_kv, v_w_out, v_ln_g, v_ln_b):
    given = dict(locals())
    weights = {k: given[k] for k in WEIGHT_NAMES}
    chip = 2 * lax.axis_index("x") + lax.axis_index("y")
    core = lax.axis_index("c").astype(jnp.int32).reshape(1)

    layers = []
    for l in range(DEPTH):
        taps = jnp.concatenate([conv_a_w[l], cc_dw_w[l]], axis=0)
        taps = jnp.pad(taps, ((0, TAPS_ROWS - taps.shape[0]), (0, 0)))
        g_in, g_out, g_kv, g_pw, g_taps = _all_gather_chips(
            [w_in[l].astype(BF16), w_out[l].astype(BF16), w_kv[l].astype(BF16), cc_pw_w[l].astype(BF16), taps],
            name=f"gather_weights{l}",
        )
        taps_full = jnp.transpose(g_taps, (1, 0, 2)).reshape(TAPS_ROWS, D_G)
        layers.append(
            dict(
                w_in=g_in,
                w_out=g_out.reshape(D_MIX, D_MODEL),
                w_kv=g_kv.reshape(D_MODEL, 2 * D_G),
                cc_pw_w=g_pw.reshape(D_G, D_G),
                conv_a_w=taps_full[0:CONV_A],
                cc_dw_w=taps_full[CONV_A : CONV_A + CONV_D],
                **{k: weights[k][l] for k in WEIGHT_NAMES if k not in LARGE + ("conv_a_w", "cc_dw_w")},
            )
        )

    loss_local, grad_x, grads = _local_step(x[0], mem[0], loss_target[0], layers)
    loss = lax.psum(loss_local, ("x", "y", "c"))

    reduced = []
    for l in reversed(range(DEPTH)):
        g = grads[l]
        r_in, r_out, r_kv, r_pw, r_small = _reduce_scatter(
            [
                g["w_in"],
                g["w_out"].reshape(N_CHIPS, D_MIX // N_CHIPS, D_MODEL),
                g["w_kv"].reshape(N_CHIPS, D_MODEL // N_CHIPS, 2 * D_G),
                g["cc_pw_w"].reshape(N_CHIPS, D_G // N_CHIPS, D_G),
                g["small"].reshape(N_CHIPS, RG_ROWS // N_CHIPS, D_G),
            ],
            core,
            tag=str(l),
        )
        (small_all,) = _all_gather_chips([r_small], name=f"gather_small_grads{l}")
        out = _unpack_small_grads(small_all.reshape(RG_ROWS, D_G), chip)
        out.update(w_in=r_in, w_out=r_out, w_kv=r_kv, cc_pw_w=r_pw)
        reduced.append(out)
    reduced = reduced[::-1]

    grad, delta, new_m, new_v = {}, {}, {}, {}
    for k in LARGE:
        w3 = weights[k]
        grad[k], delta[k], new_m[k], new_v[k] = _adamw_large(
            w3, given["m_" + k], given["v_" + k], [reduced[l][k] for l in range(DEPTH)], name=f"adamw_{k}"
        )
    small_names = [k for k in WEIGHT_NAMES if k not in LARGE]
    for k in small_names:
        grad[k] = jnp.stack([reduced[l][k] for l in range(DEPTH)])
    d_s, m_s, v_s = _adamw_small(
        [weights[k] for k in small_names],
        [grad[k] for k in small_names],
        [given["m_" + k] for k in small_names],
        [given["v_" + k] for k in small_names],
        name="adamw_small",
    )
    for a, k in enumerate(small_names):
        delta[k], new_m[k], new_v[k] = d_s[a], m_s[a], v_s[a]

    return (
        loss,
        grad_x[None],
        *[grad[k] for k in WEIGHT_NAMES],
        *[delta[k] for k in WEIGHT_NAMES],
        *[new_m[k] for k in WEIGHT_NAMES],
        *[new_v[k] for k in WEIGHT_NAMES],
    )
```

```python
import functools
import math

import jax
import jax.numpy as jnp
from jax import lax
from jax.experimental import pallas as pl
from jax.experimental.pallas import tpu as pltpu

F32 = jnp.float32
BF16 = jnp.bfloat16

D_MODEL = 1024
DEPTH = 2
D_G = 256
D_MIX = 5 * D_G
D_IN = 9 * D_G + D_MIX
N_SUB = 4
HEAD_DIM = 64
CONV_A = 3
CONV_D = 31
CHUNK = 128
MEM_LEN = 256
N_CHIPS = 4
W_IN_SHARD = D_IN // N_CHIPS
LN_EPS = 1e-5
ALPHA = (2.0 * DEPTH) ** 0.25
ATT_SCALE = 1.0 / math.sqrt(HEAD_DIM)
GELU_C = math.sqrt(2.0 / math.pi)
GELU_A = 0.044715

ADAM_LR = 0.001
ADAM_B1 = 0.9
ADAM_B2 = 0.999
ADAM_EPS = 1e-08
ADAM_WD = 0.01
ADAM_STEP = 10

C_XA, C_BA, C_CA, C_U, C_V, C_XC, C_DA, C_DG, C_Q, C_GATE = (D_G * i for i in range(10))

HALO_A = 8
HALO_C = 16
HALO_D = 32

RW_VEC = 0
RW_CONVA = 16
RW_DW = 24
RW_SGB = 56
RW_ROWS = RW_SGB + CHUNK

RG_VEC = 0
RG_CONVA = 16
RG_DW = 24
RG_SGW = 56
RG_SGB = RG_SGW + 2 * CHUNK
RG_POOL = RG_SGB + CHUNK
RG_LN = RG_POOL + D_G
RG_ROWS = 768

VMEM_LIMIT = 56 * 1024 * 1024
SEQ_TILE = 256
MM_TILE = 512

MESH = pl.DeviceIdType.MESH
ANY = pl.BlockSpec(memory_space=pl.ANY)
NT = (((1,), (1,)), ((), ()))
TN = (((0,), (0,)), ((), ()))


def _dot(a, b):
    return jnp.dot(a, b, preferred_element_type=F32)


def _dot_nt(a, b):
    return lax.dot_general(a, b, NT, preferred_element_type=F32)


def _dot_tn(a, b):
    return lax.dot_general(a, b, TN, preferred_element_type=F32)


def _full(shape):
    zeros = (0,) * len(shape)
    return pl.BlockSpec(shape, lambda *_: zeros)


def _sigmoid(x):
    return 1.0 / (1.0 + jnp.exp(-x))


def _gelu(x):
    t = jnp.tanh(GELU_C * (x + GELU_A * x * x * x))
    return 0.5 * x * (1.0 + t), t


def _gelu_grad(x, t):
    return 0.5 * (1.0 + t) + 0.5 * x * (1.0 - t * t) * (GELU_C * (1.0 + 3.0 * GELU_A * x * x))


def _normalize(v):
    mu = jnp.mean(v, axis=-1, keepdims=True)
    d = v - mu
    var = jnp.mean(d * d, axis=-1, keepdims=True)
    rstd = lax.rsqrt(var + LN_EPS)
    return d * rstd, rstd


def _normalize_grad(dhat, hat, rstd):
    m1 = jnp.mean(dhat, axis=-1, keepdims=True)
    m2 = jnp.mean(dhat * hat, axis=-1, keepdims=True)
    return rstd * (dhat - m1 - hat * m2)


def _lane(width=D_G):
    return lax.broadcasted_iota(jnp.int32, (1, width), 1)


def _head_masks():
    head = _lane() // HEAD_DIM
    return [(head == h).astype(F32) for h in range(N_SUB)]


def _stack_heads(v, masks):
    return jnp.concatenate([v * m for m in masks], axis=0)


def _tril_mask_cat():
    t = lax.broadcasted_iota(jnp.int32, (CHUNK, N_SUB * CHUNK), 0)
    s = lax.broadcasted_iota(jnp.int32, (CHUNK, N_SUB * CHUNK), 1) % CHUNK
    return s <= t


def _triu_mask_cat():
    s = lax.broadcasted_iota(jnp.int32, (CHUNK, N_SUB * CHUNK), 0)
    t = lax.broadcasted_iota(jnp.int32, (CHUNK, N_SUB * CHUNK), 1) % CHUNK
    return t >= s


def _pool_select(a2, a4, a8, a16):
    lane = _lane()
    return jnp.where(lane < 64, a2, jnp.where(lane < 128, a4, jnp.where(lane < 192, a8, a16)))


def _pool_count(row0, rows):
    t = row0 + lax.broadcasted_iota(jnp.int32, (rows, D_G), 0)
    lane = lax.broadcasted_iota(jnp.int32, (rows, D_G), 1)
    win = jnp.where(lane < 64, 2, jnp.where(lane < 128, 4, jnp.where(lane < 192, 8, 16)))
    return jnp.minimum(t + 1, win).astype(F32)


def _trailing_window_sum(halo, cur):
    e = jnp.concatenate([halo, cur], axis=0)
    s2 = e + pltpu.roll(e, 1, 0)
    s4 = s2 + pltpu.roll(s2, 2, 0)
    s8 = s4 + pltpu.roll(s4, 4, 0)
    s16 = s8 + pltpu.roll(s8, 8, 0)
    return _pool_select(s2, s4, s8, s16)[HALO_C:]


def _leading_window_sum(cur, halo):
    e = jnp.concatenate([cur, halo], axis=0)
    n = e.shape[0]
    s2 = e + pltpu.roll(e, n - 1, 0)
    s4 = s2 + pltpu.roll(s2, n - 2, 0)
    s8 = s4 + pltpu.roll(s4, n - 4, 0)
    s16 = s8 + pltpu.roll(s8, n - 8, 0)
    return _pool_select(s2, s4, s8, s16)[: cur.shape[0]]


def _softmax_blocks(sc):
    out = []
    for h in range(N_SUB):
        s = sc[:, h * MEM_LEN : (h + 1) * MEM_LEN]
        e = jnp.exp(s - jnp.max(s, axis=-1, keepdims=True))
        out.append(e / jnp.sum(e, axis=-1, keepdims=True))
    return jnp.concatenate(out, axis=-1)


def _sgu_mix(vn, wcat_b, sgb, masks):
    vbd = _stack_heads(vn, masks).astype(BF16)
    return _dot(wcat_b, vbd) + sgb, vbd


def _layer_forward(x, win, kst, vst, wout, sw, wcat, poolw, pww, ln, tgt, *, name):
    seq = x.shape[0]
    tile = min(SEQ_TILE, seq)
    n_tiles = seq // tile
    last = tgt is not None

    def body(*refs):
        x_ref, win_ref, kst_ref, vst_ref, wout_ref, sw_ref, wcat_ref, pool_ref, pw_ref, ln_ref = refs[:10]
        refs = refs[10:]
        if last:
            tgt_ref, refs = refs[0], refs[1:]
        proj_ref, y_ref, z_ref, out_ref = refs[:4]
        refs = refs[4:]
        if last:
            loss_ref, refs = refs[0], refs[1:]
        pbuf, xchalo, gbuf = refs
        i = pl.program_id(0)

        @pl.when(i == 0)
        def _():
            pbuf[0:HALO_A, :] = jnp.zeros((HALO_A, D_G), F32)
            xchalo[...] = jnp.zeros((HALO_C, D_G), F32)
            gbuf[0:HALO_D, :] = jnp.zeros((HALO_D, D_G), F32)
            if last:
                loss_ref[...] = jnp.zeros((8, 128), F32)

        xt = x_ref[...]
        xb = xt.astype(BF16)
        for k in range(N_CHIPS):
            proj_ref[:, k * W_IN_SHARD : (k + 1) * W_IN_SHARD] = _dot(xb, win_ref[k])
        masks = _head_masks()

        pbuf[HALO_A : HALO_A + tile, :] = proj_ref[:, C_CA : C_CA + D_G] * proj_ref[:, C_XA : C_XA + D_G]
        cv = jnp.zeros((tile, D_G), F32)
        for k in range(CONV_A):
            off = HALO_A - (CONV_A - 1) + k
            cv = cv + sw_ref[RW_CONVA + k : RW_CONVA + k + 1, :] * pbuf[off : off + tile, :]
        y_ref[:, 0:D_G] = proj_ref[:, C_BA : C_BA + D_G] * cv
        pbuf[0:HALO_A, :] = pbuf[tile : tile + HALO_A, :]

        ua, _ = _gelu(proj_ref[:, C_U : C_U + D_G])
        vg, _ = _gelu(proj_ref[:, C_V : C_V + D_G])
        vhat, _ = _normalize(vg)
        vn = vhat * sw_ref[RW_VEC : RW_VEC + 1, :] + sw_ref[RW_VEC + 1 : RW_VEC + 2, :]
        wcat_b = jnp.where(_tril_mask_cat(), wcat_ref[...], 0.0).astype(BF16)
        sgb = sw_ref[RW_SGB : RW_SGB + CHUNK, :]
        for j in range(tile // CHUNK):
            rows = slice(j * CHUNK, (j + 1) * CHUNK)
            mixed, _ = _sgu_mix(vn[rows], wcat_b, sgb, masks)
            y_ref[rows, D_G : 2 * D_G] = ua[rows] * mixed

        xc = proj_ref[:, C_XC : C_XC + D_G]
        wsum = _trailing_window_sum(xchalo[...], xc)
        pm = wsum / _pool_count(i * tile, tile) - xc
        y_ref[:, 2 * D_G : 3 * D_G] = _dot(pm.astype(BF16), pool_ref[...]) * sw_ref[RW_VEC + 2 : RW_VEC + 3, :]
        xchalo[...] = xc[tile - HALO_C :, :]

        gbuf[HALO_D : HALO_D + tile, :] = proj_ref[:, C_DA : C_DA + D_G] * _sigmoid(proj_ref[:, C_DG : C_DG + D_G])
        cvd = jnp.zeros((tile, D_G), F32) + sw_ref[RW_VEC + 3 : RW_VEC + 4, :]
        for k in range(CONV_D):
            off = HALO_D - (CONV_D - 1) + k
            cvd = cvd + sw_ref[RW_DW + k : RW_DW + k + 1, :] * gbuf[off : off + tile, :]
        nhat, _ = _normalize(cvd)
        nrm = nhat * sw_ref[RW_VEC + 4 : RW_VEC + 5, :] + sw_ref[RW_VEC + 5 : RW_VEC + 6, :]
        y_ref[:, 3 * D_G : 4 * D_G] = _dot((nrm * _sigmoid(nrm)).astype(BF16), pw_ref[...])
        gbuf[0:HALO_D, :] = gbuf[tile : tile + HALO_D, :]

        qb = proj_ref[:, C_Q : C_Q + D_G].astype(BF16)
        p_all = _softmax_blocks(_dot_nt(qb, kst_ref[...]) * ATT_SCALE)
        y_ref[:, 4 * D_G : 5 * D_G] = _dot(p_all.astype(BF16), vst_ref[...])

        gate = proj_ref[:, C_GATE:]
        hid = y_ref[...] * (gate * _sigmoid(gate))
        z = ALPHA * xt + _dot(hid.astype(BF16), wout_ref[...])
        z_ref[...] = z
        zhat, _ = _normalize(z)
        xn = zhat * ln_ref[0:1, :] + ln_ref[1:2, :]
        if last:
            err = xn - tgt_ref[...]
            out_ref[...] = err * (1.0 / D_MODEL)
            loss_ref[...] += jnp.sum(err * err) * (0.5 / D_MODEL)
        else:
            out_ref[...] = xn

    def rows(width):
        return pl.BlockSpec((tile, width), lambda i: (i, 0))

    operands = [x, win, kst, vst, wout, sw, wcat, poolw, pww, ln]
    in_specs = [rows(D_MODEL)] + [_full(a.shape) for a in operands[1:]]
    out_shape = [
        jax.ShapeDtypeStruct((seq, D_IN), F32),
        jax.ShapeDtypeStruct((seq, D_MIX), F32),
        jax.ShapeDtypeStruct((seq, D_MODEL), F32),
        jax.ShapeDtypeStruct((seq, D_MODEL), F32),
    ]
    out_specs = [rows(D_IN), rows(D_MIX), rows(D_MODEL), rows(D_MODEL)]
    if last:
        operands.append(tgt)
        in_specs.append(rows(D_MODEL))
        out_shape.append(jax.ShapeDtypeStruct((8, 128), F32))
        out_specs.append(_full((8, 128)))
    return pl.pallas_call(
        body,
        name=name,
        grid=(n_tiles,),
        in_specs=in_specs,
        out_specs=out_specs,
        out_shape=out_shape,
        scratch_shapes=[
            pltpu.VMEM((HALO_A + tile, D_G), F32),
            pltpu.VMEM((HALO_C, D_G), F32),
            pltpu.VMEM((HALO_D + tile, D_G), F32),
        ],
        compiler_params=pltpu.CompilerParams(dimension_semantics=("arbitrary",), vmem_limit_bytes=VMEM_LIMIT),
    )(*operands)


def _layer_backward(dxn, z, proj, y, kst, vst, wout, sw, wcat, wcat_t, poolw, pww, ln, *, name):
    seq = dxn.shape[0]
    tile = min(SEQ_TILE, seq)
    n_tiles = seq // tile
    halo_blocks = tile // HALO_D

    def body(
        dxn_ref, z_ref, proj_ref, halo_ref, y_ref, kst_ref, vst_ref, wout_ref, sw_ref, wcat_ref, wcat_t_ref, pool_ref,
        pw_ref, ln_ref, dproj_ref, dz_ref, dwout_ref, dkst_ref, dvst_ref, dpw_ref, sg_ref,
        pbuf, dcvbuf, rhalo, gbuf, dgbuf,
    ):
        i = pl.program_id(0)
        ti = n_tiles - 1 - i

        @pl.when(i == 0)
        def _():
            dwout_ref[...] = jnp.zeros(dwout_ref.shape, F32)
            dkst_ref[...] = jnp.zeros(dkst_ref.shape, F32)
            dvst_ref[...] = jnp.zeros(dvst_ref.shape, F32)
            dpw_ref[...] = jnp.zeros(dpw_ref.shape, F32)
            sg_ref[...] = jnp.zeros(sg_ref.shape, F32)
            dcvbuf[tile : tile + HALO_A, :] = jnp.zeros((HALO_A, D_G), F32)
            rhalo[...] = jnp.zeros((HALO_C, D_G), F32)
            dgbuf[tile : tile + HALO_D, :] = jnp.zeros((HALO_D, D_G), F32)

        def acc_row(row, val):
            sg_ref[row : row + 1, :] += jnp.sum(val, axis=0, keepdims=True)

        masks = _head_masks()
        has_past = (ti > 0).astype(F32)

        zhat, zrstd = _normalize(z_ref[...])
        dxn_t = dxn_ref[...]
        dlg = jnp.sum(dxn_t * zhat, axis=0, keepdims=True)
        dlb = jnp.sum(dxn_t, axis=0, keepdims=True)
        for j in range(D_MODEL // D_G):
            sg_ref[RG_LN + j : RG_LN + j + 1, :] += dlg[:, j * D_G : (j + 1) * D_G]
            sg_ref[RG_LN + 4 + j : RG_LN + 5 + j, :] += dlb[:, j * D_G : (j + 1) * D_G]
        dz = _normalize_grad(dxn_t * ln_ref[0:1, :], zhat, zrstd)
        dz_ref[...] = dz
        dzb = dz.astype(BF16)

        gate = proj_ref[:, C_GATE:]
        sgm = _sigmoid(gate)
        silu = gate * sgm
        yc = y_ref[...]
        dwout_ref[...] += _dot_tn((yc * silu).astype(BF16), dzb)
        dh = _dot_nt(dzb, wout_ref[...])
        dproj_ref[:, C_GATE:] = (dh * yc * (sgm * (1.0 + gate * (1.0 - sgm)))).astype(BF16)
        dy = dh * silu

        dya = dy[:, 0:D_G]
        xa = proj_ref[:, C_XA : C_XA + D_G]
        ba = proj_ref[:, C_BA : C_BA + D_G]
        ca = proj_ref[:, C_CA : C_CA + D_G]
        past = slice(HALO_D - HALO_A, HALO_D)
        pbuf[0:HALO_A, :] = halo_ref[past, C_CA : C_CA + D_G] * halo_ref[past, C_XA : C_XA + D_G] * has_past
        pbuf[HALO_A : HALO_A + tile, :] = ca * xa
        cv = jnp.zeros((tile, D_G), F32)
        for k in range(CONV_A):
            off = HALO_A - (CONV_A - 1) + k
            cv = cv + sw_ref[RW_CONVA + k : RW_CONVA + k + 1, :] * pbuf[off : off + tile, :]
        dproj_ref[:, C_BA : C_BA + D_G] = (dya * cv).astype(BF16)
        dcv = dya * ba
        dcvbuf[0:tile, :] = dcv
        dp = jnp.zeros((tile, D_G), F32)
        for k in range(CONV_A):
            off = HALO_A - (CONV_A - 1) + k
            acc_row(RG_CONVA + k, dcv * pbuf[off : off + tile, :])
            back = CONV_A - 1 - k
            dp = dp + sw_ref[RW_CONVA + k : RW_CONVA + k + 1, :] * dcvbuf[back : back + tile, :]
        dproj_ref[:, C_CA : C_CA + D_G] = (dp * xa).astype(BF16)
        dproj_ref[:, C_XA : C_XA + D_G] = (dp * ca).astype(BF16)
        dcvbuf[tile : tile + HALO_A, :] = dcvbuf[0:HALO_A, :]

        dyb = dy[:, D_G : 2 * D_G]
        u = proj_ref[:, C_U : C_U + D_G]
        v = proj_ref[:, C_V : C_V + D_G]
        ua, ut = _gelu(u)
        vg, vt = _gelu(v)
        vhat, vrstd = _normalize(vg)
        sg_g = sw_ref[RW_VEC : RW_VEC + 1, :]
        vn = vhat * sg_g + sw_ref[RW_VEC + 1 : RW_VEC + 2, :]
        tril = _tril_mask_cat()
        wcat_b = jnp.where(tril, wcat_ref[...], 0.0).astype(BF16)
        wcat_tb = jnp.where(_triu_mask_cat(), wcat_t_ref[...], 0.0).astype(BF16)
        sgb = sw_ref[RW_SGB : RW_SGB + CHUNK, :]
        dmixed = dyb * ua
        dvn_parts = []
        du_parts = []
        dwcat = jnp.zeros((CHUNK, N_SUB * CHUNK), F32)
        dsgb = jnp.zeros((CHUNK, D_G), F32)
        for j in range(tile // CHUNK):
            rows = slice(j * CHUNK, (j + 1) * CHUNK)
            mixed, vbd = _sgu_mix(vn[rows], wcat_b, sgb, masks)
            du_parts.append(dyb[rows] * mixed)
            dmx = dmixed[rows]
            dsgb = dsgb + dmx
            dwcat = dwcat + _dot_nt(dmx.astype(BF16), vbd)
            dvn_parts.append(_dot(wcat_tb, _stack_heads(dmx, masks).astype(BF16)))
        dwcat = jnp.where(tril, dwcat, 0.0)
        sg_ref[RG_SGW : RG_SGW + CHUNK, :] += dwcat[:, 0:D_G]
        sg_ref[RG_SGW + CHUNK : RG_SGW + 2 * CHUNK, :] += dwcat[:, D_G:]
        sg_ref[RG_SGB : RG_SGB + CHUNK, :] += dsgb
        dvn = jnp.concatenate(dvn_parts, axis=0)
        du_act = jnp.concatenate(du_parts, axis=0)
        acc_row(RG_VEC, dvn * vhat)
        acc_row(RG_VEC + 1, dvn)
        dvg = _normalize_grad(dvn * sg_g, vhat, vrstd)
        dproj_ref[:, C_U : C_U + D_G] = (du_act * _gelu_grad(u, ut)).astype(BF16)
        dproj_ref[:, C_V : C_V + D_G] = (dvg * _gelu_grad(v, vt)).astype(BF16)

        dyc = dy[:, 2 * D_G : 3 * D_G]
        xc = proj_ref[:, C_XC : C_XC + D_G]
        xc_past = halo_ref[HALO_D - HALO_C : HALO_D, C_XC : C_XC + D_G] * has_past
        cnt = _pool_count(ti * tile, tile)
        pm = _trailing_window_sum(xc_past, xc) / cnt - xc
        pmb = pm.astype(BF16)
        pool_b = pool_ref[...]
        scale = sw_ref[RW_VEC + 2 : RW_VEC + 3, :]
        acc_row(RG_VEC + 2, dyc * _dot(pmb, pool_b))
        dpre = (dyc * scale).astype(BF16)
        sg_ref[RG_POOL : RG_POOL + D_G, :] += _dot_tn(pmb, dpre)
        dpm = _dot_nt(dpre, pool_b)
        r = dpm / cnt
        dproj_ref[:, C_XC : C_XC + D_G] = (_leading_window_sum(r, rhalo[...]) - dpm).astype(BF16)
        rhalo[...] = r[0:HALO_C, :]

        dyd = dy[:, 3 * D_G : 4 * D_G]
        da = proj_ref[:, C_DA : C_DA + D_G]
        sgd = _sigmoid(proj_ref[:, C_DG : C_DG + D_G])
        gbuf[0:HALO_D, :] = halo_ref[:, C_DA : C_DA + D_G] * _sigmoid(halo_ref[:, C_DG : C_DG + D_G]) * has_past
        gbuf[HALO_D : HALO_D + tile, :] = da * sgd
        cvd = jnp.zeros((tile, D_G), F32) + sw_ref[RW_VEC + 3 : RW_VEC + 4, :]
        for k in range(CONV_D):
            off = HALO_D - (CONV_D - 1) + k
            cvd = cvd + sw_ref[RW_DW + k : RW_DW + k + 1, :] * gbuf[off : off + tile, :]
        nhat, nrstd = _normalize(cvd)
        cc_g = sw_ref[RW_VEC + 4 : RW_VEC + 5, :]
        nrm = nhat * cc_g + sw_ref[RW_VEC + 5 : RW_VEC + 6, :]
        sgn = _sigmoid(nrm)
        dydb = dyd.astype(BF16)
        dpw_ref[...] += _dot_tn((nrm * sgn).astype(BF16), dydb)
        dn = _dot_nt(dydb, pw_ref[...]) * (sgn * (1.0 + nrm * (1.0 - sgn)))
        acc_row(RG_VEC + 4, dn * nhat)
        acc_row(RG_VEC + 5, dn)
        dcvd = _normalize_grad(dn * cc_g, nhat, nrstd)
        acc_row(RG_VEC + 3, dcvd)
        dgbuf[0:tile, :] = dcvd
        dg = jnp.zeros((tile, D_G), F32)
        for k in range(CONV_D):
            off = HALO_D - (CONV_D - 1) + k
            acc_row(RG_DW + k, dcvd * gbuf[off : off + tile, :])
            back = CONV_D - 1 - k
            dg = dg + sw_ref[RW_DW + k : RW_DW + k + 1, :] * dgbuf[back : back + tile, :]
        dproj_ref[:, C_DA : C_DA + D_G] = (dg * sgd).astype(BF16)
        dproj_ref[:, C_DG : C_DG + D_G] = (dg * da * sgd * (1.0 - sgd)).astype(BF16)
        dgbuf[tile : tile + HALO_D, :] = dgbuf[0:HALO_D, :]

        dyeb = dy[:, 4 * D_G : 5 * D_G].astype(BF16)
        qb = proj_ref[:, C_Q : C_Q + D_G].astype(BF16)
        kst_b = kst_ref[...]
        p_all = _softmax_blocks(_dot_nt(qb, kst_b) * ATT_SCALE)
        dvst_ref[...] += _dot_tn(p_all.astype(BF16), dyeb)
        dp_all = _dot_nt(dyeb, vst_ref[...])
        ds = []
        for h in range(N_SUB):
            blk = slice(h * MEM_LEN, (h + 1) * MEM_LEN)
            p, dpb = p_all[:, blk], dp_all[:, blk]
            ds.append(p * (dpb - jnp.sum(dpb * p, axis=-1, keepdims=True)))
        dsb = (jnp.concatenate(ds, axis=-1) * ATT_SCALE).astype(BF16)
        dproj_ref[:, C_Q : C_Q + D_G] = _dot(dsb, kst_b).astype(BF16)
        dkst_ref[...] += _dot_tn(dsb, qb)

    def rows(width):
        return pl.BlockSpec((tile, width), lambda i: (n_tiles - 1 - i, 0))

    halo_spec = pl.BlockSpec((HALO_D, D_IN), lambda i: (jnp.maximum((n_tiles - 1 - i) * halo_blocks - 1, 0), 0))
    weights = [kst, vst, wout, sw, wcat, wcat_t, poolw, pww, ln]
    acc_shapes = [(D_MIX, D_MODEL), (N_SUB * MEM_LEN, D_G), (N_SUB * MEM_LEN, D_G), (D_G, D_G), (RG_ROWS, D_G)]
    return pl.pallas_call(
        body,
        name=name,
        grid=(n_tiles,),
        in_specs=[rows(D_MODEL), rows(D_MODEL), rows(D_IN), halo_spec, rows(D_MIX)] + [_full(a.shape) for a in weights],
        out_specs=[rows(D_IN), rows(D_MODEL)] + [_full(s) for s in acc_shapes],
        out_shape=[jax.ShapeDtypeStruct((seq, D_IN), BF16), jax.ShapeDtypeStruct((seq, D_MODEL), F32)]
        + [jax.ShapeDtypeStruct(s, F32) for s in acc_shapes],
        scratch_shapes=[
            pltpu.VMEM((HALO_A + tile, D_G), F32),
            pltpu.VMEM((tile + HALO_A, D_G), F32),
            pltpu.VMEM((HALO_C, D_G), F32),
            pltpu.VMEM((HALO_D + tile, D_G), F32),
            pltpu.VMEM((tile + HALO_D, D_G), F32),
        ],
        compiler_params=pltpu.CompilerParams(dimension_semantics=("arbitrary",), vmem_limit_bytes=VMEM_LIMIT),
    )(dxn, z, proj, proj, y, *weights)


def _kv_forward(mem, wkv, *, name):
    def body(mem_ref, wkv_ref, kst_ref, vst_ref):
        kv = _dot(mem_ref[...].astype(BF16), wkv_ref[...])
        masks = _head_masks()
        kst_ref[...] = _stack_heads(kv[:, 0:D_G], masks).astype(BF16)
        vst_ref[...] = _stack_heads(kv[:, D_G:], masks).astype(BF16)

    shape = jax.ShapeDtypeStruct((N_SUB * MEM_LEN, D_G), BF16)
    return pl.pallas_call(body, name=name, out_shape=[shape, shape])(mem, wkv)


def _kv_backward(mem, dkst, dvst, *, name):
    def body(mem_ref, dkst_ref, dvst_ref, dwkv_ref):
        masks = _head_masks()
        memb = mem_ref[...].astype(BF16)
        for col, ref in ((0, dkst_ref), (D_G, dvst_ref)):
            d = jnp.zeros((MEM_LEN, D_G), F32)
            for h in range(N_SUB):
                d = d + ref[h * MEM_LEN : (h + 1) * MEM_LEN, :] * masks[h]
            dwkv_ref[:, col : col + D_G] = _dot_tn(memb, d.astype(BF16))

    return pl.pallas_call(body, name=name, out_shape=jax.ShapeDtypeStruct((D_MODEL, 2 * D_G), F32))(mem, dkst, dvst)


def _input_grad(dproj, dz, win, *, name):
    seq = dproj.shape[0]
    tile = min(MM_TILE, seq)

    def body(dproj_ref, dz_ref, win_ref, dx_ref):
        acc = ALPHA * dz_ref[...]
        for k in range(N_CHIPS):
            acc = acc + _dot_nt(dproj_ref[:, k * W_IN_SHARD : (k + 1) * W_IN_SHARD], win_ref[k])
        dx_ref[...] = acc

    return pl.pallas_call(
        body,
        name=name,
        grid=(seq // tile,),
        in_specs=[
            pl.BlockSpec((tile, D_IN), lambda i: (i, 0)),
            pl.BlockSpec((tile, D_MODEL), lambda i: (i, 0)),
            _full(win.shape),
        ],
        out_specs=pl.BlockSpec((tile, D_MODEL), lambda i: (i, 0)),
        out_shape=jax.ShapeDtypeStruct((seq, D_MODEL), F32),
        compiler_params=pltpu.CompilerParams(dimension_semantics=("arbitrary",), vmem_limit_bytes=VMEM_LIMIT),
    )(dproj, dz, win)


def _input_weight_grad(x, dproj, *, name):
    seq = x.shape[0]
    tile = min(MM_TILE, seq)

    def body(x_ref, dproj_ref, dwin_ref):
        @pl.when(pl.program_id(1) == 0)
        def _():
            dwin_ref[...] = jnp.zeros(dwin_ref.shape, F32)

        dwin_ref[0] += _dot_tn(x_ref[...].astype(BF16), dproj_ref[...])

    return pl.pallas_call(
        body,
        name=name,
        grid=(N_CHIPS, seq // tile),
        in_specs=[
            pl.BlockSpec((tile, D_MODEL), lambda k, i: (i, 0)),
            pl.BlockSpec((tile, W_IN_SHARD), lambda k, i: (i, k)),
        ],
        out_specs=pl.BlockSpec((1, D_MODEL, W_IN_SHARD), lambda k, i: (k, 0, 0)),
        out_shape=jax.ShapeDtypeStruct((N_CHIPS, D_MODEL, W_IN_SHARD), F32),
        compiler_params=pltpu.CompilerParams(dimension_semantics=("arbitrary", "arbitrary"), vmem_limit_bytes=VMEM_LIMIT),
    )(x, dproj)


def _expand_sgb(sg_b):
    return jnp.repeat(sg_b.T, HEAD_DIM, axis=1)


def _pack_small_weights(sg_ln_g, sg_ln_b, pool_scale, cc_dw_b, cc_ln_g, cc_ln_b, conv_a_w, cc_dw_w, sg_b):
    vec = jnp.stack([sg_ln_g, sg_ln_b, pool_scale, cc_dw_b, cc_ln_g, cc_ln_b])
    return jnp.concatenate(
        [
            jnp.pad(vec, ((0, RW_CONVA - RW_VEC - 6), (0, 0))),
            jnp.pad(conv_a_w, ((0, RW_DW - RW_CONVA - CONV_A), (0, 0))),
            jnp.pad(cc_dw_w, ((0, RW_SGB - RW_DW - CONV_D), (0, 0))),
            _expand_sgb(sg_b),
        ]
    )


def _sg_w_cat(sg_w):
    cat = jnp.transpose(sg_w, (1, 0, 2)).reshape(CHUNK, N_SUB * CHUNK)
    cat_t = jnp.transpose(sg_w, (2, 0, 1)).reshape(CHUNK, N_SUB * CHUNK)
    return cat, cat_t


def _pool_block_diag(pool_w):
    out = jnp.zeros((D_G, D_G), pool_w.dtype)
    for g in range(N_SUB):
        out = out.at[g * HEAD_DIM : (g + 1) * HEAD_DIM, g * HEAD_DIM : (g + 1) * HEAD_DIM].set(pool_w[g])
    return out


def _local_step(x, mem, tgt, layers):
    prepared = []
    for l, w in enumerate(layers):
        cat, cat_t = _sg_w_cat(w["sg_w"])
        kst, vst = _kv_forward(mem, w["w_kv"], name=f"kv_fwd{l}")
        prepared.append(
            dict(
                win=w["w_in"],
                wout=w["w_out"],
                pww=w["cc_pw_w"],
                sw=_pack_small_weights(
                    w["sg_ln_g"], w["sg_ln_b"], w["pool_scale"], w["cc_dw_b"], w["cc_ln_g"], w["cc_ln_b"],
                    w["conv_a_w"], w["cc_dw_w"], w["sg_b"],
                ),
                wcat=cat,
                wcat_t=cat_t,
                poolw=_pool_block_diag(w["pool_w"]).astype(BF16),
                ln=jnp.stack([w["ln_g"], w["ln_b"]]),
                kst=kst,
                vst=vst,
            )
        )

    saved = []
    h = x
    loss = None
    for l, p in enumerate(prepared):
        last = l == len(prepared) - 1
        outs = _layer_forward(
            h, p["win"], p["kst"], p["vst"], p["wout"], p["sw"], p["wcat"], p["poolw"], p["pww"], p["ln"],
            tgt if last else None, name=f"layer_fwd{l}",
        )
        saved.append(dict(x=h, proj=outs[0], y=outs[1], z=outs[2]))
        h = outs[3]
        if last:
            loss = outs[4][0, 0]

    grads = [None] * len(prepared)
    dxn = h
    for l in reversed(range(len(prepared))):
        p, s = prepared[l], saved[l]
        dproj, dz, dwout, dkst, dvst, dpw, small = _layer_backward(
            dxn, s["z"], s["proj"], s["y"], p["kst"], p["vst"], p["wout"], p["sw"], p["wcat"], p["wcat_t"], p["poolw"],
            p["pww"], p["ln"], name=f"layer_bwd{l}",
        )
        grads[l] = dict(
            w_in=_input_weight_grad(s["x"], dproj, name=f"w_in_grad{l}"),
            w_out=dwout,
            w_kv=_kv_backward(mem, dkst, dvst, name=f"kv_bwd{l}"),
            cc_pw_w=dpw,
            small=small,
        )
        dxn = _input_grad(dproj, dz, p["win"], name=f"input_grad{l}")
    return loss, dxn, grads


def _place():
    x, y, c = lax.axis_index("x"), lax.axis_index("y"), lax.axis_index("c")
    others = [(1 - x, y), (x, 1 - y), (1 - x, 1 - y)]
    return x, y, c, others


def _half(ref, c, axis):
    n = ref.shape[axis] // 2
    if axis == 0:
        return ref.at[pl.ds(c * n, n)]
    return ref.at[:, pl.ds(c * n, n)]


def _all_gather_chips(place, shards, dtypes, *, name):
    return _all_gather_placed(_place_own_block(place, shards, dtypes, name=name + "_place"), name=name)


def _place_own_block(place, shards, dtypes, *, name):
    n = len(shards)

    def body(place_ref, *refs):
        for a in range(n):
            refs[n + a][0] = refs[a][...].astype(dtypes[a])

    return pl.pallas_call(
        body,
        name=name,
        grid_spec=pltpu.PrefetchScalarGridSpec(
            num_scalar_prefetch=1,
            grid=(1,),
            in_specs=[pl.BlockSpec(s.shape, lambda i, place_ref: (0, 0)) for s in shards],
            out_specs=[pl.BlockSpec((1,) + s.shape, lambda i, place_ref: (place_ref[1], 0, 0)) for s in shards],
        ),
        out_shape=[jax.ShapeDtypeStruct((N_CHIPS,) + s.shape, dt) for s, dt in zip(shards, dtypes)],
        compiler_params=pltpu.CompilerParams(dimension_semantics=("arbitrary",), vmem_limit_bytes=VMEM_LIMIT),
    )(place, *shards)


def _all_gather_placed(bufs, *, name):
    n = len(bufs)

    def body(*refs):
        outs = refs[n : 2 * n]
        send_sems, recv_sems = refs[2 * n :]
        x, y, c, others = _place()
        me_chip = 2 * x + y
        sibling = (x, y, 1 - c)

        def remote(block, k, to):
            return pltpu.make_async_remote_copy(
                src_ref=block, dst_ref=block, send_sem=send_sems.at[k], recv_sem=recv_sems.at[k], device_id=to, device_id_type=MESH
            )

        started = []
        for j, (px, py) in enumerate(others):
            for a in range(n):
                cp = remote(_half(outs[a].at[me_chip], c, 0), 3 * a + j, (px, py, c))
                cp.start()
                started.append(cp)
        for j, (px, py) in enumerate(others):
            for a in range(n):
                landed = _half(outs[a].at[2 * px + py], c, 0)
                remote(landed, 3 * a + j, (px, py, c)).wait_recv()
                cp = remote(landed, 3 * n + 3 * a + j, sibling)
                cp.start()
                started.append(cp)
        for j, (px, py) in enumerate(others):
            for a in range(n):
                remote(_half(outs[a].at[2 * px + py], 1 - c, 0), 3 * n + 3 * a + j, sibling).wait_recv()
        for cp in started:
            cp.wait_send()

    return pl.pallas_call(
        body,
        name=name,
        in_specs=[ANY] * n,
        out_specs=[ANY] * n,
        out_shape=[jax.ShapeDtypeStruct(b.shape, b.dtype) for b in bufs],
        input_output_aliases={a: a for a in range(n)},
        scratch_shapes=[pltpu.SemaphoreType.DMA((6 * n,)), pltpu.SemaphoreType.DMA((6 * n,))],
    )(*bufs)


def _swap_halves_with_sibling(grads, *, name):
    n = len(grads)

    def body(*refs):
        ins, outs = refs[:n], refs[n : 2 * n]
        send_sems, recv_sems = refs[2 * n :]
        x, y, c, _ = _place()
        copies = [
            pltpu.make_async_remote_copy(
                src_ref=_half(ins[a], 1 - c, 1), dst_ref=outs[a], send_sem=send_sems.at[a], recv_sem=recv_sems.at[a],
                device_id=(x, y, 1 - c), device_id_type=MESH,
            )
            for a in range(n)
        ]
        for cp in copies:
            cp.start()
        for cp in copies:
            cp.wait()

    return pl.pallas_call(
        body,
        name=name,
        in_specs=[ANY] * n,
        out_specs=[ANY] * n,
        out_shape=[jax.ShapeDtypeStruct((N_CHIPS, g.shape[1] // 2, g.shape[2]), g.dtype) for g in grads],
        scratch_shapes=[pltpu.SemaphoreType.DMA((n,)), pltpu.SemaphoreType.DMA((n,))],
    )(*grads)


def _add_sibling_half(place, grads, received, *, name):
    n = len(grads)

    def body(place_ref, *refs):
        k = pl.program_id(0)
        for a in range(n):
            pair = (refs[a][...] + refs[n + a][...]).astype(BF16)
            refs[2 * n + a][...] = pair

            @pl.when(k == place_ref[1])
            def _(a=a, pair=pair):
                refs[3 * n + a][...] = pair

    def block(g):
        return (1, g.shape[1] // 2, g.shape[2])

    return pl.pallas_call(
        body,
        name=name,
        grid_spec=pltpu.PrefetchScalarGridSpec(
            num_scalar_prefetch=1,
            grid=(N_CHIPS,),
            in_specs=[pl.BlockSpec(block(g), lambda k, place_ref: (k, place_ref[0], 0)) for g in grads]
            + [pl.BlockSpec(block(g), lambda k, place_ref: (k, 0, 0)) for g in grads],
            out_specs=[pl.BlockSpec(block(g), lambda k, place_ref: (k, 0, 0)) for g in grads]
            + [pl.BlockSpec(block(g), lambda k, place_ref: (place_ref[1], 0, 0)) for g in grads],
        ),
        out_shape=[jax.ShapeDtypeStruct(r.shape, BF16) for r in received] * 2,
        compiler_params=pltpu.CompilerParams(dimension_semantics=("arbitrary",), vmem_limit_bytes=VMEM_LIMIT),
    )(place, *grads, *received)


def _scatter_to_chips(pairs, landing, *, name):
    n = len(pairs)

    def body(*refs):
        ins, outs = refs[:n], refs[2 * n : 3 * n]
        send_sems, recv_sems = refs[3 * n :]
        x, y, c, others = _place()
        me_chip = 2 * x + y
        copies = []
        for j, (px, py) in enumerate(others):
            for a in range(n):
                copies.append(
                    pltpu.make_async_remote_copy(
                        src_ref=ins[a].at[2 * px + py], dst_ref=outs[a].at[me_chip], send_sem=send_sems.at[3 * a + j],
                        recv_sem=recv_sems.at[3 * a + j], device_id=(px, py, c), device_id_type=MESH,
                    )
                )
        for cp in copies:
            cp.start()
        for j, (px, py) in enumerate(others):
            for a in range(n):
                landed = outs[a].at[2 * px + py]
                pltpu.make_async_remote_copy(
                    src_ref=landed, dst_ref=landed, send_sem=send_sems.at[3 * a + j], recv_sem=recv_sems.at[3 * a + j],
                    device_id=(px, py, c), device_id_type=MESH,
                ).wait_recv()
        for cp in copies:
            cp.wait_send()

    return pl.pallas_call(
        body,
        name=name,
        in_specs=[ANY] * (2 * n),
        out_specs=[ANY] * n,
        out_shape=[jax.ShapeDtypeStruct(p.shape, p.dtype) for p in landing],
        input_output_aliases={n + a: a for a in range(n)},
        scratch_shapes=[pltpu.SemaphoreType.DMA((3 * n,)), pltpu.SemaphoreType.DMA((3 * n,))],
    )(*pairs, *landing)


SUM_STEPS = 2


def _sum_chip_blocks(place, parts, keep_chip_axis, *, name):
    n = len(parts)

    def body(place_ref, *refs):
        for a in range(n):
            p = refs[a]
            total = (p[0].astype(F32) + p[1].astype(F32)) + (p[2].astype(F32) + p[3].astype(F32))
            if keep_chip_axis[a]:
                refs[n + a][0] = total
            else:
                refs[n + a][...] = total

    def in_spec(p):
        return pl.BlockSpec((N_CHIPS, p.shape[1] // SUM_STEPS, p.shape[2]), lambda i, place_ref: (0, i, 0))

    def out_spec(p, keep):
        rows = p.shape[1] // SUM_STEPS
        if keep:
            return pl.BlockSpec((1, rows, p.shape[2]), lambda i, place_ref: (place_ref[1], place_ref[0] * SUM_STEPS + i, 0))
        return pl.BlockSpec((rows, p.shape[2]), lambda i, place_ref: (place_ref[0] * SUM_STEPS + i, 0))

    def out_shape(p, keep):
        shape = (2 * p.shape[1], p.shape[2])
        return jax.ShapeDtypeStruct((N_CHIPS,) + shape if keep else shape, F32)

    return pl.pallas_call(
        body,
        name=name,
        grid_spec=pltpu.PrefetchScalarGridSpec(
            num_scalar_prefetch=1,
            grid=(SUM_STEPS,),
            in_specs=[in_spec(p) for p in parts],
            out_specs=[out_spec(p, k) for p, k in zip(parts, keep_chip_axis)],
        ),
        out_shape=[out_shape(p, k) for p, k in zip(parts, keep_chip_axis)],
        compiler_params=pltpu.CompilerParams(dimension_semantics=("arbitrary",), vmem_limit_bytes=VMEM_LIMIT),
    )(place, *parts)


def _join_halves_with_sibling(bufs, keep_chip_axis, *, name):
    n = len(bufs)

    def body(*refs):
        outs = refs[n : 2 * n]
        send_sems, recv_sems = refs[2 * n :]
        x, y, c, _ = _place()
        me_chip = 2 * x + y

        def half(a, cc):
            return _half(outs[a].at[me_chip] if keep_chip_axis[a] else outs[a], cc, 0)

        def remote(a, cc):
            return pltpu.make_async_remote_copy(
                src_ref=half(a, cc), dst_ref=half(a, cc), send_sem=send_sems.at[a], recv_sem=recv_sems.at[a],
                device_id=(x, y, 1 - c), device_id_type=MESH,
            )

        copies = [remote(a, c) for a in range(n)]
        for cp in copies:
            cp.start()
        for a in range(n):
            remote(a, 1 - c).wait_recv()
        for cp in copies:
            cp.wait_send()

    return pl.pallas_call(
        body,
        name=name,
        in_specs=[ANY] * n,
        out_specs=[ANY] * n,
        out_shape=[jax.ShapeDtypeStruct(b.shape, b.dtype) for b in bufs],
        input_output_aliases={a: a for a in range(n)},
        scratch_shapes=[pltpu.SemaphoreType.DMA((n,)), pltpu.SemaphoreType.DMA((n,))],
    )(*bufs)


def _reduce_scatter(grads, keep_chip_axis, place, *, tag):
    received = _swap_halves_with_sibling(grads, name=f"rs_swap{tag}")
    outs = _add_sibling_half(place, grads, received, name=f"rs_pair{tag}")
    n = len(grads)
    parts = _scatter_to_chips(outs[:n], outs[n:], name=f"rs_scatter{tag}")
    halves = _sum_chip_blocks(place, parts, keep_chip_axis, name=f"rs_sum{tag}")
    return _join_halves_with_sibling(halves, keep_chip_axis, name=f"rs_join{tag}")


def _adamw(w, g, m, v):
    m = ADAM_B1 * m + (1.0 - ADAM_B1) * g
    v = ADAM_B2 * v + (1.0 - ADAM_B2) * (g * g)
    m_hat = m / (1.0 - ADAM_B1**ADAM_STEP)
    v_hat = v / (1.0 - ADAM_B2**ADAM_STEP)
    delta = -ADAM_LR * (m_hat / (jnp.sqrt(v_hat) + ADAM_EPS) + ADAM_WD * w)
    return delta, m, v


def _adamw_large(w, m, v, layer_grads, *, name):
    depth, rows, cols = w.shape
    tile = math.gcd(rows, MM_TILE)
    assert tile % 8 == 0

    def body(w_ref, m_ref, v_ref, *refs):
        g_refs, (g_out, d_out, m_out, v_out) = refs[:depth], refs[depth:]
        for l in range(depth):

            @pl.when(pl.program_id(0) == l)
            def _(l=l):
                g = g_refs[l][...]
                delta, m_new, v_new = _adamw(w_ref[0], g, m_ref[0], v_ref[0])
                g_out[0], d_out[0], m_out[0], v_out[0] = g, delta, m_new, v_new

    def stacked():
        return pl.BlockSpec((1, tile, cols), lambda l, i: (l, i, 0))

    def layer_spec(l):
        return pl.BlockSpec((tile, cols), lambda k, i: (jnp.where(k == l, i, 0), 0))

    shape = jax.ShapeDtypeStruct(w.shape, F32)
    return pl.pallas_call(
        body,
        name=name,
        grid=(depth, rows // tile),
        in_specs=[stacked(), stacked(), stacked()] + [layer_spec(l) for l in range(depth)],
        out_specs=[stacked()] * 4,
        out_shape=[shape] * 4,
        compiler_params=pltpu.CompilerParams(dimension_semantics=("arbitrary", "arbitrary"), vmem_limit_bytes=VMEM_LIMIT),
    )(w, m, v, *layer_grads)


def _adamw_small(ws, gs, ms, vs, *, name):
    n = len(ws)

    def body(*refs):
        for a in range(n):
            delta, m_new, v_new = _adamw(refs[a][...], refs[n + a][...], refs[2 * n + a][...], refs[3 * n + a][...])
            refs[4 * n + a][...] = delta
            refs[5 * n + a][...] = m_new
            refs[6 * n + a][...] = v_new

    shapes = [jax.ShapeDtypeStruct(w.shape, F32) for w in ws]
    outs = pl.pallas_call(body, name=name, out_shape=shapes * 3)(*ws, *gs, *ms, *vs)
    return outs[:n], outs[n : 2 * n], outs[2 * n :]


WEIGHT_NAMES = (
    "w_in", "conv_a_w", "sg_ln_g", "sg_ln_b", "sg_w", "sg_b", "pool_w", "pool_scale", "cc_dw_w", "cc_dw_b", "cc_ln_g",
    "cc_ln_b", "cc_pw_w", "w_kv", "w_out", "ln_g", "ln_b",
)
LARGE = ("w_in", "cc_pw_w", "w_kv", "w_out")
TAPS_ROWS = 48


def _unpack_small_grads(small, chip):
    out = {}
    for r, k in enumerate(("sg_ln_g", "sg_ln_b", "pool_scale", "cc_dw_b", "cc_ln_g", "cc_ln_b")):
        out[k] = small[RG_VEC + r]
    out["conv_a_w"] = lax.dynamic_slice_in_dim(small[RG_CONVA : RG_CONVA + CONV_A], chip * HEAD_DIM, HEAD_DIM, axis=1)
    out["cc_dw_w"] = lax.dynamic_slice_in_dim(small[RG_DW : RG_DW + CONV_D], chip * HEAD_DIM, HEAD_DIM, axis=1)
    cat = jnp.concatenate([small[RG_SGW : RG_SGW + CHUNK], small[RG_SGW + CHUNK : RG_SGW + 2 * CHUNK]], axis=1)
    out["sg_w"] = jnp.transpose(cat.reshape(CHUNK, N_SUB, CHUNK), (1, 0, 2))
    out["sg_b"] = small[RG_SGB : RG_SGB + CHUNK].reshape(CHUNK, N_SUB, HEAD_DIM).sum(-1).T
    pool = small[RG_POOL : RG_POOL + D_G]
    out["pool_w"] = jnp.stack(
        [pool[g * HEAD_DIM : (g + 1) * HEAD_DIM, g * HEAD_DIM : (g + 1) * HEAD_DIM] for g in range(N_SUB)]
    )
    out["ln_g"] = small[RG_LN : RG_LN + 4].reshape(D_MODEL)
    out["ln_b"] = small[RG_LN + 4 : RG_LN + 8].reshape(D_MODEL)
    return out


def kernel(x, mem, w_in, conv_a_w, sg_ln_g, sg_ln_b, sg_w, sg_b, pool_w, pool_scale, cc_dw_w, cc_dw_b, cc_ln_g, cc_ln_b, cc_pw_w, w_kv, w_out, ln_g, ln_b, loss_target, m_w_in, m_conv_a_w, m_sg_ln_g, m_sg_ln_b, m_sg_w, m_sg_b, m_pool_w, m_pool_scale, m_cc_dw_w, m_cc_dw_b, m_cc_ln_g, m_cc_ln_b, m_cc_pw_w, m_w_kv, m_w_out, m_ln_g, m_ln_b, v_w_in, v_conv_a_w, v_sg_ln_g, v_sg_ln_b, v_sg_w, v_sg_b, v_pool_w, v_pool_scale, v_cc_dw_w, v_cc_dw_b, v_cc_ln_g, v_cc_ln_b, v_cc_pw_w, v_w_kv, v_w_out, v_ln_g, v_ln_b):
    given = dict(locals())
    weights = {k: given[k] for k in WEIGHT_NAMES}
    chip = 2 * lax.axis_index("x") + lax.axis_index("y")
    place = jnp.stack([lax.axis_index("c"), chip]).astype(jnp.int32)

    layers = []
    for l in range(DEPTH):
        taps = jnp.concatenate([conv_a_w[l], cc_dw_w[l]], axis=0)
        taps = jnp.pad(taps, ((0, TAPS_ROWS - taps.shape[0]), (0, 0)))
        g_in, g_out, g_kv, g_pw, g_taps = _all_gather_chips(
            place, [w_in[l], w_out[l], w_kv[l], cc_pw_w[l], taps], [BF16, BF16, BF16, BF16, F32], name=f"gather_weights{l}"
        )
        taps_full = jnp.transpose(g_taps, (1, 0, 2)).reshape(TAPS_ROWS, D_G)
        layers.append(
            dict(
                w_in=g_in,
                w_out=g_out.reshape(D_MIX, D_MODEL),
                w_kv=g_kv.reshape(D_MODEL, 2 * D_G),
                cc_pw_w=g_pw.reshape(D_G, D_G),
                conv_a_w=taps_full[0:CONV_A],
                cc_dw_w=taps_full[CONV_A : CONV_A + CONV_D],
                **{k: weights[k][l] for k in WEIGHT_NAMES if k not in LARGE + ("conv_a_w", "cc_dw_w")},
            )
        )

    loss_local, grad_x, grads = _local_step(x[0], mem[0], loss_target[0], layers)
    loss = lax.psum(loss_local, ("x", "y", "c"))

    reduced = []
    for l in reversed(range(DEPTH)):
        g = grads[l]
        r_in, r_out, r_kv, r_pw, r_small = _reduce_scatter(
            [
                g["w_in"],
                g["w_out"].reshape(N_CHIPS, D_MIX // N_CHIPS, D_MODEL),
                g["w_kv"].reshape(N_CHIPS, D_MODEL // N_CHIPS, 2 * D_G),
                g["cc_pw_w"].reshape(N_CHIPS, D_G // N_CHIPS, D_G),
                g["small"].reshape(N_CHIPS, RG_ROWS // N_CHIPS, D_G),
            ],
            [False, False, False, False, True],
            place,
            tag=str(l),
        )
        (small_all,) = _all_gather_placed([r_small], name=f"gather_small_grads{l}")
        out = _unpack_small_grads(small_all.reshape(RG_ROWS, D_G), chip)
        out.update(w_in=r_in, w_out=r_out, w_kv=r_kv, cc_pw_w=r_pw)
        reduced.append(out)
    reduced = reduced[::-1]

    grad, delta, new_m, new_v = {}, {}, {}, {}
    for k in LARGE:
        w3 = weights[k]
        grad[k], delta[k], new_m[k], new_v[k] = _adamw_large(
            w3, given["m_" + k], given["v_" + k], [reduced[l][k] for l in range(DEPTH)], name=f"adamw_{k}"
        )
    small_names = [k for k in WEIGHT_NAMES if k not in LARGE]
    for k in small_names:
        grad[k] = jnp.stack([reduced[l][k] for l in range(DEPTH)])
    d_s, m_s, v_s = _adamw_small(
        [weights[k] for k in small_names],
        [grad[k] for k in small_names],
        [given["m_" + k] for k in small_names],
        [given["v_" + k] for k in small_names],
        name="adamw_small",
    )
    for a, k in enumerate(small_names):
        delta[k], new_m[k], new_v[k] = d_s[a], m_s[a], v_s[a]

    return (
        loss,
        grad_x[None],
        *[grad[k] for k in WEIGHT_NAMES],
        *[delta[k] for k in WEIGHT_NAMES],
        *[new_m[k] for k in WEIGHT_NAMES],
        *[new_v[k] for k in WEIGHT_NAMES],
    )
```

```python
import functools
import math

import jax
import jax.numpy as jnp
from jax import lax
from jax.experimental import pallas as pl
from jax.experimental.pallas import tpu as pltpu

F32 = jnp.float32
BF16 = jnp.bfloat16

D_MODEL = 1024
DEPTH = 2
D_G = 256
D_MIX = 5 * D_G
D_IN = 9 * D_G + D_MIX
N_SUB = 4
HEAD_DIM = 64
CONV_A = 3
CONV_D = 31
CHUNK = 128
MEM_LEN = 256
N_CHIPS = 4
W_IN_SHARD = D_IN // N_CHIPS
LN_EPS = 1e-5
ALPHA = (2.0 * DEPTH) ** 0.25
ATT_SCALE = 1.0 / math.sqrt(HEAD_DIM)
GELU_C = math.sqrt(2.0 / math.pi)
GELU_A = 0.044715

ADAM_LR = 0.001
ADAM_B1 = 0.9
ADAM_B2 = 0.999
ADAM_EPS = 1e-08
ADAM_WD = 0.01
ADAM_STEP = 10

C_XA, C_BA, C_CA, C_U, C_V, C_XC, C_DA, C_DG, C_Q, C_GATE = (D_G * i for i in range(10))

HALO_A = 8
HALO_C = 16
HALO_D = 32

RW_VEC = 0
RW_CONVA = 16
RW_DW = 24
RW_SGB = 56
RW_ROWS = RW_SGB + CHUNK

RG_VEC = 0
RG_CONVA = 16
RG_DW = 24
RG_SGW = 56
RG_SGB = RG_SGW + 2 * CHUNK
RG_POOL = RG_SGB + CHUNK
RG_LN = RG_POOL + D_G
RG_ROWS = 768

VMEM_LIMIT = 56 * 1024 * 1024
SEQ_TILE = 256
MM_TILE = 512

MESH = pl.DeviceIdType.MESH
ANY = pl.BlockSpec(memory_space=pl.ANY)
NT = (((1,), (1,)), ((), ()))
TN = (((0,), (0,)), ((), ()))


def _dot(a, b):
    return jnp.dot(a, b, preferred_element_type=F32)


def _dot_nt(a, b):
    return lax.dot_general(a, b, NT, preferred_element_type=F32)


def _dot_tn(a, b):
    return lax.dot_general(a, b, TN, preferred_element_type=F32)


def _full(shape):
    zeros = (0,) * len(shape)
    return pl.BlockSpec(shape, lambda *_: zeros)


class _Exchange:
    def __init__(self, operands, outputs, sem_counts, before, after):
        self.operands, self.outputs, self.sem_counts, self.before, self.after = operands, outputs, sem_counts, before, after

    def specs(self, first_input, first_output):
        aliases = {first_input + src: first_output + j for j, (_, src) in enumerate(self.outputs) if src is not None}
        return (
            [ANY] * len(self.operands),
            [ANY] * len(self.outputs),
            [sds for sds, _ in self.outputs],
            [pltpu.SemaphoreType.DMA((k,)) for k in self.sem_counts],
            aliases,
        )

    def split(self, ins, outs):
        refs = list(ins)
        for j, (_, src) in enumerate(self.outputs):
            if src is not None:
                refs[src] = outs[j]
        return refs


def _when(cond, fn):
    if isinstance(cond, bool):
        if cond:
            fn()
    else:
        pl.when(cond)(fn)


def _run_exchange(exchange, *, name):
    n_in, n_out = len(exchange.operands), len(exchange.outputs)
    in_specs, out_specs, out_shape, sems, aliases = exchange.specs(0, 0)

    def body(*refs):
        ins, outs, sem_refs = refs[:n_in], refs[n_in : n_in + n_out], refs[n_in + n_out :]
        refs = exchange.split(ins, outs)
        exchange.before(0, 1, refs, outs, sem_refs)
        exchange.after(0, 1, refs, outs, sem_refs)

    return pl.pallas_call(
        body, name=name, in_specs=in_specs, out_specs=out_specs, out_shape=out_shape, scratch_shapes=sems,
        input_output_aliases=aliases,
    )(*exchange.operands)


def _gridded_call(body, *, name, steps, in_specs, out_specs, out_shape, scratch_shapes, operands, exchange=None):
    params = pltpu.CompilerParams(dimension_semantics=("arbitrary",), vmem_limit_bytes=VMEM_LIMIT)
    if exchange is None:
        outs = pl.pallas_call(
            body, name=name, grid=(steps,), in_specs=in_specs, out_specs=out_specs, out_shape=out_shape,
            scratch_shapes=scratch_shapes, compiler_params=params,
        )(*operands)
        return list(outs), []
    n_in, n_out, n_scr = len(in_specs), len(out_specs), len(scratch_shapes)
    x_in, x_out = len(exchange.operands), len(exchange.outputs)
    ex_in_specs, ex_out_specs, ex_out_shape, ex_sems, aliases = exchange.specs(n_in, n_out)

    def full(*refs):
        own_in, refs = refs[:n_in], refs[n_in:]
        ex_in, refs = refs[:x_in], refs[x_in:]
        own_out, refs = refs[:n_out], refs[n_out:]
        ex_out, refs = refs[:x_out], refs[x_out:]
        own_scr, sem_refs = refs[:n_scr], refs[n_scr:]
        ex_refs = exchange.split(ex_in, ex_out)
        step = pl.program_id(0)
        exchange.before(step, steps, ex_refs, ex_out, sem_refs)
        body(*own_in, *own_out, *own_scr)
        exchange.after(step, steps, ex_refs, ex_out, sem_refs)

    outs = pl.pallas_call(
        full, name=name, grid=(steps,), in_specs=in_specs + ex_in_specs, out_specs=out_specs + ex_out_specs,
        out_shape=out_shape + ex_out_shape, scratch_shapes=scratch_shapes + ex_sems, input_output_aliases=aliases,
        compiler_params=params,
    )(*operands, *exchange.operands)
    return list(outs[:n_out]), list(outs[n_out:])


def _sigmoid(x):
    return 1.0 / (1.0 + jnp.exp(-x))


def _gelu(x):
    t = jnp.tanh(GELU_C * (x + GELU_A * x * x * x))
    return 0.5 * x * (1.0 + t), t


def _gelu_grad(x, t):
    return 0.5 * (1.0 + t) + 0.5 * x * (1.0 - t * t) * (GELU_C * (1.0 + 3.0 * GELU_A * x * x))


def _normalize(v):
    mu = jnp.mean(v, axis=-1, keepdims=True)
    d = v - mu
    var = jnp.mean(d * d, axis=-1, keepdims=True)
    rstd = lax.rsqrt(var + LN_EPS)
    return d * rstd, rstd


def _normalize_grad(dhat, hat, rstd):
    m1 = jnp.mean(dhat, axis=-1, keepdims=True)
    m2 = jnp.mean(dhat * hat, axis=-1, keepdims=True)
    return rstd * (dhat - m1 - hat * m2)


def _lane(width=D_G):
    return lax.broadcasted_iota(jnp.int32, (1, width), 1)


def _head_masks():
    head = _lane() // HEAD_DIM
    return [(head == h).astype(F32) for h in range(N_SUB)]


def _stack_heads(v, masks):
    return jnp.concatenate([v * m for m in masks], axis=0)


def _tril_mask_cat():
    t = lax.broadcasted_iota(jnp.int32, (CHUNK, N_SUB * CHUNK), 0)
    s = lax.broadcasted_iota(jnp.int32, (CHUNK, N_SUB * CHUNK), 1) % CHUNK
    return s <= t


def _triu_mask_cat():
    s = lax.broadcasted_iota(jnp.int32, (CHUNK, N_SUB * CHUNK), 0)
    t = lax.broadcasted_iota(jnp.int32, (CHUNK, N_SUB * CHUNK), 1) % CHUNK
    return t >= s


def _pool_select(a2, a4, a8, a16):
    lane = _lane()
    return jnp.where(lane < 64, a2, jnp.where(lane < 128, a4, jnp.where(lane < 192, a8, a16)))


def _pool_count(row0, rows):
    t = row0 + lax.broadcasted_iota(jnp.int32, (rows, D_G), 0)
    lane = lax.broadcasted_iota(jnp.int32, (rows, D_G), 1)
    win = jnp.where(lane < 64, 2, jnp.where(lane < 128, 4, jnp.where(lane < 192, 8, 16)))
    return jnp.minimum(t + 1, win).astype(F32)


def _trailing_window_sum(halo, cur):
    e = jnp.concatenate([halo, cur], axis=0)
    s2 = e + pltpu.roll(e, 1, 0)
    s4 = s2 + pltpu.roll(s2, 2, 0)
    s8 = s4 + pltpu.roll(s4, 4, 0)
    s16 = s8 + pltpu.roll(s8, 8, 0)
    return _pool_select(s2, s4, s8, s16)[HALO_C:]


def _leading_window_sum(cur, halo):
    e = jnp.concatenate([cur, halo], axis=0)
    n = e.shape[0]
    s2 = e + pltpu.roll(e, n - 1, 0)
    s4 = s2 + pltpu.roll(s2, n - 2, 0)
    s8 = s4 + pltpu.roll(s4, n - 4, 0)
    s16 = s8 + pltpu.roll(s8, n - 8, 0)
    return _pool_select(s2, s4, s8, s16)[: cur.shape[0]]


def _softmax_blocks(sc):
    out = []
    for h in range(N_SUB):
        s = sc[:, h * MEM_LEN : (h + 1) * MEM_LEN]
        e = jnp.exp(s - jnp.max(s, axis=-1, keepdims=True))
        out.append(e / jnp.sum(e, axis=-1, keepdims=True))
    return jnp.concatenate(out, axis=-1)


def _sgu_mix(vn, wcat_b, sgb, masks):
    vbd = _stack_heads(vn, masks).astype(BF16)
    return _dot(wcat_b, vbd) + sgb, vbd


def _layer_forward(x, win, kst, vst, wout, sw, wcat, poolw, pww, ln, tgt, *, name, exchange=None):
    seq = x.shape[0]
    tile = min(SEQ_TILE, seq)
    n_tiles = seq // tile
    last = tgt is not None

    def body(*refs):
        x_ref, win_ref, kst_ref, vst_ref, wout_ref, sw_ref, wcat_ref, pool_ref, pw_ref, ln_ref = refs[:10]
        refs = refs[10:]
        if last:
            tgt_ref, refs = refs[0], refs[1:]
        proj_ref, y_ref, z_ref, out_ref = refs[:4]
        refs = refs[4:]
        if last:
            loss_ref, refs = refs[0], refs[1:]
        pbuf, xchalo, gbuf = refs
        i = pl.program_id(0)

        @pl.when(i == 0)
        def _():
            pbuf[0:HALO_A, :] = jnp.zeros((HALO_A, D_G), F32)
            xchalo[...] = jnp.zeros((HALO_C, D_G), F32)
            gbuf[0:HALO_D, :] = jnp.zeros((HALO_D, D_G), F32)
            if last:
                loss_ref[...] = jnp.zeros((8, 128), F32)

        xt = x_ref[...]
        xb = xt.astype(BF16)
        for k in range(N_CHIPS):
            proj_ref[:, k * W_IN_SHARD : (k + 1) * W_IN_SHARD] = _dot(xb, win_ref[k])
        masks = _head_masks()

        pbuf[HALO_A : HALO_A + tile, :] = proj_ref[:, C_CA : C_CA + D_G] * proj_ref[:, C_XA : C_XA + D_G]
        cv = jnp.zeros((tile, D_G), F32)
        for k in range(CONV_A):
            off = HALO_A - (CONV_A - 1) + k
            cv = cv + sw_ref[RW_CONVA + k : RW_CONVA + k + 1, :] * pbuf[off : off + tile, :]
        y_ref[:, 0:D_G] = proj_ref[:, C_BA : C_BA + D_G] * cv
        pbuf[0:HALO_A, :] = pbuf[tile : tile + HALO_A, :]

        ua, _ = _gelu(proj_ref[:, C_U : C_U + D_G])
        vg, _ = _gelu(proj_ref[:, C_V : C_V + D_G])
        vhat, _ = _normalize(vg)
        vn = vhat * sw_ref[RW_VEC : RW_VEC + 1, :] + sw_ref[RW_VEC + 1 : RW_VEC + 2, :]
        wcat_b = jnp.where(_tril_mask_cat(), wcat_ref[...], 0.0).astype(BF16)
        sgb = sw_ref[RW_SGB : RW_SGB + CHUNK, :]
        for j in range(tile // CHUNK):
            rows = slice(j * CHUNK, (j + 1) * CHUNK)
            mixed, _ = _sgu_mix(vn[rows], wcat_b, sgb, masks)
            y_ref[rows, D_G : 2 * D_G] = ua[rows] * mixed

        xc = proj_ref[:, C_XC : C_XC + D_G]
        wsum = _trailing_window_sum(xchalo[...], xc)
        pm = wsum / _pool_count(i * tile, tile) - xc
        y_ref[:, 2 * D_G : 3 * D_G] = _dot(pm.astype(BF16), pool_ref[...]) * sw_ref[RW_VEC + 2 : RW_VEC + 3, :]
        xchalo[...] = xc[tile - HALO_C :, :]

        gbuf[HALO_D : HALO_D + tile, :] = proj_ref[:, C_DA : C_DA + D_G] * _sigmoid(proj_ref[:, C_DG : C_DG + D_G])
        cvd = jnp.zeros((tile, D_G), F32) + sw_ref[RW_VEC + 3 : RW_VEC + 4, :]
        for k in range(CONV_D):
            off = HALO_D - (CONV_D - 1) + k
            cvd = cvd + sw_ref[RW_DW + k : RW_DW + k + 1, :] * gbuf[off : off + tile, :]
        nhat, _ = _normalize(cvd)
        nrm = nhat * sw_ref[RW_VEC + 4 : RW_VEC + 5, :] + sw_ref[RW_VEC + 5 : RW_VEC + 6, :]
        y_ref[:, 3 * D_G : 4 * D_G] = _dot((nrm * _sigmoid(nrm)).astype(BF16), pw_ref[...])
        gbuf[0:HALO_D, :] = gbuf[tile : tile + HALO_D, :]

        qb = proj_ref[:, C_Q : C_Q + D_G].astype(BF16)
        p_all = _softmax_blocks(_dot_nt(qb, kst_ref[...]) * ATT_SCALE)
        y_ref[:, 4 * D_G : 5 * D_G] = _dot(p_all.astype(BF16), vst_ref[...])

        gate = proj_ref[:, C_GATE:]
        hid = y_ref[...] * (gate * _sigmoid(gate))
        z = ALPHA * xt + _dot(hid.astype(BF16), wout_ref[...])
        z_ref[...] = z
        zhat, _ = _normalize(z)
        xn = zhat * ln_ref[0:1, :] + ln_ref[1:2, :]
        if last:
            err = xn - tgt_ref[...]
            out_ref[...] = err * (1.0 / D_MODEL)
            loss_ref[...] += jnp.sum(err * err) * (0.5 / D_MODEL)
        else:
            out_ref[...] = xn

    def rows(width):
        return pl.BlockSpec((tile, width), lambda i: (i, 0))

    operands = [x, win, kst, vst, wout, sw, wcat, poolw, pww, ln]
    in_specs = [rows(D_MODEL)] + [_full(a.shape) for a in operands[1:]]
    out_shape = [
        jax.ShapeDtypeStruct((seq, D_IN), F32),
        jax.ShapeDtypeStruct((seq, D_MIX), F32),
        jax.ShapeDtypeStruct((seq, D_MODEL), F32),
        jax.ShapeDtypeStruct((seq, D_MODEL), F32),
    ]
    out_specs = [rows(D_IN), rows(D_MIX), rows(D_MODEL), rows(D_MODEL)]
    if last:
        operands.append(tgt)
        in_specs.append(rows(D_MODEL))
        out_shape.append(jax.ShapeDtypeStruct((8, 128), F32))
        out_specs.append(_full((8, 128)))
    return _gridded_call(
        body,
        name=name,
        steps=n_tiles,
        in_specs=in_specs,
        out_specs=out_specs,
        out_shape=out_shape,
        scratch_shapes=[
            pltpu.VMEM((HALO_A + tile, D_G), F32),
            pltpu.VMEM((HALO_C, D_G), F32),
            pltpu.VMEM((HALO_D + tile, D_G), F32),
        ],
        operands=operands,
        exchange=exchange,
    )


def _layer_backward(dxn, z, proj, y, kst, vst, wout, sw, wcat, wcat_t, poolw, pww, ln, *, name, exchange=None):
    seq = dxn.shape[0]
    tile = min(SEQ_TILE, seq)
    n_tiles = seq // tile
    halo_blocks = tile // HALO_D

    def body(
        dxn_ref, z_ref, proj_ref, halo_ref, y_ref, kst_ref, vst_ref, wout_ref, sw_ref, wcat_ref, wcat_t_ref, pool_ref,
        pw_ref, ln_ref, dproj_ref, dz_ref, dwout_ref, dkst_ref, dvst_ref, dpw_ref, sg_ref,
        pbuf, dcvbuf, rhalo, gbuf, dgbuf,
    ):
        i = pl.program_id(0)
        ti = n_tiles - 1 - i

        @pl.when(i == 0)
        def _():
            dwout_ref[...] = jnp.zeros(dwout_ref.shape, F32)
            dkst_ref[...] = jnp.zeros(dkst_ref.shape, F32)
            dvst_ref[...] = jnp.zeros(dvst_ref.shape, F32)
            dpw_ref[...] = jnp.zeros(dpw_ref.shape, F32)
            sg_ref[...] = jnp.zeros(sg_ref.shape, F32)
            dcvbuf[tile : tile + HALO_A, :] = jnp.zeros((HALO_A, D_G), F32)
            rhalo[...] = jnp.zeros((HALO_C, D_G), F32)
            dgbuf[tile : tile + HALO_D, :] = jnp.zeros((HALO_D, D_G), F32)

        def acc_row(row, val):
            sg_ref[row : row + 1, :] += jnp.sum(val, axis=0, keepdims=True)

        masks = _head_masks()
        has_past = (ti > 0).astype(F32)

        zhat, zrstd = _normalize(z_ref[...])
        dxn_t = dxn_ref[...]
        dlg = jnp.sum(dxn_t * zhat, axis=0, keepdims=True)
        dlb = jnp.sum(dxn_t, axis=0, keepdims=True)
        for j in range(D_MODEL // D_G):
            sg_ref[RG_LN + j : RG_LN + j + 1, :] += dlg[:, j * D_G : (j + 1) * D_G]
            sg_ref[RG_LN + 4 + j : RG_LN + 5 + j, :] += dlb[:, j * D_G : (j + 1) * D_G]
        dz = _normalize_grad(dxn_t * ln_ref[0:1, :], zhat, zrstd)
        dz_ref[...] = dz
        dzb = dz.astype(BF16)

        gate = proj_ref[:, C_GATE:]
        sgm = _sigmoid(gate)
        silu = gate * sgm
        yc = y_ref[...]
        dwout_ref[...] += _dot_tn((yc * silu).astype(BF16), dzb)
        dh = _dot_nt(dzb, wout_ref[...])
        dproj_ref[:, C_GATE:] = (dh * yc * (sgm * (1.0 + gate * (1.0 - sgm)))).astype(BF16)
        dy = dh * silu

        dya = dy[:, 0:D_G]
        xa = proj_ref[:, C_XA : C_XA + D_G]
        ba = proj_ref[:, C_BA : C_BA + D_G]
        ca = proj_ref[:, C_CA : C_CA + D_G]
        past = slice(HALO_D - HALO_A, HALO_D)
        pbuf[0:HALO_A, :] = halo_ref[past, C_CA : C_CA + D_G] * halo_ref[past, C_XA : C_XA + D_G] * has_past
        pbuf[HALO_A : HALO_A + tile, :] = ca * xa
        cv = jnp.zeros((tile, D_G), F32)
        for k in range(CONV_A):
            off = HALO_A - (CONV_A - 1) + k
            cv = cv + sw_ref[RW_CONVA + k : RW_CONVA + k + 1, :] * pbuf[off : off + tile, :]
        dproj_ref[:, C_BA : C_BA + D_G] = (dya * cv).astype(BF16)
        dcv = dya * ba
        dcvbuf[0:tile, :] = dcv
        dp = jnp.zeros((tile, D_G), F32)
        for k in range(CONV_A):
            off = HALO_A - (CONV_A - 1) + k
            acc_row(RG_CONVA + k, dcv * pbuf[off : off + tile, :])
            back = CONV_A - 1 - k
            dp = dp + sw_ref[RW_CONVA + k : RW_CONVA + k + 1, :] * dcvbuf[back : back + tile, :]
        dproj_ref[:, C_CA : C_CA + D_G] = (dp * xa).astype(BF16)
        dproj_ref[:, C_XA : C_XA + D_G] = (dp * ca).astype(BF16)
        dcvbuf[tile : tile + HALO_A, :] = dcvbuf[0:HALO_A, :]

        dyb = dy[:, D_G : 2 * D_G]
        u = proj_ref[:, C_U : C_U + D_G]
        v = proj_ref[:, C_V : C_V + D_G]
        ua, ut = _gelu(u)
        vg, vt = _gelu(v)
        vhat, vrstd = _normalize(vg)
        sg_g = sw_ref[RW_VEC : RW_VEC + 1, :]
        vn = vhat * sg_g + sw_ref[RW_VEC + 1 : RW_VEC + 2, :]
        tril = _tril_mask_cat()
        wcat_b = jnp.where(tril, wcat_ref[...], 0.0).astype(BF16)
        wcat_tb = jnp.where(_triu_mask_cat(), wcat_t_ref[...], 0.0).astype(BF16)
        sgb = sw_ref[RW_SGB : RW_SGB + CHUNK, :]
        dmixed = dyb * ua
        dvn_parts = []
        du_parts = []
        dwcat = jnp.zeros((CHUNK, N_SUB * CHUNK), F32)
        dsgb = jnp.zeros((CHUNK, D_G), F32)
        for j in range(tile // CHUNK):
            rows = slice(j * CHUNK, (j + 1) * CHUNK)
            mixed, vbd = _sgu_mix(vn[rows], wcat_b, sgb, masks)
            du_parts.append(dyb[rows] * mixed)
            dmx = dmixed[rows]
            dsgb = dsgb + dmx
            dwcat = dwcat + _dot_nt(dmx.astype(BF16), vbd)
            dvn_parts.append(_dot(wcat_tb, _stack_heads(dmx, masks).astype(BF16)))
        dwcat = jnp.where(tril, dwcat, 0.0)
        sg_ref[RG_SGW : RG_SGW + CHUNK, :] += dwcat[:, 0:D_G]
        sg_ref[RG_SGW + CHUNK : RG_SGW + 2 * CHUNK, :] += dwcat[:, D_G:]
        sg_ref[RG_SGB : RG_SGB + CHUNK, :] += dsgb
        dvn = jnp.concatenate(dvn_parts, axis=0)
        du_act = jnp.concatenate(du_parts, axis=0)
        acc_row(RG_VEC, dvn * vhat)
        acc_row(RG_VEC + 1, dvn)
        dvg = _normalize_grad(dvn * sg_g, vhat, vrstd)
        dproj_ref[:, C_U : C_U + D_G] = (du_act * _gelu_grad(u, ut)).astype(BF16)
        dproj_ref[:, C_V : C_V + D_G] = (dvg * _gelu_grad(v, vt)).astype(BF16)

        dyc = dy[:, 2 * D_G : 3 * D_G]
        xc = proj_ref[:, C_XC : C_XC + D_G]
        xc_past = halo_ref[HALO_D - HALO_C : HALO_D, C_XC : C_XC + D_G] * has_past
        cnt = _pool_count(ti * tile, tile)
        pm = _trailing_window_sum(xc_past, xc) / cnt - xc
        pmb = pm.astype(BF16)
        pool_b = pool_ref[...]
        scale = sw_ref[RW_VEC + 2 : RW_VEC + 3, :]
        acc_row(RG_VEC + 2, dyc * _dot(pmb, pool_b))
        dpre = (dyc * scale).astype(BF16)
        sg_ref[RG_POOL : RG_POOL + D_G, :] += _dot_tn(pmb, dpre)
        dpm = _dot_nt(dpre, pool_b)
        r = dpm / cnt
        dproj_ref[:, C_XC : C_XC + D_G] = (_leading_window_sum(r, rhalo[...]) - dpm).astype(BF16)
        rhalo[...] = r[0:HALO_C, :]

        dyd = dy[:, 3 * D_G : 4 * D_G]
        da = proj_ref[:, C_DA : C_DA + D_G]
        sgd = _sigmoid(proj_ref[:, C_DG : C_DG + D_G])
        gbuf[0:HALO_D, :] = halo_ref[:, C_DA : C_DA + D_G] * _sigmoid(halo_ref[:, C_DG : C_DG + D_G]) * has_past
        gbuf[HALO_D : HALO_D + tile, :] = da * sgd
        cvd = jnp.zeros((tile, D_G), F32) + sw_ref[RW_VEC + 3 : RW_VEC + 4, :]
        for k in range(CONV_D):
            off = HALO_D - (CONV_D - 1) + k
            cvd = cvd + sw_ref[RW_DW + k : RW_DW + k + 1, :] * gbuf[off : off + tile, :]
        nhat, nrstd = _normalize(cvd)
        cc_g = sw_ref[RW_VEC + 4 : RW_VEC + 5, :]
        nrm = nhat * cc_g + sw_ref[RW_VEC + 5 : RW_VEC + 6, :]
        sgn = _sigmoid(nrm)
        dydb = dyd.astype(BF16)
        dpw_ref[...] += _dot_tn((nrm * sgn).astype(BF16), dydb)
        dn = _dot_nt(dydb, pw_ref[...]) * (sgn * (1.0 + nrm * (1.0 - sgn)))
        acc_row(RG_VEC + 4, dn * nhat)
        acc_row(RG_VEC + 5, dn)
        dcvd = _normalize_grad(dn * cc_g, nhat, nrstd)
        acc_row(RG_VEC + 3, dcvd)
        dgbuf[0:tile, :] = dcvd
        dg = jnp.zeros((tile, D_G), F32)
        for k in range(CONV_D):
            off = HALO_D - (CONV_D - 1) + k
            acc_row(RG_DW + k, dcvd * gbuf[off : off + tile, :])
            back = CONV_D - 1 - k
            dg = dg + sw_ref[RW_DW + k : RW_DW + k + 1, :] * dgbuf[back : back + tile, :]
        dproj_ref[:, C_DA : C_DA + D_G] = (dg * sgd).astype(BF16)
        dproj_ref[:, C_DG : C_DG + D_G] = (dg * da * sgd * (1.0 - sgd)).astype(BF16)
        dgbuf[tile : tile + HALO_D, :] = dgbuf[0:HALO_D, :]

        dyeb = dy[:, 4 * D_G : 5 * D_G].astype(BF16)
        qb = proj_ref[:, C_Q : C_Q + D_G].astype(BF16)
        kst_b = kst_ref[...]
        p_all = _softmax_blocks(_dot_nt(qb, kst_b) * ATT_SCALE)
        dvst_ref[...] += _dot_tn(p_all.astype(BF16), dyeb)
        dp_all = _dot_nt(dyeb, vst_ref[...])
        ds = []
        for h in range(N_SUB):
            blk = slice(h * MEM_LEN, (h + 1) * MEM_LEN)
            p, dpb = p_all[:, blk], dp_all[:, blk]
            ds.append(p * (dpb - jnp.sum(dpb * p, axis=-1, keepdims=True)))
        dsb = (jnp.concatenate(ds, axis=-1) * ATT_SCALE).astype(BF16)
        dproj_ref[:, C_Q : C_Q + D_G] = _dot(dsb, kst_b).astype(BF16)
        dkst_ref[...] += _dot_tn(dsb, qb)

    def rows(width):
        return pl.BlockSpec((tile, width), lambda i: (n_tiles - 1 - i, 0))

    halo_spec = pl.BlockSpec((HALO_D, D_IN), lambda i: (jnp.maximum((n_tiles - 1 - i) * halo_blocks - 1, 0), 0))
    weights = [kst, vst, wout, sw, wcat, wcat_t, poolw, pww, ln]
    acc_shapes = [(D_MIX, D_MODEL), (N_SUB * MEM_LEN, D_G), (N_SUB * MEM_LEN, D_G), (D_G, D_G), (RG_ROWS, D_G)]
    return _gridded_call(
        body,
        name=name,
        steps=n_tiles,
        in_specs=[rows(D_MODEL), rows(D_MODEL), rows(D_IN), halo_spec, rows(D_MIX)] + [_full(a.shape) for a in weights],
        out_specs=[rows(D_IN), rows(D_MODEL)] + [_full(s) for s in acc_shapes],
        out_shape=[jax.ShapeDtypeStruct((seq, D_IN), BF16), jax.ShapeDtypeStruct((seq, D_MODEL), F32)]
        + [jax.ShapeDtypeStruct(s, F32) for s in acc_shapes],
        scratch_shapes=[
            pltpu.VMEM((HALO_A + tile, D_G), F32),
            pltpu.VMEM((tile + HALO_A, D_G), F32),
            pltpu.VMEM((HALO_C, D_G), F32),
            pltpu.VMEM((HALO_D + tile, D_G), F32),
            pltpu.VMEM((tile + HALO_D, D_G), F32),
        ],
        operands=[dxn, z, proj, proj, y, *weights],
        exchange=exchange,
    )


def _kv_forward(mem, wkv, *, name):
    def body(mem_ref, wkv_ref, kst_ref, vst_ref):
        kv = _dot(mem_ref[...].astype(BF16), wkv_ref[...])
        masks = _head_masks()
        kst_ref[...] = _stack_heads(kv[:, 0:D_G], masks).astype(BF16)
        vst_ref[...] = _stack_heads(kv[:, D_G:], masks).astype(BF16)

    shape = jax.ShapeDtypeStruct((N_SUB * MEM_LEN, D_G), BF16)
    return pl.pallas_call(body, name=name, out_shape=[shape, shape])(mem, wkv)


def _kv_backward(mem, dkst, dvst, *, name):
    def body(mem_ref, dkst_ref, dvst_ref, dwkv_ref):
        masks = _head_masks()
        memb = mem_ref[...].astype(BF16)
        for col, ref in ((0, dkst_ref), (D_G, dvst_ref)):
            d = jnp.zeros((MEM_LEN, D_G), F32)
            for h in range(N_SUB):
                d = d + ref[h * MEM_LEN : (h + 1) * MEM_LEN, :] * masks[h]
            dwkv_ref[:, col : col + D_G] = _dot_tn(memb, d.astype(BF16))

    return pl.pallas_call(body, name=name, out_shape=jax.ShapeDtypeStruct((D_MODEL, 2 * D_G), F32))(mem, dkst, dvst)


def _input_grad(dproj, dz, win, *, name, exchange=None):
    seq = dproj.shape[0]
    tile = min(MM_TILE, seq)

    def body(dproj_ref, dz_ref, win_ref, dx_ref):
        acc = ALPHA * dz_ref[...]
        for k in range(N_CHIPS):
            acc = acc + _dot_nt(dproj_ref[:, k * W_IN_SHARD : (k + 1) * W_IN_SHARD], win_ref[k])
        dx_ref[...] = acc

    return _gridded_call(
        body,
        name=name,
        steps=seq // tile,
        in_specs=[
            pl.BlockSpec((tile, D_IN), lambda i: (i, 0)),
            pl.BlockSpec((tile, D_MODEL), lambda i: (i, 0)),
            _full(win.shape),
        ],
        out_specs=[pl.BlockSpec((tile, D_MODEL), lambda i: (i, 0))],
        out_shape=[jax.ShapeDtypeStruct((seq, D_MODEL), F32)],
        scratch_shapes=[],
        operands=[dproj, dz, win],
        exchange=exchange,
    )


def _input_weight_grad(x, dproj, *, name):
    seq = x.shape[0]
    tile = min(MM_TILE, seq)

    def body(x_ref, dproj_ref, dwin_ref):
        @pl.when(pl.program_id(1) == 0)
        def _():
            dwin_ref[...] = jnp.zeros(dwin_ref.shape, F32)

        dwin_ref[0] += _dot_tn(x_ref[...].astype(BF16), dproj_ref[...])

    return pl.pallas_call(
        body,
        name=name,
        grid=(N_CHIPS, seq // tile),
        in_specs=[
            pl.BlockSpec((tile, D_MODEL), lambda k, i: (i, 0)),
            pl.BlockSpec((tile, W_IN_SHARD), lambda k, i: (i, k)),
        ],
        out_specs=pl.BlockSpec((1, D_MODEL, W_IN_SHARD), lambda k, i: (k, 0, 0)),
        out_shape=jax.ShapeDtypeStruct((N_CHIPS, D_MODEL, W_IN_SHARD), F32),
        compiler_params=pltpu.CompilerParams(dimension_semantics=("arbitrary", "arbitrary"), vmem_limit_bytes=VMEM_LIMIT),
    )(x, dproj)


def _expand_sgb(sg_b):
    return jnp.repeat(sg_b.T, HEAD_DIM, axis=1)


def _pack_small_weights(sg_ln_g, sg_ln_b, pool_scale, cc_dw_b, cc_ln_g, cc_ln_b, conv_a_w, cc_dw_w, sg_b):
    vec = jnp.stack([sg_ln_g, sg_ln_b, pool_scale, cc_dw_b, cc_ln_g, cc_ln_b])
    return jnp.concatenate(
        [
            jnp.pad(vec, ((0, RW_CONVA - RW_VEC - 6), (0, 0))),
            jnp.pad(conv_a_w, ((0, RW_DW - RW_CONVA - CONV_A), (0, 0))),
            jnp.pad(cc_dw_w, ((0, RW_SGB - RW_DW - CONV_D), (0, 0))),
            _expand_sgb(sg_b),
        ]
    )


def _sg_w_cat(sg_w):
    cat = jnp.transpose(sg_w, (1, 0, 2)).reshape(CHUNK, N_SUB * CHUNK)
    cat_t = jnp.transpose(sg_w, (2, 0, 1)).reshape(CHUNK, N_SUB * CHUNK)
    return cat, cat_t


def _pool_block_diag(pool_w):
    out = jnp.zeros((D_G, D_G), pool_w.dtype)
    for g in range(N_SUB):
        out = out.at[g * HEAD_DIM : (g + 1) * HEAD_DIM, g * HEAD_DIM : (g + 1) * HEAD_DIM].set(pool_w[g])
    return out


def _prepare_layer(mem, w, l):
    cat, cat_t = _sg_w_cat(w["sg_w"])
    kst, vst = _kv_forward(mem, w["w_kv"], name=f"kv_fwd{l}")
    return dict(
        win=w["w_in"],
        wout=w["w_out"],
        pww=w["cc_pw_w"],
        sw=_pack_small_weights(
            w["sg_ln_g"], w["sg_ln_b"], w["pool_scale"], w["cc_dw_b"], w["cc_ln_g"], w["cc_ln_b"],
            w["conv_a_w"], w["cc_dw_w"], w["sg_b"],
        ),
        wcat=cat,
        wcat_t=cat_t,
        poolw=_pool_block_diag(w["pool_w"]).astype(BF16),
        ln=jnp.stack([w["ln_g"], w["ln_b"]]),
        kst=kst,
        vst=vst,
    )


def _forward(l, h, p, tgt, exchange=None):
    return _layer_forward(
        h, p["win"], p["kst"], p["vst"], p["wout"], p["sw"], p["wcat"], p["poolw"], p["pww"], p["ln"], tgt,
        name=f"layer_fwd{l}", exchange=exchange,
    )


def _backward(l, dxn, s, p, exchange=None):
    return _layer_backward(
        dxn, s["z"], s["proj"], s["y"], p["kst"], p["vst"], p["wout"], p["sw"], p["wcat"], p["wcat_t"], p["poolw"],
        p["pww"], p["ln"], name=f"layer_bwd{l}", exchange=exchange,
    )


def _place():
    x, y, c = lax.axis_index("x"), lax.axis_index("y"), lax.axis_index("c")
    others = [(1 - x, y), (x, 1 - y), (1 - x, 1 - y)]
    return x, y, c, others


def _half(ref, c, axis):
    n = ref.shape[axis] // 2
    if axis == 0:
        return ref.at[pl.ds(c * n, n)]
    return ref.at[:, pl.ds(c * n, n)]


def _place_own_block(place, shards, dtypes, *, name):
    n = len(shards)

    def body(place_ref, *refs):
        for a in range(n):
            refs[n + a][0] = refs[a][...].astype(dtypes[a])

    return pl.pallas_call(
        body,
        name=name,
        grid_spec=pltpu.PrefetchScalarGridSpec(
            num_scalar_prefetch=1,
            grid=(1,),
            in_specs=[pl.BlockSpec(s.shape, lambda i, place_ref: (0, 0)) for s in shards],
            out_specs=[pl.BlockSpec((1,) + s.shape, lambda i, place_ref: (place_ref[1], 0, 0)) for s in shards],
        ),
        out_shape=[jax.ShapeDtypeStruct((N_CHIPS,) + s.shape, dt) for s, dt in zip(shards, dtypes)],
        compiler_params=pltpu.CompilerParams(dimension_semantics=("arbitrary",), vmem_limit_bytes=VMEM_LIMIT),
    )(place, *shards)


def _sds(a):
    return jax.ShapeDtypeStruct(a.shape, a.dtype)


def _gather_exchange(bufs):
    n = len(bufs)

    def remote(sems, block, k, to):
        return pltpu.make_async_remote_copy(
            src_ref=block, dst_ref=block, send_sem=sems[0].at[k], recv_sem=sems[1].at[k], device_id=to, device_id_type=MESH
        )

    def before(step, steps, refs, outs, sems):
        def send():
            x, y, c, others = _place()
            for j, (px, py) in enumerate(others):
                for a in range(n):
                    remote(sems, _half(refs[a].at[2 * x + y], c, 0), 3 * a + j, (px, py, c)).start()

        _when(step == 0, send)

    def after(step, steps, refs, outs, sems):
        def pass_on():
            x, y, c, others = _place()
            for j, (px, py) in enumerate(others):
                for a in range(n):
                    landed = _half(refs[a].at[2 * px + py], c, 0)
                    remote(sems, landed, 3 * a + j, (px, py, c)).wait_recv()
                    remote(sems, landed, 3 * n + 3 * a + j, (x, y, 1 - c)).start()

        def finish():
            x, y, c, others = _place()
            for j, (px, py) in enumerate(others):
                for a in range(n):
                    remote(sems, _half(refs[a].at[2 * px + py], 1 - c, 0), 3 * n + 3 * a + j, (x, y, 1 - c)).wait_recv()
            for a in range(n):
                mine = _half(refs[a].at[2 * x + y], c, 0)
                for k in range(3 * a, 3 * a + 3):
                    remote(sems, mine, k, (x, y, 1 - c)).wait_send()
                    remote(sems, mine, 3 * n + k, (x, y, 1 - c)).wait_send()

        _when(step == (3 * steps) // 4, pass_on)
        _when(step == steps - 1, finish)

    return _Exchange(bufs, [(_sds(b), a) for a, b in enumerate(bufs)], [6 * n, 6 * n], before, after)


def _swap_exchange(grads):
    n = len(grads)

    def copy(refs, outs, sems, a):
        x, y, c, _ = _place()
        return pltpu.make_async_remote_copy(
            src_ref=_half(refs[a], 1 - c, 1), dst_ref=outs[a], send_sem=sems[0].at[a], recv_sem=sems[1].at[a],
            device_id=(x, y, 1 - c), device_id_type=MESH,
        )

    def before(step, steps, refs, outs, sems):
        _when(step == 0, lambda: [copy(refs, outs, sems, a).start() for a in range(n)] and None)

    def after(step, steps, refs, outs, sems):
        _when(step == steps - 1, lambda: [copy(refs, outs, sems, a).wait() for a in range(n)] and None)

    outputs = [(jax.ShapeDtypeStruct((N_CHIPS, g.shape[1] // 2, g.shape[2]), g.dtype), None) for g in grads]
    return _Exchange(grads, outputs, [n, n], before, after)


def _add_sibling_half(place, grads, received, *, name):
    n = len(grads)

    def body(place_ref, *refs):
        k = pl.program_id(0)
        for a in range(n):
            pair = (refs[a][...] + refs[n + a][...]).astype(BF16)
            refs[2 * n + a][...] = pair

            @pl.when(k == place_ref[1])
            def _(a=a, pair=pair):
                refs[3 * n + a][...] = pair

    def block(g):
        return (1, g.shape[1] // 2, g.shape[2])

    return pl.pallas_call(
        body,
        name=name,
        grid_spec=pltpu.PrefetchScalarGridSpec(
            num_scalar_prefetch=1,
            grid=(N_CHIPS,),
            in_specs=[pl.BlockSpec(block(g), lambda k, place_ref: (k, place_ref[0], 0)) for g in grads]
            + [pl.BlockSpec(block(g), lambda k, place_ref: (k, 0, 0)) for g in grads],
            out_specs=[pl.BlockSpec(block(g), lambda k, place_ref: (k, 0, 0)) for g in grads]
            + [pl.BlockSpec(block(g), lambda k, place_ref: (place_ref[1], 0, 0)) for g in grads],
        ),
        out_shape=[jax.ShapeDtypeStruct(r.shape, BF16) for r in received] * 2,
        compiler_params=pltpu.CompilerParams(dimension_semantics=("arbitrary",), vmem_limit_bytes=VMEM_LIMIT),
    )(place, *grads, *received)


def _scatter_exchange(pairs, landing):
    n = len(pairs)

    def copy(refs, sems, a, j, px, py):
        x, y, c, _ = _place()
        return pltpu.make_async_remote_copy(
            src_ref=refs[a].at[2 * px + py], dst_ref=refs[n + a].at[2 * x + y], send_sem=sems[0].at[3 * a + j],
            recv_sem=sems[1].at[3 * a + j], device_id=(px, py, c), device_id_type=MESH,
        )

    def before(step, steps, refs, outs, sems):
        def send():
            for j, (px, py) in enumerate(_place()[3]):
                for a in range(n):
                    copy(refs, sems, a, j, px, py).start()

        _when(step == 0, send)

    def after(step, steps, refs, outs, sems):
        def finish():
            x, y, c, others = _place()
            for j, (px, py) in enumerate(others):
                for a in range(n):
                    landed = refs[n + a].at[2 * px + py]
                    pltpu.make_async_remote_copy(
                        src_ref=landed, dst_ref=landed, send_sem=sems[0].at[3 * a + j], recv_sem=sems[1].at[3 * a + j],
                        device_id=(px, py, c), device_id_type=MESH,
                    ).wait_recv()
            for j, (px, py) in enumerate(others):
                for a in range(n):
                    copy(refs, sems, a, j, px, py).wait_send()

        _when(step == steps - 1, finish)

    return _Exchange(pairs + landing, [(_sds(b), n + a) for a, b in enumerate(landing)], [3 * n, 3 * n], before, after)


SUM_STEPS = 2


def _sum_chip_blocks(place, parts, keep_chip_axis, *, name):
    n = len(parts)

    def body(place_ref, *refs):
        for a in range(n):
            p = refs[a]
            total = (p[0].astype(F32) + p[1].astype(F32)) + (p[2].astype(F32) + p[3].astype(F32))
            if keep_chip_axis[a]:
                refs[n + a][0] = total
            else:
                refs[n + a][...] = total

    def in_spec(p):
        return pl.BlockSpec((N_CHIPS, p.shape[1] // SUM_STEPS, p.shape[2]), lambda i, place_ref: (0, i, 0))

    def out_spec(p, keep):
        rows = p.shape[1] // SUM_STEPS
        if keep:
            return pl.BlockSpec((1, rows, p.shape[2]), lambda i, place_ref: (place_ref[1], place_ref[0] * SUM_STEPS + i, 0))
        return pl.BlockSpec((rows, p.shape[2]), lambda i, place_ref: (place_ref[0] * SUM_STEPS + i, 0))

    def out_shape(p, keep):
        shape = (2 * p.shape[1], p.shape[2])
        return jax.ShapeDtypeStruct((N_CHIPS,) + shape if keep else shape, F32)

    return pl.pallas_call(
        body,
        name=name,
        grid_spec=pltpu.PrefetchScalarGridSpec(
            num_scalar_prefetch=1,
            grid=(SUM_STEPS,),
            in_specs=[in_spec(p) for p in parts],
            out_specs=[out_spec(p, k) for p, k in zip(parts, keep_chip_axis)],
        ),
        out_shape=[out_shape(p, k) for p, k in zip(parts, keep_chip_axis)],
        compiler_params=pltpu.CompilerParams(dimension_semantics=("arbitrary",), vmem_limit_bytes=VMEM_LIMIT),
    )(place, *parts)


def _join_exchange(bufs, keep_chip_axis):
    n = len(bufs)

    def remote(refs, sems, a, cc):
        x, y, c, _ = _place()
        half = _half(refs[a].at[2 * x + y] if keep_chip_axis[a] else refs[a], cc, 0)
        return pltpu.make_async_remote_copy(
            src_ref=half, dst_ref=half, send_sem=sems[0].at[a], recv_sem=sems[1].at[a], device_id=(x, y, 1 - c),
            device_id_type=MESH,
        )

    def before(step, steps, refs, outs, sems):
        def send():
            c = lax.axis_index("c")
            for a in range(n):
                remote(refs, sems, a, c).start()

        _when(step == 0, send)

    def after(step, steps, refs, outs, sems):
        def finish():
            c = lax.axis_index("c")
            for a in range(n):
                remote(refs, sems, a, 1 - c).wait_recv()
            for a in range(n):
                remote(refs, sems, a, c).wait_send()

        _when(step == steps - 1, finish)

    return _Exchange(bufs, [(_sds(b), a) for a, b in enumerate(bufs)], [n, n], before, after)


def _adamw(w, g, m, v):
    m = ADAM_B1 * m + (1.0 - ADAM_B1) * g
    v = ADAM_B2 * v + (1.0 - ADAM_B2) * (g * g)
    m_hat = m / (1.0 - ADAM_B1**ADAM_STEP)
    v_hat = v / (1.0 - ADAM_B2**ADAM_STEP)
    delta = -ADAM_LR * (m_hat / (jnp.sqrt(v_hat) + ADAM_EPS) + ADAM_WD * w)
    return delta, m, v


def _adamw_large(w, m, v, layer_grads, *, name):
    depth, rows, cols = w.shape
    tile = math.gcd(rows, MM_TILE)
    assert tile % 8 == 0

    def body(w_ref, m_ref, v_ref, *refs):
        g_refs, (g_out, d_out, m_out, v_out) = refs[:depth], refs[depth:]
        for l in range(depth):

            @pl.when(pl.program_id(0) == l)
            def _(l=l):
                g = g_refs[l][...]
                delta, m_new, v_new = _adamw(w_ref[0], g, m_ref[0], v_ref[0])
                g_out[0], d_out[0], m_out[0], v_out[0] = g, delta, m_new, v_new

    def stacked():
        return pl.BlockSpec((1, tile, cols), lambda l, i: (l, i, 0))

    def layer_spec(l):
        return pl.BlockSpec((tile, cols), lambda k, i: (jnp.where(k == l, i, 0), 0))

    shape = jax.ShapeDtypeStruct(w.shape, F32)
    return pl.pallas_call(
        body,
        name=name,
        grid=(depth, rows // tile),
        in_specs=[stacked(), stacked(), stacked()] + [layer_spec(l) for l in range(depth)],
        out_specs=[stacked()] * 4,
        out_shape=[shape] * 4,
        compiler_params=pltpu.CompilerParams(dimension_semantics=("arbitrary", "arbitrary"), vmem_limit_bytes=VMEM_LIMIT),
    )(w, m, v, *layer_grads)


def _adamw_small(ws, gs, ms, vs, *, name):
    n = len(ws)

    def body(*refs):
        for a in range(n):
            delta, m_new, v_new = _adamw(refs[a][...], refs[n + a][...], refs[2 * n + a][...], refs[3 * n + a][...])
            refs[4 * n + a][...] = delta
            refs[5 * n + a][...] = m_new
            refs[6 * n + a][...] = v_new

    shapes = [jax.ShapeDtypeStruct(w.shape, F32) for w in ws]
    outs = pl.pallas_call(body, name=name, out_shape=shapes * 3)(*ws, *gs, *ms, *vs)
    return outs[:n], outs[n : 2 * n], outs[2 * n :]


WEIGHT_NAMES = (
    "w_in", "conv_a_w", "sg_ln_g", "sg_ln_b", "sg_w", "sg_b", "pool_w", "pool_scale", "cc_dw_w", "cc_dw_b", "cc_ln_g",
    "cc_ln_b", "cc_pw_w", "w_kv", "w_out", "ln_g", "ln_b",
)
LARGE = ("w_in", "cc_pw_w", "w_kv", "w_out")
TAPS_ROWS = 48


def _unpack_small_grads(small, chip):
    out = {}
    for r, k in enumerate(("sg_ln_g", "sg_ln_b", "pool_scale", "cc_dw_b", "cc_ln_g", "cc_ln_b")):
        out[k] = small[RG_VEC + r]
    out["conv_a_w"] = lax.dynamic_slice_in_dim(small[RG_CONVA : RG_CONVA + CONV_A], chip * HEAD_DIM, HEAD_DIM, axis=1)
    out["cc_dw_w"] = lax.dynamic_slice_in_dim(small[RG_DW : RG_DW + CONV_D], chip * HEAD_DIM, HEAD_DIM, axis=1)
    cat = jnp.concatenate([small[RG_SGW : RG_SGW + CHUNK], small[RG_SGW + CHUNK : RG_SGW + 2 * CHUNK]], axis=1)
    out["sg_w"] = jnp.transpose(cat.reshape(CHUNK, N_SUB, CHUNK), (1, 0, 2))
    out["sg_b"] = small[RG_SGB : RG_SGB + CHUNK].reshape(CHUNK, N_SUB, HEAD_DIM).sum(-1).T
    pool = small[RG_POOL : RG_POOL + D_G]
    out["pool_w"] = jnp.stack(
        [pool[g * HEAD_DIM : (g + 1) * HEAD_DIM, g * HEAD_DIM : (g + 1) * HEAD_DIM] for g in range(N_SUB)]
    )
    out["ln_g"] = small[RG_LN : RG_LN + 4].reshape(D_MODEL)
    out["ln_b"] = small[RG_LN + 4 : RG_LN + 8].reshape(D_MODEL)
    return out


def kernel(x, mem, w_in, conv_a_w, sg_ln_g, sg_ln_b, sg_w, sg_b, pool_w, pool_scale, cc_dw_w, cc_dw_b, cc_ln_g, cc_ln_b, cc_pw_w, w_kv, w_out, ln_g, ln_b, loss_target, m_w_in, m_conv_a_w, m_sg_ln_g, m_sg_ln_b, m_sg_w, m_sg_b, m_pool_w, m_pool_scale, m_cc_dw_w, m_cc_dw_b, m_cc_ln_g, m_cc_ln_b, m_cc_pw_w, m_w_kv, m_w_out, m_ln_g, m_ln_b, v_w_in, v_conv_a_w, v_sg_ln_g, v_sg_ln_b, v_sg_w, v_sg_b, v_pool_w, v_pool_scale, v_cc_dw_w, v_cc_dw_b, v_cc_ln_g, v_cc_ln_b, v_cc_pw_w, v_w_kv, v_w_out, v_ln_g, v_ln_b):
    given = dict(locals())
    weights = {k: given[k] for k in WEIGHT_NAMES}
    chip = 2 * lax.axis_index("x") + lax.axis_index("y")
    place = jnp.stack([lax.axis_index("c"), chip]).astype(jnp.int32)

    x0, mem0 = x[0], mem[0]

    def own_blocks(l):
        taps = jnp.concatenate([conv_a_w[l], cc_dw_w[l]], axis=0)
        taps = jnp.pad(taps, ((0, TAPS_ROWS - taps.shape[0]), (0, 0)))
        return _place_own_block(
            place, [w_in[l], w_out[l], w_kv[l], cc_pw_w[l], taps], [BF16, BF16, BF16, BF16, F32], name=f"place_weights{l}"
        )

    def layer_operands(l, gathered):
        g_in, g_out, g_kv, g_pw, g_taps = gathered
        taps_full = jnp.transpose(g_taps, (1, 0, 2)).reshape(TAPS_ROWS, D_G)
        full = dict(
            w_in=g_in,
            w_out=g_out.reshape(D_MIX, D_MODEL),
            w_kv=g_kv.reshape(D_MODEL, 2 * D_G),
            cc_pw_w=g_pw.reshape(D_G, D_G),
            conv_a_w=taps_full[0:CONV_A],
            cc_dw_w=taps_full[CONV_A : CONV_A + CONV_D],
            **{k: weights[k][l] for k in WEIGHT_NAMES if k not in LARGE + ("conv_a_w", "cc_dw_w")},
        )
        return _prepare_layer(mem0, full, l)

    def layer_grads(l, x_in, p, bwd):
        dproj, _, dwout, dkst, dvst, dpw, small = bwd
        return [
            _input_weight_grad(x_in, dproj, name=f"w_in_grad{l}"),
            dwout.reshape(N_CHIPS, D_MIX // N_CHIPS, D_MODEL),
            _kv_backward(mem0, dkst, dvst, name=f"kv_bwd{l}").reshape(N_CHIPS, D_MODEL // N_CHIPS, 2 * D_G),
            dpw.reshape(N_CHIPS, D_G // N_CHIPS, D_G),
            small.reshape(N_CHIPS, RG_ROWS // N_CHIPS, D_G),
        ]

    n_red = 5
    keep = [False, False, False, False, True]

    def finish_reduce(l, parts):
        halves = _sum_chip_blocks(place, parts, keep, name=f"rs_sum{l}")
        r_in, r_out, r_kv, r_pw, r_small = _run_exchange(_join_exchange(halves, keep), name=f"rs_join{l}")
        (small_all,) = _run_exchange(_gather_exchange([r_small]), name=f"gather_small_grads{l}")
        out = _unpack_small_grads(small_all.reshape(RG_ROWS, D_G), chip)
        out.update(w_in=r_in, w_out=r_out, w_kv=r_kv, cc_pw_w=r_pw)
        return out

    blocks0, blocks1 = own_blocks(0), own_blocks(1)
    p0 = layer_operands(0, _run_exchange(_gather_exchange(blocks0), name="gather_weights0"))
    fwd0, gathered1 = _forward(0, x0, p0, None, exchange=_gather_exchange(blocks1))
    p1 = layer_operands(1, gathered1)
    x1 = fwd0[3]
    fwd1, _ = _forward(1, x1, p1, loss_target[0])
    loss = lax.psum(fwd1[4][0, 0], ("x", "y", "c"))

    bwd1, _ = _backward(1, fwd1[3], dict(proj=fwd1[0], y=fwd1[1], z=fwd1[2]), p1)
    grads1 = layer_grads(1, x1, p1, bwd1)
    (dx1,), received1 = _input_grad(bwd1[0], bwd1[1], p1["win"], name="input_grad1", exchange=_swap_exchange(grads1))
    pairs1 = _add_sibling_half(place, grads1, received1, name="rs_pair1")
    bwd0, parts1 = _backward(
        0, dx1, dict(proj=fwd0[0], y=fwd0[1], z=fwd0[2]), p0, exchange=_scatter_exchange(pairs1[:n_red], pairs1[n_red:])
    )
    reduced1 = finish_reduce(1, parts1)
    grads0 = layer_grads(0, x0, p0, bwd0)
    received0 = _run_exchange(_swap_exchange(grads0), name="rs_swap0")
    pairs0 = _add_sibling_half(place, grads0, received0, name="rs_pair0")
    (grad_x,), parts0 = _input_grad(
        bwd0[0], bwd0[1], p0["win"], name="input_grad0", exchange=_scatter_exchange(pairs0[:n_red], pairs0[n_red:])
    )
    reduced = [finish_reduce(0, parts0), reduced1]

    grad, delta, new_m, new_v = {}, {}, {}, {}
    for k in LARGE:
        w3 = weights[k]
        grad[k], delta[k], new_m[k], new_v[k] = _adamw_large(
            w3, given["m_" + k], given["v_" + k], [reduced[l][k] for l in range(DEPTH)], name=f"adamw_{k}"
        )
    small_names = [k for k in WEIGHT_NAMES if k not in LARGE]
    for k in small_names:
        grad[k] = jnp.stack([reduced[l][k] for l in range(DEPTH)])
    d_s, m_s, v_s = _adamw_small(
        [weights[k] for k in small_names],
        [grad[k] for k in small_names],
        [given["m_" + k] for k in small_names],
        [given["v_" + k] for k in small_names],
        name="adamw_small",
    )
    for a, k in enumerate(small_names):
        delta[k], new_m[k], new_v[k] = d_s[a], m_s[a], v_s[a]

    return (
        loss,
        grad_x[None],
        *[grad[k] for k in WEIGHT_NAMES],
        *[delta[k] for k in WEIGHT_NAMES],
        *[new_m[k] for k in WEIGHT_NAMES],
        *[new_v[k] for k in WEIGHT_NAMES],
    )
```

```python
import functools
import math

import jax
import jax.numpy as jnp
from jax import lax
from jax.experimental import pallas as pl
from jax.experimental.pallas import tpu as pltpu

F32 = jnp.float32
BF16 = jnp.bfloat16

D_MODEL = 1024
DEPTH = 2
D_G = 256
D_MIX = 5 * D_G
D_IN = 9 * D_G + D_MIX
N_SUB = 4
HEAD_DIM = 64
CONV_A = 3
CONV_D = 31
CHUNK = 128
MEM_LEN = 256
N_CHIPS = 4
W_IN_SHARD = D_IN // N_CHIPS
LN_EPS = 1e-5
ALPHA = (2.0 * DEPTH) ** 0.25
ATT_SCALE = 1.0 / math.sqrt(HEAD_DIM)
GELU_C = math.sqrt(2.0 / math.pi)
GELU_A = 0.044715

ADAM_LR = 0.001
ADAM_B1 = 0.9
ADAM_B2 = 0.999
ADAM_EPS = 1e-08
ADAM_WD = 0.01
ADAM_STEP = 10

C_XA, C_BA, C_CA, C_U, C_V, C_XC, C_DA, C_DG, C_Q, C_GATE = (D_G * i for i in range(10))

HALO_A = 8
HALO_C = 16
HALO_D = 32

RW_VEC = 0
RW_CONVA = 16
RW_DW = 24
RW_SGB = 56
RW_ROWS = RW_SGB + CHUNK

RG_VEC = 0
RG_CONVA = 16
RG_DW = 24
RG_SGW = 56
RG_SGB = RG_SGW + 2 * CHUNK
RG_POOL = RG_SGB + CHUNK
RG_LN = RG_POOL + D_G
RG_ROWS = 768

VMEM_LIMIT = 56 * 1024 * 1024
SEQ_TILE = 256
MM_TILE = 1024
ADAM_TILE = 512

MESH = pl.DeviceIdType.MESH
ANY = pl.BlockSpec(memory_space=pl.ANY)
NT = (((1,), (1,)), ((), ()))
TN = (((0,), (0,)), ((), ()))


def _dot(a, b):
    return jnp.dot(a, b, preferred_element_type=F32)


def _dot_nt(a, b):
    return lax.dot_general(a, b, NT, preferred_element_type=F32)


def _dot_tn(a, b):
    return lax.dot_general(a, b, TN, preferred_element_type=F32)


def _full(shape):
    zeros = (0,) * len(shape)
    return pl.BlockSpec(shape, lambda *_: zeros)


class _Exchange:
    def __init__(self, operands, outputs, sem_counts, before, after):
        self.operands, self.outputs, self.sem_counts, self.before, self.after = operands, outputs, sem_counts, before, after

    def specs(self, first_input, first_output):
        aliases = {first_input + src: first_output + j for j, (_, src) in enumerate(self.outputs) if src is not None}
        return (
            [ANY] * len(self.operands),
            [ANY] * len(self.outputs),
            [sds for sds, _ in self.outputs],
            [pltpu.SemaphoreType.DMA((k,)) for k in self.sem_counts],
            aliases,
        )

    def split(self, ins, outs):
        refs = list(ins)
        for j, (_, src) in enumerate(self.outputs):
            if src is not None:
                refs[src] = outs[j]
        return refs


def _when(cond, fn):
    if isinstance(cond, bool):
        if cond:
            fn()
    else:
        pl.when(cond)(fn)


def _run_exchange(exchange, *, name):
    n_in, n_out = len(exchange.operands), len(exchange.outputs)
    in_specs, out_specs, out_shape, sems, aliases = exchange.specs(0, 0)

    def body(*refs):
        ins, outs, sem_refs = refs[:n_in], refs[n_in : n_in + n_out], refs[n_in + n_out :]
        refs = exchange.split(ins, outs)
        exchange.before(0, 1, refs, outs, sem_refs)
        exchange.after(0, 1, refs, outs, sem_refs)

    return pl.pallas_call(
        body, name=name, in_specs=in_specs, out_specs=out_specs, out_shape=out_shape, scratch_shapes=sems,
        input_output_aliases=aliases,
    )(*exchange.operands)


def _gridded_call(body, *, name, steps, in_specs, out_specs, out_shape, scratch_shapes, operands, exchange=None):
    params = pltpu.CompilerParams(dimension_semantics=("arbitrary",), vmem_limit_bytes=VMEM_LIMIT)
    if exchange is None:
        outs = pl.pallas_call(
            body, name=name, grid=(steps,), in_specs=in_specs, out_specs=out_specs, out_shape=out_shape,
            scratch_shapes=scratch_shapes, compiler_params=params,
        )(*operands)
        return list(outs), []
    n_in, n_out, n_scr = len(in_specs), len(out_specs), len(scratch_shapes)
    x_in, x_out = len(exchange.operands), len(exchange.outputs)
    ex_in_specs, ex_out_specs, ex_out_shape, ex_sems, aliases = exchange.specs(n_in, n_out)

    def full(*refs):
        own_in, refs = refs[:n_in], refs[n_in:]
        ex_in, refs = refs[:x_in], refs[x_in:]
        own_out, refs = refs[:n_out], refs[n_out:]
        ex_out, refs = refs[:x_out], refs[x_out:]
        own_scr, sem_refs = refs[:n_scr], refs[n_scr:]
        ex_refs = exchange.split(ex_in, ex_out)
        step = pl.program_id(0)
        exchange.before(step, steps, ex_refs, ex_out, sem_refs)
        body(*own_in, *own_out, *own_scr)
        exchange.after(step, steps, ex_refs, ex_out, sem_refs)

    outs = pl.pallas_call(
        full, name=name, grid=(steps,), in_specs=in_specs + ex_in_specs, out_specs=out_specs + ex_out_specs,
        out_shape=out_shape + ex_out_shape, scratch_shapes=scratch_shapes + ex_sems, input_output_aliases=aliases,
        compiler_params=params,
    )(*operands, *exchange.operands)
    return list(outs[:n_out]), list(outs[n_out:])


def _sigmoid(x):
    return 1.0 / (1.0 + jnp.exp(-x))


def _gelu(x):
    t = jnp.tanh(GELU_C * (x + GELU_A * x * x * x))
    return 0.5 * x * (1.0 + t), t


def _gelu_grad(x, t):
    return 0.5 * (1.0 + t) + 0.5 * x * (1.0 - t * t) * (GELU_C * (1.0 + 3.0 * GELU_A * x * x))


def _normalize(v):
    mu = jnp.mean(v, axis=-1, keepdims=True)
    d = v - mu
    var = jnp.mean(d * d, axis=-1, keepdims=True)
    rstd = lax.rsqrt(var + LN_EPS)
    return d * rstd, rstd


def _normalize_grad(dhat, hat, rstd):
    m1 = jnp.mean(dhat, axis=-1, keepdims=True)
    m2 = jnp.mean(dhat * hat, axis=-1, keepdims=True)
    return rstd * (dhat - m1 - hat * m2)


def _lane(width=D_G):
    return lax.broadcasted_iota(jnp.int32, (1, width), 1)


def _head_masks():
    head = _lane() // HEAD_DIM
    return [(head == h).astype(F32) for h in range(N_SUB)]


def _stack_heads(v, masks):
    return jnp.concatenate([v * m for m in masks], axis=0)


def _tril_mask_cat():
    t = lax.broadcasted_iota(jnp.int32, (CHUNK, N_SUB * CHUNK), 0)
    s = lax.broadcasted_iota(jnp.int32, (CHUNK, N_SUB * CHUNK), 1) % CHUNK
    return s <= t


def _triu_mask_cat():
    s = lax.broadcasted_iota(jnp.int32, (CHUNK, N_SUB * CHUNK), 0)
    t = lax.broadcasted_iota(jnp.int32, (CHUNK, N_SUB * CHUNK), 1) % CHUNK
    return t >= s


def _pool_select(a2, a4, a8, a16):
    lane = _lane()
    return jnp.where(lane < 64, a2, jnp.where(lane < 128, a4, jnp.where(lane < 192, a8, a16)))


def _pool_count(row0, rows):
    t = row0 + lax.broadcasted_iota(jnp.int32, (rows, D_G), 0)
    lane = lax.broadcasted_iota(jnp.int32, (rows, D_G), 1)
    win = jnp.where(lane < 64, 2, jnp.where(lane < 128, 4, jnp.where(lane < 192, 8, 16)))
    return jnp.minimum(t + 1, win).astype(F32)


def _trailing_window_sum(halo, cur):
    e = jnp.concatenate([halo, cur], axis=0)
    s2 = e + pltpu.roll(e, 1, 0)
    s4 = s2 + pltpu.roll(s2, 2, 0)
    s8 = s4 + pltpu.roll(s4, 4, 0)
    s16 = s8 + pltpu.roll(s8, 8, 0)
    return _pool_select(s2, s4, s8, s16)[HALO_C:]


def _leading_window_sum(cur, halo):
    e = jnp.concatenate([cur, halo], axis=0)
    n = e.shape[0]
    s2 = e + pltpu.roll(e, n - 1, 0)
    s4 = s2 + pltpu.roll(s2, n - 2, 0)
    s8 = s4 + pltpu.roll(s4, n - 4, 0)
    s16 = s8 + pltpu.roll(s8, n - 8, 0)
    return _pool_select(s2, s4, s8, s16)[: cur.shape[0]]


def _softmax_blocks(sc):
    out = []
    for h in range(N_SUB):
        s = sc[:, h * MEM_LEN : (h + 1) * MEM_LEN]
        e = jnp.exp(s - jnp.max(s, axis=-1, keepdims=True))
        out.append(e / jnp.sum(e, axis=-1, keepdims=True))
    return jnp.concatenate(out, axis=-1)


SHIFTS = 8


def _fill_shifts(buf):
    n = buf.shape[1] - SHIFTS
    for r in range(1, SHIFTS):
        buf[r, 0:n, :] = buf[0, r : r + n, :]


def _shifted(buf, off, rows):
    r = off % SHIFTS
    return buf[r, off - r : off - r + rows, :]


def _sgu_mix(vn, wcat_b, sgb, masks):
    vbd = _stack_heads(vn, masks).astype(BF16)
    return _dot(wcat_b, vbd) + sgb, vbd


def _layer_forward(x, win, kst, vst, wout, sw, wcat, poolw, pww, ln, tgt, *, name, exchange=None):
    seq = x.shape[0]
    tile = min(SEQ_TILE, seq)
    n_tiles = seq // tile
    last = tgt is not None

    def body(*refs):
        x_ref, win_ref, kst_ref, vst_ref, wout_ref, sw_ref, wcat_ref, pool_ref, pw_ref, ln_ref = refs[:10]
        refs = refs[10:]
        if last:
            tgt_ref, refs = refs[0], refs[1:]
        proj_ref, y_ref, z_ref, out_ref, cvd_ref = refs[:5]
        refs = refs[5:]
        if last:
            loss_ref, refs = refs[0], refs[1:]
        pbuf, xchalo, gbuf = refs
        i = pl.program_id(0)

        @pl.when(i == 0)
        def _():
            pbuf[0:HALO_A, :] = jnp.zeros((HALO_A, D_G), F32)
            xchalo[...] = jnp.zeros((HALO_C, D_G), F32)
            gbuf[0, 0:HALO_D, :] = jnp.zeros((HALO_D, D_G), F32)
            if last:
                loss_ref[...] = jnp.zeros((8, 128), F32)

        xt = x_ref[...]
        xb = xt.astype(BF16)
        for k in range(N_CHIPS):
            proj_ref[:, k * W_IN_SHARD : (k + 1) * W_IN_SHARD] = _dot(xb, win_ref[k])
        masks = _head_masks()

        pbuf[HALO_A : HALO_A + tile, :] = proj_ref[:, C_CA : C_CA + D_G] * proj_ref[:, C_XA : C_XA + D_G]
        cv = jnp.zeros((tile, D_G), F32)
        for k in range(CONV_A):
            off = HALO_A - (CONV_A - 1) + k
            cv = cv + sw_ref[RW_CONVA + k : RW_CONVA + k + 1, :] * pbuf[off : off + tile, :]
        y_ref[:, 0:D_G] = proj_ref[:, C_BA : C_BA + D_G] * cv
        pbuf[0:HALO_A, :] = pbuf[tile : tile + HALO_A, :]

        ua, _ = _gelu(proj_ref[:, C_U : C_U + D_G])
        vg, _ = _gelu(proj_ref[:, C_V : C_V + D_G])
        vhat, _ = _normalize(vg)
        vn = vhat * sw_ref[RW_VEC : RW_VEC + 1, :] + sw_ref[RW_VEC + 1 : RW_VEC + 2, :]
        wcat_b = jnp.where(_tril_mask_cat(), wcat_ref[...], 0.0).astype(BF16)
        sgb = sw_ref[RW_SGB : RW_SGB + CHUNK, :]
        for j in range(tile // CHUNK):
            rows = slice(j * CHUNK, (j + 1) * CHUNK)
            mixed, _ = _sgu_mix(vn[rows], wcat_b, sgb, masks)
            y_ref[rows, D_G : 2 * D_G] = ua[rows] * mixed

        xc = proj_ref[:, C_XC : C_XC + D_G]
        wsum = _trailing_window_sum(xchalo[...], xc)
        pm = wsum / _pool_count(i * tile, tile) - xc
        y_ref[:, 2 * D_G : 3 * D_G] = _dot(pm.astype(BF16), pool_ref[...]) * sw_ref[RW_VEC + 2 : RW_VEC + 3, :]
        xchalo[...] = xc[tile - HALO_C :, :]

        gbuf[0, HALO_D : HALO_D + tile, :] = proj_ref[:, C_DA : C_DA + D_G] * _sigmoid(proj_ref[:, C_DG : C_DG + D_G])
        _fill_shifts(gbuf)
        cvd = jnp.zeros((tile, D_G), F32) + sw_ref[RW_VEC + 3 : RW_VEC + 4, :]
        for k in range(CONV_D):
            off = HALO_D - (CONV_D - 1) + k
            cvd = cvd + sw_ref[RW_DW + k : RW_DW + k + 1, :] * _shifted(gbuf, off, tile)
        cvd_ref[...] = cvd
        nhat, _ = _normalize(cvd)
        nrm = nhat * sw_ref[RW_VEC + 4 : RW_VEC + 5, :] + sw_ref[RW_VEC + 5 : RW_VEC + 6, :]
        y_ref[:, 3 * D_G : 4 * D_G] = _dot((nrm * _sigmoid(nrm)).astype(BF16), pw_ref[...])
        gbuf[0, 0:HALO_D, :] = gbuf[0, tile : tile + HALO_D, :]

        qb = proj_ref[:, C_Q : C_Q + D_G].astype(BF16)
        p_all = _softmax_blocks(_dot_nt(qb, kst_ref[...]) * ATT_SCALE)
        y_ref[:, 4 * D_G : 5 * D_G] = _dot(p_all.astype(BF16), vst_ref[...])

        gate = proj_ref[:, C_GATE:]
        hid = y_ref[...] * (gate * _sigmoid(gate))
        z = ALPHA * xt + _dot(hid.astype(BF16), wout_ref[...])
        z_ref[...] = z
        zhat, _ = _normalize(z)
        xn = zhat * ln_ref[0:1, :] + ln_ref[1:2, :]
        if last:
            err = xn - tgt_ref[...]
            out_ref[...] = err * (1.0 / D_MODEL)
            loss_ref[...] += jnp.sum(err * err) * (0.5 / D_MODEL)
        else:
            out_ref[...] = xn

    def rows(width):
        return pl.BlockSpec((tile, width), lambda i: (i, 0))

    operands = [x, win, kst, vst, wout, sw, wcat, poolw, pww, ln]
    in_specs = [rows(D_MODEL)] + [_full(a.shape) for a in operands[1:]]
    out_shape = [
        jax.ShapeDtypeStruct((seq, D_IN), F32),
        jax.ShapeDtypeStruct((seq, D_MIX), F32),
        jax.ShapeDtypeStruct((seq, D_MODEL), F32),
        jax.ShapeDtypeStruct((seq, D_MODEL), F32),
        jax.ShapeDtypeStruct((seq, D_G), F32),
    ]
    out_specs = [rows(D_IN), rows(D_MIX), rows(D_MODEL), rows(D_MODEL), rows(D_G)]
    if last:
        operands.append(tgt)
        in_specs.append(rows(D_MODEL))
        out_shape.append(jax.ShapeDtypeStruct((8, 128), F32))
        out_specs.append(_full((8, 128)))
    return _gridded_call(
        body,
        name=name,
        steps=n_tiles,
        in_specs=in_specs,
        out_specs=out_specs,
        out_shape=out_shape,
        scratch_shapes=[
            pltpu.VMEM((HALO_A + tile, D_G), F32),
            pltpu.VMEM((HALO_C, D_G), F32),
            pltpu.VMEM((SHIFTS, HALO_D + tile, D_G), F32),
        ],
        operands=operands,
        exchange=exchange,
    )


def _layer_backward(dxn, z, proj, y, cvd, kst, vst, wout, sw, wcat, wcat_t, poolw, pww, ln, *, name, exchange=None):
    seq = dxn.shape[0]
    tile = min(SEQ_TILE, seq)
    n_tiles = seq // tile
    halo_blocks = tile // HALO_D

    def body(
        dxn_ref, z_ref, proj_ref, halo_ref, y_ref, cvd_ref, kst_ref, vst_ref, wout_ref, sw_ref, wcat_ref, wcat_t_ref,
        pool_ref, pw_ref, ln_ref, dproj_ref, dz_ref, dwout_ref, dkst_ref, dvst_ref, dpw_ref, sg_ref,
        pbuf, dcvbuf, rhalo, gbuf, dgbuf,
    ):
        i = pl.program_id(0)
        ti = n_tiles - 1 - i

        @pl.when(i == 0)
        def _():
            dwout_ref[...] = jnp.zeros(dwout_ref.shape, F32)
            dkst_ref[...] = jnp.zeros(dkst_ref.shape, F32)
            dvst_ref[...] = jnp.zeros(dvst_ref.shape, F32)
            dpw_ref[...] = jnp.zeros(dpw_ref.shape, F32)
            sg_ref[...] = jnp.zeros(sg_ref.shape, F32)
            dcvbuf[tile : tile + HALO_A, :] = jnp.zeros((HALO_A, D_G), F32)
            rhalo[...] = jnp.zeros((HALO_C, D_G), F32)
            dgbuf[0, tile : tile + HALO_D, :] = jnp.zeros((HALO_D, D_G), F32)

        def acc_row(row, val):
            sg_ref[row : row + 1, :] += jnp.sum(val, axis=0, keepdims=True)

        masks = _head_masks()
        has_past = (ti > 0).astype(F32)

        zhat, zrstd = _normalize(z_ref[...])
        dxn_t = dxn_ref[...]
        dlg = jnp.sum(dxn_t * zhat, axis=0, keepdims=True)
        dlb = jnp.sum(dxn_t, axis=0, keepdims=True)
        for j in range(D_MODEL // D_G):
            sg_ref[RG_LN + j : RG_LN + j + 1, :] += dlg[:, j * D_G : (j + 1) * D_G]
            sg_ref[RG_LN + 4 + j : RG_LN + 5 + j, :] += dlb[:, j * D_G : (j + 1) * D_G]
        dz = _normalize_grad(dxn_t * ln_ref[0:1, :], zhat, zrstd)
        dz_ref[...] = dz
        dzb = dz.astype(BF16)

        gate = proj_ref[:, C_GATE:]
        sgm = _sigmoid(gate)
        silu = gate * sgm
        yc = y_ref[...]
        dwout_ref[...] += _dot_tn((yc * silu).astype(BF16), dzb)
        dh = _dot_nt(dzb, wout_ref[...])
        dproj_ref[:, C_GATE:] = (dh * yc * (sgm * (1.0 + gate * (1.0 - sgm)))).astype(BF16)
        dy = dh * silu

        dya = dy[:, 0:D_G]
        xa = proj_ref[:, C_XA : C_XA + D_G]
        ba = proj_ref[:, C_BA : C_BA + D_G]
        ca = proj_ref[:, C_CA : C_CA + D_G]
        past = slice(HALO_D - HALO_A, HALO_D)
        pbuf[0:HALO_A, :] = halo_ref[past, C_CA : C_CA + D_G] * halo_ref[past, C_XA : C_XA + D_G] * has_past
        pbuf[HALO_A : HALO_A + tile, :] = ca * xa
        cv = jnp.zeros((tile, D_G), F32)
        for k in range(CONV_A):
            off = HALO_A - (CONV_A - 1) + k
            cv = cv + sw_ref[RW_CONVA + k : RW_CONVA + k + 1, :] * pbuf[off : off + tile, :]
        dproj_ref[:, C_BA : C_BA + D_G] = (dya * cv).astype(BF16)
        dcv = dya * ba
        dcvbuf[0:tile, :] = dcv
        dp = jnp.zeros((tile, D_G), F32)
        for k in range(CONV_A):
            off = HALO_A - (CONV_A - 1) + k
            acc_row(RG_CONVA + k, dcv * pbuf[off : off + tile, :])
            back = CONV_A - 1 - k
            dp = dp + sw_ref[RW_CONVA + k : RW_CONVA + k + 1, :] * dcvbuf[back : back + tile, :]
        dproj_ref[:, C_CA : C_CA + D_G] = (dp * xa).astype(BF16)
        dproj_ref[:, C_XA : C_XA + D_G] = (dp * ca).astype(BF16)
        dcvbuf[tile : tile + HALO_A, :] = dcvbuf[0:HALO_A, :]

        dyb = dy[:, D_G : 2 * D_G]
        u = proj_ref[:, C_U : C_U + D_G]
        v = proj_ref[:, C_V : C_V + D_G]
        ua, ut = _gelu(u)
        vg, vt = _gelu(v)
        vhat, vrstd = _normalize(vg)
        sg_g = sw_ref[RW_VEC : RW_VEC + 1, :]
        vn = vhat * sg_g + sw_ref[RW_VEC + 1 : RW_VEC + 2, :]
        tril = _tril_mask_cat()
        wcat_b = jnp.where(tril, wcat_ref[...], 0.0).astype(BF16)
        wcat_tb = jnp.where(_triu_mask_cat(), wcat_t_ref[...], 0.0).astype(BF16)
        sgb = sw_ref[RW_SGB : RW_SGB + CHUNK, :]
        dmixed = dyb * ua
        dvn_parts = []
        du_parts = []
        dwcat = jnp.zeros((CHUNK, N_SUB * CHUNK), F32)
        dsgb = jnp.zeros((CHUNK, D_G), F32)
        for j in range(tile // CHUNK):
            rows = slice(j * CHUNK, (j + 1) * CHUNK)
            mixed, vbd = _sgu_mix(vn[rows], wcat_b, sgb, masks)
            du_parts.append(dyb[rows] * mixed)
            dmx = dmixed[rows]
            dsgb = dsgb + dmx
            dwcat = dwcat + _dot_nt(dmx.astype(BF16), vbd)
            dvn_parts.append(_dot(wcat_tb, _stack_heads(dmx, masks).astype(BF16)))
        dwcat = jnp.where(tril, dwcat, 0.0)
        sg_ref[RG_SGW : RG_SGW + CHUNK, :] += dwcat[:, 0:D_G]
        sg_ref[RG_SGW + CHUNK : RG_SGW + 2 * CHUNK, :] += dwcat[:, D_G:]
        sg_ref[RG_SGB : RG_SGB + CHUNK, :] += dsgb
        dvn = jnp.concatenate(dvn_parts, axis=0)
        du_act = jnp.concatenate(du_parts, axis=0)
        acc_row(RG_VEC, dvn * vhat)
        acc_row(RG_VEC + 1, dvn)
        dvg = _normalize_grad(dvn * sg_g, vhat, vrstd)
        dproj_ref[:, C_U : C_U + D_G] = (du_act * _gelu_grad(u, ut)).astype(BF16)
        dproj_ref[:, C_V : C_V + D_G] = (dvg * _gelu_grad(v, vt)).astype(BF16)

        dyc = dy[:, 2 * D_G : 3 * D_G]
        xc = proj_ref[:, C_XC : C_XC + D_G]
        xc_past = halo_ref[HALO_D - HALO_C : HALO_D, C_XC : C_XC + D_G] * has_past
        cnt = _pool_count(ti * tile, tile)
        pm = _trailing_window_sum(xc_past, xc) / cnt - xc
        pmb = pm.astype(BF16)
        pool_b = pool_ref[...]
        scale = sw_ref[RW_VEC + 2 : RW_VEC + 3, :]
        acc_row(RG_VEC + 2, dyc * _dot(pmb, pool_b))
        dpre = (dyc * scale).astype(BF16)
        sg_ref[RG_POOL : RG_POOL + D_G, :] += _dot_tn(pmb, dpre)
        dpm = _dot_nt(dpre, pool_b)
        r = dpm / cnt
        dproj_ref[:, C_XC : C_XC + D_G] = (_leading_window_sum(r, rhalo[...]) - dpm).astype(BF16)
        rhalo[...] = r[0:HALO_C, :]

        dyd = dy[:, 3 * D_G : 4 * D_G]
        da = proj_ref[:, C_DA : C_DA + D_G]
        sgd = _sigmoid(proj_ref[:, C_DG : C_DG + D_G])
        gbuf[0, 0:HALO_D, :] = halo_ref[:, C_DA : C_DA + D_G] * _sigmoid(halo_ref[:, C_DG : C_DG + D_G]) * has_past
        gbuf[0, HALO_D : HALO_D + tile, :] = da * sgd
        _fill_shifts(gbuf)
        nhat, nrstd = _normalize(cvd_ref[...])
        cc_g = sw_ref[RW_VEC + 4 : RW_VEC + 5, :]
        nrm = nhat * cc_g + sw_ref[RW_VEC + 5 : RW_VEC + 6, :]
        sgn = _sigmoid(nrm)
        dydb = dyd.astype(BF16)
        dpw_ref[...] += _dot_tn((nrm * sgn).astype(BF16), dydb)
        dn = _dot_nt(dydb, pw_ref[...]) * (sgn * (1.0 + nrm * (1.0 - sgn)))
        acc_row(RG_VEC + 4, dn * nhat)
        acc_row(RG_VEC + 5, dn)
        dcvd = _normalize_grad(dn * cc_g, nhat, nrstd)
        acc_row(RG_VEC + 3, dcvd)
        dgbuf[0, 0:tile, :] = dcvd
        _fill_shifts(dgbuf)
        dg = jnp.zeros((tile, D_G), F32)
        for k in range(CONV_D):
            off = HALO_D - (CONV_D - 1) + k
            acc_row(RG_DW + k, dcvd * _shifted(gbuf, off, tile))
            back = CONV_D - 1 - k
            dg = dg + sw_ref[RW_DW + k : RW_DW + k + 1, :] * _shifted(dgbuf, back, tile)
        dproj_ref[:, C_DA : C_DA + D_G] = (dg * sgd).astype(BF16)
        dproj_ref[:, C_DG : C_DG + D_G] = (dg * da * sgd * (1.0 - sgd)).astype(BF16)
        dgbuf[0, tile : tile + HALO_D, :] = dgbuf[0, 0:HALO_D, :]

        dyeb = dy[:, 4 * D_G : 5 * D_G].astype(BF16)
        qb = proj_ref[:, C_Q : C_Q + D_G].astype(BF16)
        kst_b = kst_ref[...]
        p_all = _softmax_blocks(_dot_nt(qb, kst_b) * ATT_SCALE)
        dvst_ref[...] += _dot_tn(p_all.astype(BF16), dyeb)
        dp_all = _dot_nt(dyeb, vst_ref[...])
        ds = []
        for h in range(N_SUB):
            blk = slice(h * MEM_LEN, (h + 1) * MEM_LEN)
            p, dpb = p_all[:, blk], dp_all[:, blk]
            ds.append(p * (dpb - jnp.sum(dpb * p, axis=-1, keepdims=True)))
        dsb = (jnp.concatenate(ds, axis=-1) * ATT_SCALE).astype(BF16)
        dproj_ref[:, C_Q : C_Q + D_G] = _dot(dsb, kst_b).astype(BF16)
        dkst_ref[...] += _dot_tn(dsb, qb)

    def rows(width):
        return pl.BlockSpec((tile, width), lambda i: (n_tiles - 1 - i, 0))

    halo_spec = pl.BlockSpec((HALO_D, D_IN), lambda i: (jnp.maximum((n_tiles - 1 - i) * halo_blocks - 1, 0), 0))
    weights = [kst, vst, wout, sw, wcat, wcat_t, poolw, pww, ln]
    acc_shapes = [(D_MIX, D_MODEL), (N_SUB * MEM_LEN, D_G), (N_SUB * MEM_LEN, D_G), (D_G, D_G), (RG_ROWS, D_G)]
    return _gridded_call(
        body,
        name=name,
        steps=n_tiles,
        in_specs=[rows(D_MODEL), rows(D_MODEL), rows(D_IN), halo_spec, rows(D_MIX), rows(D_G)]
        + [_full(a.shape) for a in weights],
        out_specs=[rows(D_IN), rows(D_MODEL)] + [_full(s) for s in acc_shapes],
        out_shape=[jax.ShapeDtypeStruct((seq, D_IN), BF16), jax.ShapeDtypeStruct((seq, D_MODEL), F32)]
        + [jax.ShapeDtypeStruct(s, F32) for s in acc_shapes],
        scratch_shapes=[
            pltpu.VMEM((HALO_A + tile, D_G), F32),
            pltpu.VMEM((tile + HALO_A, D_G), F32),
            pltpu.VMEM((HALO_C, D_G), F32),
            pltpu.VMEM((SHIFTS, HALO_D + tile, D_G), F32),
            pltpu.VMEM((SHIFTS, tile + HALO_D, D_G), F32),
        ],
        operands=[dxn, z, proj, proj, y, cvd, *weights],
        exchange=exchange,
    )


def _kv_forward(mem, wkv, *, name):
    def body(mem_ref, wkv_ref, kst_ref, vst_ref):
        kv = _dot(mem_ref[...].astype(BF16), wkv_ref[...])
        masks = _head_masks()
        kst_ref[...] = _stack_heads(kv[:, 0:D_G], masks).astype(BF16)
        vst_ref[...] = _stack_heads(kv[:, D_G:], masks).astype(BF16)

    shape = jax.ShapeDtypeStruct((N_SUB * MEM_LEN, D_G), BF16)
    return pl.pallas_call(body, name=name, out_shape=[shape, shape])(mem, wkv)


def _kv_backward(mem, dkst, dvst, *, name):
    def body(mem_ref, dkst_ref, dvst_ref, dwkv_ref):
        masks = _head_masks()
        memb = mem_ref[...].astype(BF16)
        for col, ref in ((0, dkst_ref), (D_G, dvst_ref)):
            d = jnp.zeros((MEM_LEN, D_G), F32)
            for h in range(N_SUB):
                d = d + ref[h * MEM_LEN : (h + 1) * MEM_LEN, :] * masks[h]
            dwkv_ref[:, col : col + D_G] = _dot_tn(memb, d.astype(BF16))

    return pl.pallas_call(body, name=name, out_shape=jax.ShapeDtypeStruct((D_MODEL, 2 * D_G), F32))(mem, dkst, dvst)


def _input_grad(dproj, dz, win, *, name, exchange=None):
    seq = dproj.shape[0]
    tile = min(MM_TILE, seq)

    def body(dproj_ref, dz_ref, win_ref, dx_ref):
        acc = ALPHA * dz_ref[...]
        for k in range(N_CHIPS):
            acc = acc + _dot_nt(dproj_ref[:, k * W_IN_SHARD : (k + 1) * W_IN_SHARD], win_ref[k])
        dx_ref[...] = acc

    return _gridded_call(
        body,
        name=name,
        steps=seq // tile,
        in_specs=[
            pl.BlockSpec((tile, D_IN), lambda i: (i, 0)),
            pl.BlockSpec((tile, D_MODEL), lambda i: (i, 0)),
            _full(win.shape),
        ],
        out_specs=[pl.BlockSpec((tile, D_MODEL), lambda i: (i, 0))],
        out_shape=[jax.ShapeDtypeStruct((seq, D_MODEL), F32)],
        scratch_shapes=[],
        operands=[dproj, dz, win],
        exchange=exchange,
    )


def _input_weight_grad(x, dproj, *, name):
    seq = x.shape[0]
    tile = min(MM_TILE, seq)

    def body(x_ref, dproj_ref, dwin_ref):
        @pl.when(pl.program_id(1) == 0)
        def _():
            dwin_ref[...] = jnp.zeros(dwin_ref.shape, F32)

        dwin_ref[0] += _dot_tn(x_ref[...].astype(BF16), dproj_ref[...])

    return pl.pallas_call(
        body,
        name=name,
        grid=(N_CHIPS, seq // tile),
        in_specs=[
            pl.BlockSpec((tile, D_MODEL), lambda k, i: (i, 0)),
            pl.BlockSpec((tile, W_IN_SHARD), lambda k, i: (i, k)),
        ],
        out_specs=pl.BlockSpec((1, D_MODEL, W_IN_SHARD), lambda k, i: (k, 0, 0)),
        out_shape=jax.ShapeDtypeStruct((N_CHIPS, D_MODEL, W_IN_SHARD), F32),
        compiler_params=pltpu.CompilerParams(dimension_semantics=("arbitrary", "arbitrary"), vmem_limit_bytes=VMEM_LIMIT),
    )(x, dproj)


def _expand_sgb(sg_b):
    return jnp.repeat(sg_b.T, HEAD_DIM, axis=1)


def _pack_small_weights(sg_ln_g, sg_ln_b, pool_scale, cc_dw_b, cc_ln_g, cc_ln_b, conv_a_w, cc_dw_w, sg_b):
    vec = jnp.stack([sg_ln_g, sg_ln_b, pool_scale, cc_dw_b, cc_ln_g, cc_ln_b])
    return jnp.concatenate(
        [
            jnp.pad(vec, ((0, RW_CONVA - RW_VEC - 6), (0, 0))),
            jnp.pad(conv_a_w, ((0, RW_DW - RW_CONVA - CONV_A), (0, 0))),
            jnp.pad(cc_dw_w, ((0, RW_SGB - RW_DW - CONV_D), (0, 0))),
            _expand_sgb(sg_b),
        ]
    )


def _sg_w_cat(sg_w):
    cat = jnp.transpose(sg_w, (1, 0, 2)).reshape(CHUNK, N_SUB * CHUNK)
    cat_t = jnp.transpose(sg_w, (2, 0, 1)).reshape(CHUNK, N_SUB * CHUNK)
    return cat, cat_t


def _pool_block_diag(pool_w):
    out = jnp.zeros((D_G, D_G), pool_w.dtype)
    for g in range(N_SUB):
        out = out.at[g * HEAD_DIM : (g + 1) * HEAD_DIM, g * HEAD_DIM : (g + 1) * HEAD_DIM].set(pool_w[g])
    return out


def _prepare_layer(mem, w, l):
    cat, cat_t = _sg_w_cat(w["sg_w"])
    kst, vst = _kv_forward(mem, w["w_kv"], name=f"kv_fwd{l}")
    return dict(
        win=w["w_in"],
        wout=w["w_out"],
        pww=w["cc_pw_w"],
        sw=_pack_small_weights(
            w["sg_ln_g"], w["sg_ln_b"], w["pool_scale"], w["cc_dw_b"], w["cc_ln_g"], w["cc_ln_b"],
            w["conv_a_w"], w["cc_dw_w"], w["sg_b"],
        ),
        wcat=cat,
        wcat_t=cat_t,
        poolw=_pool_block_diag(w["pool_w"]).astype(BF16),
        ln=jnp.stack([w["ln_g"], w["ln_b"]]),
        kst=kst,
        vst=vst,
    )


def _forward(l, h, p, tgt, exchange=None):
    return _layer_forward(
        h, p["win"], p["kst"], p["vst"], p["wout"], p["sw"], p["wcat"], p["poolw"], p["pww"], p["ln"], tgt,
        name=f"layer_fwd{l}", exchange=exchange,
    )


def _backward(l, dxn, s, p, exchange=None):
    return _layer_backward(
        dxn, s[2], s[0], s[1], s[4], p["kst"], p["vst"], p["wout"], p["sw"], p["wcat"], p["wcat_t"], p["poolw"],
        p["pww"], p["ln"], name=f"layer_bwd{l}", exchange=exchange,
    )


def _place():
    x, y, c = lax.axis_index("x"), lax.axis_index("y"), lax.axis_index("c")
    others = [(1 - x, y), (x, 1 - y), (1 - x, 1 - y)]
    return x, y, c, others


def _half(ref, c, axis):
    n = ref.shape[axis] // 2
    if axis == 0:
        return ref.at[pl.ds(c * n, n)]
    return ref.at[:, pl.ds(c * n, n)]


def _place_own_block(place, stacked, layer, dtypes, *, name):
    n = len(stacked)

    def body(place_ref, *refs):
        for a in range(n):
            refs[n + a][...] = refs[a][...].astype(dtypes[a])

    def block(s):
        return (1,) + s.shape[1:]

    return pl.pallas_call(
        body,
        name=name,
        grid_spec=pltpu.PrefetchScalarGridSpec(
            num_scalar_prefetch=1,
            grid=(1,),
            in_specs=[pl.BlockSpec(block(s), lambda i, place_ref: (layer, 0, 0)) for s in stacked],
            out_specs=[pl.BlockSpec(block(s), lambda i, place_ref: (place_ref[1], 0, 0)) for s in stacked],
        ),
        out_shape=[jax.ShapeDtypeStruct((N_CHIPS,) + s.shape[1:], dt) for s, dt in zip(stacked, dtypes)],
        compiler_params=pltpu.CompilerParams(dimension_semantics=("arbitrary",), vmem_limit_bytes=VMEM_LIMIT),
    )(place, *stacked)


def _sds(a):
    return jax.ShapeDtypeStruct(a.shape, a.dtype)


def _gather_exchange(bufs):
    n = len(bufs)

    def remote(sems, block, k, to):
        return pltpu.make_async_remote_copy(
            src_ref=block, dst_ref=block, send_sem=sems[0].at[k], recv_sem=sems[1].at[k], device_id=to, device_id_type=MESH
        )

    def before(step, steps, refs, outs, sems):
        def send():
            x, y, c, others = _place()
            for j, (px, py) in enumerate(others):
                for a in range(n):
                    remote(sems, _half(refs[a].at[2 * x + y], c, 0), 3 * a + j, (px, py, c)).start()

        _when(step == 0, send)

    def after(step, steps, refs, outs, sems):
        def pass_on():
            x, y, c, others = _place()
            for j, (px, py) in enumerate(others):
                for a in range(n):
                    landed = _half(refs[a].at[2 * px + py], c, 0)
                    remote(sems, landed, 3 * a + j, (px, py, c)).wait_recv()
                    remote(sems, landed, 3 * n + 3 * a + j, (x, y, 1 - c)).start()

        def finish():
            x, y, c, others = _place()
            for j, (px, py) in enumerate(others):
                for a in range(n):
                    remote(sems, _half(refs[a].at[2 * px + py], 1 - c, 0), 3 * n + 3 * a + j, (x, y, 1 - c)).wait_recv()
            for a in range(n):
                mine = _half(refs[a].at[2 * x + y], c, 0)
                for k in range(3 * a, 3 * a + 3):
                    remote(sems, mine, k, (x, y, 1 - c)).wait_send()
                    remote(sems, mine, 3 * n + k, (x, y, 1 - c)).wait_send()

        _when(step == (3 * steps) // 4, pass_on)
        _when(step == steps - 1, finish)

    return _Exchange(bufs, [(_sds(b), a) for a, b in enumerate(bufs)], [6 * n, 6 * n], before, after)


def _swap_exchange(grads):
    n = len(grads)

    def copy(refs, outs, sems, a):
        x, y, c, _ = _place()
        return pltpu.make_async_remote_copy(
            src_ref=_half(refs[a], 1 - c, 1), dst_ref=outs[a], send_sem=sems[0].at[a], recv_sem=sems[1].at[a],
            device_id=(x, y, 1 - c), device_id_type=MESH,
        )

    def before(step, steps, refs, outs, sems):
        _when(step == 0, lambda: [copy(refs, outs, sems, a).start() for a in range(n)] and None)

    def after(step, steps, refs, outs, sems):
        _when(step == steps - 1, lambda: [copy(refs, outs, sems, a).wait() for a in range(n)] and None)

    outputs = [(jax.ShapeDtypeStruct((N_CHIPS, g.shape[1] // 2, g.shape[2]), g.dtype), None) for g in grads]
    return _Exchange(grads, outputs, [n, n], before, after)


def _add_sibling_half(place, grads, received, *, name):
    n = len(grads)

    def body(place_ref, *refs):
        k = pl.program_id(0)
        for a in range(n):
            pair = (refs[a][...] + refs[n + a][...]).astype(BF16)
            refs[2 * n + a][...] = pair

            @pl.when(k == place_ref[1])
            def _(a=a, pair=pair):
                refs[3 * n + a][...] = pair

    def block(g):
        return (1, g.shape[1] // 2, g.shape[2])

    return pl.pallas_call(
        body,
        name=name,
        grid_spec=pltpu.PrefetchScalarGridSpec(
            num_scalar_prefetch=1,
            grid=(N_CHIPS,),
            in_specs=[pl.BlockSpec(block(g), lambda k, place_ref: (k, place_ref[0], 0)) for g in grads]
            + [pl.BlockSpec(block(g), lambda k, place_ref: (k, 0, 0)) for g in grads],
            out_specs=[pl.BlockSpec(block(g), lambda k, place_ref: (k, 0, 0)) for g in grads]
            + [pl.BlockSpec(block(g), lambda k, place_ref: (place_ref[1], 0, 0)) for g in grads],
        ),
        out_shape=[jax.ShapeDtypeStruct(r.shape, BF16) for r in received] * 2,
        compiler_params=pltpu.CompilerParams(dimension_semantics=("arbitrary",), vmem_limit_bytes=VMEM_LIMIT),
    )(place, *grads, *received)


def _scatter_exchange(pairs, landing):
    n = len(pairs)

    def copy(refs, sems, a, j, px, py):
        x, y, c, _ = _place()
        return pltpu.make_async_remote_copy(
            src_ref=refs[a].at[2 * px + py], dst_ref=refs[n + a].at[2 * x + y], send_sem=sems[0].at[3 * a + j],
            recv_sem=sems[1].at[3 * a + j], device_id=(px, py, c), device_id_type=MESH,
        )

    def before(step, steps, refs, outs, sems):
        def send():
            for j, (px, py) in enumerate(_place()[3]):
                for a in range(n):
                    copy(refs, sems, a, j, px, py).start()

        _when(step == 0, send)

    def after(step, steps, refs, outs, sems):
        def finish():
            x, y, c, others = _place()
            for j, (px, py) in enumerate(others):
                for a in range(n):
                    landed = refs[n + a].at[2 * px + py]
                    pltpu.make_async_remote_copy(
                        src_ref=landed, dst_ref=landed, send_sem=sems[0].at[3 * a + j], recv_sem=sems[1].at[3 * a + j],
                        device_id=(px, py, c), device_id_type=MESH,
                    ).wait_recv()
            for j, (px, py) in enumerate(others):
                for a in range(n):
                    copy(refs, sems, a, j, px, py).wait_send()

        _when(step == steps - 1, finish)

    return _Exchange(pairs + landing, [(_sds(b), n + a) for a, b in enumerate(landing)], [3 * n, 3 * n], before, after)


SUM_STEPS = 2


def _sum_chip_blocks(place, parts, keep_chip_axis, *, name):
    n = len(parts)

    def body(place_ref, *refs):
        for a in range(n):
            p = refs[a]
            total = (p[0].astype(F32) + p[1].astype(F32)) + (p[2].astype(F32) + p[3].astype(F32))
            if keep_chip_axis[a]:
                refs[n + a][0] = total
            else:
                refs[n + a][...] = total

    def in_spec(p):
        return pl.BlockSpec((N_CHIPS, p.shape[1] // SUM_STEPS, p.shape[2]), lambda i, place_ref: (0, i, 0))

    def out_spec(p, keep):
        rows = p.shape[1] // SUM_STEPS
        if keep:
            return pl.BlockSpec((1, rows, p.shape[2]), lambda i, place_ref: (place_ref[1], place_ref[0] * SUM_STEPS + i, 0))
        return pl.BlockSpec((rows, p.shape[2]), lambda i, place_ref: (place_ref[0] * SUM_STEPS + i, 0))

    def out_shape(p, keep):
        shape = (2 * p.shape[1], p.shape[2])
        return jax.ShapeDtypeStruct((N_CHIPS,) + shape if keep else shape, F32)

    return pl.pallas_call(
        body,
        name=name,
        grid_spec=pltpu.PrefetchScalarGridSpec(
            num_scalar_prefetch=1,
            grid=(SUM_STEPS,),
            in_specs=[in_spec(p) for p in parts],
            out_specs=[out_spec(p, k) for p, k in zip(parts, keep_chip_axis)],
        ),
        out_shape=[out_shape(p, k) for p, k in zip(parts, keep_chip_axis)],
        compiler_params=pltpu.CompilerParams(dimension_semantics=("arbitrary",), vmem_limit_bytes=VMEM_LIMIT),
    )(place, *parts)


def _join_exchange(bufs, keep_chip_axis):
    n = len(bufs)

    def remote(refs, sems, a, cc):
        x, y, c, _ = _place()
        half = _half(refs[a].at[2 * x + y] if keep_chip_axis[a] else refs[a], cc, 0)
        return pltpu.make_async_remote_copy(
            src_ref=half, dst_ref=half, send_sem=sems[0].at[a], recv_sem=sems[1].at[a], device_id=(x, y, 1 - c),
            device_id_type=MESH,
        )

    def before(step, steps, refs, outs, sems):
        def send():
            c = lax.axis_index("c")
            for a in range(n):
                remote(refs, sems, a, c).start()

        _when(step == 0, send)

    def after(step, steps, refs, outs, sems):
        def finish():
            c = lax.axis_index("c")
            for a in range(n):
                remote(refs, sems, a, 1 - c).wait_recv()
            for a in range(n):
                remote(refs, sems, a, c).wait_send()

        _when(step == steps - 1, finish)

    return _Exchange(bufs, [(_sds(b), a) for a, b in enumerate(bufs)], [n, n], before, after)


def _adamw(w, g, m, v):
    m = ADAM_B1 * m + (1.0 - ADAM_B1) * g
    v = ADAM_B2 * v + (1.0 - ADAM_B2) * (g * g)
    m_hat = m / (1.0 - ADAM_B1**ADAM_STEP)
    v_hat = v / (1.0 - ADAM_B2**ADAM_STEP)
    delta = -ADAM_LR * (m_hat / (jnp.sqrt(v_hat) + ADAM_EPS) + ADAM_WD * w)
    return delta, m, v


def _adamw_large(w, m, v, layer_grads, *, name):
    depth, rows, cols = w.shape
    tile = math.gcd(rows, ADAM_TILE)
    assert tile % 8 == 0

    def body(w_ref, m_ref, v_ref, *refs):
        g_refs, (g_out, d_out, m_out, v_out) = refs[:depth], refs[depth:]
        for l in range(depth):

            @pl.when(pl.program_id(0) == l)
            def _(l=l):
                g = g_refs[l][...]
                delta, m_new, v_new = _adamw(w_ref[0], g, m_ref[0], v_ref[0])
                g_out[0], d_out[0], m_out[0], v_out[0] = g, delta, m_new, v_new

    def stacked():
        return pl.BlockSpec((1, tile, cols), lambda l, i: (l, i, 0))

    def layer_spec(l):
        return pl.BlockSpec((tile, cols), lambda k, i: (jnp.where(k == l, i, 0), 0))

    shape = jax.ShapeDtypeStruct(w.shape, F32)
    return pl.pallas_call(
        body,
        name=name,
        grid=(depth, rows // tile),
        in_specs=[stacked(), stacked(), stacked()] + [layer_spec(l) for l in range(depth)],
        out_specs=[stacked()] * 4,
        out_shape=[shape] * 4,
        compiler_params=pltpu.CompilerParams(dimension_semantics=("arbitrary", "arbitrary"), vmem_limit_bytes=VMEM_LIMIT),
    )(w, m, v, *layer_grads)


def _adamw_small(ws, gs, ms, vs, *, name):
    n = len(ws)

    def body(*refs):
        for a in range(n):
            delta, m_new, v_new = _adamw(refs[a][...], refs[n + a][...], refs[2 * n + a][...], refs[3 * n + a][...])
            refs[4 * n + a][...] = delta
            refs[5 * n + a][...] = m_new
            refs[6 * n + a][...] = v_new

    shapes = [jax.ShapeDtypeStruct(w.shape, F32) for w in ws]
    outs = pl.pallas_call(body, name=name, out_shape=shapes * 3)(*ws, *gs, *ms, *vs)
    return outs[:n], outs[n : 2 * n], outs[2 * n :]


WEIGHT_NAMES = (
    "w_in", "conv_a_w", "sg_ln_g", "sg_ln_b", "sg_w", "sg_b", "pool_w", "pool_scale", "cc_dw_w", "cc_dw_b", "cc_ln_g",
    "cc_ln_b", "cc_pw_w", "w_kv", "w_out", "ln_g", "ln_b",
)
LARGE = ("w_in", "cc_pw_w", "w_kv", "w_out")
TAPS_ROWS = 48


def _unpack_small_grads(small, chip):
    out = {}
    for r, k in enumerate(("sg_ln_g", "sg_ln_b", "pool_scale", "cc_dw_b", "cc_ln_g", "cc_ln_b")):
        out[k] = small[RG_VEC + r]
    out["conv_a_w"] = lax.dynamic_slice_in_dim(small[RG_CONVA : RG_CONVA + CONV_A], chip * HEAD_DIM, HEAD_DIM, axis=1)
    out["cc_dw_w"] = lax.dynamic_slice_in_dim(small[RG_DW : RG_DW + CONV_D], chip * HEAD_DIM, HEAD_DIM, axis=1)
    cat = jnp.concatenate([small[RG_SGW : RG_SGW + CHUNK], small[RG_SGW + CHUNK : RG_SGW + 2 * CHUNK]], axis=1)
    out["sg_w"] = jnp.transpose(cat.reshape(CHUNK, N_SUB, CHUNK), (1, 0, 2))
    out["sg_b"] = small[RG_SGB : RG_SGB + CHUNK].reshape(CHUNK, N_SUB, HEAD_DIM).sum(-1).T
    pool = small[RG_POOL : RG_POOL + D_G]
    out["pool_w"] = jnp.stack(
        [pool[g * HEAD_DIM : (g + 1) * HEAD_DIM, g * HEAD_DIM : (g + 1) * HEAD_DIM] for g in range(N_SUB)]
    )
    out["ln_g"] = small[RG_LN : RG_LN + 4].reshape(D_MODEL)
    out["ln_b"] = small[RG_LN + 4 : RG_LN + 8].reshape(D_MODEL)
    return out


def kernel(x, mem, w_in, conv_a_w, sg_ln_g, sg_ln_b, sg_w, sg_b, pool_w, pool_scale, cc_dw_w, cc_dw_b, cc_ln_g, cc_ln_b, cc_pw_w, w_kv, w_out, ln_g, ln_b, loss_target, m_w_in, m_conv_a_w, m_sg_ln_g, m_sg_ln_b, m_sg_w, m_sg_b, m_pool_w, m_pool_scale, m_cc_dw_w, m_cc_dw_b, m_cc_ln_g, m_cc_ln_b, m_cc_pw_w, m_w_kv, m_w_out, m_ln_g, m_ln_b, v_w_in, v_conv_a_w, v_sg_ln_g, v_sg_ln_b, v_sg_w, v_sg_b, v_pool_w, v_pool_scale, v_cc_dw_w, v_cc_dw_b, v_cc_ln_g, v_cc_ln_b, v_cc_pw_w, v_w_kv, v_w_out, v_ln_g, v_ln_b):
    given = dict(locals())
    weights = {k: given[k] for k in WEIGHT_NAMES}
    chip = 2 * lax.axis_index("x") + lax.axis_index("y")
    place = jnp.stack([lax.axis_index("c"), chip]).astype(jnp.int32)

    x0, mem0 = x[0], mem[0]

    taps = jnp.concatenate([conv_a_w, cc_dw_w], axis=1)
    taps = jnp.pad(taps, ((0, 0), (0, TAPS_ROWS - taps.shape[1]), (0, 0)))

    def own_blocks(l):
        return _place_own_block(
            place, [w_in, w_out, w_kv, cc_pw_w, taps], l, [BF16, BF16, BF16, BF16, F32], name=f"place_weights{l}"
        )

    def layer_operands(l, gathered):
        g_in, g_out, g_kv, g_pw, g_taps = gathered
        taps_full = jnp.transpose(g_taps, (1, 0, 2)).reshape(TAPS_ROWS, D_G)
        full = dict(
            w_in=g_in,
            w_out=g_out.reshape(D_MIX, D_MODEL),
            w_kv=g_kv.reshape(D_MODEL, 2 * D_G),
            cc_pw_w=g_pw.reshape(D_G, D_G),
            conv_a_w=taps_full[0:CONV_A],
            cc_dw_w=taps_full[CONV_A : CONV_A + CONV_D],
            **{k: weights[k][l] for k in WEIGHT_NAMES if k not in LARGE + ("conv_a_w", "cc_dw_w")},
        )
        return _prepare_layer(mem0, full, l)

    def layer_grads(l, x_in, p, bwd):
        dproj, _, dwout, dkst, dvst, dpw, small = bwd
        return [
            _input_weight_grad(x_in, dproj, name=f"w_in_grad{l}"),
            dwout.reshape(N_CHIPS, D_MIX // N_CHIPS, D_MODEL),
            _kv_backward(mem0, dkst, dvst, name=f"kv_bwd{l}").reshape(N_CHIPS, D_MODEL // N_CHIPS, 2 * D_G),
            dpw.reshape(N_CHIPS, D_G // N_CHIPS, D_G),
            small.reshape(N_CHIPS, RG_ROWS // N_CHIPS, D_G),
        ]

    n_red = 5
    keep = [False, False, False, False, True]

    def finish_reduce(l, parts):
        halves = _sum_chip_blocks(place, parts, keep, name=f"rs_sum{l}")
        r_in, r_out, r_kv, r_pw, r_small = _run_exchange(_join_exchange(halves, keep), name=f"rs_join{l}")
        (small_all,) = _run_exchange(_gather_exchange([r_small]), name=f"gather_small_grads{l}")
        out = _unpack_small_grads(small_all.reshape(RG_ROWS, D_G), chip)
        out.update(w_in=r_in, w_out=r_out, w_kv=r_kv, cc_pw_w=r_pw)
        return out

    blocks0, blocks1 = own_blocks(0), own_blocks(1)
    p0 = layer_operands(0, _run_exchange(_gather_exchange(blocks0), name="gather_weights0"))
    fwd0, gathered1 = _forward(0, x0, p0, None, exchange=_gather_exchange(blocks1))
    p1 = layer_operands(1, gathered1)
    x1 = fwd0[3]
    fwd1, _ = _forward(1, x1, p1, loss_target[0])
    loss = lax.psum(fwd1[5][0, 0], ("x", "y", "c"))

    bwd1, _ = _backward(1, fwd1[3], fwd1, p1)
    grads1 = layer_grads(1, x1, p1, bwd1)
    (dx1,), received1 = _input_grad(bwd1[0], bwd1[1], p1["win"], name="input_grad1", exchange=_swap_exchange(grads1))
    pairs1 = _add_sibling_half(place, grads1, received1, name="rs_pair1")
    bwd0, parts1 = _backward(0, dx1, fwd0, p0, exchange=_scatter_exchange(pairs1[:n_red], pairs1[n_red:]))
    reduced1 = finish_reduce(1, parts1)
    grads0 = layer_grads(0, x0, p0, bwd0)
    received0 = _run_exchange(_swap_exchange(grads0), name="rs_swap0")
    pairs0 = _add_sibling_half(place, grads0, received0, name="rs_pair0")
    (grad_x,), parts0 = _input_grad(
        bwd0[0], bwd0[1], p0["win"], name="input_grad0", exchange=_scatter_exchange(pairs0[:n_red], pairs0[n_red:])
    )
    reduced = [finish_reduce(0, parts0), reduced1]

    grad, delta, new_m, new_v = {}, {}, {}, {}
    for k in LARGE:
        w3 = weights[k]
        grad[k], delta[k], new_m[k], new_v[k] = _adamw_large(
            w3, given["m_" + k], given["v_" + k], [reduced[l][k] for l in range(DEPTH)], name=f"adamw_{k}"
        )
    small_names = [k for k in WEIGHT_NAMES if k not in LARGE]
    for k in small_names:
        grad[k] = jnp.stack([reduced[l][k] for l in range(DEPTH)])
    d_s, m_s, v_s = _adamw_small(
        [weights[k] for k in small_names],
        [grad[k] for k in small_names],
        [given["m_" + k] for k in small_names],
        [given["v_" + k] for k in small_names],
        name="adamw_small",
    )
    for a, k in enumerate(small_names):
        delta[k], new_m[k], new_v[k] = d_s[a], m_s[a], v_s[a]

    return (
        loss,
        grad_x[None],
        *[grad[k] for k in WEIGHT_NAMES],
        *[delta[k] for k in WEIGHT_NAMES],
        *[new_m[k] for k in WEIGHT_NAMES],
        *[new_v[k] for k in WEIGHT_NAMES],
    )
```

```python
import functools
import math

import jax
import jax.numpy as jnp
from jax import lax
from jax.experimental import pallas as pl
from jax.experimental.pallas import tpu as pltpu

F32 = jnp.float32
BF16 = jnp.bfloat16

D_MODEL = 1024
DEPTH = 2
D_G = 256
D_MIX = 5 * D_G
D_IN = 9 * D_G + D_MIX
N_SUB = 4
HEAD_DIM = 64
CONV_A = 3
CONV_D = 31
CHUNK = 128
MEM_LEN = 256
N_CHIPS = 4
W_IN_SHARD = D_IN // N_CHIPS
LN_EPS = 1e-5
ALPHA = (2.0 * DEPTH) ** 0.25
ATT_SCALE = 1.0 / math.sqrt(HEAD_DIM)
GELU_C = math.sqrt(2.0 / math.pi)
GELU_A = 0.044715

ADAM_LR = 0.001
ADAM_B1 = 0.9
ADAM_B2 = 0.999
ADAM_EPS = 1e-08
ADAM_WD = 0.01
ADAM_STEP = 10

C_XA, C_BA, C_CA, C_U, C_V, C_XC, C_DA, C_DG, C_Q, C_GATE = (D_G * i for i in range(10))

HALO_A = 8
HALO_C = 16
HALO_D = 32

RW_VEC = 0
RW_CONVA = 16
RW_DW = 24
RW_SGB = 56
RW_ROWS = RW_SGB + CHUNK

RG_VEC = 0
RG_CONVA = 16
RG_DW = 24
RG_SGW = 56
RG_SGB = RG_SGW + 2 * CHUNK
RG_POOL = RG_SGB + CHUNK
RG_LN = RG_POOL + D_G
RG_ROWS = 768

VMEM_LIMIT = 56 * 1024 * 1024
SEQ_TILE = 256
FWD_TILE = 512
MM_TILE = 1024
ADAM_TILE = 512

MESH = pl.DeviceIdType.MESH
ANY = pl.BlockSpec(memory_space=pl.ANY)
NT = (((1,), (1,)), ((), ()))
TN = (((0,), (0,)), ((), ()))


def _dot(a, b):
    return jnp.dot(a, b, preferred_element_type=F32)


def _dot_nt(a, b):
    return lax.dot_general(a, b, NT, preferred_element_type=F32)


def _dot_tn(a, b):
    return lax.dot_general(a, b, TN, preferred_element_type=F32)


def _full(shape):
    zeros = (0,) * len(shape)
    return pl.BlockSpec(shape, lambda *_: zeros)


class _Exchange:
    def __init__(self, operands, outputs, sem_counts, before, after):
        self.operands, self.outputs, self.sem_counts, self.before, self.after = operands, outputs, sem_counts, before, after

    def specs(self, first_input, first_output):
        aliases = {first_input + src: first_output + j for j, (_, src) in enumerate(self.outputs) if src is not None}
        return (
            [ANY] * len(self.operands),
            [ANY] * len(self.outputs),
            [sds for sds, _ in self.outputs],
            [pltpu.SemaphoreType.DMA((k,)) for k in self.sem_counts],
            aliases,
        )

    def split(self, ins, outs):
        refs = list(ins)
        for j, (_, src) in enumerate(self.outputs):
            if src is not None:
                refs[src] = outs[j]
        return refs


def _when(cond, fn):
    if isinstance(cond, bool):
        if cond:
            fn()
    else:
        pl.when(cond)(fn)


def _run_exchange(exchange, *, name):
    n_in, n_out = len(exchange.operands), len(exchange.outputs)
    in_specs, out_specs, out_shape, sems, aliases = exchange.specs(0, 0)

    def body(*refs):
        ins, outs, sem_refs = refs[:n_in], refs[n_in : n_in + n_out], refs[n_in + n_out :]
        refs = exchange.split(ins, outs)
        exchange.before(0, 1, refs, outs, sem_refs)
        exchange.after(0, 1, refs, outs, sem_refs)

    return pl.pallas_call(
        body, name=name, in_specs=in_specs, out_specs=out_specs, out_shape=out_shape, scratch_shapes=sems,
        input_output_aliases=aliases,
    )(*exchange.operands)


def _gridded_call(body, *, name, steps, in_specs, out_specs, out_shape, scratch_shapes, operands, exchange=None):
    params = pltpu.CompilerParams(dimension_semantics=("arbitrary",), vmem_limit_bytes=VMEM_LIMIT)
    if exchange is None:
        outs = pl.pallas_call(
            body, name=name, grid=(steps,), in_specs=in_specs, out_specs=out_specs, out_shape=out_shape,
            scratch_shapes=scratch_shapes, compiler_params=params,
        )(*operands)
        return list(outs), []
    n_in, n_out, n_scr = len(in_specs), len(out_specs), len(scratch_shapes)
    x_in, x_out = len(exchange.operands), len(exchange.outputs)
    ex_in_specs, ex_out_specs, ex_out_shape, ex_sems, aliases = exchange.specs(n_in, n_out)

    def full(*refs):
        own_in, refs = refs[:n_in], refs[n_in:]
        ex_in, refs = refs[:x_in], refs[x_in:]
        own_out, refs = refs[:n_out], refs[n_out:]
        ex_out, refs = refs[:x_out], refs[x_out:]
        own_scr, sem_refs = refs[:n_scr], refs[n_scr:]
        ex_refs = exchange.split(ex_in, ex_out)
        step = pl.program_id(0)
        exchange.before(step, steps, ex_refs, ex_out, sem_refs)
        body(*own_in, *own_out, *own_scr)
        exchange.after(step, steps, ex_refs, ex_out, sem_refs)

    outs = pl.pallas_call(
        full, name=name, grid=(steps,), in_specs=in_specs + ex_in_specs, out_specs=out_specs + ex_out_specs,
        out_shape=out_shape + ex_out_shape, scratch_shapes=scratch_shapes + ex_sems, input_output_aliases=aliases,
        compiler_params=params,
    )(*operands, *exchange.operands)
    return list(outs[:n_out]), list(outs[n_out:])


def _sigmoid(x):
    return 0.5 * jnp.tanh(0.5 * x) + 0.5


def _gelu(x):
    t = jnp.tanh(GELU_C * (x + GELU_A * x * x * x))
    return 0.5 * x * (1.0 + t), t


def _gelu_grad(x, t):
    return 0.5 * (1.0 + t) + 0.5 * x * (1.0 - t * t) * (GELU_C * (1.0 + 3.0 * GELU_A * x * x))


def _normalize(v):
    mu = jnp.mean(v, axis=-1, keepdims=True)
    d = v - mu
    var = jnp.mean(d * d, axis=-1, keepdims=True)
    rstd = lax.rsqrt(var + LN_EPS)
    return d * rstd, rstd


def _normalize_grad(dhat, hat, rstd):
    m1 = jnp.mean(dhat, axis=-1, keepdims=True)
    m2 = jnp.mean(dhat * hat, axis=-1, keepdims=True)
    return rstd * (dhat - m1 - hat * m2)


def _lane(width=D_G):
    return lax.broadcasted_iota(jnp.int32, (1, width), 1)


def _head_masks():
    head = _lane() // HEAD_DIM
    return [(head == h).astype(F32) for h in range(N_SUB)]


def _stack_heads(v, masks):
    return jnp.concatenate([v * m for m in masks], axis=0)


def _tril_mask_cat():
    t = lax.broadcasted_iota(jnp.int32, (CHUNK, N_SUB * CHUNK), 0)
    s = lax.broadcasted_iota(jnp.int32, (CHUNK, N_SUB * CHUNK), 1) % CHUNK
    return s <= t


def _triu_mask_cat():
    s = lax.broadcasted_iota(jnp.int32, (CHUNK, N_SUB * CHUNK), 0)
    t = lax.broadcasted_iota(jnp.int32, (CHUNK, N_SUB * CHUNK), 1) % CHUNK
    return t >= s


def _pool_select(a2, a4, a8, a16):
    lane = _lane()
    return jnp.where(lane < 64, a2, jnp.where(lane < 128, a4, jnp.where(lane < 192, a8, a16)))


def _pool_inv_count(row0, rows):
    t = row0 + lax.broadcasted_iota(jnp.int32, (HALO_C, D_G), 0)
    lane = lax.broadcasted_iota(jnp.int32, (HALO_C, D_G), 1)
    win = jnp.where(lane < 64, 2, jnp.where(lane < 128, 4, jnp.where(lane < 192, 8, 16)))
    head = 1.0 / jnp.minimum(t + 1, win).astype(F32)
    inv_win = jnp.broadcast_to(_pool_select(0.5, 0.25, 0.125, 0.0625), (rows - HALO_C, D_G))
    return jnp.concatenate([head, inv_win], axis=0)


def _trailing_window_sum(halo, cur):
    e = jnp.concatenate([halo, cur], axis=0)
    s2 = e + pltpu.roll(e, 1, 0)
    s4 = s2 + pltpu.roll(s2, 2, 0)
    s8 = s4 + pltpu.roll(s4, 4, 0)
    s16 = s8 + pltpu.roll(s8, 8, 0)
    return _pool_select(s2, s4, s8, s16)[HALO_C:]


def _leading_window_sum(cur, halo):
    e = jnp.concatenate([cur, halo], axis=0)
    n = e.shape[0]
    s2 = e + pltpu.roll(e, n - 1, 0)
    s4 = s2 + pltpu.roll(s2, n - 2, 0)
    s8 = s4 + pltpu.roll(s4, n - 4, 0)
    s16 = s8 + pltpu.roll(s8, n - 8, 0)
    return _pool_select(s2, s4, s8, s16)[: cur.shape[0]]


def _softmax_blocks(sc):
    out = []
    for h in range(N_SUB):
        s = sc[:, h * MEM_LEN : (h + 1) * MEM_LEN]
        e = jnp.exp(s - jnp.max(s, axis=-1, keepdims=True))
        out.append(e * (1.0 / jnp.sum(e, axis=-1, keepdims=True)))
    return jnp.concatenate(out, axis=-1)


SHIFTS = 8


def _fill_shifts(buf):
    n = buf.shape[1] - SHIFTS
    for r in range(1, SHIFTS):
        buf[r, 0:n, :] = buf[0, r : r + n, :]


def _shifted(buf, off, rows):
    r = off % SHIFTS
    return buf[r, off - r : off - r + rows, :]


def _sgu_mix(vn, wcat_b, sgb, masks):
    vbd = _stack_heads(vn, masks).astype(BF16)
    return _dot(wcat_b, vbd) + sgb, vbd


def _layer_forward(x, win, kst, vst, wout, sw, wcat, poolw, pww, ln, tgt, *, name, exchange=None):
    seq = x.shape[0]
    tile = min(FWD_TILE, seq)
    n_tiles = seq // tile
    last = tgt is not None

    def body(*refs):
        x_ref, win_ref, kst_ref, vst_ref, wout_ref, sw_ref, wcat_ref, pool_ref, pw_ref, ln_ref = refs[:10]
        refs = refs[10:]
        if last:
            tgt_ref, refs = refs[0], refs[1:]
        proj_ref, y_ref, z_ref, out_ref, cvd_ref = refs[:5]
        refs = refs[5:]
        if last:
            loss_ref, refs = refs[0], refs[1:]
        pbuf, xchalo, gbuf = refs
        i = pl.program_id(0)

        @pl.when(i == 0)
        def _():
            pbuf[0:HALO_A, :] = jnp.zeros((HALO_A, D_G), F32)
            xchalo[...] = jnp.zeros((HALO_C, D_G), F32)
            gbuf[0, 0:HALO_D, :] = jnp.zeros((HALO_D, D_G), F32)
            if last:
                loss_ref[...] = jnp.zeros((8, 128), F32)

        xt = x_ref[...]
        xb = xt.astype(BF16)

        blocks = {}

        def project(k):
            blocks[k] = _dot(xb, win_ref[k])
            proj_ref[:, k * W_IN_SHARD : (k + 1) * W_IN_SHARD] = blocks[k]

        def cols(start, width=D_G):
            parts, c = [], start
            while c < start + width:
                k, lo = divmod(c, W_IN_SHARD)
                hi = min(W_IN_SHARD, lo + start + width - c)
                parts.append(blocks[k][:, lo:hi])
                c += hi - lo
            return parts[0] if len(parts) == 1 else jnp.concatenate(parts, axis=1)

        project(0)
        project(1)
        masks = _head_masks()

        pbuf[HALO_A : HALO_A + tile, :] = cols(C_CA) * cols(C_XA)
        cv = jnp.zeros((tile, D_G), F32)
        for k in range(CONV_A):
            off = HALO_A - (CONV_A - 1) + k
            cv = cv + sw_ref[RW_CONVA + k : RW_CONVA + k + 1, :] * pbuf[off : off + tile, :]
        y_ref[:, 0:D_G] = cols(C_BA) * cv
        pbuf[0:HALO_A, :] = pbuf[tile : tile + HALO_A, :]

        project(2)

        ua, _ = _gelu(cols(C_U))
        vg, _ = _gelu(cols(C_V))
        vhat, _ = _normalize(vg)
        vn = vhat * sw_ref[RW_VEC : RW_VEC + 1, :] + sw_ref[RW_VEC + 1 : RW_VEC + 2, :]
        wcat_b = jnp.where(_tril_mask_cat(), wcat_ref[...], 0.0).astype(BF16)
        sgb = sw_ref[RW_SGB : RW_SGB + CHUNK, :]
        for j in range(tile // CHUNK):
            rows = slice(j * CHUNK, (j + 1) * CHUNK)
            mixed, _ = _sgu_mix(vn[rows], wcat_b, sgb, masks)
            y_ref[rows, D_G : 2 * D_G] = ua[rows] * mixed

        xc = cols(C_XC)
        wsum = _trailing_window_sum(xchalo[...], xc)
        pm = wsum * _pool_inv_count(i * tile, tile) - xc
        y_ref[:, 2 * D_G : 3 * D_G] = _dot(pm.astype(BF16), pool_ref[...]) * sw_ref[RW_VEC + 2 : RW_VEC + 3, :]
        xchalo[...] = xc[tile - HALO_C :, :]

        project(3)

        gbuf[0, HALO_D : HALO_D + tile, :] = cols(C_DA) * _sigmoid(cols(C_DG))
        _fill_shifts(gbuf)
        cvd = jnp.zeros((tile, D_G), F32) + sw_ref[RW_VEC + 3 : RW_VEC + 4, :]
        for k in range(CONV_D):
            off = HALO_D - (CONV_D - 1) + k
            cvd = cvd + sw_ref[RW_DW + k : RW_DW + k + 1, :] * _shifted(gbuf, off, tile)
        cvd_ref[...] = cvd
        nhat, _ = _normalize(cvd)
        nrm = nhat * sw_ref[RW_VEC + 4 : RW_VEC + 5, :] + sw_ref[RW_VEC + 5 : RW_VEC + 6, :]
        y_ref[:, 3 * D_G : 4 * D_G] = _dot((nrm * _sigmoid(nrm)).astype(BF16), pw_ref[...])
        gbuf[0, 0:HALO_D, :] = gbuf[0, tile : tile + HALO_D, :]

        qb = cols(C_Q).astype(BF16)
        p_all = _softmax_blocks(_dot_nt(qb, kst_ref[...]) * ATT_SCALE)
        y_ref[:, 4 * D_G : 5 * D_G] = _dot(p_all.astype(BF16), vst_ref[...])

        gate = cols(C_GATE, D_MIX)
        hid = y_ref[...] * (gate * _sigmoid(gate))
        z = ALPHA * xt + _dot(hid.astype(BF16), wout_ref[...])
        z_ref[...] = z
        zhat, _ = _normalize(z)
        xn = zhat * ln_ref[0:1, :] + ln_ref[1:2, :]
        if last:
            err = xn - tgt_ref[...]
            out_ref[...] = err * (1.0 / D_MODEL)
            loss_ref[...] += jnp.sum(err * err) * (0.5 / D_MODEL)
        else:
            out_ref[...] = xn

    def rows(width):
        return pl.BlockSpec((tile, width), lambda i: (i, 0))

    operands = [x, win, kst, vst, wout, sw, wcat, poolw, pww, ln]
    in_specs = [rows(D_MODEL)] + [_full(a.shape) for a in operands[1:]]
    out_shape = [
        jax.ShapeDtypeStruct((seq, D_IN), F32),
        jax.ShapeDtypeStruct((seq, D_MIX), F32),
        jax.ShapeDtypeStruct((seq, D_MODEL), F32),
        jax.ShapeDtypeStruct((seq, D_MODEL), F32),
        jax.ShapeDtypeStruct((seq, D_G), F32),
    ]
    out_specs = [rows(D_IN), rows(D_MIX), rows(D_MODEL), rows(D_MODEL), rows(D_G)]
    if last:
        operands.append(tgt)
        in_specs.append(rows(D_MODEL))
        out_shape.append(jax.ShapeDtypeStruct((8, 128), F32))
        out_specs.append(_full((8, 128)))
    return _gridded_call(
        body,
        name=name,
        steps=n_tiles,
        in_specs=in_specs,
        out_specs=out_specs,
        out_shape=out_shape,
        scratch_shapes=[
            pltpu.VMEM((HALO_A + tile, D_G), F32),
            pltpu.VMEM((HALO_C, D_G), F32),
            pltpu.VMEM((SHIFTS, HALO_D + tile, D_G), F32),
        ],
        operands=operands,
        exchange=exchange,
    )


def _layer_backward(dxn, z, proj, y, cvd, kst, vst, wout, sw, wcat, wcat_t, poolw, pww, ln, *, name, exchange=None):
    seq = dxn.shape[0]
    tile = min(SEQ_TILE, seq)
    n_tiles = seq // tile
    halo_blocks = tile // HALO_D

    def body(
        dxn_ref, z_ref, proj_ref, halo_ref, y_ref, cvd_ref, kst_ref, vst_ref, wout_ref, sw_ref, wcat_ref, wcat_t_ref,
        pool_ref, pw_ref, ln_ref, dproj_ref, dz_ref, dwout_ref, dkst_ref, dvst_ref, dpw_ref, sg_ref,
        pbuf, dcvbuf, rhalo, gbuf, dgbuf,
    ):
        i = pl.program_id(0)
        ti = n_tiles - 1 - i

        @pl.when(i == 0)
        def _():
            dwout_ref[...] = jnp.zeros(dwout_ref.shape, F32)
            dkst_ref[...] = jnp.zeros(dkst_ref.shape, F32)
            dvst_ref[...] = jnp.zeros(dvst_ref.shape, F32)
            dpw_ref[...] = jnp.zeros(dpw_ref.shape, F32)
            sg_ref[...] = jnp.zeros(sg_ref.shape, F32)
            dcvbuf[tile : tile + HALO_A, :] = jnp.zeros((HALO_A, D_G), F32)
            rhalo[...] = jnp.zeros((HALO_C, D_G), F32)
            dgbuf[0, tile : tile + HALO_D, :] = jnp.zeros((HALO_D, D_G), F32)

        def acc_row(row, val):
            sg_ref[row : row + 1, :] += jnp.sum(val, axis=0, keepdims=True)

        masks = _head_masks()
        has_past = (ti > 0).astype(F32)

        zhat, zrstd = _normalize(z_ref[...])
        dxn_t = dxn_ref[...]
        dlg = jnp.sum(dxn_t * zhat, axis=0, keepdims=True)
        dlb = jnp.sum(dxn_t, axis=0, keepdims=True)
        for j in range(D_MODEL // D_G):
            sg_ref[RG_LN + j : RG_LN + j + 1, :] += dlg[:, j * D_G : (j + 1) * D_G]
            sg_ref[RG_LN + 4 + j : RG_LN + 5 + j, :] += dlb[:, j * D_G : (j + 1) * D_G]
        dz = _normalize_grad(dxn_t * ln_ref[0:1, :], zhat, zrstd)
        dz_ref[...] = dz
        dzb = dz.astype(BF16)

        gate = proj_ref[:, C_GATE:]
        sgm = _sigmoid(gate)
        silu = gate * sgm
        yc = y_ref[...]
        dwout_ref[...] += _dot_tn((yc * silu).astype(BF16), dzb)
        dh = _dot_nt(dzb, wout_ref[...])
        dproj_ref[:, C_GATE:] = (dh * yc * (sgm * (1.0 + gate * (1.0 - sgm)))).astype(BF16)
        dy = dh * silu

        dya = dy[:, 0:D_G]
        xa = proj_ref[:, C_XA : C_XA + D_G]
        ba = proj_ref[:, C_BA : C_BA + D_G]
        ca = proj_ref[:, C_CA : C_CA + D_G]
        past = slice(HALO_D - HALO_A, HALO_D)
        pbuf[0:HALO_A, :] = halo_ref[past, C_CA : C_CA + D_G] * halo_ref[past, C_XA : C_XA + D_G] * has_past
        pbuf[HALO_A : HALO_A + tile, :] = ca * xa
        cv = jnp.zeros((tile, D_G), F32)
        for k in range(CONV_A):
            off = HALO_A - (CONV_A - 1) + k
            cv = cv + sw_ref[RW_CONVA + k : RW_CONVA + k + 1, :] * pbuf[off : off + tile, :]
        dproj_ref[:, C_BA : C_BA + D_G] = (dya * cv).astype(BF16)
        dcv = dya * ba
        dcvbuf[0:tile, :] = dcv
        dp = jnp.zeros((tile, D_G), F32)
        for k in range(CONV_A):
            off = HALO_A - (CONV_A - 1) + k
            acc_row(RG_CONVA + k, dcv * pbuf[off : off + tile, :])
            back = CONV_A - 1 - k
            dp = dp + sw_ref[RW_CONVA + k : RW_CONVA + k + 1, :] * dcvbuf[back : back + tile, :]
        dproj_ref[:, C_CA : C_CA + D_G] = (dp * xa).astype(BF16)
        dproj_ref[:, C_XA : C_XA + D_G] = (dp * ca).astype(BF16)
        dcvbuf[tile : tile + HALO_A, :] = dcvbuf[0:HALO_A, :]

        dyb = dy[:, D_G : 2 * D_G]
        u = proj_ref[:, C_U : C_U + D_G]
        v = proj_ref[:, C_V : C_V + D_G]
        ua, ut = _gelu(u)
        vg, vt = _gelu(v)
        vhat, vrstd = _normalize(vg)
        sg_g = sw_ref[RW_VEC : RW_VEC + 1, :]
        vn = vhat * sg_g + sw_ref[RW_VEC + 1 : RW_VEC + 2, :]
        tril = _tril_mask_cat()
        wcat_b = jnp.where(tril, wcat_ref[...], 0.0).astype(BF16)
        wcat_tb = jnp.where(_triu_mask_cat(), wcat_t_ref[...], 0.0).astype(BF16)
        sgb = sw_ref[RW_SGB : RW_SGB + CHUNK, :]
        dmixed = dyb * ua
        dvn_parts = []
        du_parts = []
        dwcat = jnp.zeros((CHUNK, N_SUB * CHUNK), F32)
        dsgb = jnp.zeros((CHUNK, D_G), F32)
        for j in range(tile // CHUNK):
            rows = slice(j * CHUNK, (j + 1) * CHUNK)
            mixed, vbd = _sgu_mix(vn[rows], wcat_b, sgb, masks)
            du_parts.append(dyb[rows] * mixed)
            dmx = dmixed[rows]
            dsgb = dsgb + dmx
            dwcat = dwcat + _dot_nt(dmx.astype(BF16), vbd)
            dvn_parts.append(_dot(wcat_tb, _stack_heads(dmx, masks).astype(BF16)))
        dwcat = jnp.where(tril, dwcat, 0.0)
        sg_ref[RG_SGW : RG_SGW + CHUNK, :] += dwcat[:, 0:D_G]
        sg_ref[RG_SGW + CHUNK : RG_SGW + 2 * CHUNK, :] += dwcat[:, D_G:]
        sg_ref[RG_SGB : RG_SGB + CHUNK, :] += dsgb
        dvn = jnp.concatenate(dvn_parts, axis=0)
        du_act = jnp.concatenate(du_parts, axis=0)
        acc_row(RG_VEC, dvn * vhat)
        acc_row(RG_VEC + 1, dvn)
        dvg = _normalize_grad(dvn * sg_g, vhat, vrstd)
        dproj_ref[:, C_U : C_U + D_G] = (du_act * _gelu_grad(u, ut)).astype(BF16)
        dproj_ref[:, C_V : C_V + D_G] = (dvg * _gelu_grad(v, vt)).astype(BF16)

        dyc = dy[:, 2 * D_G : 3 * D_G]
        xc = proj_ref[:, C_XC : C_XC + D_G]
        xc_past = halo_ref[HALO_D - HALO_C : HALO_D, C_XC : C_XC + D_G] * has_past
        inv_cnt = _pool_inv_count(ti * tile, tile)
        pm = _trailing_window_sum(xc_past, xc) * inv_cnt - xc
        pmb = pm.astype(BF16)
        pool_b = pool_ref[...]
        scale = sw_ref[RW_VEC + 2 : RW_VEC + 3, :]
        acc_row(RG_VEC + 2, dyc * _dot(pmb, pool_b))
        dpre = (dyc * scale).astype(BF16)
        sg_ref[RG_POOL : RG_POOL + D_G, :] += _dot_tn(pmb, dpre)
        dpm = _dot_nt(dpre, pool_b)
        r = dpm * inv_cnt
        dproj_ref[:, C_XC : C_XC + D_G] = (_leading_window_sum(r, rhalo[...]) - dpm).astype(BF16)
        rhalo[...] = r[0:HALO_C, :]

        dyd = dy[:, 3 * D_G : 4 * D_G]
        da = proj_ref[:, C_DA : C_DA + D_G]
        sgd = _sigmoid(proj_ref[:, C_DG : C_DG + D_G])
        gbuf[0, 0:HALO_D, :] = halo_ref[:, C_DA : C_DA + D_G] * _sigmoid(halo_ref[:, C_DG : C_DG + D_G]) * has_past
        gbuf[0, HALO_D : HALO_D + tile, :] = da * sgd
        _fill_shifts(gbuf)
        nhat, nrstd = _normalize(cvd_ref[...])
        cc_g = sw_ref[RW_VEC + 4 : RW_VEC + 5, :]
        nrm = nhat * cc_g + sw_ref[RW_VEC + 5 : RW_VEC + 6, :]
        sgn = _sigmoid(nrm)
        dydb = dyd.astype(BF16)
        dpw_ref[...] += _dot_tn((nrm * sgn).astype(BF16), dydb)
        dn = _dot_nt(dydb, pw_ref[...]) * (sgn * (1.0 + nrm * (1.0 - sgn)))
        acc_row(RG_VEC + 4, dn * nhat)
        acc_row(RG_VEC + 5, dn)
        dcvd = _normalize_grad(dn * cc_g, nhat, nrstd)
        acc_row(RG_VEC + 3, dcvd)
        dgbuf[0, 0:tile, :] = dcvd
        _fill_shifts(dgbuf)
        dg = jnp.zeros((tile, D_G), F32)
        for k in range(CONV_D):
            off = HALO_D - (CONV_D - 1) + k
            acc_row(RG_DW + k, dcvd * _shifted(gbuf, off, tile))
            back = CONV_D - 1 - k
            dg = dg + sw_ref[RW_DW + k : RW_DW + k + 1, :] * _shifted(dgbuf, back, tile)
        dproj_ref[:, C_DA : C_DA + D_G] = (dg * sgd).astype(BF16)
        dproj_ref[:, C_DG : C_DG + D_G] = (dg * da * sgd * (1.0 - sgd)).astype(BF16)
        dgbuf[0, tile : tile + HALO_D, :] = dgbuf[0, 0:HALO_D, :]

        dyeb = dy[:, 4 * D_G : 5 * D_G].astype(BF16)
        qb = proj_ref[:, C_Q : C_Q + D_G].astype(BF16)
        kst_b = kst_ref[...]
        p_all = _softmax_blocks(_dot_nt(qb, kst_b) * ATT_SCALE)
        dvst_ref[...] += _dot_tn(p_all.astype(BF16), dyeb)
        dp_all = _dot_nt(dyeb, vst_ref[...])
        ds = []
        for h in range(N_SUB):
            blk = slice(h * MEM_LEN, (h + 1) * MEM_LEN)
            p, dpb = p_all[:, blk], dp_all[:, blk]
            ds.append(p * (dpb - jnp.sum(dpb * p, axis=-1, keepdims=True)))
        dsb = (jnp.concatenate(ds, axis=-1) * ATT_SCALE).astype(BF16)
        dproj_ref[:, C_Q : C_Q + D_G] = _dot(dsb, kst_b).astype(BF16)
        dkst_ref[...] += _dot_tn(dsb, qb)

    def rows(width):
        return pl.BlockSpec((tile, width), lambda i: (n_tiles - 1 - i, 0))

    halo_spec = pl.BlockSpec((HALO_D, D_IN), lambda i: (jnp.maximum((n_tiles - 1 - i) * halo_blocks - 1, 0), 0))
    weights = [kst, vst, wout, sw, wcat, wcat_t, poolw, pww, ln]
    acc_shapes = [(D_MIX, D_MODEL), (N_SUB * MEM_LEN, D_G), (N_SUB * MEM_LEN, D_G), (D_G, D_G), (RG_ROWS, D_G)]
    return _gridded_call(
        body,
        name=name,
        steps=n_tiles,
        in_specs=[rows(D_MODEL), rows(D_MODEL), rows(D_IN), halo_spec, rows(D_MIX), rows(D_G)]
        + [_full(a.shape) for a in weights],
        out_specs=[rows(D_IN), rows(D_MODEL)] + [_full(s) for s in acc_shapes],
        out_shape=[jax.ShapeDtypeStruct((seq, D_IN), BF16), jax.ShapeDtypeStruct((seq, D_MODEL), F32)]
        + [jax.ShapeDtypeStruct(s, F32) for s in acc_shapes],
        scratch_shapes=[
            pltpu.VMEM((HALO_A + tile, D_G), F32),
            pltpu.VMEM((tile + HALO_A, D_G), F32),
            pltpu.VMEM((HALO_C, D_G), F32),
            pltpu.VMEM((SHIFTS, HALO_D + tile, D_G), F32),
            pltpu.VMEM((SHIFTS, tile + HALO_D, D_G), F32),
        ],
        operands=[dxn, z, proj, proj, y, cvd, *weights],
        exchange=exchange,
    )


def _kv_forward(mem, wkv, *, name):
    def body(mem_ref, wkv_ref, kst_ref, vst_ref):
        kv = _dot(mem_ref[...].astype(BF16), wkv_ref[...])
        masks = _head_masks()
        kst_ref[...] = _stack_heads(kv[:, 0:D_G], masks).astype(BF16)
        vst_ref[...] = _stack_heads(kv[:, D_G:], masks).astype(BF16)

    shape = jax.ShapeDtypeStruct((N_SUB * MEM_LEN, D_G), BF16)
    return pl.pallas_call(body, name=name, out_shape=[shape, shape])(mem, wkv)


def _kv_backward(mem, dkst, dvst, *, name):
    def body(mem_ref, dkst_ref, dvst_ref, dwkv_ref):
        masks = _head_masks()
        memb = mem_ref[...].astype(BF16)
        for col, ref in ((0, dkst_ref), (D_G, dvst_ref)):
            d = jnp.zeros((MEM_LEN, D_G), F32)
            for h in range(N_SUB):
                d = d + ref[h * MEM_LEN : (h + 1) * MEM_LEN, :] * masks[h]
            dwkv_ref[:, col : col + D_G] = _dot_tn(memb, d.astype(BF16))

    return pl.pallas_call(body, name=name, out_shape=jax.ShapeDtypeStruct((D_MODEL, 2 * D_G), F32))(mem, dkst, dvst)


def _input_grad(dproj, dz, win, *, name, exchange=None):
    seq = dproj.shape[0]
    tile = min(MM_TILE // 2, seq)

    def body(dproj_ref, dz_ref, win_ref, dx_ref):
        acc = ALPHA * dz_ref[...]
        for k in range(N_CHIPS):
            acc = acc + _dot_nt(dproj_ref[:, k * W_IN_SHARD : (k + 1) * W_IN_SHARD], win_ref[k])
        dx_ref[...] = acc

    return _gridded_call(
        body,
        name=name,
        steps=seq // tile,
        in_specs=[
            pl.BlockSpec((tile, D_IN), lambda i: (i, 0)),
            pl.BlockSpec((tile, D_MODEL), lambda i: (i, 0)),
            _full(win.shape),
        ],
        out_specs=[pl.BlockSpec((tile, D_MODEL), lambda i: (i, 0))],
        out_shape=[jax.ShapeDtypeStruct((seq, D_MODEL), F32)],
        scratch_shapes=[],
        operands=[dproj, dz, win],
        exchange=exchange,
    )


def _input_weight_grad(x, dproj, *, name):
    seq = x.shape[0]
    tile = min(MM_TILE, seq)

    def body(x_ref, dproj_ref, dwin_ref):
        @pl.when(pl.program_id(1) == 0)
        def _():
            dwin_ref[...] = jnp.zeros(dwin_ref.shape, F32)

        dwin_ref[0] += _dot_tn(x_ref[...].astype(BF16), dproj_ref[...])

    return pl.pallas_call(
        body,
        name=name,
        grid=(N_CHIPS, seq // tile),
        in_specs=[
            pl.BlockSpec((tile, D_MODEL), lambda k, i: (i, 0)),
            pl.BlockSpec((tile, W_IN_SHARD), lambda k, i: (i, k)),
        ],
        out_specs=pl.BlockSpec((1, D_MODEL, W_IN_SHARD), lambda k, i: (k, 0, 0)),
        out_shape=jax.ShapeDtypeStruct((N_CHIPS, D_MODEL, W_IN_SHARD), F32),
        compiler_params=pltpu.CompilerParams(dimension_semantics=("arbitrary", "arbitrary"), vmem_limit_bytes=VMEM_LIMIT),
    )(x, dproj)


def _expand_sgb(sg_b):
    return jnp.repeat(sg_b.T, HEAD_DIM, axis=1)


def _pack_small_weights(sg_ln_g, sg_ln_b, pool_scale, cc_dw_b, cc_ln_g, cc_ln_b, conv_a_w, cc_dw_w, sg_b):
    vec = jnp.stack([sg_ln_g, sg_ln_b, pool_scale, cc_dw_b, cc_ln_g, cc_ln_b])
    return jnp.concatenate(
        [
            jnp.pad(vec, ((0, RW_CONVA - RW_VEC - 6), (0, 0))),
            jnp.pad(conv_a_w, ((0, RW_DW - RW_CONVA - CONV_A), (0, 0))),
            jnp.pad(cc_dw_w, ((0, RW_SGB - RW_DW - CONV_D), (0, 0))),
            _expand_sgb(sg_b),
        ]
    )


def _sg_w_cat(sg_w):
    cat = jnp.transpose(sg_w, (1, 0, 2)).reshape(CHUNK, N_SUB * CHUNK)
    cat_t = jnp.transpose(sg_w, (2, 0, 1)).reshape(CHUNK, N_SUB * CHUNK)
    return cat, cat_t


def _pool_block_diag(pool_w):
    out = jnp.zeros((D_G, D_G), pool_w.dtype)
    for g in range(N_SUB):
        out = out.at[g * HEAD_DIM : (g + 1) * HEAD_DIM, g * HEAD_DIM : (g + 1) * HEAD_DIM].set(pool_w[g])
    return out


def _prepare_layer(mem, w, l):
    cat, cat_t = _sg_w_cat(w["sg_w"])
    kst, vst = _kv_forward(mem, w["w_kv"], name=f"kv_fwd{l}")
    return dict(
        win=w["w_in"],
        wout=w["w_out"],
        pww=w["cc_pw_w"],
        sw=_pack_small_weights(
            w["sg_ln_g"], w["sg_ln_b"], w["pool_scale"], w["cc_dw_b"], w["cc_ln_g"], w["cc_ln_b"],
            w["conv_a_w"], w["cc_dw_w"], w["sg_b"],
        ),
        wcat=cat,
        wcat_t=cat_t,
        poolw=_pool_block_diag(w["pool_w"]).astype(BF16),
        ln=jnp.stack([w["ln_g"], w["ln_b"]]),
        kst=kst,
        vst=vst,
    )


def _forward(l, h, p, tgt, exchange=None):
    return _layer_forward(
        h, p["win"], p["kst"], p["vst"], p["wout"], p["sw"], p["wcat"], p["poolw"], p["pww"], p["ln"], tgt,
        name=f"layer_fwd{l}", exchange=exchange,
    )


def _backward(l, dxn, s, p, exchange=None):
    return _layer_backward(
        dxn, s[2], s[0], s[1], s[4], p["kst"], p["vst"], p["wout"], p["sw"], p["wcat"], p["wcat_t"], p["poolw"],
        p["pww"], p["ln"], name=f"layer_bwd{l}", exchange=exchange,
    )


def _place():
    x, y, c = lax.axis_index("x"), lax.axis_index("y"), lax.axis_index("c")
    others = [(1 - x, y), (x, 1 - y), (1 - x, 1 - y)]
    return x, y, c, others


def _half(ref, c, axis):
    n = ref.shape[axis] // 2
    if axis == 0:
        return ref.at[pl.ds(c * n, n)]
    return ref.at[:, pl.ds(c * n, n)]


def _place_own_block(place, stacked, layer, dtypes, *, name):
    n = len(stacked)

    def body(place_ref, *refs):
        for a in range(n):
            refs[n + a][...] = refs[a][...].astype(dtypes[a])

    def block(s):
        return (1,) + s.shape[1:]

    return pl.pallas_call(
        body,
        name=name,
        grid_spec=pltpu.PrefetchScalarGridSpec(
            num_scalar_prefetch=1,
            grid=(1,),
            in_specs=[pl.BlockSpec(block(s), lambda i, place_ref: (layer, 0, 0)) for s in stacked],
            out_specs=[pl.BlockSpec(block(s), lambda i, place_ref: (place_ref[1], 0, 0)) for s in stacked],
        ),
        out_shape=[jax.ShapeDtypeStruct((N_CHIPS,) + s.shape[1:], dt) for s, dt in zip(stacked, dtypes)],
        compiler_params=pltpu.CompilerParams(dimension_semantics=("arbitrary",), vmem_limit_bytes=VMEM_LIMIT),
    )(place, *stacked)


def _sds(a):
    return jax.ShapeDtypeStruct(a.shape, a.dtype)


def _gather_exchange(bufs):
    n = len(bufs)

    def remote(sems, block, k, to):
        return pltpu.make_async_remote_copy(
            src_ref=block, dst_ref=block, send_sem=sems[0].at[k], recv_sem=sems[1].at[k], device_id=to, device_id_type=MESH
        )

    def before(step, steps, refs, outs, sems):
        def send():
            x, y, c, others = _place()
            for j, (px, py) in enumerate(others):
                for a in range(n):
                    remote(sems, _half(refs[a].at[2 * x + y], c, 0), 3 * a + j, (px, py, c)).start()

        _when(step == 0, send)

    def after(step, steps, refs, outs, sems):
        def pass_on():
            x, y, c, others = _place()
            for j, (px, py) in enumerate(others):
                for a in range(n):
                    landed = _half(refs[a].at[2 * px + py], c, 0)
                    remote(sems, landed, 3 * a + j, (px, py, c)).wait_recv()
                    remote(sems, landed, 3 * n + 3 * a + j, (x, y, 1 - c)).start()

        def finish():
            x, y, c, others = _place()
            for j, (px, py) in enumerate(others):
                for a in range(n):
                    remote(sems, _half(refs[a].at[2 * px + py], 1 - c, 0), 3 * n + 3 * a + j, (x, y, 1 - c)).wait_recv()
            for a in range(n):
                mine = _half(refs[a].at[2 * x + y], c, 0)
                for k in range(3 * a, 3 * a + 3):
                    remote(sems, mine, k, (x, y, 1 - c)).wait_send()
                    remote(sems, mine, 3 * n + k, (x, y, 1 - c)).wait_send()

        _when(step == (3 * steps) // 4, pass_on)
        _when(step == steps - 1, finish)

    return _Exchange(bufs, [(_sds(b), a) for a, b in enumerate(bufs)], [6 * n, 6 * n], before, after)


def _swap_exchange(grads):
    n = len(grads)

    def copy(refs, outs, sems, a):
        x, y, c, _ = _place()
        return pltpu.make_async_remote_copy(
            src_ref=_half(refs[a], 1 - c, 1), dst_ref=outs[a], send_sem=sems[0].at[a], recv_sem=sems[1].at[a],
            device_id=(x, y, 1 - c), device_id_type=MESH,
        )

    def before(step, steps, refs, outs, sems):
        _when(step == 0, lambda: [copy(refs, outs, sems, a).start() for a in range(n)] and None)

    def after(step, steps, refs, outs, sems):
        _when(step == steps - 1, lambda: [copy(refs, outs, sems, a).wait() for a in range(n)] and None)

    outputs = [(jax.ShapeDtypeStruct((N_CHIPS, g.shape[1] // 2, g.shape[2]), g.dtype), None) for g in grads]
    return _Exchange(grads, outputs, [n, n], before, after)


def _add_sibling_half(place, grads, received, *, name):
    n = len(grads)

    def body(place_ref, *refs):
        k = pl.program_id(0)
        for a in range(n):
            pair = (refs[a][...] + refs[n + a][...]).astype(BF16)
            refs[2 * n + a][...] = pair

            @pl.when(k == place_ref[1])
            def _(a=a, pair=pair):
                refs[3 * n + a][...] = pair

    def block(g):
        return (1, g.shape[1] // 2, g.shape[2])

    return pl.pallas_call(
        body,
        name=name,
        grid_spec=pltpu.PrefetchScalarGridSpec(
            num_scalar_prefetch=1,
            grid=(N_CHIPS,),
            in_specs=[pl.BlockSpec(block(g), lambda k, place_ref: (k, place_ref[0], 0)) for g in grads]
            + [pl.BlockSpec(block(g), lambda k, place_ref: (k, 0, 0)) for g in grads],
            out_specs=[pl.BlockSpec(block(g), lambda k, place_ref: (k, 0, 0)) for g in grads]
            + [pl.BlockSpec(block(g), lambda k, place_ref: (place_ref[1], 0, 0)) for g in grads],
        ),
        out_shape=[jax.ShapeDtypeStruct(r.shape, BF16) for r in received] * 2,
        compiler_params=pltpu.CompilerParams(dimension_semantics=("arbitrary",), vmem_limit_bytes=VMEM_LIMIT),
    )(place, *grads, *received)


def _scatter_exchange(pairs, landing):
    n = len(pairs)

    def copy(refs, sems, a, j, px, py):
        x, y, c, _ = _place()
        return pltpu.make_async_remote_copy(
            src_ref=refs[a].at[2 * px + py], dst_ref=refs[n + a].at[2 * x + y], send_sem=sems[0].at[3 * a + j],
            recv_sem=sems[1].at[3 * a + j], device_id=(px, py, c), device_id_type=MESH,
        )

    def before(step, steps, refs, outs, sems):
        def send():
            for j, (px, py) in enumerate(_place()[3]):
                for a in range(n):
                    copy(refs, sems, a, j, px, py).start()

        _when(step == 0, send)

    def after(step, steps, refs, outs, sems):
        def finish():
            x, y, c, others = _place()
            for j, (px, py) in enumerate(others):
                for a in range(n):
                    landed = refs[n + a].at[2 * px + py]
                    pltpu.make_async_remote_copy(
                        src_ref=landed, dst_ref=landed, send_sem=sems[0].at[3 * a + j], recv_sem=sems[1].at[3 * a + j],
                        device_id=(px, py, c), device_id_type=MESH,
                    ).wait_recv()
            for j, (px, py) in enumerate(others):
                for a in range(n):
                    copy(refs, sems, a, j, px, py).wait_send()

        _when(step == steps - 1, finish)

    return _Exchange(pairs + landing, [(_sds(b), n + a) for a, b in enumerate(landing)], [3 * n, 3 * n], before, after)


SUM_STEPS = 2


def _sum_chip_blocks(place, parts, keep_chip_axis, *, name):
    n = len(parts)

    def body(place_ref, *refs):
        for a in range(n):
            p = refs[a]
            total = (p[0].astype(F32) + p[1].astype(F32)) + (p[2].astype(F32) + p[3].astype(F32))
            if keep_chip_axis[a]:
                refs[n + a][0] = total
            else:
                refs[n + a][...] = total

    def in_spec(p):
        return pl.BlockSpec((N_CHIPS, p.shape[1] // SUM_STEPS, p.shape[2]), lambda i, place_ref: (0, i, 0))

    def out_spec(p, keep):
        rows = p.shape[1] // SUM_STEPS
        if keep:
            return pl.BlockSpec((1, rows, p.shape[2]), lambda i, place_ref: (place_ref[1], place_ref[0] * SUM_STEPS + i, 0))
        return pl.BlockSpec((rows, p.shape[2]), lambda i, place_ref: (place_ref[0] * SUM_STEPS + i, 0))

    def out_shape(p, keep):
        shape = (2 * p.shape[1], p.shape[2])
        return jax.ShapeDtypeStruct((N_CHIPS,) + shape if keep else shape, F32)

    return pl.pallas_call(
        body,
        name=name,
        grid_spec=pltpu.PrefetchScalarGridSpec(
            num_scalar_prefetch=1,
            grid=(SUM_STEPS,),
            in_specs=[in_spec(p) for p in parts],
            out_specs=[out_spec(p, k) for p, k in zip(parts, keep_chip_axis)],
        ),
        out_shape=[out_shape(p, k) for p, k in zip(parts, keep_chip_axis)],
        compiler_params=pltpu.CompilerParams(dimension_semantics=("arbitrary",), vmem_limit_bytes=VMEM_LIMIT),
    )(place, *parts)


def _join_exchange(bufs, keep_chip_axis):
    n = len(bufs)

    def remote(refs, sems, a, cc):
        x, y, c, _ = _place()
        half = _half(refs[a].at[2 * x + y] if keep_chip_axis[a] else refs[a], cc, 0)
        return pltpu.make_async_remote_copy(
            src_ref=half, dst_ref=half, send_sem=sems[0].at[a], recv_sem=sems[1].at[a], device_id=(x, y, 1 - c),
            device_id_type=MESH,
        )

    def before(step, steps, refs, outs, sems):
        def send():
            c = lax.axis_index("c")
            for a in range(n):
                remote(refs, sems, a, c).start()

        _when(step == 0, send)

    def after(step, steps, refs, outs, sems):
        def finish():
            c = lax.axis_index("c")
            for a in range(n):
                remote(refs, sems, a, 1 - c).wait_recv()
            for a in range(n):
                remote(refs, sems, a, c).wait_send()

        _when(step == steps - 1, finish)

    return _Exchange(bufs, [(_sds(b), a) for a, b in enumerate(bufs)], [n, n], before, after)


def _adamw(w, g, m, v):
    m = ADAM_B1 * m + (1.0 - ADAM_B1) * g
    v = ADAM_B2 * v + (1.0 - ADAM_B2) * (g * g)
    m_hat = m / (1.0 - ADAM_B1**ADAM_STEP)
    v_hat = v / (1.0 - ADAM_B2**ADAM_STEP)
    delta = -ADAM_LR * (m_hat / (jnp.sqrt(v_hat) + ADAM_EPS) + ADAM_WD * w)
    return delta, m, v


def _adamw_large(w, m, v, layer_grads, *, name):
    depth, rows, cols = w.shape
    tile = math.gcd(rows, ADAM_TILE)
    assert tile % 8 == 0

    def body(w_ref, m_ref, v_ref, *refs):
        g_refs, (g_out, d_out, m_out, v_out) = refs[:depth], refs[depth:]
        for l in range(depth):

            @pl.when(pl.program_id(0) == l)
            def _(l=l):
                g = g_refs[l][...]
                delta, m_new, v_new = _adamw(w_ref[0], g, m_ref[0], v_ref[0])
                g_out[0], d_out[0], m_out[0], v_out[0] = g, delta, m_new, v_new

    def stacked():
        return pl.BlockSpec((1, tile, cols), lambda l, i: (l, i, 0))

    def layer_spec(l):
        return pl.BlockSpec((tile, cols), lambda k, i: (jnp.where(k == l, i, 0), 0))

    shape = jax.ShapeDtypeStruct(w.shape, F32)
    return pl.pallas_call(
        body,
        name=name,
        grid=(depth, rows // tile),
        in_specs=[stacked(), stacked(), stacked()] + [layer_spec(l) for l in range(depth)],
        out_specs=[stacked()] * 4,
        out_shape=[shape] * 4,
        compiler_params=pltpu.CompilerParams(dimension_semantics=("arbitrary", "arbitrary"), vmem_limit_bytes=VMEM_LIMIT),
    )(w, m, v, *layer_grads)


def _adamw_small(ws, gs, ms, vs, *, name):
    n = len(ws)

    def body(*refs):
        for a in range(n):
            delta, m_new, v_new = _adamw(refs[a][...], refs[n + a][...], refs[2 * n + a][...], refs[3 * n + a][...])
            refs[4 * n + a][...] = delta
            refs[5 * n + a][...] = m_new
            refs[6 * n + a][...] = v_new

    shapes = [jax.ShapeDtypeStruct(w.shape, F32) for w in ws]
    outs = pl.pallas_call(body, name=name, out_shape=shapes * 3)(*ws, *gs, *ms, *vs)
    return outs[:n], outs[n : 2 * n], outs[2 * n :]


WEIGHT_NAMES = (
    "w_in", "conv_a_w", "sg_ln_g", "sg_ln_b", "sg_w", "sg_b", "pool_w", "pool_scale", "cc_dw_w", "cc_dw_b", "cc_ln_g",
    "cc_ln_b", "cc_pw_w", "w_kv", "w_out", "ln_g", "ln_b",
)
LARGE = ("w_in", "cc_pw_w", "w_kv", "w_out")
TAPS_ROWS = 48


def _unpack_small_grads(small, chip):
    out = {}
    for r, k in enumerate(("sg_ln_g", "sg_ln_b", "pool_scale", "cc_dw_b", "cc_ln_g", "cc_ln_b")):
        out[k] = small[RG_VEC + r]
    out["conv_a_w"] = lax.dynamic_slice_in_dim(small[RG_CONVA : RG_CONVA + CONV_A], chip * HEAD_DIM, HEAD_DIM, axis=1)
    out["cc_dw_w"] = lax.dynamic_slice_in_dim(small[RG_DW : RG_DW + CONV_D], chip * HEAD_DIM, HEAD_DIM, axis=1)
    cat = jnp.concatenate([small[RG_SGW : RG_SGW + CHUNK], small[RG_SGW + CHUNK : RG_SGW + 2 * CHUNK]], axis=1)
    out["sg_w"] = jnp.transpose(cat.reshape(CHUNK, N_SUB, CHUNK), (1, 0, 2))
    out["sg_b"] = small[RG_SGB : RG_SGB + CHUNK].reshape(CHUNK, N_SUB, HEAD_DIM).sum(-1).T
    pool = small[RG_POOL : RG_POOL + D_G]
    out["pool_w"] = jnp.stack(
        [pool[g * HEAD_DIM : (g + 1) * HEAD_DIM, g * HEAD_DIM : (g + 1) * HEAD_DIM] for g in range(N_SUB)]
    )
    out["ln_g"] = small[RG_LN : RG_LN + 4].reshape(D_MODEL)
    out["ln_b"] = small[RG_LN + 4 : RG_LN + 8].reshape(D_MODEL)
    return out


def kernel(x, mem, w_in, conv_a_w, sg_ln_g, sg_ln_b, sg_w, sg_b, pool_w, pool_scale, cc_dw_w, cc_dw_b, cc_ln_g, cc_ln_b, cc_pw_w, w_kv, w_out, ln_g, ln_b, loss_target, m_w_in, m_conv_a_w, m_sg_ln_g, m_sg_ln_b, m_sg_w, m_sg_b, m_pool_w, m_pool_scale, m_cc_dw_w, m_cc_dw_b, m_cc_ln_g, m_cc_ln_b, m_cc_pw_w, m_w_kv, m_w_out, m_ln_g, m_ln_b, v_w_in, v_conv_a_w, v_sg_ln_g, v_sg_ln_b, v_sg_w, v_sg_b, v_pool_w, v_pool_scale, v_cc_dw_w, v_cc_dw_b, v_cc_ln_g, v_cc_ln_b, v_cc_pw_w, v_w_kv, v_w_out, v_ln_g, v_ln_b):
    given = dict(locals())
    weights = {k: given[k] for k in WEIGHT_NAMES}
    chip = 2 * lax.axis_index("x") + lax.axis_index("y")
    place = jnp.stack([lax.axis_index("c"), chip]).astype(jnp.int32)

    x0, mem0 = x[0], mem[0]

    taps = jnp.concatenate([conv_a_w, cc_dw_w], axis=1)
    taps = jnp.pad(taps, ((0, 0), (0, TAPS_ROWS - taps.shape[1]), (0, 0)))

    def own_blocks(l):
        return _place_own_block(
            place, [w_in, w_out, w_kv, cc_pw_w, taps], l, [BF16, BF16, BF16, BF16, F32], name=f"place_weights{l}"
        )

    def layer_operands(l, gathered):
        g_in, g_out, g_kv, g_pw, g_taps = gathered
        taps_full = jnp.transpose(g_taps, (1, 0, 2)).reshape(TAPS_ROWS, D_G)
        full = dict(
            w_in=g_in,
            w_out=g_out.reshape(D_MIX, D_MODEL),
            w_kv=g_kv.reshape(D_MODEL, 2 * D_G),
            cc_pw_w=g_pw.reshape(D_G, D_G),
            conv_a_w=taps_full[0:CONV_A],
            cc_dw_w=taps_full[CONV_A : CONV_A + CONV_D],
            **{k: weights[k][l] for k in WEIGHT_NAMES if k not in LARGE + ("conv_a_w", "cc_dw_w")},
        )
        return _prepare_layer(mem0, full, l)

    def layer_grads(l, x_in, p, bwd):
        dproj, _, dwout, dkst, dvst, dpw, small = bwd
        return [
            _input_weight_grad(x_in, dproj, name=f"w_in_grad{l}"),
            dwout.reshape(N_CHIPS, D_MIX // N_CHIPS, D_MODEL),
            _kv_backward(mem0, dkst, dvst, name=f"kv_bwd{l}").reshape(N_CHIPS, D_MODEL // N_CHIPS, 2 * D_G),
            dpw.reshape(N_CHIPS, D_G // N_CHIPS, D_G),
            small.reshape(N_CHIPS, RG_ROWS // N_CHIPS, D_G),
        ]

    n_red = 5
    keep = [False, False, False, False, True]

    def finish_reduce(l, parts):
        halves = _sum_chip_blocks(place, parts, keep, name=f"rs_sum{l}")
        r_in, r_out, r_kv, r_pw, r_small = _run_exchange(_join_exchange(halves, keep), name=f"rs_join{l}")
        (small_all,) = _run_exchange(_gather_exchange([r_small]), name=f"gather_small_grads{l}")
        out = _unpack_small_grads(small_all.reshape(RG_ROWS, D_G), chip)
        out.update(w_in=r_in, w_out=r_out, w_kv=r_kv, cc_pw_w=r_pw)
        return out

    blocks0, blocks1 = own_blocks(0), own_blocks(1)
    p0 = layer_operands(0, _run_exchange(_gather_exchange(blocks0), name="gather_weights0"))
    fwd0, gathered1 = _forward(0, x0, p0, None, exchange=_gather_exchange(blocks1))
    p1 = layer_operands(1, gathered1)
    x1 = fwd0[3]
    fwd1, _ = _forward(1, x1, p1, loss_target[0])
    loss = lax.psum(fwd1[5][0, 0], ("x", "y", "c"))

    bwd1, _ = _backward(1, fwd1[3], fwd1, p1)
    grads1 = layer_grads(1, x1, p1, bwd1)
    (dx1,), received1 = _input_grad(bwd1[0], bwd1[1], p1["win"], name="input_grad1", exchange=_swap_exchange(grads1))
    pairs1 = _add_sibling_half(place, grads1, received1, name="rs_pair1")
    bwd0, parts1 = _backward(0, dx1, fwd0, p0, exchange=_scatter_exchange(pairs1[:n_red], pairs1[n_red:]))
    reduced1 = finish_reduce(1, parts1)
    grads0 = layer_grads(0, x0, p0, bwd0)
    received0 = _run_exchange(_swap_exchange(grads0), name="rs_swap0")
    pairs0 = _add_sibling_half(place, grads0, received0, name="rs_pair0")
    (grad_x,), parts0 = _input_grad(
        bwd0[0], bwd0[1], p0["win"], name="input_grad0", exchange=_scatter_exchange(pairs0[:n_red], pairs0[n_red:])
    )
    reduced = [finish_reduce(0, parts0), reduced1]

    grad, delta, new_m, new_v = {}, {}, {}, {}
    for k in LARGE:
        w3 = weights[k]
        grad[k], delta[k], new_m[k], new_v[k] = _adamw_large(
            w3, given["m_" + k], given["v_" + k], [reduced[l][k] for l in range(DEPTH)], name=f"adamw_{k}"
        )
    small_names = [k for k in WEIGHT_NAMES if k not in LARGE]
    for k in small_names:
        grad[k] = jnp.stack([reduced[l][k] for l in range(DEPTH)])
    d_s, m_s, v_s = _adamw_small(
        [weights[k] for k in small_names],
        [grad[k] for k in small_names],
        [given["m_" + k] for k in small_names],
        [given["v_" + k] for k in small_names],
        name="adamw_small",
    )
    for a, k in enumerate(small_names):
        delta[k], new_m[k], new_v[k] = d_s[a], m_s[a], v_s[a]

    return (
        loss,
        grad_x[None],
        *[grad[k] for k in WEIGHT_NAMES],
        *[delta[k] for k in WEIGHT_NAMES],
        *[new_m[k] for k in WEIGHT_NAMES],
        *[new_v[k] for k in WEIGHT_NAMES],
    )
```

```python
import functools
import math

import jax
import jax.numpy as jnp
from jax import lax
from jax.experimental import pallas as pl
from jax.experimental.pallas import tpu as pltpu

F32 = jnp.float32
BF16 = jnp.bfloat16

D_MODEL = 1024
DEPTH = 2
D_G = 256
D_MIX = 5 * D_G
D_IN = 9 * D_G + D_MIX
N_SUB = 4
HEAD_DIM = 64
CONV_A = 3
CONV_D = 31
CHUNK = 128
MEM_LEN = 256
N_CHIPS = 4
W_IN_SHARD = D_IN // N_CHIPS
LN_EPS = 1e-5
ALPHA = (2.0 * DEPTH) ** 0.25
ATT_SCALE = 1.0 / math.sqrt(HEAD_DIM)
GELU_C = math.sqrt(2.0 / math.pi)
GELU_A = 0.044715

ADAM_LR = 0.001
ADAM_B1 = 0.9
ADAM_B2 = 0.999
ADAM_EPS = 1e-08
ADAM_WD = 0.01
ADAM_STEP = 10

C_XA, C_BA, C_CA, C_U, C_V, C_XC, C_DA, C_DG, C_Q, C_GATE = (D_G * i for i in range(10))

HALO_A = 8
HALO_C = 16
HALO_D = 32

RW_VEC = 0
RW_CONVA = 16
RW_DW = 24
RW_SGB = 56
RW_ROWS = RW_SGB + CHUNK

RG_VEC = 0
RG_CONVA = 16
RG_DW = 24
RG_SGW = 56
RG_SGB = RG_SGW + 2 * CHUNK
RG_POOL = RG_SGB + CHUNK
RG_LN = RG_POOL + D_G
RG_LOSS = 8
RG_ROWS = 768

VMEM_LIMIT = 56 * 1024 * 1024
SEQ_TILE = 256
FWD_TILE = 512
MM_TILE = 1024
ADAM_TILE = 512

MESH = pl.DeviceIdType.MESH
ANY = pl.BlockSpec(memory_space=pl.ANY)
NT = (((1,), (1,)), ((), ()))
TN = (((0,), (0,)), ((), ()))


def _dot(a, b):
    return jnp.dot(a, b, preferred_element_type=F32)


def _dot_nt(a, b):
    return lax.dot_general(a, b, NT, preferred_element_type=F32)


def _dot_tn(a, b):
    return lax.dot_general(a, b, TN, preferred_element_type=F32)


def _full(shape):
    zeros = (0,) * len(shape)
    return pl.BlockSpec(shape, lambda *_: zeros)


class _Exchange:
    def __init__(self, operands, outputs, sem_counts, before, after):
        self.operands, self.outputs, self.sem_counts, self.before, self.after = operands, outputs, sem_counts, before, after

    def specs(self, first_input, first_output):
        aliases = {first_input + src: first_output + j for j, (_, src) in enumerate(self.outputs) if src is not None}
        return (
            [ANY] * len(self.operands),
            [ANY] * len(self.outputs),
            [sds for sds, _ in self.outputs],
            [pltpu.SemaphoreType.DMA((k,)) for k in self.sem_counts],
            aliases,
        )

    def split(self, ins, outs):
        refs = list(ins)
        for j, (_, src) in enumerate(self.outputs):
            if src is not None:
                refs[src] = outs[j]
        return refs


def _when(cond, fn):
    if isinstance(cond, bool):
        if cond:
            fn()
    else:
        pl.when(cond)(fn)


def _run_exchange(exchange, *, name):
    n_in, n_out = len(exchange.operands), len(exchange.outputs)
    in_specs, out_specs, out_shape, sems, aliases = exchange.specs(0, 0)

    def body(*refs):
        ins, outs, sem_refs = refs[:n_in], refs[n_in : n_in + n_out], refs[n_in + n_out :]
        refs = exchange.split(ins, outs)
        exchange.before(0, 1, refs, outs, sem_refs)
        exchange.after(0, 1, refs, outs, sem_refs)

    return pl.pallas_call(
        body, name=name, in_specs=in_specs, out_specs=out_specs, out_shape=out_shape, scratch_shapes=sems,
        input_output_aliases=aliases,
    )(*exchange.operands)


def _gridded_call(body, *, name, steps, in_specs, out_specs, out_shape, scratch_shapes, operands, exchange=None):
    params = pltpu.CompilerParams(dimension_semantics=("arbitrary",), vmem_limit_bytes=VMEM_LIMIT)
    if exchange is None:
        outs = pl.pallas_call(
            body, name=name, grid=(steps,), in_specs=in_specs, out_specs=out_specs, out_shape=out_shape,
            scratch_shapes=scratch_shapes, compiler_params=params,
        )(*operands)
        return list(outs), []
    n_in, n_out, n_scr = len(in_specs), len(out_specs), len(scratch_shapes)
    x_in, x_out = len(exchange.operands), len(exchange.outputs)
    ex_in_specs, ex_out_specs, ex_out_shape, ex_sems, aliases = exchange.specs(n_in, n_out)

    def full(*refs):
        own_in, refs = refs[:n_in], refs[n_in:]
        ex_in, refs = refs[:x_in], refs[x_in:]
        own_out, refs = refs[:n_out], refs[n_out:]
        ex_out, refs = refs[:x_out], refs[x_out:]
        own_scr, sem_refs = refs[:n_scr], refs[n_scr:]
        ex_refs = exchange.split(ex_in, ex_out)
        step = pl.program_id(0)
        exchange.before(step, steps, ex_refs, ex_out, sem_refs)
        body(*own_in, *own_out, *own_scr)
        exchange.after(step, steps, ex_refs, ex_out, sem_refs)

    outs = pl.pallas_call(
        full, name=name, grid=(steps,), in_specs=in_specs + ex_in_specs, out_specs=out_specs + ex_out_specs,
        out_shape=out_shape + ex_out_shape, scratch_shapes=scratch_shapes + ex_sems, input_output_aliases=aliases,
        compiler_params=params,
    )(*operands, *exchange.operands)
    return list(outs[:n_out]), list(outs[n_out:])


def _sigmoid(x):
    return 0.5 * jnp.tanh(0.5 * x) + 0.5


def _gelu(x):
    t = jnp.tanh(GELU_C * (x + GELU_A * x * x * x))
    return 0.5 * x * (1.0 + t), t


def _gelu_grad(x, t):
    return 0.5 * (1.0 + t) + 0.5 * x * (1.0 - t * t) * (GELU_C * (1.0 + 3.0 * GELU_A * x * x))


def _normalize(v):
    mu = jnp.mean(v, axis=-1, keepdims=True)
    d = v - mu
    var = jnp.mean(d * d, axis=-1, keepdims=True)
    rstd = lax.rsqrt(var + LN_EPS)
    return d * rstd, rstd


def _normalize_grad(dhat, hat, rstd):
    m1 = jnp.mean(dhat, axis=-1, keepdims=True)
    m2 = jnp.mean(dhat * hat, axis=-1, keepdims=True)
    return rstd * (dhat - m1 - hat * m2)


def _lane(width=D_G):
    return lax.broadcasted_iota(jnp.int32, (1, width), 1)


def _head_masks():
    head = _lane() // HEAD_DIM
    return [(head == h).astype(F32) for h in range(N_SUB)]


def _stack_heads(v, masks):
    return jnp.concatenate([v * m for m in masks], axis=0)


def _tril_mask_cat():
    t = lax.broadcasted_iota(jnp.int32, (CHUNK, N_SUB * CHUNK), 0)
    s = lax.broadcasted_iota(jnp.int32, (CHUNK, N_SUB * CHUNK), 1) % CHUNK
    return s <= t


def _triu_mask_cat():
    s = lax.broadcasted_iota(jnp.int32, (CHUNK, N_SUB * CHUNK), 0)
    t = lax.broadcasted_iota(jnp.int32, (CHUNK, N_SUB * CHUNK), 1) % CHUNK
    return t >= s


def _pool_select(a2, a4, a8, a16):
    lane = _lane()
    return jnp.where(lane < 64, a2, jnp.where(lane < 128, a4, jnp.where(lane < 192, a8, a16)))


def _pool_inv_count(row0, rows):
    t = row0 + lax.broadcasted_iota(jnp.int32, (HALO_C, D_G), 0)
    lane = lax.broadcasted_iota(jnp.int32, (HALO_C, D_G), 1)
    win = jnp.where(lane < 64, 2, jnp.where(lane < 128, 4, jnp.where(lane < 192, 8, 16)))
    head = 1.0 / jnp.minimum(t + 1, win).astype(F32)
    inv_win = jnp.broadcast_to(_pool_select(0.5, 0.25, 0.125, 0.0625), (rows - HALO_C, D_G))
    return jnp.concatenate([head, inv_win], axis=0)


def _trailing_window_sum(halo, cur):
    e = jnp.concatenate([halo, cur], axis=0)
    s2 = e + pltpu.roll(e, 1, 0)
    s4 = s2 + pltpu.roll(s2, 2, 0)
    s8 = s4 + pltpu.roll(s4, 4, 0)
    s16 = s8 + pltpu.roll(s8, 8, 0)
    return _pool_select(s2, s4, s8, s16)[HALO_C:]


def _leading_window_sum(cur, halo):
    e = jnp.concatenate([cur, halo], axis=0)
    n = e.shape[0]
    s2 = e + pltpu.roll(e, n - 1, 0)
    s4 = s2 + pltpu.roll(s2, n - 2, 0)
    s8 = s4 + pltpu.roll(s4, n - 4, 0)
    s16 = s8 + pltpu.roll(s8, n - 8, 0)
    return _pool_select(s2, s4, s8, s16)[: cur.shape[0]]


def _softmax_blocks(sc):
    out = []
    for h in range(N_SUB):
        s = sc[:, h * MEM_LEN : (h + 1) * MEM_LEN]
        e = jnp.exp(s - jnp.max(s, axis=-1, keepdims=True))
        out.append(e * (1.0 / jnp.sum(e, axis=-1, keepdims=True)))
    return jnp.concatenate(out, axis=-1)


SHIFTS = 8


def _fill_shifts(buf):
    n = buf.shape[1] - SHIFTS
    for r in range(1, SHIFTS):
        buf[r, 0:n, :] = buf[0, r : r + n, :]


def _shifted(buf, off, rows):
    r = off % SHIFTS
    return buf[r, off - r : off - r + rows, :]


def _sgu_mix(vn, wcat_b, sgb, masks):
    vbd = _stack_heads(vn, masks).astype(BF16)
    return _dot(wcat_b, vbd) + sgb, vbd


def _layer_forward(x, win, kst, vst, wout, sw, wcat, poolw, pww, ln, tgt, *, name, exchange=None):
    seq = x.shape[0]
    tile = min(FWD_TILE, seq)
    n_tiles = seq // tile
    last = tgt is not None

    def body(*refs):
        x_ref, win_ref, kst_ref, vst_ref, wout_ref, sw_ref, wcat_ref, pool_ref, pw_ref, ln_ref = refs[:10]
        refs = refs[10:]
        if last:
            tgt_ref, refs = refs[0], refs[1:]
        proj_ref, y_ref, z_ref, out_ref, cvd_ref = refs[:5]
        refs = refs[5:]
        if last:
            loss_ref, refs = refs[0], refs[1:]
        pbuf, xchalo, gbuf = refs
        i = pl.program_id(0)

        @pl.when(i == 0)
        def _():
            pbuf[0:HALO_A, :] = jnp.zeros((HALO_A, D_G), F32)
            xchalo[...] = jnp.zeros((HALO_C, D_G), F32)
            gbuf[0, 0:HALO_D, :] = jnp.zeros((HALO_D, D_G), F32)
            if last:
                loss_ref[...] = jnp.zeros((8, 128), F32)

        xt = x_ref[...]
        xb = xt.astype(BF16)

        blocks = {}

        def project(k):
            blocks[k] = _dot(xb, win_ref[k])
            proj_ref[:, k * W_IN_SHARD : (k + 1) * W_IN_SHARD] = blocks[k]

        def cols(start, width=D_G):
            parts, c = [], start
            while c < start + width:
                k, lo = divmod(c, W_IN_SHARD)
                hi = min(W_IN_SHARD, lo + start + width - c)
                parts.append(blocks[k][:, lo:hi])
                c += hi - lo
            return parts[0] if len(parts) == 1 else jnp.concatenate(parts, axis=1)

        project(0)
        project(1)
        masks = _head_masks()

        pbuf[HALO_A : HALO_A + tile, :] = cols(C_CA) * cols(C_XA)
        cv = jnp.zeros((tile, D_G), F32)
        for k in range(CONV_A):
            off = HALO_A - (CONV_A - 1) + k
            cv = cv + sw_ref[RW_CONVA + k : RW_CONVA + k + 1, :] * pbuf[off : off + tile, :]
        y_ref[:, 0:D_G] = cols(C_BA) * cv
        pbuf[0:HALO_A, :] = pbuf[tile : tile + HALO_A, :]

        project(2)

        ua, _ = _gelu(cols(C_U))
        vg, _ = _gelu(cols(C_V))
        vhat, _ = _normalize(vg)
        vn = vhat * sw_ref[RW_VEC : RW_VEC + 1, :] + sw_ref[RW_VEC + 1 : RW_VEC + 2, :]
        wcat_b = jnp.where(_tril_mask_cat(), wcat_ref[...], 0.0).astype(BF16)
        sgb = sw_ref[RW_SGB : RW_SGB + CHUNK, :]
        for j in range(tile // CHUNK):
            rows = slice(j * CHUNK, (j + 1) * CHUNK)
            mixed, _ = _sgu_mix(vn[rows], wcat_b, sgb, masks)
            y_ref[rows, D_G : 2 * D_G] = ua[rows] * mixed

        xc = cols(C_XC)
        wsum = _trailing_window_sum(xchalo[...], xc)
        pm = wsum * _pool_inv_count(i * tile, tile) - xc
        y_ref[:, 2 * D_G : 3 * D_G] = _dot(pm.astype(BF16), pool_ref[...]) * sw_ref[RW_VEC + 2 : RW_VEC + 3, :]
        xchalo[...] = xc[tile - HALO_C :, :]

        project(3)

        gbuf[0, HALO_D : HALO_D + tile, :] = cols(C_DA) * _sigmoid(cols(C_DG))
        _fill_shifts(gbuf)
        cvd = jnp.zeros((tile, D_G), F32) + sw_ref[RW_VEC + 3 : RW_VEC + 4, :]
        for k in range(CONV_D):
            off = HALO_D - (CONV_D - 1) + k
            cvd = cvd + sw_ref[RW_DW + k : RW_DW + k + 1, :] * _shifted(gbuf, off, tile)
        cvd_ref[...] = cvd
        nhat, _ = _normalize(cvd)
        nrm = nhat * sw_ref[RW_VEC + 4 : RW_VEC + 5, :] + sw_ref[RW_VEC + 5 : RW_VEC + 6, :]
        y_ref[:, 3 * D_G : 4 * D_G] = _dot((nrm * _sigmoid(nrm)).astype(BF16), pw_ref[...])
        gbuf[0, 0:HALO_D, :] = gbuf[0, tile : tile + HALO_D, :]

        qb = cols(C_Q).astype(BF16)
        p_all = _softmax_blocks(_dot_nt(qb, kst_ref[...]) * ATT_SCALE)
        y_ref[:, 4 * D_G : 5 * D_G] = _dot(p_all.astype(BF16), vst_ref[...])

        gate = cols(C_GATE, D_MIX)
        hid = y_ref[...] * (gate * _sigmoid(gate))
        z = ALPHA * xt + _dot(hid.astype(BF16), wout_ref[...])
        z_ref[...] = z
        zhat, _ = _normalize(z)
        xn = zhat * ln_ref[0:1, :] + ln_ref[1:2, :]
        if last:
            err = xn - tgt_ref[...]
            out_ref[...] = err * (1.0 / D_MODEL)
            loss_ref[...] += jnp.sum(err * err) * (0.5 / D_MODEL)
        else:
            out_ref[...] = xn

    def rows(width):
        return pl.BlockSpec((tile, width), lambda i: (i, 0))

    operands = [x, win, kst, vst, wout, sw, wcat, poolw, pww, ln]
    in_specs = [rows(D_MODEL)] + [_full(a.shape) for a in operands[1:]]
    out_shape = [
        jax.ShapeDtypeStruct((seq, D_IN), F32),
        jax.ShapeDtypeStruct((seq, D_MIX), F32),
        jax.ShapeDtypeStruct((seq, D_MODEL), F32),
        jax.ShapeDtypeStruct((seq, D_MODEL), F32),
        jax.ShapeDtypeStruct((seq, D_G), F32),
    ]
    out_specs = [rows(D_IN), rows(D_MIX), rows(D_MODEL), rows(D_MODEL), rows(D_G)]
    if last:
        operands.append(tgt)
        in_specs.append(rows(D_MODEL))
        out_shape.append(jax.ShapeDtypeStruct((8, 128), F32))
        out_specs.append(_full((8, 128)))
    return _gridded_call(
        body,
        name=name,
        steps=n_tiles,
        in_specs=in_specs,
        out_specs=out_specs,
        out_shape=out_shape,
        scratch_shapes=[
            pltpu.VMEM((HALO_A + tile, D_G), F32),
            pltpu.VMEM((HALO_C, D_G), F32),
            pltpu.VMEM((SHIFTS, HALO_D + tile, D_G), F32),
        ],
        operands=operands,
        exchange=exchange,
    )


def _layer_backward(dxn, z, proj, y, cvd, kst, vst, wout, sw, wcat, wcat_t, poolw, pww, ln, *, name, exchange=None):
    seq = dxn.shape[0]
    tile = min(SEQ_TILE, seq)
    n_tiles = seq // tile
    halo_blocks = tile // HALO_D

    def body(
        dxn_ref, z_ref, proj_ref, halo_ref, y_ref, cvd_ref, kst_ref, vst_ref, wout_ref, sw_ref, wcat_ref, wcat_t_ref,
        pool_ref, pw_ref, ln_ref, dproj_ref, dz_ref, dwout_ref, dkst_ref, dvst_ref, dpw_ref, sg_ref,
        pbuf, dcvbuf, rhalo, gbuf, dgbuf,
    ):
        i = pl.program_id(0)
        ti = n_tiles - 1 - i

        @pl.when(i == 0)
        def _():
            dwout_ref[...] = jnp.zeros(dwout_ref.shape, F32)
            dkst_ref[...] = jnp.zeros(dkst_ref.shape, F32)
            dvst_ref[...] = jnp.zeros(dvst_ref.shape, F32)
            dpw_ref[...] = jnp.zeros(dpw_ref.shape, F32)
            sg_ref[...] = jnp.zeros(sg_ref.shape, F32)
            dcvbuf[tile : tile + HALO_A, :] = jnp.zeros((HALO_A, D_G), F32)
            rhalo[...] = jnp.zeros((HALO_C, D_G), F32)
            dgbuf[0, tile : tile + HALO_D, :] = jnp.zeros((HALO_D, D_G), F32)

        def acc_row(row, val):
            sg_ref[row : row + 1, :] += jnp.sum(val, axis=0, keepdims=True)

        masks = _head_masks()
        has_past = (ti > 0).astype(F32)

        zhat, zrstd = _normalize(z_ref[...])
        dxn_t = dxn_ref[...]
        dlg = jnp.sum(dxn_t * zhat, axis=0, keepdims=True)
        dlb = jnp.sum(dxn_t, axis=0, keepdims=True)
        for j in range(D_MODEL // D_G):
            sg_ref[RG_LN + j : RG_LN + j + 1, :] += dlg[:, j * D_G : (j + 1) * D_G]
            sg_ref[RG_LN + 4 + j : RG_LN + 5 + j, :] += dlb[:, j * D_G : (j + 1) * D_G]
        dz = _normalize_grad(dxn_t * ln_ref[0:1, :], zhat, zrstd)
        dz_ref[...] = dz
        dzb = dz.astype(BF16)

        gate = proj_ref[:, C_GATE:]
        sgm = _sigmoid(gate)
        silu = gate * sgm
        yc = y_ref[...]
        dwout_ref[...] += _dot_tn((yc * silu).astype(BF16), dzb)
        dh = _dot_nt(dzb, wout_ref[...])
        dproj_ref[:, C_GATE:] = (dh * yc * (sgm * (1.0 + gate * (1.0 - sgm)))).astype(BF16)
        dy = dh * silu

        dya = dy[:, 0:D_G]
        xa = proj_ref[:, C_XA : C_XA + D_G]
        ba = proj_ref[:, C_BA : C_BA + D_G]
        ca = proj_ref[:, C_CA : C_CA + D_G]
        past = slice(HALO_D - HALO_A, HALO_D)
        pbuf[0:HALO_A, :] = halo_ref[past, C_CA : C_CA + D_G] * halo_ref[past, C_XA : C_XA + D_G] * has_past
        pbuf[HALO_A : HALO_A + tile, :] = ca * xa
        cv = jnp.zeros((tile, D_G), F32)
        for k in range(CONV_A):
            off = HALO_A - (CONV_A - 1) + k
            cv = cv + sw_ref[RW_CONVA + k : RW_CONVA + k + 1, :] * pbuf[off : off + tile, :]
        dproj_ref[:, C_BA : C_BA + D_G] = (dya * cv).astype(BF16)
        dcv = dya * ba
        dcvbuf[0:tile, :] = dcv
        dp = jnp.zeros((tile, D_G), F32)
        for k in range(CONV_A):
            off = HALO_A - (CONV_A - 1) + k
            acc_row(RG_CONVA + k, dcv * pbuf[off : off + tile, :])
            back = CONV_A - 1 - k
            dp = dp + sw_ref[RW_CONVA + k : RW_CONVA + k + 1, :] * dcvbuf[back : back + tile, :]
        dproj_ref[:, C_CA : C_CA + D_G] = (dp * xa).astype(BF16)
        dproj_ref[:, C_XA : C_XA + D_G] = (dp * ca).astype(BF16)
        dcvbuf[tile : tile + HALO_A, :] = dcvbuf[0:HALO_A, :]

        dyb = dy[:, D_G : 2 * D_G]
        u = proj_ref[:, C_U : C_U + D_G]
        v = proj_ref[:, C_V : C_V + D_G]
        ua, ut = _gelu(u)
        vg, vt = _gelu(v)
        vhat, vrstd = _normalize(vg)
        sg_g = sw_ref[RW_VEC : RW_VEC + 1, :]
        vn = vhat * sg_g + sw_ref[RW_VEC + 1 : RW_VEC + 2, :]
        tril = _tril_mask_cat()
        wcat_b = jnp.where(tril, wcat_ref[...], 0.0).astype(BF16)
        wcat_tb = jnp.where(_triu_mask_cat(), wcat_t_ref[...], 0.0).astype(BF16)
        sgb = sw_ref[RW_SGB : RW_SGB + CHUNK, :]
        dmixed = dyb * ua
        dvn_parts = []
        du_parts = []
        dwcat = jnp.zeros((CHUNK, N_SUB * CHUNK), F32)
        dsgb = jnp.zeros((CHUNK, D_G), F32)
        for j in range(tile // CHUNK):
            rows = slice(j * CHUNK, (j + 1) * CHUNK)
            mixed, vbd = _sgu_mix(vn[rows], wcat_b, sgb, masks)
            du_parts.append(dyb[rows] * mixed)
            dmx = dmixed[rows]
            dsgb = dsgb + dmx
            dwcat = dwcat + _dot_nt(dmx.astype(BF16), vbd)
            dvn_parts.append(_dot(wcat_tb, _stack_heads(dmx, masks).astype(BF16)))
        dwcat = jnp.where(tril, dwcat, 0.0)
        sg_ref[RG_SGW : RG_SGW + CHUNK, :] += dwcat[:, 0:D_G]
        sg_ref[RG_SGW + CHUNK : RG_SGW + 2 * CHUNK, :] += dwcat[:, D_G:]
        sg_ref[RG_SGB : RG_SGB + CHUNK, :] += dsgb
        dvn = jnp.concatenate(dvn_parts, axis=0)
        du_act = jnp.concatenate(du_parts, axis=0)
        acc_row(RG_VEC, dvn * vhat)
        acc_row(RG_VEC + 1, dvn)
        dvg = _normalize_grad(dvn * sg_g, vhat, vrstd)
        dproj_ref[:, C_U : C_U + D_G] = (du_act * _gelu_grad(u, ut)).astype(BF16)
        dproj_ref[:, C_V : C_V + D_G] = (dvg * _gelu_grad(v, vt)).astype(BF16)

        dyc = dy[:, 2 * D_G : 3 * D_G]
        xc = proj_ref[:, C_XC : C_XC + D_G]
        xc_past = halo_ref[HALO_D - HALO_C : HALO_D, C_XC : C_XC + D_G] * has_past
        inv_cnt = _pool_inv_count(ti * tile, tile)
        pm = _trailing_window_sum(xc_past, xc) * inv_cnt - xc
        pmb = pm.astype(BF16)
        pool_b = pool_ref[...]
        scale = sw_ref[RW_VEC + 2 : RW_VEC + 3, :]
        acc_row(RG_VEC + 2, dyc * _dot(pmb, pool_b))
        dpre = (dyc * scale).astype(BF16)
        sg_ref[RG_POOL : RG_POOL + D_G, :] += _dot_tn(pmb, dpre)
        dpm = _dot_nt(dpre, pool_b)
        r = dpm * inv_cnt
        dproj_ref[:, C_XC : C_XC + D_G] = (_leading_window_sum(r, rhalo[...]) - dpm).astype(BF16)
        rhalo[...] = r[0:HALO_C, :]

        dyd = dy[:, 3 * D_G : 4 * D_G]
        da = proj_ref[:, C_DA : C_DA + D_G]
        sgd = _sigmoid(proj_ref[:, C_DG : C_DG + D_G])
        gbuf[0, 0:HALO_D, :] = halo_ref[:, C_DA : C_DA + D_G] * _sigmoid(halo_ref[:, C_DG : C_DG + D_G]) * has_past
        gbuf[0, HALO_D : HALO_D + tile, :] = da * sgd
        _fill_shifts(gbuf)
        nhat, nrstd = _normalize(cvd_ref[...])
        cc_g = sw_ref[RW_VEC + 4 : RW_VEC + 5, :]
        nrm = nhat * cc_g + sw_ref[RW_VEC + 5 : RW_VEC + 6, :]
        sgn = _sigmoid(nrm)
        dydb = dyd.astype(BF16)
        dpw_ref[...] += _dot_tn((nrm * sgn).astype(BF16), dydb)
        dn = _dot_nt(dydb, pw_ref[...]) * (sgn * (1.0 + nrm * (1.0 - sgn)))
        acc_row(RG_VEC + 4, dn * nhat)
        acc_row(RG_VEC + 5, dn)
        dcvd = _normalize_grad(dn * cc_g, nhat, nrstd)
        acc_row(RG_VEC + 3, dcvd)
        dgbuf[0, 0:tile, :] = dcvd
        _fill_shifts(dgbuf)
        dg = jnp.zeros((tile, D_G), F32)
        for k in range(CONV_D):
            off = HALO_D - (CONV_D - 1) + k
            acc_row(RG_DW + k, dcvd * _shifted(gbuf, off, tile))
            back = CONV_D - 1 - k
            dg = dg + sw_ref[RW_DW + k : RW_DW + k + 1, :] * _shifted(dgbuf, back, tile)
        dproj_ref[:, C_DA : C_DA + D_G] = (dg * sgd).astype(BF16)
        dproj_ref[:, C_DG : C_DG + D_G] = (dg * da * sgd * (1.0 - sgd)).astype(BF16)
        dgbuf[0, tile : tile + HALO_D, :] = dgbuf[0, 0:HALO_D, :]

        dyeb = dy[:, 4 * D_G : 5 * D_G].astype(BF16)
        qb = proj_ref[:, C_Q : C_Q + D_G].astype(BF16)
        kst_b = kst_ref[...]
        p_all = _softmax_blocks(_dot_nt(qb, kst_b) * ATT_SCALE)
        dvst_ref[...] += _dot_tn(p_all.astype(BF16), dyeb)
        dp_all = _dot_nt(dyeb, vst_ref[...])
        ds = []
        for h in range(N_SUB):
            blk = slice(h * MEM_LEN, (h + 1) * MEM_LEN)
            p, dpb = p_all[:, blk], dp_all[:, blk]
            ds.append(p * (dpb - jnp.sum(dpb * p, axis=-1, keepdims=True)))
        dsb = (jnp.concatenate(ds, axis=-1) * ATT_SCALE).astype(BF16)
        dproj_ref[:, C_Q : C_Q + D_G] = _dot(dsb, kst_b).astype(BF16)
        dkst_ref[...] += _dot_tn(dsb, qb)

    def rows(width):
        return pl.BlockSpec((tile, width), lambda i: (n_tiles - 1 - i, 0))

    halo_spec = pl.BlockSpec((HALO_D, D_IN), lambda i: (jnp.maximum((n_tiles - 1 - i) * halo_blocks - 1, 0), 0))
    weights = [kst, vst, wout, sw, wcat, wcat_t, poolw, pww, ln]
    acc_shapes = [(D_MIX, D_MODEL), (N_SUB * MEM_LEN, D_G), (N_SUB * MEM_LEN, D_G), (D_G, D_G), (RG_ROWS, D_G)]
    return _gridded_call(
        body,
        name=name,
        steps=n_tiles,
        in_specs=[rows(D_MODEL), rows(D_MODEL), rows(D_IN), halo_spec, rows(D_MIX), rows(D_G)]
        + [_full(a.shape) for a in weights],
        out_specs=[rows(D_IN), rows(D_MODEL)] + [_full(s) for s in acc_shapes],
        out_shape=[jax.ShapeDtypeStruct((seq, D_IN), BF16), jax.ShapeDtypeStruct((seq, D_MODEL), F32)]
        + [jax.ShapeDtypeStruct(s, F32) for s in acc_shapes],
        scratch_shapes=[
            pltpu.VMEM((HALO_A + tile, D_G), F32),
            pltpu.VMEM((tile + HALO_A, D_G), F32),
            pltpu.VMEM((HALO_C, D_G), F32),
            pltpu.VMEM((SHIFTS, HALO_D + tile, D_G), F32),
            pltpu.VMEM((SHIFTS, tile + HALO_D, D_G), F32),
        ],
        operands=[dxn, z, proj, proj, y, cvd, *weights],
        exchange=exchange,
    )


def _kv_forward(mem, wkv, *, name):
    def body(mem_ref, wkv_ref, kst_ref, vst_ref):
        kv = _dot(mem_ref[...].astype(BF16), wkv_ref[...])
        masks = _head_masks()
        kst_ref[...] = _stack_heads(kv[:, 0:D_G], masks).astype(BF16)
        vst_ref[...] = _stack_heads(kv[:, D_G:], masks).astype(BF16)

    shape = jax.ShapeDtypeStruct((N_SUB * MEM_LEN, D_G), BF16)
    return pl.pallas_call(body, name=name, out_shape=[shape, shape])(mem, wkv)


def _kv_backward(mem, dkst, dvst, *, name):
    def body(mem_ref, dkst_ref, dvst_ref, dwkv_ref):
        masks = _head_masks()
        memb = mem_ref[...].astype(BF16)
        for col, ref in ((0, dkst_ref), (D_G, dvst_ref)):
            d = jnp.zeros((MEM_LEN, D_G), F32)
            for h in range(N_SUB):
                d = d + ref[h * MEM_LEN : (h + 1) * MEM_LEN, :] * masks[h]
            dwkv_ref[:, col : col + D_G] = _dot_tn(memb, d.astype(BF16))

    return pl.pallas_call(body, name=name, out_shape=jax.ShapeDtypeStruct((D_MODEL, 2 * D_G), F32))(mem, dkst, dvst)


def _input_grad(dproj, dz, win, *, name, exchange=None):
    seq = dproj.shape[0]
    tile = min(MM_TILE // 2, seq)

    def body(dproj_ref, dz_ref, win_ref, dx_ref):
        acc = ALPHA * dz_ref[...]
        for k in range(N_CHIPS):
            acc = acc + _dot_nt(dproj_ref[:, k * W_IN_SHARD : (k + 1) * W_IN_SHARD], win_ref[k])
        dx_ref[...] = acc

    return _gridded_call(
        body,
        name=name,
        steps=seq // tile,
        in_specs=[
            pl.BlockSpec((tile, D_IN), lambda i: (i, 0)),
            pl.BlockSpec((tile, D_MODEL), lambda i: (i, 0)),
            _full(win.shape),
        ],
        out_specs=[pl.BlockSpec((tile, D_MODEL), lambda i: (i, 0))],
        out_shape=[jax.ShapeDtypeStruct((seq, D_MODEL), F32)],
        scratch_shapes=[],
        operands=[dproj, dz, win],
        exchange=exchange,
    )


def _input_weight_grad(x, dproj, *, name, exchange=None):
    seq = x.shape[0]
    tile = min(MM_TILE, seq)
    n_rows = seq // tile

    def body(x_ref, dproj_ref, dwin_ref):
        @pl.when(pl.program_id(0) % n_rows == 0)
        def _():
            dwin_ref[...] = jnp.zeros(dwin_ref.shape, F32)

        dwin_ref[0] += _dot_tn(x_ref[...].astype(BF16), dproj_ref[...])

    return _gridded_call(
        body,
        name=name,
        steps=N_CHIPS * n_rows,
        in_specs=[
            pl.BlockSpec((tile, D_MODEL), lambda s: (s % n_rows, 0)),
            pl.BlockSpec((tile, W_IN_SHARD), lambda s: (s % n_rows, s // n_rows)),
        ],
        out_specs=[pl.BlockSpec((1, D_MODEL, W_IN_SHARD), lambda s: (s // n_rows, 0, 0))],
        out_shape=[jax.ShapeDtypeStruct((N_CHIPS, D_MODEL, W_IN_SHARD), F32)],
        scratch_shapes=[],
        operands=[x, dproj],
        exchange=exchange,
    )


def _expand_sgb(sg_b):
    return jnp.repeat(sg_b.T, HEAD_DIM, axis=1)


def _pack_small_weights(sg_ln_g, sg_ln_b, pool_scale, cc_dw_b, cc_ln_g, cc_ln_b, conv_a_w, cc_dw_w, sg_b):
    vec = jnp.stack([sg_ln_g, sg_ln_b, pool_scale, cc_dw_b, cc_ln_g, cc_ln_b])
    return jnp.concatenate(
        [
            jnp.pad(vec, ((0, RW_CONVA - RW_VEC - 6), (0, 0))),
            jnp.pad(conv_a_w, ((0, RW_DW - RW_CONVA - CONV_A), (0, 0))),
            jnp.pad(cc_dw_w, ((0, RW_SGB - RW_DW - CONV_D), (0, 0))),
            _expand_sgb(sg_b),
        ]
    )


def _sg_w_cat(sg_w):
    cat = jnp.transpose(sg_w, (1, 0, 2)).reshape(CHUNK, N_SUB * CHUNK)
    cat_t = jnp.transpose(sg_w, (2, 0, 1)).reshape(CHUNK, N_SUB * CHUNK)
    return cat, cat_t


def _pool_block_diag(pool_w):
    out = jnp.zeros((D_G, D_G), pool_w.dtype)
    for g in range(N_SUB):
        out = out.at[g * HEAD_DIM : (g + 1) * HEAD_DIM, g * HEAD_DIM : (g + 1) * HEAD_DIM].set(pool_w[g])
    return out


def _prepare_layer(mem, w, l):
    cat, cat_t = _sg_w_cat(w["sg_w"])
    kst, vst = _kv_forward(mem, w["w_kv"], name=f"kv_fwd{l}")
    return dict(
        win=w["w_in"],
        wout=w["w_out"],
        pww=w["cc_pw_w"],
        sw=_pack_small_weights(
            w["sg_ln_g"], w["sg_ln_b"], w["pool_scale"], w["cc_dw_b"], w["cc_ln_g"], w["cc_ln_b"],
            w["conv_a_w"], w["cc_dw_w"], w["sg_b"],
        ),
        wcat=cat,
        wcat_t=cat_t,
        poolw=_pool_block_diag(w["pool_w"]).astype(BF16),
        ln=jnp.stack([w["ln_g"], w["ln_b"]]),
        kst=kst,
        vst=vst,
    )


def _forward(l, h, p, tgt, exchange=None):
    return _layer_forward(
        h, p["win"], p["kst"], p["vst"], p["wout"], p["sw"], p["wcat"], p["poolw"], p["pww"], p["ln"], tgt,
        name=f"layer_fwd{l}", exchange=exchange,
    )


def _backward(l, dxn, s, p, exchange=None):
    return _layer_backward(
        dxn, s[2], s[0], s[1], s[4], p["kst"], p["vst"], p["wout"], p["sw"], p["wcat"], p["wcat_t"], p["poolw"],
        p["pww"], p["ln"], name=f"layer_bwd{l}", exchange=exchange,
    )


def _place():
    x, y, c = lax.axis_index("x"), lax.axis_index("y"), lax.axis_index("c")
    others = [(1 - x, y), (x, 1 - y), (1 - x, 1 - y)]
    return x, y, c, others


def _half(ref, c, axis):
    n = ref.shape[axis] // 2
    if axis == 0:
        return ref.at[pl.ds(c * n, n)]
    return ref.at[:, pl.ds(c * n, n)]


def _place_own_block(place, stacked, layer, dtypes, *, name):
    n = len(stacked)

    def body(place_ref, *refs):
        for a in range(n):
            refs[n + a][...] = refs[a][...].astype(dtypes[a])

    def block(s):
        return (1,) + s.shape[1:]

    return pl.pallas_call(
        body,
        name=name,
        grid_spec=pltpu.PrefetchScalarGridSpec(
            num_scalar_prefetch=1,
            grid=(1,),
            in_specs=[pl.BlockSpec(block(s), lambda i, place_ref: (layer, 0, 0)) for s in stacked],
            out_specs=[pl.BlockSpec(block(s), lambda i, place_ref: (place_ref[1], 0, 0)) for s in stacked],
        ),
        out_shape=[jax.ShapeDtypeStruct((N_CHIPS,) + s.shape[1:], dt) for s, dt in zip(stacked, dtypes)],
        compiler_params=pltpu.CompilerParams(dimension_semantics=("arbitrary",), vmem_limit_bytes=VMEM_LIMIT),
    )(place, *stacked)


def _sds(a):
    return jax.ShapeDtypeStruct(a.shape, a.dtype)


def _gather_exchange(bufs):
    n = len(bufs)

    def remote(sems, block, k, to):
        return pltpu.make_async_remote_copy(
            src_ref=block, dst_ref=block, send_sem=sems[0].at[k], recv_sem=sems[1].at[k], device_id=to, device_id_type=MESH
        )

    def before(step, steps, refs, outs, sems):
        def send():
            x, y, c, others = _place()
            for j, (px, py) in enumerate(others):
                for a in range(n):
                    remote(sems, _half(refs[a].at[2 * x + y], c, 0), 3 * a + j, (px, py, c)).start()

        _when(step == 0, send)

    def after(step, steps, refs, outs, sems):
        def pass_on():
            x, y, c, others = _place()
            for j, (px, py) in enumerate(others):
                for a in range(n):
                    landed = _half(refs[a].at[2 * px + py], c, 0)
                    remote(sems, landed, 3 * a + j, (px, py, c)).wait_recv()
                    remote(sems, landed, 3 * n + 3 * a + j, (x, y, 1 - c)).start()

        def finish():
            x, y, c, others = _place()
            for j, (px, py) in enumerate(others):
                for a in range(n):
                    remote(sems, _half(refs[a].at[2 * px + py], 1 - c, 0), 3 * n + 3 * a + j, (x, y, 1 - c)).wait_recv()
            for a in range(n):
                mine = _half(refs[a].at[2 * x + y], c, 0)
                for k in range(3 * a, 3 * a + 3):
                    remote(sems, mine, k, (x, y, 1 - c)).wait_send()
                    remote(sems, mine, 3 * n + k, (x, y, 1 - c)).wait_send()

        _when(step == (3 * steps) // 4, pass_on)
        _when(step == steps - 1, finish)

    return _Exchange(bufs, [(_sds(b), a) for a, b in enumerate(bufs)], [6 * n, 6 * n], before, after)


def _swap_exchange(grads):
    n = len(grads)

    def copy(refs, outs, sems, a):
        x, y, c, _ = _place()
        return pltpu.make_async_remote_copy(
            src_ref=_half(refs[a], 1 - c, 1), dst_ref=outs[a], send_sem=sems[0].at[a], recv_sem=sems[1].at[a],
            device_id=(x, y, 1 - c), device_id_type=MESH,
        )

    def before(step, steps, refs, outs, sems):
        _when(step == 0, lambda: [copy(refs, outs, sems, a).start() for a in range(n)] and None)

    def after(step, steps, refs, outs, sems):
        _when(step == steps - 1, lambda: [copy(refs, outs, sems, a).wait() for a in range(n)] and None)

    outputs = [(jax.ShapeDtypeStruct((N_CHIPS, g.shape[1] // 2, g.shape[2]), g.dtype), None) for g in grads]
    return _Exchange(grads, outputs, [n, n], before, after)


def _add_sibling_half(place, grads, received, wire, *, name):
    n = len(grads)

    def body(place_ref, *refs):
        k = pl.program_id(0)
        for a in range(n):
            pair = (refs[a][...] + refs[n + a][...]).astype(wire[a])
            refs[2 * n + a][...] = pair

            @pl.when(k == place_ref[1])
            def _(a=a, pair=pair):
                refs[3 * n + a][...] = pair

    def block(g):
        return (1, g.shape[1] // 2, g.shape[2])

    return pl.pallas_call(
        body,
        name=name,
        grid_spec=pltpu.PrefetchScalarGridSpec(
            num_scalar_prefetch=1,
            grid=(N_CHIPS,),
            in_specs=[pl.BlockSpec(block(g), lambda k, place_ref: (k, place_ref[0], 0)) for g in grads]
            + [pl.BlockSpec(block(g), lambda k, place_ref: (k, 0, 0)) for g in grads],
            out_specs=[pl.BlockSpec(block(g), lambda k, place_ref: (k, 0, 0)) for g in grads]
            + [pl.BlockSpec(block(g), lambda k, place_ref: (place_ref[1], 0, 0)) for g in grads],
        ),
        out_shape=[jax.ShapeDtypeStruct(r.shape, dt) for r, dt in zip(received, wire)] * 2,
        compiler_params=pltpu.CompilerParams(dimension_semantics=("arbitrary",), vmem_limit_bytes=VMEM_LIMIT),
    )(place, *grads, *received)


def _scatter_exchange(pairs, landing):
    n = len(pairs)

    def copy(refs, sems, a, j, px, py):
        x, y, c, _ = _place()
        return pltpu.make_async_remote_copy(
            src_ref=refs[a].at[2 * px + py], dst_ref=refs[n + a].at[2 * x + y], send_sem=sems[0].at[3 * a + j],
            recv_sem=sems[1].at[3 * a + j], device_id=(px, py, c), device_id_type=MESH,
        )

    def before(step, steps, refs, outs, sems):
        def send():
            for j, (px, py) in enumerate(_place()[3]):
                for a in range(n):
                    copy(refs, sems, a, j, px, py).start()

        _when(step == 0, send)

    def after(step, steps, refs, outs, sems):
        def finish():
            x, y, c, others = _place()
            for j, (px, py) in enumerate(others):
                for a in range(n):
                    landed = refs[n + a].at[2 * px + py]
                    pltpu.make_async_remote_copy(
                        src_ref=landed, dst_ref=landed, send_sem=sems[0].at[3 * a + j], recv_sem=sems[1].at[3 * a + j],
                        device_id=(px, py, c), device_id_type=MESH,
                    ).wait_recv()
            for j, (px, py) in enumerate(others):
                for a in range(n):
                    copy(refs, sems, a, j, px, py).wait_send()

        _when(step == steps - 1, finish)

    return _Exchange(pairs + landing, [(_sds(b), n + a) for a, b in enumerate(landing)], [3 * n, 3 * n], before, after)


SUM_STEPS = 2


def _sum_chip_blocks(place, parts, keep_chip_axis, *, name):
    n = len(parts)

    def body(place_ref, *refs):
        for a in range(n):
            p = refs[a]
            total = (p[0].astype(F32) + p[1].astype(F32)) + (p[2].astype(F32) + p[3].astype(F32))
            if keep_chip_axis[a]:
                refs[n + a][0] = total
            else:
                refs[n + a][...] = total

    def in_spec(p):
        return pl.BlockSpec((N_CHIPS, p.shape[1] // SUM_STEPS, p.shape[2]), lambda i, place_ref: (0, i, 0))

    def out_spec(p, keep):
        rows = p.shape[1] // SUM_STEPS
        if keep:
            return pl.BlockSpec((1, rows, p.shape[2]), lambda i, place_ref: (place_ref[1], place_ref[0] * SUM_STEPS + i, 0))
        return pl.BlockSpec((rows, p.shape[2]), lambda i, place_ref: (place_ref[0] * SUM_STEPS + i, 0))

    def out_shape(p, keep):
        shape = (2 * p.shape[1], p.shape[2])
        return jax.ShapeDtypeStruct((N_CHIPS,) + shape if keep else shape, F32)

    return pl.pallas_call(
        body,
        name=name,
        grid_spec=pltpu.PrefetchScalarGridSpec(
            num_scalar_prefetch=1,
            grid=(SUM_STEPS,),
            in_specs=[in_spec(p) for p in parts],
            out_specs=[out_spec(p, k) for p, k in zip(parts, keep_chip_axis)],
        ),
        out_shape=[out_shape(p, k) for p, k in zip(parts, keep_chip_axis)],
        compiler_params=pltpu.CompilerParams(dimension_semantics=("arbitrary",), vmem_limit_bytes=VMEM_LIMIT),
    )(place, *parts)


def _join_exchange(bufs, keep_chip_axis):
    n = len(bufs)
    kept = [a for a in range(n) if keep_chip_axis[a]]
    base = n

    def copy(refs, sems, block, k, to):
        return pltpu.make_async_remote_copy(
            src_ref=block, dst_ref=block, send_sem=sems[0].at[k], recv_sem=sems[1].at[k], device_id=to, device_id_type=MESH
        )

    def mine(refs, a, cc):
        x, y, _, _ = _place()
        return _half(refs[a].at[2 * x + y] if keep_chip_axis[a] else refs[a], cc, 0)

    def before(step, steps, refs, outs, sems):
        def send():
            x, y, c, others = _place()
            for a in range(n):
                copy(refs, sems, mine(refs, a, c), a, (x, y, 1 - c)).start()
            for i, a in enumerate(kept):
                for j, (px, py) in enumerate(others):
                    copy(refs, sems, mine(refs, a, c), base + 6 * i + j, (px, py, c)).start()

        _when(step == 0, send)

    def after(step, steps, refs, outs, sems):
        def pass_on():
            x, y, c, others = _place()
            for i, a in enumerate(kept):
                for j, (px, py) in enumerate(others):
                    landed = _half(refs[a].at[2 * px + py], c, 0)
                    copy(refs, sems, landed, base + 6 * i + j, (px, py, c)).wait_recv()
                    copy(refs, sems, landed, base + 6 * i + 3 + j, (x, y, 1 - c)).start()

        def finish():
            x, y, c, others = _place()
            for a in range(n):
                copy(refs, sems, mine(refs, a, 1 - c), a, (x, y, 1 - c)).wait_recv()
            for i, a in enumerate(kept):
                for j, (px, py) in enumerate(others):
                    passed = _half(refs[a].at[2 * px + py], 1 - c, 0)
                    copy(refs, sems, passed, base + 6 * i + 3 + j, (x, y, 1 - c)).wait_recv()
            for a in range(n):
                copy(refs, sems, mine(refs, a, c), a, (x, y, 1 - c)).wait_send()
            for i, a in enumerate(kept):
                for k in range(6):
                    copy(refs, sems, mine(refs, a, c), base + 6 * i + k, (x, y, 1 - c)).wait_send()

        _when(step == steps // 2, pass_on)
        _when(step == steps - 1, finish)

    return _Exchange(bufs, [(_sds(b), a) for a, b in enumerate(bufs)], [n + 6 * len(kept)] * 2, before, after)


def _adamw(w, g, m, v):
    m = ADAM_B1 * m + (1.0 - ADAM_B1) * g
    v = ADAM_B2 * v + (1.0 - ADAM_B2) * (g * g)
    m_hat = m / (1.0 - ADAM_B1**ADAM_STEP)
    v_hat = v / (1.0 - ADAM_B2**ADAM_STEP)
    delta = -ADAM_LR * (m_hat / (jnp.sqrt(v_hat) + ADAM_EPS) + ADAM_WD * w)
    return delta, m, v


def _adamw_large(w, m, v, layer_grads, *, name):
    depth, rows, cols = w.shape
    tile = math.gcd(rows, ADAM_TILE)
    assert tile % 8 == 0

    def body(w_ref, m_ref, v_ref, *refs):
        g_refs, (g_out, d_out, m_out, v_out) = refs[:depth], refs[depth:]
        for l in range(depth):

            @pl.when(pl.program_id(0) == l)
            def _(l=l):
                g = g_refs[l][...]
                delta, m_new, v_new = _adamw(w_ref[0], g, m_ref[0], v_ref[0])
                g_out[0], d_out[0], m_out[0], v_out[0] = g, delta, m_new, v_new

    def stacked():
        return pl.BlockSpec((1, tile, cols), lambda l, i: (l, i, 0))

    def layer_spec(l):
        return pl.BlockSpec((tile, cols), lambda k, i: (jnp.where(k == l, i, 0), 0))

    shape = jax.ShapeDtypeStruct(w.shape, F32)
    return pl.pallas_call(
        body,
        name=name,
        grid=(depth, rows // tile),
        in_specs=[stacked(), stacked(), stacked()] + [layer_spec(l) for l in range(depth)],
        out_specs=[stacked()] * 4,
        out_shape=[shape] * 4,
        compiler_params=pltpu.CompilerParams(dimension_semantics=("arbitrary", "arbitrary"), vmem_limit_bytes=VMEM_LIMIT),
    )(w, m, v, *layer_grads)


def _adamw_small(ws, gs, ms, vs, *, name):
    n = len(ws)

    def body(*refs):
        for a in range(n):
            delta, m_new, v_new = _adamw(refs[a][...], refs[n + a][...], refs[2 * n + a][...], refs[3 * n + a][...])
            refs[4 * n + a][...] = delta
            refs[5 * n + a][...] = m_new
            refs[6 * n + a][...] = v_new

    shapes = [jax.ShapeDtypeStruct(w.shape, F32) for w in ws]
    outs = pl.pallas_call(body, name=name, out_shape=shapes * 3)(*ws, *gs, *ms, *vs)
    return outs[:n], outs[n : 2 * n], outs[2 * n :]


WEIGHT_NAMES = (
    "w_in", "conv_a_w", "sg_ln_g", "sg_ln_b", "sg_w", "sg_b", "pool_w", "pool_scale", "cc_dw_w", "cc_dw_b", "cc_ln_g",
    "cc_ln_b", "cc_pw_w", "w_kv", "w_out", "ln_g", "ln_b",
)
LARGE = ("w_in", "cc_pw_w", "w_kv", "w_out")
TAPS_ROWS = 48


def _unpack_small_grads(small, chip):
    out = {}
    for r, k in enumerate(("sg_ln_g", "sg_ln_b", "pool_scale", "cc_dw_b", "cc_ln_g", "cc_ln_b")):
        out[k] = small[RG_VEC + r]
    out["conv_a_w"] = lax.dynamic_slice_in_dim(small[RG_CONVA : RG_CONVA + CONV_A], chip * HEAD_DIM, HEAD_DIM, axis=1)
    out["cc_dw_w"] = lax.dynamic_slice_in_dim(small[RG_DW : RG_DW + CONV_D], chip * HEAD_DIM, HEAD_DIM, axis=1)
    cat = jnp.concatenate([small[RG_SGW : RG_SGW + CHUNK], small[RG_SGW + CHUNK : RG_SGW + 2 * CHUNK]], axis=1)
    out["sg_w"] = jnp.transpose(cat.reshape(CHUNK, N_SUB, CHUNK), (1, 0, 2))
    out["sg_b"] = small[RG_SGB : RG_SGB + CHUNK].reshape(CHUNK, N_SUB, HEAD_DIM).sum(-1).T
    pool = small[RG_POOL : RG_POOL + D_G]
    out["pool_w"] = jnp.stack(
        [pool[g * HEAD_DIM : (g + 1) * HEAD_DIM, g * HEAD_DIM : (g + 1) * HEAD_DIM] for g in range(N_SUB)]
    )
    out["ln_g"] = small[RG_LN : RG_LN + 4].reshape(D_MODEL)
    out["ln_b"] = small[RG_LN + 4 : RG_LN + 8].reshape(D_MODEL)
    return out


def kernel(x, mem, w_in, conv_a_w, sg_ln_g, sg_ln_b, sg_w, sg_b, pool_w, pool_scale, cc_dw_w, cc_dw_b, cc_ln_g, cc_ln_b, cc_pw_w, w_kv, w_out, ln_g, ln_b, loss_target, m_w_in, m_conv_a_w, m_sg_ln_g, m_sg_ln_b, m_sg_w, m_sg_b, m_pool_w, m_pool_scale, m_cc_dw_w, m_cc_dw_b, m_cc_ln_g, m_cc_ln_b, m_cc_pw_w, m_w_kv, m_w_out, m_ln_g, m_ln_b, v_w_in, v_conv_a_w, v_sg_ln_g, v_sg_ln_b, v_sg_w, v_sg_b, v_pool_w, v_pool_scale, v_cc_dw_w, v_cc_dw_b, v_cc_ln_g, v_cc_ln_b, v_cc_pw_w, v_w_kv, v_w_out, v_ln_g, v_ln_b):
    given = dict(locals())
    weights = {k: given[k] for k in WEIGHT_NAMES}
    chip = 2 * lax.axis_index("x") + lax.axis_index("y")
    place = jnp.stack([lax.axis_index("c"), chip]).astype(jnp.int32)

    x0, mem0 = x[0], mem[0]

    taps = jnp.concatenate([conv_a_w, cc_dw_w], axis=1)
    taps = jnp.pad(taps, ((0, 0), (0, TAPS_ROWS - taps.shape[1]), (0, 0)))

    def own_blocks(l):
        return _place_own_block(
            place, [w_in, w_out, w_kv, cc_pw_w, taps], l, [BF16, BF16, BF16, BF16, F32], name=f"place_weights{l}"
        )

    def layer_operands(l, gathered):
        g_in, g_out, g_kv, g_pw, g_taps = gathered
        taps_full = jnp.transpose(g_taps, (1, 0, 2)).reshape(TAPS_ROWS, D_G)
        full = dict(
            w_in=g_in,
            w_out=g_out.reshape(D_MIX, D_MODEL),
            w_kv=g_kv.reshape(D_MODEL, 2 * D_G),
            cc_pw_w=g_pw.reshape(D_G, D_G),
            conv_a_w=taps_full[0:CONV_A],
            cc_dw_w=taps_full[CONV_A : CONV_A + CONV_D],
            **{k: weights[k][l] for k in WEIGHT_NAMES if k not in LARGE + ("conv_a_w", "cc_dw_w")},
        )
        return _prepare_layer(mem0, full, l)

    def layer_grads(l, x_in, bwd, small, exchange=None):
        dproj, _, dwout, dkst, dvst, dpw, _ = bwd
        (dwin,), carried = _input_weight_grad(x_in, dproj, name=f"w_in_grad{l}", exchange=exchange)
        grads = [
            dwin,
            dwout.reshape(N_CHIPS, D_MIX // N_CHIPS, D_MODEL),
            _kv_backward(mem0, dkst, dvst, name=f"kv_bwd{l}").reshape(N_CHIPS, D_MODEL // N_CHIPS, 2 * D_G),
            dpw.reshape(N_CHIPS, D_G // N_CHIPS, D_G),
            small.reshape(N_CHIPS, RG_ROWS // N_CHIPS, D_G),
        ]
        return grads, carried

    n_red = 5
    keep = [False, False, False, False, True]

    wire = [BF16, BF16, BF16, BF16, F32]

    def reduced_layer(joined):
        r_in, r_out, r_kv, r_pw, small_all = joined
        out = _unpack_small_grads(small_all.reshape(RG_ROWS, D_G), chip)
        out.update(w_in=r_in, w_out=r_out, w_kv=r_kv, cc_pw_w=r_pw)
        return out

    blocks0, blocks1 = own_blocks(0), own_blocks(1)
    p0 = layer_operands(0, _run_exchange(_gather_exchange(blocks0), name="gather_weights0"))
    fwd0, gathered1 = _forward(0, x0, p0, None, exchange=_gather_exchange(blocks1))
    p1 = layer_operands(1, gathered1)
    x1 = fwd0[3]
    fwd1, _ = _forward(1, x1, p1, loss_target[0])

    bwd1, _ = _backward(1, fwd1[3], fwd1, p1)
    small1 = bwd1[6].at[RG_LOSS, :].set(fwd1[5][0, 0])
    grads1, _ = layer_grads(1, x1, bwd1, small1)
    (dx1,), received1 = _input_grad(bwd1[0], bwd1[1], p1["win"], name="input_grad1", exchange=_swap_exchange(grads1))
    pairs1 = _add_sibling_half(place, grads1, received1, wire, name="rs_pair1")
    bwd0, parts1 = _backward(0, dx1, fwd0, p0, exchange=_scatter_exchange(pairs1[:n_red], pairs1[n_red:]))
    halves1 = _sum_chip_blocks(place, parts1, keep, name="rs_sum1")
    grads0, joined1 = layer_grads(0, x0, bwd0, bwd0[6], exchange=_join_exchange(halves1, keep))
    loss = joined1[4].reshape(RG_ROWS, D_G)[RG_LOSS, 0]
    received0 = _run_exchange(_swap_exchange(grads0), name="rs_swap0")
    pairs0 = _add_sibling_half(place, grads0, received0, wire, name="rs_pair0")
    (grad_x,), parts0 = _input_grad(
        bwd0[0], bwd0[1], p0["win"], name="input_grad0", exchange=_scatter_exchange(pairs0[:n_red], pairs0[n_red:])
    )
    halves0 = _sum_chip_blocks(place, parts0, keep, name="rs_sum0")
    reduced = [reduced_layer(_run_exchange(_join_exchange(halves0, keep), name="rs_join0")), reduced_layer(joined1)]

    grad, delta, new_m, new_v = {}, {}, {}, {}
    for k in LARGE:
        w3 = weights[k]
        grad[k], delta[k], new_m[k], new_v[k] = _adamw_large(
            w3, given["m_" + k], given["v_" + k], [reduced[l][k] for l in range(DEPTH)], name=f"adamw_{k}"
        )
    small_names = [k for k in WEIGHT_NAMES if k not in LARGE]
    for k in small_names:
        grad[k] = jnp.stack([reduced[l][k] for l in range(DEPTH)])
    d_s, m_s, v_s = _adamw_small(
        [weights[k] for k in small_names],
        [grad[k] for k in small_names],
        [given["m_" + k] for k in small_names],
        [given["v_" + k] for k in small_names],
        name="adamw_small",
    )
    for a, k in enumerate(small_names):
        delta[k], new_m[k], new_v[k] = d_s[a], m_s[a], v_s[a]

    return (
        loss,
        grad_x[None],
        *[grad[k] for k in WEIGHT_NAMES],
        *[delta[k] for k in WEIGHT_NAMES],
        *[new_m[k] for k in WEIGHT_NAMES],
        *[new_v[k] for k in WEIGHT_NAMES],
    )
```

```python
import functools
import math

import jax
import jax.numpy as jnp
from jax import lax
from jax.experimental import pallas as pl
from jax.experimental.pallas import tpu as pltpu

F32 = jnp.float32
BF16 = jnp.bfloat16

D_MODEL = 1024
DEPTH = 2
D_G = 256
D_MIX = 5 * D_G
D_IN = 9 * D_G + D_MIX
N_SUB = 4
HEAD_DIM = 64
CONV_A = 3
CONV_D = 31
CHUNK = 128
MEM_LEN = 256
N_CHIPS = 4
W_IN_SHARD = D_IN // N_CHIPS
LN_EPS = 1e-5
ALPHA = (2.0 * DEPTH) ** 0.25
ATT_SCALE = 1.0 / math.sqrt(HEAD_DIM)
GELU_C = math.sqrt(2.0 / math.pi)
GELU_A = 0.044715

ADAM_LR = 0.001
ADAM_B1 = 0.9
ADAM_B2 = 0.999
ADAM_EPS = 1e-08
ADAM_WD = 0.01
ADAM_STEP = 10

C_XA, C_BA, C_CA, C_U, C_V, C_XC, C_DA, C_DG, C_Q, C_GATE = (D_G * i for i in range(10))

HALO_A = 8
HALO_C = 16
HALO_D = 32

RW_VEC = 0
RW_CONVA = 16
RW_DW = 24
RW_SGB = 56
RW_ROWS = RW_SGB + CHUNK

RG_VEC = 0
RG_CONVA = 16
RG_DW = 24
RG_SGW = 56
RG_SGB = RG_SGW + 2 * CHUNK
RG_POOL = RG_SGB + CHUNK
RG_LN = RG_POOL + D_G
RG_LOSS = 8
RG_ROWS = 768

VMEM_LIMIT = 62 * 1024 * 1024

AUX_CVD = 0
AUX_PM = D_G
AUX_P = 2 * D_G
AUX_COLS = AUX_P + N_SUB * MEM_LEN
SEQ_TILE = 256
FWD_TILE = 512
MM_TILE = 1024
ADAM_TILE = 512

MESH = pl.DeviceIdType.MESH
ANY = pl.BlockSpec(memory_space=pl.ANY)
NT = (((1,), (1,)), ((), ()))
TN = (((0,), (0,)), ((), ()))


def _dot(a, b):
    return jnp.dot(a, b, preferred_element_type=F32)


def _dot_nt(a, b):
    return lax.dot_general(a, b, NT, preferred_element_type=F32)


def _dot_tn(a, b):
    return lax.dot_general(a, b, TN, preferred_element_type=F32)


def _full(shape):
    zeros = (0,) * len(shape)
    return pl.BlockSpec(shape, lambda *_: zeros)


class _Exchange:
    def __init__(self, operands, outputs, sem_counts, before, after):
        self.operands, self.outputs, self.sem_counts, self.before, self.after = operands, outputs, sem_counts, before, after

    def specs(self, first_input, first_output):
        aliases = {first_input + src: first_output + j for j, (_, src) in enumerate(self.outputs) if src is not None}
        return (
            [ANY] * len(self.operands),
            [ANY] * len(self.outputs),
            [sds for sds, _ in self.outputs],
            [pltpu.SemaphoreType.DMA((k,)) for k in self.sem_counts],
            aliases,
        )

    def split(self, ins, outs):
        refs = list(ins)
        for j, (_, src) in enumerate(self.outputs):
            if src is not None:
                refs[src] = outs[j]
        return refs


def _when(cond, fn):
    if isinstance(cond, bool):
        if cond:
            fn()
    else:
        pl.when(cond)(fn)


def _run_exchange(exchange, *, name):
    n_in, n_out = len(exchange.operands), len(exchange.outputs)
    in_specs, out_specs, out_shape, sems, aliases = exchange.specs(0, 0)

    def body(*refs):
        ins, outs, sem_refs = refs[:n_in], refs[n_in : n_in + n_out], refs[n_in + n_out :]
        refs = exchange.split(ins, outs)
        exchange.before(0, 1, refs, outs, sem_refs)
        exchange.after(0, 1, refs, outs, sem_refs)

    return pl.pallas_call(
        body, name=name, in_specs=in_specs, out_specs=out_specs, out_shape=out_shape, scratch_shapes=sems,
        input_output_aliases=aliases,
    )(*exchange.operands)


def _gridded_call(body, *, name, steps, in_specs, out_specs, out_shape, scratch_shapes, operands, exchange=None):
    params = pltpu.CompilerParams(dimension_semantics=("arbitrary",), vmem_limit_bytes=VMEM_LIMIT)
    if exchange is None:
        outs = pl.pallas_call(
            body, name=name, grid=(steps,), in_specs=in_specs, out_specs=out_specs, out_shape=out_shape,
            scratch_shapes=scratch_shapes, compiler_params=params,
        )(*operands)
        return list(outs), []
    n_in, n_out, n_scr = len(in_specs), len(out_specs), len(scratch_shapes)
    x_in, x_out = len(exchange.operands), len(exchange.outputs)
    ex_in_specs, ex_out_specs, ex_out_shape, ex_sems, aliases = exchange.specs(n_in, n_out)

    def full(*refs):
        own_in, refs = refs[:n_in], refs[n_in:]
        ex_in, refs = refs[:x_in], refs[x_in:]
        own_out, refs = refs[:n_out], refs[n_out:]
        ex_out, refs = refs[:x_out], refs[x_out:]
        own_scr, sem_refs = refs[:n_scr], refs[n_scr:]
        ex_refs = exchange.split(ex_in, ex_out)
        step = pl.program_id(0)
        exchange.before(step, steps, ex_refs, ex_out, sem_refs)
        body(*own_in, *own_out, *own_scr)
        exchange.after(step, steps, ex_refs, ex_out, sem_refs)

    outs = pl.pallas_call(
        full, name=name, grid=(steps,), in_specs=in_specs + ex_in_specs, out_specs=out_specs + ex_out_specs,
        out_shape=out_shape + ex_out_shape, scratch_shapes=scratch_shapes + ex_sems, input_output_aliases=aliases,
        compiler_params=params,
    )(*operands, *exchange.operands)
    return list(outs[:n_out]), list(outs[n_out:])


def _sigmoid(x):
    return 0.5 * jnp.tanh(0.5 * x) + 0.5


def _gelu(x):
    t = jnp.tanh(GELU_C * (x + GELU_A * x * x * x))
    return 0.5 * x * (1.0 + t), t


def _gelu_grad(x, t):
    return 0.5 * (1.0 + t) + 0.5 * x * (1.0 - t * t) * (GELU_C * (1.0 + 3.0 * GELU_A * x * x))


def _normalize(v):
    mu = jnp.mean(v, axis=-1, keepdims=True)
    d = v - mu
    var = jnp.mean(d * d, axis=-1, keepdims=True)
    rstd = lax.rsqrt(var + LN_EPS)
    return d * rstd, rstd


def _normalize_grad(dhat, hat, rstd):
    m1 = jnp.mean(dhat, axis=-1, keepdims=True)
    m2 = jnp.mean(dhat * hat, axis=-1, keepdims=True)
    return rstd * (dhat - m1 - hat * m2)


def _lane(width=D_G):
    return lax.broadcasted_iota(jnp.int32, (1, width), 1)


def _head_masks():
    head = _lane() // HEAD_DIM
    return [(head == h).astype(F32) for h in range(N_SUB)]


def _stack_heads(v, masks):
    return jnp.concatenate([v * m for m in masks], axis=0)


def _tril_mask_cat():
    t = lax.broadcasted_iota(jnp.int32, (CHUNK, N_SUB * CHUNK), 0)
    s = lax.broadcasted_iota(jnp.int32, (CHUNK, N_SUB * CHUNK), 1) % CHUNK
    return s <= t


def _triu_mask_cat():
    s = lax.broadcasted_iota(jnp.int32, (CHUNK, N_SUB * CHUNK), 0)
    t = lax.broadcasted_iota(jnp.int32, (CHUNK, N_SUB * CHUNK), 1) % CHUNK
    return t >= s


def _pool_select(a2, a4, a8, a16):
    lane = _lane()
    return jnp.where(lane < 64, a2, jnp.where(lane < 128, a4, jnp.where(lane < 192, a8, a16)))


def _pool_inv_count(row0, rows):
    t = row0 + lax.broadcasted_iota(jnp.int32, (HALO_C, D_G), 0)
    lane = lax.broadcasted_iota(jnp.int32, (HALO_C, D_G), 1)
    win = jnp.where(lane < 64, 2, jnp.where(lane < 128, 4, jnp.where(lane < 192, 8, 16)))
    head = 1.0 / jnp.minimum(t + 1, win).astype(F32)
    inv_win = jnp.broadcast_to(_pool_select(0.5, 0.25, 0.125, 0.0625), (rows - HALO_C, D_G))
    return jnp.concatenate([head, inv_win], axis=0)


def _trailing_window_sum(halo, cur):
    e = jnp.concatenate([halo, cur], axis=0)
    s2 = e + pltpu.roll(e, 1, 0)
    s4 = s2 + pltpu.roll(s2, 2, 0)
    s8 = s4 + pltpu.roll(s4, 4, 0)
    s16 = s8 + pltpu.roll(s8, 8, 0)
    return _pool_select(s2, s4, s8, s16)[HALO_C:]


def _leading_window_sum(cur, halo):
    e = jnp.concatenate([cur, halo], axis=0)
    n = e.shape[0]
    s2 = e + pltpu.roll(e, n - 1, 0)
    s4 = s2 + pltpu.roll(s2, n - 2, 0)
    s8 = s4 + pltpu.roll(s4, n - 4, 0)
    s16 = s8 + pltpu.roll(s8, n - 8, 0)
    return _pool_select(s2, s4, s8, s16)[: cur.shape[0]]


def _softmax_blocks(sc):
    out = []
    for h in range(N_SUB):
        s = sc[:, h * MEM_LEN : (h + 1) * MEM_LEN]
        e = jnp.exp(s - jnp.max(s, axis=-1, keepdims=True))
        out.append(e * (1.0 / jnp.sum(e, axis=-1, keepdims=True)))
    return jnp.concatenate(out, axis=-1)


STRIP = 32
SHIFTS = 8


def _fill_shifts(buf):
    n = buf.shape[1] - SHIFTS
    for r in range(1, SHIFTS):
        buf[r, 0:n, :] = buf[0, r : r + n, :]


def _shifted(buf, off, rows):
    r = off % SHIFTS
    return buf[r, off - r : off - r + rows, :]


def _sgu_mix(vn, wcat_b, sgb, masks):
    vbd = _stack_heads(vn, masks).astype(BF16)
    return _dot(wcat_b, vbd) + sgb, vbd


def _layer_forward(x, win, kst, vst, wout, sw, wcat, poolw, pww, ln, tgt, *, name, exchange=None):
    seq = x.shape[0]
    tile = min(FWD_TILE, seq)
    n_tiles = seq // tile
    last = tgt is not None

    def body(*refs):
        x_ref, win_ref, kst_ref, vst_ref, wout_ref, sw_ref, wcat_ref, pool_ref, pw_ref, ln_ref = refs[:10]
        refs = refs[10:]
        if last:
            tgt_ref, refs = refs[0], refs[1:]
        proj_ref, y_ref, z_ref, out_ref, aux_ref = refs[:5]
        refs = refs[5:]
        if last:
            loss_ref, refs = refs[0], refs[1:]
        pbuf, xchalo, gbuf = refs
        i = pl.program_id(0)

        @pl.when(i == 0)
        def _():
            pbuf[0:HALO_A, :] = jnp.zeros((HALO_A, D_G), F32)
            xchalo[...] = jnp.zeros((HALO_C, D_G), F32)
            gbuf[0, 0:HALO_D, :] = jnp.zeros((HALO_D, D_G), F32)
            if last:
                loss_ref[...] = jnp.zeros((8, 128), F32)

        xt = x_ref[...]
        xb = xt.astype(BF16)

        blocks = {}

        def project(k):
            blocks[k] = _dot(xb, win_ref[k])
            proj_ref[:, k * W_IN_SHARD : (k + 1) * W_IN_SHARD] = blocks[k]

        def cols(start, width=D_G):
            parts, c = [], start
            while c < start + width:
                k, lo = divmod(c, W_IN_SHARD)
                hi = min(W_IN_SHARD, lo + start + width - c)
                parts.append(blocks[k][:, lo:hi])
                c += hi - lo
            return parts[0] if len(parts) == 1 else jnp.concatenate(parts, axis=1)

        project(0)
        project(1)
        masks = _head_masks()

        pbuf[HALO_A : HALO_A + tile, :] = cols(C_CA) * cols(C_XA)
        cv = jnp.zeros((tile, D_G), F32)
        for k in range(CONV_A):
            off = HALO_A - (CONV_A - 1) + k
            cv = cv + sw_ref[RW_CONVA + k : RW_CONVA + k + 1, :] * pbuf[off : off + tile, :]
        y_ref[:, 0:D_G] = cols(C_BA) * cv
        pbuf[0:HALO_A, :] = pbuf[tile : tile + HALO_A, :]

        project(2)

        ua, _ = _gelu(cols(C_U))
        vg, _ = _gelu(cols(C_V))
        vhat, _ = _normalize(vg)
        vn = vhat * sw_ref[RW_VEC : RW_VEC + 1, :] + sw_ref[RW_VEC + 1 : RW_VEC + 2, :]
        wcat_b = jnp.where(_tril_mask_cat(), wcat_ref[...], 0.0).astype(BF16)
        sgb = sw_ref[RW_SGB : RW_SGB + CHUNK, :]
        for j in range(tile // CHUNK):
            rows = slice(j * CHUNK, (j + 1) * CHUNK)
            mixed, _ = _sgu_mix(vn[rows], wcat_b, sgb, masks)
            y_ref[rows, D_G : 2 * D_G] = ua[rows] * mixed

        xc = cols(C_XC)
        wsum = _trailing_window_sum(xchalo[...], xc)
        pm = wsum * _pool_inv_count(i * tile, tile) - xc
        aux_ref[:, AUX_PM : AUX_PM + D_G] = pm
        y_ref[:, 2 * D_G : 3 * D_G] = _dot(pm.astype(BF16), pool_ref[...]) * sw_ref[RW_VEC + 2 : RW_VEC + 3, :]
        xchalo[...] = xc[tile - HALO_C :, :]

        project(3)

        gbuf[0, HALO_D : HALO_D + tile, :] = cols(C_DA) * _sigmoid(cols(C_DG))
        _fill_shifts(gbuf)
        for r0 in range(0, tile, STRIP):
            acc = jnp.zeros((STRIP, D_G), F32) + sw_ref[RW_VEC + 3 : RW_VEC + 4, :]
            for k in range(CONV_D):
                off = HALO_D - (CONV_D - 1) + k
                acc = acc + sw_ref[RW_DW + k : RW_DW + k + 1, :] * _shifted(gbuf, off + r0, STRIP)
            aux_ref[r0 : r0 + STRIP, AUX_CVD : AUX_CVD + D_G] = acc
        nhat, _ = _normalize(aux_ref[:, AUX_CVD : AUX_CVD + D_G])
        nrm = nhat * sw_ref[RW_VEC + 4 : RW_VEC + 5, :] + sw_ref[RW_VEC + 5 : RW_VEC + 6, :]
        y_ref[:, 3 * D_G : 4 * D_G] = _dot((nrm * _sigmoid(nrm)).astype(BF16), pw_ref[...])
        gbuf[0, 0:HALO_D, :] = gbuf[0, tile : tile + HALO_D, :]

        qb = cols(C_Q).astype(BF16)
        p_all = _softmax_blocks(_dot_nt(qb, kst_ref[...]) * ATT_SCALE)
        aux_ref[:, AUX_P:] = p_all
        y_ref[:, 4 * D_G : 5 * D_G] = _dot(p_all.astype(BF16), vst_ref[...])

        gate = cols(C_GATE, D_MIX)
        hid = y_ref[...] * (gate * _sigmoid(gate))
        z = ALPHA * xt + _dot(hid.astype(BF16), wout_ref[...])
        z_ref[...] = z
        zhat, _ = _normalize(z)
        xn = zhat * ln_ref[0:1, :] + ln_ref[1:2, :]
        if last:
            err = xn - tgt_ref[...]
            out_ref[...] = err * (1.0 / D_MODEL)
            loss_ref[...] += jnp.sum(err * err) * (0.5 / D_MODEL)
        else:
            out_ref[...] = xn

    def rows(width):
        return pl.BlockSpec((tile, width), lambda i: (i, 0))

    operands = [x, win, kst, vst, wout, sw, wcat, poolw, pww, ln]
    in_specs = [rows(D_MODEL)] + [_full(a.shape) for a in operands[1:]]
    out_shape = [
        jax.ShapeDtypeStruct((seq, D_IN), F32),
        jax.ShapeDtypeStruct((seq, D_MIX), F32),
        jax.ShapeDtypeStruct((seq, D_MODEL), F32),
        jax.ShapeDtypeStruct((seq, D_MODEL), F32),
        jax.ShapeDtypeStruct((seq, AUX_COLS), F32),
    ]
    out_specs = [rows(D_IN), rows(D_MIX), rows(D_MODEL), rows(D_MODEL), rows(AUX_COLS)]
    if last:
        operands.append(tgt)
        in_specs.append(rows(D_MODEL))
        out_shape.append(jax.ShapeDtypeStruct((8, 128), F32))
        out_specs.append(_full((8, 128)))
    return _gridded_call(
        body,
        name=name,
        steps=n_tiles,
        in_specs=in_specs,
        out_specs=out_specs,
        out_shape=out_shape,
        scratch_shapes=[
            pltpu.VMEM((HALO_A + tile, D_G), F32),
            pltpu.VMEM((HALO_C, D_G), F32),
            pltpu.VMEM((SHIFTS, HALO_D + tile, D_G), F32),
        ],
        operands=operands,
        exchange=exchange,
    )


def _layer_backward(dxn, z, proj, y, cvd, kst, vst, wout, sw, wcat, wcat_t, poolw, pww, ln, *, name, exchange=None):
    seq = dxn.shape[0]
    tile = min(SEQ_TILE, seq)
    n_tiles = seq // tile
    halo_blocks = tile // HALO_D

    def body(
        dxn_ref, z_ref, proj_ref, halo_ref, y_ref, aux_ref, kst_ref, vst_ref, wout_ref, sw_ref, wcat_ref, wcat_t_ref,
        pool_ref, pw_ref, ln_ref, dproj_ref, dz_ref, dwout_ref, dkst_ref, dvst_ref, dpw_ref, sg_ref,
        pbuf, dcvbuf, rhalo, gbuf, dgbuf, dwacc,
    ):
        i = pl.program_id(0)
        ti = n_tiles - 1 - i

        @pl.when(i == 0)
        def _():
            dwout_ref[...] = jnp.zeros(dwout_ref.shape, F32)
            dkst_ref[...] = jnp.zeros(dkst_ref.shape, F32)
            dvst_ref[...] = jnp.zeros(dvst_ref.shape, F32)
            dpw_ref[...] = jnp.zeros(dpw_ref.shape, F32)
            sg_ref[...] = jnp.zeros(sg_ref.shape, F32)
            dwacc[...] = jnp.zeros(dwacc.shape, F32)
            dcvbuf[tile : tile + HALO_A, :] = jnp.zeros((HALO_A, D_G), F32)
            rhalo[...] = jnp.zeros((HALO_C, D_G), F32)
            dgbuf[0, tile : tile + HALO_D, :] = jnp.zeros((HALO_D, D_G), F32)

        def acc_row(row, val):
            sg_ref[row : row + 1, :] += jnp.sum(val, axis=0, keepdims=True)

        masks = _head_masks()
        has_past = (ti > 0).astype(F32)

        zhat, zrstd = _normalize(z_ref[...])
        dxn_t = dxn_ref[...]
        dlg = jnp.sum(dxn_t * zhat, axis=0, keepdims=True)
        dlb = jnp.sum(dxn_t, axis=0, keepdims=True)
        for j in range(D_MODEL // D_G):
            sg_ref[RG_LN + j : RG_LN + j + 1, :] += dlg[:, j * D_G : (j + 1) * D_G]
            sg_ref[RG_LN + 4 + j : RG_LN + 5 + j, :] += dlb[:, j * D_G : (j + 1) * D_G]
        dz = _normalize_grad(dxn_t * ln_ref[0:1, :], zhat, zrstd)
        dz_ref[...] = dz
        dzb = dz.astype(BF16)

        gate = proj_ref[:, C_GATE:]
        sgm = _sigmoid(gate)
        silu = gate * sgm
        yc = y_ref[...]
        dwout_ref[...] += _dot_tn((yc * silu).astype(BF16), dzb)
        dh = _dot_nt(dzb, wout_ref[...])
        dproj_ref[:, C_GATE:] = (dh * yc * (sgm * (1.0 + gate * (1.0 - sgm)))).astype(BF16)
        dy = dh * silu

        dya = dy[:, 0:D_G]
        xa = proj_ref[:, C_XA : C_XA + D_G]
        ba = proj_ref[:, C_BA : C_BA + D_G]
        ca = proj_ref[:, C_CA : C_CA + D_G]
        past = slice(HALO_D - HALO_A, HALO_D)
        pbuf[0:HALO_A, :] = halo_ref[past, C_CA : C_CA + D_G] * halo_ref[past, C_XA : C_XA + D_G] * has_past
        pbuf[HALO_A : HALO_A + tile, :] = ca * xa
        cv = jnp.zeros((tile, D_G), F32)
        for k in range(CONV_A):
            off = HALO_A - (CONV_A - 1) + k
            cv = cv + sw_ref[RW_CONVA + k : RW_CONVA + k + 1, :] * pbuf[off : off + tile, :]
        dproj_ref[:, C_BA : C_BA + D_G] = (dya * cv).astype(BF16)
        dcv = dya * ba
        dcvbuf[0:tile, :] = dcv
        dp = jnp.zeros((tile, D_G), F32)
        for k in range(CONV_A):
            off = HALO_A - (CONV_A - 1) + k
            acc_row(RG_CONVA + k, dcv * pbuf[off : off + tile, :])
            back = CONV_A - 1 - k
            dp = dp + sw_ref[RW_CONVA + k : RW_CONVA + k + 1, :] * dcvbuf[back : back + tile, :]
        dproj_ref[:, C_CA : C_CA + D_G] = (dp * xa).astype(BF16)
        dproj_ref[:, C_XA : C_XA + D_G] = (dp * ca).astype(BF16)
        dcvbuf[tile : tile + HALO_A, :] = dcvbuf[0:HALO_A, :]

        dyb = dy[:, D_G : 2 * D_G]
        u = proj_ref[:, C_U : C_U + D_G]
        v = proj_ref[:, C_V : C_V + D_G]
        ua, ut = _gelu(u)
        vg, vt = _gelu(v)
        vhat, vrstd = _normalize(vg)
        sg_g = sw_ref[RW_VEC : RW_VEC + 1, :]
        vn = vhat * sg_g + sw_ref[RW_VEC + 1 : RW_VEC + 2, :]
        tril = _tril_mask_cat()
        wcat_b = jnp.where(tril, wcat_ref[...], 0.0).astype(BF16)
        wcat_tb = jnp.where(_triu_mask_cat(), wcat_t_ref[...], 0.0).astype(BF16)
        sgb = sw_ref[RW_SGB : RW_SGB + CHUNK, :]
        dmixed = dyb * ua
        dvn_parts = []
        du_parts = []
        dwcat = jnp.zeros((CHUNK, N_SUB * CHUNK), F32)
        dsgb = jnp.zeros((CHUNK, D_G), F32)
        for j in range(tile // CHUNK):
            rows = slice(j * CHUNK, (j + 1) * CHUNK)
            mixed, vbd = _sgu_mix(vn[rows], wcat_b, sgb, masks)
            du_parts.append(dyb[rows] * mixed)
            dmx = dmixed[rows]
            dsgb = dsgb + dmx
            dwcat = dwcat + _dot_nt(dmx.astype(BF16), vbd)
            dvn_parts.append(_dot(wcat_tb, _stack_heads(dmx, masks).astype(BF16)))
        dwcat = jnp.where(tril, dwcat, 0.0)
        sg_ref[RG_SGW : RG_SGW + CHUNK, :] += dwcat[:, 0:D_G]
        sg_ref[RG_SGW + CHUNK : RG_SGW + 2 * CHUNK, :] += dwcat[:, D_G:]
        sg_ref[RG_SGB : RG_SGB + CHUNK, :] += dsgb
        dvn = jnp.concatenate(dvn_parts, axis=0)
        du_act = jnp.concatenate(du_parts, axis=0)
        acc_row(RG_VEC, dvn * vhat)
        acc_row(RG_VEC + 1, dvn)
        dvg = _normalize_grad(dvn * sg_g, vhat, vrstd)
        dproj_ref[:, C_U : C_U + D_G] = (du_act * _gelu_grad(u, ut)).astype(BF16)
        dproj_ref[:, C_V : C_V + D_G] = (dvg * _gelu_grad(v, vt)).astype(BF16)

        dyc = dy[:, 2 * D_G : 3 * D_G]
        inv_cnt = _pool_inv_count(ti * tile, tile)
        pmb = aux_ref[:, AUX_PM : AUX_PM + D_G].astype(BF16)
        pool_b = pool_ref[...]
        scale = sw_ref[RW_VEC + 2 : RW_VEC + 3, :]
        acc_row(RG_VEC + 2, dyc * _dot(pmb, pool_b))
        dpre = (dyc * scale).astype(BF16)
        sg_ref[RG_POOL : RG_POOL + D_G, :] += _dot_tn(pmb, dpre)
        dpm = _dot_nt(dpre, pool_b)
        r = dpm * inv_cnt
        dproj_ref[:, C_XC : C_XC + D_G] = (_leading_window_sum(r, rhalo[...]) - dpm).astype(BF16)
        rhalo[...] = r[0:HALO_C, :]

        dyd = dy[:, 3 * D_G : 4 * D_G]
        da = proj_ref[:, C_DA : C_DA + D_G]
        sgd = _sigmoid(proj_ref[:, C_DG : C_DG + D_G])
        gbuf[0, 0:HALO_D, :] = halo_ref[:, C_DA : C_DA + D_G] * _sigmoid(halo_ref[:, C_DG : C_DG + D_G]) * has_past
        gbuf[0, HALO_D : HALO_D + tile, :] = da * sgd
        _fill_shifts(gbuf)
        nhat, nrstd = _normalize(aux_ref[:, AUX_CVD : AUX_CVD + D_G])
        cc_g = sw_ref[RW_VEC + 4 : RW_VEC + 5, :]
        nrm = nhat * cc_g + sw_ref[RW_VEC + 5 : RW_VEC + 6, :]
        sgn = _sigmoid(nrm)
        dydb = dyd.astype(BF16)
        dpw_ref[...] += _dot_tn((nrm * sgn).astype(BF16), dydb)
        dn = _dot_nt(dydb, pw_ref[...]) * (sgn * (1.0 + nrm * (1.0 - sgn)))
        acc_row(RG_VEC + 4, dn * nhat)
        acc_row(RG_VEC + 5, dn)
        dcvd = _normalize_grad(dn * cc_g, nhat, nrstd)
        acc_row(RG_VEC + 3, dcvd)
        dgbuf[0, 0:tile, :] = dcvd
        _fill_shifts(dgbuf)
        for r0 in range(0, tile, STRIP):
            d_s = dgbuf[0, r0 : r0 + STRIP, :]
            dg = jnp.zeros((STRIP, D_G), F32)
            for k in range(CONV_D):
                off = HALO_D - (CONV_D - 1) + k
                prod = d_s * _shifted(gbuf, off + r0, STRIP)
                part = prod[0:8]
                for q in range(8, STRIP, 8):
                    part = part + prod[q : q + 8]
                dwacc[8 * k : 8 * k + 8, :] += part
                back = CONV_D - 1 - k
                dg = dg + sw_ref[RW_DW + k : RW_DW + k + 1, :] * _shifted(dgbuf, back + r0, STRIP)
            da_s = proj_ref[r0 : r0 + STRIP, C_DA : C_DA + D_G]
            sgd_s = _sigmoid(proj_ref[r0 : r0 + STRIP, C_DG : C_DG + D_G])
            dproj_ref[r0 : r0 + STRIP, C_DA : C_DA + D_G] = (dg * sgd_s).astype(BF16)
            dproj_ref[r0 : r0 + STRIP, C_DG : C_DG + D_G] = (dg * da_s * sgd_s * (1.0 - sgd_s)).astype(BF16)
        dgbuf[0, tile : tile + HALO_D, :] = dgbuf[0, 0:HALO_D, :]

        @pl.when(i == n_tiles - 1)
        def _():
            for k in range(CONV_D):
                sg_ref[RG_DW + k : RG_DW + k + 1, :] = jnp.sum(dwacc[8 * k : 8 * k + 8, :], axis=0, keepdims=True)

        dyeb = dy[:, 4 * D_G : 5 * D_G].astype(BF16)
        qb = proj_ref[:, C_Q : C_Q + D_G].astype(BF16)
        kst_b = kst_ref[...]
        p_all = aux_ref[:, AUX_P:]
        dvst_ref[...] += _dot_tn(p_all.astype(BF16), dyeb)
        dp_all = _dot_nt(dyeb, vst_ref[...])
        ds = []
        for h in range(N_SUB):
            blk = slice(h * MEM_LEN, (h + 1) * MEM_LEN)
            p, dpb = p_all[:, blk], dp_all[:, blk]
            ds.append(p * (dpb - jnp.sum(dpb * p, axis=-1, keepdims=True)))
        dsb = (jnp.concatenate(ds, axis=-1) * ATT_SCALE).astype(BF16)
        dproj_ref[:, C_Q : C_Q + D_G] = _dot(dsb, kst_b).astype(BF16)
        dkst_ref[...] += _dot_tn(dsb, qb)

    def rows(width):
        return pl.BlockSpec((tile, width), lambda i: (n_tiles - 1 - i, 0))

    halo_spec = pl.BlockSpec((HALO_D, D_IN), lambda i: (jnp.maximum((n_tiles - 1 - i) * halo_blocks - 1, 0), 0))
    weights = [kst, vst, wout, sw, wcat, wcat_t, poolw, pww, ln]
    acc_shapes = [(D_MIX, D_MODEL), (N_SUB * MEM_LEN, D_G), (N_SUB * MEM_LEN, D_G), (D_G, D_G), (RG_ROWS, D_G)]
    return _gridded_call(
        body,
        name=name,
        steps=n_tiles,
        in_specs=[rows(D_MODEL), rows(D_MODEL), rows(D_IN), halo_spec, rows(D_MIX), rows(AUX_COLS)]
        + [_full(a.shape) for a in weights],
        out_specs=[rows(D_IN), rows(D_MODEL)] + [_full(s) for s in acc_shapes],
        out_shape=[jax.ShapeDtypeStruct((seq, D_IN), BF16), jax.ShapeDtypeStruct((seq, D_MODEL), F32)]
        + [jax.ShapeDtypeStruct(s, F32) for s in acc_shapes],
        scratch_shapes=[
            pltpu.VMEM((HALO_A + tile, D_G), F32),
            pltpu.VMEM((tile + HALO_A, D_G), F32),
            pltpu.VMEM((HALO_C, D_G), F32),
            pltpu.VMEM((SHIFTS, HALO_D + tile, D_G), F32),
            pltpu.VMEM((SHIFTS, tile + HALO_D, D_G), F32),
            pltpu.VMEM((8 * CONV_D, D_G), F32),
        ],
        operands=[dxn, z, proj, proj, y, cvd, *weights],
        exchange=exchange,
    )


def _kv_forward(mem, wkv, *, name):
    def body(mem_ref, wkv_ref, kst_ref, vst_ref):
        kv = _dot(mem_ref[...].astype(BF16), wkv_ref[...])
        masks = _head_masks()
        kst_ref[...] = _stack_heads(kv[:, 0:D_G], masks).astype(BF16)
        vst_ref[...] = _stack_heads(kv[:, D_G:], masks).astype(BF16)

    shape = jax.ShapeDtypeStruct((N_SUB * MEM_LEN, D_G), BF16)
    return pl.pallas_call(body, name=name, out_shape=[shape, shape])(mem, wkv)


def _kv_backward(mem, dkst, dvst, *, name):
    def body(mem_ref, dkst_ref, dvst_ref, dwkv_ref):
        masks = _head_masks()
        memb = mem_ref[...].astype(BF16)
        for col, ref in ((0, dkst_ref), (D_G, dvst_ref)):
            d = jnp.zeros((MEM_LEN, D_G), F32)
            for h in range(N_SUB):
                d = d + ref[h * MEM_LEN : (h + 1) * MEM_LEN, :] * masks[h]
            dwkv_ref[:, col : col + D_G] = _dot_tn(memb, d.astype(BF16))

    return pl.pallas_call(body, name=name, out_shape=jax.ShapeDtypeStruct((D_MODEL, 2 * D_G), F32))(mem, dkst, dvst)


def _input_grad(dproj, dz, win, *, name, exchange=None):
    seq = dproj.shape[0]
    tile = min(MM_TILE // 2, seq)

    def body(dproj_ref, dz_ref, win_ref, dx_ref):
        acc = ALPHA * dz_ref[...]
        for k in range(N_CHIPS):
            acc = acc + _dot_nt(dproj_ref[:, k * W_IN_SHARD : (k + 1) * W_IN_SHARD], win_ref[k])
        dx_ref[...] = acc

    return _gridded_call(
        body,
        name=name,
        steps=seq // tile,
        in_specs=[
            pl.BlockSpec((tile, D_IN), lambda i: (i, 0)),
            pl.BlockSpec((tile, D_MODEL), lambda i: (i, 0)),
            _full(win.shape),
        ],
        out_specs=[pl.BlockSpec((tile, D_MODEL), lambda i: (i, 0))],
        out_shape=[jax.ShapeDtypeStruct((seq, D_MODEL), F32)],
        scratch_shapes=[],
        operands=[dproj, dz, win],
        exchange=exchange,
    )


def _input_weight_grad(x, dproj, *, name, exchange=None):
    seq = x.shape[0]
    tile = min(MM_TILE, seq)
    n_rows = seq // tile

    def body(x_ref, dproj_ref, dwin_ref):
        @pl.when(pl.program_id(0) % n_rows == 0)
        def _():
            dwin_ref[...] = jnp.zeros(dwin_ref.shape, F32)

        dwin_ref[0] += _dot_tn(x_ref[...].astype(BF16), dproj_ref[...])

    return _gridded_call(
        body,
        name=name,
        steps=N_CHIPS * n_rows,
        in_specs=[
            pl.BlockSpec((tile, D_MODEL), lambda s: (s % n_rows, 0)),
            pl.BlockSpec((tile, W_IN_SHARD), lambda s: (s % n_rows, s // n_rows)),
        ],
        out_specs=[pl.BlockSpec((1, D_MODEL, W_IN_SHARD), lambda s: (s // n_rows, 0, 0))],
        out_shape=[jax.ShapeDtypeStruct((N_CHIPS, D_MODEL, W_IN_SHARD), F32)],
        scratch_shapes=[],
        operands=[x, dproj],
        exchange=exchange,
    )


def _expand_sgb(sg_b):
    return jnp.repeat(sg_b.T, HEAD_DIM, axis=1)


def _pack_small_weights(sg_ln_g, sg_ln_b, pool_scale, cc_dw_b, cc_ln_g, cc_ln_b, conv_a_w, cc_dw_w, sg_b):
    vec = jnp.stack([sg_ln_g, sg_ln_b, pool_scale, cc_dw_b, cc_ln_g, cc_ln_b])
    return jnp.concatenate(
        [
            jnp.pad(vec, ((0, RW_CONVA - RW_VEC - 6), (0, 0))),
            jnp.pad(conv_a_w, ((0, RW_DW - RW_CONVA - CONV_A), (0, 0))),
            jnp.pad(cc_dw_w, ((0, RW_SGB - RW_DW - CONV_D), (0, 0))),
            _expand_sgb(sg_b),
        ]
    )


def _sg_w_cat(sg_w):
    cat = jnp.transpose(sg_w, (1, 0, 2)).reshape(CHUNK, N_SUB * CHUNK)
    cat_t = jnp.transpose(sg_w, (2, 0, 1)).reshape(CHUNK, N_SUB * CHUNK)
    return cat, cat_t


def _pool_block_diag(pool_w):
    out = jnp.zeros((D_G, D_G), pool_w.dtype)
    for g in range(N_SUB):
        out = out.at[g * HEAD_DIM : (g + 1) * HEAD_DIM, g * HEAD_DIM : (g + 1) * HEAD_DIM].set(pool_w[g])
    return out


def _prepare_layer(mem, w, l):
    cat, cat_t = _sg_w_cat(w["sg_w"])
    kst, vst = _kv_forward(mem, w["w_kv"], name=f"kv_fwd{l}")
    return dict(
        win=w["w_in"],
        wout=w["w_out"],
        pww=w["cc_pw_w"],
        sw=_pack_small_weights(
            w["sg_ln_g"], w["sg_ln_b"], w["pool_scale"], w["cc_dw_b"], w["cc_ln_g"], w["cc_ln_b"],
            w["conv_a_w"], w["cc_dw_w"], w["sg_b"],
        ),
        wcat=cat,
        wcat_t=cat_t,
        poolw=_pool_block_diag(w["pool_w"]).astype(BF16),
        ln=jnp.stack([w["ln_g"], w["ln_b"]]),
        kst=kst,
        vst=vst,
    )


def _forward(l, h, p, tgt, exchange=None):
    return _layer_forward(
        h, p["win"], p["kst"], p["vst"], p["wout"], p["sw"], p["wcat"], p["poolw"], p["pww"], p["ln"], tgt,
        name=f"layer_fwd{l}", exchange=exchange,
    )


def _backward(l, dxn, s, p, exchange=None):
    return _layer_backward(
        dxn, s[2], s[0], s[1], s[4], p["kst"], p["vst"], p["wout"], p["sw"], p["wcat"], p["wcat_t"], p["poolw"],
        p["pww"], p["ln"], name=f"layer_bwd{l}", exchange=exchange,
    )


def _place():
    x, y, c = lax.axis_index("x"), lax.axis_index("y"), lax.axis_index("c")
    others = [(1 - x, y), (x, 1 - y), (1 - x, 1 - y)]
    return x, y, c, others


def _half(ref, c, axis):
    n = ref.shape[axis] // 2
    if axis == 0:
        return ref.at[pl.ds(c * n, n)]
    return ref.at[:, pl.ds(c * n, n)]


def _place_own_block(place, stacked, layer, dtypes, *, name):
    n = len(stacked)

    def body(place_ref, *refs):
        for a in range(n):
            refs[n + a][...] = refs[a][...].astype(dtypes[a])

    def block(s):
        return (1,) + s.shape[1:]

    return pl.pallas_call(
        body,
        name=name,
        grid_spec=pltpu.PrefetchScalarGridSpec(
            num_scalar_prefetch=1,
            grid=(1,),
            in_specs=[pl.BlockSpec(block(s), lambda i, place_ref: (layer, 0, 0)) for s in stacked],
            out_specs=[pl.BlockSpec(block(s), lambda i, place_ref: (place_ref[1], 0, 0)) for s in stacked],
        ),
        out_shape=[jax.ShapeDtypeStruct((N_CHIPS,) + s.shape[1:], dt) for s, dt in zip(stacked, dtypes)],
        compiler_params=pltpu.CompilerParams(dimension_semantics=("arbitrary",), vmem_limit_bytes=VMEM_LIMIT),
    )(place, *stacked)


def _sds(a):
    return jax.ShapeDtypeStruct(a.shape, a.dtype)


def _gather_exchange(bufs):
    n = len(bufs)

    def remote(sems, block, k, to):
        return pltpu.make_async_remote_copy(
            src_ref=block, dst_ref=block, send_sem=sems[0].at[k], recv_sem=sems[1].at[k], device_id=to, device_id_type=MESH
        )

    def before(step, steps, refs, outs, sems):
        def send():
            x, y, c, others = _place()
            for j, (px, py) in enumerate(others):
                for a in range(n):
                    remote(sems, _half(refs[a].at[2 * x + y], c, 0), 3 * a + j, (px, py, c)).start()

        _when(step == 0, send)

    def after(step, steps, refs, outs, sems):
        def pass_on():
            x, y, c, others = _place()
            for j, (px, py) in enumerate(others):
                for a in range(n):
                    landed = _half(refs[a].at[2 * px + py], c, 0)
                    remote(sems, landed, 3 * a + j, (px, py, c)).wait_recv()
                    remote(sems, landed, 3 * n + 3 * a + j, (x, y, 1 - c)).start()

        def finish():
            x, y, c, others = _place()
            for j, (px, py) in enumerate(others):
                for a in range(n):
                    remote(sems, _half(refs[a].at[2 * px + py], 1 - c, 0), 3 * n + 3 * a + j, (x, y, 1 - c)).wait_recv()
            for a in range(n):
                mine = _half(refs[a].at[2 * x + y], c, 0)
                for k in range(3 * a, 3 * a + 3):
                    remote(sems, mine, k, (x, y, 1 - c)).wait_send()
                    remote(sems, mine, 3 * n + k, (x, y, 1 - c)).wait_send()

        _when(step == (3 * steps) // 4, pass_on)
        _when(step == steps - 1, finish)

    return _Exchange(bufs, [(_sds(b), a) for a, b in enumerate(bufs)], [6 * n, 6 * n], before, after)


def _swap_exchange(grads):
    n = len(grads)

    def copy(refs, outs, sems, a):
        x, y, c, _ = _place()
        return pltpu.make_async_remote_copy(
            src_ref=_half(refs[a], 1 - c, 1), dst_ref=outs[a], send_sem=sems[0].at[a], recv_sem=sems[1].at[a],
            device_id=(x, y, 1 - c), device_id_type=MESH,
        )

    def before(step, steps, refs, outs, sems):
        _when(step == 0, lambda: [copy(refs, outs, sems, a).start() for a in range(n)] and None)

    def after(step, steps, refs, outs, sems):
        _when(step == steps - 1, lambda: [copy(refs, outs, sems, a).wait() for a in range(n)] and None)

    outputs = [(jax.ShapeDtypeStruct((N_CHIPS, g.shape[1] // 2, g.shape[2]), g.dtype), None) for g in grads]
    return _Exchange(grads, outputs, [n, n], before, after)


def _add_sibling_half(place, grads, received, wire, *, name):
    n = len(grads)

    def body(place_ref, *refs):
        k = pl.program_id(0)
        for a in range(n):
            pair = (refs[a][...] + refs[n + a][...]).astype(wire[a])
            refs[2 * n + a][...] = pair

            @pl.when(k == place_ref[1])
            def _(a=a, pair=pair):
                refs[3 * n + a][...] = pair

    def block(g):
        return (1, g.shape[1] // 2, g.shape[2])

    return pl.pallas_call(
        body,
        name=name,
        grid_spec=pltpu.PrefetchScalarGridSpec(
            num_scalar_prefetch=1,
            grid=(N_CHIPS,),
            in_specs=[pl.BlockSpec(block(g), lambda k, place_ref: (k, place_ref[0], 0)) for g in grads]
            + [pl.BlockSpec(block(g), lambda k, place_ref: (k, 0, 0)) for g in grads],
            out_specs=[pl.BlockSpec(block(g), lambda k, place_ref: (k, 0, 0)) for g in grads]
            + [pl.BlockSpec(block(g), lambda k, place_ref: (place_ref[1], 0, 0)) for g in grads],
        ),
        out_shape=[jax.ShapeDtypeStruct(r.shape, dt) for r, dt in zip(received, wire)] * 2,
        compiler_params=pltpu.CompilerParams(dimension_semantics=("arbitrary",), vmem_limit_bytes=VMEM_LIMIT),
    )(place, *grads, *received)


def _scatter_exchange(pairs, landing):
    n = len(pairs)

    def copy(refs, sems, a, j, px, py):
        x, y, c, _ = _place()
        return pltpu.make_async_remote_copy(
            src_ref=refs[a].at[2 * px + py], dst_ref=refs[n + a].at[2 * x + y], send_sem=sems[0].at[3 * a + j],
            recv_sem=sems[1].at[3 * a + j], device_id=(px, py, c), device_id_type=MESH,
        )

    def before(step, steps, refs, outs, sems):
        def send():
            for j, (px, py) in enumerate(_place()[3]):
                for a in range(n):
                    copy(refs, sems, a, j, px, py).start()

        _when(step == 0, send)

    def after(step, steps, refs, outs, sems):
        def finish():
            x, y, c, others = _place()
            for j, (px, py) in enumerate(others):
                for a in range(n):
                    landed = refs[n + a].at[2 * px + py]
                    pltpu.make_async_remote_copy(
                        src_ref=landed, dst_ref=landed, send_sem=sems[0].at[3 * a + j], recv_sem=sems[1].at[3 * a + j],
                        device_id=(px, py, c), device_id_type=MESH,
                    ).wait_recv()
            for j, (px, py) in enumerate(others):
                for a in range(n):
                    copy(refs, sems, a, j, px, py).wait_send()

        _when(step == steps - 1, finish)

    return _Exchange(pairs + landing, [(_sds(b), n + a) for a, b in enumerate(landing)], [3 * n, 3 * n], before, after)


SUM_STEPS = 2


def _sum_chip_blocks(place, parts, keep_chip_axis, *, name):
    n = len(parts)

    def body(place_ref, *refs):
        for a in range(n):
            p = refs[a]
            total = (p[0].astype(F32) + p[1].astype(F32)) + (p[2].astype(F32) + p[3].astype(F32))
            if keep_chip_axis[a]:
                refs[n + a][0] = total
            else:
                refs[n + a][...] = total

    def in_spec(p):
        return pl.BlockSpec((N_CHIPS, p.shape[1] // SUM_STEPS, p.shape[2]), lambda i, place_ref: (0, i, 0))

    def out_spec(p, keep):
        rows = p.shape[1] // SUM_STEPS
        if keep:
            return pl.BlockSpec((1, rows, p.shape[2]), lambda i, place_ref: (place_ref[1], place_ref[0] * SUM_STEPS + i, 0))
        return pl.BlockSpec((rows, p.shape[2]), lambda i, place_ref: (place_ref[0] * SUM_STEPS + i, 0))

    def out_shape(p, keep):
        shape = (2 * p.shape[1], p.shape[2])
        return jax.ShapeDtypeStruct((N_CHIPS,) + shape if keep else shape, F32)

    return pl.pallas_call(
        body,
        name=name,
        grid_spec=pltpu.PrefetchScalarGridSpec(
            num_scalar_prefetch=1,
            grid=(SUM_STEPS,),
            in_specs=[in_spec(p) for p in parts],
            out_specs=[out_spec(p, k) for p, k in zip(parts, keep_chip_axis)],
        ),
        out_shape=[out_shape(p, k) for p, k in zip(parts, keep_chip_axis)],
        compiler_params=pltpu.CompilerParams(dimension_semantics=("arbitrary",), vmem_limit_bytes=VMEM_LIMIT),
    )(place, *parts)


def _join_exchange(bufs, keep_chip_axis):
    n = len(bufs)
    kept = [a for a in range(n) if keep_chip_axis[a]]
    base = n

    def copy(refs, sems, block, k, to):
        return pltpu.make_async_remote_copy(
            src_ref=block, dst_ref=block, send_sem=sems[0].at[k], recv_sem=sems[1].at[k], device_id=to, device_id_type=MESH
        )

    def mine(refs, a, cc):
        x, y, _, _ = _place()
        return _half(refs[a].at[2 * x + y] if keep_chip_axis[a] else refs[a], cc, 0)

    def before(step, steps, refs, outs, sems):
        def send():
            x, y, c, others = _place()
            for a in range(n):
                copy(refs, sems, mine(refs, a, c), a, (x, y, 1 - c)).start()
            for i, a in enumerate(kept):
                for j, (px, py) in enumerate(others):
                    copy(refs, sems, mine(refs, a, c), base + 6 * i + j, (px, py, c)).start()

        _when(step == 0, send)

    def after(step, steps, refs, outs, sems):
        def pass_on():
            x, y, c, others = _place()
            for i, a in enumerate(kept):
                for j, (px, py) in enumerate(others):
                    landed = _half(refs[a].at[2 * px + py], c, 0)
                    copy(refs, sems, landed, base + 6 * i + j, (px, py, c)).wait_recv()
                    copy(refs, sems, landed, base + 6 * i + 3 + j, (x, y, 1 - c)).start()

        def finish():
            x, y, c, others = _place()
            for a in range(n):
                copy(refs, sems, mine(refs, a, 1 - c), a, (x, y, 1 - c)).wait_recv()
            for i, a in enumerate(kept):
                for j, (px, py) in enumerate(others):
                    passed = _half(refs[a].at[2 * px + py], 1 - c, 0)
                    copy(refs, sems, passed, base + 6 * i + 3 + j, (x, y, 1 - c)).wait_recv()
            for a in range(n):
                copy(refs, sems, mine(refs, a, c), a, (x, y, 1 - c)).wait_send()
            for i, a in enumerate(kept):
                for k in range(6):
                    copy(refs, sems, mine(refs, a, c), base + 6 * i + k, (x, y, 1 - c)).wait_send()

        _when(step == steps // 2, pass_on)
        _when(step == steps - 1, finish)

    return _Exchange(bufs, [(_sds(b), a) for a, b in enumerate(bufs)], [n + 6 * len(kept)] * 2, before, after)


def _adamw(w, g, m, v):
    m = ADAM_B1 * m + (1.0 - ADAM_B1) * g
    v = ADAM_B2 * v + (1.0 - ADAM_B2) * (g * g)
    m_hat = m / (1.0 - ADAM_B1**ADAM_STEP)
    v_hat = v / (1.0 - ADAM_B2**ADAM_STEP)
    delta = -ADAM_LR * (m_hat / (jnp.sqrt(v_hat) + ADAM_EPS) + ADAM_WD * w)
    return delta, m, v


def _adamw_large(w, m, v, layer_grads, *, name):
    depth, rows, cols = w.shape
    tile = math.gcd(rows, ADAM_TILE)
    assert tile % 8 == 0

    def body(w_ref, m_ref, v_ref, *refs):
        g_refs, (g_out, d_out, m_out, v_out) = refs[:depth], refs[depth:]
        for l in range(depth):

            @pl.when(pl.program_id(0) == l)
            def _(l=l):
                g = g_refs[l][...]
                delta, m_new, v_new = _adamw(w_ref[0], g, m_ref[0], v_ref[0])
                g_out[0], d_out[0], m_out[0], v_out[0] = g, delta, m_new, v_new

    def stacked():
        return pl.BlockSpec((1, tile, cols), lambda l, i: (l, i, 0))

    def layer_spec(l):
        return pl.BlockSpec((tile, cols), lambda k, i: (jnp.where(k == l, i, 0), 0))

    shape = jax.ShapeDtypeStruct(w.shape, F32)
    return pl.pallas_call(
        body,
        name=name,
        grid=(depth, rows // tile),
        in_specs=[stacked(), stacked(), stacked()] + [layer_spec(l) for l in range(depth)],
        out_specs=[stacked()] * 4,
        out_shape=[shape] * 4,
        compiler_params=pltpu.CompilerParams(dimension_semantics=("arbitrary", "arbitrary"), vmem_limit_bytes=VMEM_LIMIT),
    )(w, m, v, *layer_grads)


def _adamw_small(ws, gs, ms, vs, *, name):
    n = len(ws)

    def body(*refs):
        for a in range(n):
            delta, m_new, v_new = _adamw(refs[a][...], refs[n + a][...], refs[2 * n + a][...], refs[3 * n + a][...])
            refs[4 * n + a][...] = delta
            refs[5 * n + a][...] = m_new
            refs[6 * n + a][...] = v_new

    shapes = [jax.ShapeDtypeStruct(w.shape, F32) for w in ws]
    outs = pl.pallas_call(body, name=name, out_shape=shapes * 3)(*ws, *gs, *ms, *vs)
    return outs[:n], outs[n : 2 * n], outs[2 * n :]


WEIGHT_NAMES = (
    "w_in", "conv_a_w", "sg_ln_g", "sg_ln_b", "sg_w", "sg_b", "pool_w", "pool_scale", "cc_dw_w", "cc_dw_b", "cc_ln_g",
    "cc_ln_b", "cc_pw_w", "w_kv", "w_out", "ln_g", "ln_b",
)
LARGE = ("w_in", "cc_pw_w", "w_kv", "w_out")
TAPS_ROWS = 48


def _unpack_small_grads(small, chip):
    out = {}
    for r, k in enumerate(("sg_ln_g", "sg_ln_b", "pool_scale", "cc_dw_b", "cc_ln_g", "cc_ln_b")):
        out[k] = small[RG_VEC + r]
    out["conv_a_w"] = lax.dynamic_slice_in_dim(small[RG_CONVA : RG_CONVA + CONV_A], chip * HEAD_DIM, HEAD_DIM, axis=1)
    out["cc_dw_w"] = lax.dynamic_slice_in_dim(small[RG_DW : RG_DW + CONV_D], chip * HEAD_DIM, HEAD_DIM, axis=1)
    cat = jnp.concatenate([small[RG_SGW : RG_SGW + CHUNK], small[RG_SGW + CHUNK : RG_SGW + 2 * CHUNK]], axis=1)
    out["sg_w"] = jnp.transpose(cat.reshape(CHUNK, N_SUB, CHUNK), (1, 0, 2))
    out["sg_b"] = small[RG_SGB : RG_SGB + CHUNK].reshape(CHUNK, N_SUB, HEAD_DIM).sum(-1).T
    pool = small[RG_POOL : RG_POOL + D_G]
    out["pool_w"] = jnp.stack(
        [pool[g * HEAD_DIM : (g + 1) * HEAD_DIM, g * HEAD_DIM : (g + 1) * HEAD_DIM] for g in range(N_SUB)]
    )
    out["ln_g"] = small[RG_LN : RG_LN + 4].reshape(D_MODEL)
    out["ln_b"] = small[RG_LN + 4 : RG_LN + 8].reshape(D_MODEL)
    return out


def kernel(x, mem, w_in, conv_a_w, sg_ln_g, sg_ln_b, sg_w, sg_b, pool_w, pool_scale, cc_dw_w, cc_dw_b, cc_ln_g, cc_ln_b, cc_pw_w, w_kv, w_out, ln_g, ln_b, loss_target, m_w_in, m_conv_a_w, m_sg_ln_g, m_sg_ln_b, m_sg_w, m_sg_b, m_pool_w, m_pool_scale, m_cc_dw_w, m_cc_dw_b, m_cc_ln_g, m_cc_ln_b, m_cc_pw_w, m_w_kv, m_w_out, m_ln_g, m_ln_b, v_w_in, v_conv_a_w, v_sg_ln_g, v_sg_ln_b, v_sg_w, v_sg_b, v_pool_w, v_pool_scale, v_cc_dw_w, v_cc_dw_b, v_cc_ln_g, v_cc_ln_b, v_cc_pw_w, v_w_kv, v_w_out, v_ln_g, v_ln_b):
    given = dict(locals())
    weights = {k: given[k] for k in WEIGHT_NAMES}
    chip = 2 * lax.axis_index("x") + lax.axis_index("y")
    place = jnp.stack([lax.axis_index("c"), chip]).astype(jnp.int32)

    x0, mem0 = x[0], mem[0]

    taps = jnp.concatenate([conv_a_w, cc_dw_w], axis=1)
    taps = jnp.pad(taps, ((0, 0), (0, TAPS_ROWS - taps.shape[1]), (0, 0)))

    def own_blocks(l):
        return _place_own_block(
            place, [w_in, w_out, w_kv, cc_pw_w, taps], l, [BF16, BF16, BF16, BF16, F32], name=f"place_weights{l}"
        )

    def layer_operands(l, gathered):
        g_in, g_out, g_kv, g_pw, g_taps = gathered
        taps_full = jnp.transpose(g_taps, (1, 0, 2)).reshape(TAPS_ROWS, D_G)
        full = dict(
            w_in=g_in,
            w_out=g_out.reshape(D_MIX, D_MODEL),
            w_kv=g_kv.reshape(D_MODEL, 2 * D_G),
            cc_pw_w=g_pw.reshape(D_G, D_G),
            conv_a_w=taps_full[0:CONV_A],
            cc_dw_w=taps_full[CONV_A : CONV_A + CONV_D],
            **{k: weights[k][l] for k in WEIGHT_NAMES if k not in LARGE + ("conv_a_w", "cc_dw_w")},
        )
        return _prepare_layer(mem0, full, l)

    def layer_grads(l, x_in, bwd, small, exchange=None):
        dproj, _, dwout, dkst, dvst, dpw, _ = bwd
        (dwin,), carried = _input_weight_grad(x_in, dproj, name=f"w_in_grad{l}", exchange=exchange)
        grads = [
            dwin,
            dwout.reshape(N_CHIPS, D_MIX // N_CHIPS, D_MODEL),
            _kv_backward(mem0, dkst, dvst, name=f"kv_bwd{l}").reshape(N_CHIPS, D_MODEL // N_CHIPS, 2 * D_G),
            dpw.reshape(N_CHIPS, D_G // N_CHIPS, D_G),
            small.reshape(N_CHIPS, RG_ROWS // N_CHIPS, D_G),
        ]
        return grads, carried

    n_red = 5
    keep = [False, False, False, False, True]

    wire = [BF16, BF16, BF16, BF16, F32]

    def reduced_layer(joined):
        r_in, r_out, r_kv, r_pw, small_all = joined
        out = _unpack_small_grads(small_all.reshape(RG_ROWS, D_G), chip)
        out.update(w_in=r_in, w_out=r_out, w_kv=r_kv, cc_pw_w=r_pw)
        return out

    blocks0, blocks1 = own_blocks(0), own_blocks(1)
    p0 = layer_operands(0, _run_exchange(_gather_exchange(blocks0), name="gather_weights0"))
    fwd0, gathered1 = _forward(0, x0, p0, None, exchange=_gather_exchange(blocks1))
    p1 = layer_operands(1, gathered1)
    x1 = fwd0[3]
    fwd1, _ = _forward(1, x1, p1, loss_target[0])

    bwd1, _ = _backward(1, fwd1[3], fwd1, p1)
    small1 = bwd1[6].at[RG_LOSS, :].set(fwd1[5][0, 0])
    grads1, _ = layer_grads(1, x1, bwd1, small1)
    (dx1,), received1 = _input_grad(bwd1[0], bwd1[1], p1["win"], name="input_grad1", exchange=_swap_exchange(grads1))
    pairs1 = _add_sibling_half(place, grads1, received1, wire, name="rs_pair1")
    bwd0, parts1 = _backward(0, dx1, fwd0, p0, exchange=_scatter_exchange(pairs1[:n_red], pairs1[n_red:]))
    halves1 = _sum_chip_blocks(place, parts1, keep, name="rs_sum1")
    grads0, joined1 = layer_grads(0, x0, bwd0, bwd0[6], exchange=_join_exchange(halves1, keep))
    loss = joined1[4].reshape(RG_ROWS, D_G)[RG_LOSS, 0]
    received0 = _run_exchange(_swap_exchange(grads0), name="rs_swap0")
    pairs0 = _add_sibling_half(place, grads0, received0, wire, name="rs_pair0")
    (grad_x,), parts0 = _input_grad(
        bwd0[0], bwd0[1], p0["win"], name="input_grad0", exchange=_scatter_exchange(pairs0[:n_red], pairs0[n_red:])
    )
    halves0 = _sum_chip_blocks(place, parts0, keep, name="rs_sum0")
    reduced = [reduced_layer(_run_exchange(_join_exchange(halves0, keep), name="rs_join0")), reduced_layer(joined1)]

    grad, delta, new_m, new_v = {}, {}, {}, {}
    for k in LARGE:
        w3 = weights[k]
        grad[k], delta[k], new_m[k], new_v[k] = _adamw_large(
            w3, given["m_" + k], given["v_" + k], [reduced[l][k] for l in range(DEPTH)], name=f"adamw_{k}"
        )
    small_names = [k for k in WEIGHT_NAMES if k not in LARGE]
    for k in small_names:
        grad[k] = jnp.stack([reduced[l][k] for l in range(DEPTH)])
    d_s, m_s, v_s = _adamw_small(
        [weights[k] for k in small_names],
        [grad[k] for k in small_names],
        [given["m_" + k] for k in small_names],
        [given["v_" + k] for k in small_names],
        name="adamw_small",
    )
    for a, k in enumerate(small_names):
        delta[k], new_m[k], new_v[k] = d_s[a], m_s[a], v_s[a]

    return (
        loss,
        grad_x[None],
        *[grad[k] for k in WEIGHT_NAMES],
        *[delta[k] for k in WEIGHT_NAMES],
        *[new_m[k] for k in WEIGHT_NAMES],
        *[new_v[k] for k in WEIGHT_NAMES],
    )
```

```python
import functools
import math

import jax
import jax.numpy as jnp
from jax import lax
from jax.experimental import pallas as pl
from jax.experimental.pallas import tpu as pltpu

F32 = jnp.float32
BF16 = jnp.bfloat16

D_MODEL = 1024
DEPTH = 2
D_G = 256
D_MIX = 5 * D_G
D_IN = 9 * D_G + D_MIX
N_SUB = 4
HEAD_DIM = 64
CONV_A = 3
CONV_D = 31
CHUNK = 128
MEM_LEN = 256
N_CHIPS = 4
W_IN_SHARD = D_IN // N_CHIPS
LN_EPS = 1e-5
ALPHA = (2.0 * DEPTH) ** 0.25
ATT_SCALE = 1.0 / math.sqrt(HEAD_DIM)
GELU_C = math.sqrt(2.0 / math.pi)
GELU_A = 0.044715

ADAM_LR = 0.001
ADAM_B1 = 0.9
ADAM_B2 = 0.999
ADAM_EPS = 1e-08
ADAM_WD = 0.01
ADAM_STEP = 10

C_XA, C_BA, C_CA, C_U, C_V, C_XC, C_DA, C_DG, C_Q, C_GATE = (D_G * i for i in range(10))

HALO_A = 8
HALO_C = 16
HALO_D = 32

RW_VEC = 0
RW_CONVA = 16
RW_DW = 24
RW_SGB = 56
RW_ROWS = RW_SGB + CHUNK

RG_VEC = 0
RG_CONVA = 16
RG_DW = 24
RG_SGW = 56
RG_SGB = RG_SGW + 2 * CHUNK
RG_POOL = RG_SGB + CHUNK
RG_LN = RG_POOL + D_G
RG_LOSS = 8
RG_ROWS = 768

VMEM_LIMIT = 62 * 1024 * 1024

AUX_CVD = 0
AUX_PM = D_G
AUX_P = 2 * D_G
AUX_COLS = AUX_P + N_SUB * MEM_LEN
SEQ_TILE = 256
FWD_TILE = 512
MM_TILE = 1024
ADAM_TILE = 512

MESH = pl.DeviceIdType.MESH
ANY = pl.BlockSpec(memory_space=pl.ANY)
NT = (((1,), (1,)), ((), ()))
TN = (((0,), (0,)), ((), ()))


def _dot(a, b):
    return jnp.dot(a, b, preferred_element_type=F32)


def _dot_nt(a, b):
    return lax.dot_general(a, b, NT, preferred_element_type=F32)


def _dot_tn(a, b):
    return lax.dot_general(a, b, TN, preferred_element_type=F32)


def _full(shape):
    zeros = (0,) * len(shape)
    return pl.BlockSpec(shape, lambda *_: zeros)


class _Exchange:
    def __init__(self, operands, outputs, sem_counts, before, after):
        self.operands, self.outputs, self.sem_counts, self.before, self.after = operands, outputs, sem_counts, before, after

    def specs(self, first_input, first_output):
        aliases = {first_input + src: first_output + j for j, (_, src) in enumerate(self.outputs) if src is not None}
        return (
            [ANY] * len(self.operands),
            [ANY] * len(self.outputs),
            [sds for sds, _ in self.outputs],
            [pltpu.SemaphoreType.DMA((k,)) for k in self.sem_counts],
            aliases,
        )

    def split(self, ins, outs):
        refs = list(ins)
        for j, (_, src) in enumerate(self.outputs):
            if src is not None:
                refs[src] = outs[j]
        return refs


def _when(cond, fn):
    if isinstance(cond, bool):
        if cond:
            fn()
    else:
        pl.when(cond)(fn)


def _run_exchange(exchange, *, name):
    n_in, n_out = len(exchange.operands), len(exchange.outputs)
    in_specs, out_specs, out_shape, sems, aliases = exchange.specs(0, 0)

    def body(*refs):
        ins, outs, sem_refs = refs[:n_in], refs[n_in : n_in + n_out], refs[n_in + n_out :]
        refs = exchange.split(ins, outs)
        exchange.before(0, 1, refs, outs, sem_refs)
        exchange.after(0, 1, refs, outs, sem_refs)

    return pl.pallas_call(
        body, name=name, in_specs=in_specs, out_specs=out_specs, out_shape=out_shape, scratch_shapes=sems,
        input_output_aliases=aliases,
    )(*exchange.operands)


def _gridded_call(body, *, name, steps, in_specs, out_specs, out_shape, scratch_shapes, operands, exchange=None):
    params = pltpu.CompilerParams(dimension_semantics=("arbitrary",), vmem_limit_bytes=VMEM_LIMIT)
    if exchange is None:
        outs = pl.pallas_call(
            body, name=name, grid=(steps,), in_specs=in_specs, out_specs=out_specs, out_shape=out_shape,
            scratch_shapes=scratch_shapes, compiler_params=params,
        )(*operands)
        return list(outs), []
    n_in, n_out, n_scr = len(in_specs), len(out_specs), len(scratch_shapes)
    x_in, x_out = len(exchange.operands), len(exchange.outputs)
    ex_in_specs, ex_out_specs, ex_out_shape, ex_sems, aliases = exchange.specs(n_in, n_out)

    def full(*refs):
        own_in, refs = refs[:n_in], refs[n_in:]
        ex_in, refs = refs[:x_in], refs[x_in:]
        own_out, refs = refs[:n_out], refs[n_out:]
        ex_out, refs = refs[:x_out], refs[x_out:]
        own_scr, sem_refs = refs[:n_scr], refs[n_scr:]
        ex_refs = exchange.split(ex_in, ex_out)
        step = pl.program_id(0)
        exchange.before(step, steps, ex_refs, ex_out, sem_refs)
        body(*own_in, *own_out, *own_scr)
        exchange.after(step, steps, ex_refs, ex_out, sem_refs)

    outs = pl.pallas_call(
        full, name=name, grid=(steps,), in_specs=in_specs + ex_in_specs, out_specs=out_specs + ex_out_specs,
        out_shape=out_shape + ex_out_shape, scratch_shapes=scratch_shapes + ex_sems, input_output_aliases=aliases,
        compiler_params=params,
    )(*operands, *exchange.operands)
    return list(outs[:n_out]), list(outs[n_out:])


def _sigmoid(x):
    return 0.5 * jnp.tanh(0.5 * x) + 0.5


def _gelu(x):
    t = jnp.tanh(GELU_C * (x + GELU_A * x * x * x))
    return 0.5 * x * (1.0 + t), t


def _gelu_grad(x, t):
    return 0.5 * (1.0 + t) + 0.5 * x * (1.0 - t * t) * (GELU_C * (1.0 + 3.0 * GELU_A * x * x))


def _normalize(v):
    mu = jnp.mean(v, axis=-1, keepdims=True)
    d = v - mu
    var = jnp.mean(d * d, axis=-1, keepdims=True)
    rstd = lax.rsqrt(var + LN_EPS)
    return d * rstd, rstd


def _normalize_grad(dhat, hat, rstd):
    m1 = jnp.mean(dhat, axis=-1, keepdims=True)
    m2 = jnp.mean(dhat * hat, axis=-1, keepdims=True)
    return rstd * (dhat - m1 - hat * m2)


def _lane(width=D_G):
    return lax.broadcasted_iota(jnp.int32, (1, width), 1)


def _head_masks():
    head = _lane() // HEAD_DIM
    return [(head == h).astype(F32) for h in range(N_SUB)]


def _stack_heads(v, masks):
    return jnp.concatenate([v * m for m in masks], axis=0)


def _tril_mask_cat():
    t = lax.broadcasted_iota(jnp.int32, (CHUNK, N_SUB * CHUNK), 0)
    s = lax.broadcasted_iota(jnp.int32, (CHUNK, N_SUB * CHUNK), 1) % CHUNK
    return s <= t


def _triu_mask_cat():
    s = lax.broadcasted_iota(jnp.int32, (CHUNK, N_SUB * CHUNK), 0)
    t = lax.broadcasted_iota(jnp.int32, (CHUNK, N_SUB * CHUNK), 1) % CHUNK
    return t >= s


def _pool_select(a2, a4, a8, a16):
    lane = _lane()
    return jnp.where(lane < 64, a2, jnp.where(lane < 128, a4, jnp.where(lane < 192, a8, a16)))


def _pool_inv_count(row0, rows):
    t = row0 + lax.broadcasted_iota(jnp.int32, (HALO_C, D_G), 0)
    lane = lax.broadcasted_iota(jnp.int32, (HALO_C, D_G), 1)
    win = jnp.where(lane < 64, 2, jnp.where(lane < 128, 4, jnp.where(lane < 192, 8, 16)))
    head = 1.0 / jnp.minimum(t + 1, win).astype(F32)
    inv_win = jnp.broadcast_to(_pool_select(0.5, 0.25, 0.125, 0.0625), (rows - HALO_C, D_G))
    return jnp.concatenate([head, inv_win], axis=0)


def _trailing_window_sum(halo, cur):
    e = jnp.concatenate([halo, cur], axis=0)
    s2 = e + pltpu.roll(e, 1, 0)
    s4 = s2 + pltpu.roll(s2, 2, 0)
    s8 = s4 + pltpu.roll(s4, 4, 0)
    s16 = s8 + pltpu.roll(s8, 8, 0)
    return _pool_select(s2, s4, s8, s16)[HALO_C:]


def _leading_window_sum(cur, halo):
    e = jnp.concatenate([cur, halo], axis=0)
    n = e.shape[0]
    s2 = e + pltpu.roll(e, n - 1, 0)
    s4 = s2 + pltpu.roll(s2, n - 2, 0)
    s8 = s4 + pltpu.roll(s4, n - 4, 0)
    s16 = s8 + pltpu.roll(s8, n - 8, 0)
    return _pool_select(s2, s4, s8, s16)[: cur.shape[0]]


def _softmax_blocks(sc):
    out = []
    for h in range(N_SUB):
        s = sc[:, h * MEM_LEN : (h + 1) * MEM_LEN]
        e = jnp.exp(s - jnp.max(s, axis=-1, keepdims=True))
        out.append(e * (1.0 / jnp.sum(e, axis=-1, keepdims=True)))
    return jnp.concatenate(out, axis=-1)


STRIP = 32
SHIFTS = 8


def _fill_shifts(buf):
    n = buf.shape[1] - SHIFTS
    for r in range(1, SHIFTS):
        buf[r, 0:n, :] = buf[0, r : r + n, :]


def _shifted(buf, off, rows):
    r = off % SHIFTS
    return buf[r, off - r : off - r + rows, :]


def _sgu_mix(vn, wcat_b, sgb, masks):
    vbd = _stack_heads(vn, masks).astype(BF16)
    return _dot(wcat_b, vbd) + sgb, vbd


def _layer_forward(x, win, kst, vst, wout, sw, wcat, poolw, pww, ln, tgt, *, name, exchange=None):
    seq = x.shape[0]
    tile = min(FWD_TILE, seq)
    n_tiles = seq // tile
    last = tgt is not None

    def body(*refs):
        x_ref, win_ref, kst_ref, vst_ref, wout_ref, sw_ref, wcat_ref, pool_ref, pw_ref, ln_ref = refs[:10]
        refs = refs[10:]
        if last:
            tgt_ref, refs = refs[0], refs[1:]
        proj_ref, y_ref, z_ref, out_ref, aux_ref = refs[:5]
        refs = refs[5:]
        if last:
            loss_ref, refs = refs[0], refs[1:]
        pbuf, xchalo, gbuf = refs
        i = pl.program_id(0)

        @pl.when(i == 0)
        def _():
            pbuf[0:HALO_A, :] = jnp.zeros((HALO_A, D_G), F32)
            xchalo[...] = jnp.zeros((HALO_C, D_G), F32)
            gbuf[0, 0:HALO_D, :] = jnp.zeros((HALO_D, D_G), F32)
            if last:
                loss_ref[...] = jnp.zeros((8, 128), F32)

        xt = x_ref[...]
        xb = xt.astype(BF16)

        blocks = {}

        def project(k):
            blocks[k] = _dot(xb, win_ref[k])
            proj_ref[:, k * W_IN_SHARD : (k + 1) * W_IN_SHARD] = blocks[k]

        def cols(start, width=D_G):
            parts, c = [], start
            while c < start + width:
                k, lo = divmod(c, W_IN_SHARD)
                hi = min(W_IN_SHARD, lo + start + width - c)
                parts.append(blocks[k][:, lo:hi])
                c += hi - lo
            return parts[0] if len(parts) == 1 else jnp.concatenate(parts, axis=1)

        project(0)
        project(1)
        masks = _head_masks()

        pbuf[HALO_A : HALO_A + tile, :] = cols(C_CA) * cols(C_XA)
        cv = jnp.zeros((tile, D_G), F32)
        for k in range(CONV_A):
            off = HALO_A - (CONV_A - 1) + k
            cv = cv + sw_ref[RW_CONVA + k : RW_CONVA + k + 1, :] * pbuf[off : off + tile, :]
        y_ref[:, 0:D_G] = cols(C_BA) * cv
        pbuf[0:HALO_A, :] = pbuf[tile : tile + HALO_A, :]

        project(2)

        ua, _ = _gelu(cols(C_U))
        vg, _ = _gelu(cols(C_V))
        vhat, _ = _normalize(vg)
        vn = vhat * sw_ref[RW_VEC : RW_VEC + 1, :] + sw_ref[RW_VEC + 1 : RW_VEC + 2, :]
        wcat_b = jnp.where(_tril_mask_cat(), wcat_ref[...], 0.0).astype(BF16)
        sgb = sw_ref[RW_SGB : RW_SGB + CHUNK, :]
        for j in range(tile // CHUNK):
            rows = slice(j * CHUNK, (j + 1) * CHUNK)
            mixed, _ = _sgu_mix(vn[rows], wcat_b, sgb, masks)
            y_ref[rows, D_G : 2 * D_G] = ua[rows] * mixed

        xc = cols(C_XC)
        wsum = _trailing_window_sum(xchalo[...], xc)
        pm = wsum * _pool_inv_count(i * tile, tile) - xc
        aux_ref[:, AUX_PM : AUX_PM + D_G] = pm
        y_ref[:, 2 * D_G : 3 * D_G] = _dot(pm.astype(BF16), pool_ref[...]) * sw_ref[RW_VEC + 2 : RW_VEC + 3, :]
        xchalo[...] = xc[tile - HALO_C :, :]

        project(3)

        gbuf[0, HALO_D : HALO_D + tile, :] = cols(C_DA) * _sigmoid(cols(C_DG))
        _fill_shifts(gbuf)
        for r0 in range(0, tile, STRIP):
            acc = jnp.zeros((STRIP, D_G), F32) + sw_ref[RW_VEC + 3 : RW_VEC + 4, :]
            for k in range(CONV_D):
                off = HALO_D - (CONV_D - 1) + k
                acc = acc + sw_ref[RW_DW + k : RW_DW + k + 1, :] * _shifted(gbuf, off + r0, STRIP)
            aux_ref[r0 : r0 + STRIP, AUX_CVD : AUX_CVD + D_G] = acc
        nhat, _ = _normalize(aux_ref[:, AUX_CVD : AUX_CVD + D_G])
        nrm = nhat * sw_ref[RW_VEC + 4 : RW_VEC + 5, :] + sw_ref[RW_VEC + 5 : RW_VEC + 6, :]
        y_ref[:, 3 * D_G : 4 * D_G] = _dot((nrm * _sigmoid(nrm)).astype(BF16), pw_ref[...])
        gbuf[0, 0:HALO_D, :] = gbuf[0, tile : tile + HALO_D, :]

        qb = cols(C_Q).astype(BF16)
        p_all = _softmax_blocks(_dot_nt(qb, kst_ref[...]) * ATT_SCALE)
        aux_ref[:, AUX_P:] = p_all
        y_ref[:, 4 * D_G : 5 * D_G] = _dot(p_all.astype(BF16), vst_ref[...])

        gate = cols(C_GATE, D_MIX)
        hid = y_ref[...] * (gate * _sigmoid(gate))
        z = ALPHA * xt + _dot(hid.astype(BF16), wout_ref[...])
        z_ref[...] = z
        zhat, _ = _normalize(z)
        xn = zhat * ln_ref[0:1, :] + ln_ref[1:2, :]
        if last:
            err = xn - tgt_ref[...]
            out_ref[...] = err * (1.0 / D_MODEL)
            loss_ref[...] += jnp.sum(err * err) * (0.5 / D_MODEL)
        else:
            out_ref[...] = xn

    def rows(width):
        return pl.BlockSpec((tile, width), lambda i: (i, 0))

    operands = [x, win, kst, vst, wout, sw, wcat, poolw, pww, ln]
    in_specs = [rows(D_MODEL)] + [_full(a.shape) for a in operands[1:]]
    out_shape = [
        jax.ShapeDtypeStruct((seq, D_IN), F32),
        jax.ShapeDtypeStruct((seq, D_MIX), F32),
        jax.ShapeDtypeStruct((seq, D_MODEL), F32),
        jax.ShapeDtypeStruct((seq, D_MODEL), F32),
        jax.ShapeDtypeStruct((seq, AUX_COLS), F32),
    ]
    out_specs = [rows(D_IN), rows(D_MIX), rows(D_MODEL), rows(D_MODEL), rows(AUX_COLS)]
    if last:
        operands.append(tgt)
        in_specs.append(rows(D_MODEL))
        out_shape.append(jax.ShapeDtypeStruct((8, 128), F32))
        out_specs.append(_full((8, 128)))
    return _gridded_call(
        body,
        name=name,
        steps=n_tiles,
        in_specs=in_specs,
        out_specs=out_specs,
        out_shape=out_shape,
        scratch_shapes=[
            pltpu.VMEM((HALO_A + tile, D_G), F32),
            pltpu.VMEM((HALO_C, D_G), F32),
            pltpu.VMEM((SHIFTS, HALO_D + tile, D_G), F32),
        ],
        operands=operands,
        exchange=exchange,
    )


def _layer_backward(dxn, z, proj, y, cvd, kst, vst, wout, sw, wcat, wcat_t, poolw, pww, ln, *, name, exchange=None):
    seq = dxn.shape[0]
    tile = min(SEQ_TILE, seq)
    n_tiles = seq // tile
    halo_blocks = tile // HALO_D

    def body(
        dxn_ref, z_ref, proj_ref, halo_ref, y_ref, aux_ref, kst_ref, vst_ref, wout_ref, sw_ref, wcat_ref, wcat_t_ref,
        pool_ref, pw_ref, ln_ref, dproj_ref, dz_ref, dwout_ref, dkst_ref, dvst_ref, dpw_ref, sg_ref,
        pbuf, dcvbuf, rhalo, gbuf, dgbuf, dwacc,
    ):
        i = pl.program_id(0)
        ti = n_tiles - 1 - i

        @pl.when(i == 0)
        def _():
            dwout_ref[...] = jnp.zeros(dwout_ref.shape, F32)
            dkst_ref[...] = jnp.zeros(dkst_ref.shape, F32)
            dvst_ref[...] = jnp.zeros(dvst_ref.shape, F32)
            dpw_ref[...] = jnp.zeros(dpw_ref.shape, F32)
            sg_ref[...] = jnp.zeros(sg_ref.shape, F32)
            dwacc[...] = jnp.zeros(dwacc.shape, F32)
            dcvbuf[tile : tile + HALO_A, :] = jnp.zeros((HALO_A, D_G), F32)
            rhalo[...] = jnp.zeros((HALO_C, D_G), F32)
            dgbuf[0, tile : tile + HALO_D, :] = jnp.zeros((HALO_D, D_G), F32)

        def acc_row(row, val):
            sg_ref[row : row + 1, :] += jnp.sum(val, axis=0, keepdims=True)

        masks = _head_masks()
        has_past = (ti > 0).astype(F32)

        zhat, zrstd = _normalize(z_ref[...])
        dxn_t = dxn_ref[...]
        dlg = jnp.sum(dxn_t * zhat, axis=0, keepdims=True)
        dlb = jnp.sum(dxn_t, axis=0, keepdims=True)
        for j in range(D_MODEL // D_G):
            sg_ref[RG_LN + j : RG_LN + j + 1, :] += dlg[:, j * D_G : (j + 1) * D_G]
            sg_ref[RG_LN + 4 + j : RG_LN + 5 + j, :] += dlb[:, j * D_G : (j + 1) * D_G]
        dz = _normalize_grad(dxn_t * ln_ref[0:1, :], zhat, zrstd)
        dz_ref[...] = dz
        dzb = dz.astype(BF16)

        gate = proj_ref[:, C_GATE:]
        sgm = _sigmoid(gate)
        silu = gate * sgm
        yc = y_ref[...]
        dwout_ref[...] += _dot_tn((yc * silu).astype(BF16), dzb)
        dh = _dot_nt(dzb, wout_ref[...])
        dproj_ref[:, C_GATE:] = (dh * yc * (sgm * (1.0 + gate * (1.0 - sgm)))).astype(BF16)
        dy = dh * silu

        dya = dy[:, 0:D_G]
        xa = proj_ref[:, C_XA : C_XA + D_G]
        ba = proj_ref[:, C_BA : C_BA + D_G]
        ca = proj_ref[:, C_CA : C_CA + D_G]
        past = slice(HALO_D - HALO_A, HALO_D)
        pbuf[0:HALO_A, :] = halo_ref[past, C_CA : C_CA + D_G] * halo_ref[past, C_XA : C_XA + D_G] * has_past
        pbuf[HALO_A : HALO_A + tile, :] = ca * xa
        cv = jnp.zeros((tile, D_G), F32)
        for k in range(CONV_A):
            off = HALO_A - (CONV_A - 1) + k
            cv = cv + sw_ref[RW_CONVA + k : RW_CONVA + k + 1, :] * pbuf[off : off + tile, :]
        dproj_ref[:, C_BA : C_BA + D_G] = (dya * cv).astype(BF16)
        dcv = dya * ba
        dcvbuf[0:tile, :] = dcv
        dp = jnp.zeros((tile, D_G), F32)
        for k in range(CONV_A):
            off = HALO_A - (CONV_A - 1) + k
            acc_row(RG_CONVA + k, dcv * pbuf[off : off + tile, :])
            back = CONV_A - 1 - k
            dp = dp + sw_ref[RW_CONVA + k : RW_CONVA + k + 1, :] * dcvbuf[back : back + tile, :]
        dproj_ref[:, C_CA : C_CA + D_G] = (dp * xa).astype(BF16)
        dproj_ref[:, C_XA : C_XA + D_G] = (dp * ca).astype(BF16)
        dcvbuf[tile : tile + HALO_A, :] = dcvbuf[0:HALO_A, :]

        dyb = dy[:, D_G : 2 * D_G]
        u = proj_ref[:, C_U : C_U + D_G]
        v = proj_ref[:, C_V : C_V + D_G]
        ua, ut = _gelu(u)
        vg, vt = _gelu(v)
        vhat, vrstd = _normalize(vg)
        sg_g = sw_ref[RW_VEC : RW_VEC + 1, :]
        vn = vhat * sg_g + sw_ref[RW_VEC + 1 : RW_VEC + 2, :]
        tril = _tril_mask_cat()
        wcat_b = jnp.where(tril, wcat_ref[...], 0.0).astype(BF16)
        wcat_tb = jnp.where(_triu_mask_cat(), wcat_t_ref[...], 0.0).astype(BF16)
        sgb = sw_ref[RW_SGB : RW_SGB + CHUNK, :]
        dmixed = dyb * ua
        dvn_parts = []
        du_parts = []
        dwcat = jnp.zeros((CHUNK, N_SUB * CHUNK), F32)
        dsgb = jnp.zeros((CHUNK, D_G), F32)
        for j in range(tile // CHUNK):
            rows = slice(j * CHUNK, (j + 1) * CHUNK)
            mixed, vbd = _sgu_mix(vn[rows], wcat_b, sgb, masks)
            du_parts.append(dyb[rows] * mixed)
            dmx = dmixed[rows]
            dsgb = dsgb + dmx
            dwcat = dwcat + _dot_nt(dmx.astype(BF16), vbd)
            dvn_parts.append(_dot(wcat_tb, _stack_heads(dmx, masks).astype(BF16)))
        dwcat = jnp.where(tril, dwcat, 0.0)
        sg_ref[RG_SGW : RG_SGW + CHUNK, :] += dwcat[:, 0:D_G]
        sg_ref[RG_SGW + CHUNK : RG_SGW + 2 * CHUNK, :] += dwcat[:, D_G:]
        sg_ref[RG_SGB : RG_SGB + CHUNK, :] += dsgb
        dvn = jnp.concatenate(dvn_parts, axis=0)
        du_act = jnp.concatenate(du_parts, axis=0)
        acc_row(RG_VEC, dvn * vhat)
        acc_row(RG_VEC + 1, dvn)
        dvg = _normalize_grad(dvn * sg_g, vhat, vrstd)
        dproj_ref[:, C_U : C_U + D_G] = (du_act * _gelu_grad(u, ut)).astype(BF16)
        dproj_ref[:, C_V : C_V + D_G] = (dvg * _gelu_grad(v, vt)).astype(BF16)

        dyc = dy[:, 2 * D_G : 3 * D_G]
        inv_cnt = _pool_inv_count(ti * tile, tile)
        pmb = aux_ref[:, AUX_PM : AUX_PM + D_G].astype(BF16)
        pool_b = pool_ref[...]
        scale = sw_ref[RW_VEC + 2 : RW_VEC + 3, :]
        acc_row(RG_VEC + 2, dyc * _dot(pmb, pool_b))
        dpre = (dyc * scale).astype(BF16)
        sg_ref[RG_POOL : RG_POOL + D_G, :] += _dot_tn(pmb, dpre)
        dpm = _dot_nt(dpre, pool_b)
        r = dpm * inv_cnt
        dproj_ref[:, C_XC : C_XC + D_G] = (_leading_window_sum(r, rhalo[...]) - dpm).astype(BF16)
        rhalo[...] = r[0:HALO_C, :]

        dyd = dy[:, 3 * D_G : 4 * D_G]
        da = proj_ref[:, C_DA : C_DA + D_G]
        sgd = _sigmoid(proj_ref[:, C_DG : C_DG + D_G])
        gbuf[0, 0:HALO_D, :] = halo_ref[:, C_DA : C_DA + D_G] * _sigmoid(halo_ref[:, C_DG : C_DG + D_G]) * has_past
        gbuf[0, HALO_D : HALO_D + tile, :] = da * sgd
        _fill_shifts(gbuf)
        nhat, nrstd = _normalize(aux_ref[:, AUX_CVD : AUX_CVD + D_G])
        cc_g = sw_ref[RW_VEC + 4 : RW_VEC + 5, :]
        nrm = nhat * cc_g + sw_ref[RW_VEC + 5 : RW_VEC + 6, :]
        sgn = _sigmoid(nrm)
        dydb = dyd.astype(BF16)
        dpw_ref[...] += _dot_tn((nrm * sgn).astype(BF16), dydb)
        dn = _dot_nt(dydb, pw_ref[...]) * (sgn * (1.0 + nrm * (1.0 - sgn)))
        acc_row(RG_VEC + 4, dn * nhat)
        acc_row(RG_VEC + 5, dn)
        dcvd = _normalize_grad(dn * cc_g, nhat, nrstd)
        acc_row(RG_VEC + 3, dcvd)
        dgbuf[0, 0:tile, :] = dcvd
        _fill_shifts(dgbuf)
        for r0 in range(0, tile, STRIP):
            d_s = dgbuf[0, r0 : r0 + STRIP, :]
            dg = jnp.zeros((STRIP, D_G), F32)
            for k in range(CONV_D):
                off = HALO_D - (CONV_D - 1) + k
                prod = d_s * _shifted(gbuf, off + r0, STRIP)
                part = prod[0:8]
                for q in range(8, STRIP, 8):
                    part = part + prod[q : q + 8]
                dwacc[8 * k : 8 * k + 8, :] += part
                back = CONV_D - 1 - k
                dg = dg + sw_ref[RW_DW + k : RW_DW + k + 1, :] * _shifted(dgbuf, back + r0, STRIP)
            da_s = proj_ref[r0 : r0 + STRIP, C_DA : C_DA + D_G]
            sgd_s = _sigmoid(proj_ref[r0 : r0 + STRIP, C_DG : C_DG + D_G])
            dproj_ref[r0 : r0 + STRIP, C_DA : C_DA + D_G] = (dg * sgd_s).astype(BF16)
            dproj_ref[r0 : r0 + STRIP, C_DG : C_DG + D_G] = (dg * da_s * sgd_s * (1.0 - sgd_s)).astype(BF16)
        dgbuf[0, tile : tile + HALO_D, :] = dgbuf[0, 0:HALO_D, :]

        @pl.when(i == n_tiles - 1)
        def _():
            for k in range(CONV_D):
                sg_ref[RG_DW + k : RG_DW + k + 1, :] = jnp.sum(dwacc[8 * k : 8 * k + 8, :], axis=0, keepdims=True)

        dyeb = dy[:, 4 * D_G : 5 * D_G].astype(BF16)
        qb = proj_ref[:, C_Q : C_Q + D_G].astype(BF16)
        kst_b = kst_ref[...]
        p_all = aux_ref[:, AUX_P:]
        dvst_ref[...] += _dot_tn(p_all.astype(BF16), dyeb)
        dp_all = _dot_nt(dyeb, vst_ref[...])
        ds = []
        for h in range(N_SUB):
            blk = slice(h * MEM_LEN, (h + 1) * MEM_LEN)
            p, dpb = p_all[:, blk], dp_all[:, blk]
            ds.append(p * (dpb - jnp.sum(dpb * p, axis=-1, keepdims=True)))
        dsb = (jnp.concatenate(ds, axis=-1) * ATT_SCALE).astype(BF16)
        dproj_ref[:, C_Q : C_Q + D_G] = _dot(dsb, kst_b).astype(BF16)
        dkst_ref[...] += _dot_tn(dsb, qb)

    def rows(width):
        return pl.BlockSpec((tile, width), lambda i: (n_tiles - 1 - i, 0))

    halo_spec = pl.BlockSpec((HALO_D, D_IN), lambda i: (jnp.maximum((n_tiles - 1 - i) * halo_blocks - 1, 0), 0))
    weights = [kst, vst, wout, sw, wcat, wcat_t, poolw, pww, ln]
    acc_shapes = [(D_MIX, D_MODEL), (N_SUB * MEM_LEN, D_G), (N_SUB * MEM_LEN, D_G), (D_G, D_G), (RG_ROWS, D_G)]
    return _gridded_call(
        body,
        name=name,
        steps=n_tiles,
        in_specs=[rows(D_MODEL), rows(D_MODEL), rows(D_IN), halo_spec, rows(D_MIX), rows(AUX_COLS)]
        + [_full(a.shape) for a in weights],
        out_specs=[rows(D_IN), rows(D_MODEL)] + [_full(s) for s in acc_shapes],
        out_shape=[jax.ShapeDtypeStruct((seq, D_IN), BF16), jax.ShapeDtypeStruct((seq, D_MODEL), F32)]
        + [jax.ShapeDtypeStruct(s, F32) for s in acc_shapes],
        scratch_shapes=[
            pltpu.VMEM((HALO_A + tile, D_G), F32),
            pltpu.VMEM((tile + HALO_A, D_G), F32),
            pltpu.VMEM((HALO_C, D_G), F32),
            pltpu.VMEM((SHIFTS, HALO_D + tile, D_G), F32),
            pltpu.VMEM((SHIFTS, tile + HALO_D, D_G), F32),
            pltpu.VMEM((8 * CONV_D, D_G), F32),
        ],
        operands=[dxn, z, proj, proj, y, cvd, *weights],
        exchange=exchange,
    )


def _kv_forward(mem, wkv, *, name):
    def body(mem_ref, wkv_ref, kst_ref, vst_ref):
        kv = _dot(mem_ref[...].astype(BF16), wkv_ref[...])
        masks = _head_masks()
        kst_ref[...] = _stack_heads(kv[:, 0:D_G], masks).astype(BF16)
        vst_ref[...] = _stack_heads(kv[:, D_G:], masks).astype(BF16)

    shape = jax.ShapeDtypeStruct((N_SUB * MEM_LEN, D_G), BF16)
    return pl.pallas_call(body, name=name, out_shape=[shape, shape])(mem, wkv)


def _kv_backward(mem, dkst, dvst, *, name):
    def body(mem_ref, dkst_ref, dvst_ref, dwkv_ref):
        masks = _head_masks()
        memb = mem_ref[...].astype(BF16)
        for col, ref in ((0, dkst_ref), (D_G, dvst_ref)):
            d = jnp.zeros((MEM_LEN, D_G), F32)
            for h in range(N_SUB):
                d = d + ref[h * MEM_LEN : (h + 1) * MEM_LEN, :] * masks[h]
            dwkv_ref[:, col : col + D_G] = _dot_tn(memb, d.astype(BF16))

    return pl.pallas_call(body, name=name, out_shape=jax.ShapeDtypeStruct((D_MODEL, 2 * D_G), F32))(mem, dkst, dvst)


def _input_grad(dproj, dz, win, *, name, exchange=None):
    seq = dproj.shape[0]
    tile = min(MM_TILE // 2, seq)

    def body(dproj_ref, dz_ref, win_ref, dx_ref):
        acc = ALPHA * dz_ref[...]
        for k in range(N_CHIPS):
            acc = acc + _dot_nt(dproj_ref[:, k * W_IN_SHARD : (k + 1) * W_IN_SHARD], win_ref[k])
        dx_ref[...] = acc

    return _gridded_call(
        body,
        name=name,
        steps=seq // tile,
        in_specs=[
            pl.BlockSpec((tile, D_IN), lambda i: (i, 0)),
            pl.BlockSpec((tile, D_MODEL), lambda i: (i, 0)),
            _full(win.shape),
        ],
        out_specs=[pl.BlockSpec((tile, D_MODEL), lambda i: (i, 0))],
        out_shape=[jax.ShapeDtypeStruct((seq, D_MODEL), F32)],
        scratch_shapes=[],
        operands=[dproj, dz, win],
        exchange=exchange,
    )


def _input_weight_grad(x, dproj, *, name, exchange=None):
    seq = x.shape[0]
    tile = min(MM_TILE, seq)
    n_rows = seq // tile

    def body(x_ref, dproj_ref, dwin_ref):
        @pl.when(pl.program_id(0) % n_rows == 0)
        def _():
            dwin_ref[...] = jnp.zeros(dwin_ref.shape, F32)

        dwin_ref[0] += _dot_tn(x_ref[...].astype(BF16), dproj_ref[...])

    return _gridded_call(
        body,
        name=name,
        steps=N_CHIPS * n_rows,
        in_specs=[
            pl.BlockSpec((tile, D_MODEL), lambda s: (s % n_rows, 0)),
            pl.BlockSpec((tile, W_IN_SHARD), lambda s: (s % n_rows, s // n_rows)),
        ],
        out_specs=[pl.BlockSpec((1, D_MODEL, W_IN_SHARD), lambda s: (s // n_rows, 0, 0))],
        out_shape=[jax.ShapeDtypeStruct((N_CHIPS, D_MODEL, W_IN_SHARD), F32)],
        scratch_shapes=[],
        operands=[x, dproj],
        exchange=exchange,
    )


def _expand_sgb(sg_b):
    return jnp.repeat(sg_b.T, HEAD_DIM, axis=1)


def _pack_small_weights(sg_ln_g, sg_ln_b, pool_scale, cc_dw_b, cc_ln_g, cc_ln_b, conv_a_w, cc_dw_w, sg_b):
    vec = jnp.stack([sg_ln_g, sg_ln_b, pool_scale, cc_dw_b, cc_ln_g, cc_ln_b])
    return jnp.concatenate(
        [
            jnp.pad(vec, ((0, RW_CONVA - RW_VEC - 6), (0, 0))),
            jnp.pad(conv_a_w, ((0, RW_DW - RW_CONVA - CONV_A), (0, 0))),
            jnp.pad(cc_dw_w, ((0, RW_SGB - RW_DW - CONV_D), (0, 0))),
            _expand_sgb(sg_b),
        ]
    )


def _sg_w_cat(sg_w):
    cat = jnp.transpose(sg_w, (1, 0, 2)).reshape(CHUNK, N_SUB * CHUNK)
    cat_t = jnp.transpose(sg_w, (2, 0, 1)).reshape(CHUNK, N_SUB * CHUNK)
    return cat, cat_t


def _pool_block_diag(pool_w):
    tiled = jnp.tile(pool_w.reshape(D_G, HEAD_DIM), (1, N_SUB))
    row = lax.broadcasted_iota(jnp.int32, (D_G, D_G), 0) // HEAD_DIM
    col = lax.broadcasted_iota(jnp.int32, (D_G, D_G), 1) // HEAD_DIM
    return jnp.where(row == col, tiled, 0.0)


def _prepare_layer(mem, w, l):
    cat, cat_t = _sg_w_cat(w["sg_w"])
    kst, vst = _kv_forward(mem, w["w_kv"], name=f"kv_fwd{l}")
    return dict(
        win=w["w_in"],
        wout=w["w_out"],
        pww=w["cc_pw_w"],
        sw=_pack_small_weights(
            w["sg_ln_g"], w["sg_ln_b"], w["pool_scale"], w["cc_dw_b"], w["cc_ln_g"], w["cc_ln_b"],
            w["conv_a_w"], w["cc_dw_w"], w["sg_b"],
        ),
        wcat=cat,
        wcat_t=cat_t,
        poolw=_pool_block_diag(w["pool_w"]).astype(BF16),
        ln=jnp.stack([w["ln_g"], w["ln_b"]]),
        kst=kst,
        vst=vst,
    )


def _forward(l, h, p, tgt, exchange=None):
    return _layer_forward(
        h, p["win"], p["kst"], p["vst"], p["wout"], p["sw"], p["wcat"], p["poolw"], p["pww"], p["ln"], tgt,
        name=f"layer_fwd{l}", exchange=exchange,
    )


def _backward(l, dxn, s, p, exchange=None):
    return _layer_backward(
        dxn, s[2], s[0], s[1], s[4], p["kst"], p["vst"], p["wout"], p["sw"], p["wcat"], p["wcat_t"], p["poolw"],
        p["pww"], p["ln"], name=f"layer_bwd{l}", exchange=exchange,
    )


def _place():
    x, y, c = lax.axis_index("x"), lax.axis_index("y"), lax.axis_index("c")
    others = [(1 - x, y), (x, 1 - y), (1 - x, 1 - y)]
    return x, y, c, others


def _half(ref, c, axis):
    n = ref.shape[axis] // 2
    if axis == 0:
        return ref.at[pl.ds(c * n, n)]
    return ref.at[:, pl.ds(c * n, n)]


def _place_own_block(place, stacked, layer, dtypes, *, name):
    n = len(stacked)

    def body(place_ref, *refs):
        for a in range(n):
            refs[n + a][...] = refs[a][...].astype(dtypes[a])

    def block(s):
        return (1,) + s.shape[1:]

    return pl.pallas_call(
        body,
        name=name,
        grid_spec=pltpu.PrefetchScalarGridSpec(
            num_scalar_prefetch=1,
            grid=(1,),
            in_specs=[pl.BlockSpec(block(s), lambda i, place_ref: (layer, 0, 0)) for s in stacked],
            out_specs=[pl.BlockSpec(block(s), lambda i, place_ref: (place_ref[1], 0, 0)) for s in stacked],
        ),
        out_shape=[jax.ShapeDtypeStruct((N_CHIPS,) + s.shape[1:], dt) for s, dt in zip(stacked, dtypes)],
        compiler_params=pltpu.CompilerParams(dimension_semantics=("arbitrary",), vmem_limit_bytes=VMEM_LIMIT),
    )(place, *stacked)


def _sds(a):
    return jax.ShapeDtypeStruct(a.shape, a.dtype)


def _gather_exchange(bufs):
    n = len(bufs)

    def remote(sems, block, k, to):
        return pltpu.make_async_remote_copy(
            src_ref=block, dst_ref=block, send_sem=sems[0].at[k], recv_sem=sems[1].at[k], device_id=to, device_id_type=MESH
        )

    def before(step, steps, refs, outs, sems):
        def send():
            x, y, c, others = _place()
            for j, (px, py) in enumerate(others):
                for a in range(n):
                    remote(sems, _half(refs[a].at[2 * x + y], c, 0), 3 * a + j, (px, py, c)).start()

        _when(step == 0, send)

    def after(step, steps, refs, outs, sems):
        def pass_on():
            x, y, c, others = _place()
            for j, (px, py) in enumerate(others):
                for a in range(n):
                    landed = _half(refs[a].at[2 * px + py], c, 0)
                    remote(sems, landed, 3 * a + j, (px, py, c)).wait_recv()
                    remote(sems, landed, 3 * n + 3 * a + j, (x, y, 1 - c)).start()

        def finish():
            x, y, c, others = _place()
            for j, (px, py) in enumerate(others):
                for a in range(n):
                    remote(sems, _half(refs[a].at[2 * px + py], 1 - c, 0), 3 * n + 3 * a + j, (x, y, 1 - c)).wait_recv()
            for a in range(n):
                mine = _half(refs[a].at[2 * x + y], c, 0)
                for k in range(3 * a, 3 * a + 3):
                    remote(sems, mine, k, (x, y, 1 - c)).wait_send()
                    remote(sems, mine, 3 * n + k, (x, y, 1 - c)).wait_send()

        _when(step == (3 * steps) // 4, pass_on)
        _when(step == steps - 1, finish)

    return _Exchange(bufs, [(_sds(b), a) for a, b in enumerate(bufs)], [6 * n, 6 * n], before, after)


def _swap_exchange(grads):
    n = len(grads)

    def copy(refs, outs, sems, a):
        x, y, c, _ = _place()
        return pltpu.make_async_remote_copy(
            src_ref=_half(refs[a], 1 - c, 1), dst_ref=outs[a], send_sem=sems[0].at[a], recv_sem=sems[1].at[a],
            device_id=(x, y, 1 - c), device_id_type=MESH,
        )

    def before(step, steps, refs, outs, sems):
        _when(step == 0, lambda: [copy(refs, outs, sems, a).start() for a in range(n)] and None)

    def after(step, steps, refs, outs, sems):
        _when(step == steps - 1, lambda: [copy(refs, outs, sems, a).wait() for a in range(n)] and None)

    outputs = [(jax.ShapeDtypeStruct((N_CHIPS, g.shape[1] // 2, g.shape[2]), g.dtype), None) for g in grads]
    return _Exchange(grads, outputs, [n, n], before, after)


def _add_sibling_half(place, grads, received, wire, *, name):
    n = len(grads)

    def body(place_ref, *refs):
        k = pl.program_id(0)
        for a in range(n):
            pair = (refs[a][...] + refs[n + a][...]).astype(wire[a])
            refs[2 * n + a][...] = pair

            @pl.when(k == place_ref[1])
            def _(a=a, pair=pair):
                refs[3 * n + a][...] = pair

    def block(g):
        return (1, g.shape[1] // 2, g.shape[2])

    return pl.pallas_call(
        body,
        name=name,
        grid_spec=pltpu.PrefetchScalarGridSpec(
            num_scalar_prefetch=1,
            grid=(N_CHIPS,),
            in_specs=[pl.BlockSpec(block(g), lambda k, place_ref: (k, place_ref[0], 0)) for g in grads]
            + [pl.BlockSpec(block(g), lambda k, place_ref: (k, 0, 0)) for g in grads],
            out_specs=[pl.BlockSpec(block(g), lambda k, place_ref: (k, 0, 0)) for g in grads]
            + [pl.BlockSpec(block(g), lambda k, place_ref: (place_ref[1], 0, 0)) for g in grads],
        ),
        out_shape=[jax.ShapeDtypeStruct(r.shape, dt) for r, dt in zip(received, wire)] * 2,
        compiler_params=pltpu.CompilerParams(dimension_semantics=("arbitrary",), vmem_limit_bytes=VMEM_LIMIT),
    )(place, *grads, *received)


def _scatter_exchange(pairs, landing):
    n = len(pairs)

    def copy(refs, sems, a, j, px, py):
        x, y, c, _ = _place()
        return pltpu.make_async_remote_copy(
            src_ref=refs[a].at[2 * px + py], dst_ref=refs[n + a].at[2 * x + y], send_sem=sems[0].at[3 * a + j],
            recv_sem=sems[1].at[3 * a + j], device_id=(px, py, c), device_id_type=MESH,
        )

    def before(step, steps, refs, outs, sems):
        def send():
            for j, (px, py) in enumerate(_place()[3]):
                for a in range(n):
                    copy(refs, sems, a, j, px, py).start()

        _when(step == 0, send)

    def after(step, steps, refs, outs, sems):
        def finish():
            x, y, c, others = _place()
            for j, (px, py) in enumerate(others):
                for a in range(n):
                    landed = refs[n + a].at[2 * px + py]
                    pltpu.make_async_remote_copy(
                        src_ref=landed, dst_ref=landed, send_sem=sems[0].at[3 * a + j], recv_sem=sems[1].at[3 * a + j],
                        device_id=(px, py, c), device_id_type=MESH,
                    ).wait_recv()
            for j, (px, py) in enumerate(others):
                for a in range(n):
                    copy(refs, sems, a, j, px, py).wait_send()

        _when(step == steps - 1, finish)

    return _Exchange(pairs + landing, [(_sds(b), n + a) for a, b in enumerate(landing)], [3 * n, 3 * n], before, after)


SUM_STEPS = 2


def _sum_chip_blocks(place, parts, keep_chip_axis, *, name):
    n = len(parts)

    def body(place_ref, *refs):
        for a in range(n):
            p = refs[a]
            total = (p[0].astype(F32) + p[1].astype(F32)) + (p[2].astype(F32) + p[3].astype(F32))
            if keep_chip_axis[a]:
                refs[n + a][0] = total
            else:
                refs[n + a][...] = total

    def in_spec(p):
        return pl.BlockSpec((N_CHIPS, p.shape[1] // SUM_STEPS, p.shape[2]), lambda i, place_ref: (0, i, 0))

    def out_spec(p, keep):
        rows = p.shape[1] // SUM_STEPS
        if keep:
            return pl.BlockSpec((1, rows, p.shape[2]), lambda i, place_ref: (place_ref[1], place_ref[0] * SUM_STEPS + i, 0))
        return pl.BlockSpec((rows, p.shape[2]), lambda i, place_ref: (place_ref[0] * SUM_STEPS + i, 0))

    def out_shape(p, keep):
        shape = (2 * p.shape[1], p.shape[2])
        return jax.ShapeDtypeStruct((N_CHIPS,) + shape if keep else shape, F32)

    return pl.pallas_call(
        body,
        name=name,
        grid_spec=pltpu.PrefetchScalarGridSpec(
            num_scalar_prefetch=1,
            grid=(SUM_STEPS,),
            in_specs=[in_spec(p) for p in parts],
            out_specs=[out_spec(p, k) for p, k in zip(parts, keep_chip_axis)],
        ),
        out_shape=[out_shape(p, k) for p, k in zip(parts, keep_chip_axis)],
        compiler_params=pltpu.CompilerParams(dimension_semantics=("arbitrary",), vmem_limit_bytes=VMEM_LIMIT),
    )(place, *parts)


def _join_exchange(bufs, keep_chip_axis):
    n = len(bufs)
    kept = [a for a in range(n) if keep_chip_axis[a]]
    base = n

    def copy(refs, sems, block, k, to):
        return pltpu.make_async_remote_copy(
            src_ref=block, dst_ref=block, send_sem=sems[0].at[k], recv_sem=sems[1].at[k], device_id=to, device_id_type=MESH
        )

    def mine(refs, a, cc):
        x, y, _, _ = _place()
        return _half(refs[a].at[2 * x + y] if keep_chip_axis[a] else refs[a], cc, 0)

    def before(step, steps, refs, outs, sems):
        def send():
            x, y, c, others = _place()
            for a in range(n):
                copy(refs, sems, mine(refs, a, c), a, (x, y, 1 - c)).start()
            for i, a in enumerate(kept):
                for j, (px, py) in enumerate(others):
                    copy(refs, sems, mine(refs, a, c), base + 6 * i + j, (px, py, c)).start()

        _when(step == 0, send)

    def after(step, steps, refs, outs, sems):
        def pass_on():
            x, y, c, others = _place()
            for i, a in enumerate(kept):
                for j, (px, py) in enumerate(others):
                    landed = _half(refs[a].at[2 * px + py], c, 0)
                    copy(refs, sems, landed, base + 6 * i + j, (px, py, c)).wait_recv()
                    copy(refs, sems, landed, base + 6 * i + 3 + j, (x, y, 1 - c)).start()

        def finish():
            x, y, c, others = _place()
            for a in range(n):
                copy(refs, sems, mine(refs, a, 1 - c), a, (x, y, 1 - c)).wait_recv()
            for i, a in enumerate(kept):
                for j, (px, py) in enumerate(others):
                    passed = _half(refs[a].at[2 * px + py], 1 - c, 0)
                    copy(refs, sems, passed, base + 6 * i + 3 + j, (x, y, 1 - c)).wait_recv()
            for a in range(n):
                copy(refs, sems, mine(refs, a, c), a, (x, y, 1 - c)).wait_send()
            for i, a in enumerate(kept):
                for k in range(6):
                    copy(refs, sems, mine(refs, a, c), base + 6 * i + k, (x, y, 1 - c)).wait_send()

        _when(step == steps // 2, pass_on)
        _when(step == steps - 1, finish)

    return _Exchange(bufs, [(_sds(b), a) for a, b in enumerate(bufs)], [n + 6 * len(kept)] * 2, before, after)


def _adamw(w, g, m, v):
    m = ADAM_B1 * m + (1.0 - ADAM_B1) * g
    v = ADAM_B2 * v + (1.0 - ADAM_B2) * (g * g)
    m_hat = m / (1.0 - ADAM_B1**ADAM_STEP)
    v_hat = v / (1.0 - ADAM_B2**ADAM_STEP)
    delta = -ADAM_LR * (m_hat / (jnp.sqrt(v_hat) + ADAM_EPS) + ADAM_WD * w)
    return delta, m, v


def _adamw_large(w, m, v, layer_grads, *, name):
    depth, rows, cols = w.shape
    tile = math.gcd(rows, ADAM_TILE)
    assert tile % 8 == 0

    def body(w_ref, m_ref, v_ref, *refs):
        g_refs, (g_out, d_out, m_out, v_out) = refs[:depth], refs[depth:]
        for l in range(depth):

            @pl.when(pl.program_id(0) == l)
            def _(l=l):
                g = g_refs[l][...]
                delta, m_new, v_new = _adamw(w_ref[0], g, m_ref[0], v_ref[0])
                g_out[0], d_out[0], m_out[0], v_out[0] = g, delta, m_new, v_new

    def stacked():
        return pl.BlockSpec((1, tile, cols), lambda l, i: (l, i, 0))

    def layer_spec(l):
        return pl.BlockSpec((tile, cols), lambda k, i: (jnp.where(k == l, i, 0), 0))

    shape = jax.ShapeDtypeStruct(w.shape, F32)
    return pl.pallas_call(
        body,
        name=name,
        grid=(depth, rows // tile),
        in_specs=[stacked(), stacked(), stacked()] + [layer_spec(l) for l in range(depth)],
        out_specs=[stacked()] * 4,
        out_shape=[shape] * 4,
        compiler_params=pltpu.CompilerParams(dimension_semantics=("arbitrary", "arbitrary"), vmem_limit_bytes=VMEM_LIMIT),
    )(w, m, v, *layer_grads)


def _packed_pieces(name, l):
    vecs = ("sg_ln_g", "sg_ln_b", "pool_scale", "cc_dw_b", "cc_ln_g", "cc_ln_b")
    if name in vecs:
        r = RG_VEC + vecs.index(name)
        return [((slice(l, l + 1), slice(None)), slice(r, r + 1), slice(None))]
    if name in ("ln_g", "ln_b"):
        r = RG_LN + (4 if name == "ln_b" else 0)
        return [((slice(l, l + 1), slice(j * D_G, (j + 1) * D_G)), slice(r + j, r + j + 1), slice(None)) for j in range(4)]
    assert name == "sg_w"
    return [
        ((l, h), slice(RG_SGW + CHUNK * (h // 2), RG_SGW + CHUNK * (h // 2 + 1)), slice(CHUNK * (h % 2), CHUNK * (h % 2 + 1)))
        for h in range(N_SUB)
    ]


PACKED_NAMES = ("sg_ln_g", "sg_ln_b", "pool_scale", "cc_dw_b", "cc_ln_g", "cc_ln_b", "ln_g", "ln_b", "sg_w")


def _adamw_packed(packed, ws, ms, vs, *, name):
    n, depth = len(ws), len(packed)

    def body(*refs):
        packed_refs, refs = refs[:depth], refs[depth:]
        for a, leaf in enumerate(PACKED_NAMES):
            for l in range(depth):
                for at, rows, cols in _packed_pieces(leaf, l):
                    g = packed_refs[l][rows, cols]
                    delta, m_new, v_new = _adamw(refs[a][at], g, refs[n + a][at], refs[2 * n + a][at])
                    refs[3 * n + a][at] = g
                    refs[4 * n + a][at] = delta
                    refs[5 * n + a][at] = m_new
                    refs[6 * n + a][at] = v_new

    shapes = [jax.ShapeDtypeStruct(w.shape, F32) for w in ws]
    outs = pl.pallas_call(body, name=name, out_shape=shapes * 4)(*packed, *ws, *ms, *vs)
    return outs[:n], outs[n : 2 * n], outs[2 * n : 3 * n], outs[3 * n :]


def _adamw_small(ws, gs, ms, vs, *, name):
    n = len(ws)

    def body(*refs):
        for a in range(n):
            delta, m_new, v_new = _adamw(refs[a][...], refs[n + a][...], refs[2 * n + a][...], refs[3 * n + a][...])
            refs[4 * n + a][...] = delta
            refs[5 * n + a][...] = m_new
            refs[6 * n + a][...] = v_new

    shapes = [jax.ShapeDtypeStruct(w.shape, F32) for w in ws]
    outs = pl.pallas_call(body, name=name, out_shape=shapes * 3)(*ws, *gs, *ms, *vs)
    return outs[:n], outs[n : 2 * n], outs[2 * n :]


WEIGHT_NAMES = (
    "w_in", "conv_a_w", "sg_ln_g", "sg_ln_b", "sg_w", "sg_b", "pool_w", "pool_scale", "cc_dw_w", "cc_dw_b", "cc_ln_g",
    "cc_ln_b", "cc_pw_w", "w_kv", "w_out", "ln_g", "ln_b",
)
LARGE = ("w_in", "cc_pw_w", "w_kv", "w_out")
TAPS_ROWS = 48


def _unpack_small_grads(small, chip):
    out = {}
    for r, k in enumerate(("sg_ln_g", "sg_ln_b", "pool_scale", "cc_dw_b", "cc_ln_g", "cc_ln_b")):
        out[k] = small[RG_VEC + r]
    out["conv_a_w"] = lax.dynamic_slice_in_dim(small[RG_CONVA : RG_CONVA + CONV_A], chip * HEAD_DIM, HEAD_DIM, axis=1)
    out["cc_dw_w"] = lax.dynamic_slice_in_dim(small[RG_DW : RG_DW + CONV_D], chip * HEAD_DIM, HEAD_DIM, axis=1)
    cat = jnp.concatenate([small[RG_SGW : RG_SGW + CHUNK], small[RG_SGW + CHUNK : RG_SGW + 2 * CHUNK]], axis=1)
    out["sg_w"] = jnp.transpose(cat.reshape(CHUNK, N_SUB, CHUNK), (1, 0, 2))
    out["sg_b"] = small[RG_SGB : RG_SGB + CHUNK].reshape(CHUNK, N_SUB, HEAD_DIM).sum(-1).T
    pool = small[RG_POOL : RG_POOL + D_G]
    out["pool_w"] = jnp.stack(
        [pool[g * HEAD_DIM : (g + 1) * HEAD_DIM, g * HEAD_DIM : (g + 1) * HEAD_DIM] for g in range(N_SUB)]
    )
    out["ln_g"] = small[RG_LN : RG_LN + 4].reshape(D_MODEL)
    out["ln_b"] = small[RG_LN + 4 : RG_LN + 8].reshape(D_MODEL)
    return out


def kernel(x, mem, w_in, conv_a_w, sg_ln_g, sg_ln_b, sg_w, sg_b, pool_w, pool_scale, cc_dw_w, cc_dw_b, cc_ln_g, cc_ln_b, cc_pw_w, w_kv, w_out, ln_g, ln_b, loss_target, m_w_in, m_conv_a_w, m_sg_ln_g, m_sg_ln_b, m_sg_w, m_sg_b, m_pool_w, m_pool_scale, m_cc_dw_w, m_cc_dw_b, m_cc_ln_g, m_cc_ln_b, m_cc_pw_w, m_w_kv, m_w_out, m_ln_g, m_ln_b, v_w_in, v_conv_a_w, v_sg_ln_g, v_sg_ln_b, v_sg_w, v_sg_b, v_pool_w, v_pool_scale, v_cc_dw_w, v_cc_dw_b, v_cc_ln_g, v_cc_ln_b, v_cc_pw_w, v_w_kv, v_w_out, v_ln_g, v_ln_b):
    given = dict(locals())
    weights = {k: given[k] for k in WEIGHT_NAMES}
    chip = 2 * lax.axis_index("x") + lax.axis_index("y")
    place = jnp.stack([lax.axis_index("c"), chip]).astype(jnp.int32)

    x0, mem0 = x[0], mem[0]

    taps = jnp.concatenate([conv_a_w, cc_dw_w], axis=1)
    taps = jnp.pad(taps, ((0, 0), (0, TAPS_ROWS - taps.shape[1]), (0, 0)))

    def own_blocks(l):
        return _place_own_block(
            place, [w_in, w_out, w_kv, cc_pw_w, taps], l, [BF16, BF16, BF16, BF16, F32], name=f"place_weights{l}"
        )

    def layer_operands(l, gathered):
        g_in, g_out, g_kv, g_pw, g_taps = gathered
        taps_full = jnp.transpose(g_taps, (1, 0, 2)).reshape(TAPS_ROWS, D_G)
        full = dict(
            w_in=g_in,
            w_out=g_out.reshape(D_MIX, D_MODEL),
            w_kv=g_kv.reshape(D_MODEL, 2 * D_G),
            cc_pw_w=g_pw.reshape(D_G, D_G),
            conv_a_w=taps_full[0:CONV_A],
            cc_dw_w=taps_full[CONV_A : CONV_A + CONV_D],
            **{k: weights[k][l] for k in WEIGHT_NAMES if k not in LARGE + ("conv_a_w", "cc_dw_w")},
        )
        return _prepare_layer(mem0, full, l)

    def layer_grads(l, x_in, bwd, small, exchange=None):
        dproj, _, dwout, dkst, dvst, dpw, _ = bwd
        (dwin,), carried = _input_weight_grad(x_in, dproj, name=f"w_in_grad{l}", exchange=exchange)
        grads = [
            dwin,
            dwout.reshape(N_CHIPS, D_MIX // N_CHIPS, D_MODEL),
            _kv_backward(mem0, dkst, dvst, name=f"kv_bwd{l}").reshape(N_CHIPS, D_MODEL // N_CHIPS, 2 * D_G),
            dpw.reshape(N_CHIPS, D_G // N_CHIPS, D_G),
            small.reshape(N_CHIPS, RG_ROWS // N_CHIPS, D_G),
        ]
        return grads, carried

    n_red = 5
    keep = [False, False, False, False, True]

    wire = [BF16, BF16, BF16, BF16, F32]

    def reduced_layer(joined):
        r_in, r_out, r_kv, r_pw, small_all = joined
        packed = small_all.reshape(RG_ROWS, D_G)
        out = _unpack_small_grads(packed, chip)
        out.update(w_in=r_in, w_out=r_out, w_kv=r_kv, cc_pw_w=r_pw, packed=packed)
        return out

    blocks0, blocks1 = own_blocks(0), own_blocks(1)
    p0 = layer_operands(0, _run_exchange(_gather_exchange(blocks0), name="gather_weights0"))
    fwd0, gathered1 = _forward(0, x0, p0, None, exchange=_gather_exchange(blocks1))
    p1 = layer_operands(1, gathered1)
    x1 = fwd0[3]
    fwd1, _ = _forward(1, x1, p1, loss_target[0])

    bwd1, _ = _backward(1, fwd1[3], fwd1, p1)
    small1 = bwd1[6].at[RG_LOSS, :].set(fwd1[5][0, 0])
    grads1, _ = layer_grads(1, x1, bwd1, small1)
    (dx1,), received1 = _input_grad(bwd1[0], bwd1[1], p1["win"], name="input_grad1", exchange=_swap_exchange(grads1))
    pairs1 = _add_sibling_half(place, grads1, received1, wire, name="rs_pair1")
    bwd0, parts1 = _backward(0, dx1, fwd0, p0, exchange=_scatter_exchange(pairs1[:n_red], pairs1[n_red:]))
    halves1 = _sum_chip_blocks(place, parts1, keep, name="rs_sum1")
    grads0, joined1 = layer_grads(0, x0, bwd0, bwd0[6], exchange=_join_exchange(halves1, keep))
    loss = joined1[4].reshape(RG_ROWS, D_G)[RG_LOSS, 0]
    received0 = _run_exchange(_swap_exchange(grads0), name="rs_swap0")
    pairs0 = _add_sibling_half(place, grads0, received0, wire, name="rs_pair0")
    (grad_x,), parts0 = _input_grad(
        bwd0[0], bwd0[1], p0["win"], name="input_grad0", exchange=_scatter_exchange(pairs0[:n_red], pairs0[n_red:])
    )
    halves0 = _sum_chip_blocks(place, parts0, keep, name="rs_sum0")
    reduced = [reduced_layer(_run_exchange(_join_exchange(halves0, keep), name="rs_join0")), reduced_layer(joined1)]

    grad, delta, new_m, new_v = {}, {}, {}, {}
    for k in LARGE:
        w3 = weights[k]
        grad[k], delta[k], new_m[k], new_v[k] = _adamw_large(
            w3, given["m_" + k], given["v_" + k], [reduced[l][k] for l in range(DEPTH)], name=f"adamw_{k}"
        )
    g_p, d_p, m_p, v_p = _adamw_packed(
        [reduced[l]["packed"] for l in range(DEPTH)],
        [weights[k] for k in PACKED_NAMES],
        [given["m_" + k] for k in PACKED_NAMES],
        [given["v_" + k] for k in PACKED_NAMES],
        name="adamw_packed",
    )
    for a, k in enumerate(PACKED_NAMES):
        grad[k], delta[k], new_m[k], new_v[k] = g_p[a], d_p[a], m_p[a], v_p[a]
    small_names = [k for k in WEIGHT_NAMES if k not in LARGE + PACKED_NAMES]
    for k in small_names:
        grad[k] = jnp.stack([reduced[l][k] for l in range(DEPTH)])
    d_s, m_s, v_s = _adamw_small(
        [weights[k] for k in small_names],
        [grad[k] for k in small_names],
        [given["m_" + k] for k in small_names],
        [given["v_" + k] for k in small_names],
        name="adamw_small",
    )
    for a, k in enumerate(small_names):
        delta[k], new_m[k], new_v[k] = d_s[a], m_s[a], v_s[a]

    return (
        loss,
        grad_x[None],
        *[grad[k] for k in WEIGHT_NAMES],
        *[delta[k] for k in WEIGHT_NAMES],
        *[new_m[k] for k in WEIGHT_NAMES],
        *[new_v[k] for k in WEIGHT_NAMES],
    )
```

```python
import functools
import math

import jax
import jax.numpy as jnp
from jax import lax
from jax.experimental import pallas as pl
from jax.experimental.pallas import tpu as pltpu

F32 = jnp.float32
BF16 = jnp.bfloat16

D_MODEL = 1024
DEPTH = 2
D_G = 256
D_MIX = 5 * D_G
D_IN = 9 * D_G + D_MIX
N_SUB = 4
HEAD_DIM = 64
CONV_A = 3
CONV_D = 31
CHUNK = 128
MEM_LEN = 256
N_CHIPS = 4
W_IN_SHARD = D_IN // N_CHIPS
LN_EPS = 1e-5
ALPHA = (2.0 * DEPTH) ** 0.25
ATT_SCALE = 1.0 / math.sqrt(HEAD_DIM)
GELU_C = math.sqrt(2.0 / math.pi)
GELU_A = 0.044715

ADAM_LR = 0.001
ADAM_B1 = 0.9
ADAM_B2 = 0.999
ADAM_EPS = 1e-08
ADAM_WD = 0.01
ADAM_STEP = 10

C_XA, C_BA, C_CA, C_U, C_V, C_XC, C_DA, C_DG, C_Q, C_GATE = (D_G * i for i in range(10))

HALO_A = 8
HALO_C = 16
HALO_D = 32

RW_VEC = 0
RW_CONVA = 16
RW_DW = 24
RW_SGB = 56
RW_ROWS = RW_SGB + CHUNK

RG_VEC = 0
RG_CONVA = 16
RG_DW = 24
RG_SGW = 56
RG_SGB = RG_SGW + 2 * CHUNK
RG_POOL = RG_SGB + CHUNK
RG_LN = RG_POOL + D_G
RG_LOSS = 8
RG_ROWS = 768

VMEM_LIMIT = 62 * 1024 * 1024

AUX_CVD = 0
AUX_PM = D_G
AUX_P = 2 * D_G
AUX_COLS = AUX_P + N_SUB * MEM_LEN
SEQ_TILE = 256
FWD_TILE = 512
MM_TILE = 1024
ADAM_TILE = 512

MESH = pl.DeviceIdType.MESH
ANY = pl.BlockSpec(memory_space=pl.ANY)
NT = (((1,), (1,)), ((), ()))
TN = (((0,), (0,)), ((), ()))


def _dot(a, b):
    return jnp.dot(a, b, preferred_element_type=F32)


def _dot_nt(a, b):
    return lax.dot_general(a, b, NT, preferred_element_type=F32)


def _dot_tn(a, b):
    return lax.dot_general(a, b, TN, preferred_element_type=F32)


def _full(shape):
    zeros = (0,) * len(shape)
    return pl.BlockSpec(shape, lambda *_: zeros)


class _Exchange:
    def __init__(self, operands, outputs, sem_counts, before, after):
        self.operands, self.outputs, self.sem_counts, self.before, self.after = operands, outputs, sem_counts, before, after

    def specs(self, first_input, first_output):
        aliases = {first_input + src: first_output + j for j, (_, src) in enumerate(self.outputs) if src is not None}
        return (
            [ANY] * len(self.operands),
            [ANY] * len(self.outputs),
            [sds for sds, _ in self.outputs],
            [pltpu.SemaphoreType.DMA((k,)) for k in self.sem_counts],
            aliases,
        )

    def split(self, ins, outs):
        refs = list(ins)
        for j, (_, src) in enumerate(self.outputs):
            if src is not None:
                refs[src] = outs[j]
        return refs


def _when(cond, fn):
    if isinstance(cond, bool):
        if cond:
            fn()
    else:
        pl.when(cond)(fn)


def _run_exchange(exchange, *, name):
    n_in, n_out = len(exchange.operands), len(exchange.outputs)
    in_specs, out_specs, out_shape, sems, aliases = exchange.specs(0, 0)

    def body(*refs):
        ins, outs, sem_refs = refs[:n_in], refs[n_in : n_in + n_out], refs[n_in + n_out :]
        refs = exchange.split(ins, outs)
        exchange.before(0, 1, refs, outs, sem_refs)
        exchange.after(0, 1, refs, outs, sem_refs)

    return pl.pallas_call(
        body, name=name, in_specs=in_specs, out_specs=out_specs, out_shape=out_shape, scratch_shapes=sems,
        input_output_aliases=aliases,
    )(*exchange.operands)


def _gridded_call(body, *, name, steps, in_specs, out_specs, out_shape, scratch_shapes, operands, exchange=None):
    params = pltpu.CompilerParams(dimension_semantics=("arbitrary",), vmem_limit_bytes=VMEM_LIMIT)
    if exchange is None:
        outs = pl.pallas_call(
            body, name=name, grid=(steps,), in_specs=in_specs, out_specs=out_specs, out_shape=out_shape,
            scratch_shapes=scratch_shapes, compiler_params=params,
        )(*operands)
        return list(outs), []
    n_in, n_out, n_scr = len(in_specs), len(out_specs), len(scratch_shapes)
    x_in, x_out = len(exchange.operands), len(exchange.outputs)
    ex_in_specs, ex_out_specs, ex_out_shape, ex_sems, aliases = exchange.specs(n_in, n_out)

    def full(*refs):
        own_in, refs = refs[:n_in], refs[n_in:]
        ex_in, refs = refs[:x_in], refs[x_in:]
        own_out, refs = refs[:n_out], refs[n_out:]
        ex_out, refs = refs[:x_out], refs[x_out:]
        own_scr, sem_refs = refs[:n_scr], refs[n_scr:]
        ex_refs = exchange.split(ex_in, ex_out)
        step = pl.program_id(0)
        exchange.before(step, steps, ex_refs, ex_out, sem_refs)
        body(*own_in, *own_out, *own_scr)
        exchange.after(step, steps, ex_refs, ex_out, sem_refs)

    outs = pl.pallas_call(
        full, name=name, grid=(steps,), in_specs=in_specs + ex_in_specs, out_specs=out_specs + ex_out_specs,
        out_shape=out_shape + ex_out_shape, scratch_shapes=scratch_shapes + ex_sems, input_output_aliases=aliases,
        compiler_params=params,
    )(*operands, *exchange.operands)
    return list(outs[:n_out]), list(outs[n_out:])


def _sigmoid(x):
    return 0.5 * jnp.tanh(0.5 * x) + 0.5


def _gelu(x):
    t = jnp.tanh(GELU_C * (x + GELU_A * x * x * x))
    return 0.5 * x * (1.0 + t), t


def _gelu_grad(x, t):
    return 0.5 * (1.0 + t) + 0.5 * x * (1.0 - t * t) * (GELU_C * (1.0 + 3.0 * GELU_A * x * x))


def _normalize(v):
    mu = jnp.mean(v, axis=-1, keepdims=True)
    d = v - mu
    var = jnp.mean(d * d, axis=-1, keepdims=True)
    rstd = lax.rsqrt(var + LN_EPS)
    return d * rstd, rstd


def _normalize_grad(dhat, hat, rstd):
    m1 = jnp.mean(dhat, axis=-1, keepdims=True)
    m2 = jnp.mean(dhat * hat, axis=-1, keepdims=True)
    return rstd * (dhat - m1 - hat * m2)


def _lane(width=D_G):
    return lax.broadcasted_iota(jnp.int32, (1, width), 1)


def _head_masks():
    head = _lane() // HEAD_DIM
    return [(head == h).astype(F32) for h in range(N_SUB)]


def _stack_heads(v, masks):
    return jnp.concatenate([v * m for m in masks], axis=0)


def _tril_mask_cat():
    t = lax.broadcasted_iota(jnp.int32, (CHUNK, N_SUB * CHUNK), 0)
    s = lax.broadcasted_iota(jnp.int32, (CHUNK, N_SUB * CHUNK), 1) % CHUNK
    return s <= t


def _triu_mask_cat():
    s = lax.broadcasted_iota(jnp.int32, (CHUNK, N_SUB * CHUNK), 0)
    t = lax.broadcasted_iota(jnp.int32, (CHUNK, N_SUB * CHUNK), 1) % CHUNK
    return t >= s


def _pool_select(a2, a4, a8, a16):
    lane = _lane()
    return jnp.where(lane < 64, a2, jnp.where(lane < 128, a4, jnp.where(lane < 192, a8, a16)))


def _pool_inv_count(row0, rows):
    t = row0 + lax.broadcasted_iota(jnp.int32, (HALO_C, D_G), 0)
    lane = lax.broadcasted_iota(jnp.int32, (HALO_C, D_G), 1)
    win = jnp.where(lane < 64, 2, jnp.where(lane < 128, 4, jnp.where(lane < 192, 8, 16)))
    head = 1.0 / jnp.minimum(t + 1, win).astype(F32)
    inv_win = jnp.broadcast_to(_pool_select(0.5, 0.25, 0.125, 0.0625), (rows - HALO_C, D_G))
    return jnp.concatenate([head, inv_win], axis=0)


def _trailing_window_sum(halo, cur):
    e = jnp.concatenate([halo, cur], axis=0)
    s2 = e + pltpu.roll(e, 1, 0)
    s4 = s2 + pltpu.roll(s2, 2, 0)
    s8 = s4 + pltpu.roll(s4, 4, 0)
    s16 = s8 + pltpu.roll(s8, 8, 0)
    return _pool_select(s2, s4, s8, s16)[HALO_C:]


def _leading_window_sum(cur, halo):
    e = jnp.concatenate([cur, halo], axis=0)
    n = e.shape[0]
    s2 = e + pltpu.roll(e, n - 1, 0)
    s4 = s2 + pltpu.roll(s2, n - 2, 0)
    s8 = s4 + pltpu.roll(s4, n - 4, 0)
    s16 = s8 + pltpu.roll(s8, n - 8, 0)
    return _pool_select(s2, s4, s8, s16)[: cur.shape[0]]


def _softmax_blocks(sc):
    out = []
    for h in range(N_SUB):
        s = sc[:, h * MEM_LEN : (h + 1) * MEM_LEN]
        e = jnp.exp(s - jnp.max(s, axis=-1, keepdims=True))
        out.append(e * (1.0 / jnp.sum(e, axis=-1, keepdims=True)))
    return jnp.concatenate(out, axis=-1)


STRIP = 32
SHIFTS = 8


def _fill_shifts(buf):
    n = buf.shape[1] - SHIFTS
    for r in range(1, SHIFTS):
        buf[r, 0:n, :] = buf[0, r : r + n, :]


def _shifted(buf, off, rows):
    r = off % SHIFTS
    return buf[r, off - r : off - r + rows, :]


def _sgu_mix(vn, wcat_b, sgb, masks):
    vbd = _stack_heads(vn, masks).astype(BF16)
    return _dot(wcat_b, vbd) + sgb, vbd


def _layer_forward(x, win, kst, vst, wout, sw, wcat, poolw, pww, ln, tgt, *, name, exchange=None, projected=False):
    seq = x.shape[0]
    tile = min(FWD_TILE, seq)
    n_tiles = seq // tile
    last = tgt is not None

    def body(*refs):
        x_ref, win_ref, kst_ref, vst_ref, wout_ref, sw_ref, wcat_ref, pool_ref, pw_ref, ln_ref = refs[:10]
        refs = refs[10:]
        if last:
            tgt_ref, refs = refs[0], refs[1:]
        if not projected:
            proj_ref, refs = refs[0], refs[1:]
        y_ref, z_ref, out_ref, aux_ref = refs[:4]
        refs = refs[4:]
        if last:
            loss_ref, refs = refs[0], refs[1:]
        pbuf, xchalo, gbuf = refs
        i = pl.program_id(0)

        @pl.when(i == 0)
        def _():
            pbuf[0:HALO_A, :] = jnp.zeros((HALO_A, D_G), F32)
            xchalo[...] = jnp.zeros((HALO_C, D_G), F32)
            gbuf[0, 0:HALO_D, :] = jnp.zeros((HALO_D, D_G), F32)
            if last:
                loss_ref[...] = jnp.zeros((8, 128), F32)

        xt = x_ref[...]
        blocks = {}

        if projected:

            def project(k):
                blocks[k] = win_ref[:, k * W_IN_SHARD : (k + 1) * W_IN_SHARD]

        else:
            xb = xt.astype(BF16)

            def project(k):
                blocks[k] = _dot(xb, win_ref[k])
                proj_ref[:, k * W_IN_SHARD : (k + 1) * W_IN_SHARD] = blocks[k]

        def cols(start, width=D_G):
            parts, c = [], start
            while c < start + width:
                k, lo = divmod(c, W_IN_SHARD)
                hi = min(W_IN_SHARD, lo + start + width - c)
                parts.append(blocks[k][:, lo:hi])
                c += hi - lo
            return parts[0] if len(parts) == 1 else jnp.concatenate(parts, axis=1)

        project(0)
        project(1)
        masks = _head_masks()

        pbuf[HALO_A : HALO_A + tile, :] = cols(C_CA) * cols(C_XA)
        cv = jnp.zeros((tile, D_G), F32)
        for k in range(CONV_A):
            off = HALO_A - (CONV_A - 1) + k
            cv = cv + sw_ref[RW_CONVA + k : RW_CONVA + k + 1, :] * pbuf[off : off + tile, :]
        y_ref[:, 0:D_G] = cols(C_BA) * cv
        pbuf[0:HALO_A, :] = pbuf[tile : tile + HALO_A, :]

        project(2)

        ua, _ = _gelu(cols(C_U))
        vg, _ = _gelu(cols(C_V))
        vhat, _ = _normalize(vg)
        vn = vhat * sw_ref[RW_VEC : RW_VEC + 1, :] + sw_ref[RW_VEC + 1 : RW_VEC + 2, :]
        wcat_b = jnp.where(_tril_mask_cat(), wcat_ref[...], 0.0).astype(BF16)
        sgb = sw_ref[RW_SGB : RW_SGB + CHUNK, :]
        for j in range(tile // CHUNK):
            rows = slice(j * CHUNK, (j + 1) * CHUNK)
            mixed, _ = _sgu_mix(vn[rows], wcat_b, sgb, masks)
            y_ref[rows, D_G : 2 * D_G] = ua[rows] * mixed

        xc = cols(C_XC)
        wsum = _trailing_window_sum(xchalo[...], xc)
        pm = wsum * _pool_inv_count(i * tile, tile) - xc
        aux_ref[:, AUX_PM : AUX_PM + D_G] = pm
        y_ref[:, 2 * D_G : 3 * D_G] = _dot(pm.astype(BF16), pool_ref[...]) * sw_ref[RW_VEC + 2 : RW_VEC + 3, :]
        xchalo[...] = xc[tile - HALO_C :, :]

        project(3)

        gbuf[0, HALO_D : HALO_D + tile, :] = cols(C_DA) * _sigmoid(cols(C_DG))
        _fill_shifts(gbuf)
        for r0 in range(0, tile, STRIP):
            acc = jnp.zeros((STRIP, D_G), F32) + sw_ref[RW_VEC + 3 : RW_VEC + 4, :]
            for k in range(CONV_D):
                off = HALO_D - (CONV_D - 1) + k
                acc = acc + sw_ref[RW_DW + k : RW_DW + k + 1, :] * _shifted(gbuf, off + r0, STRIP)
            aux_ref[r0 : r0 + STRIP, AUX_CVD : AUX_CVD + D_G] = acc
        nhat, _ = _normalize(aux_ref[:, AUX_CVD : AUX_CVD + D_G])
        nrm = nhat * sw_ref[RW_VEC + 4 : RW_VEC + 5, :] + sw_ref[RW_VEC + 5 : RW_VEC + 6, :]
        y_ref[:, 3 * D_G : 4 * D_G] = _dot((nrm * _sigmoid(nrm)).astype(BF16), pw_ref[...])
        gbuf[0, 0:HALO_D, :] = gbuf[0, tile : tile + HALO_D, :]

        qb = cols(C_Q).astype(BF16)
        p_all = _softmax_blocks(_dot_nt(qb, kst_ref[...]) * ATT_SCALE)
        aux_ref[:, AUX_P:] = p_all
        y_ref[:, 4 * D_G : 5 * D_G] = _dot(p_all.astype(BF16), vst_ref[...])

        gate = cols(C_GATE, D_MIX)
        hid = y_ref[...] * (gate * _sigmoid(gate))
        z = ALPHA * xt + _dot(hid.astype(BF16), wout_ref[...])
        z_ref[...] = z
        zhat, _ = _normalize(z)
        xn = zhat * ln_ref[0:1, :] + ln_ref[1:2, :]
        if last:
            err = xn - tgt_ref[...]
            out_ref[...] = err * (1.0 / D_MODEL)
            loss_ref[...] += jnp.sum(err * err) * (0.5 / D_MODEL)
        else:
            out_ref[...] = xn

    def rows(width):
        return pl.BlockSpec((tile, width), lambda i: (i, 0))

    operands = [x, win, kst, vst, wout, sw, wcat, poolw, pww, ln]
    in_specs = [rows(D_MODEL)] + [_full(a.shape) for a in operands[1:]]
    widths = [D_IN, D_MIX, D_MODEL, D_MODEL, AUX_COLS]
    if projected:
        in_specs[1] = rows(D_IN)
        widths = widths[1:]
    out_shape = [jax.ShapeDtypeStruct((seq, w), F32) for w in widths]
    out_specs = [rows(w) for w in widths]
    if last:
        operands.append(tgt)
        in_specs.append(rows(D_MODEL))
        out_shape.append(jax.ShapeDtypeStruct((8, 128), F32))
        out_specs.append(_full((8, 128)))
    outs, carried = _gridded_call(
        body,
        name=name,
        steps=n_tiles,
        in_specs=in_specs,
        out_specs=out_specs,
        out_shape=out_shape,
        scratch_shapes=[
            pltpu.VMEM((HALO_A + tile, D_G), F32),
            pltpu.VMEM((HALO_C, D_G), F32),
            pltpu.VMEM((SHIFTS, HALO_D + tile, D_G), F32),
        ],
        operands=operands,
        exchange=exchange,
    )
    return ([win] + outs if projected else outs), carried


def _layer_backward(dxn, z, proj, y, cvd, kst, vst, wout, sw, wcat, wcat_t, poolw, pww, ln, *, name, exchange=None):
    seq = dxn.shape[0]
    tile = min(SEQ_TILE, seq)
    n_tiles = seq // tile
    halo_blocks = tile // HALO_D

    def body(
        dxn_ref, z_ref, proj_ref, halo_ref, y_ref, aux_ref, kst_ref, vst_ref, wout_ref, sw_ref, wcat_ref, wcat_t_ref,
        pool_ref, pw_ref, ln_ref, dproj_ref, dz_ref, dwout_ref, dkst_ref, dvst_ref, dpw_ref, sg_ref,
        pbuf, dcvbuf, rhalo, gbuf, dgbuf, dwacc,
    ):
        i = pl.program_id(0)
        ti = n_tiles - 1 - i

        @pl.when(i == 0)
        def _():
            dwout_ref[...] = jnp.zeros(dwout_ref.shape, F32)
            dkst_ref[...] = jnp.zeros(dkst_ref.shape, F32)
            dvst_ref[...] = jnp.zeros(dvst_ref.shape, F32)
            dpw_ref[...] = jnp.zeros(dpw_ref.shape, F32)
            sg_ref[...] = jnp.zeros(sg_ref.shape, F32)
            dwacc[...] = jnp.zeros(dwacc.shape, F32)
            dcvbuf[tile : tile + HALO_A, :] = jnp.zeros((HALO_A, D_G), F32)
            rhalo[...] = jnp.zeros((HALO_C, D_G), F32)
            dgbuf[0, tile : tile + HALO_D, :] = jnp.zeros((HALO_D, D_G), F32)

        def acc_row(row, val):
            sg_ref[row : row + 1, :] += jnp.sum(val, axis=0, keepdims=True)

        masks = _head_masks()
        has_past = (ti > 0).astype(F32)

        zhat, zrstd = _normalize(z_ref[...])
        dxn_t = dxn_ref[...]
        dlg = jnp.sum(dxn_t * zhat, axis=0, keepdims=True)
        dlb = jnp.sum(dxn_t, axis=0, keepdims=True)
        for j in range(D_MODEL // D_G):
            sg_ref[RG_LN + j : RG_LN + j + 1, :] += dlg[:, j * D_G : (j + 1) * D_G]
            sg_ref[RG_LN + 4 + j : RG_LN + 5 + j, :] += dlb[:, j * D_G : (j + 1) * D_G]
        dz = _normalize_grad(dxn_t * ln_ref[0:1, :], zhat, zrstd)
        dz_ref[...] = dz
        dzb = dz.astype(BF16)

        gate = proj_ref[:, C_GATE:]
        sgm = _sigmoid(gate)
        silu = gate * sgm
        yc = y_ref[...]
        dwout_ref[...] += _dot_tn((yc * silu).astype(BF16), dzb)
        dh = _dot_nt(dzb, wout_ref[...])
        dproj_ref[:, C_GATE:] = (dh * yc * (sgm * (1.0 + gate * (1.0 - sgm)))).astype(BF16)
        dy = dh * silu

        dya = dy[:, 0:D_G]
        xa = proj_ref[:, C_XA : C_XA + D_G]
        ba = proj_ref[:, C_BA : C_BA + D_G]
        ca = proj_ref[:, C_CA : C_CA + D_G]
        past = slice(HALO_D - HALO_A, HALO_D)
        pbuf[0:HALO_A, :] = halo_ref[past, C_CA : C_CA + D_G] * halo_ref[past, C_XA : C_XA + D_G] * has_past
        pbuf[HALO_A : HALO_A + tile, :] = ca * xa
        cv = jnp.zeros((tile, D_G), F32)
        for k in range(CONV_A):
            off = HALO_A - (CONV_A - 1) + k
            cv = cv + sw_ref[RW_CONVA + k : RW_CONVA + k + 1, :] * pbuf[off : off + tile, :]
        dproj_ref[:, C_BA : C_BA + D_G] = (dya * cv).astype(BF16)
        dcv = dya * ba
        dcvbuf[0:tile, :] = dcv
        dp = jnp.zeros((tile, D_G), F32)
        for k in range(CONV_A):
            off = HALO_A - (CONV_A - 1) + k
            acc_row(RG_CONVA + k, dcv * pbuf[off : off + tile, :])
            back = CONV_A - 1 - k
            dp = dp + sw_ref[RW_CONVA + k : RW_CONVA + k + 1, :] * dcvbuf[back : back + tile, :]
        dproj_ref[:, C_CA : C_CA + D_G] = (dp * xa).astype(BF16)
        dproj_ref[:, C_XA : C_XA + D_G] = (dp * ca).astype(BF16)
        dcvbuf[tile : tile + HALO_A, :] = dcvbuf[0:HALO_A, :]

        dyb = dy[:, D_G : 2 * D_G]
        u = proj_ref[:, C_U : C_U + D_G]
        v = proj_ref[:, C_V : C_V + D_G]
        ua, ut = _gelu(u)
        vg, vt = _gelu(v)
        vhat, vrstd = _normalize(vg)
        sg_g = sw_ref[RW_VEC : RW_VEC + 1, :]
        vn = vhat * sg_g + sw_ref[RW_VEC + 1 : RW_VEC + 2, :]
        tril = _tril_mask_cat()
        wcat_b = jnp.where(tril, wcat_ref[...], 0.0).astype(BF16)
        wcat_tb = jnp.where(_triu_mask_cat(), wcat_t_ref[...], 0.0).astype(BF16)
        sgb = sw_ref[RW_SGB : RW_SGB + CHUNK, :]
        dmixed = dyb * ua
        dvn_parts = []
        du_parts = []
        dwcat = jnp.zeros((CHUNK, N_SUB * CHUNK), F32)
        dsgb = jnp.zeros((CHUNK, D_G), F32)
        for j in range(tile // CHUNK):
            rows = slice(j * CHUNK, (j + 1) * CHUNK)
            mixed, vbd = _sgu_mix(vn[rows], wcat_b, sgb, masks)
            du_parts.append(dyb[rows] * mixed)
            dmx = dmixed[rows]
            dsgb = dsgb + dmx
            dwcat = dwcat + _dot_nt(dmx.astype(BF16), vbd)
            dvn_parts.append(_dot(wcat_tb, _stack_heads(dmx, masks).astype(BF16)))
        dwcat = jnp.where(tril, dwcat, 0.0)
        sg_ref[RG_SGW : RG_SGW + CHUNK, :] += dwcat[:, 0:D_G]
        sg_ref[RG_SGW + CHUNK : RG_SGW + 2 * CHUNK, :] += dwcat[:, D_G:]
        sg_ref[RG_SGB : RG_SGB + CHUNK, :] += dsgb
        dvn = jnp.concatenate(dvn_parts, axis=0)
        du_act = jnp.concatenate(du_parts, axis=0)
        acc_row(RG_VEC, dvn * vhat)
        acc_row(RG_VEC + 1, dvn)
        dvg = _normalize_grad(dvn * sg_g, vhat, vrstd)
        dproj_ref[:, C_U : C_U + D_G] = (du_act * _gelu_grad(u, ut)).astype(BF16)
        dproj_ref[:, C_V : C_V + D_G] = (dvg * _gelu_grad(v, vt)).astype(BF16)

        dyc = dy[:, 2 * D_G : 3 * D_G]
        inv_cnt = _pool_inv_count(ti * tile, tile)
        pmb = aux_ref[:, AUX_PM : AUX_PM + D_G].astype(BF16)
        pool_b = pool_ref[...]
        scale = sw_ref[RW_VEC + 2 : RW_VEC + 3, :]
        acc_row(RG_VEC + 2, dyc * _dot(pmb, pool_b))
        dpre = (dyc * scale).astype(BF16)
        sg_ref[RG_POOL : RG_POOL + D_G, :] += _dot_tn(pmb, dpre)
        dpm = _dot_nt(dpre, pool_b)
        r = dpm * inv_cnt
        dproj_ref[:, C_XC : C_XC + D_G] = (_leading_window_sum(r, rhalo[...]) - dpm).astype(BF16)
        rhalo[...] = r[0:HALO_C, :]

        dyd = dy[:, 3 * D_G : 4 * D_G]
        da = proj_ref[:, C_DA : C_DA + D_G]
        sgd = _sigmoid(proj_ref[:, C_DG : C_DG + D_G])
        gbuf[0, 0:HALO_D, :] = halo_ref[:, C_DA : C_DA + D_G] * _sigmoid(halo_ref[:, C_DG : C_DG + D_G]) * has_past
        gbuf[0, HALO_D : HALO_D + tile, :] = da * sgd
        _fill_shifts(gbuf)
        nhat, nrstd = _normalize(aux_ref[:, AUX_CVD : AUX_CVD + D_G])
        cc_g = sw_ref[RW_VEC + 4 : RW_VEC + 5, :]
        nrm = nhat * cc_g + sw_ref[RW_VEC + 5 : RW_VEC + 6, :]
        sgn = _sigmoid(nrm)
        dydb = dyd.astype(BF16)
        dpw_ref[...] += _dot_tn((nrm * sgn).astype(BF16), dydb)
        dn = _dot_nt(dydb, pw_ref[...]) * (sgn * (1.0 + nrm * (1.0 - sgn)))
        acc_row(RG_VEC + 4, dn * nhat)
        acc_row(RG_VEC + 5, dn)
        dcvd = _normalize_grad(dn * cc_g, nhat, nrstd)
        acc_row(RG_VEC + 3, dcvd)
        dgbuf[0, 0:tile, :] = dcvd
        _fill_shifts(dgbuf)
        for r0 in range(0, tile, STRIP):
            d_s = dgbuf[0, r0 : r0 + STRIP, :]
            dg = jnp.zeros((STRIP, D_G), F32)
            for k in range(CONV_D):
                off = HALO_D - (CONV_D - 1) + k
                prod = d_s * _shifted(gbuf, off + r0, STRIP)
                part = prod[0:8]
                for q in range(8, STRIP, 8):
                    part = part + prod[q : q + 8]
                dwacc[8 * k : 8 * k + 8, :] += part
                back = CONV_D - 1 - k
                dg = dg + sw_ref[RW_DW + k : RW_DW + k + 1, :] * _shifted(dgbuf, back + r0, STRIP)
            da_s = proj_ref[r0 : r0 + STRIP, C_DA : C_DA + D_G]
            sgd_s = _sigmoid(proj_ref[r0 : r0 + STRIP, C_DG : C_DG + D_G])
            dproj_ref[r0 : r0 + STRIP, C_DA : C_DA + D_G] = (dg * sgd_s).astype(BF16)
            dproj_ref[r0 : r0 + STRIP, C_DG : C_DG + D_G] = (dg * da_s * sgd_s * (1.0 - sgd_s)).astype(BF16)
        dgbuf[0, tile : tile + HALO_D, :] = dgbuf[0, 0:HALO_D, :]

        @pl.when(i == n_tiles - 1)
        def _():
            for k in range(CONV_D):
                sg_ref[RG_DW + k : RG_DW + k + 1, :] = jnp.sum(dwacc[8 * k : 8 * k + 8, :], axis=0, keepdims=True)

        dyeb = dy[:, 4 * D_G : 5 * D_G].astype(BF16)
        qb = proj_ref[:, C_Q : C_Q + D_G].astype(BF16)
        kst_b = kst_ref[...]
        p_all = aux_ref[:, AUX_P:]
        dvst_ref[...] += _dot_tn(p_all.astype(BF16), dyeb)
        dp_all = _dot_nt(dyeb, vst_ref[...])
        ds = []
        for h in range(N_SUB):
            blk = slice(h * MEM_LEN, (h + 1) * MEM_LEN)
            p, dpb = p_all[:, blk], dp_all[:, blk]
            ds.append(p * (dpb - jnp.sum(dpb * p, axis=-1, keepdims=True)))
        dsb = (jnp.concatenate(ds, axis=-1) * ATT_SCALE).astype(BF16)
        dproj_ref[:, C_Q : C_Q + D_G] = _dot(dsb, kst_b).astype(BF16)
        dkst_ref[...] += _dot_tn(dsb, qb)

    def rows(width):
        return pl.BlockSpec((tile, width), lambda i: (n_tiles - 1 - i, 0))

    halo_spec = pl.BlockSpec((HALO_D, D_IN), lambda i: (jnp.maximum((n_tiles - 1 - i) * halo_blocks - 1, 0), 0))
    weights = [kst, vst, wout, sw, wcat, wcat_t, poolw, pww, ln]
    acc_shapes = [(D_MIX, D_MODEL), (N_SUB * MEM_LEN, D_G), (N_SUB * MEM_LEN, D_G), (D_G, D_G), (RG_ROWS, D_G)]
    return _gridded_call(
        body,
        name=name,
        steps=n_tiles,
        in_specs=[rows(D_MODEL), rows(D_MODEL), rows(D_IN), halo_spec, rows(D_MIX), rows(AUX_COLS)]
        + [_full(a.shape) for a in weights],
        out_specs=[rows(D_IN), rows(D_MODEL)] + [_full(s) for s in acc_shapes],
        out_shape=[jax.ShapeDtypeStruct((seq, D_IN), BF16), jax.ShapeDtypeStruct((seq, D_MODEL), F32)]
        + [jax.ShapeDtypeStruct(s, F32) for s in acc_shapes],
        scratch_shapes=[
            pltpu.VMEM((HALO_A + tile, D_G), F32),
            pltpu.VMEM((tile + HALO_A, D_G), F32),
            pltpu.VMEM((HALO_C, D_G), F32),
            pltpu.VMEM((SHIFTS, HALO_D + tile, D_G), F32),
            pltpu.VMEM((SHIFTS, tile + HALO_D, D_G), F32),
            pltpu.VMEM((8 * CONV_D, D_G), F32),
        ],
        operands=[dxn, z, proj, proj, y, cvd, *weights],
        exchange=exchange,
    )


def _kv_forward(mem, wkv, *, name):
    def body(mem_ref, wkv_ref, kst_ref, vst_ref):
        kv = _dot(mem_ref[...].astype(BF16), wkv_ref[...])
        masks = _head_masks()
        kst_ref[...] = _stack_heads(kv[:, 0:D_G], masks).astype(BF16)
        vst_ref[...] = _stack_heads(kv[:, D_G:], masks).astype(BF16)

    shape = jax.ShapeDtypeStruct((N_SUB * MEM_LEN, D_G), BF16)
    return pl.pallas_call(body, name=name, out_shape=[shape, shape])(mem, wkv)


def _kv_backward(mem, dkst, dvst, *, name):
    def body(mem_ref, dkst_ref, dvst_ref, dwkv_ref):
        masks = _head_masks()
        memb = mem_ref[...].astype(BF16)
        for col, ref in ((0, dkst_ref), (D_G, dvst_ref)):
            d = jnp.zeros((MEM_LEN, D_G), F32)
            for h in range(N_SUB):
                d = d + ref[h * MEM_LEN : (h + 1) * MEM_LEN, :] * masks[h]
            dwkv_ref[:, col : col + D_G] = _dot_tn(memb, d.astype(BF16))

    return pl.pallas_call(body, name=name, out_shape=jax.ShapeDtypeStruct((D_MODEL, 2 * D_G), F32))(mem, dkst, dvst)


def _input_grad(dproj, dz, win, *, name, exchange=None):
    seq = dproj.shape[0]
    tile = min(MM_TILE // 2, seq)

    def body(dproj_ref, dz_ref, win_ref, dx_ref):
        acc = ALPHA * dz_ref[...]
        for k in range(N_CHIPS):
            acc = acc + _dot_nt(dproj_ref[:, k * W_IN_SHARD : (k + 1) * W_IN_SHARD], win_ref[k])
        dx_ref[...] = acc

    return _gridded_call(
        body,
        name=name,
        steps=seq // tile,
        in_specs=[
            pl.BlockSpec((tile, D_IN), lambda i: (i, 0)),
            pl.BlockSpec((tile, D_MODEL), lambda i: (i, 0)),
            _full(win.shape),
        ],
        out_specs=[pl.BlockSpec((tile, D_MODEL), lambda i: (i, 0))],
        out_shape=[jax.ShapeDtypeStruct((seq, D_MODEL), F32)],
        scratch_shapes=[],
        operands=[dproj, dz, win],
        exchange=exchange,
    )


def _input_weight_grad(x, dproj, *, name, exchange=None):
    seq = x.shape[0]
    tile = min(MM_TILE, seq)
    n_rows = seq // tile

    def body(x_ref, dproj_ref, dwin_ref):
        @pl.when(pl.program_id(0) % n_rows == 0)
        def _():
            dwin_ref[...] = jnp.zeros(dwin_ref.shape, F32)

        dwin_ref[0] += _dot_tn(x_ref[...].astype(BF16), dproj_ref[...])

    return _gridded_call(
        body,
        name=name,
        steps=N_CHIPS * n_rows,
        in_specs=[
            pl.BlockSpec((tile, D_MODEL), lambda s: (s % n_rows, 0)),
            pl.BlockSpec((tile, W_IN_SHARD), lambda s: (s % n_rows, s // n_rows)),
        ],
        out_specs=[pl.BlockSpec((1, D_MODEL, W_IN_SHARD), lambda s: (s // n_rows, 0, 0))],
        out_shape=[jax.ShapeDtypeStruct((N_CHIPS, D_MODEL, W_IN_SHARD), F32)],
        scratch_shapes=[],
        operands=[x, dproj],
        exchange=exchange,
    )


def _expand_sgb(sg_b):
    return jnp.repeat(sg_b.T, HEAD_DIM, axis=1)


def _pack_small_weights(sg_ln_g, sg_ln_b, pool_scale, cc_dw_b, cc_ln_g, cc_ln_b, conv_a_w, cc_dw_w, sg_b):
    vec = jnp.stack([sg_ln_g, sg_ln_b, pool_scale, cc_dw_b, cc_ln_g, cc_ln_b])
    return jnp.concatenate(
        [
            jnp.pad(vec, ((0, RW_CONVA - RW_VEC - 6), (0, 0))),
            jnp.pad(conv_a_w, ((0, RW_DW - RW_CONVA - CONV_A), (0, 0))),
            jnp.pad(cc_dw_w, ((0, RW_SGB - RW_DW - CONV_D), (0, 0))),
            _expand_sgb(sg_b),
        ]
    )


def _sg_w_cat(sg_w):
    cat = jnp.transpose(sg_w, (1, 0, 2)).reshape(CHUNK, N_SUB * CHUNK)
    cat_t = jnp.transpose(sg_w, (2, 0, 1)).reshape(CHUNK, N_SUB * CHUNK)
    return cat, cat_t


def _pool_block_diag(pool_w):
    tiled = jnp.tile(pool_w.reshape(D_G, HEAD_DIM), (1, N_SUB))
    row = lax.broadcasted_iota(jnp.int32, (D_G, D_G), 0) // HEAD_DIM
    col = lax.broadcasted_iota(jnp.int32, (D_G, D_G), 1) // HEAD_DIM
    return jnp.where(row == col, tiled, 0.0)


def _prepare_layer(mem, w, l):
    cat, cat_t = _sg_w_cat(w["sg_w"])
    kst, vst = _kv_forward(mem, w["w_kv"], name=f"kv_fwd{l}")
    return dict(
        win=w["w_in"],
        wout=w["w_out"],
        pww=w["cc_pw_w"],
        sw=_pack_small_weights(
            w["sg_ln_g"], w["sg_ln_b"], w["pool_scale"], w["cc_dw_b"], w["cc_ln_g"], w["cc_ln_b"],
            w["conv_a_w"], w["cc_dw_w"], w["sg_b"],
        ),
        wcat=cat,
        wcat_t=cat_t,
        poolw=_pool_block_diag(w["pool_w"]).astype(BF16),
        ln=jnp.stack([w["ln_g"], w["ln_b"]]),
        kst=kst,
        vst=vst,
    )


def _forward(l, h, p, tgt, exchange=None, proj=None):
    return _layer_forward(
        h, p["win"] if proj is None else proj, p["kst"], p["vst"], p["wout"], p["sw"], p["wcat"], p["poolw"], p["pww"],
        p["ln"], tgt, name=f"layer_fwd{l}", exchange=exchange, projected=proj is not None,
    )


def _backward(l, dxn, s, p, exchange=None):
    return _layer_backward(
        dxn, s[2], s[0], s[1], s[4], p["kst"], p["vst"], p["wout"], p["sw"], p["wcat"], p["wcat_t"], p["poolw"],
        p["pww"], p["ln"], name=f"layer_bwd{l}", exchange=exchange,
    )


def _place():
    x, y, c = lax.axis_index("x"), lax.axis_index("y"), lax.axis_index("c")
    others = [(1 - x, y), (x, 1 - y), (1 - x, 1 - y)]
    return x, y, c, others


def _half(ref, c, axis):
    n = ref.shape[axis] // 2
    if axis == 0:
        return ref.at[pl.ds(c * n, n)]
    return ref.at[:, pl.ds(c * n, n)]


def _place_own_block(place, stacked, layer, dtypes, *, name):
    n = len(stacked)

    def body(place_ref, *refs):
        for a in range(n):
            refs[n + a][...] = refs[a][...].astype(dtypes[a])

    def block(s):
        return (1,) + s.shape[1:]

    return pl.pallas_call(
        body,
        name=name,
        grid_spec=pltpu.PrefetchScalarGridSpec(
            num_scalar_prefetch=1,
            grid=(1,),
            in_specs=[pl.BlockSpec(block(s), lambda i, place_ref: (layer, 0, 0)) for s in stacked],
            out_specs=[pl.BlockSpec(block(s), lambda i, place_ref: (place_ref[1], 0, 0)) for s in stacked],
        ),
        out_shape=[jax.ShapeDtypeStruct((N_CHIPS,) + s.shape[1:], dt) for s, dt in zip(stacked, dtypes)],
        compiler_params=pltpu.CompilerParams(dimension_semantics=("arbitrary",), vmem_limit_bytes=VMEM_LIMIT),
    )(place, *stacked)


def _sds(a):
    return jax.ShapeDtypeStruct(a.shape, a.dtype)


def _gather_exchange(bufs):
    n = len(bufs)

    def remote(sems, block, k, to):
        return pltpu.make_async_remote_copy(
            src_ref=block, dst_ref=block, send_sem=sems[0].at[k], recv_sem=sems[1].at[k], device_id=to, device_id_type=MESH
        )

    def before(step, steps, refs, outs, sems):
        def send():
            x, y, c, others = _place()
            for j, (px, py) in enumerate(others):
                for a in range(n):
                    remote(sems, _half(refs[a].at[2 * x + y], c, 0), 3 * a + j, (px, py, c)).start()

        _when(step == 0, send)

    def after(step, steps, refs, outs, sems):
        def pass_on():
            x, y, c, others = _place()
            for j, (px, py) in enumerate(others):
                for a in range(n):
                    landed = _half(refs[a].at[2 * px + py], c, 0)
                    remote(sems, landed, 3 * a + j, (px, py, c)).wait_recv()
                    remote(sems, landed, 3 * n + 3 * a + j, (x, y, 1 - c)).start()

        def finish():
            x, y, c, others = _place()
            for j, (px, py) in enumerate(others):
                for a in range(n):
                    remote(sems, _half(refs[a].at[2 * px + py], 1 - c, 0), 3 * n + 3 * a + j, (x, y, 1 - c)).wait_recv()
            for a in range(n):
                mine = _half(refs[a].at[2 * x + y], c, 0)
                for k in range(3 * a, 3 * a + 3):
                    remote(sems, mine, k, (x, y, 1 - c)).wait_send()
                    remote(sems, mine, 3 * n + k, (x, y, 1 - c)).wait_send()

        _when(step == (3 * steps) // 4, pass_on)
        _when(step == steps - 1, finish)

    return _Exchange(bufs, [(_sds(b), a) for a, b in enumerate(bufs)], [6 * n, 6 * n], before, after)


def _project_while_gathering(x, order, bufs, *, name):
    n = len(bufs)
    seq = x.shape[0]
    tile = min(MM_TILE, seq)
    n_rows = seq // tile

    def body(order_ref, x_ref, *refs):
        outs = refs[n : 2 * n]
        proj_ref = refs[2 * n]
        wblk, local_sem, send_sems, recv_sems = refs[2 * n + 1 :]
        s = pl.program_id(0)
        x_, y_, c, others = _place()
        me, sibling = 2 * x_ + y_, (x_, y_, 1 - c)

        def remote(block, k, to):
            return pltpu.make_async_remote_copy(
                src_ref=block, dst_ref=block, send_sem=send_sems.at[k], recv_sem=recv_sems.at[k], device_id=to, device_id_type=MESH
            )

        @pl.when(s == 0)
        def _():
            for a in range(n):
                for j, (px, py) in enumerate(others):
                    remote(_half(outs[a].at[me], c, 0), 3 * a + j, (px, py, c)).start()

        def take(a, j):
            px, py = others[j]
            landed = _half(outs[a].at[2 * px + py], c, 0)
            remote(landed, 3 * a + j, (px, py, c)).wait_recv()
            remote(landed, 3 * n + 3 * a + j, sibling).start()
            remote(_half(outs[a].at[2 * px + py], 1 - c, 0), 3 * n + 3 * a + j, sibling).wait_recv()

        for j in range(3):
            pl.when(s == (j + 1) * n_rows)(functools.partial(take, 0, j))

        @pl.when(s % n_rows == 0)
        def _():
            copy = pltpu.make_async_copy(outs[0].at[order_ref[s // n_rows]], wblk, local_sem)
            copy.start()
            copy.wait()

        proj_ref[...] = _dot(x_ref[...].astype(BF16), wblk[...])

        @pl.when(s == N_CHIPS * n_rows - 1)
        def _():
            for a in range(1, n):
                for j in range(3):
                    take(a, j)
            for a in range(n):
                mine = _half(outs[a].at[me], c, 0)
                for k in range(3 * a, 3 * a + 3):
                    remote(mine, k, sibling).wait_send()
                    remote(mine, 3 * n + k, sibling).wait_send()

    outs = pl.pallas_call(
        body,
        name=name,
        grid_spec=pltpu.PrefetchScalarGridSpec(
            num_scalar_prefetch=1,
            grid=(N_CHIPS * n_rows,),
            in_specs=[pl.BlockSpec((tile, D_MODEL), lambda s, order_ref: (s % n_rows, 0))] + [ANY] * n,
            out_specs=[ANY] * n + [pl.BlockSpec((tile, W_IN_SHARD), lambda s, order_ref: (s % n_rows, order_ref[s // n_rows]))],
            scratch_shapes=[
                pltpu.VMEM(bufs[0].shape[1:], bufs[0].dtype),
                pltpu.SemaphoreType.DMA,
                pltpu.SemaphoreType.DMA((6 * n,)),
                pltpu.SemaphoreType.DMA((6 * n,)),
            ],
        ),
        out_shape=[_sds(b) for b in bufs] + [jax.ShapeDtypeStruct((seq, D_IN), F32)],
        input_output_aliases={2 + a: a for a in range(n)},
        compiler_params=pltpu.CompilerParams(dimension_semantics=("arbitrary",), vmem_limit_bytes=VMEM_LIMIT),
    )(order, x, *bufs)
    return outs[n], list(outs[:n])


def _swap_exchange(grads):
    n = len(grads)

    def copy(refs, outs, sems, a):
        x, y, c, _ = _place()
        return pltpu.make_async_remote_copy(
            src_ref=_half(refs[a], 1 - c, 1), dst_ref=outs[a], send_sem=sems[0].at[a], recv_sem=sems[1].at[a],
            device_id=(x, y, 1 - c), device_id_type=MESH,
        )

    def before(step, steps, refs, outs, sems):
        _when(step == 0, lambda: [copy(refs, outs, sems, a).start() for a in range(n)] and None)

    def after(step, steps, refs, outs, sems):
        _when(step == steps - 1, lambda: [copy(refs, outs, sems, a).wait() for a in range(n)] and None)

    outputs = [(jax.ShapeDtypeStruct((N_CHIPS, g.shape[1] // 2, g.shape[2]), g.dtype), None) for g in grads]
    return _Exchange(grads, outputs, [n, n], before, after)


def _add_sibling_half(place, grads, received, wire, *, name):
    n = len(grads)

    def body(place_ref, *refs):
        k = pl.program_id(0)
        for a in range(n):
            pair = (refs[a][...] + refs[n + a][...]).astype(wire[a])
            refs[2 * n + a][...] = pair

            @pl.when(k == place_ref[1])
            def _(a=a, pair=pair):
                refs[3 * n + a][...] = pair

    def block(g):
        return (1, g.shape[1] // 2, g.shape[2])

    return pl.pallas_call(
        body,
        name=name,
        grid_spec=pltpu.PrefetchScalarGridSpec(
            num_scalar_prefetch=1,
            grid=(N_CHIPS,),
            in_specs=[pl.BlockSpec(block(g), lambda k, place_ref: (k, place_ref[0], 0)) for g in grads]
            + [pl.BlockSpec(block(g), lambda k, place_ref: (k, 0, 0)) for g in grads],
            out_specs=[pl.BlockSpec(block(g), lambda k, place_ref: (k, 0, 0)) for g in grads]
            + [pl.BlockSpec(block(g), lambda k, place_ref: (place_ref[1], 0, 0)) for g in grads],
        ),
        out_shape=[jax.ShapeDtypeStruct(r.shape, dt) for r, dt in zip(received, wire)] * 2,
        compiler_params=pltpu.CompilerParams(dimension_semantics=("arbitrary",), vmem_limit_bytes=VMEM_LIMIT),
    )(place, *grads, *received)


def _scatter_exchange(pairs, landing):
    n = len(pairs)

    def copy(refs, sems, a, j, px, py):
        x, y, c, _ = _place()
        return pltpu.make_async_remote_copy(
            src_ref=refs[a].at[2 * px + py], dst_ref=refs[n + a].at[2 * x + y], send_sem=sems[0].at[3 * a + j],
            recv_sem=sems[1].at[3 * a + j], device_id=(px, py, c), device_id_type=MESH,
        )

    def before(step, steps, refs, outs, sems):
        def send():
            for j, (px, py) in enumerate(_place()[3]):
                for a in range(n):
                    copy(refs, sems, a, j, px, py).start()

        _when(step == 0, send)

    def after(step, steps, refs, outs, sems):
        def finish():
            x, y, c, others = _place()
            for j, (px, py) in enumerate(others):
                for a in range(n):
                    landed = refs[n + a].at[2 * px + py]
                    pltpu.make_async_remote_copy(
                        src_ref=landed, dst_ref=landed, send_sem=sems[0].at[3 * a + j], recv_sem=sems[1].at[3 * a + j],
                        device_id=(px, py, c), device_id_type=MESH,
                    ).wait_recv()
            for j, (px, py) in enumerate(others):
                for a in range(n):
                    copy(refs, sems, a, j, px, py).wait_send()

        _when(step == steps - 1, finish)

    return _Exchange(pairs + landing, [(_sds(b), n + a) for a, b in enumerate(landing)], [3 * n, 3 * n], before, after)


SUM_STEPS = 2


def _sum_chip_blocks(place, parts, keep_chip_axis, *, name):
    n = len(parts)

    def body(place_ref, *refs):
        for a in range(n):
            p = refs[a]
            total = (p[0].astype(F32) + p[1].astype(F32)) + (p[2].astype(F32) + p[3].astype(F32))
            if keep_chip_axis[a]:
                refs[n + a][0] = total
            else:
                refs[n + a][...] = total

    def in_spec(p):
        return pl.BlockSpec((N_CHIPS, p.shape[1] // SUM_STEPS, p.shape[2]), lambda i, place_ref: (0, i, 0))

    def out_spec(p, keep):
        rows = p.shape[1] // SUM_STEPS
        if keep:
            return pl.BlockSpec((1, rows, p.shape[2]), lambda i, place_ref: (place_ref[1], place_ref[0] * SUM_STEPS + i, 0))
        return pl.BlockSpec((rows, p.shape[2]), lambda i, place_ref: (place_ref[0] * SUM_STEPS + i, 0))

    def out_shape(p, keep):
        shape = (2 * p.shape[1], p.shape[2])
        return jax.ShapeDtypeStruct((N_CHIPS,) + shape if keep else shape, F32)

    return pl.pallas_call(
        body,
        name=name,
        grid_spec=pltpu.PrefetchScalarGridSpec(
            num_scalar_prefetch=1,
            grid=(SUM_STEPS,),
            in_specs=[in_spec(p) for p in parts],
            out_specs=[out_spec(p, k) for p, k in zip(parts, keep_chip_axis)],
        ),
        out_shape=[out_shape(p, k) for p, k in zip(parts, keep_chip_axis)],
        compiler_params=pltpu.CompilerParams(dimension_semantics=("arbitrary",), vmem_limit_bytes=VMEM_LIMIT),
    )(place, *parts)


def _join_exchange(bufs, keep_chip_axis):
    n = len(bufs)
    kept = [a for a in range(n) if keep_chip_axis[a]]
    base = n

    def copy(refs, sems, block, k, to):
        return pltpu.make_async_remote_copy(
            src_ref=block, dst_ref=block, send_sem=sems[0].at[k], recv_sem=sems[1].at[k], device_id=to, device_id_type=MESH
        )

    def mine(refs, a, cc):
        x, y, _, _ = _place()
        return _half(refs[a].at[2 * x + y] if keep_chip_axis[a] else refs[a], cc, 0)

    def before(step, steps, refs, outs, sems):
        def send():
            x, y, c, others = _place()
            for a in range(n):
                copy(refs, sems, mine(refs, a, c), a, (x, y, 1 - c)).start()
            for i, a in enumerate(kept):
                for j, (px, py) in enumerate(others):
                    copy(refs, sems, mine(refs, a, c), base + 6 * i + j, (px, py, c)).start()

        _when(step == 0, send)

    def after(step, steps, refs, outs, sems):
        def pass_on():
            x, y, c, others = _place()
            for i, a in enumerate(kept):
                for j, (px, py) in enumerate(others):
                    landed = _half(refs[a].at[2 * px + py], c, 0)
                    copy(refs, sems, landed, base + 6 * i + j, (px, py, c)).wait_recv()
                    copy(refs, sems, landed, base + 6 * i + 3 + j, (x, y, 1 - c)).start()

        def finish():
            x, y, c, others = _place()
            for a in range(n):
                copy(refs, sems, mine(refs, a, 1 - c), a, (x, y, 1 - c)).wait_recv()
            for i, a in enumerate(kept):
                for j, (px, py) in enumerate(others):
                    passed = _half(refs[a].at[2 * px + py], 1 - c, 0)
                    copy(refs, sems, passed, base + 6 * i + 3 + j, (x, y, 1 - c)).wait_recv()
            for a in range(n):
                copy(refs, sems, mine(refs, a, c), a, (x, y, 1 - c)).wait_send()
            for i, a in enumerate(kept):
                for k in range(6):
                    copy(refs, sems, mine(refs, a, c), base + 6 * i + k, (x, y, 1 - c)).wait_send()

        _when(step == steps // 2, pass_on)
        _when(step == steps - 1, finish)

    return _Exchange(bufs, [(_sds(b), a) for a, b in enumerate(bufs)], [n + 6 * len(kept)] * 2, before, after)


def _adamw(w, g, m, v):
    m = ADAM_B1 * m + (1.0 - ADAM_B1) * g
    v = ADAM_B2 * v + (1.0 - ADAM_B2) * (g * g)
    m_hat = m / (1.0 - ADAM_B1**ADAM_STEP)
    v_hat = v / (1.0 - ADAM_B2**ADAM_STEP)
    delta = -ADAM_LR * (m_hat / (jnp.sqrt(v_hat) + ADAM_EPS) + ADAM_WD * w)
    return delta, m, v


def _adamw_large(w, m, v, layer_grads, *, name):
    depth, rows, cols = w.shape
    tile = math.gcd(rows, ADAM_TILE)
    assert tile % 8 == 0

    def body(w_ref, m_ref, v_ref, *refs):
        g_refs, (g_out, d_out, m_out, v_out) = refs[:depth], refs[depth:]
        for l in range(depth):

            @pl.when(pl.program_id(0) == l)
            def _(l=l):
                g = g_refs[l][...]
                delta, m_new, v_new = _adamw(w_ref[0], g, m_ref[0], v_ref[0])
                g_out[0], d_out[0], m_out[0], v_out[0] = g, delta, m_new, v_new

    def stacked():
        return pl.BlockSpec((1, tile, cols), lambda l, i: (l, i, 0))

    def layer_spec(l):
        return pl.BlockSpec((tile, cols), lambda k, i: (jnp.where(k == l, i, 0), 0))

    shape = jax.ShapeDtypeStruct(w.shape, F32)
    return pl.pallas_call(
        body,
        name=name,
        grid=(depth, rows // tile),
        in_specs=[stacked(), stacked(), stacked()] + [layer_spec(l) for l in range(depth)],
        out_specs=[stacked()] * 4,
        out_shape=[shape] * 4,
        compiler_params=pltpu.CompilerParams(dimension_semantics=("arbitrary", "arbitrary"), vmem_limit_bytes=VMEM_LIMIT),
    )(w, m, v, *layer_grads)


def _packed_pieces(name, l):
    vecs = ("sg_ln_g", "sg_ln_b", "pool_scale", "cc_dw_b", "cc_ln_g", "cc_ln_b")
    if name in vecs:
        r = RG_VEC + vecs.index(name)
        return [((slice(l, l + 1), slice(None)), slice(r, r + 1), slice(None))]
    if name in ("ln_g", "ln_b"):
        r = RG_LN + (4 if name == "ln_b" else 0)
        return [((slice(l, l + 1), slice(j * D_G, (j + 1) * D_G)), slice(r + j, r + j + 1), slice(None)) for j in range(4)]
    assert name == "sg_w"
    return [
        ((l, h), slice(RG_SGW + CHUNK * (h // 2), RG_SGW + CHUNK * (h // 2 + 1)), slice(CHUNK * (h % 2), CHUNK * (h % 2 + 1)))
        for h in range(N_SUB)
    ]


PACKED_NAMES = ("sg_ln_g", "sg_ln_b", "pool_scale", "cc_dw_b", "cc_ln_g", "cc_ln_b", "ln_g", "ln_b", "sg_w")


def _adamw_packed(packed, ws, ms, vs, *, name):
    n, depth = len(ws), len(packed)

    def body(*refs):
        packed_refs, refs = refs[:depth], refs[depth:]
        for a, leaf in enumerate(PACKED_NAMES):
            for l in range(depth):
                for at, rows, cols in _packed_pieces(leaf, l):
                    g = packed_refs[l][rows, cols]
                    delta, m_new, v_new = _adamw(refs[a][at], g, refs[n + a][at], refs[2 * n + a][at])
                    refs[3 * n + a][at] = g
                    refs[4 * n + a][at] = delta
                    refs[5 * n + a][at] = m_new
                    refs[6 * n + a][at] = v_new

    shapes = [jax.ShapeDtypeStruct(w.shape, F32) for w in ws]
    outs = pl.pallas_call(body, name=name, out_shape=shapes * 4)(*packed, *ws, *ms, *vs)
    return outs[:n], outs[n : 2 * n], outs[2 * n : 3 * n], outs[3 * n :]


def _adamw_small(ws, gs, ms, vs, *, name):
    n = len(ws)

    def body(*refs):
        for a in range(n):
            delta, m_new, v_new = _adamw(refs[a][...], refs[n + a][...], refs[2 * n + a][...], refs[3 * n + a][...])
            refs[4 * n + a][...] = delta
            refs[5 * n + a][...] = m_new
            refs[6 * n + a][...] = v_new

    shapes = [jax.ShapeDtypeStruct(w.shape, F32) for w in ws]
    outs = pl.pallas_call(body, name=name, out_shape=shapes * 3)(*ws, *gs, *ms, *vs)
    return outs[:n], outs[n : 2 * n], outs[2 * n :]


WEIGHT_NAMES = (
    "w_in", "conv_a_w", "sg_ln_g", "sg_ln_b", "sg_w", "sg_b", "pool_w", "pool_scale", "cc_dw_w", "cc_dw_b", "cc_ln_g",
    "cc_ln_b", "cc_pw_w", "w_kv", "w_out", "ln_g", "ln_b",
)
LARGE = ("w_in", "cc_pw_w", "w_kv", "w_out")
TAPS_ROWS = 48


def _unpack_small_grads(small, chip):
    out = {}
    for r, k in enumerate(("sg_ln_g", "sg_ln_b", "pool_scale", "cc_dw_b", "cc_ln_g", "cc_ln_b")):
        out[k] = small[RG_VEC + r]
    out["conv_a_w"] = lax.dynamic_slice_in_dim(small[RG_CONVA : RG_CONVA + CONV_A], chip * HEAD_DIM, HEAD_DIM, axis=1)
    out["cc_dw_w"] = lax.dynamic_slice_in_dim(small[RG_DW : RG_DW + CONV_D], chip * HEAD_DIM, HEAD_DIM, axis=1)
    cat = jnp.concatenate([small[RG_SGW : RG_SGW + CHUNK], small[RG_SGW + CHUNK : RG_SGW + 2 * CHUNK]], axis=1)
    out["sg_w"] = jnp.transpose(cat.reshape(CHUNK, N_SUB, CHUNK), (1, 0, 2))
    out["sg_b"] = small[RG_SGB : RG_SGB + CHUNK].reshape(CHUNK, N_SUB, HEAD_DIM).sum(-1).T
    pool = small[RG_POOL : RG_POOL + D_G]
    out["pool_w"] = jnp.stack(
        [pool[g * HEAD_DIM : (g + 1) * HEAD_DIM, g * HEAD_DIM : (g + 1) * HEAD_DIM] for g in range(N_SUB)]
    )
    out["ln_g"] = small[RG_LN : RG_LN + 4].reshape(D_MODEL)
    out["ln_b"] = small[RG_LN + 4 : RG_LN + 8].reshape(D_MODEL)
    return out


def kernel(x, mem, w_in, conv_a_w, sg_ln_g, sg_ln_b, sg_w, sg_b, pool_w, pool_scale, cc_dw_w, cc_dw_b, cc_ln_g, cc_ln_b, cc_pw_w, w_kv, w_out, ln_g, ln_b, loss_target, m_w_in, m_conv_a_w, m_sg_ln_g, m_sg_ln_b, m_sg_w, m_sg_b, m_pool_w, m_pool_scale, m_cc_dw_w, m_cc_dw_b, m_cc_ln_g, m_cc_ln_b, m_cc_pw_w, m_w_kv, m_w_out, m_ln_g, m_ln_b, v_w_in, v_conv_a_w, v_sg_ln_g, v_sg_ln_b, v_sg_w, v_sg_b, v_pool_w, v_pool_scale, v_cc_dw_w, v_cc_dw_b, v_cc_ln_g, v_cc_ln_b, v_cc_pw_w, v_w_kv, v_w_out, v_ln_g, v_ln_b):
    given = dict(locals())
    weights = {k: given[k] for k in WEIGHT_NAMES}
    chip = 2 * lax.axis_index("x") + lax.axis_index("y")
    place = jnp.stack([lax.axis_index("c"), chip]).astype(jnp.int32)

    x0, mem0 = x[0], mem[0]

    taps = jnp.concatenate([conv_a_w, cc_dw_w], axis=1)
    taps = jnp.pad(taps, ((0, 0), (0, TAPS_ROWS - taps.shape[1]), (0, 0)))

    def own_blocks(l):
        return _place_own_block(
            place, [w_in, w_out, w_kv, cc_pw_w, taps], l, [BF16, BF16, BF16, BF16, F32], name=f"place_weights{l}"
        )

    def layer_operands(l, gathered):
        g_in, g_out, g_kv, g_pw, g_taps = gathered
        taps_full = jnp.transpose(g_taps, (1, 0, 2)).reshape(TAPS_ROWS, D_G)
        full = dict(
            w_in=g_in,
            w_out=g_out.reshape(D_MIX, D_MODEL),
            w_kv=g_kv.reshape(D_MODEL, 2 * D_G),
            cc_pw_w=g_pw.reshape(D_G, D_G),
            conv_a_w=taps_full[0:CONV_A],
            cc_dw_w=taps_full[CONV_A : CONV_A + CONV_D],
            **{k: weights[k][l] for k in WEIGHT_NAMES if k not in LARGE + ("conv_a_w", "cc_dw_w")},
        )
        return _prepare_layer(mem0, full, l)

    def layer_grads(l, x_in, bwd, small, exchange=None):
        dproj, _, dwout, dkst, dvst, dpw, _ = bwd
        (dwin,), carried = _input_weight_grad(x_in, dproj, name=f"w_in_grad{l}", exchange=exchange)
        grads = [
            dwin,
            dwout.reshape(N_CHIPS, D_MIX // N_CHIPS, D_MODEL),
            _kv_backward(mem0, dkst, dvst, name=f"kv_bwd{l}").reshape(N_CHIPS, D_MODEL // N_CHIPS, 2 * D_G),
            dpw.reshape(N_CHIPS, D_G // N_CHIPS, D_G),
            small.reshape(N_CHIPS, RG_ROWS // N_CHIPS, D_G),
        ]
        return grads, carried

    n_red = 5
    keep = [False, False, False, False, True]

    wire = [BF16, BF16, BF16, BF16, F32]

    def reduced_layer(joined):
        r_in, r_out, r_kv, r_pw, small_all = joined
        packed = small_all.reshape(RG_ROWS, D_G)
        out = _unpack_small_grads(packed, chip)
        out.update(w_in=r_in, w_out=r_out, w_kv=r_kv, cc_pw_w=r_pw, packed=packed)
        return out

    blocks0, blocks1 = own_blocks(0), own_blocks(1)
    xi, yi = lax.axis_index("x"), lax.axis_index("y")
    order = jnp.stack([chip, 2 * (1 - xi) + yi, 2 * xi + (1 - yi), 2 * (1 - xi) + (1 - yi)]).astype(jnp.int32)
    proj0, gathered0 = _project_while_gathering(x0, order, blocks0, name="project_gather0")
    p0 = layer_operands(0, gathered0)
    fwd0, gathered1 = _forward(0, x0, p0, None, exchange=_gather_exchange(blocks1), proj=proj0)
    p1 = layer_operands(1, gathered1)
    x1 = fwd0[3]
    fwd1, _ = _forward(1, x1, p1, loss_target[0])

    bwd1, _ = _backward(1, fwd1[3], fwd1, p1)
    small1 = bwd1[6].at[RG_LOSS, :].set(fwd1[5][0, 0])
    grads1, _ = layer_grads(1, x1, bwd1, small1)
    (dx1,), received1 = _input_grad(bwd1[0], bwd1[1], p1["win"], name="input_grad1", exchange=_swap_exchange(grads1))
    pairs1 = _add_sibling_half(place, grads1, received1, wire, name="rs_pair1")
    bwd0, parts1 = _backward(0, dx1, fwd0, p0, exchange=_scatter_exchange(pairs1[:n_red], pairs1[n_red:]))
    halves1 = _sum_chip_blocks(place, parts1, keep, name="rs_sum1")
    grads0, joined1 = layer_grads(0, x0, bwd0, bwd0[6], exchange=_join_exchange(halves1, keep))
    loss = joined1[4].reshape(RG_ROWS, D_G)[RG_LOSS, 0]
    received0 = _run_exchange(_swap_exchange(grads0), name="rs_swap0")
    pairs0 = _add_sibling_half(place, grads0, received0, wire, name="rs_pair0")
    (grad_x,), parts0 = _input_grad(
        bwd0[0], bwd0[1], p0["win"], name="input_grad0", exchange=_scatter_exchange(pairs0[:n_red], pairs0[n_red:])
    )
    halves0 = _sum_chip_blocks(place, parts0, keep, name="rs_sum0")
    reduced = [reduced_layer(_run_exchange(_join_exchange(halves0, keep), name="rs_join0")), reduced_layer(joined1)]

    grad, delta, new_m, new_v = {}, {}, {}, {}
    for k in LARGE:
        w3 = weights[k]
        grad[k], delta[k], new_m[k], new_v[k] = _adamw_large(
            w3, given["m_" + k], given["v_" + k], [reduced[l][k] for l in range(DEPTH)], name=f"adamw_{k}"
        )
    g_p, d_p, m_p, v_p = _adamw_packed(
        [reduced[l]["packed"] for l in range(DEPTH)],
        [weights[k] for k in PACKED_NAMES],
        [given["m_" + k] for k in PACKED_NAMES],
        [given["v_" + k] for k in PACKED_NAMES],
        name="adamw_packed",
    )
    for a, k in enumerate(PACKED_NAMES):
        grad[k], delta[k], new_m[k], new_v[k] = g_p[a], d_p[a], m_p[a], v_p[a]
    small_names = [k for k in WEIGHT_NAMES if k not in LARGE + PACKED_NAMES]
    for k in small_names:
        grad[k] = jnp.stack([reduced[l][k] for l in range(DEPTH)])
    d_s, m_s, v_s = _adamw_small(
        [weights[k] for k in small_names],
        [grad[k] for k in small_names],
        [given["m_" + k] for k in small_names],
        [given["v_" + k] for k in small_names],
        name="adamw_small",
    )
    for a, k in enumerate(small_names):
        delta[k], new_m[k], new_v[k] = d_s[a], m_s[a], v_s[a]

    return (
        loss,
        grad_x[None],
        *[grad[k] for k in WEIGHT_NAMES],
        *[delta[k] for k in WEIGHT_NAMES],
        *[new_m[k] for k in WEIGHT_NAMES],
        *[new_v[k] for k in WEIGHT_NAMES],
    )
```

```python
import functools
import math

import jax
import jax.numpy as jnp
from jax import lax
from jax.experimental import pallas as pl
from jax.experimental.pallas import tpu as pltpu

F32 = jnp.float32
BF16 = jnp.bfloat16

D_MODEL = 1024
DEPTH = 2
D_G = 256
D_MIX = 5 * D_G
D_IN = 9 * D_G + D_MIX
N_SUB = 4
HEAD_DIM = 64
CONV_A = 3
CONV_D = 31
CHUNK = 128
MEM_LEN = 256
N_CHIPS = 4
W_IN_SHARD = D_IN // N_CHIPS
LN_EPS = 1e-5
ALPHA = (2.0 * DEPTH) ** 0.25
ATT_SCALE = 1.0 / math.sqrt(HEAD_DIM)
GELU_C = math.sqrt(2.0 / math.pi)
GELU_A = 0.044715

ADAM_LR = 0.001
ADAM_B1 = 0.9
ADAM_B2 = 0.999
ADAM_EPS = 1e-08
ADAM_WD = 0.01
ADAM_STEP = 10

C_XA, C_BA, C_CA, C_U, C_V, C_XC, C_DA, C_DG, C_Q, C_GATE = (D_G * i for i in range(10))

HALO_A = 8
HALO_C = 16
HALO_D = 32

RW_VEC = 0
RW_CONVA = 16
RW_DW = 24
RW_SGB = 56
RW_ROWS = RW_SGB + CHUNK

RG_VEC = 0
RG_CONVA = 16
RG_DW = 24
RG_SGW = 56
RG_SGB = RG_SGW + 2 * CHUNK
RG_POOL = RG_SGB + CHUNK
RG_LN = RG_POOL + D_G
RG_LOSS = 8
RG_ROWS = 768

VMEM_LIMIT = 62 * 1024 * 1024

AUX_CVD = 0
AUX_PM = D_G
AUX_P = 2 * D_G
AUX_COLS = AUX_P + N_SUB * MEM_LEN
SEQ_TILE = 256
FWD_TILE = 512
MM_TILE = 1024
ADAM_TILE = 512

MESH = pl.DeviceIdType.MESH
ANY = pl.BlockSpec(memory_space=pl.ANY)
NT = (((1,), (1,)), ((), ()))
TN = (((0,), (0,)), ((), ()))


def _dot(a, b):
    return jnp.dot(a, b, preferred_element_type=F32)


def _dot_nt(a, b):
    return lax.dot_general(a, b, NT, preferred_element_type=F32)


def _dot_tn(a, b):
    return lax.dot_general(a, b, TN, preferred_element_type=F32)


def _full(shape):
    zeros = (0,) * len(shape)
    return pl.BlockSpec(shape, lambda *_: zeros)


class _Exchange:
    def __init__(self, operands, outputs, sem_counts, before, after):
        self.operands, self.outputs, self.sem_counts, self.before, self.after = operands, outputs, sem_counts, before, after

    def specs(self, first_input, first_output):
        aliases = {first_input + src: first_output + j for j, (_, src) in enumerate(self.outputs) if src is not None}
        return (
            [ANY] * len(self.operands),
            [ANY] * len(self.outputs),
            [sds for sds, _ in self.outputs],
            [pltpu.SemaphoreType.DMA((k,)) for k in self.sem_counts],
            aliases,
        )

    def split(self, ins, outs):
        refs = list(ins)
        for j, (_, src) in enumerate(self.outputs):
            if src is not None:
                refs[src] = outs[j]
        return refs


def _both(first, second):
    n1, m1, s1 = len(first.operands), len(first.outputs), len(first.sem_counts)
    outputs = first.outputs + [(sds, None if src is None else n1 + src) for sds, src in second.outputs]

    def before(step, steps, refs, outs, sems):
        first.before(step, steps, refs[:n1], outs[:m1], sems[:s1])
        second.before(step, steps, refs[n1:], outs[m1:], sems[s1:])

    def after(step, steps, refs, outs, sems):
        first.after(step, steps, refs[:n1], outs[:m1], sems[:s1])
        second.after(step, steps, refs[n1:], outs[m1:], sems[s1:])

    return _Exchange(first.operands + second.operands, outputs, first.sem_counts + second.sem_counts, before, after)


def _when(cond, fn):
    if isinstance(cond, bool):
        if cond:
            fn()
    else:
        pl.when(cond)(fn)


def _run_exchange(exchange, *, name):
    n_in, n_out = len(exchange.operands), len(exchange.outputs)
    in_specs, out_specs, out_shape, sems, aliases = exchange.specs(0, 0)

    def body(*refs):
        ins, outs, sem_refs = refs[:n_in], refs[n_in : n_in + n_out], refs[n_in + n_out :]
        refs = exchange.split(ins, outs)
        exchange.before(0, 1, refs, outs, sem_refs)
        exchange.after(0, 1, refs, outs, sem_refs)

    return pl.pallas_call(
        body, name=name, in_specs=in_specs, out_specs=out_specs, out_shape=out_shape, scratch_shapes=sems,
        input_output_aliases=aliases,
    )(*exchange.operands)


def _gridded_call(body, *, name, steps, in_specs, out_specs, out_shape, scratch_shapes, operands, exchange=None):
    params = pltpu.CompilerParams(dimension_semantics=("arbitrary",), vmem_limit_bytes=VMEM_LIMIT)
    if exchange is None:
        outs = pl.pallas_call(
            body, name=name, grid=(steps,), in_specs=in_specs, out_specs=out_specs, out_shape=out_shape,
            scratch_shapes=scratch_shapes, compiler_params=params,
        )(*operands)
        return list(outs), []
    n_in, n_out, n_scr = len(in_specs), len(out_specs), len(scratch_shapes)
    x_in, x_out = len(exchange.operands), len(exchange.outputs)
    ex_in_specs, ex_out_specs, ex_out_shape, ex_sems, aliases = exchange.specs(n_in, n_out)

    def full(*refs):
        own_in, refs = refs[:n_in], refs[n_in:]
        ex_in, refs = refs[:x_in], refs[x_in:]
        own_out, refs = refs[:n_out], refs[n_out:]
        ex_out, refs = refs[:x_out], refs[x_out:]
        own_scr, sem_refs = refs[:n_scr], refs[n_scr:]
        ex_refs = exchange.split(ex_in, ex_out)
        step = pl.program_id(0)
        exchange.before(step, steps, ex_refs, ex_out, sem_refs)
        body(*own_in, *own_out, *own_scr)
        exchange.after(step, steps, ex_refs, ex_out, sem_refs)

    outs = pl.pallas_call(
        full, name=name, grid=(steps,), in_specs=in_specs + ex_in_specs, out_specs=out_specs + ex_out_specs,
        out_shape=out_shape + ex_out_shape, scratch_shapes=scratch_shapes + ex_sems, input_output_aliases=aliases,
        compiler_params=params,
    )(*operands, *exchange.operands)
    return list(outs[:n_out]), list(outs[n_out:])


def _sigmoid(x):
    return 0.5 * jnp.tanh(0.5 * x) + 0.5


def _gelu(x):
    t = jnp.tanh(GELU_C * (x + GELU_A * x * x * x))
    return 0.5 * x * (1.0 + t), t


def _gelu_grad(x, t):
    return 0.5 * (1.0 + t) + 0.5 * x * (1.0 - t * t) * (GELU_C * (1.0 + 3.0 * GELU_A * x * x))


def _normalize(v):
    mu = jnp.mean(v, axis=-1, keepdims=True)
    d = v - mu
    var = jnp.mean(d * d, axis=-1, keepdims=True)
    rstd = lax.rsqrt(var + LN_EPS)
    return d * rstd, rstd


def _normalize_grad(dhat, hat, rstd):
    m1 = jnp.mean(dhat, axis=-1, keepdims=True)
    m2 = jnp.mean(dhat * hat, axis=-1, keepdims=True)
    return rstd * (dhat - m1 - hat * m2)


def _lane(width=D_G):
    return lax.broadcasted_iota(jnp.int32, (1, width), 1)


def _head_masks():
    head = _lane() // HEAD_DIM
    return [(head == h).astype(F32) for h in range(N_SUB)]


def _stack_heads(v, masks):
    return jnp.concatenate([v * m for m in masks], axis=0)


def _tril_mask_cat():
    t = lax.broadcasted_iota(jnp.int32, (CHUNK, N_SUB * CHUNK), 0)
    s = lax.broadcasted_iota(jnp.int32, (CHUNK, N_SUB * CHUNK), 1) % CHUNK
    return s <= t


def _triu_mask_cat():
    s = lax.broadcasted_iota(jnp.int32, (CHUNK, N_SUB * CHUNK), 0)
    t = lax.broadcasted_iota(jnp.int32, (CHUNK, N_SUB * CHUNK), 1) % CHUNK
    return t >= s


def _pool_select(a2, a4, a8, a16):
    lane = _lane()
    return jnp.where(lane < 64, a2, jnp.where(lane < 128, a4, jnp.where(lane < 192, a8, a16)))


def _pool_inv_count(row0, rows):
    t = row0 + lax.broadcasted_iota(jnp.int32, (HALO_C, D_G), 0)
    lane = lax.broadcasted_iota(jnp.int32, (HALO_C, D_G), 1)
    win = jnp.where(lane < 64, 2, jnp.where(lane < 128, 4, jnp.where(lane < 192, 8, 16)))
    head = 1.0 / jnp.minimum(t + 1, win).astype(F32)
    inv_win = jnp.broadcast_to(_pool_select(0.5, 0.25, 0.125, 0.0625), (rows - HALO_C, D_G))
    return jnp.concatenate([head, inv_win], axis=0)


def _trailing_window_sum(halo, cur):
    e = jnp.concatenate([halo, cur], axis=0)
    s2 = e + pltpu.roll(e, 1, 0)
    s4 = s2 + pltpu.roll(s2, 2, 0)
    s8 = s4 + pltpu.roll(s4, 4, 0)
    s16 = s8 + pltpu.roll(s8, 8, 0)
    return _pool_select(s2, s4, s8, s16)[HALO_C:]


def _leading_window_sum(cur, halo):
    e = jnp.concatenate([cur, halo], axis=0)
    n = e.shape[0]
    s2 = e + pltpu.roll(e, n - 1, 0)
    s4 = s2 + pltpu.roll(s2, n - 2, 0)
    s8 = s4 + pltpu.roll(s4, n - 4, 0)
    s16 = s8 + pltpu.roll(s8, n - 8, 0)
    return _pool_select(s2, s4, s8, s16)[: cur.shape[0]]


def _softmax_blocks(sc):
    out = []
    for h in range(N_SUB):
        s = sc[:, h * MEM_LEN : (h + 1) * MEM_LEN]
        e = jnp.exp(s - jnp.max(s, axis=-1, keepdims=True))
        out.append(e * (1.0 / jnp.sum(e, axis=-1, keepdims=True)))
    return jnp.concatenate(out, axis=-1)


STRIP = 32
SHIFTS = 8


def _fill_shifts(buf):
    n = buf.shape[1] - SHIFTS
    for r in range(1, SHIFTS):
        buf[r, 0:n, :] = buf[0, r : r + n, :]


def _shifted(buf, off, rows):
    r = off % SHIFTS
    return buf[r, off - r : off - r + rows, :]


def _sgu_mix(vn, wcat_b, sgb, masks):
    vbd = _stack_heads(vn, masks).astype(BF16)
    return _dot(wcat_b, vbd) + sgb, vbd


def _layer_forward(x, win, kst, vst, wout, sw, wcat, poolw, pww, ln, tgt, *, name, exchange=None):
    seq = x.shape[0]
    tile = min(FWD_TILE, seq)
    n_tiles = seq // tile
    last = tgt is not None

    def body(*refs):
        x_ref, win_ref, kst_ref, vst_ref, wout_ref, sw_ref, wcat_ref, pool_ref, pw_ref, ln_ref = refs[:10]
        refs = refs[10:]
        if last:
            tgt_ref, refs = refs[0], refs[1:]
        proj_ref, y_ref, z_ref, out_ref, aux_ref = refs[:5]
        refs = refs[5:]
        if last:
            loss_ref, refs = refs[0], refs[1:]
        pbuf, xchalo, gbuf = refs
        i = pl.program_id(0)

        @pl.when(i == 0)
        def _():
            pbuf[0:HALO_A, :] = jnp.zeros((HALO_A, D_G), F32)
            xchalo[...] = jnp.zeros((HALO_C, D_G), F32)
            gbuf[0, 0:HALO_D, :] = jnp.zeros((HALO_D, D_G), F32)
            if last:
                loss_ref[...] = jnp.zeros((8, 128), F32)

        xt = x_ref[...]
        xb = xt.astype(BF16)

        blocks = {}

        def project(k):
            blocks[k] = _dot(xb, win_ref[k])
            proj_ref[:, k * W_IN_SHARD : (k + 1) * W_IN_SHARD] = blocks[k]

        def cols(start, width=D_G):
            parts, c = [], start
            while c < start + width:
                k, lo = divmod(c, W_IN_SHARD)
                hi = min(W_IN_SHARD, lo + start + width - c)
                parts.append(blocks[k][:, lo:hi])
                c += hi - lo
            return parts[0] if len(parts) == 1 else jnp.concatenate(parts, axis=1)

        project(0)
        project(1)
        masks = _head_masks()

        pbuf[HALO_A : HALO_A + tile, :] = cols(C_CA) * cols(C_XA)
        cv = jnp.zeros((tile, D_G), F32)
        for k in range(CONV_A):
            off = HALO_A - (CONV_A - 1) + k
            cv = cv + sw_ref[RW_CONVA + k : RW_CONVA + k + 1, :] * pbuf[off : off + tile, :]
        y_ref[:, 0:D_G] = cols(C_BA) * cv
        pbuf[0:HALO_A, :] = pbuf[tile : tile + HALO_A, :]

        project(2)

        ua, _ = _gelu(cols(C_U))
        vg, _ = _gelu(cols(C_V))
        vhat, _ = _normalize(vg)
        vn = vhat * sw_ref[RW_VEC : RW_VEC + 1, :] + sw_ref[RW_VEC + 1 : RW_VEC + 2, :]
        wcat_b = jnp.where(_tril_mask_cat(), wcat_ref[...], 0.0).astype(BF16)
        sgb = sw_ref[RW_SGB : RW_SGB + CHUNK, :]
        for j in range(tile // CHUNK):
            rows = slice(j * CHUNK, (j + 1) * CHUNK)
            mixed, _ = _sgu_mix(vn[rows], wcat_b, sgb, masks)
            y_ref[rows, D_G : 2 * D_G] = ua[rows] * mixed

        xc = cols(C_XC)
        wsum = _trailing_window_sum(xchalo[...], xc)
        pm = wsum * _pool_inv_count(i * tile, tile) - xc
        aux_ref[:, AUX_PM : AUX_PM + D_G] = pm
        y_ref[:, 2 * D_G : 3 * D_G] = _dot(pm.astype(BF16), pool_ref[...]) * sw_ref[RW_VEC + 2 : RW_VEC + 3, :]
        xchalo[...] = xc[tile - HALO_C :, :]

        project(3)

        gbuf[0, HALO_D : HALO_D + tile, :] = cols(C_DA) * _sigmoid(cols(C_DG))
        _fill_shifts(gbuf)
        for r0 in range(0, tile, STRIP):
            acc = jnp.zeros((STRIP, D_G), F32) + sw_ref[RW_VEC + 3 : RW_VEC + 4, :]
            for k in range(CONV_D):
                off = HALO_D - (CONV_D - 1) + k
                acc = acc + sw_ref[RW_DW + k : RW_DW + k + 1, :] * _shifted(gbuf, off + r0, STRIP)
            aux_ref[r0 : r0 + STRIP, AUX_CVD : AUX_CVD + D_G] = acc
        nhat, _ = _normalize(aux_ref[:, AUX_CVD : AUX_CVD + D_G])
        nrm = nhat * sw_ref[RW_VEC + 4 : RW_VEC + 5, :] + sw_ref[RW_VEC + 5 : RW_VEC + 6, :]
        y_ref[:, 3 * D_G : 4 * D_G] = _dot((nrm * _sigmoid(nrm)).astype(BF16), pw_ref[...])
        gbuf[0, 0:HALO_D, :] = gbuf[0, tile : tile + HALO_D, :]

        qb = cols(C_Q).astype(BF16)
        p_all = _softmax_blocks(_dot_nt(qb, kst_ref[...]) * ATT_SCALE)
        aux_ref[:, AUX_P:] = p_all
        y_ref[:, 4 * D_G : 5 * D_G] = _dot(p_all.astype(BF16), vst_ref[...])

        gate = cols(C_GATE, D_MIX)
        hid = y_ref[...] * (gate * _sigmoid(gate))
        z = ALPHA * xt + _dot(hid.astype(BF16), wout_ref[...])
        z_ref[...] = z
        zhat, _ = _normalize(z)
        xn = zhat * ln_ref[0:1, :] + ln_ref[1:2, :]
        if last:
            err = xn - tgt_ref[...]
            out_ref[...] = err * (1.0 / D_MODEL)
            loss_ref[...] += jnp.sum(err * err) * (0.5 / D_MODEL)
        else:
            out_ref[...] = xn

    def rows(width):
        return pl.BlockSpec((tile, width), lambda i: (i, 0))

    operands = [x, win, kst, vst, wout, sw, wcat, poolw, pww, ln]
    in_specs = [rows(D_MODEL)] + [_full(a.shape) for a in operands[1:]]
    widths = [D_IN, D_MIX, D_MODEL, D_MODEL, AUX_COLS]
    out_shape = [jax.ShapeDtypeStruct((seq, w), F32) for w in widths]
    out_specs = [rows(w) for w in widths]
    if last:
        operands.append(tgt)
        in_specs.append(rows(D_MODEL))
        out_shape.append(jax.ShapeDtypeStruct((8, 128), F32))
        out_specs.append(_full((8, 128)))
    return _gridded_call(
        body,
        name=name,
        steps=n_tiles,
        in_specs=in_specs,
        out_specs=out_specs,
        out_shape=out_shape,
        scratch_shapes=[
            pltpu.VMEM((HALO_A + tile, D_G), F32),
            pltpu.VMEM((HALO_C, D_G), F32),
            pltpu.VMEM((SHIFTS, HALO_D + tile, D_G), F32),
        ],
        operands=operands,
        exchange=exchange,
    )


def _layer_backward(dxn, z, proj, y, cvd, kst, vst, wout, sw, wcat, wcat_t, poolw, pww, ln, *, name, exchange=None):
    seq = dxn.shape[0]
    tile = min(SEQ_TILE, seq)
    n_tiles = seq // tile
    halo_blocks = tile // HALO_D

    def body(
        dxn_ref, z_ref, proj_ref, halo_ref, y_ref, aux_ref, kst_ref, vst_ref, wout_ref, sw_ref, wcat_ref, wcat_t_ref,
        pool_ref, pw_ref, ln_ref, dproj_ref, dz_ref, dwout_ref, dkst_ref, dvst_ref, dpw_ref, sg_ref,
        pbuf, dcvbuf, rhalo, gbuf, dgbuf, dwacc,
    ):
        i = pl.program_id(0)
        ti = n_tiles - 1 - i

        @pl.when(i == 0)
        def _():
            dwout_ref[...] = jnp.zeros(dwout_ref.shape, F32)
            dkst_ref[...] = jnp.zeros(dkst_ref.shape, F32)
            dvst_ref[...] = jnp.zeros(dvst_ref.shape, F32)
            dpw_ref[...] = jnp.zeros(dpw_ref.shape, F32)
            sg_ref[...] = jnp.zeros(sg_ref.shape, F32)
            dwacc[...] = jnp.zeros(dwacc.shape, F32)
            dcvbuf[tile : tile + HALO_A, :] = jnp.zeros((HALO_A, D_G), F32)
            rhalo[...] = jnp.zeros((HALO_C, D_G), F32)
            dgbuf[0, tile : tile + HALO_D, :] = jnp.zeros((HALO_D, D_G), F32)

        def acc_row(row, val):
            sg_ref[row : row + 1, :] += jnp.sum(val, axis=0, keepdims=True)

        masks = _head_masks()
        has_past = (ti > 0).astype(F32)

        zhat, zrstd = _normalize(z_ref[...])
        dxn_t = dxn_ref[...]
        dlg = jnp.sum(dxn_t * zhat, axis=0, keepdims=True)
        dlb = jnp.sum(dxn_t, axis=0, keepdims=True)
        for j in range(D_MODEL // D_G):
            sg_ref[RG_LN + j : RG_LN + j + 1, :] += dlg[:, j * D_G : (j + 1) * D_G]
            sg_ref[RG_LN + 4 + j : RG_LN + 5 + j, :] += dlb[:, j * D_G : (j + 1) * D_G]
        dz = _normalize_grad(dxn_t * ln_ref[0:1, :], zhat, zrstd)
        dz_ref[...] = dz
        dzb = dz.astype(BF16)

        gate = proj_ref[:, C_GATE:]
        sgm = _sigmoid(gate)
        silu = gate * sgm
        yc = y_ref[...]
        dwout_ref[...] += _dot_tn((yc * silu).astype(BF16), dzb)
        dh = _dot_nt(dzb, wout_ref[...])
        dproj_ref[:, C_GATE:] = (dh * yc * (sgm * (1.0 + gate * (1.0 - sgm)))).astype(BF16)
        dy = dh * silu

        dya = dy[:, 0:D_G]
        xa = proj_ref[:, C_XA : C_XA + D_G]
        ba = proj_ref[:, C_BA : C_BA + D_G]
        ca = proj_ref[:, C_CA : C_CA + D_G]
        past = slice(HALO_D - HALO_A, HALO_D)
        pbuf[0:HALO_A, :] = halo_ref[past, C_CA : C_CA + D_G] * halo_ref[past, C_XA : C_XA + D_G] * has_past
        pbuf[HALO_A : HALO_A + tile, :] = ca * xa
        cv = jnp.zeros((tile, D_G), F32)
        for k in range(CONV_A):
            off = HALO_A - (CONV_A - 1) + k
            cv = cv + sw_ref[RW_CONVA + k : RW_CONVA + k + 1, :] * pbuf[off : off + tile, :]
        dproj_ref[:, C_BA : C_BA + D_G] = (dya * cv).astype(BF16)
        dcv = dya * ba
        dcvbuf[0:tile, :] = dcv
        dp = jnp.zeros((tile, D_G), F32)
        for k in range(CONV_A):
            off = HALO_A - (CONV_A - 1) + k
            acc_row(RG_CONVA + k, dcv * pbuf[off : off + tile, :])
            back = CONV_A - 1 - k
            dp = dp + sw_ref[RW_CONVA + k : RW_CONVA + k + 1, :] * dcvbuf[back : back + tile, :]
        dproj_ref[:, C_CA : C_CA + D_G] = (dp * xa).astype(BF16)
        dproj_ref[:, C_XA : C_XA + D_G] = (dp * ca).astype(BF16)
        dcvbuf[tile : tile + HALO_A, :] = dcvbuf[0:HALO_A, :]

        dyb = dy[:, D_G : 2 * D_G]
        u = proj_ref[:, C_U : C_U + D_G]
        v = proj_ref[:, C_V : C_V + D_G]
        ua, ut = _gelu(u)
        vg, vt = _gelu(v)
        vhat, vrstd = _normalize(vg)
        sg_g = sw_ref[RW_VEC : RW_VEC + 1, :]
        vn = vhat * sg_g + sw_ref[RW_VEC + 1 : RW_VEC + 2, :]
        tril = _tril_mask_cat()
        wcat_b = jnp.where(tril, wcat_ref[...], 0.0).astype(BF16)
        wcat_tb = jnp.where(_triu_mask_cat(), wcat_t_ref[...], 0.0).astype(BF16)
        sgb = sw_ref[RW_SGB : RW_SGB + CHUNK, :]
        dmixed = dyb * ua
        dvn_parts = []
        du_parts = []
        dwcat = jnp.zeros((CHUNK, N_SUB * CHUNK), F32)
        dsgb = jnp.zeros((CHUNK, D_G), F32)
        for j in range(tile // CHUNK):
            rows = slice(j * CHUNK, (j + 1) * CHUNK)
            mixed, vbd = _sgu_mix(vn[rows], wcat_b, sgb, masks)
            du_parts.append(dyb[rows] * mixed)
            dmx = dmixed[rows]
            dsgb = dsgb + dmx
            dwcat = dwcat + _dot_nt(dmx.astype(BF16), vbd)
            dvn_parts.append(_dot(wcat_tb, _stack_heads(dmx, masks).astype(BF16)))
        dwcat = jnp.where(tril, dwcat, 0.0)
        sg_ref[RG_SGW : RG_SGW + CHUNK, :] += dwcat[:, 0:D_G]
        sg_ref[RG_SGW + CHUNK : RG_SGW + 2 * CHUNK, :] += dwcat[:, D_G:]
        sg_ref[RG_SGB : RG_SGB + CHUNK, :] += dsgb
        dvn = jnp.concatenate(dvn_parts, axis=0)
        du_act = jnp.concatenate(du_parts, axis=0)
        acc_row(RG_VEC, dvn * vhat)
        acc_row(RG_VEC + 1, dvn)
        dvg = _normalize_grad(dvn * sg_g, vhat, vrstd)
        dproj_ref[:, C_U : C_U + D_G] = (du_act * _gelu_grad(u, ut)).astype(BF16)
        dproj_ref[:, C_V : C_V + D_G] = (dvg * _gelu_grad(v, vt)).astype(BF16)

        dyc = dy[:, 2 * D_G : 3 * D_G]
        inv_cnt = _pool_inv_count(ti * tile, tile)
        pmb = aux_ref[:, AUX_PM : AUX_PM + D_G].astype(BF16)
        pool_b = pool_ref[...]
        scale = sw_ref[RW_VEC + 2 : RW_VEC + 3, :]
        acc_row(RG_VEC + 2, dyc * _dot(pmb, pool_b))
        dpre = (dyc * scale).astype(BF16)
        sg_ref[RG_POOL : RG_POOL + D_G, :] += _dot_tn(pmb, dpre)
        dpm = _dot_nt(dpre, pool_b)
        r = dpm * inv_cnt
        dproj_ref[:, C_XC : C_XC + D_G] = (_leading_window_sum(r, rhalo[...]) - dpm).astype(BF16)
        rhalo[...] = r[0:HALO_C, :]

        dyd = dy[:, 3 * D_G : 4 * D_G]
        da = proj_ref[:, C_DA : C_DA + D_G]
        sgd = _sigmoid(proj_ref[:, C_DG : C_DG + D_G])
        gbuf[0, 0:HALO_D, :] = halo_ref[:, C_DA : C_DA + D_G] * _sigmoid(halo_ref[:, C_DG : C_DG + D_G]) * has_past
        gbuf[0, HALO_D : HALO_D + tile, :] = da * sgd
        _fill_shifts(gbuf)
        nhat, nrstd = _normalize(aux_ref[:, AUX_CVD : AUX_CVD + D_G])
        cc_g = sw_ref[RW_VEC + 4 : RW_VEC + 5, :]
        nrm = nhat * cc_g + sw_ref[RW_VEC + 5 : RW_VEC + 6, :]
        sgn = _sigmoid(nrm)
        dydb = dyd.astype(BF16)
        dpw_ref[...] += _dot_tn((nrm * sgn).astype(BF16), dydb)
        dn = _dot_nt(dydb, pw_ref[...]) * (sgn * (1.0 + nrm * (1.0 - sgn)))
        acc_row(RG_VEC + 4, dn * nhat)
        acc_row(RG_VEC + 5, dn)
        dcvd = _normalize_grad(dn * cc_g, nhat, nrstd)
        acc_row(RG_VEC + 3, dcvd)
        dgbuf[0, 0:tile, :] = dcvd
        _fill_shifts(dgbuf)
        for r0 in range(0, tile, STRIP):
            d_s = dgbuf[0, r0 : r0 + STRIP, :]
            dg = jnp.zeros((STRIP, D_G), F32)
            for k in range(CONV_D):
                off = HALO_D - (CONV_D - 1) + k
                prod = d_s * _shifted(gbuf, off + r0, STRIP)
                part = prod[0:8]
                for q in range(8, STRIP, 8):
                    part = part + prod[q : q + 8]
                dwacc[8 * k : 8 * k + 8, :] += part
                back = CONV_D - 1 - k
                dg = dg + sw_ref[RW_DW + k : RW_DW + k + 1, :] * _shifted(dgbuf, back + r0, STRIP)
            da_s = proj_ref[r0 : r0 + STRIP, C_DA : C_DA + D_G]
            sgd_s = _sigmoid(proj_ref[r0 : r0 + STRIP, C_DG : C_DG + D_G])
            dproj_ref[r0 : r0 + STRIP, C_DA : C_DA + D_G] = (dg * sgd_s).astype(BF16)
            dproj_ref[r0 : r0 + STRIP, C_DG : C_DG + D_G] = (dg * da_s * sgd_s * (1.0 - sgd_s)).astype(BF16)
        dgbuf[0, tile : tile + HALO_D, :] = dgbuf[0, 0:HALO_D, :]

        @pl.when(i == n_tiles - 1)
        def _():
            for k in range(CONV_D):
                sg_ref[RG_DW + k : RG_DW + k + 1, :] = jnp.sum(dwacc[8 * k : 8 * k + 8, :], axis=0, keepdims=True)

        dyeb = dy[:, 4 * D_G : 5 * D_G].astype(BF16)
        qb = proj_ref[:, C_Q : C_Q + D_G].astype(BF16)
        kst_b = kst_ref[...]
        p_all = aux_ref[:, AUX_P:]
        dvst_ref[...] += _dot_tn(p_all.astype(BF16), dyeb)
        dp_all = _dot_nt(dyeb, vst_ref[...])
        ds = []
        for h in range(N_SUB):
            blk = slice(h * MEM_LEN, (h + 1) * MEM_LEN)
            p, dpb = p_all[:, blk], dp_all[:, blk]
            ds.append(p * (dpb - jnp.sum(dpb * p, axis=-1, keepdims=True)))
        dsb = (jnp.concatenate(ds, axis=-1) * ATT_SCALE).astype(BF16)
        dproj_ref[:, C_Q : C_Q + D_G] = _dot(dsb, kst_b).astype(BF16)
        dkst_ref[...] += _dot_tn(dsb, qb)

    def rows(width):
        return pl.BlockSpec((tile, width), lambda i: (n_tiles - 1 - i, 0))

    halo_spec = pl.BlockSpec((HALO_D, D_IN), lambda i: (jnp.maximum((n_tiles - 1 - i) * halo_blocks - 1, 0), 0))
    weights = [kst, vst, wout, sw, wcat, wcat_t, poolw, pww, ln]
    acc_shapes = [(D_MIX, D_MODEL), (N_SUB * MEM_LEN, D_G), (N_SUB * MEM_LEN, D_G), (D_G, D_G), (RG_ROWS, D_G)]
    return _gridded_call(
        body,
        name=name,
        steps=n_tiles,
        in_specs=[rows(D_MODEL), rows(D_MODEL), rows(D_IN), halo_spec, rows(D_MIX), rows(AUX_COLS)]
        + [_full(a.shape) for a in weights],
        out_specs=[rows(D_IN), rows(D_MODEL)] + [_full(s) for s in acc_shapes],
        out_shape=[jax.ShapeDtypeStruct((seq, D_IN), BF16), jax.ShapeDtypeStruct((seq, D_MODEL), F32)]
        + [jax.ShapeDtypeStruct(s, F32) for s in acc_shapes],
        scratch_shapes=[
            pltpu.VMEM((HALO_A + tile, D_G), F32),
            pltpu.VMEM((tile + HALO_A, D_G), F32),
            pltpu.VMEM((HALO_C, D_G), F32),
            pltpu.VMEM((SHIFTS, HALO_D + tile, D_G), F32),
            pltpu.VMEM((SHIFTS, tile + HALO_D, D_G), F32),
            pltpu.VMEM((8 * CONV_D, D_G), F32),
        ],
        operands=[dxn, z, proj, proj, y, cvd, *weights],
        exchange=exchange,
    )


def _kv_forward(mem, wkv, *, name):
    def body(mem_ref, wkv_ref, kst_ref, vst_ref):
        kv = _dot(mem_ref[...].astype(BF16), wkv_ref[...])
        masks = _head_masks()
        kst_ref[...] = _stack_heads(kv[:, 0:D_G], masks).astype(BF16)
        vst_ref[...] = _stack_heads(kv[:, D_G:], masks).astype(BF16)

    shape = jax.ShapeDtypeStruct((N_SUB * MEM_LEN, D_G), BF16)
    return pl.pallas_call(body, name=name, out_shape=[shape, shape])(mem, wkv)


def _kv_backward(mem, dkst, dvst, *, name):
    def body(mem_ref, dkst_ref, dvst_ref, dwkv_ref):
        masks = _head_masks()
        memb = mem_ref[...].astype(BF16)
        for col, ref in ((0, dkst_ref), (D_G, dvst_ref)):
            d = jnp.zeros((MEM_LEN, D_G), F32)
            for h in range(N_SUB):
                d = d + ref[h * MEM_LEN : (h + 1) * MEM_LEN, :] * masks[h]
            dwkv_ref[:, col : col + D_G] = _dot_tn(memb, d.astype(BF16))

    return pl.pallas_call(body, name=name, out_shape=jax.ShapeDtypeStruct((D_MODEL, 2 * D_G), F32))(mem, dkst, dvst)


def _input_grad(dproj, dz, win, *, name, exchange=None):
    seq = dproj.shape[0]
    tile = min(MM_TILE // 2, seq)

    def body(dproj_ref, dz_ref, win_ref, dx_ref):
        acc = ALPHA * dz_ref[...]
        for k in range(N_CHIPS):
            acc = acc + _dot_nt(dproj_ref[:, k * W_IN_SHARD : (k + 1) * W_IN_SHARD], win_ref[k])
        dx_ref[...] = acc

    return _gridded_call(
        body,
        name=name,
        steps=seq // tile,
        in_specs=[
            pl.BlockSpec((tile, D_IN), lambda i: (i, 0)),
            pl.BlockSpec((tile, D_MODEL), lambda i: (i, 0)),
            _full(win.shape),
        ],
        out_specs=[pl.BlockSpec((tile, D_MODEL), lambda i: (i, 0))],
        out_shape=[jax.ShapeDtypeStruct((seq, D_MODEL), F32)],
        scratch_shapes=[],
        operands=[dproj, dz, win],
        exchange=exchange,
    )


def _input_weight_grad(x, dproj, *, name, exchange=None):
    seq = x.shape[0]
    tile = min(MM_TILE, seq)
    n_rows = seq // tile

    def body(x_ref, dproj_ref, dwin_ref):
        @pl.when(pl.program_id(0) % n_rows == 0)
        def _():
            dwin_ref[...] = jnp.zeros(dwin_ref.shape, F32)

        dwin_ref[0] += _dot_tn(x_ref[...].astype(BF16), dproj_ref[...])

    return _gridded_call(
        body,
        name=name,
        steps=N_CHIPS * n_rows,
        in_specs=[
            pl.BlockSpec((tile, D_MODEL), lambda s: (s % n_rows, 0)),
            pl.BlockSpec((tile, W_IN_SHARD), lambda s: (s % n_rows, s // n_rows)),
        ],
        out_specs=[pl.BlockSpec((1, D_MODEL, W_IN_SHARD), lambda s: (s // n_rows, 0, 0))],
        out_shape=[jax.ShapeDtypeStruct((N_CHIPS, D_MODEL, W_IN_SHARD), F32)],
        scratch_shapes=[],
        operands=[x, dproj],
        exchange=exchange,
    )


def _expand_sgb(sg_b):
    return jnp.repeat(sg_b.T, HEAD_DIM, axis=1)


def _pack_small_weights(sg_ln_g, sg_ln_b, pool_scale, cc_dw_b, cc_ln_g, cc_ln_b, conv_a_w, cc_dw_w, sg_b):
    vec = jnp.stack([sg_ln_g, sg_ln_b, pool_scale, cc_dw_b, cc_ln_g, cc_ln_b])
    return jnp.concatenate(
        [
            jnp.pad(vec, ((0, RW_CONVA - RW_VEC - 6), (0, 0))),
            jnp.pad(conv_a_w, ((0, RW_DW - RW_CONVA - CONV_A), (0, 0))),
            jnp.pad(cc_dw_w, ((0, RW_SGB - RW_DW - CONV_D), (0, 0))),
            _expand_sgb(sg_b),
        ]
    )


def _sg_w_cat(sg_w):
    cat = jnp.transpose(sg_w, (1, 0, 2)).reshape(CHUNK, N_SUB * CHUNK)
    cat_t = jnp.transpose(sg_w, (2, 0, 1)).reshape(CHUNK, N_SUB * CHUNK)
    return cat, cat_t


def _pool_block_diag(pool_w):
    tiled = jnp.tile(pool_w.reshape(D_G, HEAD_DIM), (1, N_SUB))
    row = lax.broadcasted_iota(jnp.int32, (D_G, D_G), 0) // HEAD_DIM
    col = lax.broadcasted_iota(jnp.int32, (D_G, D_G), 1) // HEAD_DIM
    return jnp.where(row == col, tiled, 0.0)


def _prepare_layer(mem, w, l):
    cat, cat_t = _sg_w_cat(w["sg_w"])
    kst, vst = _kv_forward(mem, w["w_kv"], name=f"kv_fwd{l}")
    return dict(
        win=w["w_in"],
        wout=w["w_out"],
        pww=w["cc_pw_w"],
        sw=_pack_small_weights(
            w["sg_ln_g"], w["sg_ln_b"], w["pool_scale"], w["cc_dw_b"], w["cc_ln_g"], w["cc_ln_b"],
            w["conv_a_w"], w["cc_dw_w"], w["sg_b"],
        ),
        wcat=cat,
        wcat_t=cat_t,
        poolw=_pool_block_diag(w["pool_w"]).astype(BF16),
        ln=jnp.stack([w["ln_g"], w["ln_b"]]),
        kst=kst,
        vst=vst,
    )


def _forward(l, h, p, tgt, exchange=None):
    return _layer_forward(
        h, p["win"], p["kst"], p["vst"], p["wout"], p["sw"], p["wcat"], p["poolw"], p["pww"], p["ln"], tgt,
        name=f"layer_fwd{l}", exchange=exchange,
    )


def _backward(l, dxn, s, p, exchange=None):
    return _layer_backward(
        dxn, s[2], s[0], s[1], s[4], p["kst"], p["vst"], p["wout"], p["sw"], p["wcat"], p["wcat_t"], p["poolw"],
        p["pww"], p["ln"], name=f"layer_bwd{l}", exchange=exchange,
    )


def _place():
    x, y, c = lax.axis_index("x"), lax.axis_index("y"), lax.axis_index("c")
    others = [(1 - x, y), (x, 1 - y), (1 - x, 1 - y)]
    return x, y, c, others


def _half(ref, c, axis):
    n = ref.shape[axis] // 2
    if axis == 0:
        return ref.at[pl.ds(c * n, n)]
    return ref.at[:, pl.ds(c * n, n)]


def _place_own_block(place, stacked, layer, dtypes, *, name):
    n = len(stacked)

    def body(place_ref, *refs):
        for a in range(n):
            refs[n + a][...] = refs[a][...].astype(dtypes[a])

    def block(s):
        return (1,) + s.shape[1:]

    return pl.pallas_call(
        body,
        name=name,
        grid_spec=pltpu.PrefetchScalarGridSpec(
            num_scalar_prefetch=1,
            grid=(1,),
            in_specs=[pl.BlockSpec(block(s), lambda i, place_ref: (layer, 0, 0)) for s in stacked],
            out_specs=[pl.BlockSpec(block(s), lambda i, place_ref: (place_ref[1], 0, 0)) for s in stacked],
        ),
        out_shape=[jax.ShapeDtypeStruct((N_CHIPS,) + s.shape[1:], dt) for s, dt in zip(stacked, dtypes)],
        compiler_params=pltpu.CompilerParams(dimension_semantics=("arbitrary",), vmem_limit_bytes=VMEM_LIMIT),
    )(place, *stacked)


def _sds(a):
    return jax.ShapeDtypeStruct(a.shape, a.dtype)


def _gather_exchange(bufs):
    n = len(bufs)

    def remote(sems, block, k, to):
        return pltpu.make_async_remote_copy(
            src_ref=block, dst_ref=block, send_sem=sems[0].at[k], recv_sem=sems[1].at[k], device_id=to, device_id_type=MESH
        )

    def before(step, steps, refs, outs, sems):
        def send():
            x, y, c, others = _place()
            for j, (px, py) in enumerate(others):
                for a in range(n):
                    remote(sems, _half(refs[a].at[2 * x + y], c, 0), 3 * a + j, (px, py, c)).start()

        _when(step == 0, send)

    def after(step, steps, refs, outs, sems):
        def pass_on():
            x, y, c, others = _place()
            for j, (px, py) in enumerate(others):
                for a in range(n):
                    landed = _half(refs[a].at[2 * px + py], c, 0)
                    remote(sems, landed, 3 * a + j, (px, py, c)).wait_recv()
                    remote(sems, landed, 3 * n + 3 * a + j, (x, y, 1 - c)).start()

        def finish():
            x, y, c, others = _place()
            for j, (px, py) in enumerate(others):
                for a in range(n):
                    remote(sems, _half(refs[a].at[2 * px + py], 1 - c, 0), 3 * n + 3 * a + j, (x, y, 1 - c)).wait_recv()
            for a in range(n):
                mine = _half(refs[a].at[2 * x + y], c, 0)
                for k in range(3 * a, 3 * a + 3):
                    remote(sems, mine, k, (x, y, 1 - c)).wait_send()
                    remote(sems, mine, 3 * n + k, (x, y, 1 - c)).wait_send()

        _when(step == (3 * steps) // 4, pass_on)
        _when(step == steps - 1, finish)

    return _Exchange(bufs, [(_sds(b), a) for a, b in enumerate(bufs)], [6 * n, 6 * n], before, after)


def _swap_exchange(grads):
    n = len(grads)

    def copy(refs, outs, sems, a):
        x, y, c, _ = _place()
        return pltpu.make_async_remote_copy(
            src_ref=_half(refs[a], 1 - c, 1), dst_ref=outs[a], send_sem=sems[0].at[a], recv_sem=sems[1].at[a],
            device_id=(x, y, 1 - c), device_id_type=MESH,
        )

    def before(step, steps, refs, outs, sems):
        _when(step == 0, lambda: [copy(refs, outs, sems, a).start() for a in range(n)] and None)

    def after(step, steps, refs, outs, sems):
        _when(step == steps - 1, lambda: [copy(refs, outs, sems, a).wait() for a in range(n)] and None)

    outputs = [(jax.ShapeDtypeStruct((N_CHIPS, g.shape[1] // 2, g.shape[2]), g.dtype), None) for g in grads]
    return _Exchange(grads, outputs, [n, n], before, after)


def _add_sibling_half(place, grads, received, wire, *, name):
    n = len(grads)

    def body(place_ref, *refs):
        k = pl.program_id(0)
        for a in range(n):
            pair = (refs[a][...] + refs[n + a][...]).astype(wire[a])
            refs[2 * n + a][...] = pair

            @pl.when(k == place_ref[1])
            def _(a=a, pair=pair):
                refs[3 * n + a][...] = pair

    def block(g):
        return (1, g.shape[1] // 2, g.shape[2])

    return pl.pallas_call(
        body,
        name=name,
        grid_spec=pltpu.PrefetchScalarGridSpec(
            num_scalar_prefetch=1,
            grid=(N_CHIPS,),
            in_specs=[pl.BlockSpec(block(g), lambda k, place_ref: (k, place_ref[0], 0)) for g in grads]
            + [pl.BlockSpec(block(g), lambda k, place_ref: (k, 0, 0)) for g in grads],
            out_specs=[pl.BlockSpec(block(g), lambda k, place_ref: (k, 0, 0)) for g in grads]
            + [pl.BlockSpec(block(g), lambda k, place_ref: (place_ref[1], 0, 0)) for g in grads],
        ),
        out_shape=[jax.ShapeDtypeStruct(r.shape, dt) for r, dt in zip(received, wire)] * 2,
        compiler_params=pltpu.CompilerParams(dimension_semantics=("arbitrary",), vmem_limit_bytes=VMEM_LIMIT),
    )(place, *grads, *received)


def _scatter_exchange(pairs, landing):
    n = len(pairs)

    def copy(refs, sems, a, j, px, py):
        x, y, c, _ = _place()
        return pltpu.make_async_remote_copy(
            src_ref=refs[a].at[2 * px + py], dst_ref=refs[n + a].at[2 * x + y], send_sem=sems[0].at[3 * a + j],
            recv_sem=sems[1].at[3 * a + j], device_id=(px, py, c), device_id_type=MESH,
        )

    def before(step, steps, refs, outs, sems):
        def send():
            for j, (px, py) in enumerate(_place()[3]):
                for a in range(n):
                    copy(refs, sems, a, j, px, py).start()

        _when(step == 0, send)

    def after(step, steps, refs, outs, sems):
        def finish():
            x, y, c, others = _place()
            for j, (px, py) in enumerate(others):
                for a in range(n):
                    landed = refs[n + a].at[2 * px + py]
                    pltpu.make_async_remote_copy(
                        src_ref=landed, dst_ref=landed, send_sem=sems[0].at[3 * a + j], recv_sem=sems[1].at[3 * a + j],
                        device_id=(px, py, c), device_id_type=MESH,
                    ).wait_recv()
            for j, (px, py) in enumerate(others):
                for a in range(n):
                    copy(refs, sems, a, j, px, py).wait_send()

        _when(step == steps - 1, finish)

    return _Exchange(pairs + landing, [(_sds(b), n + a) for a, b in enumerate(landing)], [3 * n, 3 * n], before, after)


SUM_STEPS = 2


def _sum_chip_blocks(place, parts, keep_chip_axis, *, name):
    n = len(parts)

    def body(place_ref, *refs):
        for a in range(n):
            p = refs[a]
            total = (p[0].astype(F32) + p[1].astype(F32)) + (p[2].astype(F32) + p[3].astype(F32))
            if keep_chip_axis[a]:
                refs[n + a][0] = total
            else:
                refs[n + a][...] = total

    def in_spec(p):
        return pl.BlockSpec((N_CHIPS, p.shape[1] // SUM_STEPS, p.shape[2]), lambda i, place_ref: (0, i, 0))

    def out_spec(p, keep):
        rows = p.shape[1] // SUM_STEPS
        if keep:
            return pl.BlockSpec((1, rows, p.shape[2]), lambda i, place_ref: (place_ref[1], place_ref[0] * SUM_STEPS + i, 0))
        return pl.BlockSpec((rows, p.shape[2]), lambda i, place_ref: (place_ref[0] * SUM_STEPS + i, 0))

    def out_shape(p, keep):
        shape = (2 * p.shape[1], p.shape[2])
        return jax.ShapeDtypeStruct((N_CHIPS,) + shape if keep else shape, F32)

    return pl.pallas_call(
        body,
        name=name,
        grid_spec=pltpu.PrefetchScalarGridSpec(
            num_scalar_prefetch=1,
            grid=(SUM_STEPS,),
            in_specs=[in_spec(p) for p in parts],
            out_specs=[out_spec(p, k) for p, k in zip(parts, keep_chip_axis)],
        ),
        out_shape=[out_shape(p, k) for p, k in zip(parts, keep_chip_axis)],
        compiler_params=pltpu.CompilerParams(dimension_semantics=("arbitrary",), vmem_limit_bytes=VMEM_LIMIT),
    )(place, *parts)


def _join_exchange(bufs, keep_chip_axis):
    n = len(bufs)
    kept = [a for a in range(n) if keep_chip_axis[a]]
    base = n

    def copy(refs, sems, block, k, to):
        return pltpu.make_async_remote_copy(
            src_ref=block, dst_ref=block, send_sem=sems[0].at[k], recv_sem=sems[1].at[k], device_id=to, device_id_type=MESH
        )

    def mine(refs, a, cc):
        x, y, _, _ = _place()
        return _half(refs[a].at[2 * x + y] if keep_chip_axis[a] else refs[a], cc, 0)

    def before(step, steps, refs, outs, sems):
        def send():
            x, y, c, others = _place()
            for a in range(n):
                copy(refs, sems, mine(refs, a, c), a, (x, y, 1 - c)).start()
            for i, a in enumerate(kept):
                for j, (px, py) in enumerate(others):
                    copy(refs, sems, mine(refs, a, c), base + 6 * i + j, (px, py, c)).start()

        _when(step == 0, send)

    def after(step, steps, refs, outs, sems):
        def pass_on():
            x, y, c, others = _place()
            for i, a in enumerate(kept):
                for j, (px, py) in enumerate(others):
                    landed = _half(refs[a].at[2 * px + py], c, 0)
                    copy(refs, sems, landed, base + 6 * i + j, (px, py, c)).wait_recv()
                    copy(refs, sems, landed, base + 6 * i + 3 + j, (x, y, 1 - c)).start()

        def finish():
            x, y, c, others = _place()
            for a in range(n):
                copy(refs, sems, mine(refs, a, 1 - c), a, (x, y, 1 - c)).wait_recv()
            for i, a in enumerate(kept):
                for j, (px, py) in enumerate(others):
                    passed = _half(refs[a].at[2 * px + py], 1 - c, 0)
                    copy(refs, sems, passed, base + 6 * i + 3 + j, (x, y, 1 - c)).wait_recv()
            for a in range(n):
                copy(refs, sems, mine(refs, a, c), a, (x, y, 1 - c)).wait_send()
            for i, a in enumerate(kept):
                for k in range(6):
                    copy(refs, sems, mine(refs, a, c), base + 6 * i + k, (x, y, 1 - c)).wait_send()

        _when(step == steps // 2, pass_on)
        _when(step == steps - 1, finish)

    return _Exchange(bufs, [(_sds(b), a) for a, b in enumerate(bufs)], [n + 6 * len(kept)] * 2, before, after)


def _adamw(w, g, m, v):
    m = ADAM_B1 * m + (1.0 - ADAM_B1) * g
    v = ADAM_B2 * v + (1.0 - ADAM_B2) * (g * g)
    m_hat = m / (1.0 - ADAM_B1**ADAM_STEP)
    v_hat = v / (1.0 - ADAM_B2**ADAM_STEP)
    delta = -ADAM_LR * (m_hat / (jnp.sqrt(v_hat) + ADAM_EPS) + ADAM_WD * w)
    return delta, m, v


def _adamw_large(w, m, v, layer_grads, *, name):
    depth, rows, cols = w.shape
    tile = math.gcd(rows, ADAM_TILE)
    assert tile % 8 == 0

    def body(w_ref, m_ref, v_ref, *refs):
        g_refs, (g_out, d_out, m_out, v_out) = refs[:depth], refs[depth:]
        for l in range(depth):

            @pl.when(pl.program_id(0) == l)
            def _(l=l):
                g = g_refs[l][...]
                delta, m_new, v_new = _adamw(w_ref[0], g, m_ref[0], v_ref[0])
                g_out[0], d_out[0], m_out[0], v_out[0] = g, delta, m_new, v_new

    def stacked():
        return pl.BlockSpec((1, tile, cols), lambda l, i: (l, i, 0))

    def layer_spec(l):
        return pl.BlockSpec((tile, cols), lambda k, i: (jnp.where(k == l, i, 0), 0))

    shape = jax.ShapeDtypeStruct(w.shape, F32)
    return pl.pallas_call(
        body,
        name=name,
        grid=(depth, rows // tile),
        in_specs=[stacked(), stacked(), stacked()] + [layer_spec(l) for l in range(depth)],
        out_specs=[stacked()] * 4,
        out_shape=[shape] * 4,
        compiler_params=pltpu.CompilerParams(dimension_semantics=("arbitrary", "arbitrary"), vmem_limit_bytes=VMEM_LIMIT),
    )(w, m, v, *layer_grads)


def _packed_pieces(name, l):
    vecs = ("sg_ln_g", "sg_ln_b", "pool_scale", "cc_dw_b", "cc_ln_g", "cc_ln_b")
    if name in vecs:
        r = RG_VEC + vecs.index(name)
        return [((slice(l, l + 1), slice(None)), slice(r, r + 1), slice(None))]
    if name in ("ln_g", "ln_b"):
        r = RG_LN + (4 if name == "ln_b" else 0)
        return [((slice(l, l + 1), slice(j * D_G, (j + 1) * D_G)), slice(r + j, r + j + 1), slice(None)) for j in range(4)]
    assert name == "sg_w"
    return [
        ((l, h), slice(RG_SGW + CHUNK * (h // 2), RG_SGW + CHUNK * (h // 2 + 1)), slice(CHUNK * (h % 2), CHUNK * (h % 2 + 1)))
        for h in range(N_SUB)
    ]


PACKED_NAMES = ("sg_ln_g", "sg_ln_b", "pool_scale", "cc_dw_b", "cc_ln_g", "cc_ln_b", "ln_g", "ln_b", "sg_w")


def _adamw_packed(packed, ws, ms, vs, *, name):
    n, depth = len(ws), len(packed)

    def body(*refs):
        packed_refs, refs = refs[:depth], refs[depth:]
        for a, leaf in enumerate(PACKED_NAMES):
            for l in range(depth):
                for at, rows, cols in _packed_pieces(leaf, l):
                    g = packed_refs[l][rows, cols]
                    delta, m_new, v_new = _adamw(refs[a][at], g, refs[n + a][at], refs[2 * n + a][at])
                    refs[3 * n + a][at] = g
                    refs[4 * n + a][at] = delta
                    refs[5 * n + a][at] = m_new
                    refs[6 * n + a][at] = v_new

    shapes = [jax.ShapeDtypeStruct(w.shape, F32) for w in ws]
    outs = pl.pallas_call(body, name=name, out_shape=shapes * 4)(*packed, *ws, *ms, *vs)
    return outs[:n], outs[n : 2 * n], outs[2 * n : 3 * n], outs[3 * n :]


def _adamw_small(ws, gs, ms, vs, *, name):
    n = len(ws)

    def body(*refs):
        for a in range(n):
            delta, m_new, v_new = _adamw(refs[a][...], refs[n + a][...], refs[2 * n + a][...], refs[3 * n + a][...])
            refs[4 * n + a][...] = delta
            refs[5 * n + a][...] = m_new
            refs[6 * n + a][...] = v_new

    shapes = [jax.ShapeDtypeStruct(w.shape, F32) for w in ws]
    outs = pl.pallas_call(body, name=name, out_shape=shapes * 3)(*ws, *gs, *ms, *vs)
    return outs[:n], outs[n : 2 * n], outs[2 * n :]


WEIGHT_NAMES = (
    "w_in", "conv_a_w", "sg_ln_g", "sg_ln_b", "sg_w", "sg_b", "pool_w", "pool_scale", "cc_dw_w", "cc_dw_b", "cc_ln_g",
    "cc_ln_b", "cc_pw_w", "w_kv", "w_out", "ln_g", "ln_b",
)
LARGE = ("w_in", "cc_pw_w", "w_kv", "w_out")
TAPS_ROWS = 48


def _unpack_small_grads(small, chip):
    out = {}
    for r, k in enumerate(("sg_ln_g", "sg_ln_b", "pool_scale", "cc_dw_b", "cc_ln_g", "cc_ln_b")):
        out[k] = small[RG_VEC + r]
    out["conv_a_w"] = lax.dynamic_slice_in_dim(small[RG_CONVA : RG_CONVA + CONV_A], chip * HEAD_DIM, HEAD_DIM, axis=1)
    out["cc_dw_w"] = lax.dynamic_slice_in_dim(small[RG_DW : RG_DW + CONV_D], chip * HEAD_DIM, HEAD_DIM, axis=1)
    cat = jnp.concatenate([small[RG_SGW : RG_SGW + CHUNK], small[RG_SGW + CHUNK : RG_SGW + 2 * CHUNK]], axis=1)
    out["sg_w"] = jnp.transpose(cat.reshape(CHUNK, N_SUB, CHUNK), (1, 0, 2))
    out["sg_b"] = small[RG_SGB : RG_SGB + CHUNK].reshape(CHUNK, N_SUB, HEAD_DIM).sum(-1).T
    pool = small[RG_POOL : RG_POOL + D_G]
    out["pool_w"] = jnp.stack(
        [pool[g * HEAD_DIM : (g + 1) * HEAD_DIM, g * HEAD_DIM : (g + 1) * HEAD_DIM] for g in range(N_SUB)]
    )
    out["ln_g"] = small[RG_LN : RG_LN + 4].reshape(D_MODEL)
    out["ln_b"] = small[RG_LN + 4 : RG_LN + 8].reshape(D_MODEL)
    return out


def kernel(x, mem, w_in, conv_a_w, sg_ln_g, sg_ln_b, sg_w, sg_b, pool_w, pool_scale, cc_dw_w, cc_dw_b, cc_ln_g, cc_ln_b, cc_pw_w, w_kv, w_out, ln_g, ln_b, loss_target, m_w_in, m_conv_a_w, m_sg_ln_g, m_sg_ln_b, m_sg_w, m_sg_b, m_pool_w, m_pool_scale, m_cc_dw_w, m_cc_dw_b, m_cc_ln_g, m_cc_ln_b, m_cc_pw_w, m_w_kv, m_w_out, m_ln_g, m_ln_b, v_w_in, v_conv_a_w, v_sg_ln_g, v_sg_ln_b, v_sg_w, v_sg_b, v_pool_w, v_pool_scale, v_cc_dw_w, v_cc_dw_b, v_cc_ln_g, v_cc_ln_b, v_cc_pw_w, v_w_kv, v_w_out, v_ln_g, v_ln_b):
    given = dict(locals())
    weights = {k: given[k] for k in WEIGHT_NAMES}
    chip = 2 * lax.axis_index("x") + lax.axis_index("y")
    place = jnp.stack([lax.axis_index("c"), chip]).astype(jnp.int32)

    x0, mem0 = x[0], mem[0]

    taps = jnp.concatenate([conv_a_w, cc_dw_w], axis=1)
    taps = jnp.pad(taps, ((0, 0), (0, TAPS_ROWS - taps.shape[1]), (0, 0)))

    def own_blocks(l):
        return _place_own_block(
            place, [w_in, w_out, w_kv, cc_pw_w, taps], l, [BF16, BF16, BF16, BF16, F32], name=f"place_weights{l}"
        )

    def layer_operands(l, gathered):
        g_in, g_out, g_kv, g_pw, g_taps = gathered
        taps_full = jnp.transpose(g_taps, (1, 0, 2)).reshape(TAPS_ROWS, D_G)
        full = dict(
            w_in=g_in,
            w_out=g_out.reshape(D_MIX, D_MODEL),
            w_kv=g_kv.reshape(D_MODEL, 2 * D_G),
            cc_pw_w=g_pw.reshape(D_G, D_G),
            conv_a_w=taps_full[0:CONV_A],
            cc_dw_w=taps_full[CONV_A : CONV_A + CONV_D],
            **{k: weights[k][l] for k in WEIGHT_NAMES if k not in LARGE + ("conv_a_w", "cc_dw_w")},
        )
        return _prepare_layer(mem0, full, l)

    def other_grads(l, bwd, small):
        _, _, dwout, dkst, dvst, dpw, _ = bwd
        return [
            dwout.reshape(N_CHIPS, D_MIX // N_CHIPS, D_MODEL),
            _kv_backward(mem0, dkst, dvst, name=f"kv_bwd{l}").reshape(N_CHIPS, D_MODEL // N_CHIPS, 2 * D_G),
            dpw.reshape(N_CHIPS, D_G // N_CHIPS, D_G),
            small.reshape(N_CHIPS, RG_ROWS // N_CHIPS, D_G),
        ]

    n_red = 5
    keep = [False, False, False, False, True]

    wire = [BF16, BF16, BF16, BF16, F32]

    def reduced_layer(joined):
        r_in, r_out, r_kv, r_pw, small_all = joined
        packed = small_all.reshape(RG_ROWS, D_G)
        out = _unpack_small_grads(packed, chip)
        out.update(w_in=r_in, w_out=r_out, w_kv=r_kv, cc_pw_w=r_pw, packed=packed)
        return out

    blocks0, blocks1 = own_blocks(0), own_blocks(1)
    p0 = layer_operands(0, _run_exchange(_gather_exchange(blocks0), name="gather_weights0"))
    fwd0, gathered1 = _forward(0, x0, p0, None, exchange=_gather_exchange(blocks1))
    p1 = layer_operands(1, gathered1)
    x1 = fwd0[3]
    fwd1, _ = _forward(1, x1, p1, loss_target[0])

    bwd1, _ = _backward(1, fwd1[3], fwd1, p1)
    small1 = bwd1[6].at[RG_LOSS, :].set(fwd1[5][0, 0])
    (dwin1,), _ = _input_weight_grad(x1, bwd1[0], name="w_in_grad1")
    grads1 = [dwin1] + other_grads(1, bwd1, small1)
    (dx1,), received1 = _input_grad(bwd1[0], bwd1[1], p1["win"], name="input_grad1", exchange=_swap_exchange(grads1))
    pairs1 = _add_sibling_half(place, grads1, received1, wire, name="rs_pair1")
    bwd0, parts1 = _backward(0, dx1, fwd0, p0, exchange=_scatter_exchange(pairs1[:n_red], pairs1[n_red:]))
    halves1 = _sum_chip_blocks(place, parts1, keep, name="rs_sum1")
    rest0 = other_grads(0, bwd0, bwd0[6])
    n_rest = len(rest0)
    pairs0_rest = _add_sibling_half(
        place, rest0, _run_exchange(_swap_exchange(rest0), name="rs_swap0_rest"), wire[1:], name="rs_pair0_rest"
    )
    (dwin0,), carried = _input_weight_grad(
        x0, bwd0[0], name="w_in_grad0",
        exchange=_both(_join_exchange(halves1, keep), _scatter_exchange(pairs0_rest[:n_rest], pairs0_rest[n_rest:])),
    )
    joined1, parts0_rest = carried[:n_red], carried[n_red:]
    loss = joined1[4].reshape(RG_ROWS, D_G)[RG_LOSS, 0]
    pairs0_win = _add_sibling_half(
        place, [dwin0], _run_exchange(_swap_exchange([dwin0]), name="rs_swap0_win"), wire[:1], name="rs_pair0_win"
    )
    (grad_x,), parts0_win = _input_grad(
        bwd0[0], bwd0[1], p0["win"], name="input_grad0", exchange=_scatter_exchange(pairs0_win[:1], pairs0_win[1:])
    )
    halves0 = _sum_chip_blocks(place, parts0_win + parts0_rest, keep, name="rs_sum0")
    reduced = [reduced_layer(_run_exchange(_join_exchange(halves0, keep), name="rs_join0")), reduced_layer(joined1)]

    grad, delta, new_m, new_v = {}, {}, {}, {}
    for k in LARGE:
        w3 = weights[k]
        grad[k], delta[k], new_m[k], new_v[k] = _adamw_large(
            w3, given["m_" + k], given["v_" + k], [reduced[l][k] for l in range(DEPTH)], name=f"adamw_{k}"
        )
    g_p, d_p, m_p, v_p = _adamw_packed(
        [reduced[l]["packed"] for l in range(DEPTH)],
        [weights[k] for k in PACKED_NAMES],
        [given["m_" + k] for k in PACKED_NAMES],
        [given["v_" + k] for k in PACKED_NAMES],
        name="adamw_packed",
    )
    for a, k in enumerate(PACKED_NAMES):
        grad[k], delta[k], new_m[k], new_v[k] = g_p[a], d_p[a], m_p[a], v_p[a]
    small_names = [k for k in WEIGHT_NAMES if k not in LARGE + PACKED_NAMES]
    for k in small_names:
        grad[k] = jnp.stack([reduced[l][k] for l in range(DEPTH)])
    d_s, m_s, v_s = _adamw_small(
        [weights[k] for k in small_names],
        [grad[k] for k in small_names],
        [given["m_" + k] for k in small_names],
        [given["v_" + k] for k in small_names],
        name="adamw_small",
    )
    for a, k in enumerate(small_names):
        delta[k], new_m[k], new_v[k] = d_s[a], m_s[a], v_s[a]

    return (
        loss,
        grad_x[None],
        *[grad[k] for k in WEIGHT_NAMES],
        *[delta[k] for k in WEIGHT_NAMES],
        *[new_m[k] for k in WEIGHT_NAMES],
        *[new_v[k] for k in WEIGHT_NAMES],
    )
```

```python
import math

import jax
import jax.numpy as jnp
from jax import lax
from jax.experimental import pallas as pl
from jax.experimental.pallas import tpu as pltpu

F32 = jnp.float32
BF16 = jnp.bfloat16

D_MODEL = 1024
DEPTH = 2
D_G = 256
D_MIX = 5 * D_G
D_IN = 9 * D_G + D_MIX
N_SUB = 4
HEAD_DIM = 64
CONV_A = 3
CONV_D = 31
CHUNK = 128
MEM_LEN = 256
N_CHIPS = 4
W_IN_SHARD = D_IN // N_CHIPS
LN_EPS = 1e-5
ALPHA = (2.0 * DEPTH) ** 0.25
ATT_SCALE = 1.0 / math.sqrt(HEAD_DIM)
GELU_C = math.sqrt(2.0 / math.pi)
GELU_A = 0.044715

ADAM_LR = 0.001
ADAM_B1 = 0.9
ADAM_B2 = 0.999
ADAM_EPS = 1e-08
ADAM_WD = 0.01
ADAM_STEP = 10

C_XA, C_BA, C_CA, C_U, C_V, C_XC, C_DA, C_DG, C_Q, C_GATE = (D_G * i for i in range(10))

HALO_A = 8
HALO_C = 16
HALO_D = 32

RW_VEC = 0
RW_CONVA = 16
RW_DW = 24
RW_SGB = 56

RG_VEC = 0
RG_CONVA = 16
RG_DW = 24
RG_SGW = 56
RG_SGB = RG_SGW + 2 * CHUNK
RG_POOL = RG_SGB + CHUNK
RG_LN = RG_POOL + D_G
RG_LOSS = 8
RG_ROWS = 768

VMEM_LIMIT = 62 * 1024 * 1024

AUX_CVD = 0
AUX_PM = D_G
AUX_P = 2 * D_G
AUX_COLS = AUX_P + N_SUB * MEM_LEN
SEQ_TILE = 256
FWD_TILE = 512
MM_TILE = 1024
W_GRAD_TILE = 4096
ADAM_TILE = 512

MESH = pl.DeviceIdType.MESH
ANY = pl.BlockSpec(memory_space=pl.ANY)
NT = (((1,), (1,)), ((), ()))
TN = (((0,), (0,)), ((), ()))


def _dot(a, b):
    return jnp.dot(a, b, preferred_element_type=F32)


def _dot_nt(a, b):
    return lax.dot_general(a, b, NT, preferred_element_type=F32)


def _dot_tn(a, b):
    return lax.dot_general(a, b, TN, preferred_element_type=F32)


def _full(shape):
    zeros = (0,) * len(shape)
    return pl.BlockSpec(shape, lambda *_: zeros)


class _Exchange:
    def __init__(self, operands, outputs, sem_counts, before, after):
        self.operands, self.outputs, self.sem_counts, self.before, self.after = operands, outputs, sem_counts, before, after

    def specs(self, first_input, first_output):
        aliases = {first_input + src: first_output + j for j, (_, src) in enumerate(self.outputs) if src is not None}
        return (
            [ANY] * len(self.operands),
            [ANY] * len(self.outputs),
            [sds for sds, _ in self.outputs],
            [pltpu.SemaphoreType.DMA((k,)) for k in self.sem_counts],
            aliases,
        )

    def split(self, ins, outs):
        refs = list(ins)
        for j, (_, src) in enumerate(self.outputs):
            if src is not None:
                refs[src] = outs[j]
        return refs


def _both(first, second):
    n1, m1, s1 = len(first.operands), len(first.outputs), len(first.sem_counts)
    outputs = first.outputs + [(sds, None if src is None else n1 + src) for sds, src in second.outputs]

    def before(step, steps, refs, outs, sems):
        first.before(step, steps, refs[:n1], outs[:m1], sems[:s1])
        second.before(step, steps, refs[n1:], outs[m1:], sems[s1:])

    def after(step, steps, refs, outs, sems):
        first.after(step, steps, refs[:n1], outs[:m1], sems[:s1])
        second.after(step, steps, refs[n1:], outs[m1:], sems[s1:])

    return _Exchange(first.operands + second.operands, outputs, first.sem_counts + second.sem_counts, before, after)


def _when(cond, fn):
    if isinstance(cond, bool):
        if cond:
            fn()
    else:
        pl.when(cond)(fn)


def _run_exchange(exchange, *, name):
    n_in, n_out = len(exchange.operands), len(exchange.outputs)
    in_specs, out_specs, out_shape, sems, aliases = exchange.specs(0, 0)

    def body(*refs):
        ins, outs, sem_refs = refs[:n_in], refs[n_in : n_in + n_out], refs[n_in + n_out :]
        refs = exchange.split(ins, outs)
        exchange.before(0, 1, refs, outs, sem_refs)
        exchange.after(0, 1, refs, outs, sem_refs)

    return pl.pallas_call(
        body, name=name, in_specs=in_specs, out_specs=out_specs, out_shape=out_shape, scratch_shapes=sems,
        input_output_aliases=aliases,
    )(*exchange.operands)


def _gridded_call(body, *, name, steps, in_specs, out_specs, out_shape, scratch_shapes, operands, exchange=None):
    params = pltpu.CompilerParams(dimension_semantics=("arbitrary",), vmem_limit_bytes=VMEM_LIMIT)
    if exchange is None:
        outs = pl.pallas_call(
            body, name=name, grid=(steps,), in_specs=in_specs, out_specs=out_specs, out_shape=out_shape,
            scratch_shapes=scratch_shapes, compiler_params=params,
        )(*operands)
        return list(outs), []
    n_in, n_out, n_scr = len(in_specs), len(out_specs), len(scratch_shapes)
    x_in, x_out = len(exchange.operands), len(exchange.outputs)
    ex_in_specs, ex_out_specs, ex_out_shape, ex_sems, aliases = exchange.specs(n_in, n_out)

    def full(*refs):
        own_in, refs = refs[:n_in], refs[n_in:]
        ex_in, refs = refs[:x_in], refs[x_in:]
        own_out, refs = refs[:n_out], refs[n_out:]
        ex_out, refs = refs[:x_out], refs[x_out:]
        own_scr, sem_refs = refs[:n_scr], refs[n_scr:]
        ex_refs = exchange.split(ex_in, ex_out)
        step = pl.program_id(0)
        exchange.before(step, steps, ex_refs, ex_out, sem_refs)
        body(*own_in, *own_out, *own_scr)
        exchange.after(step, steps, ex_refs, ex_out, sem_refs)

    outs = pl.pallas_call(
        full, name=name, grid=(steps,), in_specs=in_specs + ex_in_specs, out_specs=out_specs + ex_out_specs,
        out_shape=out_shape + ex_out_shape, scratch_shapes=scratch_shapes + ex_sems, input_output_aliases=aliases,
        compiler_params=params,
    )(*operands, *exchange.operands)
    return list(outs[:n_out]), list(outs[n_out:])


def _sigmoid(x):
    return 0.5 * jnp.tanh(0.5 * x) + 0.5


def _gelu(x):
    t = jnp.tanh(GELU_C * (x + GELU_A * x * x * x))
    return 0.5 * x * (1.0 + t), t


def _gelu_grad(x, t):
    return 0.5 * (1.0 + t) + 0.5 * x * (1.0 - t * t) * (GELU_C * (1.0 + 3.0 * GELU_A * x * x))


def _normalize(v):
    mu = jnp.mean(v, axis=-1, keepdims=True)
    d = v - mu
    var = jnp.mean(d * d, axis=-1, keepdims=True)
    rstd = lax.rsqrt(var + LN_EPS)
    return d * rstd, rstd


def _normalize_grad(dhat, hat, rstd):
    m1 = jnp.mean(dhat, axis=-1, keepdims=True)
    m2 = jnp.mean(dhat * hat, axis=-1, keepdims=True)
    return rstd * (dhat - m1 - hat * m2)


def _lane(width=D_G):
    return lax.broadcasted_iota(jnp.int32, (1, width), 1)


def _head_masks():
    head = _lane() // HEAD_DIM
    return [(head == h).astype(F32) for h in range(N_SUB)]


def _stack_heads(v, masks):
    return jnp.concatenate([v * m for m in masks], axis=0)


def _tril_mask_cat():
    t = lax.broadcasted_iota(jnp.int32, (CHUNK, N_SUB * CHUNK), 0)
    s = lax.broadcasted_iota(jnp.int32, (CHUNK, N_SUB * CHUNK), 1) % CHUNK
    return s <= t


def _triu_mask_cat():
    s = lax.broadcasted_iota(jnp.int32, (CHUNK, N_SUB * CHUNK), 0)
    t = lax.broadcasted_iota(jnp.int32, (CHUNK, N_SUB * CHUNK), 1) % CHUNK
    return t >= s


def _pool_select(a2, a4, a8, a16):
    lane = _lane()
    return jnp.where(lane < 64, a2, jnp.where(lane < 128, a4, jnp.where(lane < 192, a8, a16)))


def _pool_inv_count(row0, rows):
    t = row0 + lax.broadcasted_iota(jnp.int32, (HALO_C, D_G), 0)
    lane = lax.broadcasted_iota(jnp.int32, (HALO_C, D_G), 1)
    win = jnp.where(lane < 64, 2, jnp.where(lane < 128, 4, jnp.where(lane < 192, 8, 16)))
    head = 1.0 / jnp.minimum(t + 1, win).astype(F32)
    inv_win = jnp.broadcast_to(_pool_select(0.5, 0.25, 0.125, 0.0625), (rows - HALO_C, D_G))
    return jnp.concatenate([head, inv_win], axis=0)


def _trailing_window_sum(halo, cur):
    e = jnp.concatenate([halo, cur], axis=0)
    s2 = e + pltpu.roll(e, 1, 0)
    s4 = s2 + pltpu.roll(s2, 2, 0)
    s8 = s4 + pltpu.roll(s4, 4, 0)
    s16 = s8 + pltpu.roll(s8, 8, 0)
    return _pool_select(s2, s4, s8, s16)[HALO_C:]


def _leading_window_sum(cur, halo):
    e = jnp.concatenate([cur, halo], axis=0)
    n = e.shape[0]
    s2 = e + pltpu.roll(e, n - 1, 0)
    s4 = s2 + pltpu.roll(s2, n - 2, 0)
    s8 = s4 + pltpu.roll(s4, n - 4, 0)
    s16 = s8 + pltpu.roll(s8, n - 8, 0)
    return _pool_select(s2, s4, s8, s16)[: cur.shape[0]]


def _softmax_blocks(sc):
    out = []
    for h in range(N_SUB):
        s = sc[:, h * MEM_LEN : (h + 1) * MEM_LEN]
        e = jnp.exp(s - jnp.max(s, axis=-1, keepdims=True))
        out.append(e * (1.0 / jnp.sum(e, axis=-1, keepdims=True)))
    return jnp.concatenate(out, axis=-1)


STRIP = 32
SHIFTS = 8


def _fill_shifts(buf):
    n = buf.shape[1] - SHIFTS
    for r in range(1, SHIFTS):
        buf[r, 0:n, :] = buf[0, r : r + n, :]


def _shifted(buf, off, rows):
    r = off % SHIFTS
    return buf[r, off - r : off - r + rows, :]


def _sgu_mix(vn, wcat_b, sgb, masks):
    vbd = _stack_heads(vn, masks).astype(BF16)
    return _dot(wcat_b, vbd) + sgb, vbd


def _layer_forward(x, win, kst, vst, wout, sw, wcat, poolw, pww, ln, tgt, *, name, exchange=None):
    seq = x.shape[0]
    tile = min(FWD_TILE, seq)
    n_tiles = seq // tile
    last = tgt is not None

    def body(*refs):
        x_ref, win_ref, kst_ref, vst_ref, wout_ref, sw_ref, wcat_ref, pool_ref, pw_ref, ln_ref = refs[:10]
        refs = refs[10:]
        if last:
            tgt_ref, refs = refs[0], refs[1:]
        proj_ref, y_ref, z_ref, out_ref, aux_ref = refs[:5]
        refs = refs[5:]
        if last:
            loss_ref, refs = refs[0], refs[1:]
        pbuf, xchalo, gbuf = refs
        i = pl.program_id(0)

        @pl.when(i == 0)
        def _():
            pbuf[0:HALO_A, :] = jnp.zeros((HALO_A, D_G), F32)
            xchalo[...] = jnp.zeros((HALO_C, D_G), F32)
            gbuf[0, 0:HALO_D, :] = jnp.zeros((HALO_D, D_G), F32)
            if last:
                loss_ref[...] = jnp.zeros((8, 128), F32)

        xt = x_ref[...]
        xb = xt.astype(BF16)

        blocks = {}

        def project(k):
            blocks[k] = _dot(xb, win_ref[k])
            proj_ref[:, k * W_IN_SHARD : (k + 1) * W_IN_SHARD] = blocks[k]

        def cols(start, width=D_G):
            parts, c = [], start
            while c < start + width:
                k, lo = divmod(c, W_IN_SHARD)
                hi = min(W_IN_SHARD, lo + start + width - c)
                parts.append(blocks[k][:, lo:hi])
                c += hi - lo
            return parts[0] if len(parts) == 1 else jnp.concatenate(parts, axis=1)

        project(0)
        project(1)
        masks = _head_masks()

        pbuf[HALO_A : HALO_A + tile, :] = cols(C_CA) * cols(C_XA)
        cv = jnp.zeros((tile, D_G), F32)
        for k in range(CONV_A):
            off = HALO_A - (CONV_A - 1) + k
            cv = cv + sw_ref[RW_CONVA + k : RW_CONVA + k + 1, :] * pbuf[off : off + tile, :]
        y_ref[:, 0:D_G] = cols(C_BA) * cv
        pbuf[0:HALO_A, :] = pbuf[tile : tile + HALO_A, :]

        project(2)

        ua, _ = _gelu(cols(C_U))
        vg, _ = _gelu(cols(C_V))
        vhat, _ = _normalize(vg)
        vn = vhat * sw_ref[RW_VEC : RW_VEC + 1, :] + sw_ref[RW_VEC + 1 : RW_VEC + 2, :]
        wcat_b = jnp.where(_tril_mask_cat(), wcat_ref[...], 0.0).astype(BF16)
        sgb = sw_ref[RW_SGB : RW_SGB + CHUNK, :]
        for j in range(tile // CHUNK):
            rows = slice(j * CHUNK, (j + 1) * CHUNK)
            mixed, _ = _sgu_mix(vn[rows], wcat_b, sgb, masks)
            y_ref[rows, D_G : 2 * D_G] = ua[rows] * mixed

        xc = cols(C_XC)
        wsum = _trailing_window_sum(xchalo[...], xc)
        pm = wsum * _pool_inv_count(i * tile, tile) - xc
        aux_ref[:, AUX_PM : AUX_PM + D_G] = pm
        y_ref[:, 2 * D_G : 3 * D_G] = _dot(pm.astype(BF16), pool_ref[...]) * sw_ref[RW_VEC + 2 : RW_VEC + 3, :]
        xchalo[...] = xc[tile - HALO_C :, :]

        project(3)

        gbuf[0, HALO_D : HALO_D + tile, :] = cols(C_DA) * _sigmoid(cols(C_DG))
        _fill_shifts(gbuf)
        for r0 in range(0, tile, STRIP):
            acc = jnp.zeros((STRIP, D_G), F32) + sw_ref[RW_VEC + 3 : RW_VEC + 4, :]
            for k in range(CONV_D):
                off = HALO_D - (CONV_D - 1) + k
                acc = acc + sw_ref[RW_DW + k : RW_DW + k + 1, :] * _shifted(gbuf, off + r0, STRIP)
            aux_ref[r0 : r0 + STRIP, AUX_CVD : AUX_CVD + D_G] = acc
        nhat, _ = _normalize(aux_ref[:, AUX_CVD : AUX_CVD + D_G])
        nrm = nhat * sw_ref[RW_VEC + 4 : RW_VEC + 5, :] + sw_ref[RW_VEC + 5 : RW_VEC + 6, :]
        y_ref[:, 3 * D_G : 4 * D_G] = _dot((nrm * _sigmoid(nrm)).astype(BF16), pw_ref[...])
        gbuf[0, 0:HALO_D, :] = gbuf[0, tile : tile + HALO_D, :]

        qb = cols(C_Q).astype(BF16)
        p_all = _softmax_blocks(_dot_nt(qb, kst_ref[...]) * ATT_SCALE)
        aux_ref[:, AUX_P:] = p_all
        y_ref[:, 4 * D_G : 5 * D_G] = _dot(p_all.astype(BF16), vst_ref[...])

        gate = cols(C_GATE, D_MIX)
        hid = y_ref[...] * (gate * _sigmoid(gate))
        z = ALPHA * xt + _dot(hid.astype(BF16), wout_ref[...])
        z_ref[...] = z
        zhat, _ = _normalize(z)
        xn = zhat * ln_ref[0:1, :] + ln_ref[1:2, :]
        if last:
            err = xn - tgt_ref[...]
            out_ref[...] = err * (1.0 / D_MODEL)
            loss_ref[...] += jnp.sum(err * err) * (0.5 / D_MODEL)
        else:
            out_ref[...] = xn

    def rows(width):
        return pl.BlockSpec((tile, width), lambda i: (i, 0))

    operands = [x, win, kst, vst, wout, sw, wcat, poolw, pww, ln]
    in_specs = [rows(D_MODEL)] + [_full(a.shape) for a in operands[1:]]
    widths = [D_IN, D_MIX, D_MODEL, D_MODEL, AUX_COLS]
    out_shape = [jax.ShapeDtypeStruct((seq, w), F32) for w in widths]
    out_specs = [rows(w) for w in widths]
    if last:
        operands.append(tgt)
        in_specs.append(rows(D_MODEL))
        out_shape.append(jax.ShapeDtypeStruct((8, 128), F32))
        out_specs.append(_full((8, 128)))
    return _gridded_call(
        body,
        name=name,
        steps=n_tiles,
        in_specs=in_specs,
        out_specs=out_specs,
        out_shape=out_shape,
        scratch_shapes=[
            pltpu.VMEM((HALO_A + tile, D_G), F32),
            pltpu.VMEM((HALO_C, D_G), F32),
            pltpu.VMEM((SHIFTS, HALO_D + tile, D_G), F32),
        ],
        operands=operands,
        exchange=exchange,
    )


def _layer_backward(dxn, z, proj, y, cvd, kst, vst, wout, sw, wcat, wcat_t, poolw, pww, ln, *, name, exchange=None):
    seq = dxn.shape[0]
    tile = min(SEQ_TILE, seq)
    n_tiles = seq // tile
    halo_blocks = tile // HALO_D

    def body(
        dxn_ref, z_ref, proj_ref, halo_ref, y_ref, aux_ref, kst_ref, vst_ref, wout_ref, sw_ref, wcat_ref, wcat_t_ref,
        pool_ref, pw_ref, ln_ref, dproj_ref, dz_ref, dwout_ref, dkst_ref, dvst_ref, dpw_ref, sg_ref,
        pbuf, dcvbuf, rhalo, gbuf, dgbuf, dwacc,
    ):
        i = pl.program_id(0)
        ti = n_tiles - 1 - i

        @pl.when(i == 0)
        def _():
            dwout_ref[...] = jnp.zeros(dwout_ref.shape, F32)
            dkst_ref[...] = jnp.zeros(dkst_ref.shape, F32)
            dvst_ref[...] = jnp.zeros(dvst_ref.shape, F32)
            dpw_ref[...] = jnp.zeros(dpw_ref.shape, F32)
            sg_ref[...] = jnp.zeros(sg_ref.shape, F32)
            dwacc[...] = jnp.zeros(dwacc.shape, F32)
            dcvbuf[tile : tile + HALO_A, :] = jnp.zeros((HALO_A, D_G), F32)
            rhalo[...] = jnp.zeros((HALO_C, D_G), F32)
            dgbuf[0, tile : tile + HALO_D, :] = jnp.zeros((HALO_D, D_G), F32)

        def acc_row(row, val):
            sg_ref[row : row + 1, :] += jnp.sum(val, axis=0, keepdims=True)

        masks = _head_masks()
        has_past = (ti > 0).astype(F32)

        zhat, zrstd = _normalize(z_ref[...])
        dxn_t = dxn_ref[...]
        dlg = jnp.sum(dxn_t * zhat, axis=0, keepdims=True)
        dlb = jnp.sum(dxn_t, axis=0, keepdims=True)
        for j in range(D_MODEL // D_G):
            sg_ref[RG_LN + j : RG_LN + j + 1, :] += dlg[:, j * D_G : (j + 1) * D_G]
            sg_ref[RG_LN + 4 + j : RG_LN + 5 + j, :] += dlb[:, j * D_G : (j + 1) * D_G]
        dz = _normalize_grad(dxn_t * ln_ref[0:1, :], zhat, zrstd)
        dz_ref[...] = dz
        dzb = dz.astype(BF16)

        gate = proj_ref[:, C_GATE:]
        sgm = _sigmoid(gate)
        silu = gate * sgm
        yc = y_ref[...]
        dwout_ref[...] += _dot_tn((yc * silu).astype(BF16), dzb)
        dh = _dot_nt(dzb, wout_ref[...])
        dproj_ref[:, C_GATE:] = (dh * yc * (sgm * (1.0 + gate * (1.0 - sgm)))).astype(BF16)
        dy = dh * silu

        dya = dy[:, 0:D_G]
        xa = proj_ref[:, C_XA : C_XA + D_G]
        ba = proj_ref[:, C_BA : C_BA + D_G]
        ca = proj_ref[:, C_CA : C_CA + D_G]
        past = slice(HALO_D - HALO_A, HALO_D)
        pbuf[0:HALO_A, :] = halo_ref[past, C_CA : C_CA + D_G] * halo_ref[past, C_XA : C_XA + D_G] * has_past
        pbuf[HALO_A : HALO_A + tile, :] = ca * xa
        cv = jnp.zeros((tile, D_G), F32)
        for k in range(CONV_A):
            off = HALO_A - (CONV_A - 1) + k
            cv = cv + sw_ref[RW_CONVA + k : RW_CONVA + k + 1, :] * pbuf[off : off + tile, :]
        dproj_ref[:, C_BA : C_BA + D_G] = (dya * cv).astype(BF16)
        dcv = dya * ba
        dcvbuf[0:tile, :] = dcv
        dp = jnp.zeros((tile, D_G), F32)
        for k in range(CONV_A):
            off = HALO_A - (CONV_A - 1) + k
            acc_row(RG_CONVA + k, dcv * pbuf[off : off + tile, :])
            back = CONV_A - 1 - k
            dp = dp + sw_ref[RW_CONVA + k : RW_CONVA + k + 1, :] * dcvbuf[back : back + tile, :]
        dproj_ref[:, C_CA : C_CA + D_G] = (dp * xa).astype(BF16)
        dproj_ref[:, C_XA : C_XA + D_G] = (dp * ca).astype(BF16)
        dcvbuf[tile : tile + HALO_A, :] = dcvbuf[0:HALO_A, :]

        dyb = dy[:, D_G : 2 * D_G]
        u = proj_ref[:, C_U : C_U + D_G]
        v = proj_ref[:, C_V : C_V + D_G]
        ua, ut = _gelu(u)
        vg, vt = _gelu(v)
        vhat, vrstd = _normalize(vg)
        sg_g = sw_ref[RW_VEC : RW_VEC + 1, :]
        vn = vhat * sg_g + sw_ref[RW_VEC + 1 : RW_VEC + 2, :]
        tril = _tril_mask_cat()
        wcat_b = jnp.where(tril, wcat_ref[...], 0.0).astype(BF16)
        wcat_tb = jnp.where(_triu_mask_cat(), wcat_t_ref[...], 0.0).astype(BF16)
        sgb = sw_ref[RW_SGB : RW_SGB + CHUNK, :]
        dmixed = dyb * ua
        dvn_parts = []
        du_parts = []
        dwcat = jnp.zeros((CHUNK, N_SUB * CHUNK), F32)
        dsgb = jnp.zeros((CHUNK, D_G), F32)
        for j in range(tile // CHUNK):
            rows = slice(j * CHUNK, (j + 1) * CHUNK)
            mixed, vbd = _sgu_mix(vn[rows], wcat_b, sgb, masks)
            du_parts.append(dyb[rows] * mixed)
            dmx = dmixed[rows]
            dsgb = dsgb + dmx
            dwcat = dwcat + _dot_nt(dmx.astype(BF16), vbd)
            dvn_parts.append(_dot(wcat_tb, _stack_heads(dmx, masks).astype(BF16)))
        dwcat = jnp.where(tril, dwcat, 0.0)
        sg_ref[RG_SGW : RG_SGW + CHUNK, :] += dwcat[:, 0:D_G]
        sg_ref[RG_SGW + CHUNK : RG_SGW + 2 * CHUNK, :] += dwcat[:, D_G:]
        sg_ref[RG_SGB : RG_SGB + CHUNK, :] += dsgb
        dvn = jnp.concatenate(dvn_parts, axis=0)
        du_act = jnp.concatenate(du_parts, axis=0)
        acc_row(RG_VEC, dvn * vhat)
        acc_row(RG_VEC + 1, dvn)
        dvg = _normalize_grad(dvn * sg_g, vhat, vrstd)
        dproj_ref[:, C_U : C_U + D_G] = (du_act * _gelu_grad(u, ut)).astype(BF16)
        dproj_ref[:, C_V : C_V + D_G] = (dvg * _gelu_grad(v, vt)).astype(BF16)

        dyc = dy[:, 2 * D_G : 3 * D_G]
        inv_cnt = _pool_inv_count(ti * tile, tile)
        pmb = aux_ref[:, AUX_PM : AUX_PM + D_G].astype(BF16)
        pool_b = pool_ref[...]
        scale = sw_ref[RW_VEC + 2 : RW_VEC + 3, :]
        acc_row(RG_VEC + 2, dyc * _dot(pmb, pool_b))
        dpre = (dyc * scale).astype(BF16)
        sg_ref[RG_POOL : RG_POOL + D_G, :] += _dot_tn(pmb, dpre)
        dpm = _dot_nt(dpre, pool_b)
        r = dpm * inv_cnt
        dproj_ref[:, C_XC : C_XC + D_G] = (_leading_window_sum(r, rhalo[...]) - dpm).astype(BF16)
        rhalo[...] = r[0:HALO_C, :]

        dyd = dy[:, 3 * D_G : 4 * D_G]
        da = proj_ref[:, C_DA : C_DA + D_G]
        sgd = _sigmoid(proj_ref[:, C_DG : C_DG + D_G])
        gbuf[0, 0:HALO_D, :] = halo_ref[:, C_DA : C_DA + D_G] * _sigmoid(halo_ref[:, C_DG : C_DG + D_G]) * has_past
        gbuf[0, HALO_D : HALO_D + tile, :] = da * sgd
        _fill_shifts(gbuf)
        nhat, nrstd = _normalize(aux_ref[:, AUX_CVD : AUX_CVD + D_G])
        cc_g = sw_ref[RW_VEC + 4 : RW_VEC + 5, :]
        nrm = nhat * cc_g + sw_ref[RW_VEC + 5 : RW_VEC + 6, :]
        sgn = _sigmoid(nrm)
        dydb = dyd.astype(BF16)
        dpw_ref[...] += _dot_tn((nrm * sgn).astype(BF16), dydb)
        dn = _dot_nt(dydb, pw_ref[...]) * (sgn * (1.0 + nrm * (1.0 - sgn)))
        acc_row(RG_VEC + 4, dn * nhat)
        acc_row(RG_VEC + 5, dn)
        dcvd = _normalize_grad(dn * cc_g, nhat, nrstd)
        acc_row(RG_VEC + 3, dcvd)
        dgbuf[0, 0:tile, :] = dcvd
        _fill_shifts(dgbuf)
        for r0 in range(0, tile, STRIP):
            d_s = dgbuf[0, r0 : r0 + STRIP, :]
            dg = jnp.zeros((STRIP, D_G), F32)
            for k in range(CONV_D):
                off = HALO_D - (CONV_D - 1) + k
                prod = d_s * _shifted(gbuf, off + r0, STRIP)
                part = prod[0:8]
                for q in range(8, STRIP, 8):
                    part = part + prod[q : q + 8]
                dwacc[8 * k : 8 * k + 8, :] += part
                back = CONV_D - 1 - k
                dg = dg + sw_ref[RW_DW + k : RW_DW + k + 1, :] * _shifted(dgbuf, back + r0, STRIP)
            da_s = proj_ref[r0 : r0 + STRIP, C_DA : C_DA + D_G]
            sgd_s = _sigmoid(proj_ref[r0 : r0 + STRIP, C_DG : C_DG + D_G])
            dproj_ref[r0 : r0 + STRIP, C_DA : C_DA + D_G] = (dg * sgd_s).astype(BF16)
            dproj_ref[r0 : r0 + STRIP, C_DG : C_DG + D_G] = (dg * da_s * sgd_s * (1.0 - sgd_s)).astype(BF16)
        dgbuf[0, tile : tile + HALO_D, :] = dgbuf[0, 0:HALO_D, :]

        @pl.when(i == n_tiles - 1)
        def _():
            for k in range(CONV_D):
                sg_ref[RG_DW + k : RG_DW + k + 1, :] = jnp.sum(dwacc[8 * k : 8 * k + 8, :], axis=0, keepdims=True)

        dyeb = dy[:, 4 * D_G : 5 * D_G].astype(BF16)
        qb = proj_ref[:, C_Q : C_Q + D_G].astype(BF16)
        kst_b = kst_ref[...]
        p_all = aux_ref[:, AUX_P:]
        dvst_ref[...] += _dot_tn(p_all.astype(BF16), dyeb)
        dp_all = _dot_nt(dyeb, vst_ref[...])
        ds = []
        for h in range(N_SUB):
            blk = slice(h * MEM_LEN, (h + 1) * MEM_LEN)
            p, dpb = p_all[:, blk], dp_all[:, blk]
            ds.append(p * (dpb - jnp.sum(dpb * p, axis=-1, keepdims=True)))
        dsb = (jnp.concatenate(ds, axis=-1) * ATT_SCALE).astype(BF16)
        dproj_ref[:, C_Q : C_Q + D_G] = _dot(dsb, kst_b).astype(BF16)
        dkst_ref[...] += _dot_tn(dsb, qb)

    def rows(width):
        return pl.BlockSpec((tile, width), lambda i: (n_tiles - 1 - i, 0))

    halo_spec = pl.BlockSpec((HALO_D, D_IN), lambda i: (jnp.maximum((n_tiles - 1 - i) * halo_blocks - 1, 0), 0))
    weights = [kst, vst, wout, sw, wcat, wcat_t, poolw, pww, ln]
    acc_shapes = [(D_MIX, D_MODEL), (N_SUB * MEM_LEN, D_G), (N_SUB * MEM_LEN, D_G), (D_G, D_G), (RG_ROWS, D_G)]
    return _gridded_call(
        body,
        name=name,
        steps=n_tiles,
        in_specs=[rows(D_MODEL), rows(D_MODEL), rows(D_IN), halo_spec, rows(D_MIX), rows(AUX_COLS)]
        + [_full(a.shape) for a in weights],
        out_specs=[rows(D_IN), rows(D_MODEL)] + [_full(s) for s in acc_shapes],
        out_shape=[jax.ShapeDtypeStruct((seq, D_IN), BF16), jax.ShapeDtypeStruct((seq, D_MODEL), F32)]
        + [jax.ShapeDtypeStruct(s, F32) for s in acc_shapes],
        scratch_shapes=[
            pltpu.VMEM((HALO_A + tile, D_G), F32),
            pltpu.VMEM((tile + HALO_A, D_G), F32),
            pltpu.VMEM((HALO_C, D_G), F32),
            pltpu.VMEM((SHIFTS, HALO_D + tile, D_G), F32),
            pltpu.VMEM((SHIFTS, tile + HALO_D, D_G), F32),
            pltpu.VMEM((8 * CONV_D, D_G), F32),
        ],
        operands=[dxn, z, proj, proj, y, cvd, *weights],
        exchange=exchange,
    )


def _kv_forward(mem, wkv, *, name):
    def body(mem_ref, wkv_ref, kst_ref, vst_ref):
        kv = _dot(mem_ref[...].astype(BF16), wkv_ref[...])
        masks = _head_masks()
        kst_ref[...] = _stack_heads(kv[:, 0:D_G], masks).astype(BF16)
        vst_ref[...] = _stack_heads(kv[:, D_G:], masks).astype(BF16)

    shape = jax.ShapeDtypeStruct((N_SUB * MEM_LEN, D_G), BF16)
    return pl.pallas_call(body, name=name, out_shape=[shape, shape])(mem, wkv)


def _kv_backward(mem, dkst, dvst, *, name):
    def body(mem_ref, dkst_ref, dvst_ref, dwkv_ref):
        masks = _head_masks()
        memb = mem_ref[...].astype(BF16)
        for col, ref in ((0, dkst_ref), (D_G, dvst_ref)):
            d = jnp.zeros((MEM_LEN, D_G), F32)
            for h in range(N_SUB):
                d = d + ref[h * MEM_LEN : (h + 1) * MEM_LEN, :] * masks[h]
            dwkv_ref[:, col : col + D_G] = _dot_tn(memb, d.astype(BF16))

    return pl.pallas_call(body, name=name, out_shape=jax.ShapeDtypeStruct((D_MODEL, 2 * D_G), F32))(mem, dkst, dvst)


def _input_grad(dproj, dz, win, *, name, exchange=None):
    seq = dproj.shape[0]
    tile = min(MM_TILE // 2, seq)

    def body(dproj_ref, dz_ref, win_ref, dx_ref):
        acc = ALPHA * dz_ref[...]
        for k in range(N_CHIPS):
            acc = acc + _dot_nt(dproj_ref[:, k * W_IN_SHARD : (k + 1) * W_IN_SHARD], win_ref[k])
        dx_ref[...] = acc

    return _gridded_call(
        body,
        name=name,
        steps=seq // tile,
        in_specs=[
            pl.BlockSpec((tile, D_IN), lambda i: (i, 0)),
            pl.BlockSpec((tile, D_MODEL), lambda i: (i, 0)),
            _full(win.shape),
        ],
        out_specs=[pl.BlockSpec((tile, D_MODEL), lambda i: (i, 0))],
        out_shape=[jax.ShapeDtypeStruct((seq, D_MODEL), F32)],
        scratch_shapes=[],
        operands=[dproj, dz, win],
        exchange=exchange,
    )


def _input_weight_grad(x, dproj, *, name, exchange=None):
    seq = x.shape[0]
    tile = min(W_GRAD_TILE, seq)
    n_rows = seq // tile

    def body(x_ref, dproj_ref, dwin_ref):
        part = _dot_tn(x_ref[...].astype(BF16), dproj_ref[...])
        if n_rows == 1:
            dwin_ref[0] = part
        else:

            @pl.when(pl.program_id(0) % n_rows == 0)
            def _():
                dwin_ref[...] = jnp.zeros(dwin_ref.shape, F32)

            dwin_ref[0] += part

    return _gridded_call(
        body,
        name=name,
        steps=N_CHIPS * n_rows,
        in_specs=[
            pl.BlockSpec((tile, D_MODEL), lambda s: (s % n_rows, 0)),
            pl.BlockSpec((tile, W_IN_SHARD), lambda s: (s % n_rows, s // n_rows)),
        ],
        out_specs=[pl.BlockSpec((1, D_MODEL, W_IN_SHARD), lambda s: (s // n_rows, 0, 0))],
        out_shape=[jax.ShapeDtypeStruct((N_CHIPS, D_MODEL, W_IN_SHARD), F32)],
        scratch_shapes=[],
        operands=[x, dproj],
        exchange=exchange,
    )


def _expand_sgb(sg_b):
    return jnp.repeat(sg_b.T, HEAD_DIM, axis=1)


def _pack_small_weights(sg_ln_g, sg_ln_b, pool_scale, cc_dw_b, cc_ln_g, cc_ln_b, conv_a_w, cc_dw_w, sg_b):
    vec = jnp.stack([sg_ln_g, sg_ln_b, pool_scale, cc_dw_b, cc_ln_g, cc_ln_b])
    return jnp.concatenate(
        [
            jnp.pad(vec, ((0, RW_CONVA - RW_VEC - 6), (0, 0))),
            jnp.pad(conv_a_w, ((0, RW_DW - RW_CONVA - CONV_A), (0, 0))),
            jnp.pad(cc_dw_w, ((0, RW_SGB - RW_DW - CONV_D), (0, 0))),
            _expand_sgb(sg_b),
        ]
    )


def _sg_w_cat(sg_w):
    cat = jnp.transpose(sg_w, (1, 0, 2)).reshape(CHUNK, N_SUB * CHUNK)
    cat_t = jnp.transpose(sg_w, (2, 0, 1)).reshape(CHUNK, N_SUB * CHUNK)
    return cat, cat_t


def _pool_block_diag(pool_w):
    tiled = jnp.tile(pool_w.reshape(D_G, HEAD_DIM), (1, N_SUB))
    row = lax.broadcasted_iota(jnp.int32, (D_G, D_G), 0) // HEAD_DIM
    col = lax.broadcasted_iota(jnp.int32, (D_G, D_G), 1) // HEAD_DIM
    return jnp.where(row == col, tiled, 0.0)


def _prepare_layer(mem, w, l):
    cat, cat_t = _sg_w_cat(w["sg_w"])
    kst, vst = _kv_forward(mem, w["w_kv"], name=f"kv_fwd{l}")
    return dict(
        win=w["w_in"],
        wout=w["w_out"],
        pww=w["cc_pw_w"],
        sw=_pack_small_weights(
            w["sg_ln_g"], w["sg_ln_b"], w["pool_scale"], w["cc_dw_b"], w["cc_ln_g"], w["cc_ln_b"],
            w["conv_a_w"], w["cc_dw_w"], w["sg_b"],
        ),
        wcat=cat,
        wcat_t=cat_t,
        poolw=_pool_block_diag(w["pool_w"]).astype(BF16),
        ln=jnp.stack([w["ln_g"], w["ln_b"]]),
        kst=kst,
        vst=vst,
    )


def _forward(l, h, p, tgt, exchange=None):
    return _layer_forward(
        h, p["win"], p["kst"], p["vst"], p["wout"], p["sw"], p["wcat"], p["poolw"], p["pww"], p["ln"], tgt,
        name=f"layer_fwd{l}", exchange=exchange,
    )


def _backward(l, dxn, s, p, exchange=None):
    return _layer_backward(
        dxn, s[2], s[0], s[1], s[4], p["kst"], p["vst"], p["wout"], p["sw"], p["wcat"], p["wcat_t"], p["poolw"],
        p["pww"], p["ln"], name=f"layer_bwd{l}", exchange=exchange,
    )


def _place():
    x, y, c = lax.axis_index("x"), lax.axis_index("y"), lax.axis_index("c")
    others = [(1 - x, y), (x, 1 - y), (1 - x, 1 - y)]
    return x, y, c, others


def _half(ref, c, axis):
    n = ref.shape[axis] // 2
    if axis == 0:
        return ref.at[pl.ds(c * n, n)]
    return ref.at[:, pl.ds(c * n, n)]


def _place_own_block(place, stacked, layer, dtypes, *, name):
    n = len(stacked)

    def body(place_ref, *refs):
        for a in range(n):
            refs[n + a][...] = refs[a][...].astype(dtypes[a])

    def block(s):
        return (1,) + s.shape[1:]

    return pl.pallas_call(
        body,
        name=name,
        grid_spec=pltpu.PrefetchScalarGridSpec(
            num_scalar_prefetch=1,
            grid=(1,),
            in_specs=[pl.BlockSpec(block(s), lambda i, place_ref: (layer, 0, 0)) for s in stacked],
            out_specs=[pl.BlockSpec(block(s), lambda i, place_ref: (place_ref[1], 0, 0)) for s in stacked],
        ),
        out_shape=[jax.ShapeDtypeStruct((N_CHIPS,) + s.shape[1:], dt) for s, dt in zip(stacked, dtypes)],
        compiler_params=pltpu.CompilerParams(dimension_semantics=("arbitrary",), vmem_limit_bytes=VMEM_LIMIT),
    )(place, *stacked)


def _sds(a):
    return jax.ShapeDtypeStruct(a.shape, a.dtype)


def _gather_exchange(bufs):
    n = len(bufs)

    def remote(sems, block, k, to):
        return pltpu.make_async_remote_copy(
            src_ref=block, dst_ref=block, send_sem=sems[0].at[k], recv_sem=sems[1].at[k], device_id=to, device_id_type=MESH
        )

    def before(step, steps, refs, outs, sems):
        def send():
            x, y, c, others = _place()
            for j, (px, py) in enumerate(others):
                for a in range(n):
                    remote(sems, _half(refs[a].at[2 * x + y], c, 0), 3 * a + j, (px, py, c)).start()

        _when(step == 0, send)

    def after(step, steps, refs, outs, sems):
        def pass_on():
            x, y, c, others = _place()
            for j, (px, py) in enumerate(others):
                for a in range(n):
                    landed = _half(refs[a].at[2 * px + py], c, 0)
                    remote(sems, landed, 3 * a + j, (px, py, c)).wait_recv()
                    remote(sems, landed, 3 * n + 3 * a + j, (x, y, 1 - c)).start()

        def finish():
            x, y, c, others = _place()
            for j, (px, py) in enumerate(others):
                for a in range(n):
                    remote(sems, _half(refs[a].at[2 * px + py], 1 - c, 0), 3 * n + 3 * a + j, (x, y, 1 - c)).wait_recv()
            for a in range(n):
                mine = _half(refs[a].at[2 * x + y], c, 0)
                for k in range(3 * a, 3 * a + 3):
                    remote(sems, mine, k, (x, y, 1 - c)).wait_send()
                    remote(sems, mine, 3 * n + k, (x, y, 1 - c)).wait_send()

        _when(step == (3 * steps) // 4, pass_on)
        _when(step == steps - 1, finish)

    return _Exchange(bufs, [(_sds(b), a) for a, b in enumerate(bufs)], [6 * n, 6 * n], before, after)


def _swap_exchange(grads):
    n = len(grads)

    def copy(refs, outs, sems, a):
        x, y, c, _ = _place()
        return pltpu.make_async_remote_copy(
            src_ref=_half(refs[a], 1 - c, 1), dst_ref=outs[a], send_sem=sems[0].at[a], recv_sem=sems[1].at[a],
            device_id=(x, y, 1 - c), device_id_type=MESH,
        )

    def before(step, steps, refs, outs, sems):
        _when(step == 0, lambda: [copy(refs, outs, sems, a).start() for a in range(n)] and None)

    def after(step, steps, refs, outs, sems):
        _when(step == steps - 1, lambda: [copy(refs, outs, sems, a).wait() for a in range(n)] and None)

    outputs = [(jax.ShapeDtypeStruct((N_CHIPS, g.shape[1] // 2, g.shape[2]), g.dtype), None) for g in grads]
    return _Exchange(grads, outputs, [n, n], before, after)


def _add_sibling_half(place, grads, received, wire, *, name):
    n = len(grads)

    def body(place_ref, *refs):
        k = pl.program_id(0)
        for a in range(n):
            pair = (refs[a][...] + refs[n + a][...]).astype(wire[a])
            refs[2 * n + a][...] = pair

            @pl.when(k == place_ref[1])
            def _(a=a, pair=pair):
                refs[3 * n + a][...] = pair

    def block(g):
        return (1, g.shape[1] // 2, g.shape[2])

    return pl.pallas_call(
        body,
        name=name,
        grid_spec=pltpu.PrefetchScalarGridSpec(
            num_scalar_prefetch=1,
            grid=(N_CHIPS,),
            in_specs=[pl.BlockSpec(block(g), lambda k, place_ref: (k, place_ref[0], 0)) for g in grads]
            + [pl.BlockSpec(block(g), lambda k, place_ref: (k, 0, 0)) for g in grads],
            out_specs=[pl.BlockSpec(block(g), lambda k, place_ref: (k, 0, 0)) for g in grads]
            + [pl.BlockSpec(block(g), lambda k, place_ref: (place_ref[1], 0, 0)) for g in grads],
        ),
        out_shape=[jax.ShapeDtypeStruct(r.shape, dt) for r, dt in zip(received, wire)] * 2,
        compiler_params=pltpu.CompilerParams(dimension_semantics=("arbitrary",), vmem_limit_bytes=VMEM_LIMIT),
    )(place, *grads, *received)


def _scatter_exchange(pairs, landing):
    n = len(pairs)

    def copy(refs, sems, a, j, px, py):
        x, y, c, _ = _place()
        return pltpu.make_async_remote_copy(
            src_ref=refs[a].at[2 * px + py], dst_ref=refs[n + a].at[2 * x + y], send_sem=sems[0].at[3 * a + j],
            recv_sem=sems[1].at[3 * a + j], device_id=(px, py, c), device_id_type=MESH,
        )

    def before(step, steps, refs, outs, sems):
        def send():
            for j, (px, py) in enumerate(_place()[3]):
                for a in range(n):
                    copy(refs, sems, a, j, px, py).start()

        _when(step == 0, send)

    def after(step, steps, refs, outs, sems):
        def finish():
            x, y, c, others = _place()
            for j, (px, py) in enumerate(others):
                for a in range(n):
                    landed = refs[n + a].at[2 * px + py]
                    pltpu.make_async_remote_copy(
                        src_ref=landed, dst_ref=landed, send_sem=sems[0].at[3 * a + j], recv_sem=sems[1].at[3 * a + j],
                        device_id=(px, py, c), device_id_type=MESH,
                    ).wait_recv()
            for j, (px, py) in enumerate(others):
                for a in range(n):
                    copy(refs, sems, a, j, px, py).wait_send()

        _when(step == steps - 1, finish)

    return _Exchange(pairs + landing, [(_sds(b), n + a) for a, b in enumerate(landing)], [3 * n, 3 * n], before, after)


SUM_STEPS = 2


def _sum_chip_blocks(place, parts, keep_chip_axis, *, name):
    n = len(parts)

    def body(place_ref, *refs):
        for a in range(n):
            p = refs[a]
            total = (p[0].astype(F32) + p[1].astype(F32)) + (p[2].astype(F32) + p[3].astype(F32))
            if keep_chip_axis[a]:
                refs[n + a][0] = total
            else:
                refs[n + a][...] = total

    def in_spec(p):
        return pl.BlockSpec((N_CHIPS, p.shape[1] // SUM_STEPS, p.shape[2]), lambda i, place_ref: (0, i, 0))

    def out_spec(p, keep):
        rows = p.shape[1] // SUM_STEPS
        if keep:
            return pl.BlockSpec((1, rows, p.shape[2]), lambda i, place_ref: (place_ref[1], place_ref[0] * SUM_STEPS + i, 0))
        return pl.BlockSpec((rows, p.shape[2]), lambda i, place_ref: (place_ref[0] * SUM_STEPS + i, 0))

    def out_shape(p, keep):
        shape = (2 * p.shape[1], p.shape[2])
        return jax.ShapeDtypeStruct((N_CHIPS,) + shape if keep else shape, F32)

    return pl.pallas_call(
        body,
        name=name,
        grid_spec=pltpu.PrefetchScalarGridSpec(
            num_scalar_prefetch=1,
            grid=(SUM_STEPS,),
            in_specs=[in_spec(p) for p in parts],
            out_specs=[out_spec(p, k) for p, k in zip(parts, keep_chip_axis)],
        ),
        out_shape=[out_shape(p, k) for p, k in zip(parts, keep_chip_axis)],
        compiler_params=pltpu.CompilerParams(dimension_semantics=("arbitrary",), vmem_limit_bytes=VMEM_LIMIT),
    )(place, *parts)


def _join_exchange(bufs, keep_chip_axis):
    n = len(bufs)
    kept = [a for a in range(n) if keep_chip_axis[a]]
    base = n

    def copy(refs, sems, block, k, to):
        return pltpu.make_async_remote_copy(
            src_ref=block, dst_ref=block, send_sem=sems[0].at[k], recv_sem=sems[1].at[k], device_id=to, device_id_type=MESH
        )

    def mine(refs, a, cc):
        x, y, _, _ = _place()
        return _half(refs[a].at[2 * x + y] if keep_chip_axis[a] else refs[a], cc, 0)

    def before(step, steps, refs, outs, sems):
        def send():
            x, y, c, others = _place()
            for a in range(n):
                copy(refs, sems, mine(refs, a, c), a, (x, y, 1 - c)).start()
            for i, a in enumerate(kept):
                for j, (px, py) in enumerate(others):
                    copy(refs, sems, mine(refs, a, c), base + 6 * i + j, (px, py, c)).start()

        _when(step == 0, send)

    def after(step, steps, refs, outs, sems):
        def pass_on():
            x, y, c, others = _place()
            for i, a in enumerate(kept):
                for j, (px, py) in enumerate(others):
                    landed = _half(refs[a].at[2 * px + py], c, 0)
                    copy(refs, sems, landed, base + 6 * i + j, (px, py, c)).wait_recv()
                    copy(refs, sems, landed, base + 6 * i + 3 + j, (x, y, 1 - c)).start()

        def finish():
            x, y, c, others = _place()
            for a in range(n):
                copy(refs, sems, mine(refs, a, 1 - c), a, (x, y, 1 - c)).wait_recv()
            for i, a in enumerate(kept):
                for j, (px, py) in enumerate(others):
                    passed = _half(refs[a].at[2 * px + py], 1 - c, 0)
                    copy(refs, sems, passed, base + 6 * i + 3 + j, (x, y, 1 - c)).wait_recv()
            for a in range(n):
                copy(refs, sems, mine(refs, a, c), a, (x, y, 1 - c)).wait_send()
            for i, a in enumerate(kept):
                for k in range(6):
                    copy(refs, sems, mine(refs, a, c), base + 6 * i + k, (x, y, 1 - c)).wait_send()

        _when(step == steps // 2, pass_on)
        _when(step == steps - 1, finish)

    return _Exchange(bufs, [(_sds(b), a) for a, b in enumerate(bufs)], [n + 6 * len(kept)] * 2, before, after)


def _adamw(w, g, m, v):
    m = ADAM_B1 * m + (1.0 - ADAM_B1) * g
    v = ADAM_B2 * v + (1.0 - ADAM_B2) * (g * g)
    m_hat = m / (1.0 - ADAM_B1**ADAM_STEP)
    v_hat = v / (1.0 - ADAM_B2**ADAM_STEP)
    delta = -ADAM_LR * (m_hat / (jnp.sqrt(v_hat) + ADAM_EPS) + ADAM_WD * w)
    return delta, m, v


def _adamw_large(w, m, v, layer_grads, *, name):
    depth, rows, cols = w.shape
    tile = math.gcd(rows, ADAM_TILE)
    assert tile % 8 == 0

    def body(w_ref, m_ref, v_ref, *refs):
        g_refs, (g_out, d_out, m_out, v_out) = refs[:depth], refs[depth:]
        for l in range(depth):

            @pl.when(pl.program_id(0) == l)
            def _(l=l):
                g = g_refs[l][...]
                delta, m_new, v_new = _adamw(w_ref[0], g, m_ref[0], v_ref[0])
                g_out[0], d_out[0], m_out[0], v_out[0] = g, delta, m_new, v_new

    def stacked():
        return pl.BlockSpec((1, tile, cols), lambda l, i: (l, i, 0))

    def layer_spec(l):
        return pl.BlockSpec((tile, cols), lambda k, i: (jnp.where(k == l, i, 0), 0))

    shape = jax.ShapeDtypeStruct(w.shape, F32)
    return pl.pallas_call(
        body,
        name=name,
        grid=(depth, rows // tile),
        in_specs=[stacked(), stacked(), stacked()] + [layer_spec(l) for l in range(depth)],
        out_specs=[stacked()] * 4,
        out_shape=[shape] * 4,
        compiler_params=pltpu.CompilerParams(dimension_semantics=("arbitrary", "arbitrary"), vmem_limit_bytes=VMEM_LIMIT),
    )(w, m, v, *layer_grads)


def _packed_pieces(name, l):
    vecs = ("sg_ln_g", "sg_ln_b", "pool_scale", "cc_dw_b", "cc_ln_g", "cc_ln_b")
    if name in vecs:
        r = RG_VEC + vecs.index(name)
        return [((slice(l, l + 1), slice(None)), slice(r, r + 1), slice(None))]
    if name in ("ln_g", "ln_b"):
        r = RG_LN + (4 if name == "ln_b" else 0)
        return [((slice(l, l + 1), slice(j * D_G, (j + 1) * D_G)), slice(r + j, r + j + 1), slice(None)) for j in range(4)]
    assert name == "sg_w"
    return [
        ((l, h), slice(RG_SGW + CHUNK * (h // 2), RG_SGW + CHUNK * (h // 2 + 1)), slice(CHUNK * (h % 2), CHUNK * (h % 2 + 1)))
        for h in range(N_SUB)
    ]


PACKED_NAMES = ("sg_ln_g", "sg_ln_b", "pool_scale", "cc_dw_b", "cc_ln_g", "cc_ln_b", "ln_g", "ln_b", "sg_w")


def _adamw_packed(packed, ws, ms, vs, *, name):
    n, depth = len(ws), len(packed)

    def body(*refs):
        packed_refs, refs = refs[:depth], refs[depth:]
        for a, leaf in enumerate(PACKED_NAMES):
            for l in range(depth):
                for at, rows, cols in _packed_pieces(leaf, l):
                    g = packed_refs[l][rows, cols]
                    delta, m_new, v_new = _adamw(refs[a][at], g, refs[n + a][at], refs[2 * n + a][at])
                    refs[3 * n + a][at] = g
                    refs[4 * n + a][at] = delta
                    refs[5 * n + a][at] = m_new
                    refs[6 * n + a][at] = v_new

    shapes = [jax.ShapeDtypeStruct(w.shape, F32) for w in ws]
    outs = pl.pallas_call(body, name=name, out_shape=shapes * 4)(*packed, *ws, *ms, *vs)
    return outs[:n], outs[n : 2 * n], outs[2 * n : 3 * n], outs[3 * n :]


def _adamw_small(ws, gs, ms, vs, *, name):
    n = len(ws)

    def body(*refs):
        for a in range(n):
            delta, m_new, v_new = _adamw(refs[a][...], refs[n + a][...], refs[2 * n + a][...], refs[3 * n + a][...])
            refs[4 * n + a][...] = delta
            refs[5 * n + a][...] = m_new
            refs[6 * n + a][...] = v_new

    shapes = [jax.ShapeDtypeStruct(w.shape, F32) for w in ws]
    outs = pl.pallas_call(body, name=name, out_shape=shapes * 3)(*ws, *gs, *ms, *vs)
    return outs[:n], outs[n : 2 * n], outs[2 * n :]


WEIGHT_NAMES = (
    "w_in", "conv_a_w", "sg_ln_g", "sg_ln_b", "sg_w", "sg_b", "pool_w", "pool_scale", "cc_dw_w", "cc_dw_b", "cc_ln_g",
    "cc_ln_b", "cc_pw_w", "w_kv", "w_out", "ln_g", "ln_b",
)
LARGE = ("w_in", "cc_pw_w", "w_kv", "w_out")
TAPS_ROWS = 48


def _unpack_small_grads(small, chip):
    out = {}
    for r, k in enumerate(("sg_ln_g", "sg_ln_b", "pool_scale", "cc_dw_b", "cc_ln_g", "cc_ln_b")):
        out[k] = small[RG_VEC + r]
    out["conv_a_w"] = lax.dynamic_slice_in_dim(small[RG_CONVA : RG_CONVA + CONV_A], chip * HEAD_DIM, HEAD_DIM, axis=1)
    out["cc_dw_w"] = lax.dynamic_slice_in_dim(small[RG_DW : RG_DW + CONV_D], chip * HEAD_DIM, HEAD_DIM, axis=1)
    cat = jnp.concatenate([small[RG_SGW : RG_SGW + CHUNK], small[RG_SGW + CHUNK : RG_SGW + 2 * CHUNK]], axis=1)
    out["sg_w"] = jnp.transpose(cat.reshape(CHUNK, N_SUB, CHUNK), (1, 0, 2))
    out["sg_b"] = small[RG_SGB : RG_SGB + CHUNK].reshape(CHUNK, N_SUB, HEAD_DIM).sum(-1).T
    pool = small[RG_POOL : RG_POOL + D_G]
    out["pool_w"] = jnp.stack(
        [pool[g * HEAD_DIM : (g + 1) * HEAD_DIM, g * HEAD_DIM : (g + 1) * HEAD_DIM] for g in range(N_SUB)]
    )
    out["ln_g"] = small[RG_LN : RG_LN + 4].reshape(D_MODEL)
    out["ln_b"] = small[RG_LN + 4 : RG_LN + 8].reshape(D_MODEL)
    return out


def kernel(x, mem, w_in, conv_a_w, sg_ln_g, sg_ln_b, sg_w, sg_b, pool_w, pool_scale, cc_dw_w, cc_dw_b, cc_ln_g, cc_ln_b, cc_pw_w, w_kv, w_out, ln_g, ln_b, loss_target, m_w_in, m_conv_a_w, m_sg_ln_g, m_sg_ln_b, m_sg_w, m_sg_b, m_pool_w, m_pool_scale, m_cc_dw_w, m_cc_dw_b, m_cc_ln_g, m_cc_ln_b, m_cc_pw_w, m_w_kv, m_w_out, m_ln_g, m_ln_b, v_w_in, v_conv_a_w, v_sg_ln_g, v_sg_ln_b, v_sg_w, v_sg_b, v_pool_w, v_pool_scale, v_cc_dw_w, v_cc_dw_b, v_cc_ln_g, v_cc_ln_b, v_cc_pw_w, v_w_kv, v_w_out, v_ln_g, v_ln_b):
    given = dict(locals())
    weights = {k: given[k] for k in WEIGHT_NAMES}
    chip = 2 * lax.axis_index("x") + lax.axis_index("y")
    place = jnp.stack([lax.axis_index("c"), chip]).astype(jnp.int32)

    x0, mem0 = x[0], mem[0]

    taps = jnp.concatenate([conv_a_w, cc_dw_w], axis=1)
    taps = jnp.pad(taps, ((0, 0), (0, TAPS_ROWS - taps.shape[1]), (0, 0)))

    def own_blocks(l):
        return _place_own_block(
            place, [w_in, w_out, w_kv, cc_pw_w, taps], l, [BF16, BF16, BF16, BF16, F32], name=f"place_weights{l}"
        )

    def layer_operands(l, gathered):
        g_in, g_out, g_kv, g_pw, g_taps = gathered
        taps_full = jnp.transpose(g_taps, (1, 0, 2)).reshape(TAPS_ROWS, D_G)
        full = dict(
            w_in=g_in,
            w_out=g_out.reshape(D_MIX, D_MODEL),
            w_kv=g_kv.reshape(D_MODEL, 2 * D_G),
            cc_pw_w=g_pw.reshape(D_G, D_G),
            conv_a_w=taps_full[0:CONV_A],
            cc_dw_w=taps_full[CONV_A : CONV_A + CONV_D],
            **{k: weights[k][l] for k in WEIGHT_NAMES if k not in LARGE + ("conv_a_w", "cc_dw_w")},
        )
        return _prepare_layer(mem0, full, l)

    def other_grads(l, bwd, small):
        _, _, dwout, dkst, dvst, dpw, _ = bwd
        return [
            dwout.reshape(N_CHIPS, D_MIX // N_CHIPS, D_MODEL),
            _kv_backward(mem0, dkst, dvst, name=f"kv_bwd{l}").reshape(N_CHIPS, D_MODEL // N_CHIPS, 2 * D_G),
            dpw.reshape(N_CHIPS, D_G // N_CHIPS, D_G),
            small.reshape(N_CHIPS, RG_ROWS // N_CHIPS, D_G),
        ]

    n_red = 5
    keep = [False, False, False, False, True]

    wire = [BF16, BF16, BF16, BF16, F32]

    def reduced_layer(joined):
        r_in, r_out, r_kv, r_pw, small_all = joined
        packed = small_all.reshape(RG_ROWS, D_G)
        out = _unpack_small_grads(packed, chip)
        out.update(w_in=r_in, w_out=r_out, w_kv=r_kv, cc_pw_w=r_pw, packed=packed)
        return out

    blocks0, blocks1 = own_blocks(0), own_blocks(1)
    p0 = layer_operands(0, _run_exchange(_gather_exchange(blocks0), name="gather_weights0"))
    fwd0, gathered1 = _forward(0, x0, p0, None, exchange=_gather_exchange(blocks1))
    p1 = layer_operands(1, gathered1)
    x1 = fwd0[3]
    fwd1, _ = _forward(1, x1, p1, loss_target[0])

    bwd1, _ = _backward(1, fwd1[3], fwd1, p1)
    small1 = bwd1[6].at[RG_LOSS, :].set(fwd1[5][0, 0])
    (dwin1,), _ = _input_weight_grad(x1, bwd1[0], name="w_in_grad1")
    grads1 = [dwin1] + other_grads(1, bwd1, small1)
    (dx1,), received1 = _input_grad(bwd1[0], bwd1[1], p1["win"], name="input_grad1", exchange=_swap_exchange(grads1))
    pairs1 = _add_sibling_half(place, grads1, received1, wire, name="rs_pair1")
    bwd0, parts1 = _backward(0, dx1, fwd0, p0, exchange=_scatter_exchange(pairs1[:n_red], pairs1[n_red:]))
    halves1 = _sum_chip_blocks(place, parts1, keep, name="rs_sum1")
    rest0 = other_grads(0, bwd0, bwd0[6])
    n_rest = len(rest0)
    pairs0_rest = _add_sibling_half(
        place, rest0, _run_exchange(_swap_exchange(rest0), name="rs_swap0_rest"), wire[1:], name="rs_pair0_rest"
    )
    (dwin0,), carried = _input_weight_grad(
        x0, bwd0[0], name="w_in_grad0",
        exchange=_both(_join_exchange(halves1, keep), _scatter_exchange(pairs0_rest[:n_rest], pairs0_rest[n_rest:])),
    )
    joined1, parts0_rest = carried[:n_red], carried[n_red:]
    loss = joined1[4].reshape(RG_ROWS, D_G)[RG_LOSS, 0]
    pairs0_win = _add_sibling_half(
        place, [dwin0], _run_exchange(_swap_exchange([dwin0]), name="rs_swap0_win"), wire[:1], name="rs_pair0_win"
    )
    (grad_x,), parts0_win = _input_grad(
        bwd0[0], bwd0[1], p0["win"], name="input_grad0", exchange=_scatter_exchange(pairs0_win[:1], pairs0_win[1:])
    )
    halves0 = _sum_chip_blocks(place, parts0_win + parts0_rest, keep, name="rs_sum0")
    reduced = [reduced_layer(_run_exchange(_join_exchange(halves0, keep), name="rs_join0")), reduced_layer(joined1)]

    grad, delta, new_m, new_v = {}, {}, {}, {}
    for k in LARGE:
        w3 = weights[k]
        grad[k], delta[k], new_m[k], new_v[k] = _adamw_large(
            w3, given["m_" + k], given["v_" + k], [reduced[l][k] for l in range(DEPTH)], name=f"adamw_{k}"
        )
    g_p, d_p, m_p, v_p = _adamw_packed(
        [reduced[l]["packed"] for l in range(DEPTH)],
        [weights[k] for k in PACKED_NAMES],
        [given["m_" + k] for k in PACKED_NAMES],
        [given["v_" + k] for k in PACKED_NAMES],
        name="adamw_packed",
    )
    for a, k in enumerate(PACKED_NAMES):
        grad[k], delta[k], new_m[k], new_v[k] = g_p[a], d_p[a], m_p[a], v_p[a]
    small_names = [k for k in WEIGHT_NAMES if k not in LARGE + PACKED_NAMES]
    for k in small_names:
        grad[k] = jnp.stack([reduced[l][k] for l in range(DEPTH)])
    d_s, m_s, v_s = _adamw_small(
        [weights[k] for k in small_names],
        [grad[k] for k in small_names],
        [given["m_" + k] for k in small_names],
        [given["v_" + k] for k in small_names],
        name="adamw_small",
    )
    for a, k in enumerate(small_names):
        delta[k], new_m[k], new_v[k] = d_s[a], m_s[a], v_s[a]

    return (
        loss,
        grad_x[None],
        *[grad[k] for k in WEIGHT_NAMES],
        *[delta[k] for k in WEIGHT_NAMES],
        *[new_m[k] for k in WEIGHT_NAMES],
        *[new_v[k] for k in WEIGHT_NAMES],
    )
```

```python
import math

import jax
import jax.numpy as jnp
from jax import lax
from jax.experimental import pallas as pl
from jax.experimental.pallas import tpu as pltpu

F32 = jnp.float32
BF16 = jnp.bfloat16

D_MODEL = 1024
DEPTH = 2
D_G = 256
D_MIX = 5 * D_G
D_IN = 9 * D_G + D_MIX
N_SUB = 4
HEAD_DIM = 64
CONV_A = 3
CONV_D = 31
CHUNK = 128
MEM_LEN = 256
N_CHIPS = 4
W_IN_SHARD = D_IN // N_CHIPS
LN_EPS = 1e-5
ALPHA = (2.0 * DEPTH) ** 0.25
ATT_SCALE = 1.0 / math.sqrt(HEAD_DIM)
GELU_C = math.sqrt(2.0 / math.pi)
GELU_A = 0.044715

ADAM_LR = 0.001
ADAM_B1 = 0.9
ADAM_B2 = 0.999
ADAM_EPS = 1e-08
ADAM_WD = 0.01
ADAM_STEP = 10

C_XA, C_BA, C_CA, C_U, C_V, C_XC, C_DA, C_DG, C_Q, C_GATE = (D_G * i for i in range(10))

HALO_A = 8
HALO_C = 16
HALO_D = 32

RW_VEC = 0
RW_CONVA = 16
RW_DW = 24
RW_SGB = 56

RG_VEC = 0
RG_CONVA = 16
RG_DW = 24
RG_SGW = 56
RG_SGB = RG_SGW + 2 * CHUNK
RG_POOL = RG_SGB + CHUNK
RG_LN = RG_POOL + D_G
RG_LOSS = 8
RG_ROWS = 768

VMEM_LIMIT = 62 * 1024 * 1024

AUX_CVD = 0
AUX_PM = D_G
AUX_P = 2 * D_G
AUX_COLS = AUX_P + N_SUB * MEM_LEN
SEQ_TILE = 256
FWD_TILE = 512
MM_TILE = 1024
W_GRAD_TILE = 4096
ADAM_TILE = 512

MESH = pl.DeviceIdType.MESH
ANY = pl.BlockSpec(memory_space=pl.ANY)
NT = (((1,), (1,)), ((), ()))
TN = (((0,), (0,)), ((), ()))


def _dot(a, b):
    return jnp.dot(a, b, preferred_element_type=F32)


def _dot_nt(a, b):
    return lax.dot_general(a, b, NT, preferred_element_type=F32)


def _dot_tn(a, b):
    return lax.dot_general(a, b, TN, preferred_element_type=F32)


def _full(shape):
    zeros = (0,) * len(shape)
    return pl.BlockSpec(shape, lambda *_: zeros)


class _Exchange:
    def __init__(self, operands, outputs, sem_counts, before, after):
        self.operands, self.outputs, self.sem_counts, self.before, self.after = operands, outputs, sem_counts, before, after
        self.late_start = False

    def starting_late(self):
        self.late_start = True
        return self

    def specs(self, first_input, first_output):
        aliases = {first_input + src: first_output + j for j, (_, src) in enumerate(self.outputs) if src is not None}
        return (
            [ANY] * len(self.operands),
            [ANY] * len(self.outputs),
            [sds for sds, _ in self.outputs],
            [pltpu.SemaphoreType.DMA((k,)) for k in self.sem_counts],
            aliases,
        )

    def split(self, ins, outs):
        refs = list(ins)
        for j, (_, src) in enumerate(self.outputs):
            if src is not None:
                refs[src] = outs[j]
        return refs


def _both(first, second):
    n1, m1, s1 = len(first.operands), len(first.outputs), len(first.sem_counts)
    outputs = first.outputs + [(sds, None if src is None else n1 + src) for sds, src in second.outputs]

    def before(step, steps, refs, outs, sems):
        first.before(step, steps, refs[:n1], outs[:m1], sems[:s1])
        second.before(step, steps, refs[n1:], outs[m1:], sems[s1:])

    def after(step, steps, refs, outs, sems):
        first.after(step, steps, refs[:n1], outs[:m1], sems[:s1])
        second.after(step, steps, refs[n1:], outs[m1:], sems[s1:])

    return _Exchange(first.operands + second.operands, outputs, first.sem_counts + second.sem_counts, before, after)


def _when(cond, fn):
    if isinstance(cond, bool):
        if cond:
            fn()
    else:
        pl.when(cond)(fn)


def _run_exchange(exchange, *, name):
    n_in, n_out = len(exchange.operands), len(exchange.outputs)
    in_specs, out_specs, out_shape, sems, aliases = exchange.specs(0, 0)

    def body(*refs):
        ins, outs, sem_refs = refs[:n_in], refs[n_in : n_in + n_out], refs[n_in + n_out :]
        refs = exchange.split(ins, outs)
        exchange.before(0, 1, refs, outs, sem_refs)
        exchange.after(0, 1, refs, outs, sem_refs)

    return pl.pallas_call(
        body, name=name, in_specs=in_specs, out_specs=out_specs, out_shape=out_shape, scratch_shapes=sems,
        input_output_aliases=aliases,
    )(*exchange.operands)


def _gridded_call(body, *, name, steps, in_specs, out_specs, out_shape, scratch_shapes, operands, exchange=None):
    params = pltpu.CompilerParams(dimension_semantics=("arbitrary",), vmem_limit_bytes=VMEM_LIMIT)
    if exchange is None:
        outs = pl.pallas_call(
            body, name=name, grid=(steps,), in_specs=in_specs, out_specs=out_specs, out_shape=out_shape,
            scratch_shapes=scratch_shapes, compiler_params=params,
        )(*operands)
        return list(outs), []
    n_in, n_out, n_scr = len(in_specs), len(out_specs), len(scratch_shapes)
    x_in, x_out = len(exchange.operands), len(exchange.outputs)
    ex_in_specs, ex_out_specs, ex_out_shape, ex_sems, aliases = exchange.specs(n_in, n_out)

    def full(*refs):
        own_in, refs = refs[:n_in], refs[n_in:]
        ex_in, refs = refs[:x_in], refs[x_in:]
        own_out, refs = refs[:n_out], refs[n_out:]
        ex_out, refs = refs[:x_out], refs[x_out:]
        own_scr, sem_refs = refs[:n_scr], refs[n_scr:]
        ex_refs = exchange.split(ex_in, ex_out)
        step = pl.program_id(0)
        if not exchange.late_start:
            exchange.before(step, steps, ex_refs, ex_out, sem_refs)
        body(*own_in, *own_out, *own_scr)
        if exchange.late_start:
            exchange.before(step, steps, ex_refs, ex_out, sem_refs)
        exchange.after(step, steps, ex_refs, ex_out, sem_refs)

    outs = pl.pallas_call(
        full, name=name, grid=(steps,), in_specs=in_specs + ex_in_specs, out_specs=out_specs + ex_out_specs,
        out_shape=out_shape + ex_out_shape, scratch_shapes=scratch_shapes + ex_sems, input_output_aliases=aliases,
        compiler_params=params,
    )(*operands, *exchange.operands)
    return list(outs[:n_out]), list(outs[n_out:])


def _sigmoid(x):
    return 0.5 * jnp.tanh(0.5 * x) + 0.5


def _gelu(x):
    t = jnp.tanh(GELU_C * (x + GELU_A * x * x * x))
    return 0.5 * x * (1.0 + t), t


def _gelu_grad(x, t):
    return 0.5 * (1.0 + t) + 0.5 * x * (1.0 - t * t) * (GELU_C * (1.0 + 3.0 * GELU_A * x * x))


def _normalize(v):
    mu = jnp.mean(v, axis=-1, keepdims=True)
    d = v - mu
    var = jnp.mean(d * d, axis=-1, keepdims=True)
    rstd = lax.rsqrt(var + LN_EPS)
    return d * rstd, rstd


def _normalize_grad(dhat, hat, rstd):
    m1 = jnp.mean(dhat, axis=-1, keepdims=True)
    m2 = jnp.mean(dhat * hat, axis=-1, keepdims=True)
    return rstd * (dhat - m1 - hat * m2)


def _lane(width=D_G):
    return lax.broadcasted_iota(jnp.int32, (1, width), 1)


def _head_masks():
    head = _lane() // HEAD_DIM
    return [(head == h).astype(F32) for h in range(N_SUB)]


def _stack_heads(v, masks):
    return jnp.concatenate([v * m for m in masks], axis=0)


def _tril_mask_cat():
    t = lax.broadcasted_iota(jnp.int32, (CHUNK, N_SUB * CHUNK), 0)
    s = lax.broadcasted_iota(jnp.int32, (CHUNK, N_SUB * CHUNK), 1) % CHUNK
    return s <= t


def _triu_mask_cat():
    s = lax.broadcasted_iota(jnp.int32, (CHUNK, N_SUB * CHUNK), 0)
    t = lax.broadcasted_iota(jnp.int32, (CHUNK, N_SUB * CHUNK), 1) % CHUNK
    return t >= s


def _pool_select(a2, a4, a8, a16):
    lane = _lane()
    return jnp.where(lane < 64, a2, jnp.where(lane < 128, a4, jnp.where(lane < 192, a8, a16)))


def _pool_inv_count(row0, rows):
    t = row0 + lax.broadcasted_iota(jnp.int32, (HALO_C, D_G), 0)
    lane = lax.broadcasted_iota(jnp.int32, (HALO_C, D_G), 1)
    win = jnp.where(lane < 64, 2, jnp.where(lane < 128, 4, jnp.where(lane < 192, 8, 16)))
    head = 1.0 / jnp.minimum(t + 1, win).astype(F32)
    inv_win = jnp.broadcast_to(_pool_select(0.5, 0.25, 0.125, 0.0625), (rows - HALO_C, D_G))
    return jnp.concatenate([head, inv_win], axis=0)


def _trailing_window_sum(halo, cur):
    e = jnp.concatenate([halo, cur], axis=0)
    s2 = e + pltpu.roll(e, 1, 0)
    s4 = s2 + pltpu.roll(s2, 2, 0)
    s8 = s4 + pltpu.roll(s4, 4, 0)
    s16 = s8 + pltpu.roll(s8, 8, 0)
    return _pool_select(s2, s4, s8, s16)[HALO_C:]


def _leading_window_sum(cur, halo):
    e = jnp.concatenate([cur, halo], axis=0)
    n = e.shape[0]
    s2 = e + pltpu.roll(e, n - 1, 0)
    s4 = s2 + pltpu.roll(s2, n - 2, 0)
    s8 = s4 + pltpu.roll(s4, n - 4, 0)
    s16 = s8 + pltpu.roll(s8, n - 8, 0)
    return _pool_select(s2, s4, s8, s16)[: cur.shape[0]]


def _softmax_blocks(sc):
    out = []
    for h in range(N_SUB):
        s = sc[:, h * MEM_LEN : (h + 1) * MEM_LEN]
        e = jnp.exp(s - jnp.max(s, axis=-1, keepdims=True))
        out.append(e * (1.0 / jnp.sum(e, axis=-1, keepdims=True)))
    return jnp.concatenate(out, axis=-1)


STRIP = 32
SHIFTS = 8


def _fill_shifts(buf):
    n = buf.shape[1] - SHIFTS
    for r in range(1, SHIFTS):
        buf[r, 0:n, :] = buf[0, r : r + n, :]


def _shifted(buf, off, rows):
    r = off % SHIFTS
    return buf[r, off - r : off - r + rows, :]


def _sgu_mix(vn, wcat_b, sgb, masks):
    vbd = _stack_heads(vn, masks).astype(BF16)
    return _dot(wcat_b, vbd) + sgb, vbd


def _layer_forward(x, win, kst, vst, wout, sw, wcat, poolw, pww, ln, tgt, *, name, exchange=None):
    seq = x.shape[0]
    tile = min(FWD_TILE, seq)
    n_tiles = seq // tile
    last = tgt is not None

    def body(*refs):
        x_ref, win_ref, kst_ref, vst_ref, wout_ref, sw_ref, wcat_ref, pool_ref, pw_ref, ln_ref = refs[:10]
        refs = refs[10:]
        if last:
            tgt_ref, refs = refs[0], refs[1:]
        proj_ref, y_ref, z_ref, out_ref, aux_ref = refs[:5]
        refs = refs[5:]
        if last:
            loss_ref, refs = refs[0], refs[1:]
        pbuf, xchalo, gbuf = refs
        i = pl.program_id(0)

        @pl.when(i == 0)
        def _():
            pbuf[0:HALO_A, :] = jnp.zeros((HALO_A, D_G), F32)
            xchalo[...] = jnp.zeros((HALO_C, D_G), F32)
            gbuf[0, 0:HALO_D, :] = jnp.zeros((HALO_D, D_G), F32)
            if last:
                loss_ref[...] = jnp.zeros((8, 128), F32)

        xt = x_ref[...]
        xb = xt.astype(BF16)

        blocks = {}

        def project(k):
            blocks[k] = _dot(xb, win_ref[k])
            proj_ref[:, k * W_IN_SHARD : (k + 1) * W_IN_SHARD] = blocks[k]

        def cols(start, width=D_G):
            parts, c = [], start
            while c < start + width:
                k, lo = divmod(c, W_IN_SHARD)
                hi = min(W_IN_SHARD, lo + start + width - c)
                parts.append(blocks[k][:, lo:hi])
                c += hi - lo
            return parts[0] if len(parts) == 1 else jnp.concatenate(parts, axis=1)

        project(0)
        project(1)
        masks = _head_masks()

        pbuf[HALO_A : HALO_A + tile, :] = cols(C_CA) * cols(C_XA)
        cv = jnp.zeros((tile, D_G), F32)
        for k in range(CONV_A):
            off = HALO_A - (CONV_A - 1) + k
            cv = cv + sw_ref[RW_CONVA + k : RW_CONVA + k + 1, :] * pbuf[off : off + tile, :]
        y_ref[:, 0:D_G] = cols(C_BA) * cv
        pbuf[0:HALO_A, :] = pbuf[tile : tile + HALO_A, :]

        project(2)

        ua, _ = _gelu(cols(C_U))
        vg, _ = _gelu(cols(C_V))
        vhat, _ = _normalize(vg)
        vn = vhat * sw_ref[RW_VEC : RW_VEC + 1, :] + sw_ref[RW_VEC + 1 : RW_VEC + 2, :]
        wcat_b = jnp.where(_tril_mask_cat(), wcat_ref[...], 0.0).astype(BF16)
        sgb = sw_ref[RW_SGB : RW_SGB + CHUNK, :]
        for j in range(tile // CHUNK):
            rows = slice(j * CHUNK, (j + 1) * CHUNK)
            mixed, _ = _sgu_mix(vn[rows], wcat_b, sgb, masks)
            y_ref[rows, D_G : 2 * D_G] = ua[rows] * mixed

        xc = cols(C_XC)
        wsum = _trailing_window_sum(xchalo[...], xc)
        pm = wsum * _pool_inv_count(i * tile, tile) - xc
        aux_ref[:, AUX_PM : AUX_PM + D_G] = pm
        y_ref[:, 2 * D_G : 3 * D_G] = _dot(pm.astype(BF16), pool_ref[...]) * sw_ref[RW_VEC + 2 : RW_VEC + 3, :]
        xchalo[...] = xc[tile - HALO_C :, :]

        project(3)

        gbuf[0, HALO_D : HALO_D + tile, :] = cols(C_DA) * _sigmoid(cols(C_DG))
        _fill_shifts(gbuf)
        for r0 in range(0, tile, STRIP):
            acc = jnp.zeros((STRIP, D_G), F32) + sw_ref[RW_VEC + 3 : RW_VEC + 4, :]
            for k in range(CONV_D):
                off = HALO_D - (CONV_D - 1) + k
                acc = acc + sw_ref[RW_DW + k : RW_DW + k + 1, :] * _shifted(gbuf, off + r0, STRIP)
            aux_ref[r0 : r0 + STRIP, AUX_CVD : AUX_CVD + D_G] = acc
        nhat, _ = _normalize(aux_ref[:, AUX_CVD : AUX_CVD + D_G])
        nrm = nhat * sw_ref[RW_VEC + 4 : RW_VEC + 5, :] + sw_ref[RW_VEC + 5 : RW_VEC + 6, :]
        y_ref[:, 3 * D_G : 4 * D_G] = _dot((nrm * _sigmoid(nrm)).astype(BF16), pw_ref[...])
        gbuf[0, 0:HALO_D, :] = gbuf[0, tile : tile + HALO_D, :]

        qb = cols(C_Q).astype(BF16)
        p_all = _softmax_blocks(_dot_nt(qb, kst_ref[...]) * ATT_SCALE)
        aux_ref[:, AUX_P:] = p_all
        y_ref[:, 4 * D_G : 5 * D_G] = _dot(p_all.astype(BF16), vst_ref[...])

        gate = cols(C_GATE, D_MIX)
        hid = y_ref[...] * (gate * _sigmoid(gate))
        z = ALPHA * xt + _dot(hid.astype(BF16), wout_ref[...])
        z_ref[...] = z
        zhat, _ = _normalize(z)
        xn = zhat * ln_ref[0:1, :] + ln_ref[1:2, :]
        if last:
            err = xn - tgt_ref[...]
            out_ref[...] = err * (1.0 / D_MODEL)
            loss_ref[...] += jnp.sum(err * err) * (0.5 / D_MODEL)
        else:
            out_ref[...] = xn

    def rows(width):
        return pl.BlockSpec((tile, width), lambda i: (i, 0))

    operands = [x, win, kst, vst, wout, sw, wcat, poolw, pww, ln]
    in_specs = [rows(D_MODEL)] + [_full(a.shape) for a in operands[1:]]
    widths = [D_IN, D_MIX, D_MODEL, D_MODEL, AUX_COLS]
    out_shape = [jax.ShapeDtypeStruct((seq, w), F32) for w in widths]
    out_specs = [rows(w) for w in widths]
    if last:
        operands.append(tgt)
        in_specs.append(rows(D_MODEL))
        out_shape.append(jax.ShapeDtypeStruct((8, 128), F32))
        out_specs.append(_full((8, 128)))
    return _gridded_call(
        body,
        name=name,
        steps=n_tiles,
        in_specs=in_specs,
        out_specs=out_specs,
        out_shape=out_shape,
        scratch_shapes=[
            pltpu.VMEM((HALO_A + tile, D_G), F32),
            pltpu.VMEM((HALO_C, D_G), F32),
            pltpu.VMEM((SHIFTS, HALO_D + tile, D_G), F32),
        ],
        operands=operands,
        exchange=exchange,
    )


def _layer_backward(dxn, z, proj, y, cvd, kst, vst, wout, sw, wcat, wcat_t, poolw, pww, ln, *, name, exchange=None):
    seq = dxn.shape[0]
    tile = min(SEQ_TILE, seq)
    n_tiles = seq // tile
    halo_blocks = tile // HALO_D

    def body(
        dxn_ref, z_ref, proj_ref, halo_ref, y_ref, aux_ref, kst_ref, vst_ref, wout_ref, sw_ref, wcat_ref, wcat_t_ref,
        pool_ref, pw_ref, ln_ref, dproj_ref, dz_ref, dwout_ref, dkst_ref, dvst_ref, dpw_ref, sg_ref,
        pbuf, dcvbuf, rhalo, gbuf, dgbuf, dwacc,
    ):
        i = pl.program_id(0)
        ti = n_tiles - 1 - i

        @pl.when(i == 0)
        def _():
            dwout_ref[...] = jnp.zeros(dwout_ref.shape, F32)
            dkst_ref[...] = jnp.zeros(dkst_ref.shape, F32)
            dvst_ref[...] = jnp.zeros(dvst_ref.shape, F32)
            dpw_ref[...] = jnp.zeros(dpw_ref.shape, F32)
            sg_ref[...] = jnp.zeros(sg_ref.shape, F32)
            dwacc[...] = jnp.zeros(dwacc.shape, F32)
            dcvbuf[tile : tile + HALO_A, :] = jnp.zeros((HALO_A, D_G), F32)
            rhalo[...] = jnp.zeros((HALO_C, D_G), F32)
            dgbuf[0, tile : tile + HALO_D, :] = jnp.zeros((HALO_D, D_G), F32)

        def acc_row(row, val):
            sg_ref[row : row + 1, :] += jnp.sum(val, axis=0, keepdims=True)

        masks = _head_masks()
        has_past = (ti > 0).astype(F32)

        zhat, zrstd = _normalize(z_ref[...])
        dxn_t = dxn_ref[...]
        dlg = jnp.sum(dxn_t * zhat, axis=0, keepdims=True)
        dlb = jnp.sum(dxn_t, axis=0, keepdims=True)
        for j in range(D_MODEL // D_G):
            sg_ref[RG_LN + j : RG_LN + j + 1, :] += dlg[:, j * D_G : (j + 1) * D_G]
            sg_ref[RG_LN + 4 + j : RG_LN + 5 + j, :] += dlb[:, j * D_G : (j + 1) * D_G]
        dz = _normalize_grad(dxn_t * ln_ref[0:1, :], zhat, zrstd)
        dz_ref[...] = dz
        dzb = dz.astype(BF16)

        gate = proj_ref[:, C_GATE:]
        sgm = _sigmoid(gate)
        silu = gate * sgm
        yc = y_ref[...]
        dwout_ref[...] += _dot_tn((yc * silu).astype(BF16), dzb)
        dh = _dot_nt(dzb, wout_ref[...])
        dproj_ref[:, C_GATE:] = (dh * yc * (sgm * (1.0 + gate * (1.0 - sgm)))).astype(BF16)
        dy = dh * silu

        dya = dy[:, 0:D_G]
        xa = proj_ref[:, C_XA : C_XA + D_G]
        ba = proj_ref[:, C_BA : C_BA + D_G]
        ca = proj_ref[:, C_CA : C_CA + D_G]
        past = slice(HALO_D - HALO_A, HALO_D)
        pbuf[0:HALO_A, :] = halo_ref[past, C_CA : C_CA + D_G] * halo_ref[past, C_XA : C_XA + D_G] * has_past
        pbuf[HALO_A : HALO_A + tile, :] = ca * xa
        cv = jnp.zeros((tile, D_G), F32)
        for k in range(CONV_A):
            off = HALO_A - (CONV_A - 1) + k
            cv = cv + sw_ref[RW_CONVA + k : RW_CONVA + k + 1, :] * pbuf[off : off + tile, :]
        dproj_ref[:, C_BA : C_BA + D_G] = (dya * cv).astype(BF16)
        dcv = dya * ba
        dcvbuf[0:tile, :] = dcv
        dp = jnp.zeros((tile, D_G), F32)
        for k in range(CONV_A):
            off = HALO_A - (CONV_A - 1) + k
            acc_row(RG_CONVA + k, dcv * pbuf[off : off + tile, :])
            back = CONV_A - 1 - k
            dp = dp + sw_ref[RW_CONVA + k : RW_CONVA + k + 1, :] * dcvbuf[back : back + tile, :]
        dproj_ref[:, C_CA : C_CA + D_G] = (dp * xa).astype(BF16)
        dproj_ref[:, C_XA : C_XA + D_G] = (dp * ca).astype(BF16)
        dcvbuf[tile : tile + HALO_A, :] = dcvbuf[0:HALO_A, :]

        dyb = dy[:, D_G : 2 * D_G]
        u = proj_ref[:, C_U : C_U + D_G]
        v = proj_ref[:, C_V : C_V + D_G]
        ua, ut = _gelu(u)
        vg, vt = _gelu(v)
        vhat, vrstd = _normalize(vg)
        sg_g = sw_ref[RW_VEC : RW_VEC + 1, :]
        vn = vhat * sg_g + sw_ref[RW_VEC + 1 : RW_VEC + 2, :]
        tril = _tril_mask_cat()
        wcat_b = jnp.where(tril, wcat_ref[...], 0.0).astype(BF16)
        wcat_tb = jnp.where(_triu_mask_cat(), wcat_t_ref[...], 0.0).astype(BF16)
        sgb = sw_ref[RW_SGB : RW_SGB + CHUNK, :]
        dmixed = dyb * ua
        dvn_parts = []
        du_parts = []
        dwcat = jnp.zeros((CHUNK, N_SUB * CHUNK), F32)
        dsgb = jnp.zeros((CHUNK, D_G), F32)
        for j in range(tile // CHUNK):
            rows = slice(j * CHUNK, (j + 1) * CHUNK)
            mixed, vbd = _sgu_mix(vn[rows], wcat_b, sgb, masks)
            du_parts.append(dyb[rows] * mixed)
            dmx = dmixed[rows]
            dsgb = dsgb + dmx
            dwcat = dwcat + _dot_nt(dmx.astype(BF16), vbd)
            dvn_parts.append(_dot(wcat_tb, _stack_heads(dmx, masks).astype(BF16)))
        dwcat = jnp.where(tril, dwcat, 0.0)
        sg_ref[RG_SGW : RG_SGW + CHUNK, :] += dwcat[:, 0:D_G]
        sg_ref[RG_SGW + CHUNK : RG_SGW + 2 * CHUNK, :] += dwcat[:, D_G:]
        sg_ref[RG_SGB : RG_SGB + CHUNK, :] += dsgb
        dvn = jnp.concatenate(dvn_parts, axis=0)
        du_act = jnp.concatenate(du_parts, axis=0)
        acc_row(RG_VEC, dvn * vhat)
        acc_row(RG_VEC + 1, dvn)
        dvg = _normalize_grad(dvn * sg_g, vhat, vrstd)
        dproj_ref[:, C_U : C_U + D_G] = (du_act * _gelu_grad(u, ut)).astype(BF16)
        dproj_ref[:, C_V : C_V + D_G] = (dvg * _gelu_grad(v, vt)).astype(BF16)

        dyc = dy[:, 2 * D_G : 3 * D_G]
        inv_cnt = _pool_inv_count(ti * tile, tile)
        pmb = aux_ref[:, AUX_PM : AUX_PM + D_G].astype(BF16)
        pool_b = pool_ref[...]
        scale = sw_ref[RW_VEC + 2 : RW_VEC + 3, :]
        acc_row(RG_VEC + 2, dyc * _dot(pmb, pool_b))
        dpre = (dyc * scale).astype(BF16)
        sg_ref[RG_POOL : RG_POOL + D_G, :] += _dot_tn(pmb, dpre)
        dpm = _dot_nt(dpre, pool_b)
        r = dpm * inv_cnt
        dproj_ref[:, C_XC : C_XC + D_G] = (_leading_window_sum(r, rhalo[...]) - dpm).astype(BF16)
        rhalo[...] = r[0:HALO_C, :]

        dyd = dy[:, 3 * D_G : 4 * D_G]
        da = proj_ref[:, C_DA : C_DA + D_G]
        sgd = _sigmoid(proj_ref[:, C_DG : C_DG + D_G])
        gbuf[0, 0:HALO_D, :] = halo_ref[:, C_DA : C_DA + D_G] * _sigmoid(halo_ref[:, C_DG : C_DG + D_G]) * has_past
        gbuf[0, HALO_D : HALO_D + tile, :] = da * sgd
        _fill_shifts(gbuf)
        nhat, nrstd = _normalize(aux_ref[:, AUX_CVD : AUX_CVD + D_G])
        cc_g = sw_ref[RW_VEC + 4 : RW_VEC + 5, :]
        nrm = nhat * cc_g + sw_ref[RW_VEC + 5 : RW_VEC + 6, :]
        sgn = _sigmoid(nrm)
        dydb = dyd.astype(BF16)
        dpw_ref[...] += _dot_tn((nrm * sgn).astype(BF16), dydb)
        dn = _dot_nt(dydb, pw_ref[...]) * (sgn * (1.0 + nrm * (1.0 - sgn)))
        acc_row(RG_VEC + 4, dn * nhat)
        acc_row(RG_VEC + 5, dn)
        dcvd = _normalize_grad(dn * cc_g, nhat, nrstd)
        acc_row(RG_VEC + 3, dcvd)
        dgbuf[0, 0:tile, :] = dcvd
        _fill_shifts(dgbuf)
        for r0 in range(0, tile, STRIP):
            d_s = dgbuf[0, r0 : r0 + STRIP, :]
            dg = jnp.zeros((STRIP, D_G), F32)
            for k in range(CONV_D):
                off = HALO_D - (CONV_D - 1) + k
                prod = d_s * _shifted(gbuf, off + r0, STRIP)
                part = prod[0:8]
                for q in range(8, STRIP, 8):
                    part = part + prod[q : q + 8]
                dwacc[8 * k : 8 * k + 8, :] += part
                back = CONV_D - 1 - k
                dg = dg + sw_ref[RW_DW + k : RW_DW + k + 1, :] * _shifted(dgbuf, back + r0, STRIP)
            da_s = proj_ref[r0 : r0 + STRIP, C_DA : C_DA + D_G]
            sgd_s = _sigmoid(proj_ref[r0 : r0 + STRIP, C_DG : C_DG + D_G])
            dproj_ref[r0 : r0 + STRIP, C_DA : C_DA + D_G] = (dg * sgd_s).astype(BF16)
            dproj_ref[r0 : r0 + STRIP, C_DG : C_DG + D_G] = (dg * da_s * sgd_s * (1.0 - sgd_s)).astype(BF16)
        dgbuf[0, tile : tile + HALO_D, :] = dgbuf[0, 0:HALO_D, :]

        @pl.when(i == n_tiles - 1)
        def _():
            for k in range(CONV_D):
                sg_ref[RG_DW + k : RG_DW + k + 1, :] = jnp.sum(dwacc[8 * k : 8 * k + 8, :], axis=0, keepdims=True)

        dyeb = dy[:, 4 * D_G : 5 * D_G].astype(BF16)
        qb = proj_ref[:, C_Q : C_Q + D_G].astype(BF16)
        kst_b = kst_ref[...]
        p_all = aux_ref[:, AUX_P:]
        dvst_ref[...] += _dot_tn(p_all.astype(BF16), dyeb)
        dp_all = _dot_nt(dyeb, vst_ref[...])
        ds = []
        for h in range(N_SUB):
            blk = slice(h * MEM_LEN, (h + 1) * MEM_LEN)
            p, dpb = p_all[:, blk], dp_all[:, blk]
            ds.append(p * (dpb - jnp.sum(dpb * p, axis=-1, keepdims=True)))
        dsb = (jnp.concatenate(ds, axis=-1) * ATT_SCALE).astype(BF16)
        dproj_ref[:, C_Q : C_Q + D_G] = _dot(dsb, kst_b).astype(BF16)
        dkst_ref[...] += _dot_tn(dsb, qb)

    def rows(width):
        return pl.BlockSpec((tile, width), lambda i: (n_tiles - 1 - i, 0))

    halo_spec = pl.BlockSpec((HALO_D, D_IN), lambda i: (jnp.maximum((n_tiles - 1 - i) * halo_blocks - 1, 0), 0))
    weights = [kst, vst, wout, sw, wcat, wcat_t, poolw, pww, ln]
    acc_shapes = [(D_MIX, D_MODEL), (N_SUB * MEM_LEN, D_G), (N_SUB * MEM_LEN, D_G), (D_G, D_G), (RG_ROWS, D_G)]
    return _gridded_call(
        body,
        name=name,
        steps=n_tiles,
        in_specs=[rows(D_MODEL), rows(D_MODEL), rows(D_IN), halo_spec, rows(D_MIX), rows(AUX_COLS)]
        + [_full(a.shape) for a in weights],
        out_specs=[rows(D_IN), rows(D_MODEL)] + [_full(s) for s in acc_shapes],
        out_shape=[jax.ShapeDtypeStruct((seq, D_IN), BF16), jax.ShapeDtypeStruct((seq, D_MODEL), F32)]
        + [jax.ShapeDtypeStruct(s, F32) for s in acc_shapes],
        scratch_shapes=[
            pltpu.VMEM((HALO_A + tile, D_G), F32),
            pltpu.VMEM((tile + HALO_A, D_G), F32),
            pltpu.VMEM((HALO_C, D_G), F32),
            pltpu.VMEM((SHIFTS, HALO_D + tile, D_G), F32),
            pltpu.VMEM((SHIFTS, tile + HALO_D, D_G), F32),
            pltpu.VMEM((8 * CONV_D, D_G), F32),
        ],
        operands=[dxn, z, proj, proj, y, cvd, *weights],
        exchange=exchange,
    )


def _kv_forward(mem, wkv, *, name):
    def body(mem_ref, wkv_ref, kst_ref, vst_ref):
        kv = _dot(mem_ref[...].astype(BF16), wkv_ref[...])
        masks = _head_masks()
        kst_ref[...] = _stack_heads(kv[:, 0:D_G], masks).astype(BF16)
        vst_ref[...] = _stack_heads(kv[:, D_G:], masks).astype(BF16)

    shape = jax.ShapeDtypeStruct((N_SUB * MEM_LEN, D_G), BF16)
    return pl.pallas_call(body, name=name, out_shape=[shape, shape])(mem, wkv)


def _kv_backward(mem, dkst, dvst, *, name):
    def body(mem_ref, dkst_ref, dvst_ref, dwkv_ref):
        masks = _head_masks()
        memb = mem_ref[...].astype(BF16)
        for col, ref in ((0, dkst_ref), (D_G, dvst_ref)):
            d = jnp.zeros((MEM_LEN, D_G), F32)
            for h in range(N_SUB):
                d = d + ref[h * MEM_LEN : (h + 1) * MEM_LEN, :] * masks[h]
            dwkv_ref[:, col : col + D_G] = _dot_tn(memb, d.astype(BF16))

    return pl.pallas_call(body, name=name, out_shape=jax.ShapeDtypeStruct((D_MODEL, 2 * D_G), F32))(mem, dkst, dvst)


def _input_grad(dproj, dz, win, *, name, exchange=None):
    seq = dproj.shape[0]
    tile = min(MM_TILE // 2, seq)

    def body(dproj_ref, dz_ref, win_ref, dx_ref):
        acc = ALPHA * dz_ref[...]
        for k in range(N_CHIPS):
            acc = acc + _dot_nt(dproj_ref[:, k * W_IN_SHARD : (k + 1) * W_IN_SHARD], win_ref[k])
        dx_ref[...] = acc

    return _gridded_call(
        body,
        name=name,
        steps=seq // tile,
        in_specs=[
            pl.BlockSpec((tile, D_IN), lambda i: (i, 0)),
            pl.BlockSpec((tile, D_MODEL), lambda i: (i, 0)),
            _full(win.shape),
        ],
        out_specs=[pl.BlockSpec((tile, D_MODEL), lambda i: (i, 0))],
        out_shape=[jax.ShapeDtypeStruct((seq, D_MODEL), F32)],
        scratch_shapes=[],
        operands=[dproj, dz, win],
        exchange=exchange,
    )


def _input_weight_grad(x, dproj, *, name, exchange=None):
    seq = x.shape[0]
    tile = min(W_GRAD_TILE, seq)
    n_rows = seq // tile

    def body(x_ref, dproj_ref, dwin_ref):
        part = _dot_tn(x_ref[...].astype(BF16), dproj_ref[...])
        if n_rows == 1:
            dwin_ref[0] = part
        else:

            @pl.when(pl.program_id(0) % n_rows == 0)
            def _():
                dwin_ref[...] = jnp.zeros(dwin_ref.shape, F32)

            dwin_ref[0] += part

    return _gridded_call(
        body,
        name=name,
        steps=N_CHIPS * n_rows,
        in_specs=[
            pl.BlockSpec((tile, D_MODEL), lambda s: (s % n_rows, 0)),
            pl.BlockSpec((tile, W_IN_SHARD), lambda s: (s % n_rows, s // n_rows)),
        ],
        out_specs=[pl.BlockSpec((1, D_MODEL, W_IN_SHARD), lambda s: (s // n_rows, 0, 0))],
        out_shape=[jax.ShapeDtypeStruct((N_CHIPS, D_MODEL, W_IN_SHARD), F32)],
        scratch_shapes=[],
        operands=[x, dproj],
        exchange=exchange,
    )


def _expand_sgb(sg_b):
    return jnp.repeat(sg_b.T, HEAD_DIM, axis=1)


def _pack_small_weights(sg_ln_g, sg_ln_b, pool_scale, cc_dw_b, cc_ln_g, cc_ln_b, conv_a_w, cc_dw_w, sg_b):
    vec = jnp.stack([sg_ln_g, sg_ln_b, pool_scale, cc_dw_b, cc_ln_g, cc_ln_b])
    return jnp.concatenate(
        [
            jnp.pad(vec, ((0, RW_CONVA - RW_VEC - 6), (0, 0))),
            jnp.pad(conv_a_w, ((0, RW_DW - RW_CONVA - CONV_A), (0, 0))),
            jnp.pad(cc_dw_w, ((0, RW_SGB - RW_DW - CONV_D), (0, 0))),
            _expand_sgb(sg_b),
        ]
    )


def _sg_w_cat(sg_w):
    cat = jnp.transpose(sg_w, (1, 0, 2)).reshape(CHUNK, N_SUB * CHUNK)
    cat_t = jnp.transpose(sg_w, (2, 0, 1)).reshape(CHUNK, N_SUB * CHUNK)
    return cat, cat_t


def _pool_block_diag(pool_w):
    tiled = jnp.tile(pool_w.reshape(D_G, HEAD_DIM), (1, N_SUB))
    row = lax.broadcasted_iota(jnp.int32, (D_G, D_G), 0) // HEAD_DIM
    col = lax.broadcasted_iota(jnp.int32, (D_G, D_G), 1) // HEAD_DIM
    return jnp.where(row == col, tiled, 0.0)


def _prepare_layer(mem, w, l):
    cat, cat_t = _sg_w_cat(w["sg_w"])
    kst, vst = _kv_forward(mem, w["w_kv"], name=f"kv_fwd{l}")
    return dict(
        win=w["w_in"],
        wout=w["w_out"],
        pww=w["cc_pw_w"],
        sw=_pack_small_weights(
            w["sg_ln_g"], w["sg_ln_b"], w["pool_scale"], w["cc_dw_b"], w["cc_ln_g"], w["cc_ln_b"],
            w["conv_a_w"], w["cc_dw_w"], w["sg_b"],
        ),
        wcat=cat,
        wcat_t=cat_t,
        poolw=_pool_block_diag(w["pool_w"]).astype(BF16),
        ln=jnp.stack([w["ln_g"], w["ln_b"]]),
        kst=kst,
        vst=vst,
    )


def _forward(l, h, p, tgt, exchange=None):
    return _layer_forward(
        h, p["win"], p["kst"], p["vst"], p["wout"], p["sw"], p["wcat"], p["poolw"], p["pww"], p["ln"], tgt,
        name=f"layer_fwd{l}", exchange=exchange,
    )


def _backward(l, dxn, s, p, exchange=None):
    return _layer_backward(
        dxn, s[2], s[0], s[1], s[4], p["kst"], p["vst"], p["wout"], p["sw"], p["wcat"], p["wcat_t"], p["poolw"],
        p["pww"], p["ln"], name=f"layer_bwd{l}", exchange=exchange,
    )


def _place():
    x, y, c = lax.axis_index("x"), lax.axis_index("y"), lax.axis_index("c")
    others = [(1 - x, y), (x, 1 - y), (1 - x, 1 - y)]
    return x, y, c, others


def _half(ref, c, axis):
    n = ref.shape[axis] // 2
    if axis == 0:
        return ref.at[pl.ds(c * n, n)]
    return ref.at[:, pl.ds(c * n, n)]


def _place_own_block(place, stacked, layer, dtypes, *, name):
    n = len(stacked)

    def body(place_ref, *refs):
        for a in range(n):
            refs[n + a][...] = refs[a][...].astype(dtypes[a])

    def block(s):
        return (1,) + s.shape[1:]

    return pl.pallas_call(
        body,
        name=name,
        grid_spec=pltpu.PrefetchScalarGridSpec(
            num_scalar_prefetch=1,
            grid=(1,),
            in_specs=[pl.BlockSpec(block(s), lambda i, place_ref: (layer, 0, 0)) for s in stacked],
            out_specs=[pl.BlockSpec(block(s), lambda i, place_ref: (place_ref[1], 0, 0)) for s in stacked],
        ),
        out_shape=[jax.ShapeDtypeStruct((N_CHIPS,) + s.shape[1:], dt) for s, dt in zip(stacked, dtypes)],
        compiler_params=pltpu.CompilerParams(dimension_semantics=("arbitrary",), vmem_limit_bytes=VMEM_LIMIT),
    )(place, *stacked)


def _sds(a):
    return jax.ShapeDtypeStruct(a.shape, a.dtype)


def _gather_exchange(bufs):
    n = len(bufs)

    def remote(sems, block, k, to):
        return pltpu.make_async_remote_copy(
            src_ref=block, dst_ref=block, send_sem=sems[0].at[k], recv_sem=sems[1].at[k], device_id=to, device_id_type=MESH
        )

    def before(step, steps, refs, outs, sems):
        def send():
            x, y, c, others = _place()
            for j, (px, py) in enumerate(others):
                for a in range(n):
                    remote(sems, _half(refs[a].at[2 * x + y], c, 0), 3 * a + j, (px, py, c)).start()

        _when(step == 0, send)

    def after(step, steps, refs, outs, sems):
        def pass_on():
            x, y, c, others = _place()
            for j, (px, py) in enumerate(others):
                for a in range(n):
                    landed = _half(refs[a].at[2 * px + py], c, 0)
                    remote(sems, landed, 3 * a + j, (px, py, c)).wait_recv()
                    remote(sems, landed, 3 * n + 3 * a + j, (x, y, 1 - c)).start()

        def finish():
            x, y, c, others = _place()
            for j, (px, py) in enumerate(others):
                for a in range(n):
                    remote(sems, _half(refs[a].at[2 * px + py], 1 - c, 0), 3 * n + 3 * a + j, (x, y, 1 - c)).wait_recv()
            for a in range(n):
                mine = _half(refs[a].at[2 * x + y], c, 0)
                for k in range(3 * a, 3 * a + 3):
                    remote(sems, mine, k, (x, y, 1 - c)).wait_send()
                    remote(sems, mine, 3 * n + k, (x, y, 1 - c)).wait_send()

        _when(step == (3 * steps) // 4, pass_on)
        _when(step == steps - 1, finish)

    return _Exchange(bufs, [(_sds(b), a) for a, b in enumerate(bufs)], [6 * n, 6 * n], before, after)


def _swap_exchange(grads):
    n = len(grads)

    def copy(refs, outs, sems, a):
        x, y, c, _ = _place()
        return pltpu.make_async_remote_copy(
            src_ref=_half(refs[a], 1 - c, 1), dst_ref=outs[a], send_sem=sems[0].at[a], recv_sem=sems[1].at[a],
            device_id=(x, y, 1 - c), device_id_type=MESH,
        )

    def before(step, steps, refs, outs, sems):
        _when(step == 0, lambda: [copy(refs, outs, sems, a).start() for a in range(n)] and None)

    def after(step, steps, refs, outs, sems):
        _when(step == steps - 1, lambda: [copy(refs, outs, sems, a).wait() for a in range(n)] and None)

    outputs = [(jax.ShapeDtypeStruct((N_CHIPS, g.shape[1] // 2, g.shape[2]), g.dtype), None) for g in grads]
    return _Exchange(grads, outputs, [n, n], before, after)


def _add_sibling_half(place, grads, received, wire, *, name):
    n = len(grads)

    def body(place_ref, *refs):
        k = pl.program_id(0)
        for a in range(n):
            pair = (refs[a][...] + refs[n + a][...]).astype(wire[a])
            refs[2 * n + a][...] = pair

            @pl.when(k == place_ref[1])
            def _(a=a, pair=pair):
                refs[3 * n + a][...] = pair

    def block(g):
        return (1, g.shape[1] // 2, g.shape[2])

    return pl.pallas_call(
        body,
        name=name,
        grid_spec=pltpu.PrefetchScalarGridSpec(
            num_scalar_prefetch=1,
            grid=(N_CHIPS,),
            in_specs=[pl.BlockSpec(block(g), lambda k, place_ref: (k, place_ref[0], 0)) for g in grads]
            + [pl.BlockSpec(block(g), lambda k, place_ref: (k, 0, 0)) for g in grads],
            out_specs=[pl.BlockSpec(block(g), lambda k, place_ref: (k, 0, 0)) for g in grads]
            + [pl.BlockSpec(block(g), lambda k, place_ref: (place_ref[1], 0, 0)) for g in grads],
        ),
        out_shape=[jax.ShapeDtypeStruct(r.shape, dt) for r, dt in zip(received, wire)] * 2,
        compiler_params=pltpu.CompilerParams(dimension_semantics=("arbitrary",), vmem_limit_bytes=VMEM_LIMIT),
    )(place, *grads, *received)


def _scatter_exchange(pairs, landing):
    n = len(pairs)

    def copy(refs, sems, a, j, px, py):
        x, y, c, _ = _place()
        return pltpu.make_async_remote_copy(
            src_ref=refs[a].at[2 * px + py], dst_ref=refs[n + a].at[2 * x + y], send_sem=sems[0].at[3 * a + j],
            recv_sem=sems[1].at[3 * a + j], device_id=(px, py, c), device_id_type=MESH,
        )

    def before(step, steps, refs, outs, sems):
        def send():
            for j, (px, py) in enumerate(_place()[3]):
                for a in range(n):
                    copy(refs, sems, a, j, px, py).start()

        _when(step == 0, send)

    def after(step, steps, refs, outs, sems):
        def finish():
            x, y, c, others = _place()
            for j, (px, py) in enumerate(others):
                for a in range(n):
                    landed = refs[n + a].at[2 * px + py]
                    pltpu.make_async_remote_copy(
                        src_ref=landed, dst_ref=landed, send_sem=sems[0].at[3 * a + j], recv_sem=sems[1].at[3 * a + j],
                        device_id=(px, py, c), device_id_type=MESH,
                    ).wait_recv()
            for j, (px, py) in enumerate(others):
                for a in range(n):
                    copy(refs, sems, a, j, px, py).wait_send()

        _when(step == steps - 1, finish)

    return _Exchange(pairs + landing, [(_sds(b), n + a) for a, b in enumerate(landing)], [3 * n, 3 * n], before, after)


SUM_STEPS = 2


def _sum_chip_blocks(place, parts, keep_chip_axis, *, name):
    n = len(parts)

    def body(place_ref, *refs):
        for a in range(n):
            p = refs[a]
            total = (p[0].astype(F32) + p[1].astype(F32)) + (p[2].astype(F32) + p[3].astype(F32))
            if keep_chip_axis[a]:
                refs[n + a][0] = total
            else:
                refs[n + a][...] = total

    def in_spec(p):
        return pl.BlockSpec((N_CHIPS, p.shape[1] // SUM_STEPS, p.shape[2]), lambda i, place_ref: (0, i, 0))

    def out_spec(p, keep):
        rows = p.shape[1] // SUM_STEPS
        if keep:
            return pl.BlockSpec((1, rows, p.shape[2]), lambda i, place_ref: (place_ref[1], place_ref[0] * SUM_STEPS + i, 0))
        return pl.BlockSpec((rows, p.shape[2]), lambda i, place_ref: (place_ref[0] * SUM_STEPS + i, 0))

    def out_shape(p, keep):
        shape = (2 * p.shape[1], p.shape[2])
        return jax.ShapeDtypeStruct((N_CHIPS,) + shape if keep else shape, F32)

    return pl.pallas_call(
        body,
        name=name,
        grid_spec=pltpu.PrefetchScalarGridSpec(
            num_scalar_prefetch=1,
            grid=(SUM_STEPS,),
            in_specs=[in_spec(p) for p in parts],
            out_specs=[out_spec(p, k) for p, k in zip(parts, keep_chip_axis)],
        ),
        out_shape=[out_shape(p, k) for p, k in zip(parts, keep_chip_axis)],
        compiler_params=pltpu.CompilerParams(dimension_semantics=("arbitrary",), vmem_limit_bytes=VMEM_LIMIT),
    )(place, *parts)


def _join_exchange(bufs, keep_chip_axis):
    n = len(bufs)
    kept = [a for a in range(n) if keep_chip_axis[a]]
    base = n

    def copy(refs, sems, block, k, to):
        return pltpu.make_async_remote_copy(
            src_ref=block, dst_ref=block, send_sem=sems[0].at[k], recv_sem=sems[1].at[k], device_id=to, device_id_type=MESH
        )

    def mine(refs, a, cc):
        x, y, _, _ = _place()
        return _half(refs[a].at[2 * x + y] if keep_chip_axis[a] else refs[a], cc, 0)

    def before(step, steps, refs, outs, sems):
        def send():
            x, y, c, others = _place()
            for a in range(n):
                copy(refs, sems, mine(refs, a, c), a, (x, y, 1 - c)).start()
            for i, a in enumerate(kept):
                for j, (px, py) in enumerate(others):
                    copy(refs, sems, mine(refs, a, c), base + 6 * i + j, (px, py, c)).start()

        _when(step == 0, send)

    def after(step, steps, refs, outs, sems):
        def pass_on():
            x, y, c, others = _place()
            for i, a in enumerate(kept):
                for j, (px, py) in enumerate(others):
                    landed = _half(refs[a].at[2 * px + py], c, 0)
                    copy(refs, sems, landed, base + 6 * i + j, (px, py, c)).wait_recv()
                    copy(refs, sems, landed, base + 6 * i + 3 + j, (x, y, 1 - c)).start()

        def finish():
            x, y, c, others = _place()
            for a in range(n):
                copy(refs, sems, mine(refs, a, 1 - c), a, (x, y, 1 - c)).wait_recv()
            for i, a in enumerate(kept):
                for j, (px, py) in enumerate(others):
                    passed = _half(refs[a].at[2 * px + py], 1 - c, 0)
                    copy(refs, sems, passed, base + 6 * i + 3 + j, (x, y, 1 - c)).wait_recv()
            for a in range(n):
                copy(refs, sems, mine(refs, a, c), a, (x, y, 1 - c)).wait_send()
            for i, a in enumerate(kept):
                for k in range(6):
                    copy(refs, sems, mine(refs, a, c), base + 6 * i + k, (x, y, 1 - c)).wait_send()

        _when(step == steps // 2, pass_on)
        _when(step == steps - 1, finish)

    return _Exchange(bufs, [(_sds(b), a) for a, b in enumerate(bufs)], [n + 6 * len(kept)] * 2, before, after)


def _adamw(w, g, m, v):
    m = ADAM_B1 * m + (1.0 - ADAM_B1) * g
    v = ADAM_B2 * v + (1.0 - ADAM_B2) * (g * g)
    m_hat = m / (1.0 - ADAM_B1**ADAM_STEP)
    v_hat = v / (1.0 - ADAM_B2**ADAM_STEP)
    delta = -ADAM_LR * (m_hat / (jnp.sqrt(v_hat) + ADAM_EPS) + ADAM_WD * w)
    return delta, m, v


def _adamw_large(w, m, v, layer_grads, *, name):
    depth, rows, cols = w.shape
    tile = math.gcd(rows, ADAM_TILE)
    assert tile % 8 == 0

    def body(w_ref, m_ref, v_ref, *refs):
        g_refs, (g_out, d_out, m_out, v_out) = refs[:depth], refs[depth:]
        for l in range(depth):

            @pl.when(pl.program_id(0) == l)
            def _(l=l):
                g = g_refs[l][...]
                delta, m_new, v_new = _adamw(w_ref[0], g, m_ref[0], v_ref[0])
                g_out[0], d_out[0], m_out[0], v_out[0] = g, delta, m_new, v_new

    def stacked():
        return pl.BlockSpec((1, tile, cols), lambda l, i: (l, i, 0))

    def layer_spec(l):
        return pl.BlockSpec((tile, cols), lambda k, i: (jnp.where(k == l, i, 0), 0))

    shape = jax.ShapeDtypeStruct(w.shape, F32)
    return pl.pallas_call(
        body,
        name=name,
        grid=(depth, rows // tile),
        in_specs=[stacked(), stacked(), stacked()] + [layer_spec(l) for l in range(depth)],
        out_specs=[stacked()] * 4,
        out_shape=[shape] * 4,
        compiler_params=pltpu.CompilerParams(dimension_semantics=("arbitrary", "arbitrary"), vmem_limit_bytes=VMEM_LIMIT),
    )(w, m, v, *layer_grads)


def _packed_pieces(name, l):
    vecs = ("sg_ln_g", "sg_ln_b", "pool_scale", "cc_dw_b", "cc_ln_g", "cc_ln_b")
    if name in vecs:
        r = RG_VEC + vecs.index(name)
        return [((slice(l, l + 1), slice(None)), slice(r, r + 1), slice(None))]
    if name in ("ln_g", "ln_b"):
        r = RG_LN + (4 if name == "ln_b" else 0)
        return [((slice(l, l + 1), slice(j * D_G, (j + 1) * D_G)), slice(r + j, r + j + 1), slice(None)) for j in range(4)]
    assert name == "sg_w"
    return [
        ((l, h), slice(RG_SGW + CHUNK * (h // 2), RG_SGW + CHUNK * (h // 2 + 1)), slice(CHUNK * (h % 2), CHUNK * (h % 2 + 1)))
        for h in range(N_SUB)
    ]


PACKED_NAMES = ("sg_ln_g", "sg_ln_b", "pool_scale", "cc_dw_b", "cc_ln_g", "cc_ln_b", "ln_g", "ln_b", "sg_w")


def _adamw_packed(packed, ws, ms, vs, *, name):
    n, depth = len(ws), len(packed)

    def body(*refs):
        packed_refs, refs = refs[:depth], refs[depth:]
        for a, leaf in enumerate(PACKED_NAMES):
            for l in range(depth):
                for at, rows, cols in _packed_pieces(leaf, l):
                    g = packed_refs[l][rows, cols]
                    delta, m_new, v_new = _adamw(refs[a][at], g, refs[n + a][at], refs[2 * n + a][at])
                    refs[3 * n + a][at] = g
                    refs[4 * n + a][at] = delta
                    refs[5 * n + a][at] = m_new
                    refs[6 * n + a][at] = v_new

    shapes = [jax.ShapeDtypeStruct(w.shape, F32) for w in ws]
    outs = pl.pallas_call(body, name=name, out_shape=shapes * 4)(*packed, *ws, *ms, *vs)
    return outs[:n], outs[n : 2 * n], outs[2 * n : 3 * n], outs[3 * n :]


def _adamw_small(ws, gs, ms, vs, *, name):
    n = len(ws)

    def body(*refs):
        for a in range(n):
            delta, m_new, v_new = _adamw(refs[a][...], refs[n + a][...], refs[2 * n + a][...], refs[3 * n + a][...])
            refs[4 * n + a][...] = delta
            refs[5 * n + a][...] = m_new
            refs[6 * n + a][...] = v_new

    shapes = [jax.ShapeDtypeStruct(w.shape, F32) for w in ws]
    outs = pl.pallas_call(body, name=name, out_shape=shapes * 3)(*ws, *gs, *ms, *vs)
    return outs[:n], outs[n : 2 * n], outs[2 * n :]


WEIGHT_NAMES = (
    "w_in", "conv_a_w", "sg_ln_g", "sg_ln_b", "sg_w", "sg_b", "pool_w", "pool_scale", "cc_dw_w", "cc_dw_b", "cc_ln_g",
    "cc_ln_b", "cc_pw_w", "w_kv", "w_out", "ln_g", "ln_b",
)
LARGE = ("w_in", "cc_pw_w", "w_kv", "w_out")
TAPS_ROWS = 48


def _unpack_small_grads(small, chip):
    out = {}
    for r, k in enumerate(("sg_ln_g", "sg_ln_b", "pool_scale", "cc_dw_b", "cc_ln_g", "cc_ln_b")):
        out[k] = small[RG_VEC + r]
    out["conv_a_w"] = lax.dynamic_slice_in_dim(small[RG_CONVA : RG_CONVA + CONV_A], chip * HEAD_DIM, HEAD_DIM, axis=1)
    out["cc_dw_w"] = lax.dynamic_slice_in_dim(small[RG_DW : RG_DW + CONV_D], chip * HEAD_DIM, HEAD_DIM, axis=1)
    cat = jnp.concatenate([small[RG_SGW : RG_SGW + CHUNK], small[RG_SGW + CHUNK : RG_SGW + 2 * CHUNK]], axis=1)
    out["sg_w"] = jnp.transpose(cat.reshape(CHUNK, N_SUB, CHUNK), (1, 0, 2))
    out["sg_b"] = small[RG_SGB : RG_SGB + CHUNK].reshape(CHUNK, N_SUB, HEAD_DIM).sum(-1).T
    pool = small[RG_POOL : RG_POOL + D_G]
    out["pool_w"] = jnp.stack(
        [pool[g * HEAD_DIM : (g + 1) * HEAD_DIM, g * HEAD_DIM : (g + 1) * HEAD_DIM] for g in range(N_SUB)]
    )
    out["ln_g"] = small[RG_LN : RG_LN + 4].reshape(D_MODEL)
    out["ln_b"] = small[RG_LN + 4 : RG_LN + 8].reshape(D_MODEL)
    return out


def kernel(x, mem, w_in, conv_a_w, sg_ln_g, sg_ln_b, sg_w, sg_b, pool_w, pool_scale, cc_dw_w, cc_dw_b, cc_ln_g, cc_ln_b, cc_pw_w, w_kv, w_out, ln_g, ln_b, loss_target, m_w_in, m_conv_a_w, m_sg_ln_g, m_sg_ln_b, m_sg_w, m_sg_b, m_pool_w, m_pool_scale, m_cc_dw_w, m_cc_dw_b, m_cc_ln_g, m_cc_ln_b, m_cc_pw_w, m_w_kv, m_w_out, m_ln_g, m_ln_b, v_w_in, v_conv_a_w, v_sg_ln_g, v_sg_ln_b, v_sg_w, v_sg_b, v_pool_w, v_pool_scale, v_cc_dw_w, v_cc_dw_b, v_cc_ln_g, v_cc_ln_b, v_cc_pw_w, v_w_kv, v_w_out, v_ln_g, v_ln_b):
    given = dict(locals())
    weights = {k: given[k] for k in WEIGHT_NAMES}
    chip = 2 * lax.axis_index("x") + lax.axis_index("y")
    place = jnp.stack([lax.axis_index("c"), chip]).astype(jnp.int32)

    x0, mem0 = x[0], mem[0]

    taps = jnp.concatenate([conv_a_w, cc_dw_w], axis=1)
    taps = jnp.pad(taps, ((0, 0), (0, TAPS_ROWS - taps.shape[1]), (0, 0)))

    def own_blocks(l):
        return _place_own_block(
            place, [w_in, w_out, w_kv, cc_pw_w, taps], l, [BF16, BF16, BF16, BF16, F32], name=f"place_weights{l}"
        )

    def layer_operands(l, gathered):
        g_in, g_out, g_kv, g_pw, g_taps = gathered
        taps_full = jnp.transpose(g_taps, (1, 0, 2)).reshape(TAPS_ROWS, D_G)
        full = dict(
            w_in=g_in,
            w_out=g_out.reshape(D_MIX, D_MODEL),
            w_kv=g_kv.reshape(D_MODEL, 2 * D_G),
            cc_pw_w=g_pw.reshape(D_G, D_G),
            conv_a_w=taps_full[0:CONV_A],
            cc_dw_w=taps_full[CONV_A : CONV_A + CONV_D],
            **{k: weights[k][l] for k in WEIGHT_NAMES if k not in LARGE + ("conv_a_w", "cc_dw_w")},
        )
        return _prepare_layer(mem0, full, l)

    def other_grads(l, bwd, small):
        _, _, dwout, dkst, dvst, dpw, _ = bwd
        return [
            dwout.reshape(N_CHIPS, D_MIX // N_CHIPS, D_MODEL),
            _kv_backward(mem0, dkst, dvst, name=f"kv_bwd{l}").reshape(N_CHIPS, D_MODEL // N_CHIPS, 2 * D_G),
            dpw.reshape(N_CHIPS, D_G // N_CHIPS, D_G),
            small.reshape(N_CHIPS, RG_ROWS // N_CHIPS, D_G),
        ]

    n_red = 5
    keep = [False, False, False, False, True]

    wire = [BF16, BF16, BF16, BF16, F32]

    def reduced_layer(joined):
        r_in, r_out, r_kv, r_pw, small_all = joined
        packed = small_all.reshape(RG_ROWS, D_G)
        out = _unpack_small_grads(packed, chip)
        out.update(w_in=r_in, w_out=r_out, w_kv=r_kv, cc_pw_w=r_pw, packed=packed)
        return out

    blocks0, blocks1 = own_blocks(0), own_blocks(1)
    p0 = layer_operands(0, _run_exchange(_gather_exchange(blocks0), name="gather_weights0"))
    fwd0, gathered1 = _forward(0, x0, p0, None, exchange=_gather_exchange(blocks1))
    p1 = layer_operands(1, gathered1)
    x1 = fwd0[3]
    fwd1, _ = _forward(1, x1, p1, loss_target[0])

    bwd1, _ = _backward(1, fwd1[3], fwd1, p1)
    small1 = bwd1[6].at[RG_LOSS, :].set(fwd1[5][0, 0])
    (dwin1,), _ = _input_weight_grad(x1, bwd1[0], name="w_in_grad1")
    grads1 = [dwin1] + other_grads(1, bwd1, small1)
    (dx1,), received1 = _input_grad(
        bwd1[0], bwd1[1], p1["win"], name="input_grad1", exchange=_swap_exchange(grads1).starting_late()
    )
    pairs1 = _add_sibling_half(place, grads1, received1, wire, name="rs_pair1")
    bwd0, parts1 = _backward(
        0, dx1, fwd0, p0, exchange=_scatter_exchange(pairs1[:n_red], pairs1[n_red:]).starting_late()
    )
    halves1 = _sum_chip_blocks(place, parts1, keep, name="rs_sum1")
    rest0 = other_grads(0, bwd0, bwd0[6])
    n_rest = len(rest0)
    pairs0_rest = _add_sibling_half(
        place, rest0, _run_exchange(_swap_exchange(rest0), name="rs_swap0_rest"), wire[1:], name="rs_pair0_rest"
    )
    (dwin0,), carried = _input_weight_grad(
        x0, bwd0[0], name="w_in_grad0",
        exchange=_both(_join_exchange(halves1, keep), _scatter_exchange(pairs0_rest[:n_rest], pairs0_rest[n_rest:])),
    )
    joined1, parts0_rest = carried[:n_red], carried[n_red:]
    loss = joined1[4].reshape(RG_ROWS, D_G)[RG_LOSS, 0]
    pairs0_win = _add_sibling_half(
        place, [dwin0], _run_exchange(_swap_exchange([dwin0]), name="rs_swap0_win"), wire[:1], name="rs_pair0_win"
    )
    (grad_x,), parts0_win = _input_grad(
        bwd0[0], bwd0[1], p0["win"], name="input_grad0", exchange=_scatter_exchange(pairs0_win[:1], pairs0_win[1:])
    )
    halves0 = _sum_chip_blocks(place, parts0_win + parts0_rest, keep, name="rs_sum0")
    reduced = [reduced_layer(_run_exchange(_join_exchange(halves0, keep), name="rs_join0")), reduced_layer(joined1)]

    grad, delta, new_m, new_v = {}, {}, {}, {}
    for k in LARGE:
        w3 = weights[k]
        grad[k], delta[k], new_m[k], new_v[k] = _adamw_large(
            w3, given["m_" + k], given["v_" + k], [reduced[l][k] for l in range(DEPTH)], name=f"adamw_{k}"
        )
    g_p, d_p, m_p, v_p = _adamw_packed(
        [reduced[l]["packed"] for l in range(DEPTH)],
        [weights[k] for k in PACKED_NAMES],
        [given["m_" + k] for k in PACKED_NAMES],
        [given["v_" + k] for k in PACKED_NAMES],
        name="adamw_packed",
    )
    for a, k in enumerate(PACKED_NAMES):
        grad[k], delta[k], new_m[k], new_v[k] = g_p[a], d_p[a], m_p[a], v_p[a]
    small_names = [k for k in WEIGHT_NAMES if k not in LARGE + PACKED_NAMES]
    for k in small_names:
        grad[k] = jnp.stack([reduced[l][k] for l in range(DEPTH)])
    d_s, m_s, v_s = _adamw_small(
        [weights[k] for k in small_names],
        [grad[k] for k in small_names],
        [given["m_" + k] for k in small_names],
        [given["v_" + k] for k in small_names],
        name="adamw_small",
    )
    for a, k in enumerate(small_names):
        delta[k], new_m[k], new_v[k] = d_s[a], m_s[a], v_s[a]

    return (
        loss,
        grad_x[None],
        *[grad[k] for k in WEIGHT_NAMES],
        *[delta[k] for k in WEIGHT_NAMES],
        *[new_m[k] for k in WEIGHT_NAMES],
        *[new_v[k] for k in WEIGHT_NAMES],
    )
```

```python
import math

import jax
import jax.numpy as jnp
from jax import lax
from jax.experimental import pallas as pl
from jax.experimental.pallas import tpu as pltpu

F32 = jnp.float32
BF16 = jnp.bfloat16

D_MODEL = 1024
DEPTH = 2
D_G = 256
D_MIX = 5 * D_G
D_IN = 9 * D_G + D_MIX
N_SUB = 4
HEAD_DIM = 64
CONV_A = 3
CONV_D = 31
CHUNK = 128
MEM_LEN = 256
N_CHIPS = 4
W_IN_SHARD = D_IN // N_CHIPS
LN_EPS = 1e-5
ALPHA = (2.0 * DEPTH) ** 0.25
ATT_SCALE = 1.0 / math.sqrt(HEAD_DIM)
GELU_C = math.sqrt(2.0 / math.pi)
GELU_A = 0.044715

ADAM_LR = 0.001
ADAM_B1 = 0.9
ADAM_B2 = 0.999
ADAM_EPS = 1e-08
ADAM_WD = 0.01
ADAM_STEP = 10

C_XA, C_BA, C_CA, C_U, C_V, C_XC, C_DA, C_DG, C_Q, C_GATE = (D_G * i for i in range(10))

HALO_A = 8
HALO_C = 16
HALO_D = 32

RW_VEC = 0
RW_CONVA = 16
RW_DW = 24
RW_SGB = 56

RG_VEC = 0
RG_CONVA = 16
RG_DW = 24
RG_SGW = 56
RG_SGB = RG_SGW + 2 * CHUNK
RG_POOL = RG_SGB + CHUNK
RG_LN = RG_POOL + D_G
RG_LOSS = 8
RG_ROWS = 768

VMEM_LIMIT = 62 * 1024 * 1024

AUX_CVD = 0
AUX_PM = D_G
AUX_P = 2 * D_G
AUX_COLS = AUX_P + N_SUB * MEM_LEN
SEQ_TILE = 256
FWD_TILE = 512
MM_TILE = 1024
W_GRAD_TILE = 4096
ADAM_TILE = 512

MESH = pl.DeviceIdType.MESH
ANY = pl.BlockSpec(memory_space=pl.ANY)
NT = (((1,), (1,)), ((), ()))
TN = (((0,), (0,)), ((), ()))


def _dot(a, b):
    return jnp.dot(a, b, preferred_element_type=F32)


def _dot_nt(a, b):
    return lax.dot_general(a, b, NT, preferred_element_type=F32)


def _dot_tn(a, b):
    return lax.dot_general(a, b, TN, preferred_element_type=F32)


def _full(shape):
    zeros = (0,) * len(shape)
    return pl.BlockSpec(shape, lambda *_: zeros)


class _Exchange:
    def __init__(self, operands, outputs, sem_counts, before, after):
        self.operands, self.outputs, self.sem_counts, self.before, self.after = operands, outputs, sem_counts, before, after
        self.late_start = False

    def starting_late(self):
        self.late_start = True
        return self

    def specs(self, first_input, first_output):
        aliases = {first_input + src: first_output + j for j, (_, src) in enumerate(self.outputs) if src is not None}
        return (
            [ANY] * len(self.operands),
            [ANY] * len(self.outputs),
            [sds for sds, _ in self.outputs],
            [pltpu.SemaphoreType.DMA((k,)) for k in self.sem_counts],
            aliases,
        )

    def split(self, ins, outs):
        refs = list(ins)
        for j, (_, src) in enumerate(self.outputs):
            if src is not None:
                refs[src] = outs[j]
        return refs


def _both(first, second):
    n1, m1, s1 = len(first.operands), len(first.outputs), len(first.sem_counts)
    outputs = first.outputs + [(sds, None if src is None else n1 + src) for sds, src in second.outputs]

    def before(step, steps, refs, outs, sems):
        first.before(step, steps, refs[:n1], outs[:m1], sems[:s1])
        second.before(step, steps, refs[n1:], outs[m1:], sems[s1:])

    def after(step, steps, refs, outs, sems):
        first.after(step, steps, refs[:n1], outs[:m1], sems[:s1])
        second.after(step, steps, refs[n1:], outs[m1:], sems[s1:])

    return _Exchange(first.operands + second.operands, outputs, first.sem_counts + second.sem_counts, before, after)


def _when(cond, fn):
    if isinstance(cond, bool):
        if cond:
            fn()
    else:
        pl.when(cond)(fn)


def _run_exchange(exchange, *, name):
    n_in, n_out = len(exchange.operands), len(exchange.outputs)
    in_specs, out_specs, out_shape, sems, aliases = exchange.specs(0, 0)

    def body(*refs):
        ins, outs, sem_refs = refs[:n_in], refs[n_in : n_in + n_out], refs[n_in + n_out :]
        refs = exchange.split(ins, outs)
        exchange.before(0, 1, refs, outs, sem_refs)
        exchange.after(0, 1, refs, outs, sem_refs)

    return pl.pallas_call(
        body, name=name, in_specs=in_specs, out_specs=out_specs, out_shape=out_shape, scratch_shapes=sems,
        input_output_aliases=aliases,
    )(*exchange.operands)


def _gridded_call(body, *, name, steps, in_specs, out_specs, out_shape, scratch_shapes, operands, exchange=None):
    params = pltpu.CompilerParams(dimension_semantics=("arbitrary",), vmem_limit_bytes=VMEM_LIMIT)
    if exchange is None:
        outs = pl.pallas_call(
            body, name=name, grid=(steps,), in_specs=in_specs, out_specs=out_specs, out_shape=out_shape,
            scratch_shapes=scratch_shapes, compiler_params=params,
        )(*operands)
        return list(outs), []
    n_in, n_out, n_scr = len(in_specs), len(out_specs), len(scratch_shapes)
    x_in, x_out = len(exchange.operands), len(exchange.outputs)
    ex_in_specs, ex_out_specs, ex_out_shape, ex_sems, aliases = exchange.specs(n_in, n_out)

    def full(*refs):
        own_in, refs = refs[:n_in], refs[n_in:]
        ex_in, refs = refs[:x_in], refs[x_in:]
        own_out, refs = refs[:n_out], refs[n_out:]
        ex_out, refs = refs[:x_out], refs[x_out:]
        own_scr, sem_refs = refs[:n_scr], refs[n_scr:]
        ex_refs = exchange.split(ex_in, ex_out)
        step = pl.program_id(0)
        if not exchange.late_start:
            exchange.before(step, steps, ex_refs, ex_out, sem_refs)
        body(*own_in, *own_out, *own_scr)
        if exchange.late_start:
            exchange.before(step, steps, ex_refs, ex_out, sem_refs)
        exchange.after(step, steps, ex_refs, ex_out, sem_refs)

    outs = pl.pallas_call(
        full, name=name, grid=(steps,), in_specs=in_specs + ex_in_specs, out_specs=out_specs + ex_out_specs,
        out_shape=out_shape + ex_out_shape, scratch_shapes=scratch_shapes + ex_sems, input_output_aliases=aliases,
        compiler_params=params,
    )(*operands, *exchange.operands)
    return list(outs[:n_out]), list(outs[n_out:])


def _sigmoid(x):
    return 0.5 * jnp.tanh(0.5 * x) + 0.5


def _gelu(x):
    t = jnp.tanh(GELU_C * (x + GELU_A * x * x * x))
    return 0.5 * x * (1.0 + t), t


def _gelu_grad(x, t):
    return 0.5 * (1.0 + t) + 0.5 * x * (1.0 - t * t) * (GELU_C * (1.0 + 3.0 * GELU_A * x * x))


def _normalize(v):
    mu = jnp.mean(v, axis=-1, keepdims=True)
    d = v - mu
    var = jnp.mean(d * d, axis=-1, keepdims=True)
    rstd = lax.rsqrt(var + LN_EPS)
    return d * rstd, rstd


def _normalize_grad(dhat, hat, rstd):
    m1 = jnp.mean(dhat, axis=-1, keepdims=True)
    m2 = jnp.mean(dhat * hat, axis=-1, keepdims=True)
    return rstd * (dhat - m1 - hat * m2)


def _lane(width=D_G):
    return lax.broadcasted_iota(jnp.int32, (1, width), 1)


def _head_masks():
    head = _lane() // HEAD_DIM
    return [(head == h).astype(F32) for h in range(N_SUB)]


def _stack_heads(v, masks):
    return jnp.concatenate([v * m for m in masks], axis=0)


def _tril_mask_cat():
    t = lax.broadcasted_iota(jnp.int32, (CHUNK, N_SUB * CHUNK), 0)
    s = lax.broadcasted_iota(jnp.int32, (CHUNK, N_SUB * CHUNK), 1) % CHUNK
    return s <= t


def _triu_mask_cat():
    s = lax.broadcasted_iota(jnp.int32, (CHUNK, N_SUB * CHUNK), 0)
    t = lax.broadcasted_iota(jnp.int32, (CHUNK, N_SUB * CHUNK), 1) % CHUNK
    return t >= s


def _pool_select(a2, a4, a8, a16):
    lane = _lane()
    return jnp.where(lane < 64, a2, jnp.where(lane < 128, a4, jnp.where(lane < 192, a8, a16)))


def _pool_inv_count(row0, rows):
    t = row0 + lax.broadcasted_iota(jnp.int32, (HALO_C, D_G), 0)
    lane = lax.broadcasted_iota(jnp.int32, (HALO_C, D_G), 1)
    win = jnp.where(lane < 64, 2, jnp.where(lane < 128, 4, jnp.where(lane < 192, 8, 16)))
    head = 1.0 / jnp.minimum(t + 1, win).astype(F32)
    inv_win = jnp.broadcast_to(_pool_select(0.5, 0.25, 0.125, 0.0625), (rows - HALO_C, D_G))
    return jnp.concatenate([head, inv_win], axis=0)


def _trailing_window_sum(halo, cur):
    e = jnp.concatenate([halo, cur], axis=0)
    s2 = e + pltpu.roll(e, 1, 0)
    s4 = s2 + pltpu.roll(s2, 2, 0)
    s8 = s4 + pltpu.roll(s4, 4, 0)
    s16 = s8 + pltpu.roll(s8, 8, 0)
    return _pool_select(s2, s4, s8, s16)[HALO_C:]


def _leading_window_sum(cur, halo):
    e = jnp.concatenate([cur, halo], axis=0)
    n = e.shape[0]
    s2 = e + pltpu.roll(e, n - 1, 0)
    s4 = s2 + pltpu.roll(s2, n - 2, 0)
    s8 = s4 + pltpu.roll(s4, n - 4, 0)
    s16 = s8 + pltpu.roll(s8, n - 8, 0)
    return _pool_select(s2, s4, s8, s16)[: cur.shape[0]]


def _softmax_blocks(sc):
    out = []
    for h in range(N_SUB):
        s = sc[:, h * MEM_LEN : (h + 1) * MEM_LEN]
        e = jnp.exp(s - jnp.max(s, axis=-1, keepdims=True))
        out.append(e * (1.0 / jnp.sum(e, axis=-1, keepdims=True)))
    return jnp.concatenate(out, axis=-1)


STRIP = 32
SHIFTS = 8


def _fill_shifts(buf):
    n = buf.shape[1] - SHIFTS
    for r in range(1, SHIFTS):
        buf[r, 0:n, :] = buf[0, r : r + n, :]


def _shifted(buf, off, rows):
    r = off % SHIFTS
    return buf[r, off - r : off - r + rows, :]


def _sgu_mix(vn, wcat_b, sgb, masks):
    vbd = _stack_heads(vn, masks).astype(BF16)
    return _dot(wcat_b, vbd) + sgb, vbd


def _layer_forward(x, win, kst, vst, wout, sw, wcat, poolw, pww, ln, tgt, *, name, exchange=None):
    seq = x.shape[0]
    tile = min(FWD_TILE, seq)
    n_tiles = seq // tile
    last = tgt is not None

    def body(*refs):
        x_ref, win_ref, kst_ref, vst_ref, wout_ref, sw_ref, wcat_ref, pool_ref, pw_ref, ln_ref = refs[:10]
        refs = refs[10:]
        if last:
            tgt_ref, refs = refs[0], refs[1:]
        proj_ref, y_ref, z_ref, out_ref, aux_ref = refs[:5]
        refs = refs[5:]
        if last:
            loss_ref, refs = refs[0], refs[1:]
        pbuf, xchalo, gbuf = refs
        i = pl.program_id(0)

        @pl.when(i == 0)
        def _():
            pbuf[0:HALO_A, :] = jnp.zeros((HALO_A, D_G), F32)
            xchalo[...] = jnp.zeros((HALO_C, D_G), F32)
            gbuf[0, 0:HALO_D, :] = jnp.zeros((HALO_D, D_G), F32)
            if last:
                loss_ref[...] = jnp.zeros((8, 128), F32)

        xt = x_ref[...]
        xb = xt.astype(BF16)

        blocks = {}

        def project(k):
            blocks[k] = _dot(xb, win_ref[k])
            proj_ref[:, k * W_IN_SHARD : (k + 1) * W_IN_SHARD] = blocks[k]

        def cols(start, width=D_G):
            parts, c = [], start
            while c < start + width:
                k, lo = divmod(c, W_IN_SHARD)
                hi = min(W_IN_SHARD, lo + start + width - c)
                parts.append(blocks[k][:, lo:hi])
                c += hi - lo
            return parts[0] if len(parts) == 1 else jnp.concatenate(parts, axis=1)

        project(0)
        project(1)
        masks = _head_masks()

        pbuf[HALO_A : HALO_A + tile, :] = cols(C_CA) * cols(C_XA)
        cv = jnp.zeros((tile, D_G), F32)
        for k in range(CONV_A):
            off = HALO_A - (CONV_A - 1) + k
            cv = cv + sw_ref[RW_CONVA + k : RW_CONVA + k + 1, :] * pbuf[off : off + tile, :]
        y_ref[:, 0:D_G] = cols(C_BA) * cv
        pbuf[0:HALO_A, :] = pbuf[tile : tile + HALO_A, :]

        project(2)

        ua, _ = _gelu(cols(C_U))
        vg, _ = _gelu(cols(C_V))
        vhat, _ = _normalize(vg)
        vn = vhat * sw_ref[RW_VEC : RW_VEC + 1, :] + sw_ref[RW_VEC + 1 : RW_VEC + 2, :]
        wcat_b = jnp.where(_tril_mask_cat(), wcat_ref[...], 0.0).astype(BF16)
        sgb = sw_ref[RW_SGB : RW_SGB + CHUNK, :]
        for j in range(tile // CHUNK):
            rows = slice(j * CHUNK, (j + 1) * CHUNK)
            mixed, _ = _sgu_mix(vn[rows], wcat_b, sgb, masks)
            y_ref[rows, D_G : 2 * D_G] = ua[rows] * mixed

        xc = cols(C_XC)
        wsum = _trailing_window_sum(xchalo[...], xc)
        pm = wsum * _pool_inv_count(i * tile, tile) - xc
        aux_ref[:, AUX_PM : AUX_PM + D_G] = pm
        y_ref[:, 2 * D_G : 3 * D_G] = _dot(pm.astype(BF16), pool_ref[...]) * sw_ref[RW_VEC + 2 : RW_VEC + 3, :]
        xchalo[...] = xc[tile - HALO_C :, :]

        project(3)

        gbuf[0, HALO_D : HALO_D + tile, :] = cols(C_DA) * _sigmoid(cols(C_DG))
        _fill_shifts(gbuf)
        for r0 in range(0, tile, STRIP):
            acc = jnp.zeros((STRIP, D_G), F32) + sw_ref[RW_VEC + 3 : RW_VEC + 4, :]
            for k in range(CONV_D):
                off = HALO_D - (CONV_D - 1) + k
                acc = acc + sw_ref[RW_DW + k : RW_DW + k + 1, :] * _shifted(gbuf, off + r0, STRIP)
            aux_ref[r0 : r0 + STRIP, AUX_CVD : AUX_CVD + D_G] = acc
        nhat, _ = _normalize(aux_ref[:, AUX_CVD : AUX_CVD + D_G])
        nrm = nhat * sw_ref[RW_VEC + 4 : RW_VEC + 5, :] + sw_ref[RW_VEC + 5 : RW_VEC + 6, :]
        y_ref[:, 3 * D_G : 4 * D_G] = _dot((nrm * _sigmoid(nrm)).astype(BF16), pw_ref[...])
        gbuf[0, 0:HALO_D, :] = gbuf[0, tile : tile + HALO_D, :]

        qb = cols(C_Q).astype(BF16)
        p_all = _softmax_blocks(_dot_nt(qb, kst_ref[...]) * ATT_SCALE)
        aux_ref[:, AUX_P:] = p_all
        y_ref[:, 4 * D_G : 5 * D_G] = _dot(p_all.astype(BF16), vst_ref[...])

        gate = cols(C_GATE, D_MIX)
        hid = y_ref[...] * (gate * _sigmoid(gate))
        z = ALPHA * xt + _dot(hid.astype(BF16), wout_ref[...])
        z_ref[...] = z
        zhat, _ = _normalize(z)
        xn = zhat * ln_ref[0:1, :] + ln_ref[1:2, :]
        if last:
            err = xn - tgt_ref[...]
            out_ref[...] = err * (1.0 / D_MODEL)
            loss_ref[...] += jnp.sum(err * err) * (0.5 / D_MODEL)
        else:
            out_ref[...] = xn

    def rows(width):
        return pl.BlockSpec((tile, width), lambda i: (i, 0))

    operands = [x, win, kst, vst, wout, sw, wcat, poolw, pww, ln]
    in_specs = [rows(D_MODEL)] + [_full(a.shape) for a in operands[1:]]
    widths = [D_IN, D_MIX, D_MODEL, D_MODEL, AUX_COLS]
    out_shape = [jax.ShapeDtypeStruct((seq, w), F32) for w in widths]
    out_specs = [rows(w) for w in widths]
    if last:
        operands.append(tgt)
        in_specs.append(rows(D_MODEL))
        out_shape.append(jax.ShapeDtypeStruct((8, 128), F32))
        out_specs.append(_full((8, 128)))
    return _gridded_call(
        body,
        name=name,
        steps=n_tiles,
        in_specs=in_specs,
        out_specs=out_specs,
        out_shape=out_shape,
        scratch_shapes=[
            pltpu.VMEM((HALO_A + tile, D_G), F32),
            pltpu.VMEM((HALO_C, D_G), F32),
            pltpu.VMEM((SHIFTS, HALO_D + tile, D_G), F32),
        ],
        operands=operands,
        exchange=exchange,
    )


def _layer_backward(dxn, z, proj, y, cvd, kst, vst, wout, sw, wcat, wcat_t, poolw, pww, ln, *, name, exchange=None):
    seq = dxn.shape[0]
    tile = min(SEQ_TILE, seq)
    n_tiles = seq // tile
    halo_blocks = tile // HALO_D

    def body(
        dxn_ref, z_ref, proj_ref, halo_ref, y_ref, aux_ref, kst_ref, vst_ref, wout_ref, sw_ref, wcat_ref, wcat_t_ref,
        pool_ref, pw_ref, ln_ref, dproj_ref, dz_ref, dwout_ref, dkst_ref, dvst_ref, dpw_ref, sg_ref,
        pbuf, dcvbuf, rhalo, gbuf, dgbuf, dwacc,
    ):
        i = pl.program_id(0)
        ti = n_tiles - 1 - i

        @pl.when(i == 0)
        def _():
            dwout_ref[...] = jnp.zeros(dwout_ref.shape, F32)
            dkst_ref[...] = jnp.zeros(dkst_ref.shape, F32)
            dvst_ref[...] = jnp.zeros(dvst_ref.shape, F32)
            dpw_ref[...] = jnp.zeros(dpw_ref.shape, F32)
            sg_ref[...] = jnp.zeros(sg_ref.shape, F32)
            dwacc[...] = jnp.zeros(dwacc.shape, F32)
            dcvbuf[tile : tile + HALO_A, :] = jnp.zeros((HALO_A, D_G), F32)
            rhalo[...] = jnp.zeros((HALO_C, D_G), F32)
            dgbuf[0, tile : tile + HALO_D, :] = jnp.zeros((HALO_D, D_G), F32)

        def acc_row(row, val):
            sg_ref[row : row + 1, :] += jnp.sum(val, axis=0, keepdims=True)

        masks = _head_masks()
        has_past = (ti > 0).astype(F32)

        zhat, zrstd = _normalize(z_ref[...])
        dxn_t = dxn_ref[...]
        dlg = jnp.sum(dxn_t * zhat, axis=0, keepdims=True)
        dlb = jnp.sum(dxn_t, axis=0, keepdims=True)
        for j in range(D_MODEL // D_G):
            sg_ref[RG_LN + j : RG_LN + j + 1, :] += dlg[:, j * D_G : (j + 1) * D_G]
            sg_ref[RG_LN + 4 + j : RG_LN + 5 + j, :] += dlb[:, j * D_G : (j + 1) * D_G]
        dz = _normalize_grad(dxn_t * ln_ref[0:1, :], zhat, zrstd)
        dz_ref[...] = dz
        dzb = dz.astype(BF16)

        gate = proj_ref[:, C_GATE:]
        sgm = _sigmoid(gate)
        silu = gate * sgm
        yc = y_ref[...]
        dwout_ref[...] += _dot_tn((yc * silu).astype(BF16), dzb)
        dh = _dot_nt(dzb, wout_ref[...])
        dproj_ref[:, C_GATE:] = (dh * yc * (sgm * (1.0 + gate * (1.0 - sgm)))).astype(BF16)
        dy = dh * silu

        dyeb = dy[:, 4 * D_G : 5 * D_G].astype(BF16)
        qb = proj_ref[:, C_Q : C_Q + D_G].astype(BF16)
        kst_b = kst_ref[...]
        p_all = aux_ref[:, AUX_P:]
        dvst_ref[...] += _dot_tn(p_all.astype(BF16), dyeb)
        dp_all = _dot_nt(dyeb, vst_ref[...])
        ds = []
        for h in range(N_SUB):
            blk = slice(h * MEM_LEN, (h + 1) * MEM_LEN)
            p, dpb = p_all[:, blk], dp_all[:, blk]
            ds.append(p * (dpb - jnp.sum(dpb * p, axis=-1, keepdims=True)))
        dsb = (jnp.concatenate(ds, axis=-1) * ATT_SCALE).astype(BF16)
        dproj_ref[:, C_Q : C_Q + D_G] = _dot(dsb, kst_b).astype(BF16)
        dkst_ref[...] += _dot_tn(dsb, qb)

        dya = dy[:, 0:D_G]
        xa = proj_ref[:, C_XA : C_XA + D_G]
        ba = proj_ref[:, C_BA : C_BA + D_G]
        ca = proj_ref[:, C_CA : C_CA + D_G]
        past = slice(HALO_D - HALO_A, HALO_D)
        pbuf[0:HALO_A, :] = halo_ref[past, C_CA : C_CA + D_G] * halo_ref[past, C_XA : C_XA + D_G] * has_past
        pbuf[HALO_A : HALO_A + tile, :] = ca * xa
        cv = jnp.zeros((tile, D_G), F32)
        for k in range(CONV_A):
            off = HALO_A - (CONV_A - 1) + k
            cv = cv + sw_ref[RW_CONVA + k : RW_CONVA + k + 1, :] * pbuf[off : off + tile, :]
        dproj_ref[:, C_BA : C_BA + D_G] = (dya * cv).astype(BF16)
        dcv = dya * ba
        dcvbuf[0:tile, :] = dcv
        dp = jnp.zeros((tile, D_G), F32)
        for k in range(CONV_A):
            off = HALO_A - (CONV_A - 1) + k
            acc_row(RG_CONVA + k, dcv * pbuf[off : off + tile, :])
            back = CONV_A - 1 - k
            dp = dp + sw_ref[RW_CONVA + k : RW_CONVA + k + 1, :] * dcvbuf[back : back + tile, :]
        dproj_ref[:, C_CA : C_CA + D_G] = (dp * xa).astype(BF16)
        dproj_ref[:, C_XA : C_XA + D_G] = (dp * ca).astype(BF16)
        dcvbuf[tile : tile + HALO_A, :] = dcvbuf[0:HALO_A, :]

        dyb = dy[:, D_G : 2 * D_G]
        u = proj_ref[:, C_U : C_U + D_G]
        v = proj_ref[:, C_V : C_V + D_G]
        ua, ut = _gelu(u)
        vg, vt = _gelu(v)
        vhat, vrstd = _normalize(vg)
        sg_g = sw_ref[RW_VEC : RW_VEC + 1, :]
        vn = vhat * sg_g + sw_ref[RW_VEC + 1 : RW_VEC + 2, :]
        tril = _tril_mask_cat()
        wcat_b = jnp.where(tril, wcat_ref[...], 0.0).astype(BF16)
        wcat_tb = jnp.where(_triu_mask_cat(), wcat_t_ref[...], 0.0).astype(BF16)
        sgb = sw_ref[RW_SGB : RW_SGB + CHUNK, :]
        dmixed = dyb * ua
        dvn_parts = []
        du_parts = []
        dwcat = jnp.zeros((CHUNK, N_SUB * CHUNK), F32)
        dsgb = jnp.zeros((CHUNK, D_G), F32)
        for j in range(tile // CHUNK):
            rows = slice(j * CHUNK, (j + 1) * CHUNK)
            mixed, vbd = _sgu_mix(vn[rows], wcat_b, sgb, masks)
            du_parts.append(dyb[rows] * mixed)
            dmx = dmixed[rows]
            dsgb = dsgb + dmx
            dwcat = dwcat + _dot_nt(dmx.astype(BF16), vbd)
            dvn_parts.append(_dot(wcat_tb, _stack_heads(dmx, masks).astype(BF16)))
        dwcat = jnp.where(tril, dwcat, 0.0)
        sg_ref[RG_SGW : RG_SGW + CHUNK, :] += dwcat[:, 0:D_G]
        sg_ref[RG_SGW + CHUNK : RG_SGW + 2 * CHUNK, :] += dwcat[:, D_G:]
        sg_ref[RG_SGB : RG_SGB + CHUNK, :] += dsgb
        dvn = jnp.concatenate(dvn_parts, axis=0)
        du_act = jnp.concatenate(du_parts, axis=0)
        acc_row(RG_VEC, dvn * vhat)
        acc_row(RG_VEC + 1, dvn)
        dvg = _normalize_grad(dvn * sg_g, vhat, vrstd)
        dproj_ref[:, C_U : C_U + D_G] = (du_act * _gelu_grad(u, ut)).astype(BF16)
        dproj_ref[:, C_V : C_V + D_G] = (dvg * _gelu_grad(v, vt)).astype(BF16)

        dyc = dy[:, 2 * D_G : 3 * D_G]
        inv_cnt = _pool_inv_count(ti * tile, tile)
        pmb = aux_ref[:, AUX_PM : AUX_PM + D_G].astype(BF16)
        pool_b = pool_ref[...]
        scale = sw_ref[RW_VEC + 2 : RW_VEC + 3, :]
        acc_row(RG_VEC + 2, dyc * _dot(pmb, pool_b))
        dpre = (dyc * scale).astype(BF16)
        sg_ref[RG_POOL : RG_POOL + D_G, :] += _dot_tn(pmb, dpre)
        dpm = _dot_nt(dpre, pool_b)
        r = dpm * inv_cnt
        dproj_ref[:, C_XC : C_XC + D_G] = (_leading_window_sum(r, rhalo[...]) - dpm).astype(BF16)
        rhalo[...] = r[0:HALO_C, :]

        dyd = dy[:, 3 * D_G : 4 * D_G]
        da = proj_ref[:, C_DA : C_DA + D_G]
        sgd = _sigmoid(proj_ref[:, C_DG : C_DG + D_G])
        gbuf[0, 0:HALO_D, :] = halo_ref[:, C_DA : C_DA + D_G] * _sigmoid(halo_ref[:, C_DG : C_DG + D_G]) * has_past
        gbuf[0, HALO_D : HALO_D + tile, :] = da * sgd
        _fill_shifts(gbuf)
        nhat, nrstd = _normalize(aux_ref[:, AUX_CVD : AUX_CVD + D_G])
        cc_g = sw_ref[RW_VEC + 4 : RW_VEC + 5, :]
        nrm = nhat * cc_g + sw_ref[RW_VEC + 5 : RW_VEC + 6, :]
        sgn = _sigmoid(nrm)
        dydb = dyd.astype(BF16)
        dpw_ref[...] += _dot_tn((nrm * sgn).astype(BF16), dydb)
        dn = _dot_nt(dydb, pw_ref[...]) * (sgn * (1.0 + nrm * (1.0 - sgn)))
        acc_row(RG_VEC + 4, dn * nhat)
        acc_row(RG_VEC + 5, dn)
        dcvd = _normalize_grad(dn * cc_g, nhat, nrstd)
        acc_row(RG_VEC + 3, dcvd)
        dgbuf[0, 0:tile, :] = dcvd
        _fill_shifts(dgbuf)
        for r0 in range(0, tile, STRIP):
            d_s = dgbuf[0, r0 : r0 + STRIP, :]
            dg = jnp.zeros((STRIP, D_G), F32)
            for k in range(CONV_D):
                off = HALO_D - (CONV_D - 1) + k
                prod = d_s * _shifted(gbuf, off + r0, STRIP)
                part = prod[0:8]
                for q in range(8, STRIP, 8):
                    part = part + prod[q : q + 8]
                dwacc[8 * k : 8 * k + 8, :] += part
                back = CONV_D - 1 - k
                dg = dg + sw_ref[RW_DW + k : RW_DW + k + 1, :] * _shifted(dgbuf, back + r0, STRIP)
            da_s = proj_ref[r0 : r0 + STRIP, C_DA : C_DA + D_G]
            sgd_s = _sigmoid(proj_ref[r0 : r0 + STRIP, C_DG : C_DG + D_G])
            dproj_ref[r0 : r0 + STRIP, C_DA : C_DA + D_G] = (dg * sgd_s).astype(BF16)
            dproj_ref[r0 : r0 + STRIP, C_DG : C_DG + D_G] = (dg * da_s * sgd_s * (1.0 - sgd_s)).astype(BF16)
        dgbuf[0, tile : tile + HALO_D, :] = dgbuf[0, 0:HALO_D, :]

        @pl.when(i == n_tiles - 1)
        def _():
            for k in range(CONV_D):
                sg_ref[RG_DW + k : RG_DW + k + 1, :] = jnp.sum(dwacc[8 * k : 8 * k + 8, :], axis=0, keepdims=True)

    def rows(width):
        return pl.BlockSpec((tile, width), lambda i: (n_tiles - 1 - i, 0))

    halo_spec = pl.BlockSpec((HALO_D, D_IN), lambda i: (jnp.maximum((n_tiles - 1 - i) * halo_blocks - 1, 0), 0))
    weights = [kst, vst, wout, sw, wcat, wcat_t, poolw, pww, ln]
    acc_shapes = [(D_MIX, D_MODEL), (N_SUB * MEM_LEN, D_G), (N_SUB * MEM_LEN, D_G), (D_G, D_G), (RG_ROWS, D_G)]
    return _gridded_call(
        body,
        name=name,
        steps=n_tiles,
        in_specs=[rows(D_MODEL), rows(D_MODEL), rows(D_IN), halo_spec, rows(D_MIX), rows(AUX_COLS)]
        + [_full(a.shape) for a in weights],
        out_specs=[rows(D_IN), rows(D_MODEL)] + [_full(s) for s in acc_shapes],
        out_shape=[jax.ShapeDtypeStruct((seq, D_IN), BF16), jax.ShapeDtypeStruct((seq, D_MODEL), F32)]
        + [jax.ShapeDtypeStruct(s, F32) for s in acc_shapes],
        scratch_shapes=[
            pltpu.VMEM((HALO_A + tile, D_G), F32),
            pltpu.VMEM((tile + HALO_A, D_G), F32),
            pltpu.VMEM((HALO_C, D_G), F32),
            pltpu.VMEM((SHIFTS, HALO_D + tile, D_G), F32),
            pltpu.VMEM((SHIFTS, tile + HALO_D, D_G), F32),
            pltpu.VMEM((8 * CONV_D, D_G), F32),
        ],
        operands=[dxn, z, proj, proj, y, cvd, *weights],
        exchange=exchange,
    )


def _kv_forward(mem, wkv, *, name):
    def body(mem_ref, wkv_ref, kst_ref, vst_ref):
        kv = _dot(mem_ref[...].astype(BF16), wkv_ref[...])
        masks = _head_masks()
        kst_ref[...] = _stack_heads(kv[:, 0:D_G], masks).astype(BF16)
        vst_ref[...] = _stack_heads(kv[:, D_G:], masks).astype(BF16)

    shape = jax.ShapeDtypeStruct((N_SUB * MEM_LEN, D_G), BF16)
    return pl.pallas_call(body, name=name, out_shape=[shape, shape])(mem, wkv)


def _kv_backward(mem, dkst, dvst, *, name):
    def body(mem_ref, dkst_ref, dvst_ref, dwkv_ref):
        masks = _head_masks()
        memb = mem_ref[...].astype(BF16)
        for col, ref in ((0, dkst_ref), (D_G, dvst_ref)):
            d = jnp.zeros((MEM_LEN, D_G), F32)
            for h in range(N_SUB):
                d = d + ref[h * MEM_LEN : (h + 1) * MEM_LEN, :] * masks[h]
            dwkv_ref[:, col : col + D_G] = _dot_tn(memb, d.astype(BF16))

    return pl.pallas_call(body, name=name, out_shape=jax.ShapeDtypeStruct((D_MODEL, 2 * D_G), F32))(mem, dkst, dvst)


def _input_grad(dproj, dz, win, *, name, exchange=None):
    seq = dproj.shape[0]
    tile = min(MM_TILE // 2, seq)

    def body(dproj_ref, dz_ref, win_ref, dx_ref):
        acc = ALPHA * dz_ref[...]
        for k in range(N_CHIPS):
            acc = acc + _dot_nt(dproj_ref[:, k * W_IN_SHARD : (k + 1) * W_IN_SHARD], win_ref[k])
        dx_ref[...] = acc

    return _gridded_call(
        body,
        name=name,
        steps=seq // tile,
        in_specs=[
            pl.BlockSpec((tile, D_IN), lambda i: (i, 0)),
            pl.BlockSpec((tile, D_MODEL), lambda i: (i, 0)),
            _full(win.shape),
        ],
        out_specs=[pl.BlockSpec((tile, D_MODEL), lambda i: (i, 0))],
        out_shape=[jax.ShapeDtypeStruct((seq, D_MODEL), F32)],
        scratch_shapes=[],
        operands=[dproj, dz, win],
        exchange=exchange,
    )


def _input_weight_grad(x, dproj, *, name, exchange=None):
    seq = x.shape[0]
    tile = min(W_GRAD_TILE, seq)
    n_rows = seq // tile

    def body(x_ref, dproj_ref, dwin_ref):
        part = _dot_tn(x_ref[...].astype(BF16), dproj_ref[...])
        if n_rows == 1:
            dwin_ref[0] = part
        else:

            @pl.when(pl.program_id(0) % n_rows == 0)
            def _():
                dwin_ref[...] = jnp.zeros(dwin_ref.shape, F32)

            dwin_ref[0] += part

    return _gridded_call(
        body,
        name=name,
        steps=N_CHIPS * n_rows,
        in_specs=[
            pl.BlockSpec((tile, D_MODEL), lambda s: (s % n_rows, 0)),
            pl.BlockSpec((tile, W_IN_SHARD), lambda s: (s % n_rows, s // n_rows)),
        ],
        out_specs=[pl.BlockSpec((1, D_MODEL, W_IN_SHARD), lambda s: (s // n_rows, 0, 0))],
        out_shape=[jax.ShapeDtypeStruct((N_CHIPS, D_MODEL, W_IN_SHARD), F32)],
        scratch_shapes=[],
        operands=[x, dproj],
        exchange=exchange,
    )


def _expand_sgb(sg_b):
    return jnp.repeat(sg_b.T, HEAD_DIM, axis=1)


def _pack_small_weights(sg_ln_g, sg_ln_b, pool_scale, cc_dw_b, cc_ln_g, cc_ln_b, conv_a_w, cc_dw_w, sg_b):
    vec = jnp.stack([sg_ln_g, sg_ln_b, pool_scale, cc_dw_b, cc_ln_g, cc_ln_b])
    return jnp.concatenate(
        [
            jnp.pad(vec, ((0, RW_CONVA - RW_VEC - 6), (0, 0))),
            jnp.pad(conv_a_w, ((0, RW_DW - RW_CONVA - CONV_A), (0, 0))),
            jnp.pad(cc_dw_w, ((0, RW_SGB - RW_DW - CONV_D), (0, 0))),
            _expand_sgb(sg_b),
        ]
    )


def _sg_w_cat(sg_w):
    cat = jnp.transpose(sg_w, (1, 0, 2)).reshape(CHUNK, N_SUB * CHUNK)
    cat_t = jnp.transpose(sg_w, (2, 0, 1)).reshape(CHUNK, N_SUB * CHUNK)
    return cat, cat_t


def _pool_block_diag(pool_w):
    tiled = jnp.tile(pool_w.reshape(D_G, HEAD_DIM), (1, N_SUB))
    row = lax.broadcasted_iota(jnp.int32, (D_G, D_G), 0) // HEAD_DIM
    col = lax.broadcasted_iota(jnp.int32, (D_G, D_G), 1) // HEAD_DIM
    return jnp.where(row == col, tiled, 0.0)


def _prepare_layer(mem, w, l):
    cat, cat_t = _sg_w_cat(w["sg_w"])
    kst, vst = _kv_forward(mem, w["w_kv"], name=f"kv_fwd{l}")
    return dict(
        win=w["w_in"],
        wout=w["w_out"],
        pww=w["cc_pw_w"],
        sw=_pack_small_weights(
            w["sg_ln_g"], w["sg_ln_b"], w["pool_scale"], w["cc_dw_b"], w["cc_ln_g"], w["cc_ln_b"],
            w["conv_a_w"], w["cc_dw_w"], w["sg_b"],
        ),
        wcat=cat,
        wcat_t=cat_t,
        poolw=_pool_block_diag(w["pool_w"]).astype(BF16),
        ln=jnp.stack([w["ln_g"], w["ln_b"]]),
        kst=kst,
        vst=vst,
    )


def _forward(l, h, p, tgt, exchange=None):
    return _layer_forward(
        h, p["win"], p["kst"], p["vst"], p["wout"], p["sw"], p["wcat"], p["poolw"], p["pww"], p["ln"], tgt,
        name=f"layer_fwd{l}", exchange=exchange,
    )


def _backward(l, dxn, s, p, exchange=None):
    return _layer_backward(
        dxn, s[2], s[0], s[1], s[4], p["kst"], p["vst"], p["wout"], p["sw"], p["wcat"], p["wcat_t"], p["poolw"],
        p["pww"], p["ln"], name=f"layer_bwd{l}", exchange=exchange,
    )


def _place():
    x, y, c = lax.axis_index("x"), lax.axis_index("y"), lax.axis_index("c")
    others = [(1 - x, y), (x, 1 - y), (1 - x, 1 - y)]
    return x, y, c, others


def _half(ref, c, axis):
    n = ref.shape[axis] // 2
    if axis == 0:
        return ref.at[pl.ds(c * n, n)]
    return ref.at[:, pl.ds(c * n, n)]


def _place_own_block(place, stacked, layer, dtypes, *, name):
    n = len(stacked)

    def body(place_ref, *refs):
        for a in range(n):
            refs[n + a][...] = refs[a][...].astype(dtypes[a])

    def block(s):
        return (1,) + s.shape[1:]

    return pl.pallas_call(
        body,
        name=name,
        grid_spec=pltpu.PrefetchScalarGridSpec(
            num_scalar_prefetch=1,
            grid=(1,),
            in_specs=[pl.BlockSpec(block(s), lambda i, place_ref: (layer, 0, 0)) for s in stacked],
            out_specs=[pl.BlockSpec(block(s), lambda i, place_ref: (place_ref[1], 0, 0)) for s in stacked],
        ),
        out_shape=[jax.ShapeDtypeStruct((N_CHIPS,) + s.shape[1:], dt) for s, dt in zip(stacked, dtypes)],
        compiler_params=pltpu.CompilerParams(dimension_semantics=("arbitrary",), vmem_limit_bytes=VMEM_LIMIT),
    )(place, *stacked)


def _sds(a):
    return jax.ShapeDtypeStruct(a.shape, a.dtype)


def _gather_exchange(bufs):
    n = len(bufs)

    def remote(sems, block, k, to):
        return pltpu.make_async_remote_copy(
            src_ref=block, dst_ref=block, send_sem=sems[0].at[k], recv_sem=sems[1].at[k], device_id=to, device_id_type=MESH
        )

    def before(step, steps, refs, outs, sems):
        def send():
            x, y, c, others = _place()
            for j, (px, py) in enumerate(others):
                for a in range(n):
                    remote(sems, _half(refs[a].at[2 * x + y], c, 0), 3 * a + j, (px, py, c)).start()

        _when(step == 0, send)

    def after(step, steps, refs, outs, sems):
        def pass_on():
            x, y, c, others = _place()
            for j, (px, py) in enumerate(others):
                for a in range(n):
                    landed = _half(refs[a].at[2 * px + py], c, 0)
                    remote(sems, landed, 3 * a + j, (px, py, c)).wait_recv()
                    remote(sems, landed, 3 * n + 3 * a + j, (x, y, 1 - c)).start()

        def finish():
            x, y, c, others = _place()
            for j, (px, py) in enumerate(others):
                for a in range(n):
                    remote(sems, _half(refs[a].at[2 * px + py], 1 - c, 0), 3 * n + 3 * a + j, (x, y, 1 - c)).wait_recv()
            for a in range(n):
                mine = _half(refs[a].at[2 * x + y], c, 0)
                for k in range(3 * a, 3 * a + 3):
                    remote(sems, mine, k, (x, y, 1 - c)).wait_send()
                    remote(sems, mine, 3 * n + k, (x, y, 1 - c)).wait_send()

        _when(step == (3 * steps) // 4, pass_on)
        _when(step == steps - 1, finish)

    return _Exchange(bufs, [(_sds(b), a) for a, b in enumerate(bufs)], [6 * n, 6 * n], before, after)


def _swap_exchange(grads):
    n = len(grads)

    def copy(refs, outs, sems, a):
        x, y, c, _ = _place()
        return pltpu.make_async_remote_copy(
            src_ref=_half(refs[a], 1 - c, 1), dst_ref=outs[a], send_sem=sems[0].at[a], recv_sem=sems[1].at[a],
            device_id=(x, y, 1 - c), device_id_type=MESH,
        )

    def before(step, steps, refs, outs, sems):
        _when(step == 0, lambda: [copy(refs, outs, sems, a).start() for a in range(n)] and None)

    def after(step, steps, refs, outs, sems):
        _when(step == steps - 1, lambda: [copy(refs, outs, sems, a).wait() for a in range(n)] and None)

    outputs = [(jax.ShapeDtypeStruct((N_CHIPS, g.shape[1] // 2, g.shape[2]), g.dtype), None) for g in grads]
    return _Exchange(grads, outputs, [n, n], before, after)


def _add_sibling_half(place, grads, received, wire, *, name):
    n = len(grads)

    def body(place_ref, *refs):
        k = pl.program_id(0)
        for a in range(n):
            pair = (refs[a][...] + refs[n + a][...]).astype(wire[a])
            refs[2 * n + a][...] = pair

            @pl.when(k == place_ref[1])
            def _(a=a, pair=pair):
                refs[3 * n + a][...] = pair

    def block(g):
        return (1, g.shape[1] // 2, g.shape[2])

    return pl.pallas_call(
        body,
        name=name,
        grid_spec=pltpu.PrefetchScalarGridSpec(
            num_scalar_prefetch=1,
            grid=(N_CHIPS,),
            in_specs=[pl.BlockSpec(block(g), lambda k, place_ref: (k, place_ref[0], 0)) for g in grads]
            + [pl.BlockSpec(block(g), lambda k, place_ref: (k, 0, 0)) for g in grads],
            out_specs=[pl.BlockSpec(block(g), lambda k, place_ref: (k, 0, 0)) for g in grads]
            + [pl.BlockSpec(block(g), lambda k, place_ref: (place_ref[1], 0, 0)) for g in grads],
        ),
        out_shape=[jax.ShapeDtypeStruct(r.shape, dt) for r, dt in zip(received, wire)] * 2,
        compiler_params=pltpu.CompilerParams(dimension_semantics=("arbitrary",), vmem_limit_bytes=VMEM_LIMIT),
    )(place, *grads, *received)


def _scatter_exchange(pairs, landing):
    n = len(pairs)

    def copy(refs, sems, a, j, px, py):
        x, y, c, _ = _place()
        return pltpu.make_async_remote_copy(
            src_ref=refs[a].at[2 * px + py], dst_ref=refs[n + a].at[2 * x + y], send_sem=sems[0].at[3 * a + j],
            recv_sem=sems[1].at[3 * a + j], device_id=(px, py, c), device_id_type=MESH,
        )

    def before(step, steps, refs, outs, sems):
        def send():
            for j, (px, py) in enumerate(_place()[3]):
                for a in range(n):
                    copy(refs, sems, a, j, px, py).start()

        _when(step == 0, send)

    def after(step, steps, refs, outs, sems):
        def finish():
            x, y, c, others = _place()
            for j, (px, py) in enumerate(others):
                for a in range(n):
                    landed = refs[n + a].at[2 * px + py]
                    pltpu.make_async_remote_copy(
                        src_ref=landed, dst_ref=landed, send_sem=sems[0].at[3 * a + j], recv_sem=sems[1].at[3 * a + j],
                        device_id=(px, py, c), device_id_type=MESH,
                    ).wait_recv()
            for j, (px, py) in enumerate(others):
                for a in range(n):
                    copy(refs, sems, a, j, px, py).wait_send()

        _when(step == steps - 1, finish)

    return _Exchange(pairs + landing, [(_sds(b), n + a) for a, b in enumerate(landing)], [3 * n, 3 * n], before, after)


SUM_STEPS = 2


def _sum_chip_blocks(place, parts, keep_chip_axis, *, name):
    n = len(parts)

    def body(place_ref, *refs):
        for a in range(n):
            p = refs[a]
            total = (p[0].astype(F32) + p[1].astype(F32)) + (p[2].astype(F32) + p[3].astype(F32))
            if keep_chip_axis[a]:
                refs[n + a][0] = total
            else:
                refs[n + a][...] = total

    def in_spec(p):
        return pl.BlockSpec((N_CHIPS, p.shape[1] // SUM_STEPS, p.shape[2]), lambda i, place_ref: (0, i, 0))

    def out_spec(p, keep):
        rows = p.shape[1] // SUM_STEPS
        if keep:
            return pl.BlockSpec((1, rows, p.shape[2]), lambda i, place_ref: (place_ref[1], place_ref[0] * SUM_STEPS + i, 0))
        return pl.BlockSpec((rows, p.shape[2]), lambda i, place_ref: (place_ref[0] * SUM_STEPS + i, 0))

    def out_shape(p, keep):
        shape = (2 * p.shape[1], p.shape[2])
        return jax.ShapeDtypeStruct((N_CHIPS,) + shape if keep else shape, F32)

    return pl.pallas_call(
        body,
        name=name,
        grid_spec=pltpu.PrefetchScalarGridSpec(
            num_scalar_prefetch=1,
            grid=(SUM_STEPS,),
            in_specs=[in_spec(p) for p in parts],
            out_specs=[out_spec(p, k) for p, k in zip(parts, keep_chip_axis)],
        ),
        out_shape=[out_shape(p, k) for p, k in zip(parts, keep_chip_axis)],
        compiler_params=pltpu.CompilerParams(dimension_semantics=("arbitrary",), vmem_limit_bytes=VMEM_LIMIT),
    )(place, *parts)


def _join_exchange(bufs, keep_chip_axis):
    n = len(bufs)
    kept = [a for a in range(n) if keep_chip_axis[a]]
    base = n

    def copy(refs, sems, block, k, to):
        return pltpu.make_async_remote_copy(
            src_ref=block, dst_ref=block, send_sem=sems[0].at[k], recv_sem=sems[1].at[k], device_id=to, device_id_type=MESH
        )

    def mine(refs, a, cc):
        x, y, _, _ = _place()
        return _half(refs[a].at[2 * x + y] if keep_chip_axis[a] else refs[a], cc, 0)

    def before(step, steps, refs, outs, sems):
        def send():
            x, y, c, others = _place()
            for a in range(n):
                copy(refs, sems, mine(refs, a, c), a, (x, y, 1 - c)).start()
            for i, a in enumerate(kept):
                for j, (px, py) in enumerate(others):
                    copy(refs, sems, mine(refs, a, c), base + 6 * i + j, (px, py, c)).start()

        _when(step == 0, send)

    def after(step, steps, refs, outs, sems):
        def pass_on():
            x, y, c, others = _place()
            for i, a in enumerate(kept):
                for j, (px, py) in enumerate(others):
                    landed = _half(refs[a].at[2 * px + py], c, 0)
                    copy(refs, sems, landed, base + 6 * i + j, (px, py, c)).wait_recv()
                    copy(refs, sems, landed, base + 6 * i + 3 + j, (x, y, 1 - c)).start()

        def finish():
            x, y, c, others = _place()
            for a in range(n):
                copy(refs, sems, mine(refs, a, 1 - c), a, (x, y, 1 - c)).wait_recv()
            for i, a in enumerate(kept):
                for j, (px, py) in enumerate(others):
                    passed = _half(refs[a].at[2 * px + py], 1 - c, 0)
                    copy(refs, sems, passed, base + 6 * i + 3 + j, (x, y, 1 - c)).wait_recv()
            for a in range(n):
                copy(refs, sems, mine(refs, a, c), a, (x, y, 1 - c)).wait_send()
            for i, a in enumerate(kept):
                for k in range(6):
                    copy(refs, sems, mine(refs, a, c), base + 6 * i + k, (x, y, 1 - c)).wait_send()

        _when(step == steps // 2, pass_on)
        _when(step == steps - 1, finish)

    return _Exchange(bufs, [(_sds(b), a) for a, b in enumerate(bufs)], [n + 6 * len(kept)] * 2, before, after)


def _adamw(w, g, m, v):
    m = ADAM_B1 * m + (1.0 - ADAM_B1) * g
    v = ADAM_B2 * v + (1.0 - ADAM_B2) * (g * g)
    m_hat = m / (1.0 - ADAM_B1**ADAM_STEP)
    v_hat = v / (1.0 - ADAM_B2**ADAM_STEP)
    delta = -ADAM_LR * (m_hat / (jnp.sqrt(v_hat) + ADAM_EPS) + ADAM_WD * w)
    return delta, m, v


def _adamw_large(w, m, v, layer_grads, *, name):
    depth, rows, cols = w.shape
    tile = math.gcd(rows, ADAM_TILE)
    assert tile % 8 == 0

    def body(w_ref, m_ref, v_ref, *refs):
        g_refs, (g_out, d_out, m_out, v_out) = refs[:depth], refs[depth:]
        for l in range(depth):

            @pl.when(pl.program_id(0) == l)
            def _(l=l):
                g = g_refs[l][...]
                delta, m_new, v_new = _adamw(w_ref[0], g, m_ref[0], v_ref[0])
                g_out[0], d_out[0], m_out[0], v_out[0] = g, delta, m_new, v_new

    def stacked():
        return pl.BlockSpec((1, tile, cols), lambda l, i: (l, i, 0))

    def layer_spec(l):
        return pl.BlockSpec((tile, cols), lambda k, i: (jnp.where(k == l, i, 0), 0))

    shape = jax.ShapeDtypeStruct(w.shape, F32)
    return pl.pallas_call(
        body,
        name=name,
        grid=(depth, rows // tile),
        in_specs=[stacked(), stacked(), stacked()] + [layer_spec(l) for l in range(depth)],
        out_specs=[stacked()] * 4,
        out_shape=[shape] * 4,
        compiler_params=pltpu.CompilerParams(dimension_semantics=("arbitrary", "arbitrary"), vmem_limit_bytes=VMEM_LIMIT),
    )(w, m, v, *layer_grads)


def _packed_pieces(name, l):
    vecs = ("sg_ln_g", "sg_ln_b", "pool_scale", "cc_dw_b", "cc_ln_g", "cc_ln_b")
    if name in vecs:
        r = RG_VEC + vecs.index(name)
        return [((slice(l, l + 1), slice(None)), slice(r, r + 1), slice(None))]
    if name in ("ln_g", "ln_b"):
        r = RG_LN + (4 if name == "ln_b" else 0)
        return [((slice(l, l + 1), slice(j * D_G, (j + 1) * D_G)), slice(r + j, r + j + 1), slice(None)) for j in range(4)]
    assert name == "sg_w"
    return [
        ((l, h), slice(RG_SGW + CHUNK * (h // 2), RG_SGW + CHUNK * (h // 2 + 1)), slice(CHUNK * (h % 2), CHUNK * (h % 2 + 1)))
        for h in range(N_SUB)
    ]


PACKED_NAMES = ("sg_ln_g", "sg_ln_b", "pool_scale", "cc_dw_b", "cc_ln_g", "cc_ln_b", "ln_g", "ln_b", "sg_w")


def _adamw_packed(packed, ws, ms, vs, *, name):
    n, depth = len(ws), len(packed)

    def body(*refs):
        packed_refs, refs = refs[:depth], refs[depth:]
        for a, leaf in enumerate(PACKED_NAMES):
            for l in range(depth):
                for at, rows, cols in _packed_pieces(leaf, l):
                    g = packed_refs[l][rows, cols]
                    delta, m_new, v_new = _adamw(refs[a][at], g, refs[n + a][at], refs[2 * n + a][at])
                    refs[3 * n + a][at] = g
                    refs[4 * n + a][at] = delta
                    refs[5 * n + a][at] = m_new
                    refs[6 * n + a][at] = v_new

    shapes = [jax.ShapeDtypeStruct(w.shape, F32) for w in ws]
    outs = pl.pallas_call(body, name=name, out_shape=shapes * 4)(*packed, *ws, *ms, *vs)
    return outs[:n], outs[n : 2 * n], outs[2 * n : 3 * n], outs[3 * n :]


def _adamw_small(ws, gs, ms, vs, *, name):
    n = len(ws)

    def body(*refs):
        for a in range(n):
            delta, m_new, v_new = _adamw(refs[a][...], refs[n + a][...], refs[2 * n + a][...], refs[3 * n + a][...])
            refs[4 * n + a][...] = delta
            refs[5 * n + a][...] = m_new
            refs[6 * n + a][...] = v_new

    shapes = [jax.ShapeDtypeStruct(w.shape, F32) for w in ws]
    outs = pl.pallas_call(body, name=name, out_shape=shapes * 3)(*ws, *gs, *ms, *vs)
    return outs[:n], outs[n : 2 * n], outs[2 * n :]


WEIGHT_NAMES = (
    "w_in", "conv_a_w", "sg_ln_g", "sg_ln_b", "sg_w", "sg_b", "pool_w", "pool_scale", "cc_dw_w", "cc_dw_b", "cc_ln_g",
    "cc_ln_b", "cc_pw_w", "w_kv", "w_out", "ln_g", "ln_b",
)
LARGE = ("w_in", "cc_pw_w", "w_kv", "w_out")
TAPS_ROWS = 48


def _unpack_small_grads(small, chip):
    out = {}
    for r, k in enumerate(("sg_ln_g", "sg_ln_b", "pool_scale", "cc_dw_b", "cc_ln_g", "cc_ln_b")):
        out[k] = small[RG_VEC + r]
    out["conv_a_w"] = lax.dynamic_slice_in_dim(small[RG_CONVA : RG_CONVA + CONV_A], chip * HEAD_DIM, HEAD_DIM, axis=1)
    out["cc_dw_w"] = lax.dynamic_slice_in_dim(small[RG_DW : RG_DW + CONV_D], chip * HEAD_DIM, HEAD_DIM, axis=1)
    cat = jnp.concatenate([small[RG_SGW : RG_SGW + CHUNK], small[RG_SGW + CHUNK : RG_SGW + 2 * CHUNK]], axis=1)
    out["sg_w"] = jnp.transpose(cat.reshape(CHUNK, N_SUB, CHUNK), (1, 0, 2))
    out["sg_b"] = small[RG_SGB : RG_SGB + CHUNK].reshape(CHUNK, N_SUB, HEAD_DIM).sum(-1).T
    pool = small[RG_POOL : RG_POOL + D_G]
    out["pool_w"] = jnp.stack(
        [pool[g * HEAD_DIM : (g + 1) * HEAD_DIM, g * HEAD_DIM : (g + 1) * HEAD_DIM] for g in range(N_SUB)]
    )
    out["ln_g"] = small[RG_LN : RG_LN + 4].reshape(D_MODEL)
    out["ln_b"] = small[RG_LN + 4 : RG_LN + 8].reshape(D_MODEL)
    return out


def kernel(x, mem, w_in, conv_a_w, sg_ln_g, sg_ln_b, sg_w, sg_b, pool_w, pool_scale, cc_dw_w, cc_dw_b, cc_ln_g, cc_ln_b, cc_pw_w, w_kv, w_out, ln_g, ln_b, loss_target, m_w_in, m_conv_a_w, m_sg_ln_g, m_sg_ln_b, m_sg_w, m_sg_b, m_pool_w, m_pool_scale, m_cc_dw_w, m_cc_dw_b, m_cc_ln_g, m_cc_ln_b, m_cc_pw_w, m_w_kv, m_w_out, m_ln_g, m_ln_b, v_w_in, v_conv_a_w, v_sg_ln_g, v_sg_ln_b, v_sg_w, v_sg_b, v_pool_w, v_pool_scale, v_cc_dw_w, v_cc_dw_b, v_cc_ln_g, v_cc_ln_b, v_cc_pw_w, v_w_kv, v_w_out, v_ln_g, v_ln_b):
    given = dict(locals())
    weights = {k: given[k] for k in WEIGHT_NAMES}
    chip = 2 * lax.axis_index("x") + lax.axis_index("y")
    place = jnp.stack([lax.axis_index("c"), chip]).astype(jnp.int32)

    x0, mem0 = x[0], mem[0]

    taps = jnp.concatenate([conv_a_w, cc_dw_w], axis=1)
    taps = jnp.pad(taps, ((0, 0), (0, TAPS_ROWS - taps.shape[1]), (0, 0)))

    def own_blocks(l):
        return _place_own_block(
            place, [w_in, w_out, w_kv, cc_pw_w, taps], l, [BF16, BF16, BF16, BF16, F32], name=f"place_weights{l}"
        )

    def layer_operands(l, gathered):
        g_in, g_out, g_kv, g_pw, g_taps = gathered
        taps_full = jnp.transpose(g_taps, (1, 0, 2)).reshape(TAPS_ROWS, D_G)
        full = dict(
            w_in=g_in,
            w_out=g_out.reshape(D_MIX, D_MODEL),
            w_kv=g_kv.reshape(D_MODEL, 2 * D_G),
            cc_pw_w=g_pw.reshape(D_G, D_G),
            conv_a_w=taps_full[0:CONV_A],
            cc_dw_w=taps_full[CONV_A : CONV_A + CONV_D],
            **{k: weights[k][l] for k in WEIGHT_NAMES if k not in LARGE + ("conv_a_w", "cc_dw_w")},
        )
        return _prepare_layer(mem0, full, l)

    def other_grads(l, bwd, small):
        _, _, dwout, dkst, dvst, dpw, _ = bwd
        return [
            dwout.reshape(N_CHIPS, D_MIX // N_CHIPS, D_MODEL),
            _kv_backward(mem0, dkst, dvst, name=f"kv_bwd{l}").reshape(N_CHIPS, D_MODEL // N_CHIPS, 2 * D_G),
            dpw.reshape(N_CHIPS, D_G // N_CHIPS, D_G),
            small.reshape(N_CHIPS, RG_ROWS // N_CHIPS, D_G),
        ]

    n_red = 5
    keep = [False, False, False, False, True]

    wire = [BF16, BF16, BF16, BF16, F32]

    def reduced_layer(joined):
        r_in, r_out, r_kv, r_pw, small_all = joined
        packed = small_all.reshape(RG_ROWS, D_G)
        out = _unpack_small_grads(packed, chip)
        out.update(w_in=r_in, w_out=r_out, w_kv=r_kv, cc_pw_w=r_pw, packed=packed)
        return out

    blocks0, blocks1 = own_blocks(0), own_blocks(1)
    p0 = layer_operands(0, _run_exchange(_gather_exchange(blocks0), name="gather_weights0"))
    fwd0, gathered1 = _forward(0, x0, p0, None, exchange=_gather_exchange(blocks1))
    p1 = layer_operands(1, gathered1)
    x1 = fwd0[3]
    fwd1, _ = _forward(1, x1, p1, loss_target[0])

    bwd1, _ = _backward(1, fwd1[3], fwd1, p1)
    small1 = bwd1[6].at[RG_LOSS, :].set(fwd1[5][0, 0])
    (dwin1,), _ = _input_weight_grad(x1, bwd1[0], name="w_in_grad1")
    grads1 = [dwin1] + other_grads(1, bwd1, small1)
    (dx1,), received1 = _input_grad(
        bwd1[0], bwd1[1], p1["win"], name="input_grad1", exchange=_swap_exchange(grads1).starting_late()
    )
    pairs1 = _add_sibling_half(place, grads1, received1, wire, name="rs_pair1")
    bwd0, parts1 = _backward(
        0, dx1, fwd0, p0, exchange=_scatter_exchange(pairs1[:n_red], pairs1[n_red:]).starting_late()
    )
    halves1 = _sum_chip_blocks(place, parts1, keep, name="rs_sum1")
    rest0 = other_grads(0, bwd0, bwd0[6])
    n_rest = len(rest0)
    pairs0_rest = _add_sibling_half(
        place, rest0, _run_exchange(_swap_exchange(rest0), name="rs_swap0_rest"), wire[1:], name="rs_pair0_rest"
    )
    (dwin0,), carried = _input_weight_grad(
        x0, bwd0[0], name="w_in_grad0",
        exchange=_both(_join_exchange(halves1, keep), _scatter_exchange(pairs0_rest[:n_rest], pairs0_rest[n_rest:])),
    )
    joined1, parts0_rest = carried[:n_red], carried[n_red:]
    loss = joined1[4].reshape(RG_ROWS, D_G)[RG_LOSS, 0]
    pairs0_win = _add_sibling_half(
        place, [dwin0], _run_exchange(_swap_exchange([dwin0]), name="rs_swap0_win"), wire[:1], name="rs_pair0_win"
    )
    (grad_x,), parts0_win = _input_grad(
        bwd0[0], bwd0[1], p0["win"], name="input_grad0", exchange=_scatter_exchange(pairs0_win[:1], pairs0_win[1:])
    )
    halves0 = _sum_chip_blocks(place, parts0_win + parts0_rest, keep, name="rs_sum0")
    reduced = [reduced_layer(_run_exchange(_join_exchange(halves0, keep), name="rs_join0")), reduced_layer(joined1)]

    grad, delta, new_m, new_v = {}, {}, {}, {}
    for k in LARGE:
        w3 = weights[k]
        grad[k], delta[k], new_m[k], new_v[k] = _adamw_large(
            w3, given["m_" + k], given["v_" + k], [reduced[l][k] for l in range(DEPTH)], name=f"adamw_{k}"
        )
    g_p, d_p, m_p, v_p = _adamw_packed(
        [reduced[l]["packed"] for l in range(DEPTH)],
        [weights[k] for k in PACKED_NAMES],
        [given["m_" + k] for k in PACKED_NAMES],
        [given["v_" + k] for k in PACKED_NAMES],
        name="adamw_packed",
    )
    for a, k in enumerate(PACKED_NAMES):
        grad[k], delta[k], new_m[k], new_v[k] = g_p[a], d_p[a], m_p[a], v_p[a]
    small_names = [k for k in WEIGHT_NAMES if k not in LARGE + PACKED_NAMES]
    for k in small_names:
        grad[k] = jnp.stack([reduced[l][k] for l in range(DEPTH)])
    d_s, m_s, v_s = _adamw_small(
        [weights[k] for k in small_names],
        [grad[k] for k in small_names],
        [given["m_" + k] for k in small_names],
        [given["v_" + k] for k in small_names],
        name="adamw_small",
    )
    for a, k in enumerate(small_names):
        delta[k], new_m[k], new_v[k] = d_s[a], m_s[a], v_s[a]

    return (
        loss,
        grad_x[None],
        *[grad[k] for k in WEIGHT_NAMES],
        *[delta[k] for k in WEIGHT_NAMES],
        *[new_m[k] for k in WEIGHT_NAMES],
        *[new_v[k] for k in WEIGHT_NAMES],
    )
```

```python
import math

import jax
import jax.numpy as jnp
from jax import lax
from jax.experimental import pallas as pl
from jax.experimental.pallas import tpu as pltpu

F32 = jnp.float32
BF16 = jnp.bfloat16

D_MODEL = 1024
DEPTH = 2
D_G = 256
D_MIX = 5 * D_G
D_IN = 9 * D_G + D_MIX
N_SUB = 4
HEAD_DIM = 64
CONV_A = 3
CONV_D = 31
CHUNK = 128
MEM_LEN = 256
N_CHIPS = 4
W_IN_SHARD = D_IN // N_CHIPS
LN_EPS = 1e-5
ALPHA = (2.0 * DEPTH) ** 0.25
ATT_SCALE = 1.0 / math.sqrt(HEAD_DIM)
GELU_C = math.sqrt(2.0 / math.pi)
GELU_A = 0.044715

ADAM_LR = 0.001
ADAM_B1 = 0.9
ADAM_B2 = 0.999
ADAM_EPS = 1e-08
ADAM_WD = 0.01
ADAM_STEP = 10

C_XA, C_BA, C_CA, C_U, C_V, C_XC, C_DA, C_DG, C_Q, C_GATE = (D_G * i for i in range(10))

HALO_A = 8
HALO_C = 16
HALO_D = 32

RW_VEC = 0
RW_CONVA = 16
RW_DW = 24
RW_SGB = 56

RG_VEC = 0
RG_CONVA = 16
RG_DW = 24
RG_SGW = 56
RG_SGB = RG_SGW + 2 * CHUNK
RG_POOL = RG_SGB + CHUNK
RG_LN = RG_POOL + D_G
RG_LOSS = 8
RG_ROWS = 768

VMEM_LIMIT = 62 * 1024 * 1024

AUX_CVD = 0
AUX_PM = D_G
AUX_P = 2 * D_G
AUX_COLS = AUX_P + N_SUB * MEM_LEN
SEQ_TILE = 256
FWD_TILE = 512
MM_TILE = 1024
W_GRAD_TILE = 4096
ADAM_TILE = 512

MESH = pl.DeviceIdType.MESH
ANY = pl.BlockSpec(memory_space=pl.ANY)
NT = (((1,), (1,)), ((), ()))
TN = (((0,), (0,)), ((), ()))


def _dot(a, b):
    return jnp.dot(a, b, preferred_element_type=F32)


def _dot_nt(a, b):
    return lax.dot_general(a, b, NT, preferred_element_type=F32)


def _dot_tn(a, b):
    return lax.dot_general(a, b, TN, preferred_element_type=F32)


def _full(shape):
    zeros = (0,) * len(shape)
    return pl.BlockSpec(shape, lambda *_: zeros)


class _Exchange:
    def __init__(self, operands, outputs, sem_counts, before, after):
        self.operands, self.outputs, self.sem_counts, self.before, self.after = operands, outputs, sem_counts, before, after
        self.late_start = False

    def starting_late(self):
        self.late_start = True
        return self

    def specs(self, first_input, first_output):
        aliases = {first_input + src: first_output + j for j, (_, src) in enumerate(self.outputs) if src is not None}
        return (
            [ANY] * len(self.operands),
            [ANY] * len(self.outputs),
            [sds for sds, _ in self.outputs],
            [pltpu.SemaphoreType.DMA((k,)) for k in self.sem_counts],
            aliases,
        )

    def split(self, ins, outs):
        refs = list(ins)
        for j, (_, src) in enumerate(self.outputs):
            if src is not None:
                refs[src] = outs[j]
        return refs


def _both(first, second):
    n1, m1, s1 = len(first.operands), len(first.outputs), len(first.sem_counts)
    outputs = first.outputs + [(sds, None if src is None else n1 + src) for sds, src in second.outputs]

    def before(step, steps, refs, outs, sems):
        first.before(step, steps, refs[:n1], outs[:m1], sems[:s1])
        second.before(step, steps, refs[n1:], outs[m1:], sems[s1:])

    def after(step, steps, refs, outs, sems):
        first.after(step, steps, refs[:n1], outs[:m1], sems[:s1])
        second.after(step, steps, refs[n1:], outs[m1:], sems[s1:])

    return _Exchange(first.operands + second.operands, outputs, first.sem_counts + second.sem_counts, before, after)


def _when(cond, fn):
    if isinstance(cond, bool):
        if cond:
            fn()
    else:
        pl.when(cond)(fn)


def _run_exchange(exchange, *, name):
    n_in, n_out = len(exchange.operands), len(exchange.outputs)
    in_specs, out_specs, out_shape, sems, aliases = exchange.specs(0, 0)

    def body(*refs):
        ins, outs, sem_refs = refs[:n_in], refs[n_in : n_in + n_out], refs[n_in + n_out :]
        refs = exchange.split(ins, outs)
        exchange.before(0, 1, refs, outs, sem_refs)
        exchange.after(0, 1, refs, outs, sem_refs)

    return pl.pallas_call(
        body, name=name, in_specs=in_specs, out_specs=out_specs, out_shape=out_shape, scratch_shapes=sems,
        input_output_aliases=aliases,
    )(*exchange.operands)


def _gridded_call(body, *, name, steps, in_specs, out_specs, out_shape, scratch_shapes, operands, exchange=None):
    params = pltpu.CompilerParams(dimension_semantics=("arbitrary",), vmem_limit_bytes=VMEM_LIMIT)
    if exchange is None:
        outs = pl.pallas_call(
            body, name=name, grid=(steps,), in_specs=in_specs, out_specs=out_specs, out_shape=out_shape,
            scratch_shapes=scratch_shapes, compiler_params=params,
        )(*operands)
        return list(outs), []
    n_in, n_out, n_scr = len(in_specs), len(out_specs), len(scratch_shapes)
    x_in, x_out = len(exchange.operands), len(exchange.outputs)
    ex_in_specs, ex_out_specs, ex_out_shape, ex_sems, aliases = exchange.specs(n_in, n_out)

    def full(*refs):
        own_in, refs = refs[:n_in], refs[n_in:]
        ex_in, refs = refs[:x_in], refs[x_in:]
        own_out, refs = refs[:n_out], refs[n_out:]
        ex_out, refs = refs[:x_out], refs[x_out:]
        own_scr, sem_refs = refs[:n_scr], refs[n_scr:]
        ex_refs = exchange.split(ex_in, ex_out)
        step = pl.program_id(0)
        if not exchange.late_start:
            exchange.before(step, steps, ex_refs, ex_out, sem_refs)
        body(*own_in, *own_out, *own_scr)
        if exchange.late_start:
            exchange.before(step, steps, ex_refs, ex_out, sem_refs)
        exchange.after(step, steps, ex_refs, ex_out, sem_refs)

    outs = pl.pallas_call(
        full, name=name, grid=(steps,), in_specs=in_specs + ex_in_specs, out_specs=out_specs + ex_out_specs,
        out_shape=out_shape + ex_out_shape, scratch_shapes=scratch_shapes + ex_sems, input_output_aliases=aliases,
        compiler_params=params,
    )(*operands, *exchange.operands)
    return list(outs[:n_out]), list(outs[n_out:])


def _sigmoid(x):
    return 0.5 * jnp.tanh(0.5 * x) + 0.5


def _gelu(x):
    t = jnp.tanh(GELU_C * (x + GELU_A * x * x * x))
    return 0.5 * x * (1.0 + t), t


def _gelu_grad(x, t):
    return 0.5 * (1.0 + t) + 0.5 * x * (1.0 - t * t) * (GELU_C * (1.0 + 3.0 * GELU_A * x * x))


def _normalize(v):
    mu = jnp.mean(v, axis=-1, keepdims=True)
    d = v - mu
    var = jnp.mean(d * d, axis=-1, keepdims=True)
    rstd = lax.rsqrt(var + LN_EPS)
    return d * rstd, rstd


def _normalize_grad(dhat, hat, rstd):
    m1 = jnp.mean(dhat, axis=-1, keepdims=True)
    m2 = jnp.mean(dhat * hat, axis=-1, keepdims=True)
    return rstd * (dhat - m1 - hat * m2)


def _lane(width=D_G):
    return lax.broadcasted_iota(jnp.int32, (1, width), 1)


def _head_masks():
    head = _lane() // HEAD_DIM
    return [(head == h).astype(F32) for h in range(N_SUB)]


def _stack_heads(v, masks):
    return jnp.concatenate([v * m for m in masks], axis=0)


def _tril_mask_cat():
    t = lax.broadcasted_iota(jnp.int32, (CHUNK, N_SUB * CHUNK), 0)
    s = lax.broadcasted_iota(jnp.int32, (CHUNK, N_SUB * CHUNK), 1) % CHUNK
    return s <= t


def _triu_mask_cat():
    s = lax.broadcasted_iota(jnp.int32, (CHUNK, N_SUB * CHUNK), 0)
    t = lax.broadcasted_iota(jnp.int32, (CHUNK, N_SUB * CHUNK), 1) % CHUNK
    return t >= s


def _pool_select(a2, a4, a8, a16):
    lane = _lane()
    return jnp.where(lane < 64, a2, jnp.where(lane < 128, a4, jnp.where(lane < 192, a8, a16)))


def _pool_inv_count(row0, rows):
    t = row0 + lax.broadcasted_iota(jnp.int32, (HALO_C, D_G), 0)
    lane = lax.broadcasted_iota(jnp.int32, (HALO_C, D_G), 1)
    win = jnp.where(lane < 64, 2, jnp.where(lane < 128, 4, jnp.where(lane < 192, 8, 16)))
    head = 1.0 / jnp.minimum(t + 1, win).astype(F32)
    inv_win = jnp.broadcast_to(_pool_select(0.5, 0.25, 0.125, 0.0625), (rows - HALO_C, D_G))
    return jnp.concatenate([head, inv_win], axis=0)


def _trailing_window_sum(halo, cur):
    e = jnp.concatenate([halo, cur], axis=0)
    s2 = e + pltpu.roll(e, 1, 0)
    s4 = s2 + pltpu.roll(s2, 2, 0)
    s8 = s4 + pltpu.roll(s4, 4, 0)
    s16 = s8 + pltpu.roll(s8, 8, 0)
    return _pool_select(s2, s4, s8, s16)[HALO_C:]


def _leading_window_sum(cur, halo):
    e = jnp.concatenate([cur, halo], axis=0)
    n = e.shape[0]
    s2 = e + pltpu.roll(e, n - 1, 0)
    s4 = s2 + pltpu.roll(s2, n - 2, 0)
    s8 = s4 + pltpu.roll(s4, n - 4, 0)
    s16 = s8 + pltpu.roll(s8, n - 8, 0)
    return _pool_select(s2, s4, s8, s16)[: cur.shape[0]]


def _softmax_blocks(sc):
    out = []
    for h in range(N_SUB):
        s = sc[:, h * MEM_LEN : (h + 1) * MEM_LEN]
        e = jnp.exp(s - jnp.max(s, axis=-1, keepdims=True))
        out.append(e * (1.0 / jnp.sum(e, axis=-1, keepdims=True)))
    return jnp.concatenate(out, axis=-1)


STRIP = 32
SHIFTS = 8


def _fill_shifts(buf):
    n = buf.shape[1] - SHIFTS
    for r in range(1, SHIFTS):
        buf[r, 0:n, :] = buf[0, r : r + n, :]


def _shifted(buf, off, rows):
    r = off % SHIFTS
    return buf[r, off - r : off - r + rows, :]


def _sgu_mix(vn, wcat_b, sgb, masks):
    vbd = _stack_heads(vn, masks).astype(BF16)
    return _dot(wcat_b, vbd) + sgb, vbd


def _layer_forward(x, win, kst, vst, wout, sw, wcat, poolw, pww, ln, tgt, *, name, exchange=None):
    seq = x.shape[0]
    tile = min(FWD_TILE, seq)
    n_tiles = seq // tile
    last = tgt is not None

    def body(*refs):
        x_ref, win_ref, kst_ref, vst_ref, wout_ref, sw_ref, wcat_ref, pool_ref, pw_ref, ln_ref = refs[:10]
        refs = refs[10:]
        if last:
            tgt_ref, refs = refs[0], refs[1:]
        proj_ref, y_ref, z_ref, out_ref, aux_ref = refs[:5]
        refs = refs[5:]
        if last:
            loss_ref, refs = refs[0], refs[1:]
        pbuf, xchalo, gbuf = refs
        i = pl.program_id(0)

        @pl.when(i == 0)
        def _():
            pbuf[0:HALO_A, :] = jnp.zeros((HALO_A, D_G), F32)
            xchalo[...] = jnp.zeros((HALO_C, D_G), F32)
            gbuf[0, 0:HALO_D, :] = jnp.zeros((HALO_D, D_G), F32)
            if last:
                loss_ref[...] = jnp.zeros((8, 128), F32)

        xt = x_ref[...]
        xb = xt.astype(BF16)

        blocks = {}

        def project(k):
            blocks[k] = _dot(xb, win_ref[k])
            proj_ref[:, k * W_IN_SHARD : (k + 1) * W_IN_SHARD] = blocks[k]

        def cols(start, width=D_G):
            parts, c = [], start
            while c < start + width:
                k, lo = divmod(c, W_IN_SHARD)
                hi = min(W_IN_SHARD, lo + start + width - c)
                parts.append(blocks[k][:, lo:hi])
                c += hi - lo
            return parts[0] if len(parts) == 1 else jnp.concatenate(parts, axis=1)

        project(0)
        project(1)
        masks = _head_masks()

        pbuf[HALO_A : HALO_A + tile, :] = cols(C_CA) * cols(C_XA)
        cv = jnp.zeros((tile, D_G), F32)
        for k in range(CONV_A):
            off = HALO_A - (CONV_A - 1) + k
            cv = cv + sw_ref[RW_CONVA + k : RW_CONVA + k + 1, :] * pbuf[off : off + tile, :]
        y_ref[:, 0:D_G] = cols(C_BA) * cv
        pbuf[0:HALO_A, :] = pbuf[tile : tile + HALO_A, :]

        project(2)

        ua, _ = _gelu(cols(C_U))
        vg, _ = _gelu(cols(C_V))
        vhat, _ = _normalize(vg)
        vn = vhat * sw_ref[RW_VEC : RW_VEC + 1, :] + sw_ref[RW_VEC + 1 : RW_VEC + 2, :]
        wcat_b = jnp.where(_tril_mask_cat(), wcat_ref[...], 0.0).astype(BF16)
        sgb = sw_ref[RW_SGB : RW_SGB + CHUNK, :]
        for j in range(tile // CHUNK):
            rows = slice(j * CHUNK, (j + 1) * CHUNK)
            mixed, _ = _sgu_mix(vn[rows], wcat_b, sgb, masks)
            y_ref[rows, D_G : 2 * D_G] = ua[rows] * mixed

        xc = cols(C_XC)
        wsum = _trailing_window_sum(xchalo[...], xc)
        pm = wsum * _pool_inv_count(i * tile, tile) - xc
        aux_ref[:, AUX_PM : AUX_PM + D_G] = pm
        y_ref[:, 2 * D_G : 3 * D_G] = _dot(pm.astype(BF16), pool_ref[...]) * sw_ref[RW_VEC + 2 : RW_VEC + 3, :]
        xchalo[...] = xc[tile - HALO_C :, :]

        project(3)

        gbuf[0, HALO_D : HALO_D + tile, :] = cols(C_DA) * _sigmoid(cols(C_DG))
        _fill_shifts(gbuf)
        for r0 in range(0, tile, STRIP):
            acc = jnp.zeros((STRIP, D_G), F32) + sw_ref[RW_VEC + 3 : RW_VEC + 4, :]
            for k in range(CONV_D):
                off = HALO_D - (CONV_D - 1) + k
                acc = acc + sw_ref[RW_DW + k : RW_DW + k + 1, :] * _shifted(gbuf, off + r0, STRIP)
            aux_ref[r0 : r0 + STRIP, AUX_CVD : AUX_CVD + D_G] = acc
        nhat, _ = _normalize(aux_ref[:, AUX_CVD : AUX_CVD + D_G])
        nrm = nhat * sw_ref[RW_VEC + 4 : RW_VEC + 5, :] + sw_ref[RW_VEC + 5 : RW_VEC + 6, :]
        y_ref[:, 3 * D_G : 4 * D_G] = _dot((nrm * _sigmoid(nrm)).astype(BF16), pw_ref[...])
        gbuf[0, 0:HALO_D, :] = gbuf[0, tile : tile + HALO_D, :]

        qb = cols(C_Q).astype(BF16)
        p_all = _softmax_blocks(_dot_nt(qb, kst_ref[...]) * ATT_SCALE)
        aux_ref[:, AUX_P:] = p_all
        y_ref[:, 4 * D_G : 5 * D_G] = _dot(p_all.astype(BF16), vst_ref[...])

        gate = cols(C_GATE, D_MIX)
        hid = y_ref[...] * (gate * _sigmoid(gate))
        z = ALPHA * xt + _dot(hid.astype(BF16), wout_ref[...])
        z_ref[...] = z
        zhat, _ = _normalize(z)
        xn = zhat * ln_ref[0:1, :] + ln_ref[1:2, :]
        if last:
            err = xn - tgt_ref[...]
            out_ref[...] = err * (1.0 / D_MODEL)
            loss_ref[...] += jnp.sum(err * err) * (0.5 / D_MODEL)
        else:
            out_ref[...] = xn

    def rows(width):
        return pl.BlockSpec((tile, width), lambda i: (i, 0))

    operands = [x, win, kst, vst, wout, sw, wcat, poolw, pww, ln]
    in_specs = [rows(D_MODEL)] + [_full(a.shape) for a in operands[1:]]
    widths = [D_IN, D_MIX, D_MODEL, D_MODEL, AUX_COLS]
    out_shape = [jax.ShapeDtypeStruct((seq, w), F32) for w in widths]
    out_specs = [rows(w) for w in widths]
    if last:
        operands.append(tgt)
        in_specs.append(rows(D_MODEL))
        out_shape.append(jax.ShapeDtypeStruct((8, 128), F32))
        out_specs.append(_full((8, 128)))
    return _gridded_call(
        body,
        name=name,
        steps=n_tiles,
        in_specs=in_specs,
        out_specs=out_specs,
        out_shape=out_shape,
        scratch_shapes=[
            pltpu.VMEM((HALO_A + tile, D_G), F32),
            pltpu.VMEM((HALO_C, D_G), F32),
            pltpu.VMEM((SHIFTS, HALO_D + tile, D_G), F32),
        ],
        operands=operands,
        exchange=exchange,
    )


def _layer_backward(dxn, z, proj, y, cvd, kst, vst, wout, sw, wcat, wcat_t, poolw, pww, ln, *, name, exchange=None):
    seq = dxn.shape[0]
    tile = min(SEQ_TILE, seq)
    n_tiles = seq // tile
    halo_blocks = tile // HALO_D

    def body(
        dxn_ref, z_ref, proj_ref, halo_ref, y_ref, aux_ref, kst_ref, vst_ref, wout_ref, sw_ref, wcat_ref, wcat_t_ref,
        pool_ref, pw_ref, ln_ref, dproj_ref, dz_ref, dwout_ref, dkst_ref, dvst_ref, dpw_ref, sg_ref,
        pbuf, dcvbuf, rhalo, gbuf, dgbuf, dwacc,
    ):
        i = pl.program_id(0)
        ti = n_tiles - 1 - i

        @pl.when(i == 0)
        def _():
            dwout_ref[...] = jnp.zeros(dwout_ref.shape, F32)
            dkst_ref[...] = jnp.zeros(dkst_ref.shape, F32)
            dvst_ref[...] = jnp.zeros(dvst_ref.shape, F32)
            dpw_ref[...] = jnp.zeros(dpw_ref.shape, F32)
            sg_ref[...] = jnp.zeros(sg_ref.shape, F32)
            dwacc[...] = jnp.zeros(dwacc.shape, F32)
            dcvbuf[tile : tile + HALO_A, :] = jnp.zeros((HALO_A, D_G), F32)
            rhalo[...] = jnp.zeros((HALO_C, D_G), F32)
            dgbuf[0, tile : tile + HALO_D, :] = jnp.zeros((HALO_D, D_G), F32)

        def acc_row(row, val):
            sg_ref[row : row + 1, :] += jnp.sum(val, axis=0, keepdims=True)

        masks = _head_masks()
        has_past = (ti > 0).astype(F32)

        zhat, zrstd = _normalize(z_ref[...])
        dxn_t = dxn_ref[...]
        dlg = jnp.sum(dxn_t * zhat, axis=0, keepdims=True)
        dlb = jnp.sum(dxn_t, axis=0, keepdims=True)
        for j in range(D_MODEL // D_G):
            sg_ref[RG_LN + j : RG_LN + j + 1, :] += dlg[:, j * D_G : (j + 1) * D_G]
            sg_ref[RG_LN + 4 + j : RG_LN + 5 + j, :] += dlb[:, j * D_G : (j + 1) * D_G]
        dz = _normalize_grad(dxn_t * ln_ref[0:1, :], zhat, zrstd)
        dz_ref[...] = dz
        dzb = dz.astype(BF16)

        dy = []
        for b in range(D_MIX // D_G):
            rows_b = slice(b * D_G, (b + 1) * D_G)
            gate = proj_ref[:, C_GATE + b * D_G : C_GATE + (b + 1) * D_G]
            sgm = _sigmoid(gate)
            silu = gate * sgm
            yc = y_ref[:, rows_b]
            dwout_ref[rows_b, :] += _dot_tn((yc * silu).astype(BF16), dzb)
            dh = _dot_nt(dzb, wout_ref[rows_b, :])
            dproj_ref[:, C_GATE + b * D_G : C_GATE + (b + 1) * D_G] = (
                dh * yc * (sgm * (1.0 + gate * (1.0 - sgm)))
            ).astype(BF16)
            dy.append(dh * silu)

        dyeb = dy[4].astype(BF16)
        qb = proj_ref[:, C_Q : C_Q + D_G].astype(BF16)
        kst_b = kst_ref[...]
        p_all = aux_ref[:, AUX_P:]
        dvst_ref[...] += _dot_tn(p_all.astype(BF16), dyeb)
        dp_all = _dot_nt(dyeb, vst_ref[...])
        ds = []
        for h in range(N_SUB):
            blk = slice(h * MEM_LEN, (h + 1) * MEM_LEN)
            p, dpb = p_all[:, blk], dp_all[:, blk]
            ds.append(p * (dpb - jnp.sum(dpb * p, axis=-1, keepdims=True)))
        dsb = (jnp.concatenate(ds, axis=-1) * ATT_SCALE).astype(BF16)
        dproj_ref[:, C_Q : C_Q + D_G] = _dot(dsb, kst_b).astype(BF16)
        dkst_ref[...] += _dot_tn(dsb, qb)

        dya = dy[0]
        xa = proj_ref[:, C_XA : C_XA + D_G]
        ba = proj_ref[:, C_BA : C_BA + D_G]
        ca = proj_ref[:, C_CA : C_CA + D_G]
        past = slice(HALO_D - HALO_A, HALO_D)
        pbuf[0:HALO_A, :] = halo_ref[past, C_CA : C_CA + D_G] * halo_ref[past, C_XA : C_XA + D_G] * has_past
        pbuf[HALO_A : HALO_A + tile, :] = ca * xa
        cv = jnp.zeros((tile, D_G), F32)
        for k in range(CONV_A):
            off = HALO_A - (CONV_A - 1) + k
            cv = cv + sw_ref[RW_CONVA + k : RW_CONVA + k + 1, :] * pbuf[off : off + tile, :]
        dproj_ref[:, C_BA : C_BA + D_G] = (dya * cv).astype(BF16)
        dcv = dya * ba
        dcvbuf[0:tile, :] = dcv
        dp = jnp.zeros((tile, D_G), F32)
        for k in range(CONV_A):
            off = HALO_A - (CONV_A - 1) + k
            acc_row(RG_CONVA + k, dcv * pbuf[off : off + tile, :])
            back = CONV_A - 1 - k
            dp = dp + sw_ref[RW_CONVA + k : RW_CONVA + k + 1, :] * dcvbuf[back : back + tile, :]
        dproj_ref[:, C_CA : C_CA + D_G] = (dp * xa).astype(BF16)
        dproj_ref[:, C_XA : C_XA + D_G] = (dp * ca).astype(BF16)
        dcvbuf[tile : tile + HALO_A, :] = dcvbuf[0:HALO_A, :]

        dyb = dy[1]
        u = proj_ref[:, C_U : C_U + D_G]
        v = proj_ref[:, C_V : C_V + D_G]
        ua, ut = _gelu(u)
        vg, vt = _gelu(v)
        vhat, vrstd = _normalize(vg)
        sg_g = sw_ref[RW_VEC : RW_VEC + 1, :]
        vn = vhat * sg_g + sw_ref[RW_VEC + 1 : RW_VEC + 2, :]
        tril = _tril_mask_cat()
        wcat_b = jnp.where(tril, wcat_ref[...], 0.0).astype(BF16)
        wcat_tb = jnp.where(_triu_mask_cat(), wcat_t_ref[...], 0.0).astype(BF16)
        sgb = sw_ref[RW_SGB : RW_SGB + CHUNK, :]
        dmixed = dyb * ua
        dvn_parts = []
        du_parts = []
        dwcat = jnp.zeros((CHUNK, N_SUB * CHUNK), F32)
        dsgb = jnp.zeros((CHUNK, D_G), F32)
        for j in range(tile // CHUNK):
            rows = slice(j * CHUNK, (j + 1) * CHUNK)
            mixed, vbd = _sgu_mix(vn[rows], wcat_b, sgb, masks)
            du_parts.append(dyb[rows] * mixed)
            dmx = dmixed[rows]
            dsgb = dsgb + dmx
            dwcat = dwcat + _dot_nt(dmx.astype(BF16), vbd)
            dvn_parts.append(_dot(wcat_tb, _stack_heads(dmx, masks).astype(BF16)))
        dwcat = jnp.where(tril, dwcat, 0.0)
        sg_ref[RG_SGW : RG_SGW + CHUNK, :] += dwcat[:, 0:D_G]
        sg_ref[RG_SGW + CHUNK : RG_SGW + 2 * CHUNK, :] += dwcat[:, D_G:]
        sg_ref[RG_SGB : RG_SGB + CHUNK, :] += dsgb
        dvn = jnp.concatenate(dvn_parts, axis=0)
        du_act = jnp.concatenate(du_parts, axis=0)
        acc_row(RG_VEC, dvn * vhat)
        acc_row(RG_VEC + 1, dvn)
        dvg = _normalize_grad(dvn * sg_g, vhat, vrstd)
        dproj_ref[:, C_U : C_U + D_G] = (du_act * _gelu_grad(u, ut)).astype(BF16)
        dproj_ref[:, C_V : C_V + D_G] = (dvg * _gelu_grad(v, vt)).astype(BF16)

        dyc = dy[2]
        inv_cnt = _pool_inv_count(ti * tile, tile)
        pmb = aux_ref[:, AUX_PM : AUX_PM + D_G].astype(BF16)
        pool_b = pool_ref[...]
        scale = sw_ref[RW_VEC + 2 : RW_VEC + 3, :]
        acc_row(RG_VEC + 2, dyc * _dot(pmb, pool_b))
        dpre = (dyc * scale).astype(BF16)
        sg_ref[RG_POOL : RG_POOL + D_G, :] += _dot_tn(pmb, dpre)
        dpm = _dot_nt(dpre, pool_b)
        r = dpm * inv_cnt
        dproj_ref[:, C_XC : C_XC + D_G] = (_leading_window_sum(r, rhalo[...]) - dpm).astype(BF16)
        rhalo[...] = r[0:HALO_C, :]

        dyd = dy[3]
        da = proj_ref[:, C_DA : C_DA + D_G]
        sgd = _sigmoid(proj_ref[:, C_DG : C_DG + D_G])
        gbuf[0, 0:HALO_D, :] = halo_ref[:, C_DA : C_DA + D_G] * _sigmoid(halo_ref[:, C_DG : C_DG + D_G]) * has_past
        gbuf[0, HALO_D : HALO_D + tile, :] = da * sgd
        _fill_shifts(gbuf)
        nhat, nrstd = _normalize(aux_ref[:, AUX_CVD : AUX_CVD + D_G])
        cc_g = sw_ref[RW_VEC + 4 : RW_VEC + 5, :]
        nrm = nhat * cc_g + sw_ref[RW_VEC + 5 : RW_VEC + 6, :]
        sgn = _sigmoid(nrm)
        dydb = dyd.astype(BF16)
        dpw_ref[...] += _dot_tn((nrm * sgn).astype(BF16), dydb)
        dn = _dot_nt(dydb, pw_ref[...]) * (sgn * (1.0 + nrm * (1.0 - sgn)))
        acc_row(RG_VEC + 4, dn * nhat)
        acc_row(RG_VEC + 5, dn)
        dcvd = _normalize_grad(dn * cc_g, nhat, nrstd)
        acc_row(RG_VEC + 3, dcvd)
        dgbuf[0, 0:tile, :] = dcvd
        _fill_shifts(dgbuf)
        for r0 in range(0, tile, STRIP):
            d_s = dgbuf[0, r0 : r0 + STRIP, :]
            dg = jnp.zeros((STRIP, D_G), F32)
            for k in range(CONV_D):
                off = HALO_D - (CONV_D - 1) + k
                prod = d_s * _shifted(gbuf, off + r0, STRIP)
                part = prod[0:8]
                for q in range(8, STRIP, 8):
                    part = part + prod[q : q + 8]
                dwacc[8 * k : 8 * k + 8, :] += part
                back = CONV_D - 1 - k
                dg = dg + sw_ref[RW_DW + k : RW_DW + k + 1, :] * _shifted(dgbuf, back + r0, STRIP)
            da_s = proj_ref[r0 : r0 + STRIP, C_DA : C_DA + D_G]
            sgd_s = _sigmoid(proj_ref[r0 : r0 + STRIP, C_DG : C_DG + D_G])
            dproj_ref[r0 : r0 + STRIP, C_DA : C_DA + D_G] = (dg * sgd_s).astype(BF16)
            dproj_ref[r0 : r0 + STRIP, C_DG : C_DG + D_G] = (dg * da_s * sgd_s * (1.0 - sgd_s)).astype(BF16)
        dgbuf[0, tile : tile + HALO_D, :] = dgbuf[0, 0:HALO_D, :]

        @pl.when(i == n_tiles - 1)
        def _():
            for k in range(CONV_D):
                sg_ref[RG_DW + k : RG_DW + k + 1, :] = jnp.sum(dwacc[8 * k : 8 * k + 8, :], axis=0, keepdims=True)

    def rows(width):
        return pl.BlockSpec((tile, width), lambda i: (n_tiles - 1 - i, 0))

    halo_spec = pl.BlockSpec((HALO_D, D_IN), lambda i: (jnp.maximum((n_tiles - 1 - i) * halo_blocks - 1, 0), 0))
    weights = [kst, vst, wout, sw, wcat, wcat_t, poolw, pww, ln]
    acc_shapes = [(D_MIX, D_MODEL), (N_SUB * MEM_LEN, D_G), (N_SUB * MEM_LEN, D_G), (D_G, D_G), (RG_ROWS, D_G)]
    return _gridded_call(
        body,
        name=name,
        steps=n_tiles,
        in_specs=[rows(D_MODEL), rows(D_MODEL), rows(D_IN), halo_spec, rows(D_MIX), rows(AUX_COLS)]
        + [_full(a.shape) for a in weights],
        out_specs=[rows(D_IN), rows(D_MODEL)] + [_full(s) for s in acc_shapes],
        out_shape=[jax.ShapeDtypeStruct((seq, D_IN), BF16), jax.ShapeDtypeStruct((seq, D_MODEL), F32)]
        + [jax.ShapeDtypeStruct(s, F32) for s in acc_shapes],
        scratch_shapes=[
            pltpu.VMEM((HALO_A + tile, D_G), F32),
            pltpu.VMEM((tile + HALO_A, D_G), F32),
            pltpu.VMEM((HALO_C, D_G), F32),
            pltpu.VMEM((SHIFTS, HALO_D + tile, D_G), F32),
            pltpu.VMEM((SHIFTS, tile + HALO_D, D_G), F32),
            pltpu.VMEM((8 * CONV_D, D_G), F32),
        ],
        operands=[dxn, z, proj, proj, y, cvd, *weights],
        exchange=exchange,
    )


def _kv_forward(mem, wkv, *, name):
    def body(mem_ref, wkv_ref, kst_ref, vst_ref):
        kv = _dot(mem_ref[...].astype(BF16), wkv_ref[...])
        masks = _head_masks()
        kst_ref[...] = _stack_heads(kv[:, 0:D_G], masks).astype(BF16)
        vst_ref[...] = _stack_heads(kv[:, D_G:], masks).astype(BF16)

    shape = jax.ShapeDtypeStruct((N_SUB * MEM_LEN, D_G), BF16)
    return pl.pallas_call(body, name=name, out_shape=[shape, shape])(mem, wkv)


def _kv_backward(mem, dkst, dvst, *, name):
    def body(mem_ref, dkst_ref, dvst_ref, dwkv_ref):
        masks = _head_masks()
        memb = mem_ref[...].astype(BF16)
        for col, ref in ((0, dkst_ref), (D_G, dvst_ref)):
            d = jnp.zeros((MEM_LEN, D_G), F32)
            for h in range(N_SUB):
                d = d + ref[h * MEM_LEN : (h + 1) * MEM_LEN, :] * masks[h]
            dwkv_ref[:, col : col + D_G] = _dot_tn(memb, d.astype(BF16))

    return pl.pallas_call(body, name=name, out_shape=jax.ShapeDtypeStruct((D_MODEL, 2 * D_G), F32))(mem, dkst, dvst)


def _input_grad(dproj, dz, win, *, name, exchange=None):
    seq = dproj.shape[0]
    tile = min(MM_TILE // 2, seq)

    def body(dproj_ref, dz_ref, win_ref, dx_ref):
        acc = ALPHA * dz_ref[...]
        for k in range(N_CHIPS):
            acc = acc + _dot_nt(dproj_ref[:, k * W_IN_SHARD : (k + 1) * W_IN_SHARD], win_ref[k])
        dx_ref[...] = acc

    return _gridded_call(
        body,
        name=name,
        steps=seq // tile,
        in_specs=[
            pl.BlockSpec((tile, D_IN), lambda i: (i, 0)),
            pl.BlockSpec((tile, D_MODEL), lambda i: (i, 0)),
            _full(win.shape),
        ],
        out_specs=[pl.BlockSpec((tile, D_MODEL), lambda i: (i, 0))],
        out_shape=[jax.ShapeDtypeStruct((seq, D_MODEL), F32)],
        scratch_shapes=[],
        operands=[dproj, dz, win],
        exchange=exchange,
    )


def _input_weight_grad(x, dproj, *, name, exchange=None):
    seq = x.shape[0]
    tile = min(W_GRAD_TILE, seq)
    n_rows = seq // tile

    def body(x_ref, dproj_ref, dwin_ref):
        part = _dot_tn(x_ref[...].astype(BF16), dproj_ref[...])
        if n_rows == 1:
            dwin_ref[0] = part
        else:

            @pl.when(pl.program_id(0) % n_rows == 0)
            def _():
                dwin_ref[...] = jnp.zeros(dwin_ref.shape, F32)

            dwin_ref[0] += part

    return _gridded_call(
        body,
        name=name,
        steps=N_CHIPS * n_rows,
        in_specs=[
            pl.BlockSpec((tile, D_MODEL), lambda s: (s % n_rows, 0)),
            pl.BlockSpec((tile, W_IN_SHARD), lambda s: (s % n_rows, s // n_rows)),
        ],
        out_specs=[pl.BlockSpec((1, D_MODEL, W_IN_SHARD), lambda s: (s // n_rows, 0, 0))],
        out_shape=[jax.ShapeDtypeStruct((N_CHIPS, D_MODEL, W_IN_SHARD), F32)],
        scratch_shapes=[],
        operands=[x, dproj],
        exchange=exchange,
    )


def _expand_sgb(sg_b):
    return jnp.repeat(sg_b.T, HEAD_DIM, axis=1)


def _pack_small_weights(sg_ln_g, sg_ln_b, pool_scale, cc_dw_b, cc_ln_g, cc_ln_b, conv_a_w, cc_dw_w, sg_b):
    vec = jnp.stack([sg_ln_g, sg_ln_b, pool_scale, cc_dw_b, cc_ln_g, cc_ln_b])
    return jnp.concatenate(
        [
            jnp.pad(vec, ((0, RW_CONVA - RW_VEC - 6), (0, 0))),
            jnp.pad(conv_a_w, ((0, RW_DW - RW_CONVA - CONV_A), (0, 0))),
            jnp.pad(cc_dw_w, ((0, RW_SGB - RW_DW - CONV_D), (0, 0))),
            _expand_sgb(sg_b),
        ]
    )


def _sg_w_cat(sg_w):
    cat = jnp.transpose(sg_w, (1, 0, 2)).reshape(CHUNK, N_SUB * CHUNK)
    cat_t = jnp.transpose(sg_w, (2, 0, 1)).reshape(CHUNK, N_SUB * CHUNK)
    return cat, cat_t


def _pool_block_diag(pool_w):
    tiled = jnp.tile(pool_w.reshape(D_G, HEAD_DIM), (1, N_SUB))
    row = lax.broadcasted_iota(jnp.int32, (D_G, D_G), 0) // HEAD_DIM
    col = lax.broadcasted_iota(jnp.int32, (D_G, D_G), 1) // HEAD_DIM
    return jnp.where(row == col, tiled, 0.0)


def _prepare_layer(mem, w, l):
    cat, cat_t = _sg_w_cat(w["sg_w"])
    kst, vst = _kv_forward(mem, w["w_kv"], name=f"kv_fwd{l}")
    return dict(
        win=w["w_in"],
        wout=w["w_out"],
        pww=w["cc_pw_w"],
        sw=_pack_small_weights(
            w["sg_ln_g"], w["sg_ln_b"], w["pool_scale"], w["cc_dw_b"], w["cc_ln_g"], w["cc_ln_b"],
            w["conv_a_w"], w["cc_dw_w"], w["sg_b"],
        ),
        wcat=cat,
        wcat_t=cat_t,
        poolw=_pool_block_diag(w["pool_w"]).astype(BF16),
        ln=jnp.stack([w["ln_g"], w["ln_b"]]),
        kst=kst,
        vst=vst,
    )


def _forward(l, h, p, tgt, exchange=None):
    return _layer_forward(
        h, p["win"], p["kst"], p["vst"], p["wout"], p["sw"], p["wcat"], p["poolw"], p["pww"], p["ln"], tgt,
        name=f"layer_fwd{l}", exchange=exchange,
    )


def _backward(l, dxn, s, p, exchange=None):
    return _layer_backward(
        dxn, s[2], s[0], s[1], s[4], p["kst"], p["vst"], p["wout"], p["sw"], p["wcat"], p["wcat_t"], p["poolw"],
        p["pww"], p["ln"], name=f"layer_bwd{l}", exchange=exchange,
    )


def _place():
    x, y, c = lax.axis_index("x"), lax.axis_index("y"), lax.axis_index("c")
    others = [(1 - x, y), (x, 1 - y), (1 - x, 1 - y)]
    return x, y, c, others


def _half(ref, c, axis):
    n = ref.shape[axis] // 2
    if axis == 0:
        return ref.at[pl.ds(c * n, n)]
    return ref.at[:, pl.ds(c * n, n)]


def _place_own_block(place, stacked, layer, dtypes, *, name):
    n = len(stacked)

    def body(place_ref, *refs):
        for a in range(n):
            refs[n + a][...] = refs[a][...].astype(dtypes[a])

    def block(s):
        return (1,) + s.shape[1:]

    return pl.pallas_call(
        body,
        name=name,
        grid_spec=pltpu.PrefetchScalarGridSpec(
            num_scalar_prefetch=1,
            grid=(1,),
            in_specs=[pl.BlockSpec(block(s), lambda i, place_ref: (layer, 0, 0)) for s in stacked],
            out_specs=[pl.BlockSpec(block(s), lambda i, place_ref: (place_ref[1], 0, 0)) for s in stacked],
        ),
        out_shape=[jax.ShapeDtypeStruct((N_CHIPS,) + s.shape[1:], dt) for s, dt in zip(stacked, dtypes)],
        compiler_params=pltpu.CompilerParams(dimension_semantics=("arbitrary",), vmem_limit_bytes=VMEM_LIMIT),
    )(place, *stacked)


def _sds(a):
    return jax.ShapeDtypeStruct(a.shape, a.dtype)


def _gather_exchange(bufs):
    n = len(bufs)

    def remote(sems, block, k, to):
        return pltpu.make_async_remote_copy(
            src_ref=block, dst_ref=block, send_sem=sems[0].at[k], recv_sem=sems[1].at[k], device_id=to, device_id_type=MESH
        )

    def before(step, steps, refs, outs, sems):
        def send():
            x, y, c, others = _place()
            for j, (px, py) in enumerate(others):
                for a in range(n):
                    remote(sems, _half(refs[a].at[2 * x + y], c, 0), 3 * a + j, (px, py, c)).start()

        _when(step == 0, send)

    def after(step, steps, refs, outs, sems):
        def pass_on():
            x, y, c, others = _place()
            for j, (px, py) in enumerate(others):
                for a in range(n):
                    landed = _half(refs[a].at[2 * px + py], c, 0)
                    remote(sems, landed, 3 * a + j, (px, py, c)).wait_recv()
                    remote(sems, landed, 3 * n + 3 * a + j, (x, y, 1 - c)).start()

        def finish():
            x, y, c, others = _place()
            for j, (px, py) in enumerate(others):
                for a in range(n):
                    remote(sems, _half(refs[a].at[2 * px + py], 1 - c, 0), 3 * n + 3 * a + j, (x, y, 1 - c)).wait_recv()
            for a in range(n):
                mine = _half(refs[a].at[2 * x + y], c, 0)
                for k in range(3 * a, 3 * a + 3):
                    remote(sems, mine, k, (x, y, 1 - c)).wait_send()
                    remote(sems, mine, 3 * n + k, (x, y, 1 - c)).wait_send()

        _when(step == (3 * steps) // 4, pass_on)
        _when(step == steps - 1, finish)

    return _Exchange(bufs, [(_sds(b), a) for a, b in enumerate(bufs)], [6 * n, 6 * n], before, after)


def _swap_exchange(grads):
    n = len(grads)

    def copy(refs, outs, sems, a):
        x, y, c, _ = _place()
        return pltpu.make_async_remote_copy(
            src_ref=_half(refs[a], 1 - c, 1), dst_ref=outs[a], send_sem=sems[0].at[a], recv_sem=sems[1].at[a],
            device_id=(x, y, 1 - c), device_id_type=MESH,
        )

    def before(step, steps, refs, outs, sems):
        _when(step == 0, lambda: [copy(refs, outs, sems, a).start() for a in range(n)] and None)

    def after(step, steps, refs, outs, sems):
        _when(step == steps - 1, lambda: [copy(refs, outs, sems, a).wait() for a in range(n)] and None)

    outputs = [(jax.ShapeDtypeStruct((N_CHIPS, g.shape[1] // 2, g.shape[2]), g.dtype), None) for g in grads]
    return _Exchange(grads, outputs, [n, n], before, after)


def _add_sibling_half(place, grads, received, wire, *, name):
    n = len(grads)

    def body(place_ref, *refs):
        k = pl.program_id(0)
        for a in range(n):
            pair = (refs[a][...] + refs[n + a][...]).astype(wire[a])
            refs[2 * n + a][...] = pair

            @pl.when(k == place_ref[1])
            def _(a=a, pair=pair):
                refs[3 * n + a][...] = pair

    def block(g):
        return (1, g.shape[1] // 2, g.shape[2])

    return pl.pallas_call(
        body,
        name=name,
        grid_spec=pltpu.PrefetchScalarGridSpec(
            num_scalar_prefetch=1,
            grid=(N_CHIPS,),
            in_specs=[pl.BlockSpec(block(g), lambda k, place_ref: (k, place_ref[0], 0)) for g in grads]
            + [pl.BlockSpec(block(g), lambda k, place_ref: (k, 0, 0)) for g in grads],
            out_specs=[pl.BlockSpec(block(g), lambda k, place_ref: (k, 0, 0)) for g in grads]
            + [pl.BlockSpec(block(g), lambda k, place_ref: (place_ref[1], 0, 0)) for g in grads],
        ),
        out_shape=[jax.ShapeDtypeStruct(r.shape, dt) for r, dt in zip(received, wire)] * 2,
        compiler_params=pltpu.CompilerParams(dimension_semantics=("arbitrary",), vmem_limit_bytes=VMEM_LIMIT),
    )(place, *grads, *received)


def _scatter_exchange(pairs, landing):
    n = len(pairs)

    def copy(refs, sems, a, j, px, py):
        x, y, c, _ = _place()
        return pltpu.make_async_remote_copy(
            src_ref=refs[a].at[2 * px + py], dst_ref=refs[n + a].at[2 * x + y], send_sem=sems[0].at[3 * a + j],
            recv_sem=sems[1].at[3 * a + j], device_id=(px, py, c), device_id_type=MESH,
        )

    def before(step, steps, refs, outs, sems):
        def send():
            for j, (px, py) in enumerate(_place()[3]):
                for a in range(n):
                    copy(refs, sems, a, j, px, py).start()

        _when(step == 0, send)

    def after(step, steps, refs, outs, sems):
        def finish():
            x, y, c, others = _place()
            for j, (px, py) in enumerate(others):
                for a in range(n):
                    landed = refs[n + a].at[2 * px + py]
                    pltpu.make_async_remote_copy(
                        src_ref=landed, dst_ref=landed, send_sem=sems[0].at[3 * a + j], recv_sem=sems[1].at[3 * a + j],
                        device_id=(px, py, c), device_id_type=MESH,
                    ).wait_recv()
            for j, (px, py) in enumerate(others):
                for a in range(n):
                    copy(refs, sems, a, j, px, py).wait_send()

        _when(step == steps - 1, finish)

    return _Exchange(pairs + landing, [(_sds(b), n + a) for a, b in enumerate(landing)], [3 * n, 3 * n], before, after)


SUM_STEPS = 2


def _sum_chip_blocks(place, parts, keep_chip_axis, *, name):
    n = len(parts)

    def body(place_ref, *refs):
        for a in range(n):
            p = refs[a]
            total = (p[0].astype(F32) + p[1].astype(F32)) + (p[2].astype(F32) + p[3].astype(F32))
            if keep_chip_axis[a]:
                refs[n + a][0] = total
            else:
                refs[n + a][...] = total

    def in_spec(p):
        return pl.BlockSpec((N_CHIPS, p.shape[1] // SUM_STEPS, p.shape[2]), lambda i, place_ref: (0, i, 0))

    def out_spec(p, keep):
        rows = p.shape[1] // SUM_STEPS
        if keep:
            return pl.BlockSpec((1, rows, p.shape[2]), lambda i, place_ref: (place_ref[1], place_ref[0] * SUM_STEPS + i, 0))
        return pl.BlockSpec((rows, p.shape[2]), lambda i, place_ref: (place_ref[0] * SUM_STEPS + i, 0))

    def out_shape(p, keep):
        shape = (2 * p.shape[1], p.shape[2])
        return jax.ShapeDtypeStruct((N_CHIPS,) + shape if keep else shape, F32)

    return pl.pallas_call(
        body,
        name=name,
        grid_spec=pltpu.PrefetchScalarGridSpec(
            num_scalar_prefetch=1,
            grid=(SUM_STEPS,),
            in_specs=[in_spec(p) for p in parts],
            out_specs=[out_spec(p, k) for p, k in zip(parts, keep_chip_axis)],
        ),
        out_shape=[out_shape(p, k) for p, k in zip(parts, keep_chip_axis)],
        compiler_params=pltpu.CompilerParams(dimension_semantics=("arbitrary",), vmem_limit_bytes=VMEM_LIMIT),
    )(place, *parts)


def _join_exchange(bufs, keep_chip_axis):
    n = len(bufs)
    kept = [a for a in range(n) if keep_chip_axis[a]]
    base = n

    def copy(refs, sems, block, k, to):
        return pltpu.make_async_remote_copy(
            src_ref=block, dst_ref=block, send_sem=sems[0].at[k], recv_sem=sems[1].at[k], device_id=to, device_id_type=MESH
        )

    def mine(refs, a, cc):
        x, y, _, _ = _place()
        return _half(refs[a].at[2 * x + y] if keep_chip_axis[a] else refs[a], cc, 0)

    def before(step, steps, refs, outs, sems):
        def send():
            x, y, c, others = _place()
            for a in range(n):
                copy(refs, sems, mine(refs, a, c), a, (x, y, 1 - c)).start()
            for i, a in enumerate(kept):
                for j, (px, py) in enumerate(others):
                    copy(refs, sems, mine(refs, a, c), base + 6 * i + j, (px, py, c)).start()

        _when(step == 0, send)

    def after(step, steps, refs, outs, sems):
        def pass_on():
            x, y, c, others = _place()
            for i, a in enumerate(kept):
                for j, (px, py) in enumerate(others):
                    landed = _half(refs[a].at[2 * px + py], c, 0)
                    copy(refs, sems, landed, base + 6 * i + j, (px, py, c)).wait_recv()
                    copy(refs, sems, landed, base + 6 * i + 3 + j, (x, y, 1 - c)).start()

        def finish():
            x, y, c, others = _place()
            for a in range(n):
                copy(refs, sems, mine(refs, a, 1 - c), a, (x, y, 1 - c)).wait_recv()
            for i, a in enumerate(kept):
                for j, (px, py) in enumerate(others):
                    passed = _half(refs[a].at[2 * px + py], 1 - c, 0)
                    copy(refs, sems, passed, base + 6 * i + 3 + j, (x, y, 1 - c)).wait_recv()
            for a in range(n):
                copy(refs, sems, mine(refs, a, c), a, (x, y, 1 - c)).wait_send()
            for i, a in enumerate(kept):
                for k in range(6):
                    copy(refs, sems, mine(refs, a, c), base + 6 * i + k, (x, y, 1 - c)).wait_send()

        _when(step == steps // 2, pass_on)
        _when(step == steps - 1, finish)

    return _Exchange(bufs, [(_sds(b), a) for a, b in enumerate(bufs)], [n + 6 * len(kept)] * 2, before, after)


def _adamw(w, g, m, v):
    m = ADAM_B1 * m + (1.0 - ADAM_B1) * g
    v = ADAM_B2 * v + (1.0 - ADAM_B2) * (g * g)
    m_hat = m / (1.0 - ADAM_B1**ADAM_STEP)
    v_hat = v / (1.0 - ADAM_B2**ADAM_STEP)
    delta = -ADAM_LR * (m_hat / (jnp.sqrt(v_hat) + ADAM_EPS) + ADAM_WD * w)
    return delta, m, v


def _adamw_large(w, m, v, layer_grads, *, name):
    depth, rows, cols = w.shape
    tile = math.gcd(rows, ADAM_TILE)
    assert tile % 8 == 0

    def body(w_ref, m_ref, v_ref, *refs):
        g_refs, (g_out, d_out, m_out, v_out) = refs[:depth], refs[depth:]
        for l in range(depth):

            @pl.when(pl.program_id(0) == l)
            def _(l=l):
                g = g_refs[l][...]
                delta, m_new, v_new = _adamw(w_ref[0], g, m_ref[0], v_ref[0])
                g_out[0], d_out[0], m_out[0], v_out[0] = g, delta, m_new, v_new

    def stacked():
        return pl.BlockSpec((1, tile, cols), lambda l, i: (l, i, 0))

    def layer_spec(l):
        return pl.BlockSpec((tile, cols), lambda k, i: (jnp.where(k == l, i, 0), 0))

    shape = jax.ShapeDtypeStruct(w.shape, F32)
    return pl.pallas_call(
        body,
        name=name,
        grid=(depth, rows // tile),
        in_specs=[stacked(), stacked(), stacked()] + [layer_spec(l) for l in range(depth)],
        out_specs=[stacked()] * 4,
        out_shape=[shape] * 4,
        compiler_params=pltpu.CompilerParams(dimension_semantics=("arbitrary", "arbitrary"), vmem_limit_bytes=VMEM_LIMIT),
    )(w, m, v, *layer_grads)


def _packed_pieces(name, l):
    vecs = ("sg_ln_g", "sg_ln_b", "pool_scale", "cc_dw_b", "cc_ln_g", "cc_ln_b")
    if name in vecs:
        r = RG_VEC + vecs.index(name)
        return [((slice(l, l + 1), slice(None)), slice(r, r + 1), slice(None))]
    if name in ("ln_g", "ln_b"):
        r = RG_LN + (4 if name == "ln_b" else 0)
        return [((slice(l, l + 1), slice(j * D_G, (j + 1) * D_G)), slice(r + j, r + j + 1), slice(None)) for j in range(4)]
    assert name == "sg_w"
    return [
        ((l, h), slice(RG_SGW + CHUNK * (h // 2), RG_SGW + CHUNK * (h // 2 + 1)), slice(CHUNK * (h % 2), CHUNK * (h % 2 + 1)))
        for h in range(N_SUB)
    ]


PACKED_NAMES = ("sg_ln_g", "sg_ln_b", "pool_scale", "cc_dw_b", "cc_ln_g", "cc_ln_b", "ln_g", "ln_b", "sg_w")


def _adamw_packed(packed, ws, ms, vs, *, name):
    n, depth = len(ws), len(packed)

    def body(*refs):
        packed_refs, refs = refs[:depth], refs[depth:]
        for a, leaf in enumerate(PACKED_NAMES):
            for l in range(depth):
                for at, rows, cols in _packed_pieces(leaf, l):
                    g = packed_refs[l][rows, cols]
                    delta, m_new, v_new = _adamw(refs[a][at], g, refs[n + a][at], refs[2 * n + a][at])
                    refs[3 * n + a][at] = g
                    refs[4 * n + a][at] = delta
                    refs[5 * n + a][at] = m_new
                    refs[6 * n + a][at] = v_new

    shapes = [jax.ShapeDtypeStruct(w.shape, F32) for w in ws]
    outs = pl.pallas_call(body, name=name, out_shape=shapes * 4)(*packed, *ws, *ms, *vs)
    return outs[:n], outs[n : 2 * n], outs[2 * n : 3 * n], outs[3 * n :]


def _adamw_small(ws, gs, ms, vs, *, name):
    n = len(ws)

    def body(*refs):
        for a in range(n):
            delta, m_new, v_new = _adamw(refs[a][...], refs[n + a][...], refs[2 * n + a][...], refs[3 * n + a][...])
            refs[4 * n + a][...] = delta
            refs[5 * n + a][...] = m_new
            refs[6 * n + a][...] = v_new

    shapes = [jax.ShapeDtypeStruct(w.shape, F32) for w in ws]
    outs = pl.pallas_call(body, name=name, out_shape=shapes * 3)(*ws, *gs, *ms, *vs)
    return outs[:n], outs[n : 2 * n], outs[2 * n :]


WEIGHT_NAMES = (
    "w_in", "conv_a_w", "sg_ln_g", "sg_ln_b", "sg_w", "sg_b", "pool_w", "pool_scale", "cc_dw_w", "cc_dw_b", "cc_ln_g",
    "cc_ln_b", "cc_pw_w", "w_kv", "w_out", "ln_g", "ln_b",
)
LARGE = ("w_in", "cc_pw_w", "w_kv", "w_out")
TAPS_ROWS = 48


def _unpack_small_grads(small, chip):
    out = {}
    for r, k in enumerate(("sg_ln_g", "sg_ln_b", "pool_scale", "cc_dw_b", "cc_ln_g", "cc_ln_b")):
        out[k] = small[RG_VEC + r]
    out["conv_a_w"] = lax.dynamic_slice_in_dim(small[RG_CONVA : RG_CONVA + CONV_A], chip * HEAD_DIM, HEAD_DIM, axis=1)
    out["cc_dw_w"] = lax.dynamic_slice_in_dim(small[RG_DW : RG_DW + CONV_D], chip * HEAD_DIM, HEAD_DIM, axis=1)
    cat = jnp.concatenate([small[RG_SGW : RG_SGW + CHUNK], small[RG_SGW + CHUNK : RG_SGW + 2 * CHUNK]], axis=1)
    out["sg_w"] = jnp.transpose(cat.reshape(CHUNK, N_SUB, CHUNK), (1, 0, 2))
    out["sg_b"] = small[RG_SGB : RG_SGB + CHUNK].reshape(CHUNK, N_SUB, HEAD_DIM).sum(-1).T
    pool = small[RG_POOL : RG_POOL + D_G]
    out["pool_w"] = jnp.stack(
        [pool[g * HEAD_DIM : (g + 1) * HEAD_DIM, g * HEAD_DIM : (g + 1) * HEAD_DIM] for g in range(N_SUB)]
    )
    out["ln_g"] = small[RG_LN : RG_LN + 4].reshape(D_MODEL)
    out["ln_b"] = small[RG_LN + 4 : RG_LN + 8].reshape(D_MODEL)
    return out


def kernel(x, mem, w_in, conv_a_w, sg_ln_g, sg_ln_b, sg_w, sg_b, pool_w, pool_scale, cc_dw_w, cc_dw_b, cc_ln_g, cc_ln_b, cc_pw_w, w_kv, w_out, ln_g, ln_b, loss_target, m_w_in, m_conv_a_w, m_sg_ln_g, m_sg_ln_b, m_sg_w, m_sg_b, m_pool_w, m_pool_scale, m_cc_dw_w, m_cc_dw_b, m_cc_ln_g, m_cc_ln_b, m_cc_pw_w, m_w_kv, m_w_out, m_ln_g, m_ln_b, v_w_in, v_conv_a_w, v_sg_ln_g, v_sg_ln_b, v_sg_w, v_sg_b, v_pool_w, v_pool_scale, v_cc_dw_w, v_cc_dw_b, v_cc_ln_g, v_cc_ln_b, v_cc_pw_w, v_w_kv, v_w_out, v_ln_g, v_ln_b):
    given = dict(locals())
    weights = {k: given[k] for k in WEIGHT_NAMES}
    chip = 2 * lax.axis_index("x") + lax.axis_index("y")
    place = jnp.stack([lax.axis_index("c"), chip]).astype(jnp.int32)

    x0, mem0 = x[0], mem[0]

    taps = jnp.concatenate([conv_a_w, cc_dw_w], axis=1)
    taps = jnp.pad(taps, ((0, 0), (0, TAPS_ROWS - taps.shape[1]), (0, 0)))

    def own_blocks(l):
        return _place_own_block(
            place, [w_in, w_out, w_kv, cc_pw_w, taps], l, [BF16, BF16, BF16, BF16, F32], name=f"place_weights{l}"
        )

    def layer_operands(l, gathered):
        g_in, g_out, g_kv, g_pw, g_taps = gathered
        taps_full = jnp.transpose(g_taps, (1, 0, 2)).reshape(TAPS_ROWS, D_G)
        full = dict(
            w_in=g_in,
            w_out=g_out.reshape(D_MIX, D_MODEL),
            w_kv=g_kv.reshape(D_MODEL, 2 * D_G),
            cc_pw_w=g_pw.reshape(D_G, D_G),
            conv_a_w=taps_full[0:CONV_A],
            cc_dw_w=taps_full[CONV_A : CONV_A + CONV_D],
            **{k: weights[k][l] for k in WEIGHT_NAMES if k not in LARGE + ("conv_a_w", "cc_dw_w")},
        )
        return _prepare_layer(mem0, full, l)

    def other_grads(l, bwd, small):
        _, _, dwout, dkst, dvst, dpw, _ = bwd
        return [
            dwout.reshape(N_CHIPS, D_MIX // N_CHIPS, D_MODEL),
            _kv_backward(mem0, dkst, dvst, name=f"kv_bwd{l}").reshape(N_CHIPS, D_MODEL // N_CHIPS, 2 * D_G),
            dpw.reshape(N_CHIPS, D_G // N_CHIPS, D_G),
            small.reshape(N_CHIPS, RG_ROWS // N_CHIPS, D_G),
        ]

    n_red = 5
    keep = [False, False, False, False, True]

    wire = [BF16, BF16, BF16, BF16, F32]

    def reduced_layer(joined):
        r_in, r_out, r_kv, r_pw, small_all = joined
        packed = small_all.reshape(RG_ROWS, D_G)
        out = _unpack_small_grads(packed, chip)
        out.update(w_in=r_in, w_out=r_out, w_kv=r_kv, cc_pw_w=r_pw, packed=packed)
        return out

    blocks0, blocks1 = own_blocks(0), own_blocks(1)
    p0 = layer_operands(0, _run_exchange(_gather_exchange(blocks0), name="gather_weights0"))
    fwd0, gathered1 = _forward(0, x0, p0, None, exchange=_gather_exchange(blocks1))
    p1 = layer_operands(1, gathered1)
    x1 = fwd0[3]
    fwd1, _ = _forward(1, x1, p1, loss_target[0])

    bwd1, _ = _backward(1, fwd1[3], fwd1, p1)
    small1 = bwd1[6].at[RG_LOSS, :].set(fwd1[5][0, 0])
    (dwin1,), _ = _input_weight_grad(x1, bwd1[0], name="w_in_grad1")
    grads1 = [dwin1] + other_grads(1, bwd1, small1)
    (dx1,), received1 = _input_grad(
        bwd1[0], bwd1[1], p1["win"], name="input_grad1", exchange=_swap_exchange(grads1).starting_late()
    )
    pairs1 = _add_sibling_half(place, grads1, received1, wire, name="rs_pair1")
    bwd0, parts1 = _backward(
        0, dx1, fwd0, p0, exchange=_scatter_exchange(pairs1[:n_red], pairs1[n_red:]).starting_late()
    )
    halves1 = _sum_chip_blocks(place, parts1, keep, name="rs_sum1")
    rest0 = other_grads(0, bwd0, bwd0[6])
    n_rest = len(rest0)
    pairs0_rest = _add_sibling_half(
        place, rest0, _run_exchange(_swap_exchange(rest0), name="rs_swap0_rest"), wire[1:], name="rs_pair0_rest"
    )
    (dwin0,), carried = _input_weight_grad(
        x0, bwd0[0], name="w_in_grad0",
        exchange=_both(_join_exchange(halves1, keep), _scatter_exchange(pairs0_rest[:n_rest], pairs0_rest[n_rest:])),
    )
    joined1, parts0_rest = carried[:n_red], carried[n_red:]
    loss = joined1[4].reshape(RG_ROWS, D_G)[RG_LOSS, 0]
    pairs0_win = _add_sibling_half(
        place, [dwin0], _run_exchange(_swap_exchange([dwin0]), name="rs_swap0_win"), wire[:1], name="rs_pair0_win"
    )
    (grad_x,), parts0_win = _input_grad(
        bwd0[0], bwd0[1], p0["win"], name="input_grad0", exchange=_scatter_exchange(pairs0_win[:1], pairs0_win[1:])
    )
    halves0 = _sum_chip_blocks(place, parts0_win + parts0_rest, keep, name="rs_sum0")
    reduced = [reduced_layer(_run_exchange(_join_exchange(halves0, keep), name="rs_join0")), reduced_layer(joined1)]

    grad, delta, new_m, new_v = {}, {}, {}, {}
    for k in LARGE:
        w3 = weights[k]
        grad[k], delta[k], new_m[k], new_v[k] = _adamw_large(
            w3, given["m_" + k], given["v_" + k], [reduced[l][k] for l in range(DEPTH)], name=f"adamw_{k}"
        )
    g_p, d_p, m_p, v_p = _adamw_packed(
        [reduced[l]["packed"] for l in range(DEPTH)],
        [weights[k] for k in PACKED_NAMES],
        [given["m_" + k] for k in PACKED_NAMES],
        [given["v_" + k] for k in PACKED_NAMES],
        name="adamw_packed",
    )
    for a, k in enumerate(PACKED_NAMES):
        grad[k], delta[k], new_m[k], new_v[k] = g_p[a], d_p[a], m_p[a], v_p[a]
    small_names = [k for k in WEIGHT_NAMES if k not in LARGE + PACKED_NAMES]
    for k in small_names:
        grad[k] = jnp.stack([reduced[l][k] for l in range(DEPTH)])
    d_s, m_s, v_s = _adamw_small(
        [weights[k] for k in small_names],
        [grad[k] for k in small_names],
        [given["m_" + k] for k in small_names],
        [given["v_" + k] for k in small_names],
        name="adamw_small",
    )
    for a, k in enumerate(small_names):
        delta[k], new_m[k], new_v[k] = d_s[a], m_s[a], v_s[a]

    return (
        loss,
        grad_x[None],
        *[grad[k] for k in WEIGHT_NAMES],
        *[delta[k] for k in WEIGHT_NAMES],
        *[new_m[k] for k in WEIGHT_NAMES],
        *[new_v[k] for k in WEIGHT_NAMES],
    )
```

```python
import math

import jax
import jax.numpy as jnp
from jax import lax
from jax.experimental import pallas as pl
from jax.experimental.pallas import tpu as pltpu

F32 = jnp.float32
BF16 = jnp.bfloat16

D_MODEL = 1024
DEPTH = 2
D_G = 256
D_MIX = 5 * D_G
D_IN = 9 * D_G + D_MIX
N_SUB = 4
HEAD_DIM = 64
CONV_A = 3
CONV_D = 31
CHUNK = 128
MEM_LEN = 256
N_CHIPS = 4
W_IN_SHARD = D_IN // N_CHIPS
LN_EPS = 1e-5
ALPHA = (2.0 * DEPTH) ** 0.25
ATT_SCALE = 1.0 / math.sqrt(HEAD_DIM)
GELU_C = math.sqrt(2.0 / math.pi)
GELU_A = 0.044715

ADAM_LR = 0.001
ADAM_B1 = 0.9
ADAM_B2 = 0.999
ADAM_EPS = 1e-08
ADAM_WD = 0.01
ADAM_STEP = 10

C_XA, C_BA, C_CA, C_U, C_V, C_XC, C_DA, C_DG, C_Q, C_GATE = (D_G * i for i in range(10))

HALO_A = 8
HALO_C = 16
HALO_D = 32

RW_VEC = 0
RW_CONVA = 16
RW_DW = 24
RW_SGB = 56

RG_VEC = 0
RG_CONVA = 16
RG_DW = 24
RG_SGW = 56
RG_SGB = RG_SGW + 2 * CHUNK
RG_POOL = RG_SGB + CHUNK
RG_LN = RG_POOL + D_G
RG_LOSS = 8
RG_ROWS = 768

VMEM_LIMIT = 62 * 1024 * 1024

AUX_CVD = 0
AUX_PM = D_G
AUX_P = 2 * D_G
AUX_COLS = AUX_P + N_SUB * MEM_LEN
SEQ_TILE = 256
FWD_TILE = 512
MM_TILE = 1024
W_GRAD_TILE = 4096
ADAM_TILE = 512

MESH = pl.DeviceIdType.MESH
ANY = pl.BlockSpec(memory_space=pl.ANY)
NT = (((1,), (1,)), ((), ()))
TN = (((0,), (0,)), ((), ()))


def _dot(a, b):
    return jnp.dot(a, b, preferred_element_type=F32)


def _dot_nt(a, b):
    return lax.dot_general(a, b, NT, preferred_element_type=F32)


def _dot_tn(a, b):
    return lax.dot_general(a, b, TN, preferred_element_type=F32)


def _full(shape):
    zeros = (0,) * len(shape)
    return pl.BlockSpec(shape, lambda *_: zeros)


class _Exchange:
    def __init__(self, operands, outputs, sem_counts, before, after):
        self.operands, self.outputs, self.sem_counts, self.before, self.after = operands, outputs, sem_counts, before, after
        self.late_start = False

    def starting_late(self):
        self.late_start = True
        return self

    def specs(self, first_input, first_output):
        aliases = {first_input + src: first_output + j for j, (_, src) in enumerate(self.outputs) if src is not None}
        return (
            [ANY] * len(self.operands),
            [ANY] * len(self.outputs),
            [sds for sds, _ in self.outputs],
            [pltpu.SemaphoreType.DMA((k,)) for k in self.sem_counts],
            aliases,
        )

    def split(self, ins, outs):
        refs = list(ins)
        for j, (_, src) in enumerate(self.outputs):
            if src is not None:
                refs[src] = outs[j]
        return refs


def _both(first, second):
    n1, m1, s1 = len(first.operands), len(first.outputs), len(first.sem_counts)
    outputs = first.outputs + [(sds, None if src is None else n1 + src) for sds, src in second.outputs]

    def before(step, steps, refs, outs, sems):
        first.before(step, steps, refs[:n1], outs[:m1], sems[:s1])
        second.before(step, steps, refs[n1:], outs[m1:], sems[s1:])

    def after(step, steps, refs, outs, sems):
        first.after(step, steps, refs[:n1], outs[:m1], sems[:s1])
        second.after(step, steps, refs[n1:], outs[m1:], sems[s1:])

    return _Exchange(first.operands + second.operands, outputs, first.sem_counts + second.sem_counts, before, after)


def _when(cond, fn):
    if isinstance(cond, bool):
        if cond:
            fn()
    else:
        pl.when(cond)(fn)


def _run_exchange(exchange, *, name):
    n_in, n_out = len(exchange.operands), len(exchange.outputs)
    in_specs, out_specs, out_shape, sems, aliases = exchange.specs(0, 0)

    def body(*refs):
        ins, outs, sem_refs = refs[:n_in], refs[n_in : n_in + n_out], refs[n_in + n_out :]
        refs = exchange.split(ins, outs)
        exchange.before(0, 1, refs, outs, sem_refs)
        exchange.after(0, 1, refs, outs, sem_refs)

    return pl.pallas_call(
        body, name=name, in_specs=in_specs, out_specs=out_specs, out_shape=out_shape, scratch_shapes=sems,
        input_output_aliases=aliases,
    )(*exchange.operands)


def _gridded_call(body, *, name, steps, in_specs, out_specs, out_shape, scratch_shapes, operands, exchange=None):
    params = pltpu.CompilerParams(dimension_semantics=("arbitrary",), vmem_limit_bytes=VMEM_LIMIT)
    if exchange is None:
        outs = pl.pallas_call(
            body, name=name, grid=(steps,), in_specs=in_specs, out_specs=out_specs, out_shape=out_shape,
            scratch_shapes=scratch_shapes, compiler_params=params,
        )(*operands)
        return list(outs), []
    n_in, n_out, n_scr = len(in_specs), len(out_specs), len(scratch_shapes)
    x_in, x_out = len(exchange.operands), len(exchange.outputs)
    ex_in_specs, ex_out_specs, ex_out_shape, ex_sems, aliases = exchange.specs(n_in, n_out)

    def full(*refs):
        own_in, refs = refs[:n_in], refs[n_in:]
        ex_in, refs = refs[:x_in], refs[x_in:]
        own_out, refs = refs[:n_out], refs[n_out:]
        ex_out, refs = refs[:x_out], refs[x_out:]
        own_scr, sem_refs = refs[:n_scr], refs[n_scr:]
        ex_refs = exchange.split(ex_in, ex_out)
        step = pl.program_id(0)
        if not exchange.late_start:
            exchange.before(step, steps, ex_refs, ex_out, sem_refs)
        body(*own_in, *own_out, *own_scr)
        if exchange.late_start:
            exchange.before(step, steps, ex_refs, ex_out, sem_refs)
        exchange.after(step, steps, ex_refs, ex_out, sem_refs)

    outs = pl.pallas_call(
        full, name=name, grid=(steps,), in_specs=in_specs + ex_in_specs, out_specs=out_specs + ex_out_specs,
        out_shape=out_shape + ex_out_shape, scratch_shapes=scratch_shapes + ex_sems, input_output_aliases=aliases,
        compiler_params=params,
    )(*operands, *exchange.operands)
    return list(outs[:n_out]), list(outs[n_out:])


def _sigmoid(x):
    return 0.5 * jnp.tanh(0.5 * x) + 0.5


def _gelu(x):
    t = jnp.tanh(GELU_C * (x + GELU_A * x * x * x))
    return 0.5 * x * (1.0 + t), t


def _gelu_grad(x, t):
    return 0.5 * (1.0 + t) + 0.5 * x * (1.0 - t * t) * (GELU_C * (1.0 + 3.0 * GELU_A * x * x))


def _normalize(v):
    mu = jnp.mean(v, axis=-1, keepdims=True)
    d = v - mu
    var = jnp.mean(d * d, axis=-1, keepdims=True)
    rstd = lax.rsqrt(var + LN_EPS)
    return d * rstd, rstd


def _normalize_grad(dhat, hat, rstd):
    m1 = jnp.mean(dhat, axis=-1, keepdims=True)
    m2 = jnp.mean(dhat * hat, axis=-1, keepdims=True)
    return rstd * (dhat - m1 - hat * m2)


def _lane(width=D_G):
    return lax.broadcasted_iota(jnp.int32, (1, width), 1)


def _head_masks():
    head = _lane() // HEAD_DIM
    return [(head == h).astype(F32) for h in range(N_SUB)]


def _stack_heads(v, masks):
    return jnp.concatenate([v * m for m in masks], axis=0)


def _tril_mask_cat():
    t = lax.broadcasted_iota(jnp.int32, (CHUNK, N_SUB * CHUNK), 0)
    s = lax.broadcasted_iota(jnp.int32, (CHUNK, N_SUB * CHUNK), 1) % CHUNK
    return s <= t


def _triu_mask_cat():
    s = lax.broadcasted_iota(jnp.int32, (CHUNK, N_SUB * CHUNK), 0)
    t = lax.broadcasted_iota(jnp.int32, (CHUNK, N_SUB * CHUNK), 1) % CHUNK
    return t >= s


def _pool_select(a2, a4, a8, a16):
    lane = _lane()
    return jnp.where(lane < 64, a2, jnp.where(lane < 128, a4, jnp.where(lane < 192, a8, a16)))


def _pool_inv_count(row0, rows):
    t = row0 + lax.broadcasted_iota(jnp.int32, (HALO_C, D_G), 0)
    lane = lax.broadcasted_iota(jnp.int32, (HALO_C, D_G), 1)
    win = jnp.where(lane < 64, 2, jnp.where(lane < 128, 4, jnp.where(lane < 192, 8, 16)))
    head = 1.0 / jnp.minimum(t + 1, win).astype(F32)
    inv_win = jnp.broadcast_to(_pool_select(0.5, 0.25, 0.125, 0.0625), (rows - HALO_C, D_G))
    return jnp.concatenate([head, inv_win], axis=0)


def _trailing_window_sum(halo, cur):
    e = jnp.concatenate([halo, cur], axis=0)
    s2 = e + pltpu.roll(e, 1, 0)
    s4 = s2 + pltpu.roll(s2, 2, 0)
    s8 = s4 + pltpu.roll(s4, 4, 0)
    s16 = s8 + pltpu.roll(s8, 8, 0)
    return _pool_select(s2, s4, s8, s16)[HALO_C:]


def _leading_window_sum(cur, halo):
    e = jnp.concatenate([cur, halo], axis=0)
    n = e.shape[0]
    s2 = e + pltpu.roll(e, n - 1, 0)
    s4 = s2 + pltpu.roll(s2, n - 2, 0)
    s8 = s4 + pltpu.roll(s4, n - 4, 0)
    s16 = s8 + pltpu.roll(s8, n - 8, 0)
    return _pool_select(s2, s4, s8, s16)[: cur.shape[0]]


def _softmax_blocks(sc):
    out = []
    for h in range(N_SUB):
        s = sc[:, h * MEM_LEN : (h + 1) * MEM_LEN]
        e = jnp.exp(s - jnp.max(s, axis=-1, keepdims=True))
        out.append(e * (1.0 / jnp.sum(e, axis=-1, keepdims=True)))
    return jnp.concatenate(out, axis=-1)


STRIP = 32
SHIFTS = 8


def _fill_shifts(buf):
    n = buf.shape[1] - SHIFTS
    for r in range(1, SHIFTS):
        buf[r, 0:n, :] = buf[0, r : r + n, :]


def _shifted(buf, off, rows):
    r = off % SHIFTS
    return buf[r, off - r : off - r + rows, :]


def _sgu_mix(vn, wcat_b, sgb, masks):
    vbd = _stack_heads(vn, masks).astype(BF16)
    return _dot(wcat_b, vbd) + sgb, vbd


def _layer_forward(x, win, kst, vst, wout, sw, wcat, poolw, pww, ln, tgt, *, name, exchange=None):
    seq = x.shape[0]
    tile = min(FWD_TILE, seq)
    n_tiles = seq // tile
    last = tgt is not None

    def body(*refs):
        x_ref, win_ref, kst_ref, vst_ref, wout_ref, sw_ref, wcat_ref, pool_ref, pw_ref, ln_ref = refs[:10]
        refs = refs[10:]
        if last:
            tgt_ref, refs = refs[0], refs[1:]
        proj_ref, y_ref, z_ref, out_ref, aux_ref = refs[:5]
        refs = refs[5:]
        if last:
            loss_ref, refs = refs[0], refs[1:]
        pbuf, xchalo, gbuf = refs
        i = pl.program_id(0)

        @pl.when(i == 0)
        def _():
            pbuf[0:HALO_A, :] = jnp.zeros((HALO_A, D_G), F32)
            xchalo[...] = jnp.zeros((HALO_C, D_G), F32)
            gbuf[0, 0:HALO_D, :] = jnp.zeros((HALO_D, D_G), F32)
            if last:
                loss_ref[...] = jnp.zeros((8, 128), F32)

        xt = x_ref[...]
        xb = xt.astype(BF16)

        blocks = {}

        def project(k):
            blocks[k] = _dot(xb, win_ref[k])
            proj_ref[:, k * W_IN_SHARD : (k + 1) * W_IN_SHARD] = blocks[k]

        def cols(start, width=D_G):
            parts, c = [], start
            while c < start + width:
                k, lo = divmod(c, W_IN_SHARD)
                hi = min(W_IN_SHARD, lo + start + width - c)
                parts.append(blocks[k][:, lo:hi])
                c += hi - lo
            return parts[0] if len(parts) == 1 else jnp.concatenate(parts, axis=1)

        project(0)
        project(1)
        masks = _head_masks()

        pbuf[HALO_A : HALO_A + tile, :] = cols(C_CA) * cols(C_XA)
        cv = jnp.zeros((tile, D_G), F32)
        for k in range(CONV_A):
            off = HALO_A - (CONV_A - 1) + k
            cv = cv + sw_ref[RW_CONVA + k : RW_CONVA + k + 1, :] * pbuf[off : off + tile, :]
        y_ref[:, 0:D_G] = cols(C_BA) * cv
        pbuf[0:HALO_A, :] = pbuf[tile : tile + HALO_A, :]

        project(2)

        ua, _ = _gelu(cols(C_U))
        vg, _ = _gelu(cols(C_V))
        vhat, _ = _normalize(vg)
        vn = vhat * sw_ref[RW_VEC : RW_VEC + 1, :] + sw_ref[RW_VEC + 1 : RW_VEC + 2, :]
        wcat_b = jnp.where(_tril_mask_cat(), wcat_ref[...], 0.0).astype(BF16)
        sgb = sw_ref[RW_SGB : RW_SGB + CHUNK, :]
        for j in range(tile // CHUNK):
            rows = slice(j * CHUNK, (j + 1) * CHUNK)
            mixed, _ = _sgu_mix(vn[rows], wcat_b, sgb, masks)
            y_ref[rows, D_G : 2 * D_G] = ua[rows] * mixed

        xc = cols(C_XC)
        wsum = _trailing_window_sum(xchalo[...], xc)
        pm = wsum * _pool_inv_count(i * tile, tile) - xc
        aux_ref[:, AUX_PM : AUX_PM + D_G] = pm
        y_ref[:, 2 * D_G : 3 * D_G] = _dot(pm.astype(BF16), pool_ref[...]) * sw_ref[RW_VEC + 2 : RW_VEC + 3, :]
        xchalo[...] = xc[tile - HALO_C :, :]

        project(3)

        gbuf[0, HALO_D : HALO_D + tile, :] = cols(C_DA) * _sigmoid(cols(C_DG))
        _fill_shifts(gbuf)
        for r0 in range(0, tile, STRIP):
            acc = jnp.zeros((STRIP, D_G), F32) + sw_ref[RW_VEC + 3 : RW_VEC + 4, :]
            for k in range(CONV_D):
                off = HALO_D - (CONV_D - 1) + k
                acc = acc + sw_ref[RW_DW + k : RW_DW + k + 1, :] * _shifted(gbuf, off + r0, STRIP)
            aux_ref[r0 : r0 + STRIP, AUX_CVD : AUX_CVD + D_G] = acc
        nhat, _ = _normalize(aux_ref[:, AUX_CVD : AUX_CVD + D_G])
        nrm = nhat * sw_ref[RW_VEC + 4 : RW_VEC + 5, :] + sw_ref[RW_VEC + 5 : RW_VEC + 6, :]
        y_ref[:, 3 * D_G : 4 * D_G] = _dot((nrm * _sigmoid(nrm)).astype(BF16), pw_ref[...])
        gbuf[0, 0:HALO_D, :] = gbuf[0, tile : tile + HALO_D, :]

        qb = cols(C_Q).astype(BF16)
        p_all = _softmax_blocks(_dot_nt(qb, kst_ref[...]) * ATT_SCALE)
        aux_ref[:, AUX_P:] = p_all
        y_ref[:, 4 * D_G : 5 * D_G] = _dot(p_all.astype(BF16), vst_ref[...])

        gate = cols(C_GATE, D_MIX)
        hid = y_ref[...] * (gate * _sigmoid(gate))
        z = ALPHA * xt + _dot(hid.astype(BF16), wout_ref[...])
        z_ref[...] = z
        zhat, _ = _normalize(z)
        xn = zhat * ln_ref[0:1, :] + ln_ref[1:2, :]
        if last:
            err = xn - tgt_ref[...]
            out_ref[...] = err * (1.0 / D_MODEL)
            loss_ref[...] += jnp.sum(err * err) * (0.5 / D_MODEL)
        else:
            out_ref[...] = xn

    def rows(width):
        return pl.BlockSpec((tile, width), lambda i: (i, 0))

    operands = [x, win, kst, vst, wout, sw, wcat, poolw, pww, ln]
    in_specs = [rows(D_MODEL)] + [_full(a.shape) for a in operands[1:]]
    widths = [D_IN, D_MIX, D_MODEL, D_MODEL, AUX_COLS]
    out_shape = [jax.ShapeDtypeStruct((seq, w), F32) for w in widths]
    out_specs = [rows(w) for w in widths]
    if last:
        operands.append(tgt)
        in_specs.append(rows(D_MODEL))
        out_shape.append(jax.ShapeDtypeStruct((8, 128), F32))
        out_specs.append(_full((8, 128)))
    return _gridded_call(
        body,
        name=name,
        steps=n_tiles,
        in_specs=in_specs,
        out_specs=out_specs,
        out_shape=out_shape,
        scratch_shapes=[
            pltpu.VMEM((HALO_A + tile, D_G), F32),
            pltpu.VMEM((HALO_C, D_G), F32),
            pltpu.VMEM((SHIFTS, HALO_D + tile, D_G), F32),
        ],
        operands=operands,
        exchange=exchange,
    )


def _layer_backward(dxn, z, proj, y, cvd, kst, vst, wout, sw, wcat, wcat_t, poolw, pww, ln, *, name, exchange=None):
    seq = dxn.shape[0]
    tile = min(SEQ_TILE, seq)
    n_tiles = seq // tile
    halo_blocks = tile // HALO_D

    def body(
        dxn_ref, z_ref, proj_ref, halo_ref, y_ref, aux_ref, kst_ref, vst_ref, wout_ref, sw_ref, wcat_ref, wcat_t_ref,
        pool_ref, pw_ref, ln_ref, dproj_ref, dz_ref, dwout_ref, dkst_ref, dvst_ref, dpw_ref, sg_ref,
        pbuf, dcvbuf, rhalo, gbuf, dgbuf, dwacc,
    ):
        i = pl.program_id(0)
        ti = n_tiles - 1 - i

        @pl.when(i == 0)
        def _():
            dwout_ref[...] = jnp.zeros(dwout_ref.shape, F32)
            dkst_ref[...] = jnp.zeros(dkst_ref.shape, F32)
            dvst_ref[...] = jnp.zeros(dvst_ref.shape, F32)
            dpw_ref[...] = jnp.zeros(dpw_ref.shape, F32)
            sg_ref[...] = jnp.zeros(sg_ref.shape, F32)
            dwacc[...] = jnp.zeros(dwacc.shape, F32)
            dcvbuf[tile : tile + HALO_A, :] = jnp.zeros((HALO_A, D_G), F32)
            rhalo[...] = jnp.zeros((HALO_C, D_G), F32)
            dgbuf[0, tile : tile + HALO_D, :] = jnp.zeros((HALO_D, D_G), F32)

        def acc_row(row, val):
            sg_ref[row : row + 1, :] += jnp.sum(val, axis=0, keepdims=True)

        masks = _head_masks()
        has_past = (ti > 0).astype(F32)

        dlg = jnp.zeros((1, D_MODEL), F32)
        dlb = jnp.zeros((1, D_MODEL), F32)
        for r0 in range(0, tile, 2 * STRIP):
            chunk = slice(r0, r0 + 2 * STRIP)
            zhat, zrstd = _normalize(z_ref[chunk, :])
            dxn_c = dxn_ref[chunk, :]
            dlg = dlg + jnp.sum(dxn_c * zhat, axis=0, keepdims=True)
            dlb = dlb + jnp.sum(dxn_c, axis=0, keepdims=True)
            dz_ref[chunk, :] = _normalize_grad(dxn_c * ln_ref[0:1, :], zhat, zrstd)
        for j in range(D_MODEL // D_G):
            sg_ref[RG_LN + j : RG_LN + j + 1, :] += dlg[:, j * D_G : (j + 1) * D_G]
            sg_ref[RG_LN + 4 + j : RG_LN + 5 + j, :] += dlb[:, j * D_G : (j + 1) * D_G]
        dzb = dz_ref[...].astype(BF16)

        gate = proj_ref[:, C_GATE:]
        sgm = _sigmoid(gate)
        silu = gate * sgm
        yc = y_ref[...]
        dwout_ref[...] += _dot_tn((yc * silu).astype(BF16), dzb)
        dh = _dot_nt(dzb, wout_ref[...])
        dproj_ref[:, C_GATE:] = (dh * yc * (sgm * (1.0 + gate * (1.0 - sgm)))).astype(BF16)
        dy = dh * silu

        dyeb = dy[:, 4 * D_G : 5 * D_G].astype(BF16)
        qb = proj_ref[:, C_Q : C_Q + D_G].astype(BF16)
        kst_b = kst_ref[...]
        p_all = aux_ref[:, AUX_P:]
        dvst_ref[...] += _dot_tn(p_all.astype(BF16), dyeb)
        dp_all = _dot_nt(dyeb, vst_ref[...])
        ds = []
        for h in range(N_SUB):
            blk = slice(h * MEM_LEN, (h + 1) * MEM_LEN)
            p, dpb = p_all[:, blk], dp_all[:, blk]
            ds.append(p * (dpb - jnp.sum(dpb * p, axis=-1, keepdims=True)))
        dsb = (jnp.concatenate(ds, axis=-1) * ATT_SCALE).astype(BF16)
        dproj_ref[:, C_Q : C_Q + D_G] = _dot(dsb, kst_b).astype(BF16)
        dkst_ref[...] += _dot_tn(dsb, qb)

        dya = dy[:, 0:D_G]
        xa = proj_ref[:, C_XA : C_XA + D_G]
        ba = proj_ref[:, C_BA : C_BA + D_G]
        ca = proj_ref[:, C_CA : C_CA + D_G]
        past = slice(HALO_D - HALO_A, HALO_D)
        pbuf[0:HALO_A, :] = halo_ref[past, C_CA : C_CA + D_G] * halo_ref[past, C_XA : C_XA + D_G] * has_past
        pbuf[HALO_A : HALO_A + tile, :] = ca * xa
        cv = jnp.zeros((tile, D_G), F32)
        for k in range(CONV_A):
            off = HALO_A - (CONV_A - 1) + k
            cv = cv + sw_ref[RW_CONVA + k : RW_CONVA + k + 1, :] * pbuf[off : off + tile, :]
        dproj_ref[:, C_BA : C_BA + D_G] = (dya * cv).astype(BF16)
        dcv = dya * ba
        dcvbuf[0:tile, :] = dcv
        dp = jnp.zeros((tile, D_G), F32)
        for k in range(CONV_A):
            off = HALO_A - (CONV_A - 1) + k
            acc_row(RG_CONVA + k, dcv * pbuf[off : off + tile, :])
            back = CONV_A - 1 - k
            dp = dp + sw_ref[RW_CONVA + k : RW_CONVA + k + 1, :] * dcvbuf[back : back + tile, :]
        dproj_ref[:, C_CA : C_CA + D_G] = (dp * xa).astype(BF16)
        dproj_ref[:, C_XA : C_XA + D_G] = (dp * ca).astype(BF16)
        dcvbuf[tile : tile + HALO_A, :] = dcvbuf[0:HALO_A, :]

        dyb = dy[:, D_G : 2 * D_G]
        u = proj_ref[:, C_U : C_U + D_G]
        v = proj_ref[:, C_V : C_V + D_G]
        ua, ut = _gelu(u)
        vg, vt = _gelu(v)
        vhat, vrstd = _normalize(vg)
        sg_g = sw_ref[RW_VEC : RW_VEC + 1, :]
        vn = vhat * sg_g + sw_ref[RW_VEC + 1 : RW_VEC + 2, :]
        tril = _tril_mask_cat()
        wcat_b = jnp.where(tril, wcat_ref[...], 0.0).astype(BF16)
        wcat_tb = jnp.where(_triu_mask_cat(), wcat_t_ref[...], 0.0).astype(BF16)
        sgb = sw_ref[RW_SGB : RW_SGB + CHUNK, :]
        dmixed = dyb * ua
        dvn_parts = []
        du_parts = []
        dwcat = jnp.zeros((CHUNK, N_SUB * CHUNK), F32)
        dsgb = jnp.zeros((CHUNK, D_G), F32)
        for j in range(tile // CHUNK):
            rows = slice(j * CHUNK, (j + 1) * CHUNK)
            mixed, vbd = _sgu_mix(vn[rows], wcat_b, sgb, masks)
            du_parts.append(dyb[rows] * mixed)
            dmx = dmixed[rows]
            dsgb = dsgb + dmx
            dwcat = dwcat + _dot_nt(dmx.astype(BF16), vbd)
            dvn_parts.append(_dot(wcat_tb, _stack_heads(dmx, masks).astype(BF16)))
        dwcat = jnp.where(tril, dwcat, 0.0)
        sg_ref[RG_SGW : RG_SGW + CHUNK, :] += dwcat[:, 0:D_G]
        sg_ref[RG_SGW + CHUNK : RG_SGW + 2 * CHUNK, :] += dwcat[:, D_G:]
        sg_ref[RG_SGB : RG_SGB + CHUNK, :] += dsgb
        dvn = jnp.concatenate(dvn_parts, axis=0)
        du_act = jnp.concatenate(du_parts, axis=0)
        acc_row(RG_VEC, dvn * vhat)
        acc_row(RG_VEC + 1, dvn)
        dvg = _normalize_grad(dvn * sg_g, vhat, vrstd)
        dproj_ref[:, C_U : C_U + D_G] = (du_act * _gelu_grad(u, ut)).astype(BF16)
        dproj_ref[:, C_V : C_V + D_G] = (dvg * _gelu_grad(v, vt)).astype(BF16)

        dyc = dy[:, 2 * D_G : 3 * D_G]
        inv_cnt = _pool_inv_count(ti * tile, tile)
        pmb = aux_ref[:, AUX_PM : AUX_PM + D_G].astype(BF16)
        pool_b = pool_ref[...]
        scale = sw_ref[RW_VEC + 2 : RW_VEC + 3, :]
        acc_row(RG_VEC + 2, dyc * _dot(pmb, pool_b))
        dpre = (dyc * scale).astype(BF16)
        sg_ref[RG_POOL : RG_POOL + D_G, :] += _dot_tn(pmb, dpre)
        dpm = _dot_nt(dpre, pool_b)
        r = dpm * inv_cnt
        dproj_ref[:, C_XC : C_XC + D_G] = (_leading_window_sum(r, rhalo[...]) - dpm).astype(BF16)
        rhalo[...] = r[0:HALO_C, :]

        dyd = dy[:, 3 * D_G : 4 * D_G]
        da = proj_ref[:, C_DA : C_DA + D_G]
        sgd = _sigmoid(proj_ref[:, C_DG : C_DG + D_G])
        gbuf[0, 0:HALO_D, :] = halo_ref[:, C_DA : C_DA + D_G] * _sigmoid(halo_ref[:, C_DG : C_DG + D_G]) * has_past
        gbuf[0, HALO_D : HALO_D + tile, :] = da * sgd
        _fill_shifts(gbuf)
        nhat, nrstd = _normalize(aux_ref[:, AUX_CVD : AUX_CVD + D_G])
        cc_g = sw_ref[RW_VEC + 4 : RW_VEC + 5, :]
        nrm = nhat * cc_g + sw_ref[RW_VEC + 5 : RW_VEC + 6, :]
        sgn = _sigmoid(nrm)
        dydb = dyd.astype(BF16)
        dpw_ref[...] += _dot_tn((nrm * sgn).astype(BF16), dydb)
        dn = _dot_nt(dydb, pw_ref[...]) * (sgn * (1.0 + nrm * (1.0 - sgn)))
        acc_row(RG_VEC + 4, dn * nhat)
        acc_row(RG_VEC + 5, dn)
        dcvd = _normalize_grad(dn * cc_g, nhat, nrstd)
        acc_row(RG_VEC + 3, dcvd)
        dgbuf[0, 0:tile, :] = dcvd
        _fill_shifts(dgbuf)
        for r0 in range(0, tile, STRIP):
            d_s = dgbuf[0, r0 : r0 + STRIP, :]
            dg = jnp.zeros((STRIP, D_G), F32)
            for k in range(CONV_D):
                off = HALO_D - (CONV_D - 1) + k
                prod = d_s * _shifted(gbuf, off + r0, STRIP)
                part = prod[0:8]
                for q in range(8, STRIP, 8):
                    part = part + prod[q : q + 8]
                dwacc[8 * k : 8 * k + 8, :] += part
                back = CONV_D - 1 - k
                dg = dg + sw_ref[RW_DW + k : RW_DW + k + 1, :] * _shifted(dgbuf, back + r0, STRIP)
            da_s = proj_ref[r0 : r0 + STRIP, C_DA : C_DA + D_G]
            sgd_s = _sigmoid(proj_ref[r0 : r0 + STRIP, C_DG : C_DG + D_G])
            dproj_ref[r0 : r0 + STRIP, C_DA : C_DA + D_G] = (dg * sgd_s).astype(BF16)
            dproj_ref[r0 : r0 + STRIP, C_DG : C_DG + D_G] = (dg * da_s * sgd_s * (1.0 - sgd_s)).astype(BF16)
        dgbuf[0, tile : tile + HALO_D, :] = dgbuf[0, 0:HALO_D, :]

        @pl.when(i == n_tiles - 1)
        def _():
            for k in range(CONV_D):
                sg_ref[RG_DW + k : RG_DW + k + 1, :] = jnp.sum(dwacc[8 * k : 8 * k + 8, :], axis=0, keepdims=True)

    def rows(width):
        return pl.BlockSpec((tile, width), lambda i: (n_tiles - 1 - i, 0))

    halo_spec = pl.BlockSpec((HALO_D, D_IN), lambda i: (jnp.maximum((n_tiles - 1 - i) * halo_blocks - 1, 0), 0))
    weights = [kst, vst, wout, sw, wcat, wcat_t, poolw, pww, ln]
    acc_shapes = [(D_MIX, D_MODEL), (N_SUB * MEM_LEN, D_G), (N_SUB * MEM_LEN, D_G), (D_G, D_G), (RG_ROWS, D_G)]
    return _gridded_call(
        body,
        name=name,
        steps=n_tiles,
        in_specs=[rows(D_MODEL), rows(D_MODEL), rows(D_IN), halo_spec, rows(D_MIX), rows(AUX_COLS)]
        + [_full(a.shape) for a in weights],
        out_specs=[rows(D_IN), rows(D_MODEL)] + [_full(s) for s in acc_shapes],
        out_shape=[jax.ShapeDtypeStruct((seq, D_IN), BF16), jax.ShapeDtypeStruct((seq, D_MODEL), F32)]
        + [jax.ShapeDtypeStruct(s, F32) for s in acc_shapes],
        scratch_shapes=[
            pltpu.VMEM((HALO_A + tile, D_G), F32),
            pltpu.VMEM((tile + HALO_A, D_G), F32),
            pltpu.VMEM((HALO_C, D_G), F32),
            pltpu.VMEM((SHIFTS, HALO_D + tile, D_G), F32),
            pltpu.VMEM((SHIFTS, tile + HALO_D, D_G), F32),
            pltpu.VMEM((8 * CONV_D, D_G), F32),
        ],
        operands=[dxn, z, proj, proj, y, cvd, *weights],
        exchange=exchange,
    )


def _kv_forward(mem, wkv, *, name):
    def body(mem_ref, wkv_ref, kst_ref, vst_ref):
        kv = _dot(mem_ref[...].astype(BF16), wkv_ref[...])
        masks = _head_masks()
        kst_ref[...] = _stack_heads(kv[:, 0:D_G], masks).astype(BF16)
        vst_ref[...] = _stack_heads(kv[:, D_G:], masks).astype(BF16)

    shape = jax.ShapeDtypeStruct((N_SUB * MEM_LEN, D_G), BF16)
    return pl.pallas_call(body, name=name, out_shape=[shape, shape])(mem, wkv)


def _kv_backward(mem, dkst, dvst, *, name):
    def body(mem_ref, dkst_ref, dvst_ref, dwkv_ref):
        masks = _head_masks()
        memb = mem_ref[...].astype(BF16)
        for col, ref in ((0, dkst_ref), (D_G, dvst_ref)):
            d = jnp.zeros((MEM_LEN, D_G), F32)
            for h in range(N_SUB):
                d = d + ref[h * MEM_LEN : (h + 1) * MEM_LEN, :] * masks[h]
            dwkv_ref[:, col : col + D_G] = _dot_tn(memb, d.astype(BF16))

    return pl.pallas_call(body, name=name, out_shape=jax.ShapeDtypeStruct((D_MODEL, 2 * D_G), F32))(mem, dkst, dvst)


def _input_grad(dproj, dz, win, *, name, exchange=None):
    seq = dproj.shape[0]
    tile = min(MM_TILE // 2, seq)

    def body(dproj_ref, dz_ref, win_ref, dx_ref):
        acc = ALPHA * dz_ref[...]
        for k in range(N_CHIPS):
            acc = acc + _dot_nt(dproj_ref[:, k * W_IN_SHARD : (k + 1) * W_IN_SHARD], win_ref[k])
        dx_ref[...] = acc

    return _gridded_call(
        body,
        name=name,
        steps=seq // tile,
        in_specs=[
            pl.BlockSpec((tile, D_IN), lambda i: (i, 0)),
            pl.BlockSpec((tile, D_MODEL), lambda i: (i, 0)),
            _full(win.shape),
        ],
        out_specs=[pl.BlockSpec((tile, D_MODEL), lambda i: (i, 0))],
        out_shape=[jax.ShapeDtypeStruct((seq, D_MODEL), F32)],
        scratch_shapes=[],
        operands=[dproj, dz, win],
        exchange=exchange,
    )


def _input_weight_grad(x, dproj, *, name, exchange=None):
    seq = x.shape[0]
    tile = min(W_GRAD_TILE, seq)
    n_rows = seq // tile

    def body(x_ref, dproj_ref, dwin_ref):
        part = _dot_tn(x_ref[...].astype(BF16), dproj_ref[...])
        if n_rows == 1:
            dwin_ref[0] = part
        else:

            @pl.when(pl.program_id(0) % n_rows == 0)
            def _():
                dwin_ref[...] = jnp.zeros(dwin_ref.shape, F32)

            dwin_ref[0] += part

    return _gridded_call(
        body,
        name=name,
        steps=N_CHIPS * n_rows,
        in_specs=[
            pl.BlockSpec((tile, D_MODEL), lambda s: (s % n_rows, 0)),
            pl.BlockSpec((tile, W_IN_SHARD), lambda s: (s % n_rows, s // n_rows)),
        ],
        out_specs=[pl.BlockSpec((1, D_MODEL, W_IN_SHARD), lambda s: (s // n_rows, 0, 0))],
        out_shape=[jax.ShapeDtypeStruct((N_CHIPS, D_MODEL, W_IN_SHARD), F32)],
        scratch_shapes=[],
        operands=[x, dproj],
        exchange=exchange,
    )


def _expand_sgb(sg_b):
    return jnp.repeat(sg_b.T, HEAD_DIM, axis=1)


def _pack_small_weights(sg_ln_g, sg_ln_b, pool_scale, cc_dw_b, cc_ln_g, cc_ln_b, conv_a_w, cc_dw_w, sg_b):
    vec = jnp.stack([sg_ln_g, sg_ln_b, pool_scale, cc_dw_b, cc_ln_g, cc_ln_b])
    return jnp.concatenate(
        [
            jnp.pad(vec, ((0, RW_CONVA - RW_VEC - 6), (0, 0))),
            jnp.pad(conv_a_w, ((0, RW_DW - RW_CONVA - CONV_A), (0, 0))),
            jnp.pad(cc_dw_w, ((0, RW_SGB - RW_DW - CONV_D), (0, 0))),
            _expand_sgb(sg_b),
        ]
    )


def _sg_w_cat(sg_w):
    cat = jnp.transpose(sg_w, (1, 0, 2)).reshape(CHUNK, N_SUB * CHUNK)
    cat_t = jnp.transpose(sg_w, (2, 0, 1)).reshape(CHUNK, N_SUB * CHUNK)
    return cat, cat_t


def _pool_block_diag(pool_w):
    tiled = jnp.tile(pool_w.reshape(D_G, HEAD_DIM), (1, N_SUB))
    row = lax.broadcasted_iota(jnp.int32, (D_G, D_G), 0) // HEAD_DIM
    col = lax.broadcasted_iota(jnp.int32, (D_G, D_G), 1) // HEAD_DIM
    return jnp.where(row == col, tiled, 0.0)


def _prepare_layer(mem, w, l):
    cat, cat_t = _sg_w_cat(w["sg_w"])
    kst, vst = _kv_forward(mem, w["w_kv"], name=f"kv_fwd{l}")
    return dict(
        win=w["w_in"],
        wout=w["w_out"],
        pww=w["cc_pw_w"],
        sw=_pack_small_weights(
            w["sg_ln_g"], w["sg_ln_b"], w["pool_scale"], w["cc_dw_b"], w["cc_ln_g"], w["cc_ln_b"],
            w["conv_a_w"], w["cc_dw_w"], w["sg_b"],
        ),
        wcat=cat,
        wcat_t=cat_t,
        poolw=_pool_block_diag(w["pool_w"]).astype(BF16),
        ln=jnp.stack([w["ln_g"], w["ln_b"]]),
        kst=kst,
        vst=vst,
    )


def _forward(l, h, p, tgt, exchange=None):
    return _layer_forward(
        h, p["win"], p["kst"], p["vst"], p["wout"], p["sw"], p["wcat"], p["poolw"], p["pww"], p["ln"], tgt,
        name=f"layer_fwd{l}", exchange=exchange,
    )


def _backward(l, dxn, s, p, exchange=None):
    return _layer_backward(
        dxn, s[2], s[0], s[1], s[4], p["kst"], p["vst"], p["wout"], p["sw"], p["wcat"], p["wcat_t"], p["poolw"],
        p["pww"], p["ln"], name=f"layer_bwd{l}", exchange=exchange,
    )


def _place():
    x, y, c = lax.axis_index("x"), lax.axis_index("y"), lax.axis_index("c")
    others = [(1 - x, y), (x, 1 - y), (1 - x, 1 - y)]
    return x, y, c, others


def _half(ref, c, axis):
    n = ref.shape[axis] // 2
    if axis == 0:
        return ref.at[pl.ds(c * n, n)]
    return ref.at[:, pl.ds(c * n, n)]


def _place_own_block(place, stacked, layer, dtypes, *, name):
    n = len(stacked)

    def body(place_ref, *refs):
        for a in range(n):
            refs[n + a][...] = refs[a][...].astype(dtypes[a])

    def block(s):
        return (1,) + s.shape[1:]

    return pl.pallas_call(
        body,
        name=name,
        grid_spec=pltpu.PrefetchScalarGridSpec(
            num_scalar_prefetch=1,
            grid=(1,),
            in_specs=[pl.BlockSpec(block(s), lambda i, place_ref: (layer, 0, 0)) for s in stacked],
            out_specs=[pl.BlockSpec(block(s), lambda i, place_ref: (place_ref[1], 0, 0)) for s in stacked],
        ),
        out_shape=[jax.ShapeDtypeStruct((N_CHIPS,) + s.shape[1:], dt) for s, dt in zip(stacked, dtypes)],
        compiler_params=pltpu.CompilerParams(dimension_semantics=("arbitrary",), vmem_limit_bytes=VMEM_LIMIT),
    )(place, *stacked)


def _sds(a):
    return jax.ShapeDtypeStruct(a.shape, a.dtype)


def _gather_exchange(bufs):
    n = len(bufs)

    def remote(sems, block, k, to):
        return pltpu.make_async_remote_copy(
            src_ref=block, dst_ref=block, send_sem=sems[0].at[k], recv_sem=sems[1].at[k], device_id=to, device_id_type=MESH
        )

    def before(step, steps, refs, outs, sems):
        def send():
            x, y, c, others = _place()
            for j, (px, py) in enumerate(others):
                for a in range(n):
                    remote(sems, _half(refs[a].at[2 * x + y], c, 0), 3 * a + j, (px, py, c)).start()

        _when(step == 0, send)

    def after(step, steps, refs, outs, sems):
        def pass_on():
            x, y, c, others = _place()
            for j, (px, py) in enumerate(others):
                for a in range(n):
                    landed = _half(refs[a].at[2 * px + py], c, 0)
                    remote(sems, landed, 3 * a + j, (px, py, c)).wait_recv()
                    remote(sems, landed, 3 * n + 3 * a + j, (x, y, 1 - c)).start()

        def finish():
            x, y, c, others = _place()
            for j, (px, py) in enumerate(others):
                for a in range(n):
                    remote(sems, _half(refs[a].at[2 * px + py], 1 - c, 0), 3 * n + 3 * a + j, (x, y, 1 - c)).wait_recv()
            for a in range(n):
                mine = _half(refs[a].at[2 * x + y], c, 0)
                for k in range(3 * a, 3 * a + 3):
                    remote(sems, mine, k, (x, y, 1 - c)).wait_send()
                    remote(sems, mine, 3 * n + k, (x, y, 1 - c)).wait_send()

        _when(step == (3 * steps) // 4, pass_on)
        _when(step == steps - 1, finish)

    return _Exchange(bufs, [(_sds(b), a) for a, b in enumerate(bufs)], [6 * n, 6 * n], before, after)


def _swap_exchange(grads):
    n = len(grads)

    def copy(refs, outs, sems, a):
        x, y, c, _ = _place()
        return pltpu.make_async_remote_copy(
            src_ref=_half(refs[a], 1 - c, 1), dst_ref=outs[a], send_sem=sems[0].at[a], recv_sem=sems[1].at[a],
            device_id=(x, y, 1 - c), device_id_type=MESH,
        )

    def before(step, steps, refs, outs, sems):
        _when(step == 0, lambda: [copy(refs, outs, sems, a).start() for a in range(n)] and None)

    def after(step, steps, refs, outs, sems):
        _when(step == steps - 1, lambda: [copy(refs, outs, sems, a).wait() for a in range(n)] and None)

    outputs = [(jax.ShapeDtypeStruct((N_CHIPS, g.shape[1] // 2, g.shape[2]), g.dtype), None) for g in grads]
    return _Exchange(grads, outputs, [n, n], before, after)


def _add_sibling_half(place, grads, received, wire, *, name):
    n = len(grads)

    def body(place_ref, *refs):
        k = pl.program_id(0)
        for a in range(n):
            pair = (refs[a][...] + refs[n + a][...]).astype(wire[a])
            refs[2 * n + a][...] = pair

            @pl.when(k == place_ref[1])
            def _(a=a, pair=pair):
                refs[3 * n + a][...] = pair

    def block(g):
        return (1, g.shape[1] // 2, g.shape[2])

    return pl.pallas_call(
        body,
        name=name,
        grid_spec=pltpu.PrefetchScalarGridSpec(
            num_scalar_prefetch=1,
            grid=(N_CHIPS,),
            in_specs=[pl.BlockSpec(block(g), lambda k, place_ref: (k, place_ref[0], 0)) for g in grads]
            + [pl.BlockSpec(block(g), lambda k, place_ref: (k, 0, 0)) for g in grads],
            out_specs=[pl.BlockSpec(block(g), lambda k, place_ref: (k, 0, 0)) for g in grads]
            + [pl.BlockSpec(block(g), lambda k, place_ref: (place_ref[1], 0, 0)) for g in grads],
        ),
        out_shape=[jax.ShapeDtypeStruct(r.shape, dt) for r, dt in zip(received, wire)] * 2,
        compiler_params=pltpu.CompilerParams(dimension_semantics=("arbitrary",), vmem_limit_bytes=VMEM_LIMIT),
    )(place, *grads, *received)


def _scatter_exchange(pairs, landing):
    n = len(pairs)

    def copy(refs, sems, a, j, px, py):
        x, y, c, _ = _place()
        return pltpu.make_async_remote_copy(
            src_ref=refs[a].at[2 * px + py], dst_ref=refs[n + a].at[2 * x + y], send_sem=sems[0].at[3 * a + j],
            recv_sem=sems[1].at[3 * a + j], device_id=(px, py, c), device_id_type=MESH,
        )

    def before(step, steps, refs, outs, sems):
        def send():
            for j, (px, py) in enumerate(_place()[3]):
                for a in range(n):
                    copy(refs, sems, a, j, px, py).start()

        _when(step == 0, send)

    def after(step, steps, refs, outs, sems):
        def finish():
            x, y, c, others = _place()
            for j, (px, py) in enumerate(others):
                for a in range(n):
                    landed = refs[n + a].at[2 * px + py]
                    pltpu.make_async_remote_copy(
                        src_ref=landed, dst_ref=landed, send_sem=sems[0].at[3 * a + j], recv_sem=sems[1].at[3 * a + j],
                        device_id=(px, py, c), device_id_type=MESH,
                    ).wait_recv()
            for j, (px, py) in enumerate(others):
                for a in range(n):
                    copy(refs, sems, a, j, px, py).wait_send()

        _when(step == steps - 1, finish)

    return _Exchange(pairs + landing, [(_sds(b), n + a) for a, b in enumerate(landing)], [3 * n, 3 * n], before, after)


SUM_STEPS = 2


def _sum_chip_blocks(place, parts, keep_chip_axis, *, name):
    n = len(parts)

    def body(place_ref, *refs):
        for a in range(n):
            p = refs[a]
            total = (p[0].astype(F32) + p[1].astype(F32)) + (p[2].astype(F32) + p[3].astype(F32))
            if keep_chip_axis[a]:
                refs[n + a][0] = total
            else:
                refs[n + a][...] = total

    def in_spec(p):
        return pl.BlockSpec((N_CHIPS, p.shape[1] // SUM_STEPS, p.shape[2]), lambda i, place_ref: (0, i, 0))

    def out_spec(p, keep):
        rows = p.shape[1] // SUM_STEPS
        if keep:
            return pl.BlockSpec((1, rows, p.shape[2]), lambda i, place_ref: (place_ref[1], place_ref[0] * SUM_STEPS + i, 0))
        return pl.BlockSpec((rows, p.shape[2]), lambda i, place_ref: (place_ref[0] * SUM_STEPS + i, 0))

    def out_shape(p, keep):
        shape = (2 * p.shape[1], p.shape[2])
        return jax.ShapeDtypeStruct((N_CHIPS,) + shape if keep else shape, F32)

    return pl.pallas_call(
        body,
        name=name,
        grid_spec=pltpu.PrefetchScalarGridSpec(
            num_scalar_prefetch=1,
            grid=(SUM_STEPS,),
            in_specs=[in_spec(p) for p in parts],
            out_specs=[out_spec(p, k) for p, k in zip(parts, keep_chip_axis)],
        ),
        out_shape=[out_shape(p, k) for p, k in zip(parts, keep_chip_axis)],
        compiler_params=pltpu.CompilerParams(dimension_semantics=("arbitrary",), vmem_limit_bytes=VMEM_LIMIT),
    )(place, *parts)


def _join_exchange(bufs, keep_chip_axis):
    n = len(bufs)
    kept = [a for a in range(n) if keep_chip_axis[a]]
    base = n

    def copy(refs, sems, block, k, to):
        return pltpu.make_async_remote_copy(
            src_ref=block, dst_ref=block, send_sem=sems[0].at[k], recv_sem=sems[1].at[k], device_id=to, device_id_type=MESH
        )

    def mine(refs, a, cc):
        x, y, _, _ = _place()
        return _half(refs[a].at[2 * x + y] if keep_chip_axis[a] else refs[a], cc, 0)

    def before(step, steps, refs, outs, sems):
        def send():
            x, y, c, others = _place()
            for a in range(n):
                copy(refs, sems, mine(refs, a, c), a, (x, y, 1 - c)).start()
            for i, a in enumerate(kept):
                for j, (px, py) in enumerate(others):
                    copy(refs, sems, mine(refs, a, c), base + 6 * i + j, (px, py, c)).start()

        _when(step == 0, send)

    def after(step, steps, refs, outs, sems):
        def pass_on():
            x, y, c, others = _place()
            for i, a in enumerate(kept):
                for j, (px, py) in enumerate(others):
                    landed = _half(refs[a].at[2 * px + py], c, 0)
                    copy(refs, sems, landed, base + 6 * i + j, (px, py, c)).wait_recv()
                    copy(refs, sems, landed, base + 6 * i + 3 + j, (x, y, 1 - c)).start()

        def finish():
            x, y, c, others = _place()
            for a in range(n):
                copy(refs, sems, mine(refs, a, 1 - c), a, (x, y, 1 - c)).wait_recv()
            for i, a in enumerate(kept):
                for j, (px, py) in enumerate(others):
                    passed = _half(refs[a].at[2 * px + py], 1 - c, 0)
                    copy(refs, sems, passed, base + 6 * i + 3 + j, (x, y, 1 - c)).wait_recv()
            for a in range(n):
                copy(refs, sems, mine(refs, a, c), a, (x, y, 1 - c)).wait_send()
            for i, a in enumerate(kept):
                for k in range(6):
                    copy(refs, sems, mine(refs, a, c), base + 6 * i + k, (x, y, 1 - c)).wait_send()

        _when(step == steps // 2, pass_on)
        _when(step == steps - 1, finish)

    return _Exchange(bufs, [(_sds(b), a) for a, b in enumerate(bufs)], [n + 6 * len(kept)] * 2, before, after)


def _adamw(w, g, m, v):
    m = ADAM_B1 * m + (1.0 - ADAM_B1) * g
    v = ADAM_B2 * v + (1.0 - ADAM_B2) * (g * g)
    m_hat = m / (1.0 - ADAM_B1**ADAM_STEP)
    v_hat = v / (1.0 - ADAM_B2**ADAM_STEP)
    delta = -ADAM_LR * (m_hat / (jnp.sqrt(v_hat) + ADAM_EPS) + ADAM_WD * w)
    return delta, m, v


def _adamw_large(w, m, v, layer_grads, *, name):
    depth, rows, cols = w.shape
    tile = math.gcd(rows, ADAM_TILE)
    assert tile % 8 == 0

    def body(w_ref, m_ref, v_ref, *refs):
        g_refs, (g_out, d_out, m_out, v_out) = refs[:depth], refs[depth:]
        for l in range(depth):

            @pl.when(pl.program_id(0) == l)
            def _(l=l):
                g = g_refs[l][...]
                delta, m_new, v_new = _adamw(w_ref[0], g, m_ref[0], v_ref[0])
                g_out[0], d_out[0], m_out[0], v_out[0] = g, delta, m_new, v_new

    def stacked():
        return pl.BlockSpec((1, tile, cols), lambda l, i: (l, i, 0))

    def layer_spec(l):
        return pl.BlockSpec((tile, cols), lambda k, i: (jnp.where(k == l, i, 0), 0))

    shape = jax.ShapeDtypeStruct(w.shape, F32)
    return pl.pallas_call(
        body,
        name=name,
        grid=(depth, rows // tile),
        in_specs=[stacked(), stacked(), stacked()] + [layer_spec(l) for l in range(depth)],
        out_specs=[stacked()] * 4,
        out_shape=[shape] * 4,
        compiler_params=pltpu.CompilerParams(dimension_semantics=("arbitrary", "arbitrary"), vmem_limit_bytes=VMEM_LIMIT),
    )(w, m, v, *layer_grads)


def _packed_pieces(name, l):
    vecs = ("sg_ln_g", "sg_ln_b", "pool_scale", "cc_dw_b", "cc_ln_g", "cc_ln_b")
    if name in vecs:
        r = RG_VEC + vecs.index(name)
        return [((slice(l, l + 1), slice(None)), slice(r, r + 1), slice(None))]
    if name in ("ln_g", "ln_b"):
        r = RG_LN + (4 if name == "ln_b" else 0)
        return [((slice(l, l + 1), slice(j * D_G, (j + 1) * D_G)), slice(r + j, r + j + 1), slice(None)) for j in range(4)]
    assert name == "sg_w"
    return [
        ((l, h), slice(RG_SGW + CHUNK * (h // 2), RG_SGW + CHUNK * (h // 2 + 1)), slice(CHUNK * (h % 2), CHUNK * (h % 2 + 1)))
        for h in range(N_SUB)
    ]


PACKED_NAMES = ("sg_ln_g", "sg_ln_b", "pool_scale", "cc_dw_b", "cc_ln_g", "cc_ln_b", "ln_g", "ln_b", "sg_w")


def _adamw_packed(packed, ws, ms, vs, *, name):
    n, depth = len(ws), len(packed)

    def body(*refs):
        packed_refs, refs = refs[:depth], refs[depth:]
        for a, leaf in enumerate(PACKED_NAMES):
            for l in range(depth):
                for at, rows, cols in _packed_pieces(leaf, l):
                    g = packed_refs[l][rows, cols]
                    delta, m_new, v_new = _adamw(refs[a][at], g, refs[n + a][at], refs[2 * n + a][at])
                    refs[3 * n + a][at] = g
                    refs[4 * n + a][at] = delta
                    refs[5 * n + a][at] = m_new
                    refs[6 * n + a][at] = v_new

    shapes = [jax.ShapeDtypeStruct(w.shape, F32) for w in ws]
    outs = pl.pallas_call(body, name=name, out_shape=shapes * 4)(*packed, *ws, *ms, *vs)
    return outs[:n], outs[n : 2 * n], outs[2 * n : 3 * n], outs[3 * n :]


def _adamw_small(ws, gs, ms, vs, *, name):
    n = len(ws)

    def body(*refs):
        for a in range(n):
            delta, m_new, v_new = _adamw(refs[a][...], refs[n + a][...], refs[2 * n + a][...], refs[3 * n + a][...])
            refs[4 * n + a][...] = delta
            refs[5 * n + a][...] = m_new
            refs[6 * n + a][...] = v_new

    shapes = [jax.ShapeDtypeStruct(w.shape, F32) for w in ws]
    outs = pl.pallas_call(body, name=name, out_shape=shapes * 3)(*ws, *gs, *ms, *vs)
    return outs[:n], outs[n : 2 * n], outs[2 * n :]


WEIGHT_NAMES = (
    "w_in", "conv_a_w", "sg_ln_g", "sg_ln_b", "sg_w", "sg_b", "pool_w", "pool_scale", "cc_dw_w", "cc_dw_b", "cc_ln_g",
    "cc_ln_b", "cc_pw_w", "w_kv", "w_out", "ln_g", "ln_b",
)
LARGE = ("w_in", "cc_pw_w", "w_kv", "w_out")
TAPS_ROWS = 48


def _unpack_small_grads(small, chip):
    out = {}
    for r, k in enumerate(("sg_ln_g", "sg_ln_b", "pool_scale", "cc_dw_b", "cc_ln_g", "cc_ln_b")):
        out[k] = small[RG_VEC + r]
    out["conv_a_w"] = lax.dynamic_slice_in_dim(small[RG_CONVA : RG_CONVA + CONV_A], chip * HEAD_DIM, HEAD_DIM, axis=1)
    out["cc_dw_w"] = lax.dynamic_slice_in_dim(small[RG_DW : RG_DW + CONV_D], chip * HEAD_DIM, HEAD_DIM, axis=1)
    cat = jnp.concatenate([small[RG_SGW : RG_SGW + CHUNK], small[RG_SGW + CHUNK : RG_SGW + 2 * CHUNK]], axis=1)
    out["sg_w"] = jnp.transpose(cat.reshape(CHUNK, N_SUB, CHUNK), (1, 0, 2))
    out["sg_b"] = small[RG_SGB : RG_SGB + CHUNK].reshape(CHUNK, N_SUB, HEAD_DIM).sum(-1).T
    pool = small[RG_POOL : RG_POOL + D_G]
    out["pool_w"] = jnp.stack(
        [pool[g * HEAD_DIM : (g + 1) * HEAD_DIM, g * HEAD_DIM : (g + 1) * HEAD_DIM] for g in range(N_SUB)]
    )
    out["ln_g"] = small[RG_LN : RG_LN + 4].reshape(D_MODEL)
    out["ln_b"] = small[RG_LN + 4 : RG_LN + 8].reshape(D_MODEL)
    return out


def kernel(x, mem, w_in, conv_a_w, sg_ln_g, sg_ln_b, sg_w, sg_b, pool_w, pool_scale, cc_dw_w, cc_dw_b, cc_ln_g, cc_ln_b, cc_pw_w, w_kv, w_out, ln_g, ln_b, loss_target, m_w_in, m_conv_a_w, m_sg_ln_g, m_sg_ln_b, m_sg_w, m_sg_b, m_pool_w, m_pool_scale, m_cc_dw_w, m_cc_dw_b, m_cc_ln_g, m_cc_ln_b, m_cc_pw_w, m_w_kv, m_w_out, m_ln_g, m_ln_b, v_w_in, v_conv_a_w, v_sg_ln_g, v_sg_ln_b, v_sg_w, v_sg_b, v_pool_w, v_pool_scale, v_cc_dw_w, v_cc_dw_b, v_cc_ln_g, v_cc_ln_b, v_cc_pw_w, v_w_kv, v_w_out, v_ln_g, v_ln_b):
    given = dict(locals())
    weights = {k: given[k] for k in WEIGHT_NAMES}
    chip = 2 * lax.axis_index("x") + lax.axis_index("y")
    place = jnp.stack([lax.axis_index("c"), chip]).astype(jnp.int32)

    x0, mem0 = x[0], mem[0]

    taps = jnp.concatenate([conv_a_w, cc_dw_w], axis=1)
    taps = jnp.pad(taps, ((0, 0), (0, TAPS_ROWS - taps.shape[1]), (0, 0)))

    def own_blocks(l):
        return _place_own_block(
            place, [w_in, w_out, w_kv, cc_pw_w, taps], l, [BF16, BF16, BF16, BF16, F32], name=f"place_weights{l}"
        )

    def layer_operands(l, gathered):
        g_in, g_out, g_kv, g_pw, g_taps = gathered
        taps_full = jnp.transpose(g_taps, (1, 0, 2)).reshape(TAPS_ROWS, D_G)
        full = dict(
            w_in=g_in,
            w_out=g_out.reshape(D_MIX, D_MODEL),
            w_kv=g_kv.reshape(D_MODEL, 2 * D_G),
            cc_pw_w=g_pw.reshape(D_G, D_G),
            conv_a_w=taps_full[0:CONV_A],
            cc_dw_w=taps_full[CONV_A : CONV_A + CONV_D],
            **{k: weights[k][l] for k in WEIGHT_NAMES if k not in LARGE + ("conv_a_w", "cc_dw_w")},
        )
        return _prepare_layer(mem0, full, l)

    def other_grads(l, bwd, small):
        _, _, dwout, dkst, dvst, dpw, _ = bwd
        return [
            dwout.reshape(N_CHIPS, D_MIX // N_CHIPS, D_MODEL),
            _kv_backward(mem0, dkst, dvst, name=f"kv_bwd{l}").reshape(N_CHIPS, D_MODEL // N_CHIPS, 2 * D_G),
            dpw.reshape(N_CHIPS, D_G // N_CHIPS, D_G),
            small.reshape(N_CHIPS, RG_ROWS // N_CHIPS, D_G),
        ]

    n_red = 5
    keep = [False, False, False, False, True]

    wire = [BF16, BF16, BF16, BF16, F32]

    def reduced_layer(joined):
        r_in, r_out, r_kv, r_pw, small_all = joined
        packed = small_all.reshape(RG_ROWS, D_G)
        out = _unpack_small_grads(packed, chip)
        out.update(w_in=r_in, w_out=r_out, w_kv=r_kv, cc_pw_w=r_pw, packed=packed)
        return out

    blocks0, blocks1 = own_blocks(0), own_blocks(1)
    p0 = layer_operands(0, _run_exchange(_gather_exchange(blocks0), name="gather_weights0"))
    fwd0, gathered1 = _forward(0, x0, p0, None, exchange=_gather_exchange(blocks1))
    p1 = layer_operands(1, gathered1)
    x1 = fwd0[3]
    fwd1, _ = _forward(1, x1, p1, loss_target[0])

    bwd1, _ = _backward(1, fwd1[3], fwd1, p1)
    small1 = bwd1[6].at[RG_LOSS, :].set(fwd1[5][0, 0])
    (dwin1,), _ = _input_weight_grad(x1, bwd1[0], name="w_in_grad1")
    grads1 = [dwin1] + other_grads(1, bwd1, small1)
    (dx1,), received1 = _input_grad(
        bwd1[0], bwd1[1], p1["win"], name="input_grad1", exchange=_swap_exchange(grads1).starting_late()
    )
    pairs1 = _add_sibling_half(place, grads1, received1, wire, name="rs_pair1")
    bwd0, parts1 = _backward(
        0, dx1, fwd0, p0, exchange=_scatter_exchange(pairs1[:n_red], pairs1[n_red:]).starting_late()
    )
    halves1 = _sum_chip_blocks(place, parts1, keep, name="rs_sum1")
    rest0 = other_grads(0, bwd0, bwd0[6])
    n_rest = len(rest0)
    pairs0_rest = _add_sibling_half(
        place, rest0, _run_exchange(_swap_exchange(rest0), name="rs_swap0_rest"), wire[1:], name="rs_pair0_rest"
    )
    (dwin0,), carried = _input_weight_grad(
        x0, bwd0[0], name="w_in_grad0",
        exchange=_both(_join_exchange(halves1, keep), _scatter_exchange(pairs0_rest[:n_rest], pairs0_rest[n_rest:])),
    )
    joined1, parts0_rest = carried[:n_red], carried[n_red:]
    loss = joined1[4].reshape(RG_ROWS, D_G)[RG_LOSS, 0]
    pairs0_win = _add_sibling_half(
        place, [dwin0], _run_exchange(_swap_exchange([dwin0]), name="rs_swap0_win"), wire[:1], name="rs_pair0_win"
    )
    (grad_x,), parts0_win = _input_grad(
        bwd0[0], bwd0[1], p0["win"], name="input_grad0", exchange=_scatter_exchange(pairs0_win[:1], pairs0_win[1:])
    )
    halves0 = _sum_chip_blocks(place, parts0_win + parts0_rest, keep, name="rs_sum0")
    reduced = [reduced_layer(_run_exchange(_join_exchange(halves0, keep), name="rs_join0")), reduced_layer(joined1)]

    grad, delta, new_m, new_v = {}, {}, {}, {}
    for k in LARGE:
        w3 = weights[k]
        grad[k], delta[k], new_m[k], new_v[k] = _adamw_large(
            w3, given["m_" + k], given["v_" + k], [reduced[l][k] for l in range(DEPTH)], name=f"adamw_{k}"
        )
    g_p, d_p, m_p, v_p = _adamw_packed(
        [reduced[l]["packed"] for l in range(DEPTH)],
        [weights[k] for k in PACKED_NAMES],
        [given["m_" + k] for k in PACKED_NAMES],
        [given["v_" + k] for k in PACKED_NAMES],
        name="adamw_packed",
    )
    for a, k in enumerate(PACKED_NAMES):
        grad[k], delta[k], new_m[k], new_v[k] = g_p[a], d_p[a], m_p[a], v_p[a]
    small_names = [k for k in WEIGHT_NAMES if k not in LARGE + PACKED_NAMES]
    for k in small_names:
        grad[k] = jnp.stack([reduced[l][k] for l in range(DEPTH)])
    d_s, m_s, v_s = _adamw_small(
        [weights[k] for k in small_names],
        [grad[k] for k in small_names],
        [given["m_" + k] for k in small_names],
        [given["v_" + k] for k in small_names],
        name="adamw_small",
    )
    for a, k in enumerate(small_names):
        delta[k], new_m[k], new_v[k] = d_s[a], m_s[a], v_s[a]

    return (
        loss,
        grad_x[None],
        *[grad[k] for k in WEIGHT_NAMES],
        *[delta[k] for k in WEIGHT_NAMES],
        *[new_m[k] for k in WEIGHT_NAMES],
        *[new_v[k] for k in WEIGHT_NAMES],
    )
```
